```python
import jax, jax.numpy as jnp
from jax import lax
import numpy as np

D_MODEL = 4096
BATCH = 16
SEQ = 2048
DEPTH = 1

CHUNK = 64
MIX_WIDTH = D_MODEL
GDN_WIDTH = MIX_WIDTH // 2
SC_WIDTH = MIX_WIDTH - GDN_WIDTH
HEAD_DIM = 128
GDN_HEADS = GDN_WIDTH // HEAD_DIM
SC_GROUPS = SC_WIDTH // HEAD_DIM
GDN_CONV = 4
SC_CONV = 3
D_FF = 4 * D_MODEL
NORM_EPS = 1e-6
L2_EPS = 1e-6
IN_COLS = 4 * GDN_WIDTH + 2 * GDN_HEADS + 3 * SC_WIDTH

kernel_name = "hybrid_gdn_shortconv_sqrelu_block"


def rms_norm(x, w):
    xf = x.astype(jnp.float32)
    y = xf * lax.rsqrt(jnp.mean(xf * xf, axis=-1, keepdims=True) + NORM_EPS)
    return (y * w.astype(jnp.float32)).astype(x.dtype)


def l2_normalize(t):
    return t * lax.rsqrt(jnp.sum(t * t, axis=-1, keepdims=True) + L2_EPS)


def causal_depthwise_conv(u, w):
    k_width, ch = w.shape
    return lax.conv_general_dilated(
        u, w[:, None, :].astype(u.dtype), window_strides=(1,), padding=[(k_width - 1, 0)],
        dimension_numbers=("NWC", "WIO", "NWC"), feature_group_count=ch)


def gated_delta_rule_chunked(q, k, v, g, beta):
    bsz, seq, heads, dk = q.shape
    dv = v.shape[-1]
    n = seq // CHUNK
    q = l2_normalize(q.astype(jnp.float32)) * (dk ** -0.5)
    k = l2_normalize(k.astype(jnp.float32))
    v = v.astype(jnp.float32)

    def to_chunks(t):
        return t.reshape(bsz, n, CHUNK, heads, -1).transpose(0, 1, 3, 2, 4)

    q, k, v = to_chunks(q), to_chunks(k), to_chunks(v)
    g = jnp.cumsum(g.reshape(bsz, n, CHUNK, heads).transpose(0, 1, 3, 2), axis=-1)
    beta = beta.reshape(bsz, n, CHUNK, heads).transpose(0, 1, 3, 2)

    tri_incl = jnp.tril(jnp.ones((CHUNK, CHUNK), dtype=bool))
    tri_strict = jnp.tril(jnp.ones((CHUNK, CHUNK), dtype=bool), k=-1)
    decay = jnp.exp(jnp.where(tri_incl, g[..., :, None] - g[..., None, :], -jnp.inf))

    kk = jnp.einsum("bnhid,bnhjd->bnhij", k, k)
    lower = jnp.where(tri_strict, beta[..., None] * kk * decay, 0.0)
    a_mat = lower + jnp.eye(CHUNK, dtype=jnp.float32)
    rhs = jnp.concatenate([v * beta[..., None], k * (beta * jnp.exp(g))[..., None]], axis=-1)
    sol = lax.linalg.triangular_solve(a_mat, rhs, left_side=True, lower=True, unit_diagonal=True)
    u, w = sol[..., :dv], sol[..., dv:]

    qk = jnp.einsum("bnhid,bnhjd->bnhij", q, k) * decay
    q_dec = q * jnp.exp(g)[..., None]
    k_dec = k * jnp.exp(g[..., -1:] - g)[..., None]
    g_last = jnp.exp(g[..., -1])

    def step(state, inp):
        q_c, k_c, u_c, w_c, qk_c, gl_c = inp
        v_new = u_c - jnp.einsum("bhcd,bhde->bhce", w_c, state)
        o_c = (jnp.einsum("bhcd,bhde->bhce", q_c, state)
               + jnp.einsum("bhij,bhje->bhie", qk_c, v_new))
        state = state * gl_c[..., None, None] + jnp.einsum("bhcd,bhce->bhde", k_c, v_new)
        return state, o_c

    xs = tuple(jnp.moveaxis(t, 1, 0) for t in (q_dec, k_dec, u, w, qk, g_last))
    state0 = jnp.zeros((bsz, heads, dk, dv), dtype=jnp.float32)
    _, o = lax.scan(step, state0, xs)
    return o.transpose(1, 0, 3, 2, 4).reshape(bsz, seq, heads, dv)


def hybrid_mixer(xn, w_in, conv_qkv_w, a_log, dt_bias, gdn_norm_w, conv_sc_w, w_out):
    bsz, seq, _ = xn.shape
    proj = jnp.einsum("bsd,dc->bsc", xn, w_in)
    sizes = [3 * GDN_WIDTH, GDN_WIDTH, GDN_HEADS, GDN_HEADS, SC_WIDTH, SC_WIDTH, SC_WIDTH]
    idx = [sum(sizes[: i + 1]) for i in range(len(sizes) - 1)]
    qkv, z, a, b, sc_b, sc_c, sc_h = jnp.split(proj, idx, axis=-1)

    qkv = jax.nn.silu(causal_depthwise_conv(qkv, conv_qkv_w))
    q, k, v = (t.reshape(bsz, seq, GDN_HEADS, HEAD_DIM) for t in jnp.split(qkv, 3, axis=-1))
    g = -jnp.exp(a_log.astype(jnp.float32)) * jax.nn.softplus(a.astype(jnp.float32) + dt_bias.astype(jnp.float32))
    beta = jax.nn.sigmoid(b.astype(jnp.float32))
    o = gated_delta_rule_chunked(q, k, v, g, beta)
    o = o * lax.rsqrt(jnp.mean(o * o, axis=-1, keepdims=True) + NORM_EPS) * gdn_norm_w.astype(jnp.float32)
    o = o * jax.nn.silu(z.astype(jnp.float32).reshape(bsz, seq, GDN_HEADS, HEAD_DIM))
    gdn_out = o.reshape(bsz, seq, GDN_WIDTH).astype(xn.dtype)

    sc_out = sc_b * causal_depthwise_conv(sc_c * sc_h, conv_sc_w)

    mixed = jnp.concatenate([gdn_out, sc_out], axis=-1)
    return jnp.einsum("bsm,md->bsd", mixed, w_out)


def squared_relu_mlp(xn, w_up, w_down):
    hid = jnp.square(jax.nn.relu(jnp.einsum("bsd,df->bsf", xn, w_up)))
    return jnp.einsum("bsf,fd->bsd", hid, w_down)


def _fwd_setup_inputs(seed: int = 0) -> dict:
    key = jax.random.key(seed)
    ks = jax.random.split(key, 16)
    f32 = jnp.float32

    def gain(k):
        return 1.0 + 0.02 * jax.random.normal(k, (DEPTH, D_MODEL), f32)

    x = jax.random.normal(ks[0], (BATCH, SEQ, D_MODEL), f32)
    norm_mix_pre = gain(ks[1])
    w_in = jax.random.normal(ks[2], (DEPTH, D_MODEL, IN_COLS), f32) * D_MODEL ** -0.5
    conv_qkv_w = jax.random.normal(ks[3], (DEPTH, GDN_CONV, 3 * GDN_WIDTH), f32) * GDN_CONV ** -0.5
    a_log = jnp.log(jax.random.uniform(ks[4], (DEPTH, GDN_HEADS), f32, 1.0, 16.0))
    dt_bias = 0.1 * jax.random.normal(ks[5], (DEPTH, GDN_HEADS), f32)
    gdn_norm_w = 1.0 + 0.02 * jax.random.normal(ks[6], (DEPTH, HEAD_DIM), f32)
    conv_sc_w = jax.random.normal(ks[7], (DEPTH, SC_CONV, SC_WIDTH), f32) * SC_CONV ** -0.5
    w_out = jax.random.normal(ks[8], (DEPTH, MIX_WIDTH, D_MODEL), f32) * MIX_WIDTH ** -0.5
    norm_mix_post = gain(ks[9])
    norm_mlp_pre = gain(ks[10])
    w_up = jax.random.normal(ks[11], (DEPTH, D_MODEL, D_FF), f32) * D_MODEL ** -0.5
    w_down = jax.random.normal(ks[12], (DEPTH, D_FF, D_MODEL), f32) * D_FF ** -0.5
    norm_mlp_post = gain(ks[13])
    return {"x": x, "norm_mix_pre": norm_mix_pre, "w_in": w_in, "conv_qkv_w": conv_qkv_w,
            "a_log": a_log, "dt_bias": dt_bias, "gdn_norm_w": gdn_norm_w, "conv_sc_w": conv_sc_w,
            "w_out": w_out, "norm_mix_post": norm_mix_post, "norm_mlp_pre": norm_mlp_pre,
            "w_up": w_up, "w_down": w_down, "norm_mlp_post": norm_mlp_post}


def _fwd_reference(x, norm_mix_pre, w_in, conv_qkv_w, a_log, dt_bias, gdn_norm_w, conv_sc_w,
              w_out, norm_mix_post, norm_mlp_pre, w_up, w_down, norm_mlp_post):
    h = x
    for l in range(DEPTH):
        mix = hybrid_mixer(rms_norm(h, norm_mix_pre[l]), w_in[l], conv_qkv_w[l], a_log[l],
                           dt_bias[l], gdn_norm_w[l], conv_sc_w[l], w_out[l])
        h = h + rms_norm(mix, norm_mix_post[l])
        ff = squared_relu_mlp(rms_norm(h, norm_mlp_pre[l]), w_up[l], w_down[l])
        h = h + rms_norm(ff, norm_mlp_post[l])
    return h


import jax as _jax
import jax.numpy as _jnp

TWIN_FORMAT = 'train_step'
FWD_PARAMS = ['x', 'norm_mix_pre', 'w_in', 'conv_qkv_w', 'a_log', 'dt_bias', 'gdn_norm_w', 'conv_sc_w', 'w_out', 'norm_mix_post', 'norm_mlp_pre', 'w_up', 'w_down', 'norm_mlp_post']
TWIN_WEIGHTS = ['norm_mix_pre', 'w_in', 'conv_qkv_w', 'a_log', 'dt_bias', 'gdn_norm_w', 'conv_sc_w', 'w_out', 'norm_mix_post', 'norm_mlp_pre', 'w_up', 'w_down', 'norm_mlp_post']
TWIN_DIFF_INPUT = 'x'
TWIN_INPUTS = ['x', 'norm_mix_pre', 'w_in', 'conv_qkv_w', 'a_log', 'dt_bias', 'gdn_norm_w', 'conv_sc_w', 'w_out', 'norm_mix_post', 'norm_mlp_pre', 'w_up', 'w_down', 'norm_mlp_post', 'loss_target', 'm_norm_mix_pre', 'm_w_in', 'm_conv_qkv_w', 'm_a_log', 'm_dt_bias', 'm_gdn_norm_w', 'm_conv_sc_w', 'm_w_out', 'm_norm_mix_post', 'm_norm_mlp_pre', 'm_w_up', 'm_w_down', 'm_norm_mlp_post', 'v_norm_mix_pre', 'v_w_in', 'v_conv_qkv_w', 'v_a_log', 'v_dt_bias', 'v_gdn_norm_w', 'v_conv_sc_w', 'v_w_out', 'v_norm_mix_post', 'v_norm_mlp_pre', 'v_w_up', 'v_w_down', 'v_norm_mlp_post']
TWIN_OUTPUTS = ['loss', 'grad_x', 'grad_norm_mix_pre', 'grad_w_in', 'grad_conv_qkv_w', 'grad_a_log', 'grad_dt_bias', 'grad_gdn_norm_w', 'grad_conv_sc_w', 'grad_w_out', 'grad_norm_mix_post', 'grad_norm_mlp_pre', 'grad_w_up', 'grad_w_down', 'grad_norm_mlp_post', 'delta_norm_mix_pre', 'delta_w_in', 'delta_conv_qkv_w', 'delta_a_log', 'delta_dt_bias', 'delta_gdn_norm_w', 'delta_conv_sc_w', 'delta_w_out', 'delta_norm_mix_post', 'delta_norm_mlp_pre', 'delta_w_up', 'delta_w_down', 'delta_norm_mlp_post', 'new_m_norm_mix_pre', 'new_m_w_in', 'new_m_conv_qkv_w', 'new_m_a_log', 'new_m_dt_bias', 'new_m_gdn_norm_w', 'new_m_conv_sc_w', 'new_m_w_out', 'new_m_norm_mix_post', 'new_m_norm_mlp_pre', 'new_m_w_up', 'new_m_w_down', 'new_m_norm_mlp_post', 'new_v_norm_mix_pre', 'new_v_w_in', 'new_v_conv_qkv_w', 'new_v_a_log', 'new_v_dt_bias', 'new_v_gdn_norm_w', 'new_v_conv_sc_w', 'new_v_w_out', 'new_v_norm_mix_post', 'new_v_norm_mlp_pre', 'new_v_w_up', 'new_v_w_down', 'new_v_norm_mlp_post']
TWIN_LEAF_KINDS = {'loss': 'loss', 'grad_x': 'grad_x', 'grad_norm_mix_pre': 'grad_w', 'grad_w_in': 'grad_w', 'grad_conv_qkv_w': 'grad_w', 'grad_a_log': 'grad_w', 'grad_dt_bias': 'grad_w', 'grad_gdn_norm_w': 'grad_w', 'grad_conv_sc_w': 'grad_w', 'grad_w_out': 'grad_w', 'grad_norm_mix_post': 'grad_w', 'grad_norm_mlp_pre': 'grad_w', 'grad_w_up': 'grad_w', 'grad_w_down': 'grad_w', 'grad_norm_mlp_post': 'grad_w', 'delta_norm_mix_pre': 'delta_w', 'delta_w_in': 'delta_w', 'delta_conv_qkv_w': 'delta_w', 'delta_a_log': 'delta_w', 'delta_dt_bias': 'delta_w', 'delta_gdn_norm_w': 'delta_w', 'delta_conv_sc_w': 'delta_w', 'delta_w_out': 'delta_w', 'delta_norm_mix_post': 'delta_w', 'delta_norm_mlp_pre': 'delta_w', 'delta_w_up': 'delta_w', 'delta_w_down': 'delta_w', 'delta_norm_mlp_post': 'delta_w', 'new_m_norm_mix_pre': 'new_m', 'new_m_w_in': 'new_m', 'new_m_conv_qkv_w': 'new_m', 'new_m_a_log': 'new_m', 'new_m_dt_bias': 'new_m', 'new_m_gdn_norm_w': 'new_m', 'new_m_conv_sc_w': 'new_m', 'new_m_w_out': 'new_m', 'new_m_norm_mix_post': 'new_m', 'new_m_norm_mlp_pre': 'new_m', 'new_m_w_up': 'new_m', 'new_m_w_down': 'new_m', 'new_m_norm_mlp_post': 'new_m', 'new_v_norm_mix_pre': 'new_v', 'new_v_w_in': 'new_v', 'new_v_conv_qkv_w': 'new_v', 'new_v_a_log': 'new_v', 'new_v_dt_bias': 'new_v', 'new_v_gdn_norm_w': 'new_v', 'new_v_conv_sc_w': 'new_v', 'new_v_w_out': 'new_v', 'new_v_norm_mix_post': 'new_v', 'new_v_norm_mlp_pre': 'new_v', 'new_v_w_up': 'new_v', 'new_v_w_down': 'new_v', 'new_v_norm_mlp_post': 'new_v'}


def _forward(args):
    return _fwd_reference(*[args[k] for k in FWD_PARAMS])


def _output_shape():
    def fwd():
        inp = _fwd_setup_inputs(0)
        return _fwd_reference(*[inp[k] for k in FWD_PARAMS])
    out = _jax.eval_shape(fwd)
    return out.shape, out.dtype

N_MICROBATCH = 1
ADAM_LR = 0.001
ADAM_B1 = 0.9
ADAM_B2 = 0.999
ADAM_EPS = 1e-08
ADAM_WD = 0.01
ADAM_STEP = 10
PER_EXAMPLE_BATCH_AXIS = {'x': 0, 'loss_target': 0}
SHARED_INPUTS = []
_WEIGHT_DTYPES = {'norm_mix_pre': _jnp.float32, 'w_in': _jnp.float32, 'conv_qkv_w': _jnp.float32, 'a_log': _jnp.float32, 'dt_bias': _jnp.float32, 'gdn_norm_w': _jnp.float32, 'conv_sc_w': _jnp.float32, 'w_out': _jnp.float32, 'norm_mix_post': _jnp.float32, 'norm_mlp_pre': _jnp.float32, 'w_up': _jnp.float32, 'w_down': _jnp.float32, 'norm_mlp_post': _jnp.float32}
MOMENT_SCALE = {'norm_mix_pre': 1.730354e-01, 'w_in': 9.365433e-02, 'conv_qkv_w': 9.597651e-02, 'a_log': 3.729987e-01, 'dt_bias': 2.891207e-01, 'gdn_norm_w': 9.203005e-01, 'conv_sc_w': 1.477060e-01, 'w_out': 1.904038e-01, 'norm_mix_post': 7.997514e+00, 'norm_mlp_pre': 1.377666e-01, 'w_up': 6.835815e-02, 'w_down': 2.023593e-01, 'norm_mlp_post': 8.237117e+00}


def _to_microbatches(a, axis):
    t = _jnp.moveaxis(a, axis, 0)
    t = t.reshape((N_MICROBATCH, t.shape[0] // N_MICROBATCH) + t.shape[1:])
    return _jnp.moveaxis(t, 1, axis + 1)


def setup_inputs(seed: int = 0) -> dict:
    inp = _fwd_setup_inputs(seed)
    key = _jax.random.fold_in(_jax.random.key(seed), 7919)
    shape, _ = _output_shape()
    out = dict(inp)
    out["loss_target"] = _jax.random.normal(_jax.random.fold_in(key, 0), shape, _jnp.float32)
    for i, name in enumerate(TWIN_WEIGHTS):
        w = inp[name].astype(_jnp.float32)
        if MOMENT_SCALE is None:
            s = _jnp.sqrt(_jnp.mean(_jnp.square(w)) + 1e-30)
        else:
            s = MOMENT_SCALE[name]
        km, kv = _jax.random.split(_jax.random.fold_in(key, i + 1))
        out[name] = w
        out["m_" + name] = s * _jax.random.normal(km, w.shape, _jnp.float32)
        out["v_" + name] = (s * s) * _jax.random.uniform(kv, w.shape, _jnp.float32, 0.5, 1.5)
    if N_MICROBATCH > 1:
        for name, axis in PER_EXAMPLE_BATCH_AXIS.items():
            out[name] = _to_microbatches(out[name], axis)
    return {'x': out['x'], 'norm_mix_pre': out['norm_mix_pre'], 'w_in': out['w_in'], 'conv_qkv_w': out['conv_qkv_w'], 'a_log': out['a_log'], 'dt_bias': out['dt_bias'], 'gdn_norm_w': out['gdn_norm_w'], 'conv_sc_w': out['conv_sc_w'], 'w_out': out['w_out'], 'norm_mix_post': out['norm_mix_post'], 'norm_mlp_pre': out['norm_mlp_pre'], 'w_up': out['w_up'], 'w_down': out['w_down'], 'norm_mlp_post': out['norm_mlp_post'], 'loss_target': out['loss_target'], 'm_norm_mix_pre': out['m_norm_mix_pre'], 'm_w_in': out['m_w_in'], 'm_conv_qkv_w': out['m_conv_qkv_w'], 'm_a_log': out['m_a_log'], 'm_dt_bias': out['m_dt_bias'], 'm_gdn_norm_w': out['m_gdn_norm_w'], 'm_conv_sc_w': out['m_conv_sc_w'], 'm_w_out': out['m_w_out'], 'm_norm_mix_post': out['m_norm_mix_post'], 'm_norm_mlp_pre': out['m_norm_mlp_pre'], 'm_w_up': out['m_w_up'], 'm_w_down': out['m_w_down'], 'm_norm_mlp_post': out['m_norm_mlp_post'], 'v_norm_mix_pre': out['v_norm_mix_pre'], 'v_w_in': out['v_w_in'], 'v_conv_qkv_w': out['v_conv_qkv_w'], 'v_a_log': out['v_a_log'], 'v_dt_bias': out['v_dt_bias'], 'v_gdn_norm_w': out['v_gdn_norm_w'], 'v_conv_sc_w': out['v_conv_sc_w'], 'v_w_out': out['v_w_out'], 'v_norm_mix_post': out['v_norm_mix_post'], 'v_norm_mlp_pre': out['v_norm_mlp_pre'], 'v_w_up': out['v_w_up'], 'v_w_down': out['v_w_down'], 'v_norm_mlp_post': out['v_norm_mlp_post']}


def _loss(weights, diff, rest, loss_target):
    with _jax.named_scope("forward"):
        args = {**rest, TWIN_DIFF_INPUT: diff, **{k: w.astype(_WEIGHT_DTYPES[k]) for k, w in weights.items()}}
        y = _forward(args)
    with _jax.named_scope("loss_head"):
        err = _jnp.square(y.astype(_jnp.float32) - loss_target)
        return 0.5 * _jnp.sum(_jnp.mean(err, axis=-1)) if err.ndim else 0.5 * err


def _adamw(w, g, m, v):
    m = ADAM_B1 * m + (1.0 - ADAM_B1) * g
    v = ADAM_B2 * v + (1.0 - ADAM_B2) * _jnp.square(g)
    m_hat = m / (1.0 - ADAM_B1 ** ADAM_STEP)
    v_hat = v / (1.0 - ADAM_B2 ** ADAM_STEP)
    delta = -ADAM_LR * (m_hat / (_jnp.sqrt(v_hat) + ADAM_EPS) + ADAM_WD * w)
    return delta, m, v


def reference(x, norm_mix_pre, w_in, conv_qkv_w, a_log, dt_bias, gdn_norm_w, conv_sc_w, w_out, norm_mix_post, norm_mlp_pre, w_up, w_down, norm_mlp_post, loss_target, m_norm_mix_pre, m_w_in, m_conv_qkv_w, m_a_log, m_dt_bias, m_gdn_norm_w, m_conv_sc_w, m_w_out, m_norm_mix_post, m_norm_mlp_pre, m_w_up, m_w_down, m_norm_mlp_post, v_norm_mix_pre, v_w_in, v_conv_qkv_w, v_a_log, v_dt_bias, v_gdn_norm_w, v_conv_sc_w, v_w_out, v_norm_mix_post, v_norm_mlp_pre, v_w_up, v_w_down, v_norm_mlp_post):
    given = dict(x=x, norm_mix_pre=norm_mix_pre, w_in=w_in, conv_qkv_w=conv_qkv_w, a_log=a_log, dt_bias=dt_bias, gdn_norm_w=gdn_norm_w, conv_sc_w=conv_sc_w, w_out=w_out, norm_mix_post=norm_mix_post, norm_mlp_pre=norm_mlp_pre, w_up=w_up, w_down=w_down, norm_mlp_post=norm_mlp_post, loss_target=loss_target, m_norm_mix_pre=m_norm_mix_pre, m_w_in=m_w_in, m_conv_qkv_w=m_conv_qkv_w, m_a_log=m_a_log, m_dt_bias=m_dt_bias, m_gdn_norm_w=m_gdn_norm_w, m_conv_sc_w=m_conv_sc_w, m_w_out=m_w_out, m_norm_mix_post=m_norm_mix_post, m_norm_mlp_pre=m_norm_mlp_pre, m_w_up=m_w_up, m_w_down=m_w_down, m_norm_mlp_post=m_norm_mlp_post, v_norm_mix_pre=v_norm_mix_pre, v_w_in=v_w_in, v_conv_qkv_w=v_conv_qkv_w, v_a_log=v_a_log, v_dt_bias=v_dt_bias, v_gdn_norm_w=v_gdn_norm_w, v_conv_sc_w=v_conv_sc_w, v_w_out=v_w_out, v_norm_mix_post=v_norm_mix_post, v_norm_mlp_pre=v_norm_mlp_pre, v_w_up=v_w_up, v_w_down=v_w_down, v_norm_mlp_post=v_norm_mlp_post)
    weights = {n: given[n] for n in TWIN_WEIGHTS}
    shared = {n: given[n] for n in SHARED_INPUTS}
    per_example = {n: given[n] for n in ['x']}
    grad_fn = _jax.value_and_grad(_loss, argnums=(0, 1))

    def one_microbatch(ex, loss_target):
        ex = dict(ex)
        diff = ex.pop(TWIN_DIFF_INPUT)
        return grad_fn(weights, diff, {**shared, **ex}, loss_target)

    if N_MICROBATCH == 1:
        loss, (grad_w, grad_x) = one_microbatch(per_example, given["loss_target"])
    else:
        def body(carry, xs):
            loss_sum, grad_sum = carry
            l_k, (gw_k, gx_k) = one_microbatch(xs[0], xs[1])
            with _jax.named_scope("update"):
                return (loss_sum + l_k, _jax.tree.map(_jnp.add, grad_sum, gw_k)), gx_k

        init = (_jnp.zeros((), _jnp.float32), _jax.tree.map(_jnp.zeros_like, weights))
        (loss, grad_w), grad_x = _jax.lax.scan(body, init, (per_example, given["loss_target"]))
    with _jax.named_scope("update"):
        delta_w, new_m, new_v = {}, {}, {}
        for n in TWIN_WEIGHTS:
            delta_w[n], new_m[n], new_v[n] = _adamw(weights[n], grad_w[n], given["m_" + n], given["v_" + n])
    return (loss, grad_x, *[grad_w[n] for n in TWIN_WEIGHTS], *[delta_w[n] for n in TWIN_WEIGHTS],
            *[new_m[n] for n in TWIN_WEIGHTS], *[new_v[n] for n in TWIN_WEIGHTS])
```

```python
import functools
import math

import numpy as np
import jax
import jax.numpy as jnp
from jax import lax
from jax.experimental import pallas as pl
from jax.experimental.pallas import tpu as pltpu

F32 = jnp.float32
BF16 = jnp.bfloat16
HI = lax.Precision.HIGHEST
MESH = pl.DeviceIdType.MESH

N_DEV = 8
LANE = 128
MXU = 256
CHUNK = 64
NORM_EPS = 1e-6
L2_EPS = 1e-6
VMEM_LIMIT = 56 * 1024 * 1024

ADAM_LR = 0.001
ADAM_B1 = 0.9
ADAM_B2 = 0.999
ADAM_EPS = 1e-08
ADAM_WD = 0.01
ADAM_STEP = 10

NN = (((1,), (0,)), ((), ()))
NT = (((1,), (1,)), ((), ()))
TN = (((0,), (0,)), ((), ()))


def _params(*sem):
    return pltpu.CompilerParams(dimension_semantics=sem, vmem_limit_bytes=VMEM_LIMIT)


def _tile(n, want):
    if n <= want:
        return n
    t = (want // LANE) * LANE
    while t > LANE and n % t:
        t -= LANE
    assert n % t == 0, (n, want)
    return t


def _bdot(a, b, dims=NN):
    return lax.dot_general(a.astype(BF16), b.astype(BF16), dims, preferred_element_type=F32)


def _hdot(a, b, dims=NN):
    return lax.dot_general(a, b, dims, preferred_element_type=F32, precision=HI)


def _sigmoid(x):
    return 1.0 / (1.0 + jnp.exp(-x))


def _softplus(x):
    return jnp.maximum(x, 0.0) + jnp.log(1.0 + jnp.exp(-jnp.abs(x)))


def _matmul(a, b, *, mode, out_dtypes, name, n_cols=None, tm=512, tn=512, tk=4096, epilogue=None, extras=(),
            b_spec=None, out_custom=None):
    if mode == "tn":
        K, M = a.shape
    else:
        M, K = a.shape
    N = n_cols if n_cols is not None else (b.shape[0] if mode == "nt" else b.shape[1])
    tm, tk, tn = _tile(M, tm), _tile(K, tk), _tile(N, tn)
    if b_spec is None:
        b_spec = pl.BlockSpec((tn, tk), lambda i, j, k: (j, k)) if mode == "nt" else pl.BlockSpec((tk, tn), lambda i, j, k: (k, j))
    else:
        b_spec = b_spec(tk, tn)
    gm, gn, nk = M // tm, N // tn, K // tk
    if out_custom is None:
        out_shapes = [(M, N)] * len(out_dtypes)
        out_blocks = [(tm, tn)] * len(out_dtypes)
        out_index = [lambda i, j, k: (i, j)] * len(out_dtypes)
    else:
        shape, blk, ix = out_custom(tm, tn)
        out_shapes, out_blocks, out_index = [shape] * len(out_dtypes), [blk] * len(out_dtypes), [ix] * len(out_dtypes)
    a_spec = pl.BlockSpec((tk, tm), lambda i, j, k: (k, i)) if mode == "tn" else pl.BlockSpec((tm, tk), lambda i, j, k: (i, k))
    dims = {"nn": NN, "nt": NT, "tn": TN}[mode]
    n_ex, n_out = len(extras), len(out_dtypes)

    def body(a_ref, b_ref, *rest):
        ex, outs = rest[:n_ex], rest[n_ex:n_ex + n_out]

        def finish(acc):
            res = epilogue(acc, *[e[...] for e in ex]) if epilogue is not None else (acc,)
            for o, r in zip(outs, res):
                o[...] = r.reshape(o.shape).astype(o.dtype)

        bb = b_ref[...]
        bb = bb.reshape(bb.shape[-2:])
        part = lax.dot_general(a_ref[...], bb, dims, preferred_element_type=F32)
        if nk == 1:
            finish(part)
        else:
            acc = rest[-1]
            k = pl.program_id(2)

            @pl.when(k == 0)
            def _():
                acc[...] = part

            @pl.when(k > 0)
            def _():
                acc[...] += part

            @pl.when(k == nk - 1)
            def _():
                finish(acc[...])

    return pl.pallas_call(
        body,
        grid=(gm, gn, nk),
        in_specs=[a_spec, b_spec] + [pl.BlockSpec((tm, tn), lambda i, j, k: (i, j)) for _ in extras],
        out_specs=[pl.BlockSpec(blk, ix) for blk, ix in zip(out_blocks, out_index)],
        out_shape=[jax.ShapeDtypeStruct(s, d) for s, d in zip(out_shapes, out_dtypes)],
        scratch_shapes=[pltpu.VMEM((tm, tn), F32)] if nk > 1 else [],
        compiler_params=_params("parallel", "parallel", "arbitrary"),
        name=name,
    )(a, b, *extras)


TR = 128


def _rms(x):
    return lax.rsqrt(jnp.mean(x * x, axis=-1, keepdims=True) + NORM_EPS)


def _rms_bwd(x, r, w, dy):
    u = dy * w
    dx = r * u - x * (r * r * r) * jnp.mean(x * u, axis=-1, keepdims=True)
    return dx, dy * x * r


def _row_call(body, ins, row_flags, outs, name, n_rows):
    tr = min(TR, n_rows)
    in_specs = []
    for arr, is_row in zip(ins, row_flags):
        if is_row:
            in_specs.append(pl.BlockSpec((tr, arr.shape[1]), lambda i: (i, 0)))
        else:
            in_specs.append(pl.BlockSpec(arr.shape, lambda i: (0, 0)))
    out_specs, out_shape = [], []
    for shape, dtype, kind in outs:
        if kind == "row":
            out_specs.append(pl.BlockSpec((tr, shape[1]), lambda i: (i, 0)))
        else:
            out_specs.append(pl.BlockSpec(shape, lambda i: (0, 0)))
        out_shape.append(jax.ShapeDtypeStruct(shape, dtype))
    return pl.pallas_call(body, grid=(n_rows // tr,), in_specs=in_specs, out_specs=out_specs, out_shape=out_shape,
                          compiler_params=_params("arbitrary"), name=name)(*ins)


def _acc_out(ref, val):
    @pl.when(pl.program_id(0) == 0)
    def _():
        ref[...] = val

    @pl.when(pl.program_id(0) > 0)
    def _():
        ref[...] += val


def _rms_fwd(x, g):
    T, D = x.shape

    def body(x_ref, g_ref, o_ref):
        xv = x_ref[...]
        o_ref[...] = (xv * _rms(xv) * g_ref[...]).astype(BF16)

    return _row_call(body, [x, g], [True, False], [((T, D), BF16, "row")], "rms_fwd", T)[0]


def _post1(x, mix, g2, g3):
    T, D = x.shape

    def body(x_ref, mix_ref, g2_ref, g3_ref, h_ref, xn2_ref):
        mv = mix_ref[...]
        h = x_ref[...] + mv * _rms(mv) * g2_ref[...]
        h_ref[...] = h
        xn2_ref[...] = (h * _rms(h) * g3_ref[...]).astype(BF16)

    return _row_call(body, [x, mix, g2, g3], [True, True, False, False],
                     [((T, D), F32, "row"), ((T, D), BF16, "row")], "post1", T)


def _post2_loss(h, ff, g4, target):
    T, D = h.shape

    def body(h_ref, ff_ref, g4_ref, t_ref, dff_ref, dy_ref, dg4_ref, loss_ref):
        fv = ff_ref[...]
        r = _rms(fv)
        err = h_ref[...] + fv * r * g4_ref[...] - t_ref[...]
        dy = err * (1.0 / D)
        dy_ref[...] = dy
        dff, dwt = _rms_bwd(fv, r, g4_ref[...], dy)
        dff_ref[...] = dff.astype(BF16)
        _acc_out(dg4_ref, jnp.sum(dwt, axis=0, keepdims=True))
        part = 0.5 * jnp.sum(jnp.mean(err * err, axis=-1, keepdims=True), axis=0, keepdims=True)
        _acc_out(loss_ref, jnp.broadcast_to(part, (1, LANE)))

    return _row_call(body, [h, ff, g4, target], [True, True, False, True],
                     [((T, D), BF16, "row"), ((T, D), F32, "row"), ((1, D), F32, "acc"), ((1, LANE), F32, "acc")],
                     "post2_loss", T)


def _mid_bwd(h, mix, dy, dxn2, g2, g3):
    T, D = h.shape

    def body(h_ref, mix_ref, dy_ref, dxn2_ref, g2_ref, g3_ref, dmix_ref, dh_ref, dg2_ref, dg3_ref):
        hv = h_ref[...]
        d1, dw3 = _rms_bwd(hv, _rms(hv), g3_ref[...], dxn2_ref[...])
        dh = dy_ref[...] + d1
        dh_ref[...] = dh
        mv = mix_ref[...]
        dmix, dw2 = _rms_bwd(mv, _rms(mv), g2_ref[...], dh)
        dmix_ref[...] = dmix.astype(BF16)
        _acc_out(dg2_ref, jnp.sum(dw2, axis=0, keepdims=True))
        _acc_out(dg3_ref, jnp.sum(dw3, axis=0, keepdims=True))

    return _row_call(body, [h, mix, dy, dxn2, g2, g3], [True, True, True, True, False, False],
                     [((T, D), BF16, "row"), ((T, D), F32, "row"), ((1, D), F32, "acc"), ((1, D), F32, "acc")],
                     "mid_bwd", T)


def _pre_bwd(x, dh, dxn, g1):
    T, D = x.shape

    def body(x_ref, dh_ref, dxn_ref, g1_ref, gx_ref, dg1_ref):
        xv = x_ref[...]
        d1, dw1 = _rms_bwd(xv, _rms(xv), g1_ref[...], dxn_ref[...])
        gx_ref[...] = dh_ref[...] + d1
        _acc_out(dg1_ref, jnp.sum(dw1, axis=0, keepdims=True))

    return _row_call(body, [x, dh, dxn, g1], [True, True, True, False],
                     [((T, D), F32, "row"), ((1, D), F32, "acc")], "pre_bwd", T)


def _shift_down(x, s):
    if s == 0:
        return x
    row = lax.broadcasted_iota(jnp.int32, x.shape, 0)
    return jnp.where(row >= s, pltpu.roll(x, s, axis=0), 0.0)


def _shift_up(x, s):
    if s == 0:
        return x
    n = x.shape[0]
    row = lax.broadcasted_iota(jnp.int32, x.shape, 0)
    return jnp.where(row < n - s, pltpu.roll(x, n - s, axis=0), 0.0)


def _conv(x, w):
    kw = w.shape[0]
    out = w[kw - 1:kw, :] * x
    for j in range(kw - 1):
        out = out + w[j:j + 1, :] * _shift_down(x, kw - 1 - j)
    return out


def _conv_bwd(x, w, dout):
    kw = w.shape[0]
    dx = w[kw - 1:kw, :] * dout
    dws = []
    for j in range(kw - 1):
        dx = dx + w[j:j + 1, :] * _shift_up(dout, kw - 1 - j)
        dws.append(jnp.sum(dout * _shift_down(x, kw - 1 - j), axis=0, keepdims=True))
    dws.append(jnp.sum(dout * x, axis=0, keepdims=True))
    return dx, jnp.concatenate(dws, axis=0)


def _qkvconv_fwd(proj, w, bsz, seq, gw):
    nct = gw // LANE
    kw = w.shape[0]

    def body(p_ref, w_ref, o_ref):
        cv = _conv(p_ref[...], w_ref[...])
        o_ref[...] = (cv * _sigmoid(cv)).reshape(o_ref.shape)

    return pl.pallas_call(
        body, grid=(3, bsz, nct),
        in_specs=[pl.BlockSpec((seq, LANE), lambda p, b, c: (b, p * nct + c)),
                  pl.BlockSpec((kw, LANE), lambda p, b, c: (0, p * nct + c))],
        out_specs=pl.BlockSpec((1, seq, LANE), lambda p, b, c: (p, b, c)),
        out_shape=jax.ShapeDtypeStruct((3, bsz * seq, gw), F32),
        compiler_params=_params("parallel", "parallel", "parallel"), name="qkvconv_fwd")(proj, w)


def _qkvconv_bwd(proj, w, dact, bsz, seq, gw):
    nct = gw // LANE
    kw = w.shape[0]

    def body(p_ref, w_ref, d_ref, dp_ref, dw_ref):
        pre = p_ref[...]
        wv = w_ref[...]
        cv = _conv(pre, wv)
        sg = _sigmoid(cv)
        dcv = d_ref[...].reshape(cv.shape) * (sg * (1.0 + cv * (1.0 - sg)))
        dpre, dw = _conv_bwd(pre, wv, dcv)
        dp_ref[...] = dpre.astype(BF16)
        b = pl.program_id(2)

        @pl.when(b == 0)
        def _():
            dw_ref[...] = dw

        @pl.when(b > 0)
        def _():
            dw_ref[...] += dw

    return pl.pallas_call(
        body, grid=(3, nct, bsz),
        in_specs=[pl.BlockSpec((seq, LANE), lambda p, c, b: (b, p * nct + c)),
                  pl.BlockSpec((kw, LANE), lambda p, c, b: (0, p * nct + c)),
                  pl.BlockSpec((1, seq, LANE), lambda p, c, b: (p, b, c))],
        out_specs=[pl.BlockSpec((seq, LANE), lambda p, c, b: (b, p * nct + c)),
                   pl.BlockSpec((kw, LANE), lambda p, c, b: (0, p * nct + c))],
        out_shape=[jax.ShapeDtypeStruct((bsz * seq, 3 * gw), BF16), jax.ShapeDtypeStruct((kw, 3 * gw), F32)],
        compiler_params=_params("parallel", "parallel", "arbitrary"), name="qkvconv_bwd")(proj, w, dact)


def _sc_fwd(proj, w, bsz, seq, scw, col0):
    nct = scw // LANE
    c0 = col0 // LANE
    kw = w.shape[0]

    def body(b_ref, c_ref, h_ref, w_ref, o_ref):
        o_ref[...] = (b_ref[...] * _conv(c_ref[...] * h_ref[...], w_ref[...])).astype(BF16)

    return pl.pallas_call(
        body, grid=(bsz, nct),
        in_specs=[pl.BlockSpec((seq, LANE), lambda b, c: (b, c0 + c)),
                  pl.BlockSpec((seq, LANE), lambda b, c: (b, c0 + nct + c)),
                  pl.BlockSpec((seq, LANE), lambda b, c: (b, c0 + 2 * nct + c)),
                  pl.BlockSpec((kw, LANE), lambda b, c: (0, c))],
        out_specs=pl.BlockSpec((seq, LANE), lambda b, c: (b, c)),
        out_shape=jax.ShapeDtypeStruct((bsz * seq, scw), BF16),
        compiler_params=_params("parallel", "parallel"), name="sc_fwd")(proj, proj, proj, w)


def _sc_bwd(proj, w, dout, bsz, seq, scw, col0, dcol0):
    nct = scw // LANE
    c0 = col0 // LANE
    d0 = dcol0 // LANE
    kw = w.shape[0]

    def body(b_ref, c_ref, h_ref, w_ref, d_ref, db_ref, dc_ref, dh_ref, dw_ref):
        cc, hh, wv, dv = c_ref[...], h_ref[...], w_ref[...], d_ref[...]
        m = cc * hh
        db_ref[...] = (dv * _conv(m, wv)).astype(BF16)
        dm, dw = _conv_bwd(m, wv, dv * b_ref[...])
        dc_ref[...] = (dm * hh).astype(BF16)
        dh_ref[...] = (dm * cc).astype(BF16)
        b = pl.program_id(1)

        @pl.when(b == 0)
        def _():
            dw_ref[...] = dw

        @pl.when(b > 0)
        def _():
            dw_ref[...] += dw

    return pl.pallas_call(
        body, grid=(nct, bsz),
        in_specs=[pl.BlockSpec((seq, LANE), lambda c, b: (b, c0 + c)),
                  pl.BlockSpec((seq, LANE), lambda c, b: (b, c0 + nct + c)),
                  pl.BlockSpec((seq, LANE), lambda c, b: (b, c0 + 2 * nct + c)),
                  pl.BlockSpec((kw, LANE), lambda c, b: (0, c)),
                  pl.BlockSpec((seq, LANE), lambda c, b: (b, d0 + c))],
        out_specs=[pl.BlockSpec((seq, LANE), lambda c, b: (b, c)),
                   pl.BlockSpec((seq, LANE), lambda c, b: (b, c)),
                   pl.BlockSpec((seq, LANE), lambda c, b: (b, c)),
                   pl.BlockSpec((kw, LANE), lambda c, b: (0, c))],
        out_shape=[jax.ShapeDtypeStruct((bsz * seq, scw), BF16)] * 3 + [jax.ShapeDtypeStruct((kw, scw), F32)],
        compiler_params=_params("parallel", "arbitrary"), name="sc_bwd")(proj, proj, proj, w, dout)


HEADS_PER_STEP = 4


def _colsel(tile, idx):
    lane = lax.broadcasted_iota(jnp.int32, tile.shape, 1)
    return jnp.sum(jnp.where(lane == idx, tile, 0.0), axis=1, keepdims=True)


def _rowsel(tile, idx):
    row = lax.broadcasted_iota(jnp.int32, tile.shape, 0)
    return jnp.sum(jnp.where(row == idx, tile, 0.0), axis=0, keepdims=True)


def _colput(col, idx, width=LANE):
    lane = lax.broadcasted_iota(jnp.int32, (col.shape[0], width), 1)
    return jnp.where(lane == idx, col, 0.0)


def _tri_masks(c):
    row = lax.broadcasted_iota(jnp.int32, (c, c), 0)
    col = lax.broadcasted_iota(jnp.int32, (c, c), 1)
    return row >= col, row > col, row == col


def _unit_lower_inverse(m):
    c = m.shape[0]
    _, _, eye = _tri_masks(c)
    p = -m
    t = jnp.where(eye, 1.0, 0.0) + p
    for _ in range(int(math.log2(c)) - 1):
        p = _hdot(p, p)
        t = t + _hdot(t, p)
    return t


def _gates(ab, alog, dtb):
    g = -jnp.exp(alog) * _softplus(ab + dtb)
    return g, _sigmoid(ab)


def _l2n(x):
    r = lax.rsqrt(jnp.sum(x * x, axis=-1, keepdims=True) + L2_EPS)
    return x * r, r


def _gdn_chunk_common(q, k, gc, gr, bc):
    c, dk = q.shape
    incl, strict, _ = _tri_masks(c)
    qh, rq = _l2n(q)
    kn, rk = _l2n(k)
    qn = qh * (dk ** -0.5)
    dm = jnp.where(incl, jnp.exp(jnp.where(incl, gc - gr, 0.0)), 0.0)
    kk = _bdot(kn, kn, NT)
    m = jnp.where(strict, bc * kk * dm, 0.0)
    pm = jnp.where(incl, _bdot(qn, kn, NT) * dm, 0.0)
    return qh, rq, kn, rk, qn, dm, kk, m, pm


def _gdn_fwd(qkv, proj, alog, dtb, gnw, bsz, seq, heads, z_col0, ab_col0):
    c = CHUNK
    nch = seq // c
    hb = min(HEADS_PER_STEP, heads)
    ng = heads // hb
    hd = qkv.shape[2] // heads
    wb = hb * hd

    def body(qkv_ref, z_ref, ab_ref, alog_ref, dtb_ref, gnw_ref, o_ref, ssave_ref, tsave_ref, s_ref, gc_ref, gt_ref, be_ref):
        n, hg = pl.program_id(1), pl.program_id(2)

        @pl.when((n == 0) & (hg == 0))
        def _():
            s_ref[...] = jnp.zeros_like(s_ref)

        @pl.when(hg == 0)
        def _():
            g, beta = _gates(ab_ref[...], alog_ref[...], dtb_ref[...])
            incl, _, _ = _tri_masks(c)
            gcum = _hdot(jnp.where(incl, 1.0, 0.0), g)
            gc_ref[...] = gcum
            gt_ref[...] = gcum.T
            be_ref[...] = beta

        for hh in range(hb):
            h = hg * hb + hh
            sl = slice(hh * hd, (hh + 1) * hd)
            gc = _colsel(gc_ref[...], h)
            gr = _rowsel(gt_ref[...], h)
            bc = _colsel(be_ref[...], heads + h)
            q, k, v = qkv_ref[0, :, sl], qkv_ref[1, :, sl], qkv_ref[2, :, sl]
            _, _, kn, _, qn, _, _, m, pm = _gdn_chunk_common(q, k, gc, gr, bc)
            tm = _unit_lower_inverse(m)
            s = s_ref[h]
            ssave_ref[0, 0, hh] = s
            tsave_ref[0, 0, hh] = tm
            gam = jnp.exp(gc)
            glast = gc[c - 1:c, :]
            r = v - gam * _bdot(kn, s)
            vn = _bdot(tm, bc * r)
            o = gam * _bdot(qn, s) + _bdot(pm, vn)
            s_ref[h] = s * jnp.exp(glast) + _bdot(kn * jnp.exp(glast - gc), vn, TN)
            on = o * lax.rsqrt(jnp.mean(o * o, axis=-1, keepdims=True) + NORM_EPS) * gnw_ref[...]
            zz = z_ref[:, sl]
            o_ref[:, sl] = (on * (zz * _sigmoid(zz))).astype(BF16)

    row = lambda b, n, g: b * nch + n
    return pl.pallas_call(
        body, grid=(bsz, nch, ng),
        in_specs=[pl.BlockSpec((3, c, wb), lambda b, n, g: (0, row(b, n, g), g)),
                  pl.BlockSpec((c, wb), lambda b, n, g: (row(b, n, g), z_col0 // wb + g)),
                  pl.BlockSpec((c, LANE), lambda b, n, g: (row(b, n, g), ab_col0 // LANE)),
                  pl.BlockSpec((1, LANE), lambda b, n, g: (0, 0)),
                  pl.BlockSpec((1, LANE), lambda b, n, g: (0, 0)),
                  pl.BlockSpec((1, hd), lambda b, n, g: (0, 0))],
        out_specs=[pl.BlockSpec((c, wb), lambda b, n, g: (row(b, n, g), g)),
                   pl.BlockSpec((1, 1, hb, hd, hd), lambda b, n, g: (b, n, g, 0, 0)),
                   pl.BlockSpec((1, 1, hb, c, c), lambda b, n, g: (b, n, g, 0, 0))],
        out_shape=[jax.ShapeDtypeStruct((bsz * seq, heads * hd), BF16),
                   jax.ShapeDtypeStruct((bsz, nch, heads, hd, hd), F32),
                   jax.ShapeDtypeStruct((bsz, nch, heads, c, c), F32)],
        scratch_shapes=[pltpu.VMEM((heads, hd, hd), F32), pltpu.VMEM((c, LANE), F32), pltpu.VMEM((LANE, c), F32),
                        pltpu.VMEM((c, LANE), F32)],
        compiler_params=_params("parallel", "arbitrary", "arbitrary"), name="gdn_fwd")(qkv, proj, proj, alog, dtb, gnw)


def _gdn_bwd(qkv, proj, alog, dtb, gnw, ssave, tsave, dout, bsz, seq, heads, z_col0, ab_col0):
    c = CHUNK
    nch = seq // c
    hb = min(HEADS_PER_STEP, heads)
    ng = heads // hb
    hd = qkv.shape[2] // heads
    wb = hb * hd

    def body(qkv_ref, z_ref, ab_ref, alog_ref, dtb_ref, gnw_ref, ssave_ref, tsave_ref, do_ref,
             dact_ref, dz_ref, dab_ref, dalog_ref, ddtb_ref, dgnw_ref,
             ds_ref, gc_ref, gt_ref, be_ref, dgacc_ref, dbacc_ref):
        n, hg = pl.program_id(1), pl.program_id(2)
        incl, strict, _ = _tri_masks(c)

        @pl.when((n == 0) & (hg == 0))
        def _():
            ds_ref[...] = jnp.zeros_like(ds_ref)
            dalog_ref[...] = jnp.zeros_like(dalog_ref)
            ddtb_ref[...] = jnp.zeros_like(ddtb_ref)
            dgnw_ref[...] = jnp.zeros_like(dgnw_ref)

        @pl.when(hg == 0)
        def _():
            g, beta = _gates(ab_ref[...], alog_ref[...], dtb_ref[...])
            gcum = _hdot(jnp.where(incl, 1.0, 0.0), g)
            gc_ref[...] = gcum
            gt_ref[...] = gcum.T
            be_ref[...] = beta
            dgacc_ref[...] = jnp.zeros_like(dgacc_ref)
            dbacc_ref[...] = jnp.zeros_like(dbacc_ref)

        dgnw = jnp.zeros((1, hd), F32)
        for hh in range(hb):
            h = hg * hb + hh
            sl = slice(hh * hd, (hh + 1) * hd)
            gc = _colsel(gc_ref[...], h)
            gr = _rowsel(gt_ref[...], h)
            bc = _colsel(be_ref[...], heads + h)
            q, k, v = qkv_ref[0, :, sl], qkv_ref[1, :, sl], qkv_ref[2, :, sl]
            qh, rq, kn, rk, qn, dm, kk, m, pm = _gdn_chunk_common(q, k, gc, gr, bc)
            tm = tsave_ref[0, 0, hh]
            s = ssave_ref[0, 0, hh]
            gam = jnp.exp(gc)
            glast = gc[c - 1:c, :]
            gl = jnp.exp(glast)
            ratio = jnp.exp(glast - gc)
            ks = _bdot(kn, s)
            r = v - gam * ks
            vn = _bdot(tm, bc * r)
            qs = _bdot(qn, s)
            o = gam * qs + _bdot(pm, vn)
            ro = lax.rsqrt(jnp.mean(o * o, axis=-1, keepdims=True) + NORM_EPS)
            zz = z_ref[:, sl]
            sz = _sigmoid(zz)
            dd = do_ref[:, sl]
            don = dd * (zz * sz)
            dz_ref[:, sl] = (dd * (o * ro * gnw_ref[...]) * (sz * (1.0 + zz * (1.0 - sz)))).astype(BF16)
            dgnw = dgnw + jnp.sum(don * o * ro, axis=0, keepdims=True)
            uu = don * gnw_ref[...]
            d_o = ro * uu - o * (ro * ro * ro) * jnp.mean(o * uu, axis=-1, keepdims=True)
            dqs = gam * d_o
            dq = _bdot(dqs, s, NT)
            dgam = jnp.sum(d_o * qs, axis=-1, keepdims=True)
            ds_new = _bdot(qn, dqs, TN)
            dp = jnp.where(incl, _bdot(d_o, vn, NT), 0.0)
            dvn = _bdot(pm, d_o, TN)
            ds_in = ds_ref[h]
            ds_new = ds_new + gl * ds_in
            dgl = jnp.sum(jnp.sum(ds_in * s, axis=1, keepdims=True), axis=0, keepdims=True)
            dkd = _bdot(vn, ds_in, NT)
            dvn = dvn + _bdot(kn * ratio, ds_in)
            dk = ratio * dkd
            dratio = jnp.sum(dkd * kn, axis=-1, keepdims=True)
            dpd = dp * dm
            dq = dq + _bdot(dpd, kn)
            dk = dk + _bdot(dpd, qn, TN)
            ff = dp * pm
            dx = _bdot(tm, dvn, TN)
            dr = bc * dx
            dbeta = jnp.sum(dx * r, axis=-1, keepdims=True)
            gdr = gam * dr
            dk = dk - _bdot(gdr, s, NT)
            dgam = dgam - jnp.sum(dr * ks, axis=-1, keepdims=True)
            ds_new = ds_new - _bdot(kn, gdr, TN)
            dmm = jnp.where(strict, -_bdot(dx, vn, NT), 0.0)
            ee = dmm * dm
            dbeta = dbeta + jnp.sum(ee * kk, axis=-1, keepdims=True)
            be_e = bc * ee
            dk = dk + _bdot(be_e, kn) + _bdot(be_e, kn, TN)
            ff = ff + dmm * m
            dgc = (jnp.sum(ff, axis=-1, keepdims=True) - jnp.sum(ff.T, axis=-1, keepdims=True)
                   + dgam * gam - dratio * ratio)
            dglast = jnp.sum(dratio * ratio, axis=0, keepdims=True) + dgl * gl
            rowi = lax.broadcasted_iota(jnp.int32, (c, 1), 0)
            dgc = dgc + jnp.where(rowi == c - 1, dglast, 0.0)
            dgacc_ref[...] += _colput(dgc, h)
            dbacc_ref[...] += _colput(dbeta, heads + h)
            ds_ref[h] = ds_new
            dqh = dq * (hd ** -0.5)
            dact_ref[0, :, sl] = rq * (dqh - qh * jnp.sum(qh * dqh, axis=-1, keepdims=True))
            dact_ref[1, :, sl] = rk * (dk - kn * jnp.sum(kn * dk, axis=-1, keepdims=True))
            dact_ref[2, :, sl] = dr
        dgnw_ref[0] += dgnw

        @pl.when(hg == ng - 1)
        def _():
            ab = ab_ref[...]
            ea = jnp.exp(alog_ref[...])
            g = -ea * _softplus(ab + dtb_ref[...])
            beta = be_ref[...]
            dg = _hdot(jnp.where(incl, 1.0, 0.0), dgacc_ref[...], TN)
            lane = lax.broadcasted_iota(jnp.int32, ab.shape, 1)
            da = jnp.where(lane < heads, dg * (-ea) * _sigmoid(ab + dtb_ref[...]), 0.0)
            db = dbacc_ref[...] * beta * (1.0 - beta)
            dab_ref[...] = (da + db).astype(BF16)
            dalog_ref[0] += jnp.sum(jnp.where(lane < heads, dg * g, 0.0), axis=0, keepdims=True)
            ddtb_ref[0] += jnp.sum(da, axis=0, keepdims=True)

    row = lambda b, n, g: b * nch + (nch - 1 - n)
    rev = lambda n: nch - 1 - n
    return pl.pallas_call(
        body, grid=(bsz, nch, ng),
        in_specs=[pl.BlockSpec((3, c, wb), lambda b, n, g: (0, row(b, n, g), g)),
                  pl.BlockSpec((c, wb), lambda b, n, g: (row(b, n, g), z_col0 // wb + g)),
                  pl.BlockSpec((c, LANE), lambda b, n, g: (row(b, n, g), ab_col0 // LANE)),
                  pl.BlockSpec((1, LANE), lambda b, n, g: (0, 0)),
                  pl.BlockSpec((1, LANE), lambda b, n, g: (0, 0)),
                  pl.BlockSpec((1, hd), lambda b, n, g: (0, 0)),
                  pl.BlockSpec((1, 1, hb, hd, hd), lambda b, n, g: (b, rev(n), g, 0, 0)),
                  pl.BlockSpec((1, 1, hb, c, c), lambda b, n, g: (b, rev(n), g, 0, 0)),
                  pl.BlockSpec((c, wb), lambda b, n, g: (row(b, n, g), g))],
        out_specs=[pl.BlockSpec((3, c, wb), lambda b, n, g: (0, row(b, n, g), g)),
                   pl.BlockSpec((c, wb), lambda b, n, g: (row(b, n, g), g)),
                   pl.BlockSpec((c, LANE), lambda b, n, g: (row(b, n, g), 0)),
                   pl.BlockSpec((1, 1, LANE), lambda b, n, g: (b, 0, 0)),
                   pl.BlockSpec((1, 1, LANE), lambda b, n, g: (b, 0, 0)),
                   pl.BlockSpec((1, 1, hd), lambda b, n, g: (b, 0, 0))],
        out_shape=[jax.ShapeDtypeStruct((3, bsz * seq, heads * hd), F32),
                   jax.ShapeDtypeStruct((bsz * seq, heads * hd), BF16),
                   jax.ShapeDtypeStruct((bsz * seq, LANE), BF16),
                   jax.ShapeDtypeStruct((bsz, 1, LANE), F32),
                   jax.ShapeDtypeStruct((bsz, 1, LANE), F32),
                   jax.ShapeDtypeStruct((bsz, 1, hd), F32)],
        scratch_shapes=[pltpu.VMEM((heads, hd, hd), F32), pltpu.VMEM((c, LANE), F32), pltpu.VMEM((LANE, c), F32),
                        pltpu.VMEM((c, LANE), F32), pltpu.VMEM((c, LANE), F32), pltpu.VMEM((c, LANE), F32)],
        compiler_params=_params("parallel", "arbitrary", "arbitrary"), name="gdn_bwd")(
            qkv, proj, proj, alog, dtb, gnw, ssave, tsave, dout)


ELEMWISE_BLOCK_ELEMS = 256 * 1024


def _rows_tile(rows, cols):
    want = max(16, ELEMWISE_BLOCK_ELEMS // cols)
    if rows <= want:
        return rows
    t = (want // 16) * 16
    while t > 16 and rows % t:
        t -= 16
    return t if rows % t == 0 else rows


def _piece_specs(pieces, tr, cols):
    specs, leads = [], []
    for p, (arr, lead) in enumerate(pieces):
        if arr.ndim == 3:
            specs.append(pl.BlockSpec((1, tr, cols), functools.partial(lambda i, idx, p: (idx[p], i, 0), p=p)))
        else:
            specs.append(pl.BlockSpec((tr, cols), lambda i, idx: (i, 0)))
        leads.append(jnp.asarray(0 if lead is None else lead, jnp.int32))
    return jnp.stack(leads), specs


def _sum_pieces(refs):
    total = None
    for r in refs:
        v = r[...].astype(F32)
        v = v.reshape(v.shape[-2:])
        total = v if total is None else total + v
    return total


def _adamw(w, m, v, pieces, name):
    rows, cols = w.shape
    tr = _rows_tile(rows, cols)
    leads, pspecs = _piece_specs(pieces, tr, cols)
    npc = len(pieces)
    c1 = 1.0 - ADAM_B1 ** ADAM_STEP
    c2 = 1.0 - ADAM_B2 ** ADAM_STEP

    def body(idx_ref, w_ref, m_ref, v_ref, *rest):
        g = _sum_pieces(rest[:npc])
        g_ref, d_ref, nm_ref, nv_ref = rest[npc:]
        nm = ADAM_B1 * m_ref[...] + (1.0 - ADAM_B1) * g
        nv = ADAM_B2 * v_ref[...] + (1.0 - ADAM_B2) * (g * g)
        g_ref[...] = g
        nm_ref[...] = nm
        nv_ref[...] = nv
        d_ref[...] = -ADAM_LR * ((nm / c1) / (jnp.sqrt(nv / c2) + ADAM_EPS) + ADAM_WD * w_ref[...])

    wspec = pl.BlockSpec((tr, cols), lambda i, idx: (i, 0))
    return pl.pallas_call(
        body,
        grid_spec=pltpu.PrefetchScalarGridSpec(num_scalar_prefetch=1, grid=(rows // tr,),
                                               in_specs=[wspec] * 3 + pspecs, out_specs=[wspec] * 4),
        out_shape=[jax.ShapeDtypeStruct((rows, cols), F32)] * 4,
        compiler_params=_params("parallel"), name=name)(leads, w, m, v, *[p for p, _ in pieces])


def _sum_to(pieces, out_dtype, name):
    arr0 = pieces[0][0]
    rows, cols = arr0.shape[-2:]
    tr = _rows_tile(rows, cols)
    leads, pspecs = _piece_specs(pieces, tr, cols)

    def body(idx_ref, *rest):
        rest[-1][...] = _sum_pieces(rest[:-1]).astype(out_dtype)

    return pl.pallas_call(
        body,
        grid_spec=pltpu.PrefetchScalarGridSpec(num_scalar_prefetch=1, grid=(rows // tr,), in_specs=pspecs,
                                               out_specs=pl.BlockSpec((tr, cols), lambda i, idx: (i, 0))),
        out_shape=jax.ShapeDtypeStruct((rows, cols), out_dtype),
        compiler_params=_params("parallel"), name=name)(leads, *[p for p, _ in pieces])


def _pair_add(a, recv, my_c, name):
    _, rows, cols = a.shape
    tr = _rows_tile(rows, cols)

    def body(c_ref, a_ref, r_ref, o_ref):
        o_ref[...] = (a_ref[...].astype(F32) + r_ref[...].astype(F32)).astype(BF16)

    return pl.pallas_call(
        body,
        grid_spec=pltpu.PrefetchScalarGridSpec(
            num_scalar_prefetch=1, grid=(4, rows // tr),
            in_specs=[pl.BlockSpec((1, tr, cols), lambda j, i, c: (2 * j + c[0], i, 0)),
                      pl.BlockSpec((1, tr, cols), lambda j, i, c: (j, i, 0))],
            out_specs=pl.BlockSpec((1, tr, cols), lambda j, i, c: (j, i, 0))),
        out_shape=jax.ShapeDtypeStruct((4, rows, cols), BF16),
        compiler_params=_params("parallel", "parallel"), name=name)(jnp.reshape(my_c, (1,)).astype(jnp.int32), a, recv)


def _assemble(frames, table, n_blocks, width, name):
    _, rows, _ = frames.shape
    tr = _tile(rows, 1024)

    def body(t_ref, f1_ref, f2_ref, o_ref):
        jb = pl.program_id(0)
        v = f1_ref[0]
        o_ref[...] = jnp.where(t_ref[4, jb] > 0, v + f2_ref[0], v)

    return pl.pallas_call(
        body,
        grid_spec=pltpu.PrefetchScalarGridSpec(
            num_scalar_prefetch=1, grid=(n_blocks, rows // tr),
            in_specs=[pl.BlockSpec((1, tr, width), lambda jb, i, t: (t[0, jb], i, t[1, jb])),
                      pl.BlockSpec((1, tr, width), lambda jb, i, t: (t[2, jb], i, t[3, jb]))],
            out_specs=pl.BlockSpec((tr, width), lambda jb, i, t: (i, jb))),
        out_shape=jax.ShapeDtypeStruct((rows, n_blocks * width), BF16),
        compiler_params=_params("parallel", "parallel"), name=name)(table, frames, frames)


ANY = pl.BlockSpec(memory_space=pl.ANY)


def _place():
    x, y, c = lax.axis_index("x"), lax.axis_index("y"), lax.axis_index("c")
    chips = [(1 - x, y), (x, 1 - y), (1 - x, 1 - y)]
    return x, y, c, chips


def _allgather_big(shards, name):
    n = len(shards)

    def body(*refs):
        xs, outs = refs[:n], refs[n:2 * n]
        send_sems, recv_sems, local_sems = refs[2 * n:]
        x, y, c, chips = _place()
        me, sibling = (x, y, c), (x, y, 1 - c)

        def copy(a, k, block, to, src=None):
            dst = outs[a].at[4 * block[0] + 2 * block[1] + block[2]]
            return pltpu.make_async_remote_copy(src_ref=dst if src is None else src, dst_ref=dst,
                                                send_sem=send_sems.at[a, k], recv_sem=recv_sems.at[a, k],
                                                device_id=to, device_id_type=MESH)

        mine = [pltpu.make_async_copy(xs[a], outs[a].at[4 * x + 2 * y + c], local_sems.at[a]) for a in range(n)]
        for cp in mine:
            cp.start()
        first = []
        for a in range(n):
            first.append(copy(a, 0, me, sibling, src=xs[a]))
            first += [copy(a, 1 + j, me, (*chip, c), src=xs[a]) for j, chip in enumerate(chips)]
        for cp in first:
            cp.start()
        passed = []
        for j, chip in enumerate(chips):
            for a in range(n):
                copy(a, 1 + j, (*chip, c), me).wait_recv()
                cp = copy(a, 4 + j, (*chip, c), sibling)
                cp.start()
                passed.append(cp)
        for a in range(n):
            copy(a, 0, sibling, me).wait_recv()
            for j, chip in enumerate(chips):
                copy(a, 4 + j, (*chip, 1 - c), me).wait_recv()
        for cp in first + passed:
            cp.wait_send()
        for cp in mine:
            cp.wait()

    return pl.pallas_call(
        body, in_specs=[ANY] * n, out_specs=[ANY] * n,
        out_shape=[jax.ShapeDtypeStruct((N_DEV,) + s.shape, s.dtype) for s in shards],
        scratch_shapes=[pltpu.SemaphoreType.DMA((n, 7)), pltpu.SemaphoreType.DMA((n, 7)), pltpu.SemaphoreType.DMA((n,))],
        name=name)(*shards)


def _allreduce_small(buf, name):
    rows = buf.shape[0]

    def body(x_ref, o_ref, g_ref, send_sems, recv_sems):
        x, y, c, chips = _place()
        me, sibling = (x, y, c), (x, y, 1 - c)

        def copy(k, block, to, src=None):
            dst = g_ref.at[4 * block[0] + 2 * block[1] + block[2]]
            return pltpu.make_async_remote_copy(src_ref=dst if src is None else src, dst_ref=dst,
                                                send_sem=send_sems.at[k], recv_sem=recv_sems.at[k],
                                                device_id=to, device_id_type=MESH)

        first = [copy(0, me, sibling, src=x_ref)]
        first += [copy(1 + j, me, (*chip, c), src=x_ref) for j, chip in enumerate(chips)]
        for cp in first:
            cp.start()
        passed = [copy(4 + j, (*chip, c), sibling) for j, chip in enumerate(chips)]
        for j, chip in enumerate(chips):
            copy(1 + j, (*chip, c), me).wait_recv()
            passed[j].start()
        copy(0, sibling, me).wait_recv()
        for j, chip in enumerate(chips):
            copy(4 + j, (*chip, 1 - c), me).wait_recv()
        for cp in first + passed:
            cp.wait_send()
        g_ref[4 * x + 2 * y + c] = x_ref[...]
        total = g_ref[0]
        for s in range(1, N_DEV):
            total = total + g_ref[s]
        o_ref[...] = total

    vm = pl.BlockSpec(memory_space=pltpu.VMEM)
    return pl.pallas_call(
        body, in_specs=[vm], out_specs=vm, out_shape=jax.ShapeDtypeStruct((rows, LANE), F32),
        scratch_shapes=[pltpu.VMEM((N_DEV, rows, LANE), F32), pltpu.SemaphoreType.DMA((7,)), pltpu.SemaphoreType.DMA((7,))],
        name=name)(buf)


def _rs_sibling(grads, name):
    n = len(grads)

    def body(*refs):
        gs, outs = refs[:n], refs[n:2 * n]
        send_sems, recv_sems = refs[2 * n:]
        x, y, c, _ = _place()
        copies = []
        for a in range(n):
            for j in range(4):
                copies.append(pltpu.make_async_remote_copy(
                    src_ref=gs[a].at[2 * j + (1 - c)], dst_ref=outs[a].at[j],
                    send_sem=send_sems.at[a, j], recv_sem=recv_sems.at[a, j],
                    device_id=(x, y, 1 - c), device_id_type=MESH))
        for cp in copies:
            cp.start()
        for cp in copies:
            cp.wait_recv()
        for cp in copies:
            cp.wait_send()

    return pl.pallas_call(
        body, in_specs=[ANY] * n, out_specs=[ANY] * n,
        out_shape=[jax.ShapeDtypeStruct((4,) + g.shape[1:], g.dtype) for g in grads],
        scratch_shapes=[pltpu.SemaphoreType.DMA((n, 4)), pltpu.SemaphoreType.DMA((n, 4))],
        name=name)(*grads)


def _rs_chips(pairs, name):
    n = len(pairs)

    def body(*refs):
        ps, outs = refs[:n], refs[n:2 * n]
        send_sems, recv_sems = refs[2 * n:]
        x, y, c, chips = _place()
        copies = []
        for a in range(n):
            for j, chip in enumerate(chips):
                copies.append(pltpu.make_async_remote_copy(
                    src_ref=ps[a].at[2 * chip[0] + chip[1]], dst_ref=outs[a].at[j],
                    send_sem=send_sems.at[a, j], recv_sem=recv_sems.at[a, j],
                    device_id=(*chip, c), device_id_type=MESH))
        for cp in copies:
            cp.start()
        for cp in copies:
            cp.wait_recv()
        for cp in copies:
            cp.wait_send()

    return pl.pallas_call(
        body, in_specs=[ANY] * n, out_specs=[ANY] * n,
        out_shape=[jax.ShapeDtypeStruct((3,) + p.shape[1:], p.dtype) for p in pairs],
        scratch_shapes=[pltpu.SemaphoreType.DMA((n, 3)), pltpu.SemaphoreType.DMA((n, 3))],
        name=name)(*pairs)


class _InLayout:
    def __init__(self, n_in, gw, heads, scw):
        self.n_in, self.split = n_in, 4 * gw + 2 * heads
        self.gap = LANE - 2 * heads
        self.ab_col, self.sc_col = 4 * gw, 4 * gw + LANE
        self.used = 4 * gw + LANE + 3 * scw
        p0 = [s * n_in + (self.gap if s * n_in >= self.split else 0) for s in range(N_DEV)]
        self.fstart = [(p // MXU) * MXU for p in p0]
        need = []
        for s in range(N_DEV):
            straddle = s * n_in < self.split < (s + 1) * n_in
            need.append(p0[s] - self.fstart[s] + n_in + (self.gap if straddle else 0))
        self.fw = -(-max(need) // MXU) * MXU
        self.wp = max(f + self.fw for f in self.fstart)
        assert self.wp >= self.used and self.wp % MXU == 0
        nfb = self.fw // MXU
        rows = []
        for jb in range(self.wp // MXU):
            src = [(s, jb - self.fstart[s] // MXU) for s in range(N_DEV) if 0 <= jb - self.fstart[s] // MXU < nfb]
            assert 1 <= len(src) <= 2, (jb, src)
            (s1, b1), (s2, b2) = src[0], src[-1]
            rows.append((s1, b1, s2, b2, int(len(src) == 2)))
        self.table = np.asarray(rows, np.int32).T.copy()

    def frame_block(self, s):
        p = s * self.n_in
        return (p + jnp.where(p >= self.split, self.gap, 0)) // MXU

    def offsets(self, s):
        p = s * self.n_in
        after = p >= self.split
        off1 = p + jnp.where(after, self.gap, 0) - self.frame_block(s) * MXU
        len1 = jnp.where(after, self.n_in, jnp.clip(self.split - p, 0, self.n_in))
        off2 = off1 + jnp.where(len1 < self.n_in, self.gap, 0)
        return off1, len1, off2

    def to_frame(self, w, s):
        off1, len1, off2 = self.offsets(s)
        zero = jnp.zeros((w.shape[0], self.fw), w.dtype)
        f1 = lax.dynamic_update_slice(zero, w, (0, off1))
        f2 = lax.dynamic_update_slice(zero, w, (0, off2))
        col = lax.broadcasted_iota(jnp.int32, (1, self.fw), 1)
        return jnp.where(col < off1 + len1, f1, jnp.where(col >= off2 + len1, f2, jnp.zeros_like(f2)))

    def from_frame(self, f, s):
        off1, len1, off2 = self.offsets(s)
        a = lax.dynamic_slice(f, (0, off1), (f.shape[0], self.n_in))
        b = lax.dynamic_slice(f, (0, off2), (f.shape[0], self.n_in))
        col = lax.broadcasted_iota(jnp.int32, (1, self.n_in), 1)
        return jnp.where(col < len1, a, b)


def _pack_rows(parts):
    rows = []
    for p in parts:
        flat = p.reshape(-1)
        pad = (-flat.shape[0]) % LANE
        rows.append(jnp.pad(flat, (0, pad)).reshape(-1, LANE))
    buf = jnp.concatenate(rows, axis=0)
    return jnp.pad(buf, ((0, (-buf.shape[0]) % 8), (0, 0)))


def _unpack_rows(buf, shapes):
    out, r = [], 0
    for shp in shapes:
        size = int(np.prod(shp))
        nr = -(-size // LANE)
        out.append(buf[r:r + nr].reshape(-1)[:size].reshape(shp))
        r += nr
    return out


def _pad_lanes(v):
    return jnp.pad(v, ((0, 0), (0, LANE - v.shape[1])))


def kernel(x, norm_mix_pre, w_in, conv_qkv_w, a_log, dt_bias, gdn_norm_w, conv_sc_w, w_out, norm_mix_post, norm_mlp_pre, w_up, w_down, norm_mlp_post, loss_target, m_norm_mix_pre, m_w_in, m_conv_qkv_w, m_a_log, m_dt_bias, m_gdn_norm_w, m_conv_sc_w, m_w_out, m_norm_mix_post, m_norm_mlp_pre, m_w_up, m_w_down, m_norm_mlp_post, v_norm_mix_pre, v_w_in, v_conv_qkv_w, v_a_log, v_dt_bias, v_gdn_norm_w, v_conv_sc_w, v_w_out, v_norm_mix_post, v_norm_mlp_pre, v_w_up, v_w_down, v_norm_mlp_post):
    bsz, seq, d = x.shape
    t = bsz * seq
    heads, hd = a_log.shape[-1], gdn_norm_w.shape[-1]
    gw = heads * hd
    scw = conv_sc_w.shape[-1] * N_DEV
    dff_w = w_up.shape[-1] * N_DEV
    lay = _InLayout(w_in.shape[-1], gw, heads, scw)
    mx, my, mc = lax.axis_index("x"), lax.axis_index("y"), lax.axis_index("c")
    me = 4 * mx + 2 * my + mc
    chip = 2 * mx + my

    x2 = x.reshape(t, d)
    tgt = loss_target.reshape(t, d)
    g1, g2, g3, g4 = norm_mix_pre, norm_mix_post, norm_mlp_pre, norm_mlp_post

    frame = lay.to_frame(w_in[0].astype(BF16), me)
    g_in, g_out, g_up, g_down = _allgather_big(
        [frame, w_out[0].astype(BF16), w_up[0].astype(BF16), w_down[0].astype(BF16)], "allgather_weights")
    w_pad = _assemble(g_in, jnp.asarray(lay.table), lay.wp // MXU, MXU, "assemble_w_in")
    w_out_f = g_out.reshape(d, d)
    w_down_f = g_down.reshape(dff_w, d)
    up_cols = dff_w // N_DEV
    kq, ks = conv_qkv_w.shape[1], conv_sc_w.shape[1]
    cq_n, cs_n = conv_qkv_w.shape[-1], conv_sc_w.shape[-1]
    cq_full = lax.dynamic_update_slice(jnp.zeros((kq, 3 * gw), F32), conv_qkv_w[0], (0, me * cq_n))
    cs_full = lax.dynamic_update_slice(jnp.zeros((ks, scw), F32), conv_sc_w[0], (0, me * cs_n))
    conv_q, conv_s = _unpack_rows(_allreduce_small(_pack_rows([cq_full, cs_full]), "allgather_conv"),
                                  [(kq, 3 * gw), (ks, scw)])
    alog_t, dtb_t = _pad_lanes(a_log), _pad_lanes(dt_bias)

    xn = _rms_fwd(x2, g1)
    (proj,) = _matmul(xn, w_pad, mode="nn", out_dtypes=[F32], name="in_proj", tn=768)
    qkv = _qkvconv_fwd(proj, conv_q, bsz, seq, gw)
    gdn_out, ssave, tsave = _gdn_fwd(qkv, proj, alog_t, dtb_t, gdn_norm_w, bsz, seq, heads, 3 * gw, lay.ab_col)
    sc_out = _sc_fwd(proj, conv_s, bsz, seq, scw, lay.sc_col)
    mixed = jnp.concatenate([gdn_out, sc_out], axis=1)
    (mix,) = _matmul(mixed, w_out_f, mode="nn", out_dtypes=[F32], name="out_proj")
    h, xn2 = _post1(x2, mix, g2, g3)

    def up_epilogue(acc):
        r = jnp.maximum(acc, 0.0)
        return r, r * r

    act, hid = _matmul(xn2, g_up, mode="nn", out_dtypes=[BF16, BF16], name="mlp_up", n_cols=dff_w, epilogue=up_epilogue,
                       b_spec=lambda tk, tn: pl.BlockSpec((1, tk, tn), lambda i, j, k: (j // (up_cols // tn), k, j % (up_cols // tn))))
    (ff,) = _matmul(hid, w_down_f, mode="nn", out_dtypes=[F32], name="mlp_down")
    dff, dy, dg4, loss_p = _post2_loss(h, ff, g4, tgt)

    (dpre,) = _matmul(dff, w_down_f, mode="nt", out_dtypes=[BF16], name="d_hidden", extras=[act],
                      epilogue=lambda acc, a: (acc * (2.0 * a.astype(F32)),))
    (dw_down,) = _matmul(hid, dff, mode="tn", out_dtypes=[BF16], name="dw_down")
    (dxn2,) = _matmul(dpre, g_up, mode="nt", out_dtypes=[F32], name="d_xn2", n_cols=d, tk=min(up_cols, 2048),
                      b_spec=lambda tk, tn: pl.BlockSpec((1, tn, tk), lambda i, j, k: (k // (up_cols // tk), j, k % (up_cols // tk))))
    (dw_up,) = _matmul(xn2, dpre, mode="tn", out_dtypes=[BF16], name="dw_up",
                       out_custom=lambda tm, tn: ((N_DEV, d, up_cols), (1, tm, tn),
                                                  lambda i, j, k: (j // (up_cols // tn), i, j % (up_cols // tn))))
    dmix, dh, dg2, dg3 = _mid_bwd(h, mix, dy, dxn2, g2, g3)
    (dmixed,) = _matmul(dmix, w_out_f, mode="nt", out_dtypes=[F32], name="d_mixed")
    (dw_out,) = _matmul(mixed, dmix, mode="tn", out_dtypes=[BF16], name="dw_out")
    dscb, dscc, dsch, dconv_s = _sc_bwd(proj, conv_s, dmixed, bsz, seq, scw, lay.sc_col, gw)
    dact, dz, dab, dalog, ddtb, dgnw = _gdn_bwd(qkv, proj, alog_t, dtb_t, gdn_norm_w, ssave, tsave, dmixed,
                                                bsz, seq, heads, 3 * gw, lay.ab_col)
    dqkv, dconv_q = _qkvconv_bwd(proj, conv_q, dact, bsz, seq, gw)
    dproj = jnp.concatenate([dqkv, dz, dab, dscb, dscc, dsch, jnp.zeros((t, lay.wp - lay.used), BF16)], axis=1)
    (dxn,) = _matmul(dproj, w_pad, mode="nt", out_dtypes=[F32], name="d_xn", tk=768)
    nfb = lay.fw // MXU
    (dw_in,) = _matmul(xn, dproj, mode="tn", out_dtypes=[BF16], name="dw_in", n_cols=N_DEV * lay.fw, tn=MXU,
                       b_spec=lambda tk, tn: pl.BlockSpec((tk, tn), lambda i, j, k: (k, lay.frame_block(j // nfb) + j % nfb)),
                       out_custom=lambda tm, tn: ((N_DEV, d, lay.fw), (1, tm, tn), lambda i, j, k: (j // nfb, i, j % nfb)))
    grad_x, dg1 = _pre_bwd(x2, dh, dxn, g1)

    partial = [dw_in, dw_out.reshape(N_DEV, d // N_DEV, d), dw_up, dw_down.reshape(N_DEV, dff_w // N_DEV, d)]
    from_sibling = _rs_sibling(partial, "rs_sibling")
    pairs = [_pair_add(a, r, mc, "pair_add_%d" % i) for i, (a, r) in enumerate(zip(partial, from_sibling))]
    from_chips = _rs_chips(pairs, "rs_chips")

    def pieces(i):
        return [(pairs[i], chip), (from_chips[i], 0), (from_chips[i], 1), (from_chips[i], 2)]

    gin_frame = _sum_to(pieces(0), F32, "grad_w_in_frame")
    big = {
        "w_in": _adamw(w_in[0], m_w_in[0], v_w_in[0], [(lay.from_frame(gin_frame, me), None)], "adamw_w_in"),
        "w_out": _adamw(w_out[0], m_w_out[0], v_w_out[0], pieces(1), "adamw_w_out"),
        "w_up": _adamw(w_up[0], m_w_up[0], v_w_up[0], pieces(2), "adamw_w_up"),
        "w_down": _adamw(w_down[0], m_w_down[0], v_w_down[0], pieces(3), "adamw_w_down"),
    }

    small_shapes = [(kq, 3 * gw), (ks, scw), (1, d), (1, d), (1, d), (1, d), (1, LANE), (1, LANE), (1, hd), (1, LANE)]
    small = _unpack_rows(
        _allreduce_small(_pack_rows([dconv_q, dconv_s, dg1, dg2, dg3, dg4, jnp.sum(dalog, axis=0), jnp.sum(ddtb, axis=0),
                                     jnp.sum(dgnw, axis=0), loss_p]), "allreduce_small"), small_shapes)
    gq, gs, sg1, sg2, sg3, sg4, salog, sdtb, sgnw, sloss = small
    loss = sloss[0, 0]
    small_grads = {
        "norm_mix_pre": sg1, "conv_qkv_w": lax.dynamic_slice(gq, (0, me * cq_n), (kq, cq_n)),
        "a_log": salog[:, :heads], "dt_bias": sdtb[:, :heads], "gdn_norm_w": sgnw,
        "conv_sc_w": lax.dynamic_slice(gs, (0, me * cs_n), (ks, cs_n)),
        "norm_mix_post": sg2, "norm_mlp_pre": sg3, "norm_mlp_post": sg4,
    }
    weights = {"norm_mix_pre": (norm_mix_pre, m_norm_mix_pre, v_norm_mix_pre), "conv_qkv_w": (conv_qkv_w[0], m_conv_qkv_w[0], v_conv_qkv_w[0]),
               "a_log": (a_log, m_a_log, v_a_log), "dt_bias": (dt_bias, m_dt_bias, v_dt_bias),
               "gdn_norm_w": (gdn_norm_w, m_gdn_norm_w, v_gdn_norm_w), "conv_sc_w": (conv_sc_w[0], m_conv_sc_w[0], v_conv_sc_w[0]),
               "norm_mix_post": (norm_mix_post, m_norm_mix_post, v_norm_mix_post),
               "norm_mlp_pre": (norm_mlp_pre, m_norm_mlp_pre, v_norm_mlp_pre),
               "norm_mlp_post": (norm_mlp_post, m_norm_mlp_post, v_norm_mlp_post)}
    res = dict(big)
    for name, (w, m, v) in weights.items():
        res[name] = _adamw(w, m, v, [(small_grads[name], None)], "adamw_" + name)

    order = ["norm_mix_pre", "w_in", "conv_qkv_w", "a_log", "dt_bias", "gdn_norm_w", "conv_sc_w", "w_out", "norm_mix_post",
             "norm_mlp_pre", "w_up", "w_down", "norm_mlp_post"]
    shapes = {"norm_mix_pre": norm_mix_pre.shape, "w_in": w_in.shape, "conv_qkv_w": conv_qkv_w.shape, "a_log": a_log.shape,
              "dt_bias": dt_bias.shape, "gdn_norm_w": gdn_norm_w.shape, "conv_sc_w": conv_sc_w.shape, "w_out": w_out.shape,
              "norm_mix_post": norm_mix_post.shape, "norm_mlp_pre": norm_mlp_pre.shape, "w_up": w_up.shape,
              "w_down": w_down.shape, "norm_mlp_post": norm_mlp_post.shape}
    outs = [loss, grad_x.reshape(bsz, seq, d)]
    for part in range(4):
        outs += [res[nm][part].reshape(shapes[nm]) for nm in order]
    return tuple(outs)
```

```python
import functools
import math

import numpy as np
import jax
import jax.numpy as jnp
from jax import lax
from jax.experimental import pallas as pl
from jax.experimental.pallas import tpu as pltpu

F32 = jnp.float32
BF16 = jnp.bfloat16
HI = lax.Precision.HIGHEST
MESH = pl.DeviceIdType.MESH

N_DEV = 8
LANE = 128
MXU = 256
CHUNK = 64
NORM_EPS = 1e-6
L2_EPS = 1e-6
VMEM_LIMIT = 56 * 1024 * 1024

ADAM_LR = 0.001
ADAM_B1 = 0.9
ADAM_B2 = 0.999
ADAM_EPS = 1e-08
ADAM_WD = 0.01
ADAM_STEP = 10

NN = (((1,), (0,)), ((), ()))
NT = (((1,), (1,)), ((), ()))
TN = (((0,), (0,)), ((), ()))


def _params(*sem):
    return pltpu.CompilerParams(dimension_semantics=sem, vmem_limit_bytes=VMEM_LIMIT)


def _tile(n, want):
    if n <= want:
        return n
    t = (want // LANE) * LANE
    while t > LANE and n % t:
        t -= LANE
    assert n % t == 0, (n, want)
    return t


class _Comm:
    def __init__(self, srcs, lands, plan, n, alias=None):
        self.srcs, self.lands, self.plan, self.n, self.alias = list(srcs), list(lands), plan, n, dict(alias or {})


def _pcall(body, *, grid, in_specs, out_specs, out_shape, operands, name, scratch_shapes=(), semantics=None,
           prefetch=(), comms=()):
    n_pf, n_in, n_out, n_scr = len(prefetch), len(in_specs), len(out_specs), len(scratch_shapes)
    srcs = [s for cm in comms for s in cm.srcs]
    lands = [l for cm in comms for l in cm.lands]
    n_src, n_land = len(srcs), len(lands)
    n_copies = sum(cm.n for cm in comms)
    aliases, so, lo = {}, 0, 0
    for cm in comms:
        for a, b in cm.alias.items():
            aliases[n_pf + n_in + so + a] = n_out + lo + b
        so, lo = so + len(cm.srcs), lo + len(cm.lands)
    any_spec = pl.BlockSpec(memory_space=pl.ANY)

    def wrapped(*refs):
        pf, r = refs[:n_pf], refs[n_pf:]
        ins, csrc = r[:n_in], r[n_in:n_in + n_src]
        outs = r[n_in + n_src:n_in + n_src + n_out]
        cland = r[n_in + n_src + n_out:n_in + n_src + n_out + n_land]
        rest = r[n_in + n_src + n_out + n_land:]
        scratch = rest[:n_scr]
        if not comms:
            body(*pf, *ins, *outs, *scratch)
            return
        send_sems, recv_sems = rest[n_scr:]

        def copies():
            out, k, s0, l0 = [], 0, 0, 0
            for cm in comms:
                for kind, src, dst, dev in cm.plan(csrc[s0:s0 + len(cm.srcs)], cland[l0:l0 + len(cm.lands)]):
                    if kind == "local":
                        out.append((kind, pltpu.make_async_copy(src, dst, send_sems.at[k])))
                    else:
                        out.append((kind, pltpu.make_async_remote_copy(
                            src_ref=src, dst_ref=dst, send_sem=send_sems.at[k], recv_sem=recv_sems.at[k],
                            device_id=dev, device_id_type=MESH)))
                    k += 1
                s0, l0 = s0 + len(cm.srcs), l0 + len(cm.lands)
            assert k == n_copies
            return out

        ids = [pl.program_id(a) for a in range(len(grid))]
        first = functools.reduce(jnp.logical_and, [i == 0 for i in ids])
        last = functools.reduce(jnp.logical_and, [i == g - 1 for i, g in zip(ids, grid)])

        @pl.when(first)
        def _():
            for _, cp in copies():
                cp.start()

        body(*pf, *ins, *outs, *scratch)

        @pl.when(last)
        def _():
            cps = copies()
            for kind, cp in cps:
                if kind == "remote":
                    cp.wait_recv()
            for kind, cp in cps:
                if kind == "remote":
                    cp.wait_send()
                else:
                    cp.wait()

    sems = [pltpu.SemaphoreType.DMA((n_copies,)), pltpu.SemaphoreType.DMA((n_copies,))] if comms else []
    if semantics is None or comms:
        semantics = ("arbitrary",) * len(grid)
    res = pl.pallas_call(
        wrapped,
        grid_spec=pltpu.PrefetchScalarGridSpec(
            num_scalar_prefetch=n_pf, grid=tuple(grid), in_specs=list(in_specs) + [any_spec] * n_src,
            out_specs=list(out_specs) + [any_spec] * n_land, scratch_shapes=list(scratch_shapes) + sems),
        out_shape=list(out_shape) + lands,
        input_output_aliases=aliases,
        compiler_params=_params(*semantics), name=name)(*prefetch, *operands, *srcs)
    return list(res[:n_out]), list(res[n_out:])


def _bdot(a, b, dims=NN):
    return lax.dot_general(a.astype(BF16), b.astype(BF16), dims, preferred_element_type=F32)


def _hdot(a, b, dims=NN):
    return lax.dot_general(a, b, dims, preferred_element_type=F32, precision=HI)


def _sigmoid(x):
    return 1.0 / (1.0 + jnp.exp(-x))


def _softplus(x):
    return jnp.maximum(x, 0.0) + jnp.log(1.0 + jnp.exp(-jnp.abs(x)))


def _matmul(a, b, *, mode, out_dtypes, name, n_cols=None, tm=1024, tn=512, tk=4096, epilogue=None, extras=(),
            b_spec=None, out_custom=None, a_rows=None, comms=()):
    if mode == "tn":
        K, M = a.shape
    else:
        M, K = a.shape
    r0 = 0
    if a_rows is not None:
        r0, M = a_rows
    N = n_cols if n_cols is not None else (b.shape[0] if mode == "nt" else b.shape[1])
    tm, tk, tn = _tile(M, tm), _tile(K, tk), _tile(N, tn)
    assert r0 % tm == 0
    i0 = r0 // tm
    if b_spec is None:
        b_spec = pl.BlockSpec((tn, tk), lambda i, j, k: (j, k)) if mode == "nt" else pl.BlockSpec((tk, tn), lambda i, j, k: (k, j))
    else:
        b_spec = b_spec(tk, tn)
    gm, gn, nk = M // tm, N // tn, K // tk
    if out_custom is None:
        out_shapes = [(M, N)] * len(out_dtypes)
        out_blocks = [(tm, tn)] * len(out_dtypes)
        out_index = [lambda i, j, k: (i, j)] * len(out_dtypes)
    else:
        shape, blk, ix = out_custom(tm, tn)
        out_shapes, out_blocks, out_index = [shape] * len(out_dtypes), [blk] * len(out_dtypes), [ix] * len(out_dtypes)
    a_spec = pl.BlockSpec((tk, tm), lambda i, j, k: (k, i)) if mode == "tn" else pl.BlockSpec((tm, tk), lambda i, j, k: (i + i0, k))
    hoist = mode == "tn" and nk == 1 and gn > 1
    dims = {"nn": NN, "nt": NT, "tn": TN}[mode]
    n_ex, n_out = len(extras), len(out_dtypes)

    def body(a_ref, b_ref, *rest):
        ex, outs = rest[:n_ex], rest[n_ex:n_ex + n_out]

        def finish(acc):
            res = epilogue(acc, *[e[...] for e in ex]) if epilogue is not None else (acc,)
            for o, r in zip(outs, res):
                o[...] = r.reshape(o.shape).astype(o.dtype)

        bb = b_ref[...]
        bb = bb.reshape(bb.shape[-2:])
        if hoist:
            at_ref = rest[-1]

            @pl.when(pl.program_id(1) == 0)
            def _():
                at_ref[...] = a_ref[...].T

            finish(lax.dot_general(at_ref[...], bb, NN, preferred_element_type=F32))
            return
        part = lax.dot_general(a_ref[...], bb, dims, preferred_element_type=F32)
        if nk == 1:
            finish(part)
        else:
            acc = rest[-1]
            k = pl.program_id(2)

            @pl.when(k == 0)
            def _():
                acc[...] = part

            @pl.when(k > 0)
            def _():
                acc[...] += part

            @pl.when(k == nk - 1)
            def _():
                finish(acc[...])

    scratch = [pltpu.VMEM((tm, tk), BF16)] if hoist else ([pltpu.VMEM((tm, tn), F32)] if nk > 1 else [])
    outs, lands = _pcall(
        body, grid=(gm, gn, nk),
        in_specs=[a_spec, b_spec] + [pl.BlockSpec((tm, tn), lambda i, j, k: (i, j)) for _ in extras],
        out_specs=[pl.BlockSpec(blk, ix) for blk, ix in zip(out_blocks, out_index)],
        out_shape=[jax.ShapeDtypeStruct(s, d) for s, d in zip(out_shapes, out_dtypes)],
        scratch_shapes=scratch, semantics=("parallel", "arbitrary", "arbitrary"),
        operands=[a, b, *extras], name=name, comms=comms)
    return (outs, lands) if comms else outs


TR = 128


def _rms(x):
    return lax.rsqrt(jnp.mean(x * x, axis=-1, keepdims=True) + NORM_EPS)


def _rms_bwd(x, r, w, dy):
    u = dy * w
    dx = r * u - x * (r * r * r) * jnp.mean(x * u, axis=-1, keepdims=True)
    return dx, dy * x * r


def _row_call(body, ins, row_flags, outs, name, n_rows, comms=()):
    tr = min(TR, n_rows)
    in_specs = []
    for arr, is_row in zip(ins, row_flags):
        if is_row:
            in_specs.append(pl.BlockSpec((tr, arr.shape[1]), lambda i: (i, 0)))
        else:
            in_specs.append(pl.BlockSpec(arr.shape, lambda i: (0, 0)))
    out_specs, out_shape = [], []
    for shape, dtype, kind in outs:
        if kind == "row":
            out_specs.append(pl.BlockSpec((tr, shape[1]), lambda i: (i, 0)))
        else:
            out_specs.append(pl.BlockSpec(shape, lambda i: (0, 0)))
        out_shape.append(jax.ShapeDtypeStruct(shape, dtype))
    res, lands = _pcall(body, grid=(n_rows // tr,), in_specs=in_specs, out_specs=out_specs, out_shape=out_shape,
                        operands=list(ins), name=name, comms=comms)
    return (res, lands) if comms else res


def _acc_out(ref, val):
    @pl.when(pl.program_id(0) == 0)
    def _():
        ref[...] = val

    @pl.when(pl.program_id(0) > 0)
    def _():
        ref[...] += val


def _rms_fwd(x, g):
    T, D = x.shape

    def body(x_ref, g_ref, o_ref):
        xv = x_ref[...]
        o_ref[...] = (xv * _rms(xv) * g_ref[...]).astype(BF16)

    return _row_call(body, [x, g], [True, False], [((T, D), BF16, "row")], "rms_fwd", T)[0]


def _post1(x, mix, g2, g3):
    T, D = x.shape

    def body(x_ref, mix_ref, g2_ref, g3_ref, h_ref, xn2_ref):
        mv = mix_ref[...]
        h = x_ref[...] + mv * _rms(mv) * g2_ref[...]
        h_ref[...] = h
        xn2_ref[...] = (h * _rms(h) * g3_ref[...]).astype(BF16)

    return _row_call(body, [x, mix, g2, g3], [True, True, False, False],
                     [((T, D), F32, "row"), ((T, D), BF16, "row")], "post1", T)


def _post2_loss(h, ff, g4, target):
    T, D = h.shape

    def body(h_ref, ff_ref, g4_ref, t_ref, dff_ref, dy_ref, dg4_ref, loss_ref):
        fv = ff_ref[...]
        r = _rms(fv)
        err = h_ref[...] + fv * r * g4_ref[...] - t_ref[...]
        dy = err * (1.0 / D)
        dy_ref[...] = dy
        dff, dwt = _rms_bwd(fv, r, g4_ref[...], dy)
        dff_ref[...] = dff.astype(BF16)
        _acc_out(dg4_ref, jnp.sum(dwt, axis=0, keepdims=True))
        part = 0.5 * jnp.sum(jnp.mean(err * err, axis=-1, keepdims=True), axis=0, keepdims=True)
        _acc_out(loss_ref, jnp.broadcast_to(part, (1, LANE)))

    return _row_call(body, [h, ff, g4, target], [True, True, False, True],
                     [((T, D), BF16, "row"), ((T, D), F32, "row"), ((1, D), F32, "acc"), ((1, LANE), F32, "acc")],
                     "post2_loss", T)


def _mid_bwd(h, mix, dy, dxn2, g2, g3, comms=()):
    T, D = h.shape

    def body(h_ref, mix_ref, dy_ref, dxn2_ref, g2_ref, g3_ref, dmix_ref, dh_ref, dg2_ref, dg3_ref):
        hv = h_ref[...]
        d1, dw3 = _rms_bwd(hv, _rms(hv), g3_ref[...], dxn2_ref[...])
        dh = dy_ref[...] + d1
        dh_ref[...] = dh
        mv = mix_ref[...]
        dmix, dw2 = _rms_bwd(mv, _rms(mv), g2_ref[...], dh)
        dmix_ref[...] = dmix.astype(BF16)
        _acc_out(dg2_ref, jnp.sum(dw2, axis=0, keepdims=True))
        _acc_out(dg3_ref, jnp.sum(dw3, axis=0, keepdims=True))

    return _row_call(body, [h, mix, dy, dxn2, g2, g3], [True, True, True, True, False, False],
                     [((T, D), BF16, "row"), ((T, D), F32, "row"), ((1, D), F32, "acc"), ((1, D), F32, "acc")],
                     "mid_bwd", T, comms=comms)


def _pre_bwd(x, dh, dxn, g1):
    T, D = x.shape

    def body(x_ref, dh_ref, dxn_ref, g1_ref, gx_ref, dg1_ref):
        xv = x_ref[...]
        d1, dw1 = _rms_bwd(xv, _rms(xv), g1_ref[...], dxn_ref[...])
        gx_ref[...] = dh_ref[...] + d1
        _acc_out(dg1_ref, jnp.sum(dw1, axis=0, keepdims=True))

    return _row_call(body, [x, dh, dxn, g1], [True, True, True, False],
                     [((T, D), F32, "row"), ((1, D), F32, "acc")], "pre_bwd", T)


def _shift_down(x, s):
    if s == 0:
        return x
    row = lax.broadcasted_iota(jnp.int32, x.shape, 0)
    return jnp.where(row >= s, pltpu.roll(x, s, axis=0), 0.0)


def _shift_up(x, s):
    if s == 0:
        return x
    n = x.shape[0]
    row = lax.broadcasted_iota(jnp.int32, x.shape, 0)
    return jnp.where(row < n - s, pltpu.roll(x, n - s, axis=0), 0.0)


def _conv(x, w):
    kw = w.shape[0]
    out = w[kw - 1:kw, :] * x
    for j in range(kw - 1):
        out = out + w[j:j + 1, :] * _shift_down(x, kw - 1 - j)
    return out


def _conv_bwd(x, w, dout):
    kw = w.shape[0]
    dx = w[kw - 1:kw, :] * dout
    dws = []
    for j in range(kw - 1):
        dx = dx + w[j:j + 1, :] * _shift_up(dout, kw - 1 - j)
        dws.append(jnp.sum(dout * _shift_down(x, kw - 1 - j), axis=0, keepdims=True))
    dws.append(jnp.sum(dout * x, axis=0, keepdims=True))
    return dx, jnp.concatenate(dws, axis=0)


def _qkvconv_fwd(proj, w, bsz, seq, gw, comms=()):
    nct = gw // LANE
    kw = w.shape[0]

    def body(p_ref, w_ref, o_ref):
        cv = _conv(p_ref[...], w_ref[...])
        o_ref[...] = (cv * _sigmoid(cv)).reshape(o_ref.shape)

    res, lands = _pcall(
        body, grid=(3, bsz, nct),
        in_specs=[pl.BlockSpec((seq, LANE), lambda p, b, c: (b, p * nct + c)),
                  pl.BlockSpec((kw, LANE), lambda p, b, c: (0, p * nct + c))],
        out_specs=[pl.BlockSpec((1, seq, LANE), lambda p, b, c: (p, b, c))],
        out_shape=[jax.ShapeDtypeStruct((3, bsz * seq, gw), F32)],
        semantics=("parallel", "parallel", "parallel"), operands=[proj, w], name="qkvconv_fwd", comms=comms)
    return res[0], lands


def _qkvconv_bwd(proj, w, dact, bsz, seq, gw, comms=()):
    nct = gw // LANE
    kw = w.shape[0]

    def body(p_ref, w_ref, d_ref, dp_ref, dw_ref):
        pre = p_ref[...]
        wv = w_ref[...]
        cv = _conv(pre, wv)
        sg = _sigmoid(cv)
        dcv = d_ref[...].reshape(cv.shape) * (sg * (1.0 + cv * (1.0 - sg)))
        dpre, dw = _conv_bwd(pre, wv, dcv)
        dp_ref[...] = dpre.astype(BF16)
        b = pl.program_id(2)

        @pl.when(b == 0)
        def _():
            dw_ref[...] = dw

        @pl.when(b > 0)
        def _():
            dw_ref[...] += dw

    res, lands = _pcall(
        body, grid=(3, nct, bsz),
        in_specs=[pl.BlockSpec((seq, LANE), lambda p, c, b: (b, p * nct + c)),
                  pl.BlockSpec((kw, LANE), lambda p, c, b: (0, p * nct + c)),
                  pl.BlockSpec((1, seq, LANE), lambda p, c, b: (p, b, c))],
        out_specs=[pl.BlockSpec((seq, LANE), lambda p, c, b: (b, p * nct + c)),
                   pl.BlockSpec((kw, LANE), lambda p, c, b: (0, p * nct + c))],
        out_shape=[jax.ShapeDtypeStruct((bsz * seq, 3 * gw), BF16), jax.ShapeDtypeStruct((kw, 3 * gw), F32)],
        semantics=("parallel", "parallel", "arbitrary"), operands=[proj, w, dact], name="qkvconv_bwd", comms=comms)
    return res, lands


def _sc_fwd(proj, w, bsz, seq, scw, col0):
    nct = scw // LANE
    c0 = col0 // LANE
    kw = w.shape[0]

    def body(b_ref, c_ref, h_ref, w_ref, o_ref):
        o_ref[...] = (b_ref[...] * _conv(c_ref[...] * h_ref[...], w_ref[...])).astype(BF16)

    return pl.pallas_call(
        body, grid=(bsz, nct),
        in_specs=[pl.BlockSpec((seq, LANE), lambda b, c: (b, c0 + c)),
                  pl.BlockSpec((seq, LANE), lambda b, c: (b, c0 + nct + c)),
                  pl.BlockSpec((seq, LANE), lambda b, c: (b, c0 + 2 * nct + c)),
                  pl.BlockSpec((kw, LANE), lambda b, c: (0, c))],
        out_specs=pl.BlockSpec((seq, LANE), lambda b, c: (b, c)),
        out_shape=jax.ShapeDtypeStruct((bsz * seq, scw), BF16),
        compiler_params=_params("parallel", "parallel"), name="sc_fwd")(proj, proj, proj, w)


def _sc_bwd(proj, w, dout, bsz, seq, scw, col0, dcol0, comms=()):
    nct = scw // LANE
    c0 = col0 // LANE
    d0 = dcol0 // LANE
    kw = w.shape[0]

    def body(b_ref, c_ref, h_ref, w_ref, d_ref, db_ref, dc_ref, dh_ref, dw_ref):
        cc, hh, wv, dv = c_ref[...], h_ref[...], w_ref[...], d_ref[...]
        m = cc * hh
        db_ref[...] = (dv * _conv(m, wv)).astype(BF16)
        dm, dw = _conv_bwd(m, wv, dv * b_ref[...])
        dc_ref[...] = (dm * hh).astype(BF16)
        dh_ref[...] = (dm * cc).astype(BF16)
        b = pl.program_id(1)

        @pl.when(b == 0)
        def _():
            dw_ref[...] = dw

        @pl.when(b > 0)
        def _():
            dw_ref[...] += dw

    res, lands = _pcall(
        body, grid=(nct, bsz),
        in_specs=[pl.BlockSpec((seq, LANE), lambda c, b: (b, c0 + c)),
                  pl.BlockSpec((seq, LANE), lambda c, b: (b, c0 + nct + c)),
                  pl.BlockSpec((seq, LANE), lambda c, b: (b, c0 + 2 * nct + c)),
                  pl.BlockSpec((kw, LANE), lambda c, b: (0, c)),
                  pl.BlockSpec((seq, LANE), lambda c, b: (b, d0 + c))],
        out_specs=[pl.BlockSpec((seq, LANE), lambda c, b: (b, c)),
                   pl.BlockSpec((seq, LANE), lambda c, b: (b, c)),
                   pl.BlockSpec((seq, LANE), lambda c, b: (b, c)),
                   pl.BlockSpec((kw, LANE), lambda c, b: (0, c))],
        out_shape=[jax.ShapeDtypeStruct((bsz * seq, scw), BF16)] * 3 + [jax.ShapeDtypeStruct((kw, scw), F32)],
        semantics=("parallel", "arbitrary"), operands=[proj, proj, proj, w, dout], name="sc_bwd", comms=comms)
    return res, lands


HEADS_PER_STEP = 4


def _colsel(tile, idx):
    lane = lax.broadcasted_iota(jnp.int32, tile.shape, 1)
    return jnp.sum(jnp.where(lane == idx, tile, 0.0), axis=1, keepdims=True)


def _rowsel(tile, idx):
    row = lax.broadcasted_iota(jnp.int32, tile.shape, 0)
    return jnp.sum(jnp.where(row == idx, tile, 0.0), axis=0, keepdims=True)


def _colput(col, idx, width=LANE):
    lane = lax.broadcasted_iota(jnp.int32, (col.shape[0], width), 1)
    return jnp.where(lane == idx, col, 0.0)


def _tri_masks(c):
    row = lax.broadcasted_iota(jnp.int32, (c, c), 0)
    col = lax.broadcasted_iota(jnp.int32, (c, c), 1)
    return row >= col, row > col, row == col


def _unit_lower_inverse(m):
    c = m.shape[0]
    _, _, eye = _tri_masks(c)
    p = -m
    t = jnp.where(eye, 1.0, 0.0) + p
    for _ in range(int(math.log2(c)) - 1):
        p = _hdot(p, p)
        t = t + _hdot(t, p)
    return t


def _gates(ab, alog, dtb):
    g = -jnp.exp(alog) * _softplus(ab + dtb)
    return g, _sigmoid(ab)


def _l2n(x):
    r = lax.rsqrt(jnp.sum(x * x, axis=-1, keepdims=True) + L2_EPS)
    return x * r, r


def _gdn_chunk_common(q, k, gc, gr, bc):
    c, dk = q.shape
    incl, strict, _ = _tri_masks(c)
    qh, rq = _l2n(q)
    kn, rk = _l2n(k)
    qn = qh * (dk ** -0.5)
    dm = jnp.where(incl, jnp.exp(jnp.where(incl, gc - gr, 0.0)), 0.0)
    kk = _bdot(kn, kn, NT)
    m = jnp.where(strict, bc * kk * dm, 0.0)
    pm = jnp.where(incl, _bdot(qn, kn, NT) * dm, 0.0)
    return qh, rq, kn, rk, qn, dm, kk, m, pm


def _gdn_fwd(qkv, proj, alog, dtb, gnw, bsz, seq, heads, z_col0, ab_col0, comms=()):
    c = CHUNK
    nch = seq // c
    hb = min(HEADS_PER_STEP, heads)
    ng = heads // hb
    hd = qkv.shape[2] // heads
    wb = hb * hd

    def body(qkv_ref, z_ref, ab_ref, alog_ref, dtb_ref, gnw_ref, o_ref, ssave_ref, tsave_ref, s_ref, gc_ref, gt_ref, be_ref):
        n, hg = pl.program_id(1), pl.program_id(2)

        @pl.when((n == 0) & (hg == 0))
        def _():
            s_ref[...] = jnp.zeros_like(s_ref)

        @pl.when(hg == 0)
        def _():
            g, beta = _gates(ab_ref[...], alog_ref[...], dtb_ref[...])
            incl, _, _ = _tri_masks(c)
            gcum = _hdot(jnp.where(incl, 1.0, 0.0), g)
            gc_ref[...] = gcum
            gt_ref[...] = gcum.T
            be_ref[...] = beta

        gc_t, gt_t, be_t, gnw_v = gc_ref[...], gt_ref[...], be_ref[...], gnw_ref[...]
        states = [s_ref[hg * hb + hh] for hh in range(hb)]
        done = []
        for hh in range(hb):
            h = hg * hb + hh
            sl = slice(hh * hd, (hh + 1) * hd)
            gc, gr, bc = _colsel(gc_t, h), _rowsel(gt_t, h), _colsel(be_t, heads + h)
            q, k, v = qkv_ref[0, :, sl], qkv_ref[1, :, sl], qkv_ref[2, :, sl]
            _, _, kn, _, qn, _, _, m, pm = _gdn_chunk_common(q, k, gc, gr, bc)
            tm = _unit_lower_inverse(m)
            s = states[hh]
            gam = jnp.exp(gc)
            glast = gc[c - 1:c, :]
            r = v - gam * _bdot(kn, s)
            vn = _bdot(tm, bc * r)
            o = gam * _bdot(qn, s) + _bdot(pm, vn)
            s_new = s * jnp.exp(glast) + _bdot(kn * jnp.exp(glast - gc), vn, TN)
            on = o * lax.rsqrt(jnp.mean(o * o, axis=-1, keepdims=True) + NORM_EPS) * gnw_v
            zz = z_ref[:, sl]
            done.append((tm, s_new, (on * (zz * _sigmoid(zz))).astype(BF16)))
        for hh, (tm, s_new, out) in enumerate(done):
            ssave_ref[0, 0, hh] = states[hh]
            tsave_ref[0, 0, hh] = tm
            s_ref[hg * hb + hh] = s_new
            o_ref[:, hh * hd:(hh + 1) * hd] = out

    row = lambda b, n, g: b * nch + n
    return _pcall(
        body, grid=(bsz, nch, ng),
        in_specs=[pl.BlockSpec((3, c, wb), lambda b, n, g: (0, row(b, n, g), g)),
                  pl.BlockSpec((c, wb), lambda b, n, g: (row(b, n, g), z_col0 // wb + g)),
                  pl.BlockSpec((c, LANE), lambda b, n, g: (row(b, n, g), ab_col0 // LANE)),
                  pl.BlockSpec((1, LANE), lambda b, n, g: (0, 0)),
                  pl.BlockSpec((1, LANE), lambda b, n, g: (0, 0)),
                  pl.BlockSpec((1, hd), lambda b, n, g: (0, 0))],
        out_specs=[pl.BlockSpec((c, wb), lambda b, n, g: (row(b, n, g), g)),
                   pl.BlockSpec((1, 1, hb, hd, hd), lambda b, n, g: (b, n, g, 0, 0)),
                   pl.BlockSpec((1, 1, hb, c, c), lambda b, n, g: (b, n, g, 0, 0))],
        out_shape=[jax.ShapeDtypeStruct((bsz * seq, heads * hd), BF16),
                   jax.ShapeDtypeStruct((bsz, nch, heads, hd, hd), F32),
                   jax.ShapeDtypeStruct((bsz, nch, heads, c, c), F32)],
        scratch_shapes=[pltpu.VMEM((heads, hd, hd), F32), pltpu.VMEM((c, LANE), F32), pltpu.VMEM((LANE, c), F32),
                        pltpu.VMEM((c, LANE), F32)],
        semantics=("parallel", "arbitrary", "arbitrary"), operands=[qkv, proj, proj, alog, dtb, gnw], name="gdn_fwd",
        comms=comms)


def _gdn_bwd(qkv, proj, alog, dtb, gnw, ssave, tsave, dout, bsz, seq, heads, z_col0, ab_col0, comms=()):
    c = CHUNK
    nch = seq // c
    hb = min(HEADS_PER_STEP, heads)
    ng = heads // hb
    hd = qkv.shape[2] // heads
    wb = hb * hd

    def body(qkv_ref, z_ref, ab_ref, alog_ref, dtb_ref, gnw_ref, ssave_ref, tsave_ref, do_ref,
             dact_ref, dz_ref, dab_ref, dalog_ref, ddtb_ref, dgnw_ref,
             ds_ref, gc_ref, gt_ref, be_ref, dgacc_ref, dbacc_ref):
        n, hg = pl.program_id(1), pl.program_id(2)
        incl, strict, _ = _tri_masks(c)

        @pl.when((n == 0) & (hg == 0))
        def _():
            ds_ref[...] = jnp.zeros_like(ds_ref)
            dalog_ref[...] = jnp.zeros_like(dalog_ref)
            ddtb_ref[...] = jnp.zeros_like(ddtb_ref)
            dgnw_ref[...] = jnp.zeros_like(dgnw_ref)

        @pl.when(hg == 0)
        def _():
            g, beta = _gates(ab_ref[...], alog_ref[...], dtb_ref[...])
            gcum = _hdot(jnp.where(incl, 1.0, 0.0), g)
            gc_ref[...] = gcum
            gt_ref[...] = gcum.T
            be_ref[...] = beta
            dgacc_ref[...] = jnp.zeros_like(dgacc_ref)
            dbacc_ref[...] = jnp.zeros_like(dbacc_ref)

        gc_t, gt_t, be_t, gnw_v = gc_ref[...], gt_ref[...], be_ref[...], gnw_ref[...]
        ds_all = [ds_ref[hg * hb + hh] for hh in range(hb)]
        dgnw = jnp.zeros((1, hd), F32)
        dg_tile = jnp.zeros((c, LANE), F32)
        db_tile = jnp.zeros((c, LANE), F32)
        done = []
        for hh in range(hb):
            h = hg * hb + hh
            sl = slice(hh * hd, (hh + 1) * hd)
            gc, gr, bc = _colsel(gc_t, h), _rowsel(gt_t, h), _colsel(be_t, heads + h)
            q, k, v = qkv_ref[0, :, sl], qkv_ref[1, :, sl], qkv_ref[2, :, sl]
            qh, rq, kn, rk, qn, dm, kk, m, pm = _gdn_chunk_common(q, k, gc, gr, bc)
            tm = tsave_ref[0, 0, hh]
            s = ssave_ref[0, 0, hh]
            gam = jnp.exp(gc)
            glast = gc[c - 1:c, :]
            gl = jnp.exp(glast)
            ratio = jnp.exp(glast - gc)
            ks = _bdot(kn, s)
            r = v - gam * ks
            vn = _bdot(tm, bc * r)
            qs = _bdot(qn, s)
            o = gam * qs + _bdot(pm, vn)
            ro = lax.rsqrt(jnp.mean(o * o, axis=-1, keepdims=True) + NORM_EPS)
            zz = z_ref[:, sl]
            sz = _sigmoid(zz)
            dd = do_ref[:, sl]
            don = dd * (zz * sz)
            dz_h = (dd * (o * ro * gnw_v) * (sz * (1.0 + zz * (1.0 - sz)))).astype(BF16)
            dgnw = dgnw + jnp.sum(don * o * ro, axis=0, keepdims=True)
            uu = don * gnw_v
            d_o = ro * uu - o * (ro * ro * ro) * jnp.mean(o * uu, axis=-1, keepdims=True)
            dqs = gam * d_o
            dq = _bdot(dqs, s, NT)
            dgam = jnp.sum(d_o * qs, axis=-1, keepdims=True)
            ds_new = _bdot(qn, dqs, TN)
            dp = jnp.where(incl, _bdot(d_o, vn, NT), 0.0)
            dvn = _bdot(pm, d_o, TN)
            ds_in = ds_all[hh]
            ds_new = ds_new + gl * ds_in
            dgl = jnp.sum(jnp.sum(ds_in * s, axis=1, keepdims=True), axis=0, keepdims=True)
            dkd = _bdot(vn, ds_in, NT)
            dvn = dvn + _bdot(kn * ratio, ds_in)
            dk = ratio * dkd
            dratio = jnp.sum(dkd * kn, axis=-1, keepdims=True)
            dpd = dp * dm
            dq = dq + _bdot(dpd, kn)
            dk = dk + _bdot(dpd, qn, TN)
            ff = dp * pm
            dx = _bdot(tm, dvn, TN)
            dr = bc * dx
            dbeta = jnp.sum(dx * r, axis=-1, keepdims=True)
            gdr = gam * dr
            dk = dk - _bdot(gdr, s, NT)
            dgam = dgam - jnp.sum(dr * ks, axis=-1, keepdims=True)
            ds_new = ds_new - _bdot(kn, gdr, TN)
            dmm = jnp.where(strict, -_bdot(dx, vn, NT), 0.0)
            ee = dmm * dm
            dbeta = dbeta + jnp.sum(ee * kk, axis=-1, keepdims=True)
            be_e = bc * ee
            dk = dk + _bdot(be_e, kn) + _bdot(be_e, kn, TN)
            ff = ff + dmm * m
            dgc = (jnp.sum(ff, axis=-1, keepdims=True) - jnp.sum(ff.T, axis=-1, keepdims=True)
                   + dgam * gam - dratio * ratio)
            dglast = jnp.sum(dratio * ratio, axis=0, keepdims=True) + dgl * gl
            rowi = lax.broadcasted_iota(jnp.int32, (c, 1), 0)
            dgc = dgc + jnp.where(rowi == c - 1, dglast, 0.0)
            dg_tile = dg_tile + _colput(dgc, h)
            db_tile = db_tile + _colput(dbeta, heads + h)
            dqh = dq * (hd ** -0.5)
            done.append((ds_new, dz_h, rq * (dqh - qh * jnp.sum(qh * dqh, axis=-1, keepdims=True)),
                         rk * (dk - kn * jnp.sum(kn * dk, axis=-1, keepdims=True)), dr))
        for hh, (ds_new, dz_h, dq_h, dk_h, dv_h) in enumerate(done):
            sl = slice(hh * hd, (hh + 1) * hd)
            ds_ref[hg * hb + hh] = ds_new
            dz_ref[:, sl] = dz_h
            dact_ref[0, :, sl] = dq_h
            dact_ref[1, :, sl] = dk_h
            dact_ref[2, :, sl] = dv_h
        dgacc_ref[...] += dg_tile
        dbacc_ref[...] += db_tile
        dgnw_ref[0] += dgnw

        @pl.when(hg == ng - 1)
        def _():
            ab = ab_ref[...]
            ea = jnp.exp(alog_ref[...])
            g = -ea * _softplus(ab + dtb_ref[...])
            beta = be_ref[...]
            dg = _hdot(jnp.where(incl, 1.0, 0.0), dgacc_ref[...], TN)
            lane = lax.broadcasted_iota(jnp.int32, ab.shape, 1)
            da = jnp.where(lane < heads, dg * (-ea) * _sigmoid(ab + dtb_ref[...]), 0.0)
            db = dbacc_ref[...] * beta * (1.0 - beta)
            dab_ref[...] = (da + db).astype(BF16)
            dalog_ref[0] += jnp.sum(jnp.where(lane < heads, dg * g, 0.0), axis=0, keepdims=True)
            ddtb_ref[0] += jnp.sum(da, axis=0, keepdims=True)

    row = lambda b, n, g: b * nch + (nch - 1 - n)
    rev = lambda n: nch - 1 - n
    return _pcall(
        body, grid=(bsz, nch, ng),
        in_specs=[pl.BlockSpec((3, c, wb), lambda b, n, g: (0, row(b, n, g), g)),
                  pl.BlockSpec((c, wb), lambda b, n, g: (row(b, n, g), z_col0 // wb + g)),
                  pl.BlockSpec((c, LANE), lambda b, n, g: (row(b, n, g), ab_col0 // LANE)),
                  pl.BlockSpec((1, LANE), lambda b, n, g: (0, 0)),
                  pl.BlockSpec((1, LANE), lambda b, n, g: (0, 0)),
                  pl.BlockSpec((1, hd), lambda b, n, g: (0, 0)),
                  pl.BlockSpec((1, 1, hb, hd, hd), lambda b, n, g: (b, rev(n), g, 0, 0)),
                  pl.BlockSpec((1, 1, hb, c, c), lambda b, n, g: (b, rev(n), g, 0, 0)),
                  pl.BlockSpec((c, wb), lambda b, n, g: (row(b, n, g), g))],
        out_specs=[pl.BlockSpec((3, c, wb), lambda b, n, g: (0, row(b, n, g), g)),
                   pl.BlockSpec((c, wb), lambda b, n, g: (row(b, n, g), g)),
                   pl.BlockSpec((c, LANE), lambda b, n, g: (row(b, n, g), 0)),
                   pl.BlockSpec((1, 1, LANE), lambda b, n, g: (b, 0, 0)),
                   pl.BlockSpec((1, 1, LANE), lambda b, n, g: (b, 0, 0)),
                   pl.BlockSpec((1, 1, hd), lambda b, n, g: (b, 0, 0))],
        out_shape=[jax.ShapeDtypeStruct((3, bsz * seq, heads * hd), F32),
                   jax.ShapeDtypeStruct((bsz * seq, heads * hd), BF16),
                   jax.ShapeDtypeStruct((bsz * seq, LANE), BF16),
                   jax.ShapeDtypeStruct((bsz, 1, LANE), F32),
                   jax.ShapeDtypeStruct((bsz, 1, LANE), F32),
                   jax.ShapeDtypeStruct((bsz, 1, hd), F32)],
        scratch_shapes=[pltpu.VMEM((heads, hd, hd), F32), pltpu.VMEM((c, LANE), F32), pltpu.VMEM((LANE, c), F32),
                        pltpu.VMEM((c, LANE), F32), pltpu.VMEM((c, LANE), F32), pltpu.VMEM((c, LANE), F32)],
        semantics=("parallel", "arbitrary", "arbitrary"),
        operands=[qkv, proj, proj, alog, dtb, gnw, ssave, tsave, dout], name="gdn_bwd", comms=comms)


ELEMWISE_BLOCK_ELEMS = 256 * 1024


def _rows_tile(rows, cols):
    want = max(16, ELEMWISE_BLOCK_ELEMS // cols)
    if rows <= want:
        return rows
    t = (want // 16) * 16
    while t > 16 and rows % t:
        t -= 16
    return t if rows % t == 0 else rows


def _piece_specs(pieces, tr, cols):
    specs, leads = [], []
    for p, (arr, lead) in enumerate(pieces):
        if arr.ndim == 3:
            specs.append(pl.BlockSpec((1, tr, cols), functools.partial(lambda i, idx, p: (idx[p], i, 0), p=p)))
        else:
            specs.append(pl.BlockSpec((tr, cols), lambda i, idx: (i, 0)))
        leads.append(jnp.asarray(0 if lead is None else lead, jnp.int32))
    return jnp.stack(leads), specs


def _sum_pieces(refs):
    total = None
    for r in refs:
        v = r[...].astype(F32)
        v = v.reshape(v.shape[-2:])
        total = v if total is None else total + v
    return total


def _adamw(w, m, v, pieces, name, comms=()):
    rows, cols = w.shape
    tr = _rows_tile(rows, cols)
    leads, pspecs = _piece_specs(pieces, tr, cols)
    npc = len(pieces)
    c1 = 1.0 - ADAM_B1 ** ADAM_STEP
    c2 = 1.0 - ADAM_B2 ** ADAM_STEP

    def body(idx_ref, w_ref, m_ref, v_ref, *rest):
        g = _sum_pieces(rest[:npc])
        g_ref, d_ref, nm_ref, nv_ref = rest[npc:]
        nm = ADAM_B1 * m_ref[...] + (1.0 - ADAM_B1) * g
        nv = ADAM_B2 * v_ref[...] + (1.0 - ADAM_B2) * (g * g)
        g_ref[...] = g
        nm_ref[...] = nm
        nv_ref[...] = nv
        d_ref[...] = -ADAM_LR * ((nm / c1) / (jnp.sqrt(nv / c2) + ADAM_EPS) + ADAM_WD * w_ref[...])

    wspec = pl.BlockSpec((tr, cols), lambda i, idx: (i, 0))
    res, lands = _pcall(body, grid=(rows // tr,), in_specs=[wspec] * 3 + pspecs, out_specs=[wspec] * 4,
                        out_shape=[jax.ShapeDtypeStruct((rows, cols), F32)] * 4, semantics=("parallel",),
                        prefetch=[leads], operands=[w, m, v, *[p for p, _ in pieces]], name=name, comms=comms)
    return (res, lands) if comms else res


def _sum_to(pieces, out_dtype, name):
    arr0 = pieces[0][0]
    rows, cols = arr0.shape[-2:]
    tr = _rows_tile(rows, cols)
    leads, pspecs = _piece_specs(pieces, tr, cols)

    def body(idx_ref, *rest):
        rest[-1][...] = _sum_pieces(rest[:-1]).astype(out_dtype)

    return pl.pallas_call(
        body,
        grid_spec=pltpu.PrefetchScalarGridSpec(num_scalar_prefetch=1, grid=(rows // tr,), in_specs=pspecs,
                                               out_specs=pl.BlockSpec((tr, cols), lambda i, idx: (i, 0))),
        out_shape=jax.ShapeDtypeStruct((rows, cols), out_dtype),
        compiler_params=_params("parallel"), name=name)(leads, *[p for p, _ in pieces])


def _pair_add(a, recv, my_c, name, comms=()):
    _, rows, cols = a.shape
    tr = _rows_tile(rows, cols)

    def body(c_ref, a_ref, r_ref, o_ref):
        o_ref[...] = (a_ref[...].astype(F32) + r_ref[...].astype(F32)).astype(BF16)

    res, lands = _pcall(
        body, grid=(4, rows // tr),
        in_specs=[pl.BlockSpec((1, tr, cols), lambda j, i, c: (2 * j + c[0], i, 0)),
                  pl.BlockSpec((1, tr, cols), lambda j, i, c: (j, i, 0))],
        out_specs=[pl.BlockSpec((1, tr, cols), lambda j, i, c: (j, i, 0))],
        out_shape=[jax.ShapeDtypeStruct((4, rows, cols), BF16)], semantics=("parallel", "parallel"),
        prefetch=[jnp.reshape(my_c, (1,)).astype(jnp.int32)], operands=[a, recv], name=name, comms=comms)
    return (res[0], lands) if comms else res[0]


def _assemble(frames, table, n_blocks, width, name, comms=()):
    _, rows, _ = frames.shape
    tr = _tile(rows, 1024)

    def body(t_ref, f1_ref, f2_ref, o_ref):
        jb = pl.program_id(0)
        v = f1_ref[0]
        o_ref[...] = jnp.where(t_ref[4, jb] > 0, v + f2_ref[0], v)

    res, lands = _pcall(
        body, grid=(n_blocks, rows // tr),
        in_specs=[pl.BlockSpec((1, tr, width), lambda jb, i, t: (t[0, jb], i, t[1, jb])),
                  pl.BlockSpec((1, tr, width), lambda jb, i, t: (t[2, jb], i, t[3, jb]))],
        out_specs=[pl.BlockSpec((tr, width), lambda jb, i, t: (i, jb))],
        out_shape=[jax.ShapeDtypeStruct((rows, n_blocks * width), BF16)], semantics=("parallel", "parallel"),
        prefetch=[table], operands=[frames, frames], name=name, comms=comms)
    return (res[0], lands) if comms else res[0]


ANY = pl.BlockSpec(memory_space=pl.ANY)


def _place():
    x, y, c = lax.axis_index("x"), lax.axis_index("y"), lax.axis_index("c")
    chips = [(1 - x, y), (x, 1 - y), (1 - x, 1 - y)]
    return x, y, c, chips


def _allgather_big(shards, name):
    n = len(shards)

    def body(*refs):
        xs, outs = refs[:n], refs[n:2 * n]
        send_sems, recv_sems, local_sems = refs[2 * n:]
        x, y, c, chips = _place()
        me, sibling = (x, y, c), (x, y, 1 - c)

        def copy(a, k, block, to, src=None):
            dst = outs[a].at[4 * block[0] + 2 * block[1] + block[2]]
            return pltpu.make_async_remote_copy(src_ref=dst if src is None else src, dst_ref=dst,
                                                send_sem=send_sems.at[a, k], recv_sem=recv_sems.at[a, k],
                                                device_id=to, device_id_type=MESH)

        mine = [pltpu.make_async_copy(xs[a], outs[a].at[4 * x + 2 * y + c], local_sems.at[a]) for a in range(n)]
        for cp in mine:
            cp.start()
        first = []
        for a in range(n):
            first.append(copy(a, 0, me, sibling, src=xs[a]))
            first += [copy(a, 1 + j, me, (*chip, c), src=xs[a]) for j, chip in enumerate(chips)]
        for cp in first:
            cp.start()
        passed = []
        for j, chip in enumerate(chips):
            for a in range(n):
                copy(a, 1 + j, (*chip, c), me).wait_recv()
                cp = copy(a, 4 + j, (*chip, c), sibling)
                cp.start()
                passed.append(cp)
        for a in range(n):
            copy(a, 0, sibling, me).wait_recv()
            for j, chip in enumerate(chips):
                copy(a, 4 + j, (*chip, 1 - c), me).wait_recv()
        for cp in first + passed:
            cp.wait_send()
        for cp in mine:
            cp.wait()

    return pl.pallas_call(
        body, in_specs=[ANY] * n, out_specs=[ANY] * n,
        out_shape=[jax.ShapeDtypeStruct((N_DEV,) + s.shape, s.dtype) for s in shards],
        scratch_shapes=[pltpu.SemaphoreType.DMA((n, 7)), pltpu.SemaphoreType.DMA((n, 7)), pltpu.SemaphoreType.DMA((n,))],
        name=name)(*shards)


def _allreduce_small(buf, name):
    rows = buf.shape[0]

    def body(x_ref, o_ref, g_ref, send_sems, recv_sems):
        x, y, c, chips = _place()
        me, sibling = (x, y, c), (x, y, 1 - c)

        def copy(k, block, to, src=None):
            dst = g_ref.at[4 * block[0] + 2 * block[1] + block[2]]
            return pltpu.make_async_remote_copy(src_ref=dst if src is None else src, dst_ref=dst,
                                                send_sem=send_sems.at[k], recv_sem=recv_sems.at[k],
                                                device_id=to, device_id_type=MESH)

        first = [copy(0, me, sibling, src=x_ref)]
        first += [copy(1 + j, me, (*chip, c), src=x_ref) for j, chip in enumerate(chips)]
        for cp in first:
            cp.start()
        passed = [copy(4 + j, (*chip, c), sibling) for j, chip in enumerate(chips)]
        for j, chip in enumerate(chips):
            copy(1 + j, (*chip, c), me).wait_recv()
            passed[j].start()
        copy(0, sibling, me).wait_recv()
        for j, chip in enumerate(chips):
            copy(4 + j, (*chip, 1 - c), me).wait_recv()
        for cp in first + passed:
            cp.wait_send()
        g_ref[4 * x + 2 * y + c] = x_ref[...]
        total = g_ref[0]
        for s in range(1, N_DEV):
            total = total + g_ref[s]
        o_ref[...] = total

    vm = pl.BlockSpec(memory_space=pltpu.VMEM)
    return pl.pallas_call(
        body, in_specs=[vm], out_specs=vm, out_shape=jax.ShapeDtypeStruct((rows, LANE), F32),
        scratch_shapes=[pltpu.VMEM((N_DEV, rows, LANE), F32), pltpu.SemaphoreType.DMA((7,)), pltpu.SemaphoreType.DMA((7,))],
        name=name)(buf)


def _ag_first(shard):
    def plan(srcs, lands):
        x, y, c, chips = _place()
        (s,), (g,) = srcs, lands
        dst = g.at[4 * x + 2 * y + c]
        return ([("local", s, dst, None), ("remote", s, dst, (x, y, 1 - c))]
                + [("remote", s, dst, (*chip, c)) for chip in chips])

    return _Comm([shard], [jax.ShapeDtypeStruct((N_DEV,) + shard.shape, shard.dtype)], plan, 5)


def _ag_second(g):
    def plan(srcs, lands):
        x, y, c, chips = _place()
        (buf,) = lands
        return [("remote", buf.at[4 * cx + 2 * cy + c], buf.at[4 * cx + 2 * cy + c], (x, y, 1 - c)) for cx, cy in chips]

    return _Comm([g], [jax.ShapeDtypeStruct(g.shape, g.dtype)], plan, 3, alias={0: 0})


def _rs_first(grad):
    def plan(srcs, lands):
        x, y, c, _ = _place()
        (a,), (land,) = srcs, lands
        return [("remote", a.at[2 * j + (1 - c)], land.at[j], (x, y, 1 - c)) for j in range(4)]

    return _Comm([grad], [jax.ShapeDtypeStruct((4,) + grad.shape[1:], grad.dtype)], plan, 4)


def _rs_second(pair):
    def plan(srcs, lands):
        x, y, c, chips = _place()
        (p,), (land,) = srcs, lands
        return [("remote", p.at[2 * cx + cy], land.at[j], (cx, cy, c)) for j, (cx, cy) in enumerate(chips)]

    return _Comm([pair], [jax.ShapeDtypeStruct((3,) + pair.shape[1:], pair.dtype)], plan, 3)


class _InLayout:
    def __init__(self, n_in, gw, heads, scw):
        self.n_in, self.split = n_in, 4 * gw + 2 * heads
        self.gap = LANE - 2 * heads
        self.ab_col, self.sc_col = 4 * gw, 4 * gw + LANE
        self.used = 4 * gw + LANE + 3 * scw
        p0 = [s * n_in + (self.gap if s * n_in >= self.split else 0) for s in range(N_DEV)]
        self.fstart = [(p // MXU) * MXU for p in p0]
        need = []
        for s in range(N_DEV):
            straddle = s * n_in < self.split < (s + 1) * n_in
            need.append(p0[s] - self.fstart[s] + n_in + (self.gap if straddle else 0))
        self.fw = -(-max(need) // MXU) * MXU
        self.wp = max(f + self.fw for f in self.fstart)
        assert self.wp >= self.used and self.wp % MXU == 0
        nfb = self.fw // MXU
        rows = []
        for jb in range(self.wp // MXU):
            src = [(s, jb - self.fstart[s] // MXU) for s in range(N_DEV) if 0 <= jb - self.fstart[s] // MXU < nfb]
            assert 1 <= len(src) <= 2, (jb, src)
            (s1, b1), (s2, b2) = src[0], src[-1]
            rows.append((s1, b1, s2, b2, int(len(src) == 2)))
        self.table = np.asarray(rows, np.int32).T.copy()

    def frame_block(self, s):
        p = s * self.n_in
        return (p + jnp.where(p >= self.split, self.gap, 0)) // MXU

    def offsets(self, s):
        p = s * self.n_in
        after = p >= self.split
        off1 = p + jnp.where(after, self.gap, 0) - self.frame_block(s) * MXU
        len1 = jnp.where(after, self.n_in, jnp.clip(self.split - p, 0, self.n_in))
        off2 = off1 + jnp.where(len1 < self.n_in, self.gap, 0)
        return off1, len1, off2

    def to_frame(self, w, s):
        off1, len1, off2 = self.offsets(s)
        zero = jnp.zeros((w.shape[0], self.fw), w.dtype)
        f1 = lax.dynamic_update_slice(zero, w, (0, off1))
        f2 = lax.dynamic_update_slice(zero, w, (0, off2))
        col = lax.broadcasted_iota(jnp.int32, (1, self.fw), 1)
        return jnp.where(col < off1 + len1, f1, jnp.where(col >= off2 + len1, f2, jnp.zeros_like(f2)))

    def from_frame(self, f, s):
        off1, len1, off2 = self.offsets(s)
        a = lax.dynamic_slice(f, (0, off1), (f.shape[0], self.n_in))
        b = lax.dynamic_slice(f, (0, off2), (f.shape[0], self.n_in))
        col = lax.broadcasted_iota(jnp.int32, (1, self.n_in), 1)
        return jnp.where(col < len1, a, b)


def _pack_rows(parts):
    rows = []
    for p in parts:
        flat = p.reshape(-1)
        pad = (-flat.shape[0]) % LANE
        rows.append(jnp.pad(flat, (0, pad)).reshape(-1, LANE))
    buf = jnp.concatenate(rows, axis=0)
    return jnp.pad(buf, ((0, (-buf.shape[0]) % 8), (0, 0)))


def _unpack_rows(buf, shapes):
    out, r = [], 0
    for shp in shapes:
        size = int(np.prod(shp))
        nr = -(-size // LANE)
        out.append(buf[r:r + nr].reshape(-1)[:size].reshape(shp))
        r += nr
    return out


def _pad_lanes(v):
    return jnp.pad(v, ((0, 0), (0, LANE - v.shape[1])))


def kernel(x, norm_mix_pre, w_in, conv_qkv_w, a_log, dt_bias, gdn_norm_w, conv_sc_w, w_out, norm_mix_post, norm_mlp_pre, w_up, w_down, norm_mlp_post, loss_target, m_norm_mix_pre, m_w_in, m_conv_qkv_w, m_a_log, m_dt_bias, m_gdn_norm_w, m_conv_sc_w, m_w_out, m_norm_mix_post, m_norm_mlp_pre, m_w_up, m_w_down, m_norm_mlp_post, v_norm_mix_pre, v_w_in, v_conv_qkv_w, v_a_log, v_dt_bias, v_gdn_norm_w, v_conv_sc_w, v_w_out, v_norm_mix_post, v_norm_mlp_pre, v_w_up, v_w_down, v_norm_mlp_post):
    bsz, seq, d = x.shape
    t = bsz * seq
    heads, hd = a_log.shape[-1], gdn_norm_w.shape[-1]
    gw = heads * hd
    scw = conv_sc_w.shape[-1] * N_DEV
    dff_w = w_up.shape[-1] * N_DEV
    lay = _InLayout(w_in.shape[-1], gw, heads, scw)
    mx, my, mc = lax.axis_index("x"), lax.axis_index("y"), lax.axis_index("c")
    me = 4 * mx + 2 * my + mc
    chip = 2 * mx + my

    x2 = x.reshape(t, d)
    tgt = loss_target.reshape(t, d)
    g1, g2, g3, g4 = norm_mix_pre, norm_mix_post, norm_mlp_pre, norm_mlp_post

    frame = lay.to_frame(w_in[0].astype(BF16), me)
    (g_in,) = _allgather_big([frame], "allgather_w_in")
    w_pad, (g_out,) = _assemble(g_in, jnp.asarray(lay.table), lay.wp // MXU, MXU, "assemble_w_in",
                                comms=[_ag_first(w_out[0].astype(BF16))])
    up_cols = dff_w // N_DEV
    kq, ks = conv_qkv_w.shape[1], conv_sc_w.shape[1]
    cq_n, cs_n = conv_qkv_w.shape[-1], conv_sc_w.shape[-1]
    cq_full = lax.dynamic_update_slice(jnp.zeros((kq, 3 * gw), F32), conv_qkv_w[0], (0, me * cq_n))
    cs_full = lax.dynamic_update_slice(jnp.zeros((ks, scw), F32), conv_sc_w[0], (0, me * cs_n))
    conv_q, conv_s = _unpack_rows(_allreduce_small(_pack_rows([cq_full, cs_full]), "allgather_conv"),
                                  [(kq, 3 * gw), (ks, scw)])
    alog_t, dtb_t = _pad_lanes(a_log), _pad_lanes(dt_bias)

    xn = _rms_fwd(x2, g1)
    (proj,), (g_up,) = _matmul(xn, w_pad, mode="nn", out_dtypes=[F32], name="in_proj", tn=768,
                               comms=[_ag_first(w_up[0].astype(BF16))])
    qkv, (g_out,) = _qkvconv_fwd(proj, conv_q, bsz, seq, gw, comms=[_ag_second(g_out)])
    (gdn_out, ssave, tsave), (g_up, g_down) = _gdn_fwd(
        qkv, proj, alog_t, dtb_t, gdn_norm_w, bsz, seq, heads, 3 * gw, lay.ab_col,
        comms=[_ag_second(g_up), _ag_first(w_down[0].astype(BF16))])
    sc_out = _sc_fwd(proj, conv_s, bsz, seq, scw, lay.sc_col)
    mixed = jnp.concatenate([gdn_out, sc_out], axis=1)
    w_out_f = g_out.reshape(d, d)
    (mix,), (g_down,) = _matmul(mixed, w_out_f, mode="nn", out_dtypes=[F32], name="out_proj", comms=[_ag_second(g_down)])
    w_down_f = g_down.reshape(dff_w, d)
    h, xn2 = _post1(x2, mix, g2, g3)

    def up_epilogue(acc):
        r = jnp.maximum(acc, 0.0)
        return r, r * r

    act, hid = _matmul(xn2, g_up, mode="nn", out_dtypes=[BF16, BF16], name="mlp_up", n_cols=dff_w, epilogue=up_epilogue,
                       b_spec=lambda tk, tn: pl.BlockSpec((1, tk, tn), lambda i, j, k: (j // (up_cols // tn), k, j % (up_cols // tn))))
    (ff,) = _matmul(hid, w_down_f, mode="nn", out_dtypes=[F32], name="mlp_down", tn=1024, tk=1024)
    dff, dy, dg4, loss_p = _post2_loss(h, ff, g4, tgt)

    def pieces(pair, got):
        return [(pair, chip), (got, 0), (got, 1), (got, 2)]

    (dpre,) = _matmul(dff, w_down_f, mode="nt", out_dtypes=[BF16], name="d_hidden", extras=[act],
                      epilogue=lambda acc, a: (acc * (2.0 * a.astype(F32)),))
    (dw_down,) = _matmul(hid, dff, mode="tn", out_dtypes=[BF16], name="dw_down")
    dw_down = dw_down.reshape(N_DEV, dff_w // N_DEV, d)
    (dxn2,), (sib_down,) = _matmul(
        dpre, g_up, mode="nt", out_dtypes=[F32], name="d_xn2", n_cols=d, tn=1024, tk=min(up_cols, 1024),
        b_spec=lambda tk, tn: pl.BlockSpec((1, tn, tk), lambda i, j, k: (k // (up_cols // tk), j, k % (up_cols // tk))),
        comms=[_rs_first(dw_down)])
    pair_down = _pair_add(dw_down, sib_down, mc, "pair_add_down")
    (dw_up,), (got_down,) = _matmul(
        xn2, dpre, mode="tn", out_dtypes=[BF16], name="dw_up",
        out_custom=lambda tm, tn: ((N_DEV, d, up_cols), (1, tm, tn), lambda i, j, k: (j // (up_cols // tn), i, j % (up_cols // tn))),
        comms=[_rs_second(pair_down)])
    (dmix, dh, dg2, dg3), (sib_up,) = _mid_bwd(h, mix, dy, dxn2, g2, g3, comms=[_rs_first(dw_up)])
    pair_up = _pair_add(dw_up, sib_up, mc, "pair_add_up")
    (dmixed,) = _matmul(dmix, w_out_f, mode="nt", out_dtypes=[F32], name="d_mixed")
    (dw_out,) = _matmul(mixed, dmix, mode="tn", out_dtypes=[BF16], name="dw_out")
    dw_out = dw_out.reshape(N_DEV, d // N_DEV, d)
    (dscb, dscc, dsch, dconv_s), (sib_out,) = _sc_bwd(proj, conv_s, dmixed, bsz, seq, scw, lay.sc_col, gw,
                                                      comms=[_rs_first(dw_out)])
    pair_out = _pair_add(dw_out, sib_out, mc, "pair_add_out")
    (dact, dz, dab, dalog, ddtb, dgnw), (got_up, got_out) = _gdn_bwd(
        qkv, proj, alog_t, dtb_t, gdn_norm_w, ssave, tsave, dmixed, bsz, seq, heads, 3 * gw, lay.ab_col,
        comms=[_rs_second(pair_up), _rs_second(pair_out)])
    (dqkv, dconv_q), _ = _qkvconv_bwd(proj, conv_q, dact, bsz, seq, gw)
    dproj = jnp.concatenate([dqkv, dz, dab, dscb, dscc, dsch, jnp.zeros((t, lay.wp - lay.used), BF16)], axis=1)
    nfb = lay.fw // MXU
    (dw_in,) = _matmul(xn, dproj, mode="tn", out_dtypes=[BF16], name="dw_in", n_cols=N_DEV * lay.fw, tn=MXU,
                       b_spec=lambda tk, tn: pl.BlockSpec((tk, tn), lambda i, j, k: (k, lay.frame_block(j // nfb) + j % nfb)),
                       out_custom=lambda tm, tn: ((N_DEV, d, lay.fw), (1, tm, tn), lambda i, j, k: (j // nfb, i, j % nfb)))
    half = t // 2
    (dxn_a,), (sib_in,) = _matmul(dproj, w_pad, mode="nt", out_dtypes=[F32], name="d_xn_a", tn=1024, tk=768,
                                  a_rows=(0, half), comms=[_rs_first(dw_in)])
    pair_in = _pair_add(dw_in, sib_in, mc, "pair_add_in")
    (dxn_b,), (got_in,) = _matmul(dproj, w_pad, mode="nt", out_dtypes=[F32], name="d_xn_b", tn=1024, tk=768,
                                  a_rows=(half, half), comms=[_rs_second(pair_in)])
    grad_x, dg1 = _pre_bwd(x2, dh, jnp.concatenate([dxn_a, dxn_b], axis=0), g1)

    gin_frame = _sum_to(pieces(pair_in, got_in), F32, "grad_w_in_frame")
    big = {
        "w_in": _adamw(w_in[0], m_w_in[0], v_w_in[0], [(lay.from_frame(gin_frame, me), None)], "adamw_w_in"),
        "w_out": _adamw(w_out[0], m_w_out[0], v_w_out[0], pieces(pair_out, got_out), "adamw_w_out"),
        "w_up": _adamw(w_up[0], m_w_up[0], v_w_up[0], pieces(pair_up, got_up), "adamw_w_up"),
        "w_down": _adamw(w_down[0], m_w_down[0], v_w_down[0], pieces(pair_down, got_down), "adamw_w_down"),
    }

    small_shapes = [(kq, 3 * gw), (ks, scw), (1, d), (1, d), (1, d), (1, d), (1, LANE), (1, LANE), (1, hd), (1, LANE)]
    small = _unpack_rows(
        _allreduce_small(_pack_rows([dconv_q, dconv_s, dg1, dg2, dg3, dg4, jnp.sum(dalog, axis=0), jnp.sum(ddtb, axis=0),
                                     jnp.sum(dgnw, axis=0), loss_p]), "allreduce_small"), small_shapes)
    gq, gs, sg1, sg2, sg3, sg4, salog, sdtb, sgnw, sloss = small
    loss = sloss[0, 0]
    small_grads = {
        "norm_mix_pre": sg1, "conv_qkv_w": lax.dynamic_slice(gq, (0, me * cq_n), (kq, cq_n)),
        "a_log": salog[:, :heads], "dt_bias": sdtb[:, :heads], "gdn_norm_w": sgnw,
        "conv_sc_w": lax.dynamic_slice(gs, (0, me * cs_n), (ks, cs_n)),
        "norm_mix_post": sg2, "norm_mlp_pre": sg3, "norm_mlp_post": sg4,
    }
    weights = {"norm_mix_pre": (norm_mix_pre, m_norm_mix_pre, v_norm_mix_pre), "conv_qkv_w": (conv_qkv_w[0], m_conv_qkv_w[0], v_conv_qkv_w[0]),
               "a_log": (a_log, m_a_log, v_a_log), "dt_bias": (dt_bias, m_dt_bias, v_dt_bias),
               "gdn_norm_w": (gdn_norm_w, m_gdn_norm_w, v_gdn_norm_w), "conv_sc_w": (conv_sc_w[0], m_conv_sc_w[0], v_conv_sc_w[0]),
               "norm_mix_post": (norm_mix_post, m_norm_mix_post, v_norm_mix_post),
               "norm_mlp_pre": (norm_mlp_pre, m_norm_mlp_pre, v_norm_mlp_pre),
               "norm_mlp_post": (norm_mlp_post, m_norm_mlp_post, v_norm_mlp_post)}
    res = dict(big)
    for name, (w, m, v) in weights.items():
        res[name] = _adamw(w, m, v, [(small_grads[name], None)], "adamw_" + name)

    order = ["norm_mix_pre", "w_in", "conv_qkv_w", "a_log", "dt_bias", "gdn_norm_w", "conv_sc_w", "w_out", "norm_mix_post",
             "norm_mlp_pre", "w_up", "w_down", "norm_mlp_post"]
    shapes = {"norm_mix_pre": norm_mix_pre.shape, "w_in": w_in.shape, "conv_qkv_w": conv_qkv_w.shape, "a_log": a_log.shape,
              "dt_bias": dt_bias.shape, "gdn_norm_w": gdn_norm_w.shape, "conv_sc_w": conv_sc_w.shape, "w_out": w_out.shape,
              "norm_mix_post": norm_mix_post.shape, "norm_mlp_pre": norm_mlp_pre.shape, "w_up": w_up.shape,
              "w_down": w_down.shape, "norm_mlp_post": norm_mlp_post.shape}
    outs = [loss, grad_x.reshape(bsz, seq, d)]
    for part in range(4):
        outs += [res[nm][part].reshape(shapes[nm]) for nm in order]
    return tuple(outs)
```

```python
import functools
import math

import numpy as np
import jax
import jax.numpy as jnp
from jax import lax
from jax.experimental import pallas as pl
from jax.experimental.pallas import tpu as pltpu

F32 = jnp.float32
BF16 = jnp.bfloat16
HI = lax.Precision.HIGHEST
MESH = pl.DeviceIdType.MESH

N_DEV = 8
LANE = 128
MXU = 256
CHUNK = 64
NORM_EPS = 1e-6
L2_EPS = 1e-6
VMEM_LIMIT = 56 * 1024 * 1024

ADAM_LR = 0.001
ADAM_B1 = 0.9
ADAM_B2 = 0.999
ADAM_EPS = 1e-08
ADAM_WD = 0.01
ADAM_STEP = 10

NN = (((1,), (0,)), ((), ()))
NT = (((1,), (1,)), ((), ()))
TN = (((0,), (0,)), ((), ()))


def _params(*sem):
    return pltpu.CompilerParams(dimension_semantics=sem, vmem_limit_bytes=VMEM_LIMIT)


def _tile(n, want):
    if n <= want:
        return n
    t = (want // LANE) * LANE
    while t > LANE and n % t:
        t -= LANE
    assert n % t == 0, (n, want)
    return t


class _Comm:
    def __init__(self, srcs, lands, plan, n, alias=None):
        self.srcs, self.lands, self.plan, self.n, self.alias = list(srcs), list(lands), plan, n, dict(alias or {})


def _pcall(body, *, grid, in_specs, out_specs, out_shape, operands, name, scratch_shapes=(), semantics=None,
           prefetch=(), comms=()):
    n_pf, n_in, n_out, n_scr = len(prefetch), len(in_specs), len(out_specs), len(scratch_shapes)
    srcs = [s for cm in comms for s in cm.srcs]
    lands = [l for cm in comms for l in cm.lands]
    n_src, n_land = len(srcs), len(lands)
    n_copies = sum(cm.n for cm in comms)
    aliases, so, lo = {}, 0, 0
    for cm in comms:
        for a, b in cm.alias.items():
            aliases[n_pf + n_in + so + a] = n_out + lo + b
        so, lo = so + len(cm.srcs), lo + len(cm.lands)
    any_spec = pl.BlockSpec(memory_space=pl.ANY)

    def wrapped(*refs):
        pf, r = refs[:n_pf], refs[n_pf:]
        ins, csrc = r[:n_in], r[n_in:n_in + n_src]
        outs = r[n_in + n_src:n_in + n_src + n_out]
        cland = r[n_in + n_src + n_out:n_in + n_src + n_out + n_land]
        rest = r[n_in + n_src + n_out + n_land:]
        scratch = rest[:n_scr]
        if not comms:
            body(*pf, *ins, *outs, *scratch)
            return
        send_sems, recv_sems = rest[n_scr:]

        def copies():
            out, k, s0, l0 = [], 0, 0, 0
            for cm in comms:
                for kind, src, dst, dev in cm.plan(csrc[s0:s0 + len(cm.srcs)], cland[l0:l0 + len(cm.lands)]):
                    if kind == "local":
                        out.append((kind, pltpu.make_async_copy(src, dst, send_sems.at[k])))
                    else:
                        out.append((kind, pltpu.make_async_remote_copy(
                            src_ref=src, dst_ref=dst, send_sem=send_sems.at[k], recv_sem=recv_sems.at[k],
                            device_id=dev, device_id_type=MESH)))
                    k += 1
                s0, l0 = s0 + len(cm.srcs), l0 + len(cm.lands)
            assert k == n_copies
            return out

        ids = [pl.program_id(a) for a in range(len(grid))]
        first = functools.reduce(jnp.logical_and, [i == 0 for i in ids])
        last = functools.reduce(jnp.logical_and, [i == g - 1 for i, g in zip(ids, grid)])

        @pl.when(first)
        def _():
            for _, cp in copies():
                cp.start()

        body(*pf, *ins, *outs, *scratch)

        @pl.when(last)
        def _():
            cps = copies()
            for kind, cp in cps:
                if kind == "remote":
                    cp.wait_recv()
            for kind, cp in cps:
                if kind == "remote":
                    cp.wait_send()
                else:
                    cp.wait()

    sems = [pltpu.SemaphoreType.DMA((n_copies,)), pltpu.SemaphoreType.DMA((n_copies,))] if comms else []
    if semantics is None or comms:
        semantics = ("arbitrary",) * len(grid)
    res = pl.pallas_call(
        wrapped,
        grid_spec=pltpu.PrefetchScalarGridSpec(
            num_scalar_prefetch=n_pf, grid=tuple(grid), in_specs=list(in_specs) + [any_spec] * n_src,
            out_specs=list(out_specs) + [any_spec] * n_land, scratch_shapes=list(scratch_shapes) + sems),
        out_shape=list(out_shape) + lands,
        input_output_aliases=aliases,
        compiler_params=_params(*semantics), name=name)(*prefetch, *operands, *srcs)
    return list(res[:n_out]), list(res[n_out:])


def _bdot(a, b, dims=NN):
    return lax.dot_general(a.astype(BF16), b.astype(BF16), dims, preferred_element_type=F32)


def _hdot(a, b, dims=NN):
    return lax.dot_general(a, b, dims, preferred_element_type=F32, precision=HI)


def _sigmoid(x):
    return 1.0 / (1.0 + jnp.exp(-x))


def _softplus(x):
    return jnp.maximum(x, 0.0) + jnp.log(1.0 + jnp.exp(-jnp.abs(x)))


def _matmul(a, b, *, mode, out_dtypes, name, n_cols=None, tm=1024, tn=512, tk=4096, epilogue=None, extras=(),
            b_spec=None, out_custom=None, a_rows=None, comms=()):
    if mode == "tn":
        K, M = a.shape
    else:
        M, K = a.shape
    r0 = 0
    if a_rows is not None:
        r0, M = a_rows
    N = n_cols if n_cols is not None else (b.shape[0] if mode == "nt" else b.shape[1])
    tm, tk, tn = _tile(M, tm), _tile(K, tk), _tile(N, tn)
    assert r0 % tm == 0
    i0 = r0 // tm
    if b_spec is None:
        b_spec = pl.BlockSpec((tn, tk), lambda i, j, k: (j, k)) if mode == "nt" else pl.BlockSpec((tk, tn), lambda i, j, k: (k, j))
    else:
        b_spec = b_spec(tk, tn)
    gm, gn, nk = M // tm, N // tn, K // tk
    if out_custom is None:
        out_shapes = [(M, N)] * len(out_dtypes)
        out_blocks = [(tm, tn)] * len(out_dtypes)
        out_index = [lambda i, j, k: (i, j)] * len(out_dtypes)
    else:
        shape, blk, ix = out_custom(tm, tn)
        out_shapes, out_blocks, out_index = [shape] * len(out_dtypes), [blk] * len(out_dtypes), [ix] * len(out_dtypes)
    a_spec = pl.BlockSpec((tk, tm), lambda i, j, k: (k, i)) if mode == "tn" else pl.BlockSpec((tm, tk), lambda i, j, k: (i + i0, k))
    hoist = mode == "tn" and nk == 1 and gn > 1
    dims = {"nn": NN, "nt": NT, "tn": TN}[mode]
    n_ex, n_out = len(extras), len(out_dtypes)

    def body(a_ref, b_ref, *rest):
        ex, outs = rest[:n_ex], rest[n_ex:n_ex + n_out]

        def finish(acc):
            res = epilogue(acc, *[e[...] for e in ex]) if epilogue is not None else (acc,)
            for o, r in zip(outs, res):
                o[...] = r.reshape(o.shape).astype(o.dtype)

        bb = b_ref[...]
        bb = bb.reshape(bb.shape[-2:])
        if hoist:
            at_ref = rest[-1]

            @pl.when(pl.program_id(1) == 0)
            def _():
                at_ref[...] = a_ref[...].T

            finish(lax.dot_general(at_ref[...], bb, NN, preferred_element_type=F32))
            return
        part = lax.dot_general(a_ref[...], bb, dims, preferred_element_type=F32)
        if nk == 1:
            finish(part)
        else:
            acc = rest[-1]
            k = pl.program_id(2)

            @pl.when(k == 0)
            def _():
                acc[...] = part

            @pl.when(k > 0)
            def _():
                acc[...] += part

            @pl.when(k == nk - 1)
            def _():
                finish(acc[...])

    scratch = [pltpu.VMEM((tm, tk), BF16)] if hoist else ([pltpu.VMEM((tm, tn), F32)] if nk > 1 else [])
    outs, lands = _pcall(
        body, grid=(gm, gn, nk),
        in_specs=[a_spec, b_spec] + [pl.BlockSpec((tm, tn), lambda i, j, k: (i, j)) for _ in extras],
        out_specs=[pl.BlockSpec(blk, ix) for blk, ix in zip(out_blocks, out_index)],
        out_shape=[jax.ShapeDtypeStruct(s, d) for s, d in zip(out_shapes, out_dtypes)],
        scratch_shapes=scratch, semantics=("parallel", "arbitrary", "arbitrary"),
        operands=[a, b, *extras], name=name, comms=comms)
    return (outs, lands) if comms else outs


TR = 128


def _rms(x):
    return lax.rsqrt(jnp.mean(x * x, axis=-1, keepdims=True) + NORM_EPS)


def _rms_bwd(x, r, w, dy):
    u = dy * w
    dx = r * u - x * (r * r * r) * jnp.mean(x * u, axis=-1, keepdims=True)
    return dx, dy * x * r


def _row_call(body, ins, row_flags, outs, name, n_rows, comms=()):
    tr = min(TR, n_rows)
    in_specs = []
    for arr, is_row in zip(ins, row_flags):
        if is_row:
            in_specs.append(pl.BlockSpec((tr, arr.shape[1]), lambda i: (i, 0)))
        else:
            in_specs.append(pl.BlockSpec(arr.shape, lambda i: (0, 0)))
    out_specs, out_shape = [], []
    for shape, dtype, kind in outs:
        if kind == "row":
            out_specs.append(pl.BlockSpec((tr, shape[1]), lambda i: (i, 0)))
        else:
            out_specs.append(pl.BlockSpec(shape, lambda i: (0, 0)))
        out_shape.append(jax.ShapeDtypeStruct(shape, dtype))
    res, lands = _pcall(body, grid=(n_rows // tr,), in_specs=in_specs, out_specs=out_specs, out_shape=out_shape,
                        operands=list(ins), name=name, comms=comms)
    return (res, lands) if comms else res


def _acc_out(ref, val):
    @pl.when(pl.program_id(0) == 0)
    def _():
        ref[...] = val

    @pl.when(pl.program_id(0) > 0)
    def _():
        ref[...] += val


def _rms_fwd(x, g):
    T, D = x.shape

    def body(x_ref, g_ref, o_ref):
        xv = x_ref[...]
        o_ref[...] = (xv * _rms(xv) * g_ref[...]).astype(BF16)

    return _row_call(body, [x, g], [True, False], [((T, D), BF16, "row")], "rms_fwd", T)[0]


def _post1(x, mix, g2, g3, comms=()):
    T, D = x.shape

    def body(x_ref, mix_ref, g2_ref, g3_ref, h_ref, xn2_ref):
        mv = mix_ref[...]
        h = x_ref[...] + mv * _rms(mv) * g2_ref[...]
        h_ref[...] = h
        xn2_ref[...] = (h * _rms(h) * g3_ref[...]).astype(BF16)

    return _row_call(body, [x, mix, g2, g3], [True, True, False, False],
                     [((T, D), F32, "row"), ((T, D), BF16, "row")], "post1", T, comms=comms)


def _post2_loss(h, ff, g4, target):
    T, D = h.shape

    def body(h_ref, ff_ref, g4_ref, t_ref, dff_ref, dy_ref, dg4_ref, loss_ref):
        fv = ff_ref[...]
        r = _rms(fv)
        err = h_ref[...] + fv * r * g4_ref[...] - t_ref[...]
        dy = err * (1.0 / D)
        dy_ref[...] = dy
        dff, dwt = _rms_bwd(fv, r, g4_ref[...], dy)
        dff_ref[...] = dff.astype(BF16)
        _acc_out(dg4_ref, jnp.sum(dwt, axis=0, keepdims=True))
        part = 0.5 * jnp.sum(jnp.mean(err * err, axis=-1, keepdims=True), axis=0, keepdims=True)
        _acc_out(loss_ref, jnp.broadcast_to(part, (1, LANE)))

    return _row_call(body, [h, ff, g4, target], [True, True, False, True],
                     [((T, D), BF16, "row"), ((T, D), F32, "row"), ((1, D), F32, "acc"), ((1, LANE), F32, "acc")],
                     "post2_loss", T)


def _mid_bwd(h, mix, dy, dxn2, g2, g3, comms=()):
    T, D = h.shape

    def body(h_ref, mix_ref, dy_ref, dxn2_ref, g2_ref, g3_ref, dmix_ref, dh_ref, dg2_ref, dg3_ref):
        hv = h_ref[...]
        d1, dw3 = _rms_bwd(hv, _rms(hv), g3_ref[...], dxn2_ref[...])
        dh = dy_ref[...] + d1
        dh_ref[...] = dh
        mv = mix_ref[...]
        dmix, dw2 = _rms_bwd(mv, _rms(mv), g2_ref[...], dh)
        dmix_ref[...] = dmix.astype(BF16)
        _acc_out(dg2_ref, jnp.sum(dw2, axis=0, keepdims=True))
        _acc_out(dg3_ref, jnp.sum(dw3, axis=0, keepdims=True))

    return _row_call(body, [h, mix, dy, dxn2, g2, g3], [True, True, True, True, False, False],
                     [((T, D), BF16, "row"), ((T, D), F32, "row"), ((1, D), F32, "acc"), ((1, D), F32, "acc")],
                     "mid_bwd", T, comms=comms)


def _pre_bwd(x, dh, dxn, g1, comms=()):
    T, D = x.shape

    def body(x_ref, dh_ref, dxn_ref, g1_ref, gx_ref, dg1_ref):
        xv = x_ref[...]
        d1, dw1 = _rms_bwd(xv, _rms(xv), g1_ref[...], dxn_ref[...])
        gx_ref[...] = dh_ref[...] + d1
        _acc_out(dg1_ref, jnp.sum(dw1, axis=0, keepdims=True))

    return _row_call(body, [x, dh, dxn, g1], [True, True, True, False],
                     [((T, D), F32, "row"), ((1, D), F32, "acc")], "pre_bwd", T, comms=comms)


def _shift_down(x, s):
    if s == 0:
        return x
    row = lax.broadcasted_iota(jnp.int32, x.shape, 0)
    return jnp.where(row >= s, pltpu.roll(x, s, axis=0), 0.0)


def _shift_up(x, s):
    if s == 0:
        return x
    n = x.shape[0]
    row = lax.broadcasted_iota(jnp.int32, x.shape, 0)
    return jnp.where(row < n - s, pltpu.roll(x, n - s, axis=0), 0.0)


def _conv(x, w):
    kw = w.shape[0]
    out = w[kw - 1:kw, :] * x
    for j in range(kw - 1):
        out = out + w[j:j + 1, :] * _shift_down(x, kw - 1 - j)
    return out


def _conv_bwd(x, w, dout):
    kw = w.shape[0]
    dx = w[kw - 1:kw, :] * dout
    dws = []
    for j in range(kw - 1):
        dx = dx + w[j:j + 1, :] * _shift_up(dout, kw - 1 - j)
        dws.append(jnp.sum(dout * _shift_down(x, kw - 1 - j), axis=0, keepdims=True))
    dws.append(jnp.sum(dout * x, axis=0, keepdims=True))
    return dx, jnp.concatenate(dws, axis=0)


def _qkvconv_fwd(proj, w, bsz, seq, gw, comms=()):
    nct = gw // LANE
    kw = w.shape[0]

    def body(p_ref, w_ref, o_ref):
        cv = _conv(p_ref[...], w_ref[...])
        o_ref[...] = (cv * _sigmoid(cv)).reshape(o_ref.shape)

    res, lands = _pcall(
        body, grid=(3, bsz, nct),
        in_specs=[pl.BlockSpec((seq, LANE), lambda p, b, c: (b, p * nct + c)),
                  pl.BlockSpec((kw, LANE), lambda p, b, c: (0, p * nct + c))],
        out_specs=[pl.BlockSpec((1, seq, LANE), lambda p, b, c: (p, b, c))],
        out_shape=[jax.ShapeDtypeStruct((3, bsz * seq, gw), F32)],
        semantics=("parallel", "parallel", "parallel"), operands=[proj, w], name="qkvconv_fwd", comms=comms)
    return res[0], lands


def _qkvconv_bwd(proj, w, dact, bsz, seq, gw, comms=()):
    nct = gw // LANE
    kw = w.shape[0]

    def body(p_ref, w_ref, d_ref, dp_ref, dw_ref):
        pre = p_ref[...]
        wv = w_ref[...]
        cv = _conv(pre, wv)
        sg = _sigmoid(cv)
        dcv = d_ref[...].reshape(cv.shape) * (sg * (1.0 + cv * (1.0 - sg)))
        dpre, dw = _conv_bwd(pre, wv, dcv)
        dp_ref[...] = dpre.astype(BF16)
        b = pl.program_id(2)

        @pl.when(b == 0)
        def _():
            dw_ref[...] = dw

        @pl.when(b > 0)
        def _():
            dw_ref[...] += dw

    res, lands = _pcall(
        body, grid=(3, nct, bsz),
        in_specs=[pl.BlockSpec((seq, LANE), lambda p, c, b: (b, p * nct + c)),
                  pl.BlockSpec((kw, LANE), lambda p, c, b: (0, p * nct + c)),
                  pl.BlockSpec((1, seq, LANE), lambda p, c, b: (p, b, c))],
        out_specs=[pl.BlockSpec((seq, LANE), lambda p, c, b: (b, p * nct + c)),
                   pl.BlockSpec((kw, LANE), lambda p, c, b: (0, p * nct + c))],
        out_shape=[jax.ShapeDtypeStruct((bsz * seq, 3 * gw), BF16), jax.ShapeDtypeStruct((kw, 3 * gw), F32)],
        semantics=("parallel", "parallel", "arbitrary"), operands=[proj, w, dact], name="qkvconv_bwd", comms=comms)
    return res, lands


def _sc_fwd(proj, w, bsz, seq, scw, col0):
    nct = scw // LANE
    c0 = col0 // LANE
    kw = w.shape[0]

    def body(b_ref, c_ref, h_ref, w_ref, o_ref):
        o_ref[...] = (b_ref[...] * _conv(c_ref[...] * h_ref[...], w_ref[...])).astype(BF16)

    return pl.pallas_call(
        body, grid=(bsz, nct),
        in_specs=[pl.BlockSpec((seq, LANE), lambda b, c: (b, c0 + c)),
                  pl.BlockSpec((seq, LANE), lambda b, c: (b, c0 + nct + c)),
                  pl.BlockSpec((seq, LANE), lambda b, c: (b, c0 + 2 * nct + c)),
                  pl.BlockSpec((kw, LANE), lambda b, c: (0, c))],
        out_specs=pl.BlockSpec((seq, LANE), lambda b, c: (b, c)),
        out_shape=jax.ShapeDtypeStruct((bsz * seq, scw), BF16),
        compiler_params=_params("parallel", "parallel"), name="sc_fwd")(proj, proj, proj, w)


def _sc_bwd(proj, w, dout, bsz, seq, scw, col0, dcol0, comms=()):
    nct = scw // LANE
    c0 = col0 // LANE
    d0 = dcol0 // LANE
    kw = w.shape[0]

    def body(b_ref, c_ref, h_ref, w_ref, d_ref, db_ref, dc_ref, dh_ref, dw_ref):
        cc, hh, wv, dv = c_ref[...], h_ref[...], w_ref[...], d_ref[...]
        m = cc * hh
        db_ref[...] = (dv * _conv(m, wv)).astype(BF16)
        dm, dw = _conv_bwd(m, wv, dv * b_ref[...])
        dc_ref[...] = (dm * hh).astype(BF16)
        dh_ref[...] = (dm * cc).astype(BF16)
        b = pl.program_id(1)

        @pl.when(b == 0)
        def _():
            dw_ref[...] = dw

        @pl.when(b > 0)
        def _():
            dw_ref[...] += dw

    res, lands = _pcall(
        body, grid=(nct, bsz),
        in_specs=[pl.BlockSpec((seq, LANE), lambda c, b: (b, c0 + c)),
                  pl.BlockSpec((seq, LANE), lambda c, b: (b, c0 + nct + c)),
                  pl.BlockSpec((seq, LANE), lambda c, b: (b, c0 + 2 * nct + c)),
                  pl.BlockSpec((kw, LANE), lambda c, b: (0, c)),
                  pl.BlockSpec((seq, LANE), lambda c, b: (b, d0 + c))],
        out_specs=[pl.BlockSpec((seq, LANE), lambda c, b: (b, c)),
                   pl.BlockSpec((seq, LANE), lambda c, b: (b, c)),
                   pl.BlockSpec((seq, LANE), lambda c, b: (b, c)),
                   pl.BlockSpec((kw, LANE), lambda c, b: (0, c))],
        out_shape=[jax.ShapeDtypeStruct((bsz * seq, scw), BF16)] * 3 + [jax.ShapeDtypeStruct((kw, scw), F32)],
        semantics=("parallel", "arbitrary"), operands=[proj, proj, proj, w, dout], name="sc_bwd", comms=comms)
    return res, lands


HEADS_PER_STEP = 8


def _colsel(tile, idx):
    lane = lax.broadcasted_iota(jnp.int32, tile.shape, 1)
    return jnp.sum(jnp.where(lane == idx, tile, 0.0), axis=1, keepdims=True)


def _rowsel(tile, idx):
    row = lax.broadcasted_iota(jnp.int32, tile.shape, 0)
    return jnp.sum(jnp.where(row == idx, tile, 0.0), axis=0, keepdims=True)


def _colput(col, idx, width=LANE):
    lane = lax.broadcasted_iota(jnp.int32, (col.shape[0], width), 1)
    return jnp.where(lane == idx, col, 0.0)


def _tri_masks(c):
    row = lax.broadcasted_iota(jnp.int32, (c, c), 0)
    col = lax.broadcasted_iota(jnp.int32, (c, c), 1)
    return row >= col, row > col, row == col


def _unit_lower_inverses(ms):
    c = ms[0].shape[0]
    _, _, eye = _tri_masks(c)
    ps = [-m for m in ms]
    ts = [jnp.where(eye, 1.0, 0.0) + p for p in ps]
    for _ in range(int(math.log2(c)) - 1):
        ps = [_hdot(p, p) for p in ps]
        ts = [t + _hdot(t, p) for t, p in zip(ts, ps)]
    return ts


def _gates(ab, alog, dtb):
    g = -jnp.exp(alog) * _softplus(ab + dtb)
    return g, _sigmoid(ab)


def _l2n(x):
    r = lax.rsqrt(jnp.sum(x * x, axis=-1, keepdims=True) + L2_EPS)
    return x * r, r


def _gdn_chunk_common(q, k, gc, gr, bc):
    c, dk = q.shape
    incl, strict, _ = _tri_masks(c)
    qh, rq = _l2n(q)
    kn, rk = _l2n(k)
    qn = qh * (dk ** -0.5)
    dm = jnp.where(incl, jnp.exp(jnp.where(incl, gc - gr, 0.0)), 0.0)
    kk = _bdot(kn, kn, NT)
    m = jnp.where(strict, bc * kk * dm, 0.0)
    pm = jnp.where(incl, _bdot(qn, kn, NT) * dm, 0.0)
    return qh, rq, kn, rk, qn, dm, kk, m, pm


def _gdn_fwd(qkv, proj, alog, dtb, gnw, bsz, seq, heads, z_col0, ab_col0, comms=()):
    c = CHUNK
    nch = seq // c
    hb = min(HEADS_PER_STEP, heads)
    ng = heads // hb
    hd = qkv.shape[2] // heads
    wb = hb * hd

    def body(qkv_ref, z_ref, ab_ref, alog_ref, dtb_ref, gnw_ref, o_ref, ssave_ref, tsave_ref, s_ref, gc_ref, gt_ref, be_ref):
        n, hg = pl.program_id(1), pl.program_id(2)

        @pl.when((n == 0) & (hg == 0))
        def _():
            s_ref[...] = jnp.zeros_like(s_ref)

        @pl.when(hg == 0)
        def _():
            g, beta = _gates(ab_ref[...], alog_ref[...], dtb_ref[...])
            incl, _, _ = _tri_masks(c)
            gcum = _hdot(jnp.where(incl, 1.0, 0.0), g)
            gc_ref[...] = gcum
            gt_ref[...] = gcum.T
            be_ref[...] = beta

        gc_t, gt_t, be_t, gnw_v = gc_ref[...], gt_ref[...], be_ref[...], gnw_ref[...]
        hs = range(hb)
        sls = [slice(hh * hd, (hh + 1) * hd) for hh in hs]
        states = [s_ref[hg * hb + hh] for hh in hs]
        gcs = [_colsel(gc_t, hg * hb + hh) for hh in hs]
        grs = [_rowsel(gt_t, hg * hb + hh) for hh in hs]
        bcs = [_colsel(be_t, heads + hg * hb + hh) for hh in hs]
        com = [_gdn_chunk_common(qkv_ref[0, :, sls[hh]], qkv_ref[1, :, sls[hh]], gcs[hh], grs[hh], bcs[hh]) for hh in hs]
        kns, qns, pms = [cm[2] for cm in com], [cm[4] for cm in com], [cm[8] for cm in com]
        tms = _unit_lower_inverses([cm[7] for cm in com])
        gams = [jnp.exp(gc) for gc in gcs]
        glasts = [gc[c - 1:c, :] for gc in gcs]
        kss = [_bdot(kns[hh], states[hh]) for hh in hs]
        qss = [_bdot(qns[hh], states[hh]) for hh in hs]
        vns = [_bdot(tms[hh], bcs[hh] * (qkv_ref[2, :, sls[hh]] - gams[hh] * kss[hh])) for hh in hs]
        os_ = [gams[hh] * qss[hh] + _bdot(pms[hh], vns[hh]) for hh in hs]
        snews = [states[hh] * jnp.exp(glasts[hh]) + _bdot(kns[hh] * jnp.exp(glasts[hh] - gcs[hh]), vns[hh], TN) for hh in hs]
        for hh in hs:
            o = os_[hh]
            on = o * lax.rsqrt(jnp.mean(o * o, axis=-1, keepdims=True) + NORM_EPS) * gnw_v
            zz = z_ref[:, sls[hh]]
            ssave_ref[0, 0, hh] = states[hh]
            tsave_ref[0, 0, hh] = tms[hh]
            s_ref[hg * hb + hh] = snews[hh]
            o_ref[:, sls[hh]] = (on * (zz * _sigmoid(zz))).astype(BF16)

    row = lambda b, n, g: b * nch + n
    return _pcall(
        body, grid=(bsz, nch, ng),
        in_specs=[pl.BlockSpec((3, c, wb), lambda b, n, g: (0, row(b, n, g), g)),
                  pl.BlockSpec((c, wb), lambda b, n, g: (row(b, n, g), z_col0 // wb + g)),
                  pl.BlockSpec((c, LANE), lambda b, n, g: (row(b, n, g), ab_col0 // LANE)),
                  pl.BlockSpec((1, LANE), lambda b, n, g: (0, 0)),
                  pl.BlockSpec((1, LANE), lambda b, n, g: (0, 0)),
                  pl.BlockSpec((1, hd), lambda b, n, g: (0, 0))],
        out_specs=[pl.BlockSpec((c, wb), lambda b, n, g: (row(b, n, g), g)),
                   pl.BlockSpec((1, 1, hb, hd, hd), lambda b, n, g: (b, n, g, 0, 0)),
                   pl.BlockSpec((1, 1, hb, c, c), lambda b, n, g: (b, n, g, 0, 0))],
        out_shape=[jax.ShapeDtypeStruct((bsz * seq, heads * hd), BF16),
                   jax.ShapeDtypeStruct((bsz, nch, heads, hd, hd), F32),
                   jax.ShapeDtypeStruct((bsz, nch, heads, c, c), F32)],
        scratch_shapes=[pltpu.VMEM((heads, hd, hd), F32), pltpu.VMEM((c, LANE), F32), pltpu.VMEM((LANE, c), F32),
                        pltpu.VMEM((c, LANE), F32)],
        semantics=("parallel", "arbitrary", "arbitrary"), operands=[qkv, proj, proj, alog, dtb, gnw], name="gdn_fwd",
        comms=comms)


def _gdn_bwd(qkv, proj, alog, dtb, gnw, ssave, tsave, dout, bsz, seq, heads, z_col0, ab_col0, comms=()):
    c = CHUNK
    nch = seq // c
    hb = min(HEADS_PER_STEP, heads)
    ng = heads // hb
    hd = qkv.shape[2] // heads
    wb = hb * hd

    def body(qkv_ref, z_ref, ab_ref, alog_ref, dtb_ref, gnw_ref, ssave_ref, tsave_ref, do_ref,
             dact_ref, dz_ref, dab_ref, dalog_ref, ddtb_ref, dgnw_ref,
             ds_ref, gc_ref, gt_ref, be_ref, dgacc_ref, dbacc_ref):
        n, hg = pl.program_id(1), pl.program_id(2)
        incl, strict, _ = _tri_masks(c)

        @pl.when((n == 0) & (hg == 0))
        def _():
            ds_ref[...] = jnp.zeros_like(ds_ref)
            dalog_ref[...] = jnp.zeros_like(dalog_ref)
            ddtb_ref[...] = jnp.zeros_like(ddtb_ref)
            dgnw_ref[...] = jnp.zeros_like(dgnw_ref)

        @pl.when(hg == 0)
        def _():
            g, beta = _gates(ab_ref[...], alog_ref[...], dtb_ref[...])
            gcum = _hdot(jnp.where(incl, 1.0, 0.0), g)
            gc_ref[...] = gcum
            gt_ref[...] = gcum.T
            be_ref[...] = beta
            dgacc_ref[...] = jnp.zeros_like(dgacc_ref)
            dbacc_ref[...] = jnp.zeros_like(dbacc_ref)

        gc_t, gt_t, be_t, gnw_v = gc_ref[...], gt_ref[...], be_ref[...], gnw_ref[...]
        hs = range(hb)

        def each(f):
            return [f(hh) for hh in hs]

        rsum = lambda a: jnp.sum(a, axis=-1, keepdims=True)
        sls = each(lambda i: slice(i * hd, (i + 1) * hd))
        ds_in = each(lambda i: ds_ref[hg * hb + i])
        gc = each(lambda i: _colsel(gc_t, hg * hb + i))
        gr = each(lambda i: _rowsel(gt_t, hg * hb + i))
        bc = each(lambda i: _colsel(be_t, heads + hg * hb + i))
        com = each(lambda i: _gdn_chunk_common(qkv_ref[0, :, sls[i]], qkv_ref[1, :, sls[i]], gc[i], gr[i], bc[i]))
        qh, rq, kn, rk, qn, dm, kk, m, pm = [[cm[j] for cm in com] for j in range(9)]
        tm = each(lambda i: tsave_ref[0, 0, i])
        s = each(lambda i: ssave_ref[0, 0, i])
        gam = each(lambda i: jnp.exp(gc[i]))
        glast = each(lambda i: gc[i][c - 1:c, :])
        gl = each(lambda i: jnp.exp(glast[i]))
        ratio = each(lambda i: jnp.exp(glast[i] - gc[i]))
        ks = each(lambda i: _bdot(kn[i], s[i]))
        qs = each(lambda i: _bdot(qn[i], s[i]))
        r = each(lambda i: qkv_ref[2, :, sls[i]] - gam[i] * ks[i])
        vn = each(lambda i: _bdot(tm[i], bc[i] * r[i]))
        o = each(lambda i: gam[i] * qs[i] + _bdot(pm[i], vn[i]))
        ro = each(lambda i: lax.rsqrt(jnp.mean(o[i] * o[i], axis=-1, keepdims=True) + NORM_EPS))
        zz = each(lambda i: z_ref[:, sls[i]])
        sz = each(lambda i: _sigmoid(zz[i]))
        dd = each(lambda i: do_ref[:, sls[i]])
        don = each(lambda i: dd[i] * (zz[i] * sz[i]))
        dz_h = each(lambda i: (dd[i] * (o[i] * ro[i] * gnw_v) * (sz[i] * (1.0 + zz[i] * (1.0 - sz[i])))).astype(BF16))
        dgnw = sum(each(lambda i: jnp.sum(don[i] * o[i] * ro[i], axis=0, keepdims=True)))
        uu = each(lambda i: don[i] * gnw_v)
        d_o = each(lambda i: ro[i] * uu[i] - o[i] * (ro[i] * ro[i] * ro[i]) * jnp.mean(o[i] * uu[i], axis=-1, keepdims=True))
        dqs = each(lambda i: gam[i] * d_o[i])
        dq = each(lambda i: _bdot(dqs[i], s[i], NT))
        ds_new = each(lambda i: _bdot(qn[i], dqs[i], TN))
        dp = each(lambda i: jnp.where(incl, _bdot(d_o[i], vn[i], NT), 0.0))
        dvn = each(lambda i: _bdot(pm[i], d_o[i], TN))
        dgam = each(lambda i: rsum(d_o[i] * qs[i]))
        dkd = each(lambda i: _bdot(vn[i], ds_in[i], NT))
        dvn = each(lambda i: dvn[i] + _bdot(kn[i] * ratio[i], ds_in[i]))
        ds_new = each(lambda i: ds_new[i] + gl[i] * ds_in[i])
        dgl = each(lambda i: jnp.sum(jnp.sum(ds_in[i] * s[i], axis=1, keepdims=True), axis=0, keepdims=True))
        dratio = each(lambda i: rsum(dkd[i] * kn[i]))
        dpd = each(lambda i: dp[i] * dm[i])
        dq = each(lambda i: dq[i] + _bdot(dpd[i], kn[i]))
        dk = each(lambda i: ratio[i] * dkd[i] + _bdot(dpd[i], qn[i], TN))
        dx = each(lambda i: _bdot(tm[i], dvn[i], TN))
        dr = each(lambda i: bc[i] * dx[i])
        gdr = each(lambda i: gam[i] * dr[i])
        dk = each(lambda i: dk[i] - _bdot(gdr[i], s[i], NT))
        ds_new = each(lambda i: ds_new[i] - _bdot(kn[i], gdr[i], TN))
        dmm = each(lambda i: jnp.where(strict, -_bdot(dx[i], vn[i], NT), 0.0))
        ee = each(lambda i: dmm[i] * dm[i])
        be_e = each(lambda i: bc[i] * ee[i])
        dk = each(lambda i: dk[i] + _bdot(be_e[i], kn[i]) + _bdot(be_e[i], kn[i], TN))
        dbeta = each(lambda i: rsum(dx[i] * r[i]) + rsum(ee[i] * kk[i]))
        dgam = each(lambda i: dgam[i] - rsum(dr[i] * ks[i]))
        ff = each(lambda i: dp[i] * pm[i] + dmm[i] * m[i])
        rowi = lax.broadcasted_iota(jnp.int32, (c, 1), 0)
        dgc = each(lambda i: rsum(ff[i]) - rsum(ff[i].T) + dgam[i] * gam[i] - dratio[i] * ratio[i]
                   + jnp.where(rowi == c - 1, jnp.sum(dratio[i] * ratio[i], axis=0, keepdims=True) + dgl[i] * gl[i], 0.0))
        dg_tile = sum(each(lambda i: _colput(dgc[i], hg * hb + i)))
        db_tile = sum(each(lambda i: _colput(dbeta[i], heads + hg * hb + i)))
        for i in hs:
            dqh = dq[i] * (hd ** -0.5)
            ds_ref[hg * hb + i] = ds_new[i]
            dz_ref[:, sls[i]] = dz_h[i]
            dact_ref[0, :, sls[i]] = rq[i] * (dqh - qh[i] * rsum(qh[i] * dqh))
            dact_ref[1, :, sls[i]] = rk[i] * (dk[i] - kn[i] * rsum(kn[i] * dk[i]))
            dact_ref[2, :, sls[i]] = dr[i]
        dgacc_ref[...] += dg_tile
        dbacc_ref[...] += db_tile
        dgnw_ref[0] += dgnw

        @pl.when(hg == ng - 1)
        def _():
            ab = ab_ref[...]
            ea = jnp.exp(alog_ref[...])
            g = -ea * _softplus(ab + dtb_ref[...])
            beta = be_ref[...]
            dg = _hdot(jnp.where(incl, 1.0, 0.0), dgacc_ref[...], TN)
            lane = lax.broadcasted_iota(jnp.int32, ab.shape, 1)
            da = jnp.where(lane < heads, dg * (-ea) * _sigmoid(ab + dtb_ref[...]), 0.0)
            db = dbacc_ref[...] * beta * (1.0 - beta)
            dab_ref[...] = (da + db).astype(BF16)
            dalog_ref[0] += jnp.sum(jnp.where(lane < heads, dg * g, 0.0), axis=0, keepdims=True)
            ddtb_ref[0] += jnp.sum(da, axis=0, keepdims=True)

    row = lambda b, n, g: b * nch + (nch - 1 - n)
    rev = lambda n: nch - 1 - n
    return _pcall(
        body, grid=(bsz, nch, ng),
        in_specs=[pl.BlockSpec((3, c, wb), lambda b, n, g: (0, row(b, n, g), g)),
                  pl.BlockSpec((c, wb), lambda b, n, g: (row(b, n, g), z_col0 // wb + g)),
                  pl.BlockSpec((c, LANE), lambda b, n, g: (row(b, n, g), ab_col0 // LANE)),
                  pl.BlockSpec((1, LANE), lambda b, n, g: (0, 0)),
                  pl.BlockSpec((1, LANE), lambda b, n, g: (0, 0)),
                  pl.BlockSpec((1, hd), lambda b, n, g: (0, 0)),
                  pl.BlockSpec((1, 1, hb, hd, hd), lambda b, n, g: (b, rev(n), g, 0, 0)),
                  pl.BlockSpec((1, 1, hb, c, c), lambda b, n, g: (b, rev(n), g, 0, 0)),
                  pl.BlockSpec((c, wb), lambda b, n, g: (row(b, n, g), g))],
        out_specs=[pl.BlockSpec((3, c, wb), lambda b, n, g: (0, row(b, n, g), g)),
                   pl.BlockSpec((c, wb), lambda b, n, g: (row(b, n, g), g)),
                   pl.BlockSpec((c, LANE), lambda b, n, g: (row(b, n, g), 0)),
                   pl.BlockSpec((1, 1, LANE), lambda b, n, g: (b, 0, 0)),
                   pl.BlockSpec((1, 1, LANE), lambda b, n, g: (b, 0, 0)),
                   pl.BlockSpec((1, 1, hd), lambda b, n, g: (b, 0, 0))],
        out_shape=[jax.ShapeDtypeStruct((3, bsz * seq, heads * hd), F32),
                   jax.ShapeDtypeStruct((bsz * seq, heads * hd), BF16),
                   jax.ShapeDtypeStruct((bsz * seq, LANE), BF16),
                   jax.ShapeDtypeStruct((bsz, 1, LANE), F32),
                   jax.ShapeDtypeStruct((bsz, 1, LANE), F32),
                   jax.ShapeDtypeStruct((bsz, 1, hd), F32)],
        scratch_shapes=[pltpu.VMEM((heads, hd, hd), F32), pltpu.VMEM((c, LANE), F32), pltpu.VMEM((LANE, c), F32),
                        pltpu.VMEM((c, LANE), F32), pltpu.VMEM((c, LANE), F32), pltpu.VMEM((c, LANE), F32)],
        semantics=("parallel", "arbitrary", "arbitrary"),
        operands=[qkv, proj, proj, alog, dtb, gnw, ssave, tsave, dout], name="gdn_bwd", comms=comms)


ELEMWISE_BLOCK_ELEMS = 256 * 1024


def _rows_tile(rows, cols):
    want = max(16, ELEMWISE_BLOCK_ELEMS // cols)
    if rows <= want:
        return rows
    t = (want // 16) * 16
    while t > 16 and rows % t:
        t -= 16
    return t if rows % t == 0 else rows


def _piece_specs(pieces, tr, cols):
    specs, leads = [], []
    for p, (arr, lead) in enumerate(pieces):
        if arr.ndim == 3:
            specs.append(pl.BlockSpec((1, tr, cols), functools.partial(lambda i, idx, p: (idx[p], i, 0), p=p)))
        else:
            specs.append(pl.BlockSpec((tr, cols), lambda i, idx: (i, 0)))
        leads.append(jnp.asarray(0 if lead is None else lead, jnp.int32))
    return jnp.stack(leads), specs


def _sum_pieces(refs):
    total = None
    for r in refs:
        v = r[...].astype(F32)
        v = v.reshape(v.shape[-2:])
        total = v if total is None else total + v
    return total


def _adamw(w, m, v, pieces, name, comms=()):
    rows, cols = w.shape
    tr = _rows_tile(rows, cols)
    leads, pspecs = _piece_specs(pieces, tr, cols)
    npc = len(pieces)
    c1 = 1.0 - ADAM_B1 ** ADAM_STEP
    c2 = 1.0 - ADAM_B2 ** ADAM_STEP

    def body(idx_ref, w_ref, m_ref, v_ref, *rest):
        g = _sum_pieces(rest[:npc])
        g_ref, d_ref, nm_ref, nv_ref = rest[npc:]
        nm = ADAM_B1 * m_ref[...] + (1.0 - ADAM_B1) * g
        nv = ADAM_B2 * v_ref[...] + (1.0 - ADAM_B2) * (g * g)
        g_ref[...] = g
        nm_ref[...] = nm
        nv_ref[...] = nv
        d_ref[...] = -ADAM_LR * ((nm / c1) / (jnp.sqrt(nv / c2) + ADAM_EPS) + ADAM_WD * w_ref[...])

    wspec = pl.BlockSpec((tr, cols), lambda i, idx: (i, 0))
    res, lands = _pcall(body, grid=(rows // tr,), in_specs=[wspec] * 3 + pspecs, out_specs=[wspec] * 4,
                        out_shape=[jax.ShapeDtypeStruct((rows, cols), F32)] * 4, semantics=("parallel",),
                        prefetch=[leads], operands=[w, m, v, *[p for p, _ in pieces]], name=name, comms=comms)
    return (res, lands) if comms else res


def _sum_to(pieces, out_dtype, name):
    arr0 = pieces[0][0]
    rows, cols = arr0.shape[-2:]
    tr = _rows_tile(rows, cols)
    leads, pspecs = _piece_specs(pieces, tr, cols)

    def body(idx_ref, *rest):
        rest[-1][...] = _sum_pieces(rest[:-1]).astype(out_dtype)

    return pl.pallas_call(
        body,
        grid_spec=pltpu.PrefetchScalarGridSpec(num_scalar_prefetch=1, grid=(rows // tr,), in_specs=pspecs,
                                               out_specs=pl.BlockSpec((tr, cols), lambda i, idx: (i, 0))),
        out_shape=jax.ShapeDtypeStruct((rows, cols), out_dtype),
        compiler_params=_params("parallel"), name=name)(leads, *[p for p, _ in pieces])


def _pair_add(a, recv, my_c, name, comms=()):
    _, rows, cols = a.shape
    tr = _rows_tile(rows, cols)

    def body(c_ref, a_ref, r_ref, o_ref):
        o_ref[...] = (a_ref[...].astype(F32) + r_ref[...].astype(F32)).astype(BF16)

    res, lands = _pcall(
        body, grid=(4, rows // tr),
        in_specs=[pl.BlockSpec((1, tr, cols), lambda j, i, c: (2 * j + c[0], i, 0)),
                  pl.BlockSpec((1, tr, cols), lambda j, i, c: (j, i, 0))],
        out_specs=[pl.BlockSpec((1, tr, cols), lambda j, i, c: (j, i, 0))],
        out_shape=[jax.ShapeDtypeStruct((4, rows, cols), BF16)], semantics=("parallel", "parallel"),
        prefetch=[jnp.reshape(my_c, (1,)).astype(jnp.int32)], operands=[a, recv], name=name, comms=comms)
    return (res[0], lands) if comms else res[0]


def _to_frame(w, offs, fw, name):
    rows, n = w.shape
    tr = _tile(rows, 256)

    def body(off_ref, w_ref, o_ref, pad_ref):
        pad_ref[...] = jnp.zeros_like(pad_ref)
        pad_ref[:, 0:n] = w_ref[...]
        y = pad_ref[...]
        off1, len1, off2 = off_ref[0], off_ref[1], off_ref[2]
        col = lax.broadcasted_iota(jnp.int32, y.shape, 1)
        o_ref[...] = jnp.where(col < off1 + len1, pltpu.roll(y, off1, axis=1),
                               jnp.where(col >= off2 + len1, pltpu.roll(y, off2, axis=1), 0.0)).astype(BF16)

    res, _ = _pcall(body, grid=(rows // tr,), in_specs=[pl.BlockSpec((tr, n), lambda i, o: (i, 0))],
                    out_specs=[pl.BlockSpec((tr, fw), lambda i, o: (i, 0))],
                    out_shape=[jax.ShapeDtypeStruct((rows, fw), BF16)], scratch_shapes=[pltpu.VMEM((tr, fw), F32)],
                    semantics=("parallel",), prefetch=[offs], operands=[w], name=name)
    return res[0]


def _assemble(frames, table, n_blocks, width, name, comms=()):
    _, rows, _ = frames.shape
    tr = _tile(rows, 1024)

    def body(t_ref, f1_ref, f2_ref, o_ref):
        jb = pl.program_id(0)
        v = f1_ref[0]
        o_ref[...] = jnp.where(t_ref[4, jb] > 0, v + f2_ref[0], v)

    res, lands = _pcall(
        body, grid=(n_blocks, rows // tr),
        in_specs=[pl.BlockSpec((1, tr, width), lambda jb, i, t: (t[0, jb], i, t[1, jb])),
                  pl.BlockSpec((1, tr, width), lambda jb, i, t: (t[2, jb], i, t[3, jb]))],
        out_specs=[pl.BlockSpec((tr, width), lambda jb, i, t: (i, jb))],
        out_shape=[jax.ShapeDtypeStruct((rows, n_blocks * width), BF16)], semantics=("parallel", "parallel"),
        prefetch=[table], operands=[frames, frames], name=name, comms=comms)
    return (res[0], lands) if comms else res[0]


ANY = pl.BlockSpec(memory_space=pl.ANY)


def _place():
    x, y, c = lax.axis_index("x"), lax.axis_index("y"), lax.axis_index("c")
    chips = [(1 - x, y), (x, 1 - y), (1 - x, 1 - y)]
    return x, y, c, chips


def _allgather_big(shards, name):
    n = len(shards)

    def body(*refs):
        xs, outs = refs[:n], refs[n:2 * n]
        send_sems, recv_sems, local_sems = refs[2 * n:]
        x, y, c, chips = _place()
        me, sibling = (x, y, c), (x, y, 1 - c)

        def copy(a, k, block, to, src=None):
            dst = outs[a].at[4 * block[0] + 2 * block[1] + block[2]]
            return pltpu.make_async_remote_copy(src_ref=dst if src is None else src, dst_ref=dst,
                                                send_sem=send_sems.at[a, k], recv_sem=recv_sems.at[a, k],
                                                device_id=to, device_id_type=MESH)

        mine = [pltpu.make_async_copy(xs[a], outs[a].at[4 * x + 2 * y + c], local_sems.at[a]) for a in range(n)]
        for cp in mine:
            cp.start()
        first = []
        for a in range(n):
            first.append(copy(a, 0, me, sibling, src=xs[a]))
            first += [copy(a, 1 + j, me, (*chip, c), src=xs[a]) for j, chip in enumerate(chips)]
        for cp in first:
            cp.start()
        passed = []
        for j, chip in enumerate(chips):
            for a in range(n):
                copy(a, 1 + j, (*chip, c), me).wait_recv()
                cp = copy(a, 4 + j, (*chip, c), sibling)
                cp.start()
                passed.append(cp)
        for a in range(n):
            copy(a, 0, sibling, me).wait_recv()
            for j, chip in enumerate(chips):
                copy(a, 4 + j, (*chip, 1 - c), me).wait_recv()
        for cp in first + passed:
            cp.wait_send()
        for cp in mine:
            cp.wait()

    return pl.pallas_call(
        body, in_specs=[ANY] * n, out_specs=[ANY] * n,
        out_shape=[jax.ShapeDtypeStruct((N_DEV,) + s.shape, s.dtype) for s in shards],
        scratch_shapes=[pltpu.SemaphoreType.DMA((n, 7)), pltpu.SemaphoreType.DMA((n, 7)), pltpu.SemaphoreType.DMA((n,))],
        name=name)(*shards)


def _allreduce_small(buf, name):
    rows = buf.shape[0]

    def body(x_ref, o_ref, g_ref, send_sems, recv_sems):
        x, y, c, chips = _place()
        me, sibling = (x, y, c), (x, y, 1 - c)

        def copy(k, block, to, src=None):
            dst = g_ref.at[4 * block[0] + 2 * block[1] + block[2]]
            return pltpu.make_async_remote_copy(src_ref=dst if src is None else src, dst_ref=dst,
                                                send_sem=send_sems.at[k], recv_sem=recv_sems.at[k],
                                                device_id=to, device_id_type=MESH)

        first = [copy(0, me, sibling, src=x_ref)]
        first += [copy(1 + j, me, (*chip, c), src=x_ref) for j, chip in enumerate(chips)]
        for cp in first:
            cp.start()
        passed = [copy(4 + j, (*chip, c), sibling) for j, chip in enumerate(chips)]
        for j, chip in enumerate(chips):
            copy(1 + j, (*chip, c), me).wait_recv()
            passed[j].start()
        copy(0, sibling, me).wait_recv()
        for j, chip in enumerate(chips):
            copy(4 + j, (*chip, 1 - c), me).wait_recv()
        for cp in first + passed:
            cp.wait_send()
        g_ref[4 * x + 2 * y + c] = x_ref[...]
        total = g_ref[0]
        for s in range(1, N_DEV):
            total = total + g_ref[s]
        o_ref[...] = total

    vm = pl.BlockSpec(memory_space=pltpu.VMEM)
    return pl.pallas_call(
        body, in_specs=[vm], out_specs=vm, out_shape=jax.ShapeDtypeStruct((rows, LANE), F32),
        scratch_shapes=[pltpu.VMEM((N_DEV, rows, LANE), F32), pltpu.SemaphoreType.DMA((7,)), pltpu.SemaphoreType.DMA((7,))],
        name=name)(buf)


def _rows(ref, rows):
    return ref if rows is None else ref.at[pl.ds(rows[0], rows[1] - rows[0])]


def _ag_first(shard, rows=None, into=None):
    def plan(srcs, lands):
        x, y, c, chips = _place()
        src = _rows(srcs[0], rows)
        dst = _rows(lands[0].at[4 * x + 2 * y + c], rows)
        return ([("local", src, dst, None), ("remote", src, dst, (x, y, 1 - c))]
                + [("remote", src, dst, (*chip, c)) for chip in chips])

    land = jax.ShapeDtypeStruct((N_DEV,) + shard.shape, shard.dtype)
    if into is None:
        return _Comm([shard], [land], plan, 5)
    return _Comm([shard, into], [land], plan, 5, alias={1: 0})


def _ag_second(g, rows=None):
    def plan(srcs, lands):
        x, y, c, chips = _place()
        (buf,) = lands
        refs = [_rows(buf.at[4 * cx + 2 * cy + c], rows) for cx, cy in chips]
        return [("remote", r, r, (x, y, 1 - c)) for r in refs]

    return _Comm([g], [jax.ShapeDtypeStruct(g.shape, g.dtype)], plan, 3, alias={0: 0})


def _rs_first(grad):
    def plan(srcs, lands):
        x, y, c, _ = _place()
        (a,), (land,) = srcs, lands
        return [("remote", a.at[2 * j + (1 - c)], land.at[j], (x, y, 1 - c)) for j in range(4)]

    return _Comm([grad], [jax.ShapeDtypeStruct((4,) + grad.shape[1:], grad.dtype)], plan, 4)


def _rs_second(pair, rows=None, into=None):
    def plan(srcs, lands):
        x, y, c, chips = _place()
        return [("remote", _rows(srcs[0].at[2 * cx + cy], rows), _rows(lands[0].at[j], rows), (cx, cy, c))
                for j, (cx, cy) in enumerate(chips)]

    land = jax.ShapeDtypeStruct((3,) + pair.shape[1:], pair.dtype)
    if into is None:
        return _Comm([pair], [land], plan, 3)
    return _Comm([pair, into], [land], plan, 3, alias={1: 0})


class _InLayout:
    def __init__(self, n_in, gw, heads, scw):
        self.n_in, self.split = n_in, 4 * gw + 2 * heads
        self.gap = LANE - 2 * heads
        self.ab_col, self.sc_col = 4 * gw, 4 * gw + LANE
        self.used = 4 * gw + LANE + 3 * scw
        p0 = [s * n_in + (self.gap if s * n_in >= self.split else 0) for s in range(N_DEV)]
        self.fstart = [(p // MXU) * MXU for p in p0]
        need = []
        for s in range(N_DEV):
            straddle = s * n_in < self.split < (s + 1) * n_in
            need.append(p0[s] - self.fstart[s] + n_in + (self.gap if straddle else 0))
        self.fw = -(-max(need) // MXU) * MXU
        self.wp = max(f + self.fw for f in self.fstart)
        assert self.wp >= self.used and self.wp % MXU == 0
        nfb = self.fw // MXU
        rows = []
        for jb in range(self.wp // MXU):
            src = [(s, jb - self.fstart[s] // MXU) for s in range(N_DEV) if 0 <= jb - self.fstart[s] // MXU < nfb]
            assert 1 <= len(src) <= 2, (jb, src)
            (s1, b1), (s2, b2) = src[0], src[-1]
            rows.append((s1, b1, s2, b2, int(len(src) == 2)))
        self.table = np.asarray(rows, np.int32).T.copy()

    def frame_block(self, s):
        p = s * self.n_in
        return (p + jnp.where(p >= self.split, self.gap, 0)) // MXU

    def offsets(self, s):
        p = s * self.n_in
        after = p >= self.split
        off1 = p + jnp.where(after, self.gap, 0) - self.frame_block(s) * MXU
        len1 = jnp.where(after, self.n_in, jnp.clip(self.split - p, 0, self.n_in))
        off2 = off1 + jnp.where(len1 < self.n_in, self.gap, 0)
        return off1, len1, off2

    def to_frame(self, w, s):
        return _to_frame(w, jnp.stack(self.offsets(s)).astype(jnp.int32), self.fw, "w_in_frame")

    def from_frame(self, f, s):
        off1, len1, off2 = self.offsets(s)
        a = lax.dynamic_slice(f, (0, off1), (f.shape[0], self.n_in))
        b = lax.dynamic_slice(f, (0, off2), (f.shape[0], self.n_in))
        col = lax.broadcasted_iota(jnp.int32, (1, self.n_in), 1)
        return jnp.where(col < len1, a, b)


def _pack_rows(parts):
    rows = []
    for p in parts:
        flat = p.reshape(-1)
        pad = (-flat.shape[0]) % LANE
        rows.append(jnp.pad(flat, (0, pad)).reshape(-1, LANE))
    buf = jnp.concatenate(rows, axis=0)
    return jnp.pad(buf, ((0, (-buf.shape[0]) % 8), (0, 0)))


def _unpack_rows(buf, shapes):
    out, r = [], 0
    for shp in shapes:
        size = int(np.prod(shp))
        nr = -(-size // LANE)
        out.append(buf[r:r + nr].reshape(-1)[:size].reshape(shp))
        r += nr
    return out


def _pad_lanes(v):
    return jnp.pad(v, ((0, 0), (0, LANE - v.shape[1])))


def kernel(x, norm_mix_pre, w_in, conv_qkv_w, a_log, dt_bias, gdn_norm_w, conv_sc_w, w_out, norm_mix_post, norm_mlp_pre, w_up, w_down, norm_mlp_post, loss_target, m_norm_mix_pre, m_w_in, m_conv_qkv_w, m_a_log, m_dt_bias, m_gdn_norm_w, m_conv_sc_w, m_w_out, m_norm_mix_post, m_norm_mlp_pre, m_w_up, m_w_down, m_norm_mlp_post, v_norm_mix_pre, v_w_in, v_conv_qkv_w, v_a_log, v_dt_bias, v_gdn_norm_w, v_conv_sc_w, v_w_out, v_norm_mix_post, v_norm_mlp_pre, v_w_up, v_w_down, v_norm_mlp_post):
    bsz, seq, d = x.shape
    t = bsz * seq
    heads, hd = a_log.shape[-1], gdn_norm_w.shape[-1]
    gw = heads * hd
    scw = conv_sc_w.shape[-1] * N_DEV
    dff_w = w_up.shape[-1] * N_DEV
    lay = _InLayout(w_in.shape[-1], gw, heads, scw)
    mx, my, mc = lax.axis_index("x"), lax.axis_index("y"), lax.axis_index("c")
    me = 4 * mx + 2 * my + mc
    chip = 2 * mx + my

    x2 = x.reshape(t, d)
    tgt = loss_target.reshape(t, d)
    g1, g2, g3, g4 = norm_mix_pre, norm_mix_post, norm_mlp_pre, norm_mlp_post

    frame = lay.to_frame(w_in[0], me)
    (g_in,) = _allgather_big([frame], "allgather_w_in")
    w_up_b, w_down_b = w_up[0].astype(BF16), w_down[0].astype(BF16)
    up_cols = dff_w // N_DEV
    qu, qd = d // 4, up_cols // 4
    w_pad, (g_out,) = _assemble(g_in, jnp.asarray(lay.table), lay.wp // MXU, MXU, "assemble_w_in",
                                comms=[_ag_first(w_out[0].astype(BF16))])
    kq, ks = conv_qkv_w.shape[1], conv_sc_w.shape[1]
    cq_n, cs_n = conv_qkv_w.shape[-1], conv_sc_w.shape[-1]
    cq_full = lax.dynamic_update_slice(jnp.zeros((kq, 3 * gw), F32), conv_qkv_w[0], (0, me * cq_n))
    cs_full = lax.dynamic_update_slice(jnp.zeros((ks, scw), F32), conv_sc_w[0], (0, me * cs_n))
    conv_q, conv_s = _unpack_rows(_allreduce_small(_pack_rows([cq_full, cs_full]), "allgather_conv"),
                                  [(kq, 3 * gw), (ks, scw)])
    alog_t, dtb_t = _pad_lanes(a_log), _pad_lanes(dt_bias)

    xn = _rms_fwd(x2, g1)
    (proj,), (g_up,) = _matmul(xn, w_pad, mode="nn", out_dtypes=[F32], name="in_proj", tn=768,
                               comms=[_ag_first(w_up_b, rows=(0, 3 * qu))])
    qkv, (g_out, g_up) = _qkvconv_fwd(proj, conv_q, bsz, seq, gw,
                                      comms=[_ag_second(g_out), _ag_first(w_up_b, rows=(3 * qu, 4 * qu), into=g_up)])
    (gdn_out, ssave, tsave), (g_up, g_down) = _gdn_fwd(
        qkv, proj, alog_t, dtb_t, gdn_norm_w, bsz, seq, heads, 3 * gw, lay.ab_col,
        comms=[_ag_second(g_up), _ag_first(w_down_b, rows=(0, 2 * qd))])
    sc_out = _sc_fwd(proj, conv_s, bsz, seq, scw, lay.sc_col)
    mixed = jnp.concatenate([gdn_out, sc_out], axis=1)
    w_out_f = g_out.reshape(d, d)
    (mix,), (g_down,) = _matmul(mixed, w_out_f, mode="nn", out_dtypes=[F32], name="out_proj",
                                comms=[_ag_first(w_down_b, rows=(2 * qd, 3 * qd), into=g_down)])
    (h, xn2), (g_down,) = _post1(x2, mix, g2, g3, comms=[_ag_first(w_down_b, rows=(3 * qd, 4 * qd), into=g_down)])

    def up_epilogue(acc):
        r = jnp.maximum(acc, 0.0)
        return r, r * r

    (act, hid), (g_down,) = _matmul(
        xn2, g_up, mode="nn", out_dtypes=[BF16, BF16], name="mlp_up", n_cols=dff_w, epilogue=up_epilogue,
        b_spec=lambda tk, tn: pl.BlockSpec((1, tk, tn), lambda i, j, k: (j // (up_cols // tn), k, j % (up_cols // tn))),
        comms=[_ag_second(g_down)])
    w_down_f = g_down.reshape(dff_w, d)
    (ff,) = _matmul(hid, w_down_f, mode="nn", out_dtypes=[F32], name="mlp_down", tn=1024, tk=2048)
    dff, dy, dg4, loss_p = _post2_loss(h, ff, g4, tgt)

    def pieces(pair, got):
        return [(pair, chip), (got, 0), (got, 1), (got, 2)]

    (dpre,) = _matmul(dff, w_down_f, mode="nt", out_dtypes=[BF16], name="d_hidden", extras=[act],
                      epilogue=lambda acc, a: (acc * (2.0 * a.astype(F32)),))
    (dw_down,) = _matmul(hid, dff, mode="tn", out_dtypes=[BF16], name="dw_down")
    dw_down = dw_down.reshape(N_DEV, dff_w // N_DEV, d)
    (dxn2,), (sib_down,) = _matmul(
        dpre, g_up, mode="nt", out_dtypes=[F32], name="d_xn2", n_cols=d, tn=1024, tk=min(up_cols, 2048),
        b_spec=lambda tk, tn: pl.BlockSpec((1, tn, tk), lambda i, j, k: (k // (up_cols // tk), j, k % (up_cols // tk))),
        comms=[_rs_first(dw_down)])
    pair_down = _pair_add(dw_down, sib_down, mc, "pair_add_down")
    (dw_up,), (got_down,) = _matmul(
        xn2, dpre, mode="tn", out_dtypes=[BF16], name="dw_up",
        out_custom=lambda tm, tn: ((N_DEV, d, up_cols), (1, tm, tn), lambda i, j, k: (j // (up_cols // tn), i, j % (up_cols // tn))),
        comms=[_rs_second(pair_down, rows=(0, 3 * qd))])
    (dmix, dh, dg2, dg3), (got_down, sib_up) = _mid_bwd(
        h, mix, dy, dxn2, g2, g3, comms=[_rs_second(pair_down, rows=(3 * qd, 4 * qd), into=got_down), _rs_first(dw_up)])
    pair_up = _pair_add(dw_up, sib_up, mc, "pair_add_up")
    (dmixed,) = _matmul(dmix, w_out_f, mode="nt", out_dtypes=[F32], name="d_mixed")
    (dw_out,), (got_up,) = _matmul(mixed, dmix, mode="tn", out_dtypes=[BF16], name="dw_out",
                                   comms=[_rs_second(pair_up, rows=(0, qu))])
    dw_out = dw_out.reshape(N_DEV, d // N_DEV, d)
    (dscb, dscc, dsch, dconv_s), (sib_out,) = _sc_bwd(proj, conv_s, dmixed, bsz, seq, scw, lay.sc_col, gw,
                                                      comms=[_rs_first(dw_out)])
    pair_out = _pair_add(dw_out, sib_out, mc, "pair_add_out")
    (dact, dz, dab, dalog, ddtb, dgnw), (got_up,) = _gdn_bwd(
        qkv, proj, alog_t, dtb_t, gdn_norm_w, ssave, tsave, dmixed, bsz, seq, heads, 3 * gw, lay.ab_col,
        comms=[_rs_second(pair_up, rows=(qu, 3 * qu), into=got_up)])
    (dqkv, dconv_q), (got_up,) = _qkvconv_bwd(proj, conv_q, dact, bsz, seq, gw,
                                              comms=[_rs_second(pair_up, rows=(3 * qu, 4 * qu), into=got_up)])
    dproj = jnp.concatenate([dqkv, dz, dab, dscb, dscc, dsch, jnp.zeros((t, lay.wp - lay.used), BF16)], axis=1)
    nfb = lay.fw // MXU
    (dw_in,), (got_out,) = _matmul(
        xn, dproj, mode="tn", out_dtypes=[BF16], name="dw_in", n_cols=N_DEV * lay.fw, tn=MXU,
        b_spec=lambda tk, tn: pl.BlockSpec((tk, tn), lambda i, j, k: (k, lay.frame_block(j // nfb) + j % nfb)),
        out_custom=lambda tm, tn: ((N_DEV, d, lay.fw), (1, tm, tn), lambda i, j, k: (j // nfb, i, j % nfb)),
        comms=[_rs_second(pair_out)])
    half = t // 2
    tk_in = _tile(lay.wp, 2560)
    (dxn_a,), (sib_in,) = _matmul(dproj, w_pad, mode="nt", out_dtypes=[F32], name="d_xn_a", tn=1024, tk=tk_in,
                                  a_rows=(0, half), comms=[_rs_first(dw_in)])
    pair_in = _pair_add(dw_in, sib_in, mc, "pair_add_in")
    (dxn_b,), (got_in,) = _matmul(dproj, w_pad, mode="nt", out_dtypes=[F32], name="d_xn_b", tn=1024, tk=tk_in,
                                  a_rows=(half, half), comms=[_rs_second(pair_in, rows=(0, 2 * qu))])
    (grad_x, dg1), (got_in,) = _pre_bwd(x2, dh, jnp.concatenate([dxn_a, dxn_b], axis=0), g1,
                                        comms=[_rs_second(pair_in, rows=(2 * qu, 3 * qu), into=got_in)])
    adam_up, (got_in,) = _adamw(w_up[0], m_w_up[0], v_w_up[0], pieces(pair_up, got_up), "adamw_w_up",
                                comms=[_rs_second(pair_in, rows=(3 * qu, 4 * qu), into=got_in)])

    gin_frame = _sum_to(pieces(pair_in, got_in), F32, "grad_w_in_frame")
    big = {
        "w_in": _adamw(w_in[0], m_w_in[0], v_w_in[0], [(lay.from_frame(gin_frame, me), None)], "adamw_w_in"),
        "w_out": _adamw(w_out[0], m_w_out[0], v_w_out[0], pieces(pair_out, got_out), "adamw_w_out"),
        "w_up": adam_up,
        "w_down": _adamw(w_down[0], m_w_down[0], v_w_down[0], pieces(pair_down, got_down), "adamw_w_down"),
    }

    small_shapes = [(kq, 3 * gw), (ks, scw), (1, d), (1, d), (1, d), (1, d), (1, LANE), (1, LANE), (1, hd), (1, LANE)]
    small = _unpack_rows(
        _allreduce_small(_pack_rows([dconv_q, dconv_s, dg1, dg2, dg3, dg4, jnp.sum(dalog, axis=0), jnp.sum(ddtb, axis=0),
                                     jnp.sum(dgnw, axis=0), loss_p]), "allreduce_small"), small_shapes)
    gq, gs, sg1, sg2, sg3, sg4, salog, sdtb, sgnw, sloss = small
    loss = sloss[0, 0]
    small_grads = {
        "norm_mix_pre": sg1, "conv_qkv_w": lax.dynamic_slice(gq, (0, me * cq_n), (kq, cq_n)),
        "a_log": salog[:, :heads], "dt_bias": sdtb[:, :heads], "gdn_norm_w": sgnw,
        "conv_sc_w": lax.dynamic_slice(gs, (0, me * cs_n), (ks, cs_n)),
        "norm_mix_post": sg2, "norm_mlp_pre": sg3, "norm_mlp_post": sg4,
    }
    weights = {"norm_mix_pre": (norm_mix_pre, m_norm_mix_pre, v_norm_mix_pre), "conv_qkv_w": (conv_qkv_w[0], m_conv_qkv_w[0], v_conv_qkv_w[0]),
               "a_log": (a_log, m_a_log, v_a_log), "dt_bias": (dt_bias, m_dt_bias, v_dt_bias),
               "gdn_norm_w": (gdn_norm_w, m_gdn_norm_w, v_gdn_norm_w), "conv_sc_w": (conv_sc_w[0], m_conv_sc_w[0], v_conv_sc_w[0]),
               "norm_mix_post": (norm_mix_post, m_norm_mix_post, v_norm_mix_post),
               "norm_mlp_pre": (norm_mlp_pre, m_norm_mlp_pre, v_norm_mlp_pre),
               "norm_mlp_post": (norm_mlp_post, m_norm_mlp_post, v_norm_mlp_post)}
    res = dict(big)
    for name, (w, m, v) in weights.items():
        res[name] = _adamw(w, m, v, [(small_grads[name], None)], "adamw_" + name)

    order = ["norm_mix_pre", "w_in", "conv_qkv_w", "a_log", "dt_bias", "gdn_norm_w", "conv_sc_w", "w_out", "norm_mix_post",
             "norm_mlp_pre", "w_up", "w_down", "norm_mlp_post"]
    shapes = {"norm_mix_pre": norm_mix_pre.shape, "w_in": w_in.shape, "conv_qkv_w": conv_qkv_w.shape, "a_log": a_log.shape,
              "dt_bias": dt_bias.shape, "gdn_norm_w": gdn_norm_w.shape, "conv_sc_w": conv_sc_w.shape, "w_out": w_out.shape,
              "norm_mix_post": norm_mix_post.shape, "norm_mlp_pre": norm_mlp_pre.shape, "w_up": w_up.shape,
              "w_down": w_down.shape, "norm_mlp_post": norm_mlp_post.shape}
    outs = [loss, grad_x.reshape(bsz, seq, d)]
    for part in range(4):
        outs += [res[nm][part].reshape(shapes[nm]) for nm in order]
    return tuple(outs)
```

```python
import functools
import math

import numpy as np
import jax
import jax.numpy as jnp
from jax import lax
from jax.experimental import pallas as pl
from jax.experimental.pallas import tpu as pltpu

F32 = jnp.float32
BF16 = jnp.bfloat16
HI = lax.Precision.HIGHEST
MESH = pl.DeviceIdType.MESH

N_DEV = 8
LANE = 128
MXU = 256
CHUNK = 64
NORM_EPS = 1e-6
L2_EPS = 1e-6
VMEM_LIMIT = 56 * 1024 * 1024

ADAM_LR = 0.001
ADAM_B1 = 0.9
ADAM_B2 = 0.999
ADAM_EPS = 1e-08
ADAM_WD = 0.01
ADAM_STEP = 10

NN = (((1,), (0,)), ((), ()))
NT = (((1,), (1,)), ((), ()))
TN = (((0,), (0,)), ((), ()))


def _params(*sem):
    return pltpu.CompilerParams(dimension_semantics=sem, vmem_limit_bytes=VMEM_LIMIT)


def _tile(n, want):
    if n <= want:
        return n
    t = (want // LANE) * LANE
    while t > LANE and n % t:
        t -= LANE
    assert n % t == 0, (n, want)
    return t


class _Comm:
    def __init__(self, srcs, lands, plan, n, alias=None):
        self.srcs, self.lands, self.plan, self.n, self.alias = list(srcs), list(lands), plan, n, dict(alias or {})


def _pcall(body, *, grid, in_specs, out_specs, out_shape, operands, name, scratch_shapes=(), semantics=None,
           prefetch=(), comms=(), fill=None):
    n_pf, n_in, n_out, n_scr = len(prefetch), len(in_specs), len(out_specs), len(scratch_shapes)
    srcs = [s for cm in comms for s in cm.srcs]
    lands = [l for cm in comms for l in cm.lands]
    n_src, n_land = len(srcs), len(lands)
    n_copies = sum(cm.n for cm in comms)
    aliases, so, lo = {n_pf + a: b for a, b in (fill or {}).items()}, 0, 0
    for cm in comms:
        for a, b in cm.alias.items():
            aliases[n_pf + n_in + so + a] = n_out + lo + b
        so, lo = so + len(cm.srcs), lo + len(cm.lands)
    any_spec = pl.BlockSpec(memory_space=pl.ANY)

    def wrapped(*refs):
        pf, r = refs[:n_pf], refs[n_pf:]
        ins, csrc = r[:n_in], r[n_in:n_in + n_src]
        outs = r[n_in + n_src:n_in + n_src + n_out]
        cland = r[n_in + n_src + n_out:n_in + n_src + n_out + n_land]
        rest = r[n_in + n_src + n_out + n_land:]
        scratch = rest[:n_scr]
        if not comms:
            body(*pf, *ins, *outs, *scratch)
            return
        send_sems, recv_sems = rest[n_scr:]

        def copies():
            out, k, s0, l0 = [], 0, 0, 0
            for cm in comms:
                for kind, src, dst, dev in cm.plan(csrc[s0:s0 + len(cm.srcs)], cland[l0:l0 + len(cm.lands)]):
                    if kind == "local":
                        out.append((kind, pltpu.make_async_copy(src, dst, send_sems.at[k])))
                    else:
                        out.append((kind, pltpu.make_async_remote_copy(
                            src_ref=src, dst_ref=dst, send_sem=send_sems.at[k], recv_sem=recv_sems.at[k],
                            device_id=dev, device_id_type=MESH)))
                    k += 1
                s0, l0 = s0 + len(cm.srcs), l0 + len(cm.lands)
            assert k == n_copies
            return out

        ids = [pl.program_id(a) for a in range(len(grid))]
        first = functools.reduce(jnp.logical_and, [i == 0 for i in ids])
        last = functools.reduce(jnp.logical_and, [i == g - 1 for i, g in zip(ids, grid)])

        @pl.when(first)
        def _():
            for _, cp in copies():
                cp.start()

        body(*pf, *ins, *outs, *scratch)

        @pl.when(last)
        def _():
            cps = copies()
            for kind, cp in cps:
                if kind == "remote":
                    cp.wait_recv()
            for kind, cp in cps:
                if kind == "remote":
                    cp.wait_send()
                else:
                    cp.wait()

    sems = [pltpu.SemaphoreType.DMA((n_copies,)), pltpu.SemaphoreType.DMA((n_copies,))] if comms else []
    if semantics is None or comms:
        semantics = ("arbitrary",) * len(grid)
    res = pl.pallas_call(
        wrapped,
        grid_spec=pltpu.PrefetchScalarGridSpec(
            num_scalar_prefetch=n_pf, grid=tuple(grid), in_specs=list(in_specs) + [any_spec] * n_src,
            out_specs=list(out_specs) + [any_spec] * n_land, scratch_shapes=list(scratch_shapes) + sems),
        out_shape=list(out_shape) + lands,
        input_output_aliases=aliases,
        compiler_params=_params(*semantics), name=name)(*prefetch, *operands, *srcs)
    return list(res[:n_out]), list(res[n_out:])


def _bdot(a, b, dims=NN):
    return lax.dot_general(a.astype(BF16), b.astype(BF16), dims, preferred_element_type=F32)


def _hdot(a, b, dims=NN):
    return lax.dot_general(a, b, dims, preferred_element_type=F32, precision=HI)


def _mdot(a, b, dims=NN):
    return lax.dot_general(a, b, dims, preferred_element_type=F32, precision=lax.Precision.HIGH)


def _sigmoid(x):
    return 1.0 / (1.0 + jnp.exp(-x))


def _softplus(x):
    return jnp.maximum(x, 0.0) + jnp.log(1.0 + jnp.exp(-jnp.abs(x)))


def _matmul(a, b, *, mode, out_dtypes, name, n_cols=None, tm=1024, tn=512, tk=4096, epilogue=None, extras=(),
            b_spec=None, out_custom=None, a_rows=None, out_into=None, comms=()):
    if mode == "tn":
        K, M = a.shape
    else:
        M, K = a.shape
    r0 = 0
    if a_rows is not None:
        r0, M = a_rows
    N = n_cols if n_cols is not None else (b.shape[0] if mode == "nt" else b.shape[1])
    tm, tk, tn = _tile(M, tm), _tile(K, tk), _tile(N, tn)
    assert r0 % tm == 0
    i0 = r0 // tm
    if b_spec is None:
        b_spec = pl.BlockSpec((tn, tk), lambda i, j, k: (j, k)) if mode == "nt" else pl.BlockSpec((tk, tn), lambda i, j, k: (k, j))
    else:
        b_spec = b_spec(tk, tn)
    gm, gn, nk = M // tm, N // tn, K // tk
    if a_rows is not None:
        out_shapes = [(a.shape[0], N)] * len(out_dtypes)
        out_blocks = [(tm, tn)] * len(out_dtypes)
        out_index = [lambda i, j, k: (i + i0, j)] * len(out_dtypes)
    elif out_custom is None:
        out_shapes = [(M, N)] * len(out_dtypes)
        out_blocks = [(tm, tn)] * len(out_dtypes)
        out_index = [lambda i, j, k: (i, j)] * len(out_dtypes)
    else:
        shape, blk, ix = out_custom(tm, tn)
        out_shapes, out_blocks, out_index = [shape] * len(out_dtypes), [blk] * len(out_dtypes), [ix] * len(out_dtypes)
    a_spec = pl.BlockSpec((tk, tm), lambda i, j, k: (k, i)) if mode == "tn" else pl.BlockSpec((tm, tk), lambda i, j, k: (i + i0, k))
    hoist = mode == "tn" and nk == 1 and gn > 1
    dims = {"nn": NN, "nt": NT, "tn": TN}[mode]
    n_ex, n_out = len(extras), len(out_dtypes)
    n_into = 0 if out_into is None else 1

    def body(a_ref, b_ref, *rest):
        ex, outs = rest[:n_ex], rest[n_ex + n_into:n_ex + n_into + n_out]

        def finish(acc):
            res = epilogue(acc, *[e[...] for e in ex]) if epilogue is not None else (acc,)
            for o, r in zip(outs, res):
                o[...] = r.reshape(o.shape).astype(o.dtype)

        bb = b_ref[...]
        bb = bb.reshape(bb.shape[-2:])
        if hoist:
            at_ref = rest[-1]

            @pl.when(pl.program_id(1) == 0)
            def _():
                at_ref[...] = a_ref[...].T

            finish(lax.dot_general(at_ref[...], bb, NN, preferred_element_type=F32))
            return
        part = lax.dot_general(a_ref[...], bb, dims, preferred_element_type=F32)
        if nk == 1:
            finish(part)
        else:
            acc = rest[-1]
            k = pl.program_id(2)

            @pl.when(k == 0)
            def _():
                acc[...] = part

            @pl.when(k > 0)
            def _():
                acc[...] += part

            @pl.when(k == nk - 1)
            def _():
                finish(acc[...])

    scratch = [pltpu.VMEM((tm, tk), BF16)] if hoist else ([pltpu.VMEM((tm, tn), F32)] if nk > 1 else [])
    outs, lands = _pcall(
        body, grid=(gm, gn, nk),
        in_specs=([a_spec, b_spec] + [pl.BlockSpec((tm, tn), lambda i, j, k: (i, j)) for _ in extras]
                  + [pl.BlockSpec(memory_space=pl.ANY)] * n_into),
        out_specs=[pl.BlockSpec(blk, ix) for blk, ix in zip(out_blocks, out_index)],
        out_shape=[jax.ShapeDtypeStruct(s, d) for s, d in zip(out_shapes, out_dtypes)],
        scratch_shapes=scratch, semantics=("parallel", "arbitrary", "arbitrary"),
        operands=[a, b, *extras] + ([out_into] if n_into else []), name=name, comms=comms,
        fill={2 + n_ex: 0} if n_into else None)
    return (outs, lands) if comms else outs


TR = 128


def _rms(x):
    return lax.rsqrt(jnp.mean(x * x, axis=-1, keepdims=True) + NORM_EPS)


def _rms_bwd(x, r, w, dy):
    u = dy * w
    dx = r * u - x * (r * r * r) * jnp.mean(x * u, axis=-1, keepdims=True)
    return dx, dy * x * r


def _row_call(body, ins, row_flags, outs, name, n_rows, comms=()):
    tr = min(TR, n_rows)
    in_specs = []
    for arr, is_row in zip(ins, row_flags):
        if is_row:
            in_specs.append(pl.BlockSpec((tr, arr.shape[1]), lambda i: (i, 0)))
        else:
            in_specs.append(pl.BlockSpec(arr.shape, lambda i: (0, 0)))
    out_specs, out_shape = [], []
    for shape, dtype, kind in outs:
        if kind == "row":
            out_specs.append(pl.BlockSpec((tr, shape[1]), lambda i: (i, 0)))
        else:
            out_specs.append(pl.BlockSpec(shape, lambda i: (0, 0)))
        out_shape.append(jax.ShapeDtypeStruct(shape, dtype))
    res, lands = _pcall(body, grid=(n_rows // tr,), in_specs=in_specs, out_specs=out_specs, out_shape=out_shape,
                        operands=list(ins), name=name, comms=comms)
    return (res, lands) if comms else res


def _acc_out(ref, val):
    @pl.when(pl.program_id(0) == 0)
    def _():
        ref[...] = val

    @pl.when(pl.program_id(0) > 0)
    def _():
        ref[...] += val


def _rms_fwd(x, g):
    T, D = x.shape

    def body(x_ref, g_ref, o_ref):
        xv = x_ref[...]
        o_ref[...] = (xv * _rms(xv) * g_ref[...]).astype(BF16)

    return _row_call(body, [x, g], [True, False], [((T, D), BF16, "row")], "rms_fwd", T)[0]


def _post1(x, mix, g2, g3, comms=()):
    T, D = x.shape

    def body(x_ref, mix_ref, g2_ref, g3_ref, h_ref, xn2_ref):
        mv = mix_ref[...]
        h = x_ref[...] + mv * _rms(mv) * g2_ref[...]
        h_ref[...] = h
        xn2_ref[...] = (h * _rms(h) * g3_ref[...]).astype(BF16)

    return _row_call(body, [x, mix, g2, g3], [True, True, False, False],
                     [((T, D), F32, "row"), ((T, D), BF16, "row")], "post1", T, comms=comms)


def _post2_loss(h, ff, g4, target):
    T, D = h.shape

    def body(h_ref, ff_ref, g4_ref, t_ref, dff_ref, dy_ref, dg4_ref, loss_ref):
        fv = ff_ref[...]
        r = _rms(fv)
        err = h_ref[...] + fv * r * g4_ref[...] - t_ref[...]
        dy = err * (1.0 / D)
        dy_ref[...] = dy
        dff, dwt = _rms_bwd(fv, r, g4_ref[...], dy)
        dff_ref[...] = dff.astype(BF16)
        _acc_out(dg4_ref, jnp.sum(dwt, axis=0, keepdims=True))
        part = 0.5 * jnp.sum(jnp.mean(err * err, axis=-1, keepdims=True), axis=0, keepdims=True)
        _acc_out(loss_ref, jnp.broadcast_to(part, (1, LANE)))

    return _row_call(body, [h, ff, g4, target], [True, True, False, True],
                     [((T, D), BF16, "row"), ((T, D), F32, "row"), ((1, D), F32, "acc"), ((1, LANE), F32, "acc")],
                     "post2_loss", T)


def _mid_bwd(h, mix, dy, dxn2, g2, g3, comms=()):
    T, D = h.shape

    def body(h_ref, mix_ref, dy_ref, dxn2_ref, g2_ref, g3_ref, dmix_ref, dh_ref, dg2_ref, dg3_ref):
        hv = h_ref[...]
        d1, dw3 = _rms_bwd(hv, _rms(hv), g3_ref[...], dxn2_ref[...])
        dh = dy_ref[...] + d1
        dh_ref[...] = dh
        mv = mix_ref[...]
        dmix, dw2 = _rms_bwd(mv, _rms(mv), g2_ref[...], dh)
        dmix_ref[...] = dmix.astype(BF16)
        _acc_out(dg2_ref, jnp.sum(dw2, axis=0, keepdims=True))
        _acc_out(dg3_ref, jnp.sum(dw3, axis=0, keepdims=True))

    return _row_call(body, [h, mix, dy, dxn2, g2, g3], [True, True, True, True, False, False],
                     [((T, D), BF16, "row"), ((T, D), F32, "row"), ((1, D), F32, "acc"), ((1, D), F32, "acc")],
                     "mid_bwd", T, comms=comms)


def _pre_bwd(x, dh, dxn, g1, comms=()):
    T, D = x.shape

    def body(x_ref, dh_ref, dxn_ref, g1_ref, gx_ref, dg1_ref):
        xv = x_ref[...]
        d1, dw1 = _rms_bwd(xv, _rms(xv), g1_ref[...], dxn_ref[...])
        gx_ref[...] = dh_ref[...] + d1
        _acc_out(dg1_ref, jnp.sum(dw1, axis=0, keepdims=True))

    return _row_call(body, [x, dh, dxn, g1], [True, True, True, False],
                     [((T, D), F32, "row"), ((1, D), F32, "acc")], "pre_bwd", T, comms=comms)


def _shift_down(x, s):
    if s == 0:
        return x
    row = lax.broadcasted_iota(jnp.int32, x.shape, 0)
    return jnp.where(row >= s, pltpu.roll(x, s, axis=0), 0.0)


def _shift_up(x, s):
    if s == 0:
        return x
    n = x.shape[0]
    row = lax.broadcasted_iota(jnp.int32, x.shape, 0)
    return jnp.where(row < n - s, pltpu.roll(x, n - s, axis=0), 0.0)


def _conv(x, w):
    kw = w.shape[0]
    out = w[kw - 1:kw, :] * x
    for j in range(kw - 1):
        out = out + w[j:j + 1, :] * _shift_down(x, kw - 1 - j)
    return out


def _conv_bwd(x, w, dout):
    kw = w.shape[0]
    dx = w[kw - 1:kw, :] * dout
    dws = []
    for j in range(kw - 1):
        dx = dx + w[j:j + 1, :] * _shift_up(dout, kw - 1 - j)
        dws.append(jnp.sum(dout * _shift_down(x, kw - 1 - j), axis=0, keepdims=True))
    dws.append(jnp.sum(dout * x, axis=0, keepdims=True))
    return dx, jnp.concatenate(dws, axis=0)


def _qkvconv_fwd(proj, w, bsz, seq, gw, comms=()):
    nct = gw // LANE
    kw = w.shape[0]

    def body(p_ref, w_ref, o_ref):
        cv = _conv(p_ref[...], w_ref[...])
        o_ref[...] = (cv * _sigmoid(cv)).reshape(o_ref.shape)

    res, lands = _pcall(
        body, grid=(3, bsz, nct),
        in_specs=[pl.BlockSpec((seq, LANE), lambda p, b, c: (b, p * nct + c)),
                  pl.BlockSpec((kw, LANE), lambda p, b, c: (0, p * nct + c))],
        out_specs=[pl.BlockSpec((1, seq, LANE), lambda p, b, c: (p, b, c))],
        out_shape=[jax.ShapeDtypeStruct((3, bsz * seq, gw), F32)],
        semantics=("parallel", "parallel", "parallel"), operands=[proj, w], name="qkvconv_fwd", comms=comms)
    return res[0], lands


def _qkvconv_bwd(proj, w, dact, bsz, seq, gw, comms=()):
    nct = gw // LANE
    kw = w.shape[0]

    def body(p_ref, w_ref, d_ref, dp_ref, dw_ref):
        pre = p_ref[...]
        wv = w_ref[...]
        cv = _conv(pre, wv)
        sg = _sigmoid(cv)
        dcv = d_ref[...].reshape(cv.shape) * (sg * (1.0 + cv * (1.0 - sg)))
        dpre, dw = _conv_bwd(pre, wv, dcv)
        dp_ref[...] = dpre.astype(BF16)
        b = pl.program_id(2)

        @pl.when(b == 0)
        def _():
            dw_ref[...] = dw

        @pl.when(b > 0)
        def _():
            dw_ref[...] += dw

    res, lands = _pcall(
        body, grid=(3, nct, bsz),
        in_specs=[pl.BlockSpec((seq, LANE), lambda p, c, b: (b, p * nct + c)),
                  pl.BlockSpec((kw, LANE), lambda p, c, b: (0, p * nct + c)),
                  pl.BlockSpec((1, seq, LANE), lambda p, c, b: (p, b, c))],
        out_specs=[pl.BlockSpec((seq, LANE), lambda p, c, b: (b, p * nct + c)),
                   pl.BlockSpec((kw, LANE), lambda p, c, b: (0, p * nct + c))],
        out_shape=[jax.ShapeDtypeStruct((bsz * seq, 3 * gw), BF16), jax.ShapeDtypeStruct((kw, 3 * gw), F32)],
        semantics=("parallel", "parallel", "arbitrary"), operands=[proj, w, dact], name="qkvconv_bwd", comms=comms)
    return res, lands


def _sc_fwd(proj, w, bsz, seq, scw, col0):
    nct = scw // LANE
    c0 = col0 // LANE
    kw = w.shape[0]

    def body(b_ref, c_ref, h_ref, w_ref, o_ref):
        o_ref[...] = (b_ref[...] * _conv(c_ref[...] * h_ref[...], w_ref[...])).astype(BF16)

    return pl.pallas_call(
        body, grid=(bsz, nct),
        in_specs=[pl.BlockSpec((seq, LANE), lambda b, c: (b, c0 + c)),
                  pl.BlockSpec((seq, LANE), lambda b, c: (b, c0 + nct + c)),
                  pl.BlockSpec((seq, LANE), lambda b, c: (b, c0 + 2 * nct + c)),
                  pl.BlockSpec((kw, LANE), lambda b, c: (0, c))],
        out_specs=pl.BlockSpec((seq, LANE), lambda b, c: (b, c)),
        out_shape=jax.ShapeDtypeStruct((bsz * seq, scw), BF16),
        compiler_params=_params("parallel", "parallel"), name="sc_fwd")(proj, proj, proj, w)


def _sc_bwd(proj, w, dout, bsz, seq, scw, col0, dcol0, comms=()):
    nct = scw // LANE
    c0 = col0 // LANE
    d0 = dcol0 // LANE
    kw = w.shape[0]

    def body(b_ref, c_ref, h_ref, w_ref, d_ref, db_ref, dc_ref, dh_ref, dw_ref):
        cc, hh, wv, dv = c_ref[...], h_ref[...], w_ref[...], d_ref[...]
        m = cc * hh
        db_ref[...] = (dv * _conv(m, wv)).astype(BF16)
        dm, dw = _conv_bwd(m, wv, dv * b_ref[...])
        dc_ref[...] = (dm * hh).astype(BF16)
        dh_ref[...] = (dm * cc).astype(BF16)
        b = pl.program_id(1)

        @pl.when(b == 0)
        def _():
            dw_ref[...] = dw

        @pl.when(b > 0)
        def _():
            dw_ref[...] += dw

    res, lands = _pcall(
        body, grid=(nct, bsz),
        in_specs=[pl.BlockSpec((seq, LANE), lambda c, b: (b, c0 + c)),
                  pl.BlockSpec((seq, LANE), lambda c, b: (b, c0 + nct + c)),
                  pl.BlockSpec((seq, LANE), lambda c, b: (b, c0 + 2 * nct + c)),
                  pl.BlockSpec((kw, LANE), lambda c, b: (0, c)),
                  pl.BlockSpec((seq, LANE), lambda c, b: (b, d0 + c))],
        out_specs=[pl.BlockSpec((seq, LANE), lambda c, b: (b, c)),
                   pl.BlockSpec((seq, LANE), lambda c, b: (b, c)),
                   pl.BlockSpec((seq, LANE), lambda c, b: (b, c)),
                   pl.BlockSpec((kw, LANE), lambda c, b: (0, c))],
        out_shape=[jax.ShapeDtypeStruct((bsz * seq, scw), BF16)] * 3 + [jax.ShapeDtypeStruct((kw, scw), F32)],
        semantics=("parallel", "arbitrary"), operands=[proj, proj, proj, w, dout], name="sc_bwd", comms=comms)
    return res, lands


HEADS_PER_STEP = 16


def _colsel(tile, idx):
    lane = lax.broadcasted_iota(jnp.int32, tile.shape, 1)
    return jnp.sum(jnp.where(lane == idx, tile, 0.0), axis=1, keepdims=True)


def _rowsel(tile, idx):
    row = lax.broadcasted_iota(jnp.int32, tile.shape, 0)
    return jnp.sum(jnp.where(row == idx, tile, 0.0), axis=0, keepdims=True)


def _colput(col, idx, width=LANE):
    lane = lax.broadcasted_iota(jnp.int32, (col.shape[0], width), 1)
    return jnp.where(lane == idx, col, 0.0)


def _tri_masks(c):
    row = lax.broadcasted_iota(jnp.int32, (c, c), 0)
    col = lax.broadcasted_iota(jnp.int32, (c, c), 1)
    return row >= col, row > col, row == col


def _unit_lower_inverses(ms):
    c = ms[0].shape[0]
    _, _, eye = _tri_masks(c)
    ps = [-m for m in ms]
    ts = [jnp.where(eye, 1.0, 0.0) + p for p in ps]
    for _ in range(int(math.log2(c)) - 1):
        ps = [_mdot(p, p) for p in ps]
        ts = [t + _mdot(t, p) for t, p in zip(ts, ps)]
    return ts


def _gates(ab, alog, dtb):
    g = -jnp.exp(alog) * _softplus(ab + dtb)
    return g, _sigmoid(ab)


def _l2n(x):
    r = lax.rsqrt(jnp.sum(x * x, axis=-1, keepdims=True) + L2_EPS)
    return x * r, r


def _gdn_chunk_common(q, k, gc, gr, bc):
    c, dk = q.shape
    incl, strict, _ = _tri_masks(c)
    qh, rq = _l2n(q)
    kn, rk = _l2n(k)
    qn = qh * (dk ** -0.5)
    dm = jnp.where(incl, jnp.exp(jnp.where(incl, gc - gr, 0.0)), 0.0)
    kk = _bdot(kn, kn, NT)
    m = jnp.where(strict, bc * kk * dm, 0.0)
    pm = jnp.where(incl, _bdot(qn, kn, NT) * dm, 0.0)
    return qh, rq, kn, rk, qn, dm, kk, m, pm


def _gdn_fwd(qkv, proj, alog, dtb, gnw, bsz, seq, heads, z_col0, ab_col0, comms=()):
    c = CHUNK
    nch = seq // c
    hb = min(HEADS_PER_STEP, heads)
    ng = heads // hb
    hd = qkv.shape[2] // heads
    wb = hb * hd

    def body(qkv_ref, z_ref, ab_ref, alog_ref, dtb_ref, gnw_ref, o_ref, ssave_ref, tsave_ref, s_ref, gc_ref, gt_ref, be_ref):
        n, hg = pl.program_id(1), pl.program_id(2)

        @pl.when((n == 0) & (hg == 0))
        def _():
            s_ref[...] = jnp.zeros_like(s_ref)

        @pl.when(hg == 0)
        def _():
            g, beta = _gates(ab_ref[...], alog_ref[...], dtb_ref[...])
            incl, _, _ = _tri_masks(c)
            gcum = _hdot(jnp.where(incl, 1.0, 0.0), g)
            gc_ref[...] = gcum
            gt_ref[...] = gcum.T
            be_ref[...] = beta

        gc_t, gt_t, be_t, gnw_v = gc_ref[...], gt_ref[...], be_ref[...], gnw_ref[...]
        hs = range(hb)
        sls = [slice(hh * hd, (hh + 1) * hd) for hh in hs]
        states = [s_ref[hg * hb + hh] for hh in hs]
        gcs = [_colsel(gc_t, hg * hb + hh) for hh in hs]
        grs = [_rowsel(gt_t, hg * hb + hh) for hh in hs]
        bcs = [_colsel(be_t, heads + hg * hb + hh) for hh in hs]
        com = [_gdn_chunk_common(qkv_ref[0, :, sls[hh]], qkv_ref[1, :, sls[hh]], gcs[hh], grs[hh], bcs[hh]) for hh in hs]
        kns, qns, pms = [cm[2] for cm in com], [cm[4] for cm in com], [cm[8] for cm in com]
        tms = _unit_lower_inverses([cm[7] for cm in com])
        gams = [jnp.exp(gc) for gc in gcs]
        glasts = [gc[c - 1:c, :] for gc in gcs]
        kss = [_bdot(kns[hh], states[hh]) for hh in hs]
        qss = [_bdot(qns[hh], states[hh]) for hh in hs]
        vns = [_bdot(tms[hh], bcs[hh] * (qkv_ref[2, :, sls[hh]] - gams[hh] * kss[hh])) for hh in hs]
        os_ = [gams[hh] * qss[hh] + _bdot(pms[hh], vns[hh]) for hh in hs]
        snews = [states[hh] * jnp.exp(glasts[hh]) + _bdot(kns[hh] * jnp.exp(glasts[hh] - gcs[hh]), vns[hh], TN) for hh in hs]
        for hh in hs:
            o = os_[hh]
            on = o * lax.rsqrt(jnp.mean(o * o, axis=-1, keepdims=True) + NORM_EPS) * gnw_v
            zz = z_ref[:, sls[hh]]
            ssave_ref[0, 0, hh] = states[hh]
            tsave_ref[0, 0, hh] = tms[hh]
            s_ref[hg * hb + hh] = snews[hh]
            o_ref[:, sls[hh]] = (on * (zz * _sigmoid(zz))).astype(BF16)

    row = lambda b, n, g: b * nch + n
    return _pcall(
        body, grid=(bsz, nch, ng),
        in_specs=[pl.BlockSpec((3, c, wb), lambda b, n, g: (0, row(b, n, g), g)),
                  pl.BlockSpec((c, wb), lambda b, n, g: (row(b, n, g), z_col0 // wb + g)),
                  pl.BlockSpec((c, LANE), lambda b, n, g: (row(b, n, g), ab_col0 // LANE)),
                  pl.BlockSpec((1, LANE), lambda b, n, g: (0, 0)),
                  pl.BlockSpec((1, LANE), lambda b, n, g: (0, 0)),
                  pl.BlockSpec((1, hd), lambda b, n, g: (0, 0))],
        out_specs=[pl.BlockSpec((c, wb), lambda b, n, g: (row(b, n, g), g)),
                   pl.BlockSpec((1, 1, hb, hd, hd), lambda b, n, g: (b, n, g, 0, 0)),
                   pl.BlockSpec((1, 1, hb, c, c), lambda b, n, g: (b, n, g, 0, 0))],
        out_shape=[jax.ShapeDtypeStruct((bsz * seq, heads * hd), BF16),
                   jax.ShapeDtypeStruct((bsz, nch, heads, hd, hd), F32),
                   jax.ShapeDtypeStruct((bsz, nch, heads, c, c), F32)],
        scratch_shapes=[pltpu.VMEM((heads, hd, hd), F32), pltpu.VMEM((c, LANE), F32), pltpu.VMEM((LANE, c), F32),
                        pltpu.VMEM((c, LANE), F32)],
        semantics=("parallel", "arbitrary", "arbitrary"), operands=[qkv, proj, proj, alog, dtb, gnw], name="gdn_fwd",
        comms=comms)


def _gdn_bwd(qkv, proj, alog, dtb, gnw, ssave, tsave, dout, bsz, seq, heads, z_col0, ab_col0, comms=()):
    c = CHUNK
    nch = seq // c
    hb = min(HEADS_PER_STEP, heads)
    ng = heads // hb
    hd = qkv.shape[2] // heads
    wb = hb * hd

    def body(qkv_ref, z_ref, ab_ref, alog_ref, dtb_ref, gnw_ref, ssave_ref, tsave_ref, do_ref,
             dact_ref, dz_ref, dab_ref, dalog_ref, ddtb_ref, dgnw_ref,
             ds_ref, gc_ref, gt_ref, be_ref, dgacc_ref, dbacc_ref):
        n, hg = pl.program_id(1), pl.program_id(2)
        incl, strict, _ = _tri_masks(c)

        @pl.when((n == 0) & (hg == 0))
        def _():
            ds_ref[...] = jnp.zeros_like(ds_ref)
            dalog_ref[...] = jnp.zeros_like(dalog_ref)
            ddtb_ref[...] = jnp.zeros_like(ddtb_ref)
            dgnw_ref[...] = jnp.zeros_like(dgnw_ref)

        @pl.when(hg == 0)
        def _():
            g, beta = _gates(ab_ref[...], alog_ref[...], dtb_ref[...])
            gcum = _hdot(jnp.where(incl, 1.0, 0.0), g)
            gc_ref[...] = gcum
            gt_ref[...] = gcum.T
            be_ref[...] = beta
            dgacc_ref[...] = jnp.zeros_like(dgacc_ref)
            dbacc_ref[...] = jnp.zeros_like(dbacc_ref)

        gc_t, gt_t, be_t, gnw_v = gc_ref[...], gt_ref[...], be_ref[...], gnw_ref[...]
        hs = range(hb)

        def each(f):
            return [f(hh) for hh in hs]

        rsum = lambda a: jnp.sum(a, axis=-1, keepdims=True)
        sls = each(lambda i: slice(i * hd, (i + 1) * hd))
        ds_in = each(lambda i: ds_ref[hg * hb + i])
        gc = each(lambda i: _colsel(gc_t, hg * hb + i))
        gr = each(lambda i: _rowsel(gt_t, hg * hb + i))
        bc = each(lambda i: _colsel(be_t, heads + hg * hb + i))
        com = each(lambda i: _gdn_chunk_common(qkv_ref[0, :, sls[i]], qkv_ref[1, :, sls[i]], gc[i], gr[i], bc[i]))
        qh, rq, kn, rk, qn, dm, kk, m, pm = [[cm[j] for cm in com] for j in range(9)]
        tm = each(lambda i: tsave_ref[0, 0, i])
        s = each(lambda i: ssave_ref[0, 0, i])
        gam = each(lambda i: jnp.exp(gc[i]))
        glast = each(lambda i: gc[i][c - 1:c, :])
        gl = each(lambda i: jnp.exp(glast[i]))
        ratio = each(lambda i: jnp.exp(glast[i] - gc[i]))
        ks = each(lambda i: _bdot(kn[i], s[i]))
        qs = each(lambda i: _bdot(qn[i], s[i]))
        r = each(lambda i: qkv_ref[2, :, sls[i]] - gam[i] * ks[i])
        vn = each(lambda i: _bdot(tm[i], bc[i] * r[i]))
        o = each(lambda i: gam[i] * qs[i] + _bdot(pm[i], vn[i]))
        ro = each(lambda i: lax.rsqrt(jnp.mean(o[i] * o[i], axis=-1, keepdims=True) + NORM_EPS))
        zz = each(lambda i: z_ref[:, sls[i]])
        sz = each(lambda i: _sigmoid(zz[i]))
        dd = each(lambda i: do_ref[:, sls[i]])
        don = each(lambda i: dd[i] * (zz[i] * sz[i]))
        dz_h = each(lambda i: (dd[i] * (o[i] * ro[i] * gnw_v) * (sz[i] * (1.0 + zz[i] * (1.0 - sz[i])))).astype(BF16))
        dgnw = sum(each(lambda i: jnp.sum(don[i] * o[i] * ro[i], axis=0, keepdims=True)))
        uu = each(lambda i: don[i] * gnw_v)
        d_o = each(lambda i: ro[i] * uu[i] - o[i] * (ro[i] * ro[i] * ro[i]) * jnp.mean(o[i] * uu[i], axis=-1, keepdims=True))
        dqs = each(lambda i: gam[i] * d_o[i])
        dq = each(lambda i: _bdot(dqs[i], s[i], NT))
        ds_new = each(lambda i: _bdot(qn[i], dqs[i], TN))
        dp = each(lambda i: jnp.where(incl, _bdot(d_o[i], vn[i], NT), 0.0))
        dvn = each(lambda i: _bdot(pm[i], d_o[i], TN))
        dgam = each(lambda i: rsum(d_o[i] * qs[i]))
        dkd = each(lambda i: _bdot(vn[i], ds_in[i], NT))
        dvn = each(lambda i: dvn[i] + _bdot(kn[i] * ratio[i], ds_in[i]))
        ds_new = each(lambda i: ds_new[i] + gl[i] * ds_in[i])
        dgl = each(lambda i: jnp.sum(jnp.sum(ds_in[i] * s[i], axis=1, keepdims=True), axis=0, keepdims=True))
        dratio = each(lambda i: rsum(dkd[i] * kn[i]))
        dpd = each(lambda i: dp[i] * dm[i])
        dq = each(lambda i: dq[i] + _bdot(dpd[i], kn[i]))
        dk = each(lambda i: ratio[i] * dkd[i] + _bdot(dpd[i], qn[i], TN))
        dx = each(lambda i: _bdot(tm[i], dvn[i], TN))
        dr = each(lambda i: bc[i] * dx[i])
        gdr = each(lambda i: gam[i] * dr[i])
        dk = each(lambda i: dk[i] - _bdot(gdr[i], s[i], NT))
        ds_new = each(lambda i: ds_new[i] - _bdot(kn[i], gdr[i], TN))
        dmm = each(lambda i: jnp.where(strict, -_bdot(dx[i], vn[i], NT), 0.0))
        ee = each(lambda i: dmm[i] * dm[i])
        be_e = each(lambda i: bc[i] * ee[i])
        dk = each(lambda i: dk[i] + _bdot(be_e[i], kn[i]) + _bdot(be_e[i], kn[i], TN))
        dbeta = each(lambda i: rsum(dx[i] * r[i]) + rsum(ee[i] * kk[i]))
        dgam = each(lambda i: dgam[i] - rsum(dr[i] * ks[i]))
        ff = each(lambda i: dp[i] * pm[i] + dmm[i] * m[i])
        rowi = lax.broadcasted_iota(jnp.int32, (c, 1), 0)
        dgc = each(lambda i: rsum(ff[i]) - rsum(ff[i].T) + dgam[i] * gam[i] - dratio[i] * ratio[i]
                   + jnp.where(rowi == c - 1, jnp.sum(dratio[i] * ratio[i], axis=0, keepdims=True) + dgl[i] * gl[i], 0.0))
        dg_tile = sum(each(lambda i: _colput(dgc[i], hg * hb + i)))
        db_tile = sum(each(lambda i: _colput(dbeta[i], heads + hg * hb + i)))
        for i in hs:
            dqh = dq[i] * (hd ** -0.5)
            ds_ref[hg * hb + i] = ds_new[i]
            dz_ref[:, sls[i]] = dz_h[i]
            dact_ref[0, :, sls[i]] = rq[i] * (dqh - qh[i] * rsum(qh[i] * dqh))
            dact_ref[1, :, sls[i]] = rk[i] * (dk[i] - kn[i] * rsum(kn[i] * dk[i]))
            dact_ref[2, :, sls[i]] = dr[i]
        dgacc_ref[...] += dg_tile
        dbacc_ref[...] += db_tile
        dgnw_ref[0] += dgnw

        @pl.when(hg == ng - 1)
        def _():
            ab = ab_ref[...]
            ea = jnp.exp(alog_ref[...])
            g = -ea * _softplus(ab + dtb_ref[...])
            beta = be_ref[...]
            dg = _hdot(jnp.where(incl, 1.0, 0.0), dgacc_ref[...], TN)
            lane = lax.broadcasted_iota(jnp.int32, ab.shape, 1)
            da = jnp.where(lane < heads, dg * (-ea) * _sigmoid(ab + dtb_ref[...]), 0.0)
            db = dbacc_ref[...] * beta * (1.0 - beta)
            dab_ref[...] = (da + db).astype(BF16)
            dalog_ref[0] += jnp.sum(jnp.where(lane < heads, dg * g, 0.0), axis=0, keepdims=True)
            ddtb_ref[0] += jnp.sum(da, axis=0, keepdims=True)

    row = lambda b, n, g: b * nch + (nch - 1 - n)
    rev = lambda n: nch - 1 - n
    return _pcall(
        body, grid=(bsz, nch, ng),
        in_specs=[pl.BlockSpec((3, c, wb), lambda b, n, g: (0, row(b, n, g), g)),
                  pl.BlockSpec((c, wb), lambda b, n, g: (row(b, n, g), z_col0 // wb + g)),
                  pl.BlockSpec((c, LANE), lambda b, n, g: (row(b, n, g), ab_col0 // LANE)),
                  pl.BlockSpec((1, LANE), lambda b, n, g: (0, 0)),
                  pl.BlockSpec((1, LANE), lambda b, n, g: (0, 0)),
                  pl.BlockSpec((1, hd), lambda b, n, g: (0, 0)),
                  pl.BlockSpec((1, 1, hb, hd, hd), lambda b, n, g: (b, rev(n), g, 0, 0)),
                  pl.BlockSpec((1, 1, hb, c, c), lambda b, n, g: (b, rev(n), g, 0, 0)),
                  pl.BlockSpec((c, wb), lambda b, n, g: (row(b, n, g), g))],
        out_specs=[pl.BlockSpec((3, c, wb), lambda b, n, g: (0, row(b, n, g), g)),
                   pl.BlockSpec((c, wb), lambda b, n, g: (row(b, n, g), g)),
                   pl.BlockSpec((c, LANE), lambda b, n, g: (row(b, n, g), 0)),
                   pl.BlockSpec((1, 1, LANE), lambda b, n, g: (b, 0, 0)),
                   pl.BlockSpec((1, 1, LANE), lambda b, n, g: (b, 0, 0)),
                   pl.BlockSpec((1, 1, hd), lambda b, n, g: (b, 0, 0))],
        out_shape=[jax.ShapeDtypeStruct((3, bsz * seq, heads * hd), F32),
                   jax.ShapeDtypeStruct((bsz * seq, heads * hd), BF16),
                   jax.ShapeDtypeStruct((bsz * seq, LANE), BF16),
                   jax.ShapeDtypeStruct((bsz, 1, LANE), F32),
                   jax.ShapeDtypeStruct((bsz, 1, LANE), F32),
                   jax.ShapeDtypeStruct((bsz, 1, hd), F32)],
        scratch_shapes=[pltpu.VMEM((heads, hd, hd), F32), pltpu.VMEM((c, LANE), F32), pltpu.VMEM((LANE, c), F32),
                        pltpu.VMEM((c, LANE), F32), pltpu.VMEM((c, LANE), F32), pltpu.VMEM((c, LANE), F32)],
        semantics=("parallel", "arbitrary", "arbitrary"),
        operands=[qkv, proj, proj, alog, dtb, gnw, ssave, tsave, dout], name="gdn_bwd", comms=comms)


ELEMWISE_BLOCK_ELEMS = 256 * 1024


def _rows_tile(rows, cols):
    want = max(16, ELEMWISE_BLOCK_ELEMS // cols)
    if rows <= want:
        return rows
    t = (want // 16) * 16
    while t > 16 and rows % t:
        t -= 16
    return t if rows % t == 0 else rows


def _piece_specs(pieces, tr, cols):
    specs, leads = [], []
    for p, (arr, lead) in enumerate(pieces):
        if arr.ndim == 3:
            specs.append(pl.BlockSpec((1, tr, cols), functools.partial(lambda i, idx, p: (idx[p], i, 0), p=p)))
        else:
            specs.append(pl.BlockSpec((tr, cols), lambda i, idx: (i, 0)))
        leads.append(jnp.asarray(0 if lead is None else lead, jnp.int32))
    return jnp.stack(leads), specs


def _sum_pieces(refs):
    total = None
    for r in refs:
        v = r[...].astype(F32)
        v = v.reshape(v.shape[-2:])
        total = v if total is None else total + v
    return total


def _adamw(w, m, v, pieces, name, comms=()):
    rows, cols = w.shape
    tr = _rows_tile(rows, cols)
    leads, pspecs = _piece_specs(pieces, tr, cols)
    npc = len(pieces)
    c1 = 1.0 - ADAM_B1 ** ADAM_STEP
    c2 = 1.0 - ADAM_B2 ** ADAM_STEP

    def body(idx_ref, w_ref, m_ref, v_ref, *rest):
        g = _sum_pieces(rest[:npc])
        g_ref, d_ref, nm_ref, nv_ref = rest[npc:]
        nm = ADAM_B1 * m_ref[...] + (1.0 - ADAM_B1) * g
        nv = ADAM_B2 * v_ref[...] + (1.0 - ADAM_B2) * (g * g)
        g_ref[...] = g
        nm_ref[...] = nm
        nv_ref[...] = nv
        d_ref[...] = -ADAM_LR * ((nm / c1) / (jnp.sqrt(nv / c2) + ADAM_EPS) + ADAM_WD * w_ref[...])

    wspec = pl.BlockSpec((tr, cols), lambda i, idx: (i, 0))
    res, lands = _pcall(body, grid=(rows // tr,), in_specs=[wspec] * 3 + pspecs, out_specs=[wspec] * 4,
                        out_shape=[jax.ShapeDtypeStruct((rows, cols), F32)] * 4, semantics=("parallel",),
                        prefetch=[leads], operands=[w, m, v, *[p for p, _ in pieces]], name=name, comms=comms)
    return (res, lands) if comms else res


def _sum_to(pieces, out_dtype, name):
    arr0 = pieces[0][0]
    rows, cols = arr0.shape[-2:]
    tr = _rows_tile(rows, cols)
    leads, pspecs = _piece_specs(pieces, tr, cols)

    def body(idx_ref, *rest):
        rest[-1][...] = _sum_pieces(rest[:-1]).astype(out_dtype)

    return pl.pallas_call(
        body,
        grid_spec=pltpu.PrefetchScalarGridSpec(num_scalar_prefetch=1, grid=(rows // tr,), in_specs=pspecs,
                                               out_specs=pl.BlockSpec((tr, cols), lambda i, idx: (i, 0))),
        out_shape=jax.ShapeDtypeStruct((rows, cols), out_dtype),
        compiler_params=_params("parallel"), name=name)(leads, *[p for p, _ in pieces])


def _pair_add(a, recv, place, name, comms=()):
    _, rows, cols = a.shape
    tr = _rows_tile(rows, cols)
    x, y, c = place
    idx = jnp.stack([2 * (1 - x) + y, 2 * x + (1 - y), 2 * (1 - x) + (1 - y), c]).astype(jnp.int32)

    def body(p_ref, a_ref, r_ref, o_ref):
        o_ref[...] = (a_ref[...].astype(F32) + r_ref[...].astype(F32)).astype(BF16)

    res, lands = _pcall(
        body, grid=(3, rows // tr),
        in_specs=[pl.BlockSpec((1, tr, cols), lambda j, i, p: (2 * p[j] + p[3], i, 0)),
                  pl.BlockSpec((1, tr, cols), lambda j, i, p: (p[j], i, 0))],
        out_specs=[pl.BlockSpec((1, tr, cols), lambda j, i, p: (j, i, 0))],
        out_shape=[jax.ShapeDtypeStruct((3, rows, cols), BF16)], semantics=("parallel", "parallel"),
        prefetch=[idx], operands=[a, recv], name=name, comms=comms)
    return (res[0], lands) if comms else res[0]


def _to_frame(w, offs, fw, name):
    rows, n = w.shape
    tr = _tile(rows, 256)

    def body(off_ref, w_ref, o_ref, pad_ref):
        pad_ref[...] = jnp.zeros_like(pad_ref)
        pad_ref[:, 0:n] = w_ref[...]
        y = pad_ref[...]
        off1, len1, off2 = off_ref[0], off_ref[1], off_ref[2]
        col = lax.broadcasted_iota(jnp.int32, y.shape, 1)
        o_ref[...] = jnp.where(col < off1 + len1, pltpu.roll(y, off1, axis=1),
                               jnp.where(col >= off2 + len1, pltpu.roll(y, off2, axis=1), 0.0)).astype(BF16)

    res, _ = _pcall(body, grid=(rows // tr,), in_specs=[pl.BlockSpec((tr, n), lambda i, o: (i, 0))],
                    out_specs=[pl.BlockSpec((tr, fw), lambda i, o: (i, 0))],
                    out_shape=[jax.ShapeDtypeStruct((rows, fw), BF16)], scratch_shapes=[pltpu.VMEM((tr, fw), F32)],
                    semantics=("parallel",), prefetch=[offs], operands=[w], name=name)
    return res[0]


def _assemble(frames, table, n_blocks, width, name, comms=()):
    _, rows, _ = frames.shape
    tr = _tile(rows, 1024)

    def body(t_ref, f1_ref, f2_ref, o_ref):
        jb = pl.program_id(0)
        v = f1_ref[0]
        o_ref[...] = jnp.where(t_ref[4, jb] > 0, v + f2_ref[0], v)

    res, lands = _pcall(
        body, grid=(n_blocks, rows // tr),
        in_specs=[pl.BlockSpec((1, tr, width), lambda jb, i, t: (t[0, jb], i, t[1, jb])),
                  pl.BlockSpec((1, tr, width), lambda jb, i, t: (t[2, jb], i, t[3, jb]))],
        out_specs=[pl.BlockSpec((tr, width), lambda jb, i, t: (i, jb))],
        out_shape=[jax.ShapeDtypeStruct((rows, n_blocks * width), BF16)], semantics=("parallel", "parallel"),
        prefetch=[table], operands=[frames, frames], name=name, comms=comms)
    return (res[0], lands) if comms else res[0]


ANY = pl.BlockSpec(memory_space=pl.ANY)


def _place():
    x, y, c = lax.axis_index("x"), lax.axis_index("y"), lax.axis_index("c")
    chips = [(1 - x, y), (x, 1 - y), (1 - x, 1 - y)]
    return x, y, c, chips


def _allgather_big(shards, name):
    n = len(shards)

    def body(*refs):
        xs, outs = refs[:n], refs[n:2 * n]
        send_sems, recv_sems, local_sems = refs[2 * n:]
        x, y, c, chips = _place()
        me, sibling = (x, y, c), (x, y, 1 - c)

        def copy(a, k, block, to, src=None):
            dst = outs[a].at[4 * block[0] + 2 * block[1] + block[2]]
            return pltpu.make_async_remote_copy(src_ref=dst if src is None else src, dst_ref=dst,
                                                send_sem=send_sems.at[a, k], recv_sem=recv_sems.at[a, k],
                                                device_id=to, device_id_type=MESH)

        mine = [pltpu.make_async_copy(xs[a], outs[a].at[4 * x + 2 * y + c], local_sems.at[a]) for a in range(n)]
        for cp in mine:
            cp.start()
        first = []
        for a in range(n):
            first.append(copy(a, 0, me, sibling, src=xs[a]))
            first += [copy(a, 1 + j, me, (*chip, c), src=xs[a]) for j, chip in enumerate(chips)]
        for cp in first:
            cp.start()
        passed = []
        for j, chip in enumerate(chips):
            for a in range(n):
                copy(a, 1 + j, (*chip, c), me).wait_recv()
                cp = copy(a, 4 + j, (*chip, c), sibling)
                cp.start()
                passed.append(cp)
        for a in range(n):
            copy(a, 0, sibling, me).wait_recv()
            for j, chip in enumerate(chips):
                copy(a, 4 + j, (*chip, 1 - c), me).wait_recv()
        for cp in first + passed:
            cp.wait_send()
        for cp in mine:
            cp.wait()

    return pl.pallas_call(
        body, in_specs=[ANY] * n, out_specs=[ANY] * n,
        out_shape=[jax.ShapeDtypeStruct((N_DEV,) + s.shape, s.dtype) for s in shards],
        scratch_shapes=[pltpu.SemaphoreType.DMA((n, 7)), pltpu.SemaphoreType.DMA((n, 7)), pltpu.SemaphoreType.DMA((n,))],
        name=name)(*shards)


def _allreduce_small(buf, name):
    rows = buf.shape[0]

    def body(x_ref, o_ref, g_ref, send_sems, recv_sems):
        x, y, c, chips = _place()
        me, sibling = (x, y, c), (x, y, 1 - c)

        def copy(k, block, to, src=None):
            dst = g_ref.at[4 * block[0] + 2 * block[1] + block[2]]
            return pltpu.make_async_remote_copy(src_ref=dst if src is None else src, dst_ref=dst,
                                                send_sem=send_sems.at[k], recv_sem=recv_sems.at[k],
                                                device_id=to, device_id_type=MESH)

        first = [copy(0, me, sibling, src=x_ref)]
        first += [copy(1 + j, me, (*chip, c), src=x_ref) for j, chip in enumerate(chips)]
        for cp in first:
            cp.start()
        passed = [copy(4 + j, (*chip, c), sibling) for j, chip in enumerate(chips)]
        for j, chip in enumerate(chips):
            copy(1 + j, (*chip, c), me).wait_recv()
            passed[j].start()
        copy(0, sibling, me).wait_recv()
        for j, chip in enumerate(chips):
            copy(4 + j, (*chip, 1 - c), me).wait_recv()
        for cp in first + passed:
            cp.wait_send()
        g_ref[4 * x + 2 * y + c] = x_ref[...]
        total = g_ref[0]
        for s in range(1, N_DEV):
            total = total + g_ref[s]
        o_ref[...] = total

    vm = pl.BlockSpec(memory_space=pltpu.VMEM)
    return pl.pallas_call(
        body, in_specs=[vm], out_specs=vm, out_shape=jax.ShapeDtypeStruct((rows, LANE), F32),
        scratch_shapes=[pltpu.VMEM((N_DEV, rows, LANE), F32), pltpu.SemaphoreType.DMA((7,)), pltpu.SemaphoreType.DMA((7,))],
        name=name)(buf)


def _rows(ref, rows):
    return ref if rows is None else ref.at[pl.ds(rows[0], rows[1] - rows[0])]


def _ag_first(shard, rows=None, into=None):
    def plan(srcs, lands):
        x, y, c, chips = _place()
        src = _rows(srcs[0], rows)
        dst = _rows(lands[0].at[4 * x + 2 * y + c], rows)
        return ([("local", src, dst, None), ("remote", src, dst, (x, y, 1 - c))]
                + [("remote", src, dst, (*chip, c)) for chip in chips])

    land = jax.ShapeDtypeStruct((N_DEV,) + shard.shape, shard.dtype)
    if into is None:
        return _Comm([shard], [land], plan, 5)
    return _Comm([shard, into], [land], plan, 5, alias={1: 0})


def _ag_second(g, rows=None):
    def plan(srcs, lands):
        x, y, c, chips = _place()
        (buf,) = lands
        refs = [_rows(buf.at[4 * cx + 2 * cy + c], rows) for cx, cy in chips]
        return [("remote", r, r, (x, y, 1 - c)) for r in refs]

    return _Comm([g], [jax.ShapeDtypeStruct(g.shape, g.dtype)], plan, 3, alias={0: 0})


def _rs_first(grad):
    def plan(srcs, lands):
        x, y, c, _ = _place()
        (a,), (land,) = srcs, lands
        return [("remote", a.at[2 * j + (1 - c)], land.at[j], (x, y, 1 - c)) for j in range(4)]

    return _Comm([grad], [jax.ShapeDtypeStruct((4,) + grad.shape[1:], grad.dtype)], plan, 4)


def _rs_second(pair, rows=None, into=None):
    def plan(srcs, lands):
        x, y, c, chips = _place()
        return [("remote", _rows(srcs[0].at[j], rows), _rows(lands[0].at[j], rows), (cx, cy, c))
                for j, (cx, cy) in enumerate(chips)]

    land = jax.ShapeDtypeStruct((3,) + pair.shape[1:], pair.dtype)
    if into is None:
        return _Comm([pair], [land], plan, 3)
    return _Comm([pair, into], [land], plan, 3, alias={1: 0})


class _InLayout:
    def __init__(self, n_in, gw, heads, scw):
        self.n_in, self.split = n_in, 4 * gw + 2 * heads
        self.gap = LANE - 2 * heads
        self.ab_col, self.sc_col = 4 * gw, 4 * gw + LANE
        self.used = 4 * gw + LANE + 3 * scw
        p0 = [s * n_in + (self.gap if s * n_in >= self.split else 0) for s in range(N_DEV)]
        self.fstart = [(p // MXU) * MXU for p in p0]
        need = []
        for s in range(N_DEV):
            straddle = s * n_in < self.split < (s + 1) * n_in
            need.append(p0[s] - self.fstart[s] + n_in + (self.gap if straddle else 0))
        self.fw = -(-max(need) // MXU) * MXU
        self.wp = max(f + self.fw for f in self.fstart)
        assert self.wp >= self.used and self.wp % MXU == 0
        nfb = self.fw // MXU
        rows = []
        for jb in range(self.wp // MXU):
            src = [(s, jb - self.fstart[s] // MXU) for s in range(N_DEV) if 0 <= jb - self.fstart[s] // MXU < nfb]
            assert 1 <= len(src) <= 2, (jb, src)
            (s1, b1), (s2, b2) = src[0], src[-1]
            rows.append((s1, b1, s2, b2, int(len(src) == 2)))
        self.table = np.asarray(rows, np.int32).T.copy()

    def frame_block(self, s):
        p = s * self.n_in
        return (p + jnp.where(p >= self.split, self.gap, 0)) // MXU

    def offsets(self, s):
        p = s * self.n_in
        after = p >= self.split
        off1 = p + jnp.where(after, self.gap, 0) - self.frame_block(s) * MXU
        len1 = jnp.where(after, self.n_in, jnp.clip(self.split - p, 0, self.n_in))
        off2 = off1 + jnp.where(len1 < self.n_in, self.gap, 0)
        return off1, len1, off2

    def to_frame(self, w, s):
        return _to_frame(w, jnp.stack(self.offsets(s)).astype(jnp.int32), self.fw, "w_in_frame")

    def from_frame(self, f, s):
        off1, len1, off2 = self.offsets(s)
        a = lax.dynamic_slice(f, (0, off1), (f.shape[0], self.n_in))
        b = lax.dynamic_slice(f, (0, off2), (f.shape[0], self.n_in))
        col = lax.broadcasted_iota(jnp.int32, (1, self.n_in), 1)
        return jnp.where(col < len1, a, b)


def _pack_rows(parts):
    rows = []
    for p in parts:
        flat = p.reshape(-1)
        pad = (-flat.shape[0]) % LANE
        rows.append(jnp.pad(flat, (0, pad)).reshape(-1, LANE))
    buf = jnp.concatenate(rows, axis=0)
    return jnp.pad(buf, ((0, (-buf.shape[0]) % 8), (0, 0)))


def _unpack_rows(buf, shapes):
    out, r = [], 0
    for shp in shapes:
        size = int(np.prod(shp))
        nr = -(-size // LANE)
        out.append(buf[r:r + nr].reshape(-1)[:size].reshape(shp))
        r += nr
    return out


def _pad_lanes(v):
    return jnp.pad(v, ((0, 0), (0, LANE - v.shape[1])))


def kernel(x, norm_mix_pre, w_in, conv_qkv_w, a_log, dt_bias, gdn_norm_w, conv_sc_w, w_out, norm_mix_post, norm_mlp_pre, w_up, w_down, norm_mlp_post, loss_target, m_norm_mix_pre, m_w_in, m_conv_qkv_w, m_a_log, m_dt_bias, m_gdn_norm_w, m_conv_sc_w, m_w_out, m_norm_mix_post, m_norm_mlp_pre, m_w_up, m_w_down, m_norm_mlp_post, v_norm_mix_pre, v_w_in, v_conv_qkv_w, v_a_log, v_dt_bias, v_gdn_norm_w, v_conv_sc_w, v_w_out, v_norm_mix_post, v_norm_mlp_pre, v_w_up, v_w_down, v_norm_mlp_post):
    bsz, seq, d = x.shape
    t = bsz * seq
    heads, hd = a_log.shape[-1], gdn_norm_w.shape[-1]
    gw = heads * hd
    scw = conv_sc_w.shape[-1] * N_DEV
    dff_w = w_up.shape[-1] * N_DEV
    lay = _InLayout(w_in.shape[-1], gw, heads, scw)
    mx, my, mc = lax.axis_index("x"), lax.axis_index("y"), lax.axis_index("c")
    me = 4 * mx + 2 * my + mc
    chip = 2 * mx + my

    x2 = x.reshape(t, d)
    tgt = loss_target.reshape(t, d)
    g1, g2, g3, g4 = norm_mix_pre, norm_mix_post, norm_mlp_pre, norm_mlp_post

    frame = lay.to_frame(w_in[0], me)
    (g_in,) = _allgather_big([frame], "allgather_w_in")
    w_up_b, w_down_b = w_up[0].astype(BF16), w_down[0].astype(BF16)
    up_cols = dff_w // N_DEV
    qu, qd = d // 4, up_cols // 4
    w_pad, (g_out,) = _assemble(g_in, jnp.asarray(lay.table), lay.wp // MXU, MXU, "assemble_w_in",
                                comms=[_ag_first(w_out[0].astype(BF16))])
    kq, ks = conv_qkv_w.shape[1], conv_sc_w.shape[1]
    cq_n, cs_n = conv_qkv_w.shape[-1], conv_sc_w.shape[-1]
    cq_full = lax.dynamic_update_slice(jnp.zeros((kq, 3 * gw), F32), conv_qkv_w[0], (0, me * cq_n))
    cs_full = lax.dynamic_update_slice(jnp.zeros((ks, scw), F32), conv_sc_w[0], (0, me * cs_n))
    conv_q, conv_s = _unpack_rows(_allreduce_small(_pack_rows([cq_full, cs_full]), "allgather_conv"),
                                  [(kq, 3 * gw), (ks, scw)])
    alog_t, dtb_t = _pad_lanes(a_log), _pad_lanes(dt_bias)

    xn = _rms_fwd(x2, g1)
    (proj,), (g_up,) = _matmul(xn, w_pad, mode="nn", out_dtypes=[F32], name="in_proj", tn=768,
                               comms=[_ag_first(w_up_b, rows=(0, 3 * qu))])
    qkv, (g_out, g_up) = _qkvconv_fwd(proj, conv_q, bsz, seq, gw,
                                      comms=[_ag_second(g_out), _ag_first(w_up_b, rows=(3 * qu, 4 * qu), into=g_up)])
    (gdn_out, ssave, tsave), (g_up, g_down) = _gdn_fwd(
        qkv, proj, alog_t, dtb_t, gdn_norm_w, bsz, seq, heads, 3 * gw, lay.ab_col,
        comms=[_ag_second(g_up), _ag_first(w_down_b, rows=(0, 2 * qd))])
    sc_out = _sc_fwd(proj, conv_s, bsz, seq, scw, lay.sc_col)
    mixed = jnp.concatenate([gdn_out, sc_out], axis=1)
    w_out_f = g_out.reshape(d, d)
    (mix,), (g_down,) = _matmul(mixed, w_out_f, mode="nn", out_dtypes=[F32], name="out_proj",
                                comms=[_ag_first(w_down_b, rows=(2 * qd, 3 * qd), into=g_down)])
    (h, xn2), (g_down,) = _post1(x2, mix, g2, g3, comms=[_ag_first(w_down_b, rows=(3 * qd, 4 * qd), into=g_down)])

    def up_epilogue(acc):
        r = jnp.maximum(acc, 0.0)
        return r, r * r

    (act, hid), (g_down,) = _matmul(
        xn2, g_up, mode="nn", out_dtypes=[BF16, BF16], name="mlp_up", n_cols=dff_w, epilogue=up_epilogue,
        b_spec=lambda tk, tn: pl.BlockSpec((1, tk, tn), lambda i, j, k: (j // (up_cols // tn), k, j % (up_cols // tn))),
        comms=[_ag_second(g_down)])
    w_down_f = g_down.reshape(dff_w, d)
    (ff,) = _matmul(hid, w_down_f, mode="nn", out_dtypes=[F32], name="mlp_down", tn=1024, tk=2048)
    dff, dy, dg4, loss_p = _post2_loss(h, ff, g4, tgt)

    def pieces(part, sib, got):
        return [(part, me), (sib, chip), (got, 0), (got, 1), (got, 2)]

    place = (mx, my, mc)

    (dpre,) = _matmul(dff, w_down_f, mode="nt", out_dtypes=[BF16], name="d_hidden", extras=[act],
                      epilogue=lambda acc, a: (acc * (2.0 * a.astype(F32)),))
    (dw_down,) = _matmul(hid, dff, mode="tn", out_dtypes=[BF16], name="dw_down")
    dw_down = dw_down.reshape(N_DEV, dff_w // N_DEV, d)
    (dxn2,), (sib_down,) = _matmul(
        dpre, g_up, mode="nt", out_dtypes=[F32], name="d_xn2", n_cols=d, tn=1024, tk=min(up_cols, 2048),
        b_spec=lambda tk, tn: pl.BlockSpec((1, tn, tk), lambda i, j, k: (k // (up_cols // tk), j, k % (up_cols // tk))),
        comms=[_rs_first(dw_down)])
    pair_down = _pair_add(dw_down, sib_down, place, "pair_add_down")
    (dw_up,), (got_down,) = _matmul(
        xn2, dpre, mode="tn", out_dtypes=[BF16], name="dw_up",
        out_custom=lambda tm, tn: ((N_DEV, d, up_cols), (1, tm, tn), lambda i, j, k: (j // (up_cols // tn), i, j % (up_cols // tn))),
        comms=[_rs_second(pair_down, rows=(0, 3 * qd))])
    (dmix, dh, dg2, dg3), (got_down, sib_up) = _mid_bwd(
        h, mix, dy, dxn2, g2, g3, comms=[_rs_second(pair_down, rows=(3 * qd, 4 * qd), into=got_down), _rs_first(dw_up)])
    pair_up = _pair_add(dw_up, sib_up, place, "pair_add_up")
    (dmixed,) = _matmul(dmix, w_out_f, mode="nt", out_dtypes=[F32], name="d_mixed")
    (dw_out,), (got_up,) = _matmul(mixed, dmix, mode="tn", out_dtypes=[BF16], name="dw_out",
                                   comms=[_rs_second(pair_up, rows=(0, qu))])
    dw_out = dw_out.reshape(N_DEV, d // N_DEV, d)
    (dscb, dscc, dsch, dconv_s), (sib_out,) = _sc_bwd(proj, conv_s, dmixed, bsz, seq, scw, lay.sc_col, gw,
                                                      comms=[_rs_first(dw_out)])
    pair_out = _pair_add(dw_out, sib_out, place, "pair_add_out")
    (dact, dz, dab, dalog, ddtb, dgnw), (got_up,) = _gdn_bwd(
        qkv, proj, alog_t, dtb_t, gdn_norm_w, ssave, tsave, dmixed, bsz, seq, heads, 3 * gw, lay.ab_col,
        comms=[_rs_second(pair_up, rows=(qu, 3 * qu), into=got_up)])
    (dqkv, dconv_q), (got_up,) = _qkvconv_bwd(proj, conv_q, dact, bsz, seq, gw,
                                              comms=[_rs_second(pair_up, rows=(3 * qu, 4 * qu), into=got_up)])
    dproj = jnp.concatenate([dqkv, dz, dab, dscb, dscc, dsch, jnp.zeros((t, lay.wp - lay.used), BF16)], axis=1)
    nfb = lay.fw // MXU
    (dw_in,), (got_out,) = _matmul(
        xn, dproj, mode="tn", out_dtypes=[BF16], name="dw_in", n_cols=N_DEV * lay.fw, tn=MXU,
        b_spec=lambda tk, tn: pl.BlockSpec((tk, tn), lambda i, j, k: (k, lay.frame_block(j // nfb) + j % nfb)),
        out_custom=lambda tm, tn: ((N_DEV, d, lay.fw), (1, tm, tn), lambda i, j, k: (j // nfb, i, j % nfb)),
        comms=[_rs_second(pair_out)])
    tq = t // 4
    tk_in = _tile(lay.wp, 2560)

    def d_xn(part, into, comms):
        return _matmul(dproj, w_pad, mode="nt", out_dtypes=[F32], name="d_xn_%d" % part, tn=1024, tk=tk_in,
                       a_rows=(part * tq, tq), out_into=into, comms=comms)

    (dxn,), (sib_in,) = d_xn(0, None, [_rs_first(dw_in)])
    pair_in = _pair_add(dw_in, sib_in, place, "pair_add_in")
    (dxn,), (got_in,) = d_xn(1, dxn, [_rs_second(pair_in, rows=(0, qu))])
    (dxn,), (got_in,) = d_xn(2, dxn, [_rs_second(pair_in, rows=(qu, 2 * qu), into=got_in)])
    (dxn,), (got_in,) = d_xn(3, dxn, [_rs_second(pair_in, rows=(2 * qu, 4 * qu), into=got_in)])
    grad_x, dg1 = _pre_bwd(x2, dh, dxn, g1)

    gin_frame = _sum_to(pieces(dw_in, sib_in, got_in), F32, "grad_w_in_frame")
    big = {
        "w_in": _adamw(w_in[0], m_w_in[0], v_w_in[0], [(lay.from_frame(gin_frame, me), None)], "adamw_w_in"),
        "w_out": _adamw(w_out[0], m_w_out[0], v_w_out[0], pieces(dw_out, sib_out, got_out), "adamw_w_out"),
        "w_up": _adamw(w_up[0], m_w_up[0], v_w_up[0], pieces(dw_up, sib_up, got_up), "adamw_w_up"),
        "w_down": _adamw(w_down[0], m_w_down[0], v_w_down[0], pieces(dw_down, sib_down, got_down), "adamw_w_down"),
    }

    small_shapes = [(kq, 3 * gw), (ks, scw), (1, d), (1, d), (1, d), (1, d), (1, LANE), (1, LANE), (1, hd), (1, LANE)]
    small = _unpack_rows(
        _allreduce_small(_pack_rows([dconv_q, dconv_s, dg1, dg2, dg3, dg4, jnp.sum(dalog, axis=0), jnp.sum(ddtb, axis=0),
                                     jnp.sum(dgnw, axis=0), loss_p]), "allreduce_small"), small_shapes)
    gq, gs, sg1, sg2, sg3, sg4, salog, sdtb, sgnw, sloss = small
    loss = sloss[0, 0]
    small_grads = {
        "norm_mix_pre": sg1, "conv_qkv_w": lax.dynamic_slice(gq, (0, me * cq_n), (kq, cq_n)),
        "a_log": salog[:, :heads], "dt_bias": sdtb[:, :heads], "gdn_norm_w": sgnw,
        "conv_sc_w": lax.dynamic_slice(gs, (0, me * cs_n), (ks, cs_n)),
        "norm_mix_post": sg2, "norm_mlp_pre": sg3, "norm_mlp_post": sg4,
    }
    weights = {"norm_mix_pre": (norm_mix_pre, m_norm_mix_pre, v_norm_mix_pre), "conv_qkv_w": (conv_qkv_w[0], m_conv_qkv_w[0], v_conv_qkv_w[0]),
               "a_log": (a_log, m_a_log, v_a_log), "dt_bias": (dt_bias, m_dt_bias, v_dt_bias),
               "gdn_norm_w": (gdn_norm_w, m_gdn_norm_w, v_gdn_norm_w), "conv_sc_w": (conv_sc_w[0], m_conv_sc_w[0], v_conv_sc_w[0]),
               "norm_mix_post": (norm_mix_post, m_norm_mix_post, v_norm_mix_post),
               "norm_mlp_pre": (norm_mlp_pre, m_norm_mlp_pre, v_norm_mlp_pre),
               "norm_mlp_post": (norm_mlp_post, m_norm_mlp_post, v_norm_mlp_post)}
    res = dict(big)
    for name, (w, m, v) in weights.items():
        res[name] = _adamw(w, m, v, [(small_grads[name], None)], "adamw_" + name)

    order = ["norm_mix_pre", "w_in", "conv_qkv_w", "a_log", "dt_bias", "gdn_norm_w", "conv_sc_w", "w_out", "norm_mix_post",
             "norm_mlp_pre", "w_up", "w_down", "norm_mlp_post"]
    shapes = {"norm_mix_pre": norm_mix_pre.shape, "w_in": w_in.shape, "conv_qkv_w": conv_qkv_w.shape, "a_log": a_log.shape,
              "dt_bias": dt_bias.shape, "gdn_norm_w": gdn_norm_w.shape, "conv_sc_w": conv_sc_w.shape, "w_out": w_out.shape,
              "norm_mix_post": norm_mix_post.shape, "norm_mlp_pre": norm_mlp_pre.shape, "w_up": w_up.shape,
              "w_down": w_down.shape, "norm_mlp_post": norm_mlp_post.shape}
    outs = [loss, grad_x.reshape(bsz, seq, d)]
    for part in range(4):
        outs += [res[nm][part].reshape(shapes[nm]) for nm in order]
    return tuple(outs)
```

```python
import functools
import math

import numpy as np
import jax
import jax.numpy as jnp
from jax import lax
from jax.experimental import pallas as pl
from jax.experimental.pallas import tpu as pltpu

F32 = jnp.float32
BF16 = jnp.bfloat16
HI = lax.Precision.HIGHEST
MESH = pl.DeviceIdType.MESH

N_DEV = 8
LANE = 128
MXU = 256
CHUNK = 64
NORM_EPS = 1e-6
L2_EPS = 1e-6
VMEM_LIMIT = 56 * 1024 * 1024

ADAM_LR = 0.001
ADAM_B1 = 0.9
ADAM_B2 = 0.999
ADAM_EPS = 1e-08
ADAM_WD = 0.01
ADAM_STEP = 10

NN = (((1,), (0,)), ((), ()))
NT = (((1,), (1,)), ((), ()))
TN = (((0,), (0,)), ((), ()))


def _params(*sem):
    return pltpu.CompilerParams(dimension_semantics=sem, vmem_limit_bytes=VMEM_LIMIT)


def _tile(n, want):
    if n <= want:
        return n
    t = (want // LANE) * LANE
    while t > LANE and n % t:
        t -= LANE
    assert n % t == 0, (n, want)
    return t


class _Comm:
    def __init__(self, srcs, lands, plan, n, alias=None):
        self.srcs, self.lands, self.plan, self.n, self.alias = list(srcs), list(lands), plan, n, dict(alias or {})


def _pcall(body, *, grid, in_specs, out_specs, out_shape, operands, name, scratch_shapes=(), semantics=None,
           prefetch=(), comms=(), fill=None):
    n_pf, n_in, n_out, n_scr = len(prefetch), len(in_specs), len(out_specs), len(scratch_shapes)
    srcs = [s for cm in comms for s in cm.srcs]
    lands = [l for cm in comms for l in cm.lands]
    n_src, n_land = len(srcs), len(lands)
    n_copies = sum(cm.n for cm in comms)
    aliases, so, lo = {n_pf + a: b for a, b in (fill or {}).items()}, 0, 0
    for cm in comms:
        for a, b in cm.alias.items():
            aliases[n_pf + n_in + so + a] = n_out + lo + b
        so, lo = so + len(cm.srcs), lo + len(cm.lands)
    any_spec = pl.BlockSpec(memory_space=pl.ANY)

    def wrapped(*refs):
        pf, r = refs[:n_pf], refs[n_pf:]
        ins, csrc = r[:n_in], r[n_in:n_in + n_src]
        outs = r[n_in + n_src:n_in + n_src + n_out]
        cland = r[n_in + n_src + n_out:n_in + n_src + n_out + n_land]
        rest = r[n_in + n_src + n_out + n_land:]
        scratch = rest[:n_scr]
        if not comms:
            body(*pf, *ins, *outs, *scratch)
            return
        send_sems, recv_sems = rest[n_scr:]

        def copies():
            out, k, s0, l0 = [], 0, 0, 0
            for cm in comms:
                for kind, src, dst, dev in cm.plan(csrc[s0:s0 + len(cm.srcs)], cland[l0:l0 + len(cm.lands)]):
                    if kind == "local":
                        out.append((kind, pltpu.make_async_copy(src, dst, send_sems.at[k])))
                    else:
                        out.append((kind, pltpu.make_async_remote_copy(
                            src_ref=src, dst_ref=dst, send_sem=send_sems.at[k], recv_sem=recv_sems.at[k],
                            device_id=dev, device_id_type=MESH)))
                    k += 1
                s0, l0 = s0 + len(cm.srcs), l0 + len(cm.lands)
            assert k == n_copies
            return out

        ids = [pl.program_id(a) for a in range(len(grid))]
        first = functools.reduce(jnp.logical_and, [i == 0 for i in ids])
        last = functools.reduce(jnp.logical_and, [i == g - 1 for i, g in zip(ids, grid)])

        @pl.when(first)
        def _():
            for _, cp in copies():
                cp.start()

        body(*pf, *ins, *outs, *scratch)

        @pl.when(last)
        def _():
            cps = copies()
            for kind, cp in cps:
                if kind == "remote":
                    cp.wait_recv()
            for kind, cp in cps:
                if kind == "remote":
                    cp.wait_send()
                else:
                    cp.wait()

    sems = [pltpu.SemaphoreType.DMA((n_copies,)), pltpu.SemaphoreType.DMA((n_copies,))] if comms else []
    if semantics is None or comms:
        semantics = ("arbitrary",) * len(grid)
    res = pl.pallas_call(
        wrapped,
        grid_spec=pltpu.PrefetchScalarGridSpec(
            num_scalar_prefetch=n_pf, grid=tuple(grid), in_specs=list(in_specs) + [any_spec] * n_src,
            out_specs=list(out_specs) + [any_spec] * n_land, scratch_shapes=list(scratch_shapes) + sems),
        out_shape=list(out_shape) + lands,
        input_output_aliases=aliases,
        compiler_params=_params(*semantics), name=name)(*prefetch, *operands, *srcs)
    return list(res[:n_out]), list(res[n_out:])


def _bdot(a, b, dims=NN):
    return lax.dot_general(a.astype(BF16), b.astype(BF16), dims, preferred_element_type=F32)


def _hdot(a, b, dims=NN):
    return lax.dot_general(a, b, dims, preferred_element_type=F32, precision=HI)


def _mdot(a, b, dims=NN):
    return lax.dot_general(a, b, dims, preferred_element_type=F32, precision=lax.Precision.HIGH)


def _sigmoid(x):
    return 1.0 / (1.0 + jnp.exp(-x))


def _softplus(x):
    return jnp.maximum(x, 0.0) + jnp.log(1.0 + jnp.exp(-jnp.abs(x)))


def _matmul(a, b, *, mode, out_dtypes, name, n_cols=None, tm=1024, tn=512, tk=4096, epilogue=None, extras=(),
            b_spec=None, out_custom=None, a_rows=None, out_into=None, a_spec=None, k_total=None, comms=()):
    if mode == "tn":
        K, M = a.shape
    else:
        M, K = a.shape
    if k_total is not None:
        K = k_total
    r0 = 0
    if a_rows is not None:
        r0, M = a_rows
    N = n_cols if n_cols is not None else (b.shape[0] if mode == "nt" else b.shape[1])
    tm, tk, tn = _tile(M, tm), _tile(K, tk), _tile(N, tn)
    assert r0 % tm == 0
    i0 = r0 // tm
    if b_spec is None:
        b_spec = pl.BlockSpec((tn, tk), lambda i, j, k: (j, k)) if mode == "nt" else pl.BlockSpec((tk, tn), lambda i, j, k: (k, j))
    else:
        b_spec = b_spec(tk, tn)
    gm, gn, nk = M // tm, N // tn, K // tk
    if a_rows is not None:
        out_shapes = [(a.shape[0], N)] * len(out_dtypes)
        out_blocks = [(tm, tn)] * len(out_dtypes)
        out_index = [lambda i, j, k: (i + i0, j)] * len(out_dtypes)
    elif out_custom is None:
        out_shapes = [(M, N)] * len(out_dtypes)
        out_blocks = [(tm, tn)] * len(out_dtypes)
        out_index = [lambda i, j, k: (i, j)] * len(out_dtypes)
    else:
        shape, blk, ix = out_custom(tm, tn)
        out_shapes, out_blocks, out_index = [shape] * len(out_dtypes), [blk] * len(out_dtypes), [ix] * len(out_dtypes)
    if a_spec is not None:
        a_spec = a_spec(tm, tk, r0)
    elif mode == "tn":
        a_spec = pl.BlockSpec((tk, tm), lambda i, j, k: (k, i))
    else:
        a_spec = pl.BlockSpec((tm, tk), lambda i, j, k: (i + i0, k))
    hoist = mode == "tn" and nk == 1 and gn > 1
    dims = {"nn": NN, "nt": NT, "tn": TN}[mode]
    n_ex, n_out = len(extras), len(out_dtypes)
    out_into = [] if out_into is None else list(out_into)
    n_into = len(out_into)
    assert n_into in (0, n_out)

    def body(a_ref, b_ref, *rest):
        ex, outs = rest[:n_ex], rest[n_ex + n_into:n_ex + n_into + n_out]

        def finish(acc):
            res = epilogue(acc, *[e[...] for e in ex]) if epilogue is not None else (acc,)
            for o, r in zip(outs, res):
                o[...] = r.reshape(o.shape).astype(o.dtype)

        bb = b_ref[...]
        bb = bb.reshape(bb.shape[-2:])
        if hoist:
            at_ref = rest[-1]

            @pl.when(pl.program_id(1) == 0)
            def _():
                at_ref[...] = a_ref[...].T

            finish(lax.dot_general(at_ref[...], bb, NN, preferred_element_type=F32))
            return
        part = lax.dot_general(a_ref[...], bb, dims, preferred_element_type=F32)
        if nk == 1:
            finish(part)
        else:
            acc = rest[-1]
            k = pl.program_id(2)

            @pl.when(k == 0)
            def _():
                acc[...] = part

            @pl.when(k > 0)
            def _():
                acc[...] += part

            @pl.when(k == nk - 1)
            def _():
                finish(acc[...])

    scratch = [pltpu.VMEM((tm, tk), BF16)] if hoist else ([pltpu.VMEM((tm, tn), F32)] if nk > 1 else [])
    outs, lands = _pcall(
        body, grid=(gm, gn, nk),
        in_specs=([a_spec, b_spec] + [pl.BlockSpec((tm, tn), lambda i, j, k: (i, j)) for _ in extras]
                  + [pl.BlockSpec(memory_space=pl.ANY)] * n_into),
        out_specs=[pl.BlockSpec(blk, ix) for blk, ix in zip(out_blocks, out_index)],
        out_shape=[jax.ShapeDtypeStruct(s, d) for s, d in zip(out_shapes, out_dtypes)],
        scratch_shapes=scratch, semantics=("parallel", "arbitrary", "arbitrary"),
        operands=[a, b, *extras, *out_into], name=name, comms=comms,
        fill={2 + n_ex + o: o for o in range(n_into)})
    return (outs, lands) if comms else outs


TR = 128


def _rms(x):
    return lax.rsqrt(jnp.mean(x * x, axis=-1, keepdims=True) + NORM_EPS)


def _rms_bwd(x, r, w, dy):
    u = dy * w
    dx = r * u - x * (r * r * r) * jnp.mean(x * u, axis=-1, keepdims=True)
    return dx, dy * x * r


def _row_call(body, ins, row_flags, outs, name, n_rows, comms=()):
    tr = min(TR, n_rows)
    in_specs = []
    for arr, is_row in zip(ins, row_flags):
        if is_row:
            in_specs.append(pl.BlockSpec((tr, arr.shape[1]), lambda i: (i, 0)))
        else:
            in_specs.append(pl.BlockSpec(arr.shape, lambda i: (0, 0)))
    out_specs, out_shape = [], []
    for shape, dtype, kind in outs:
        if kind == "row":
            out_specs.append(pl.BlockSpec((tr, shape[1]), lambda i: (i, 0)))
        else:
            out_specs.append(pl.BlockSpec(shape, lambda i: (0, 0)))
        out_shape.append(jax.ShapeDtypeStruct(shape, dtype))
    res, lands = _pcall(body, grid=(n_rows // tr,), in_specs=in_specs, out_specs=out_specs, out_shape=out_shape,
                        operands=list(ins), name=name, comms=comms)
    return (res, lands) if comms else res


def _acc_out(ref, val):
    @pl.when(pl.program_id(0) == 0)
    def _():
        ref[...] = val

    @pl.when(pl.program_id(0) > 0)
    def _():
        ref[...] += val


def _rms_fwd(x, g, comms):
    T, D = x.shape

    def body(x_ref, g_ref, o_ref):
        xv = x_ref[...]
        o_ref[...] = (xv * _rms(xv) * g_ref[...]).astype(BF16)

    res, lands = _row_call(body, [x, g], [True, False], [((T, D), BF16, "row")], "rms_fwd", T, comms=comms)
    return res[0], lands


def _post1(x, mix, g2, g3, comms=()):
    T, D = x.shape

    def body(x_ref, mix_ref, g2_ref, g3_ref, h_ref, xn2_ref):
        mv = mix_ref[...]
        h = x_ref[...] + mv * _rms(mv) * g2_ref[...]
        h_ref[...] = h
        xn2_ref[...] = (h * _rms(h) * g3_ref[...]).astype(BF16)

    return _row_call(body, [x, mix, g2, g3], [True, True, False, False],
                     [((T, D), F32, "row"), ((T, D), BF16, "row")], "post1", T, comms=comms)


def _post2_loss(h, ff, g4, target):
    T, D = h.shape

    def body(h_ref, ff_ref, g4_ref, t_ref, dff_ref, dy_ref, dg4_ref, loss_ref):
        fv = ff_ref[...]
        r = _rms(fv)
        err = h_ref[...] + fv * r * g4_ref[...] - t_ref[...]
        dy = err * (1.0 / D)
        dy_ref[...] = dy
        dff, dwt = _rms_bwd(fv, r, g4_ref[...], dy)
        dff_ref[...] = dff.astype(BF16)
        _acc_out(dg4_ref, jnp.sum(dwt, axis=0, keepdims=True))
        part = 0.5 * jnp.sum(jnp.mean(err * err, axis=-1, keepdims=True), axis=0, keepdims=True)
        _acc_out(loss_ref, jnp.broadcast_to(part, (1, LANE)))

    return _row_call(body, [h, ff, g4, target], [True, True, False, True],
                     [((T, D), BF16, "row"), ((T, D), F32, "row"), ((1, D), F32, "acc"), ((1, LANE), F32, "acc")],
                     "post2_loss", T)


def _mid_bwd(h, mix, dy, dxn2, g2, g3, comms=()):
    T, D = h.shape

    def body(h_ref, mix_ref, dy_ref, dxn2_ref, g2_ref, g3_ref, dmix_ref, dh_ref, dg2_ref, dg3_ref):
        hv = h_ref[...]
        d1, dw3 = _rms_bwd(hv, _rms(hv), g3_ref[...], dxn2_ref[...])
        dh = dy_ref[...] + d1
        dh_ref[...] = dh
        mv = mix_ref[...]
        dmix, dw2 = _rms_bwd(mv, _rms(mv), g2_ref[...], dh)
        dmix_ref[...] = dmix.astype(BF16)
        _acc_out(dg2_ref, jnp.sum(dw2, axis=0, keepdims=True))
        _acc_out(dg3_ref, jnp.sum(dw3, axis=0, keepdims=True))

    return _row_call(body, [h, mix, dy, dxn2, g2, g3], [True, True, True, True, False, False],
                     [((T, D), BF16, "row"), ((T, D), F32, "row"), ((1, D), F32, "acc"), ((1, D), F32, "acc")],
                     "mid_bwd", T, comms=comms)


def _pre_bwd(x, dh, dxn, g1, comms=()):
    T, D = x.shape

    def body(x_ref, dh_ref, dxn_ref, g1_ref, gx_ref, dg1_ref):
        xv = x_ref[...]
        d1, dw1 = _rms_bwd(xv, _rms(xv), g1_ref[...], dxn_ref[...])
        gx_ref[...] = dh_ref[...] + d1
        _acc_out(dg1_ref, jnp.sum(dw1, axis=0, keepdims=True))

    return _row_call(body, [x, dh, dxn, g1], [True, True, True, False],
                     [((T, D), F32, "row"), ((1, D), F32, "acc")], "pre_bwd", T, comms=comms)


def _shift_down(x, s):
    if s == 0:
        return x
    row = lax.broadcasted_iota(jnp.int32, x.shape, 0)
    return jnp.where(row >= s, pltpu.roll(x, s, axis=0), 0.0)


def _shift_up(x, s):
    if s == 0:
        return x
    n = x.shape[0]
    row = lax.broadcasted_iota(jnp.int32, x.shape, 0)
    return jnp.where(row < n - s, pltpu.roll(x, n - s, axis=0), 0.0)


def _conv(x, w):
    kw = w.shape[0]
    out = w[kw - 1:kw, :] * x
    for j in range(kw - 1):
        out = out + w[j:j + 1, :] * _shift_down(x, kw - 1 - j)
    return out


def _conv_bwd(x, w, dout):
    kw = w.shape[0]
    dx = w[kw - 1:kw, :] * dout
    dws = []
    for j in range(kw - 1):
        dx = dx + w[j:j + 1, :] * _shift_up(dout, kw - 1 - j)
        dws.append(jnp.sum(dout * _shift_down(x, kw - 1 - j), axis=0, keepdims=True))
    dws.append(jnp.sum(dout * x, axis=0, keepdims=True))
    return dx, jnp.concatenate(dws, axis=0)


def _qkvconv_fwd(proj, w, bsz, seq, gw, comms=()):
    nct = gw // LANE
    kw = w.shape[0]

    def body(p_ref, w_ref, o_ref):
        cv = _conv(p_ref[...], w_ref[...])
        o_ref[...] = (cv * _sigmoid(cv)).reshape(o_ref.shape)

    res, lands = _pcall(
        body, grid=(3, bsz, nct),
        in_specs=[pl.BlockSpec((seq, LANE), lambda p, b, c: (b, p * nct + c)),
                  pl.BlockSpec((kw, LANE), lambda p, b, c: (0, p * nct + c))],
        out_specs=[pl.BlockSpec((1, seq, LANE), lambda p, b, c: (p, b, c))],
        out_shape=[jax.ShapeDtypeStruct((3, bsz * seq, gw), F32)],
        semantics=("parallel", "parallel", "parallel"), operands=[proj, w], name="qkvconv_fwd", comms=comms)
    return res[0], lands


def _qkvconv_bwd(proj, w, dact, bsz, seq, gw, comms=()):
    nct = gw // LANE
    kw = w.shape[0]

    def body(p_ref, w_ref, d_ref, dp_ref, dw_ref):
        pre = p_ref[...]
        wv = w_ref[...]
        cv = _conv(pre, wv)
        sg = _sigmoid(cv)
        dcv = d_ref[...].reshape(cv.shape) * (sg * (1.0 + cv * (1.0 - sg)))
        dpre, dw = _conv_bwd(pre, wv, dcv)
        dp_ref[...] = dpre.astype(BF16)
        b = pl.program_id(2)

        @pl.when(b == 0)
        def _():
            dw_ref[...] = dw

        @pl.when(b > 0)
        def _():
            dw_ref[...] += dw

    res, lands = _pcall(
        body, grid=(3, nct, bsz),
        in_specs=[pl.BlockSpec((seq, LANE), lambda p, c, b: (b, p * nct + c)),
                  pl.BlockSpec((kw, LANE), lambda p, c, b: (0, p * nct + c)),
                  pl.BlockSpec((1, seq, LANE), lambda p, c, b: (p, b, c))],
        out_specs=[pl.BlockSpec((seq, LANE), lambda p, c, b: (b, p * nct + c)),
                   pl.BlockSpec((kw, LANE), lambda p, c, b: (0, p * nct + c))],
        out_shape=[jax.ShapeDtypeStruct((bsz * seq, 3 * gw), BF16), jax.ShapeDtypeStruct((kw, 3 * gw), F32)],
        semantics=("parallel", "parallel", "arbitrary"), operands=[proj, w, dact], name="qkvconv_bwd", comms=comms)
    return res, lands


def _sc_fwd(proj, w, bsz, seq, scw, col0):
    nct = scw // LANE
    c0 = col0 // LANE
    kw = w.shape[0]

    def body(b_ref, c_ref, h_ref, w_ref, o_ref):
        o_ref[...] = (b_ref[...] * _conv(c_ref[...] * h_ref[...], w_ref[...])).astype(BF16)

    return pl.pallas_call(
        body, grid=(bsz, nct),
        in_specs=[pl.BlockSpec((seq, LANE), lambda b, c: (b, c0 + c)),
                  pl.BlockSpec((seq, LANE), lambda b, c: (b, c0 + nct + c)),
                  pl.BlockSpec((seq, LANE), lambda b, c: (b, c0 + 2 * nct + c)),
                  pl.BlockSpec((kw, LANE), lambda b, c: (0, c))],
        out_specs=pl.BlockSpec((seq, LANE), lambda b, c: (b, c)),
        out_shape=jax.ShapeDtypeStruct((bsz * seq, scw), BF16),
        compiler_params=_params("parallel", "parallel"), name="sc_fwd")(proj, proj, proj, w)


def _sc_bwd(proj, w, dout, bsz, seq, scw, col0, dcol0, comms=()):
    nct = scw // LANE
    c0 = col0 // LANE
    d0 = dcol0 // LANE
    kw = w.shape[0]

    def body(b_ref, c_ref, h_ref, w_ref, d_ref, db_ref, dc_ref, dh_ref, dw_ref):
        cc, hh, wv, dv = c_ref[...], h_ref[...], w_ref[...], d_ref[...]
        m = cc * hh
        db_ref[...] = (dv * _conv(m, wv)).astype(BF16)
        dm, dw = _conv_bwd(m, wv, dv * b_ref[...])
        dc_ref[...] = (dm * hh).astype(BF16)
        dh_ref[...] = (dm * cc).astype(BF16)
        b = pl.program_id(1)

        @pl.when(b == 0)
        def _():
            dw_ref[...] = dw

        @pl.when(b > 0)
        def _():
            dw_ref[...] += dw

    res, lands = _pcall(
        body, grid=(nct, bsz),
        in_specs=[pl.BlockSpec((seq, LANE), lambda c, b: (b, c0 + c)),
                  pl.BlockSpec((seq, LANE), lambda c, b: (b, c0 + nct + c)),
                  pl.BlockSpec((seq, LANE), lambda c, b: (b, c0 + 2 * nct + c)),
                  pl.BlockSpec((kw, LANE), lambda c, b: (0, c)),
                  pl.BlockSpec((seq, LANE), lambda c, b: (b, d0 + c))],
        out_specs=[pl.BlockSpec((seq, LANE), lambda c, b: (b, c)),
                   pl.BlockSpec((seq, LANE), lambda c, b: (b, c)),
                   pl.BlockSpec((seq, LANE), lambda c, b: (b, c)),
                   pl.BlockSpec((kw, LANE), lambda c, b: (0, c))],
        out_shape=[jax.ShapeDtypeStruct((bsz * seq, scw), BF16)] * 3 + [jax.ShapeDtypeStruct((kw, scw), F32)],
        semantics=("parallel", "arbitrary"), operands=[proj, proj, proj, w, dout], name="sc_bwd", comms=comms)
    return res, lands


HEADS_PER_STEP = 16


def _colsel(tile, idx):
    lane = lax.broadcasted_iota(jnp.int32, tile.shape, 1)
    return jnp.sum(jnp.where(lane == idx, tile, 0.0), axis=1, keepdims=True)


def _rowsel(tile, idx):
    row = lax.broadcasted_iota(jnp.int32, tile.shape, 0)
    return jnp.sum(jnp.where(row == idx, tile, 0.0), axis=0, keepdims=True)


def _colput(col, idx, width=LANE):
    lane = lax.broadcasted_iota(jnp.int32, (col.shape[0], width), 1)
    return jnp.where(lane == idx, col, 0.0)


def _tri_masks(c):
    row = lax.broadcasted_iota(jnp.int32, (c, c), 0)
    col = lax.broadcasted_iota(jnp.int32, (c, c), 1)
    return row >= col, row > col, row == col


def _unit_lower_inverses(ms):
    c = ms[0].shape[0]
    _, _, eye = _tri_masks(c)
    ps = [-m for m in ms]
    ts = [jnp.where(eye, 1.0, 0.0) + p for p in ps]
    for _ in range(int(math.log2(c)) - 1):
        ps = [_mdot(p, p) for p in ps]
        ts = [t + _mdot(t, p) for t, p in zip(ts, ps)]
    return ts


def _gates(ab, alog, dtb):
    g = -jnp.exp(alog) * _softplus(ab + dtb)
    return g, _sigmoid(ab)


def _l2n(x):
    r = lax.rsqrt(jnp.sum(x * x, axis=-1, keepdims=True) + L2_EPS)
    return x * r, r


def _gdn_chunk_common(q, k, gc, gr, bc):
    c, dk = q.shape
    incl, strict, _ = _tri_masks(c)
    qh, rq = _l2n(q)
    kn, rk = _l2n(k)
    qn = qh * (dk ** -0.5)
    dm = jnp.where(incl, jnp.exp(jnp.where(incl, gc - gr, 0.0)), 0.0)
    kk = _bdot(kn, kn, NT)
    m = jnp.where(strict, bc * kk * dm, 0.0)
    pm = jnp.where(incl, _bdot(qn, kn, NT) * dm, 0.0)
    return qh, rq, kn, rk, qn, dm, kk, m, pm


def _gdn_fwd(qkv, proj, alog, dtb, gnw, bsz, seq, heads, z_col0, ab_col0, comms=()):
    c = CHUNK
    nch = seq // c
    hb = min(HEADS_PER_STEP, heads)
    ng = heads // hb
    hd = qkv.shape[2] // heads
    wb = hb * hd

    def body(qkv_ref, z_ref, ab_ref, alog_ref, dtb_ref, gnw_ref, o_ref, ssave_ref, tsave_ref, s_ref, gc_ref, gt_ref, be_ref):
        n, hg = pl.program_id(1), pl.program_id(2)

        @pl.when((n == 0) & (hg == 0))
        def _():
            s_ref[...] = jnp.zeros_like(s_ref)

        @pl.when(hg == 0)
        def _():
            g, beta = _gates(ab_ref[...], alog_ref[...], dtb_ref[...])
            incl, _, _ = _tri_masks(c)
            gcum = _hdot(jnp.where(incl, 1.0, 0.0), g)
            gc_ref[...] = gcum
            gt_ref[...] = gcum.T
            be_ref[...] = beta

        gc_t, gt_t, be_t, gnw_v = gc_ref[...], gt_ref[...], be_ref[...], gnw_ref[...]
        hs = range(hb)
        sls = [slice(hh * hd, (hh + 1) * hd) for hh in hs]
        states = [s_ref[hg * hb + hh] for hh in hs]
        gcs = [_colsel(gc_t, hg * hb + hh) for hh in hs]
        grs = [_rowsel(gt_t, hg * hb + hh) for hh in hs]
        bcs = [_colsel(be_t, heads + hg * hb + hh) for hh in hs]
        com = [_gdn_chunk_common(qkv_ref[0, :, sls[hh]], qkv_ref[1, :, sls[hh]], gcs[hh], grs[hh], bcs[hh]) for hh in hs]
        kns, qns, pms = [cm[2] for cm in com], [cm[4] for cm in com], [cm[8] for cm in com]
        tms = _unit_lower_inverses([cm[7] for cm in com])
        gams = [jnp.exp(gc) for gc in gcs]
        glasts = [gc[c - 1:c, :] for gc in gcs]
        kss = [_bdot(kns[hh], states[hh]) for hh in hs]
        qss = [_bdot(qns[hh], states[hh]) for hh in hs]
        vns = [_bdot(tms[hh], bcs[hh] * (qkv_ref[2, :, sls[hh]] - gams[hh] * kss[hh])) for hh in hs]
        os_ = [gams[hh] * qss[hh] + _bdot(pms[hh], vns[hh]) for hh in hs]
        snews = [states[hh] * jnp.exp(glasts[hh]) + _bdot(kns[hh] * jnp.exp(glasts[hh] - gcs[hh]), vns[hh], TN) for hh in hs]
        for hh in hs:
            o = os_[hh]
            on = o * lax.rsqrt(jnp.mean(o * o, axis=-1, keepdims=True) + NORM_EPS) * gnw_v
            zz = z_ref[:, sls[hh]]
            ssave_ref[0, 0, hh] = states[hh]
            tsave_ref[0, 0, hh] = tms[hh]
            s_ref[hg * hb + hh] = snews[hh]
            o_ref[:, sls[hh]] = (on * (zz * _sigmoid(zz))).astype(BF16)

    row = lambda b, n, g: b * nch + n
    return _pcall(
        body, grid=(bsz, nch, ng),
        in_specs=[pl.BlockSpec((3, c, wb), lambda b, n, g: (0, row(b, n, g), g)),
                  pl.BlockSpec((c, wb), lambda b, n, g: (row(b, n, g), z_col0 // wb + g)),
                  pl.BlockSpec((c, LANE), lambda b, n, g: (row(b, n, g), ab_col0 // LANE)),
                  pl.BlockSpec((1, LANE), lambda b, n, g: (0, 0)),
                  pl.BlockSpec((1, LANE), lambda b, n, g: (0, 0)),
                  pl.BlockSpec((1, hd), lambda b, n, g: (0, 0))],
        out_specs=[pl.BlockSpec((c, wb), lambda b, n, g: (row(b, n, g), g)),
                   pl.BlockSpec((1, 1, hb, hd, hd), lambda b, n, g: (b, n, g, 0, 0)),
                   pl.BlockSpec((1, 1, hb, c, c), lambda b, n, g: (b, n, g, 0, 0))],
        out_shape=[jax.ShapeDtypeStruct((bsz * seq, heads * hd), BF16),
                   jax.ShapeDtypeStruct((bsz, nch, heads, hd, hd), F32),
                   jax.ShapeDtypeStruct((bsz, nch, heads, c, c), F32)],
        scratch_shapes=[pltpu.VMEM((heads, hd, hd), F32), pltpu.VMEM((c, LANE), F32), pltpu.VMEM((LANE, c), F32),
                        pltpu.VMEM((c, LANE), F32)],
        semantics=("parallel", "arbitrary", "arbitrary"), operands=[qkv, proj, proj, alog, dtb, gnw], name="gdn_fwd",
        comms=comms)


def _gdn_bwd(qkv, proj, alog, dtb, gnw, ssave, tsave, dout, bsz, seq, heads, z_col0, ab_col0, comms=()):
    c = CHUNK
    nch = seq // c
    hb = min(HEADS_PER_STEP, heads)
    ng = heads // hb
    hd = qkv.shape[2] // heads
    wb = hb * hd

    def body(qkv_ref, z_ref, ab_ref, alog_ref, dtb_ref, gnw_ref, ssave_ref, tsave_ref, do_ref,
             dact_ref, dz_ref, dab_ref, dalog_ref, ddtb_ref, dgnw_ref,
             ds_ref, gc_ref, gt_ref, be_ref, dgacc_ref, dbacc_ref):
        n, hg = pl.program_id(1), pl.program_id(2)
        incl, strict, _ = _tri_masks(c)

        @pl.when((n == 0) & (hg == 0))
        def _():
            ds_ref[...] = jnp.zeros_like(ds_ref)
            dalog_ref[...] = jnp.zeros_like(dalog_ref)
            ddtb_ref[...] = jnp.zeros_like(ddtb_ref)
            dgnw_ref[...] = jnp.zeros_like(dgnw_ref)

        @pl.when(hg == 0)
        def _():
            g, beta = _gates(ab_ref[...], alog_ref[...], dtb_ref[...])
            gcum = _hdot(jnp.where(incl, 1.0, 0.0), g)
            gc_ref[...] = gcum
            gt_ref[...] = gcum.T
            be_ref[...] = beta
            dgacc_ref[...] = jnp.zeros_like(dgacc_ref)
            dbacc_ref[...] = jnp.zeros_like(dbacc_ref)

        gc_t, gt_t, be_t, gnw_v = gc_ref[...], gt_ref[...], be_ref[...], gnw_ref[...]
        hs = range(hb)

        def each(f):
            return [f(hh) for hh in hs]

        rsum = lambda a: jnp.sum(a, axis=-1, keepdims=True)
        sls = each(lambda i: slice(i * hd, (i + 1) * hd))
        ds_in = each(lambda i: ds_ref[hg * hb + i])
        gc = each(lambda i: _colsel(gc_t, hg * hb + i))
        gr = each(lambda i: _rowsel(gt_t, hg * hb + i))
        bc = each(lambda i: _colsel(be_t, heads + hg * hb + i))
        com = each(lambda i: _gdn_chunk_common(qkv_ref[0, :, sls[i]], qkv_ref[1, :, sls[i]], gc[i], gr[i], bc[i]))
        qh, rq, kn, rk, qn, dm, kk, m, pm = [[cm[j] for cm in com] for j in range(9)]
        tm = each(lambda i: tsave_ref[0, 0, i])
        s = each(lambda i: ssave_ref[0, 0, i])
        gam = each(lambda i: jnp.exp(gc[i]))
        glast = each(lambda i: gc[i][c - 1:c, :])
        gl = each(lambda i: jnp.exp(glast[i]))
        ratio = each(lambda i: jnp.exp(glast[i] - gc[i]))
        ks = each(lambda i: _bdot(kn[i], s[i]))
        qs = each(lambda i: _bdot(qn[i], s[i]))
        r = each(lambda i: qkv_ref[2, :, sls[i]] - gam[i] * ks[i])
        vn = each(lambda i: _bdot(tm[i], bc[i] * r[i]))
        o = each(lambda i: gam[i] * qs[i] + _bdot(pm[i], vn[i]))
        ro = each(lambda i: lax.rsqrt(jnp.mean(o[i] * o[i], axis=-1, keepdims=True) + NORM_EPS))
        zz = each(lambda i: z_ref[:, sls[i]])
        sz = each(lambda i: _sigmoid(zz[i]))
        dd = each(lambda i: do_ref[:, sls[i]])
        don = each(lambda i: dd[i] * (zz[i] * sz[i]))
        dz_h = each(lambda i: (dd[i] * (o[i] * ro[i] * gnw_v) * (sz[i] * (1.0 + zz[i] * (1.0 - sz[i])))).astype(BF16))
        dgnw = sum(each(lambda i: jnp.sum(don[i] * o[i] * ro[i], axis=0, keepdims=True)))
        uu = each(lambda i: don[i] * gnw_v)
        d_o = each(lambda i: ro[i] * uu[i] - o[i] * (ro[i] * ro[i] * ro[i]) * jnp.mean(o[i] * uu[i], axis=-1, keepdims=True))
        dqs = each(lambda i: gam[i] * d_o[i])
        dq = each(lambda i: _bdot(dqs[i], s[i], NT))
        ds_new = each(lambda i: _bdot(qn[i], dqs[i], TN))
        dp = each(lambda i: jnp.where(incl, _bdot(d_o[i], vn[i], NT), 0.0))
        dvn = each(lambda i: _bdot(pm[i], d_o[i], TN))
        dgam = each(lambda i: rsum(d_o[i] * qs[i]))
        dkd = each(lambda i: _bdot(vn[i], ds_in[i], NT))
        dvn = each(lambda i: dvn[i] + _bdot(kn[i] * ratio[i], ds_in[i]))
        ds_new = each(lambda i: ds_new[i] + gl[i] * ds_in[i])
        dgl = each(lambda i: jnp.sum(jnp.sum(ds_in[i] * s[i], axis=1, keepdims=True), axis=0, keepdims=True))
        dratio = each(lambda i: rsum(dkd[i] * kn[i]))
        dpd = each(lambda i: dp[i] * dm[i])
        dq = each(lambda i: dq[i] + _bdot(dpd[i], kn[i]))
        dk = each(lambda i: ratio[i] * dkd[i] + _bdot(dpd[i], qn[i], TN))
        dx = each(lambda i: _bdot(tm[i], dvn[i], TN))
        dr = each(lambda i: bc[i] * dx[i])
        gdr = each(lambda i: gam[i] * dr[i])
        dk = each(lambda i: dk[i] - _bdot(gdr[i], s[i], NT))
        ds_new = each(lambda i: ds_new[i] - _bdot(kn[i], gdr[i], TN))
        dmm = each(lambda i: jnp.where(strict, -_bdot(dx[i], vn[i], NT), 0.0))
        ee = each(lambda i: dmm[i] * dm[i])
        be_e = each(lambda i: bc[i] * ee[i])
        dk = each(lambda i: dk[i] + _bdot(be_e[i], kn[i]) + _bdot(be_e[i], kn[i], TN))
        dbeta = each(lambda i: rsum(dx[i] * r[i]) + rsum(ee[i] * kk[i]))
        dgam = each(lambda i: dgam[i] - rsum(dr[i] * ks[i]))
        ff = each(lambda i: dp[i] * pm[i] + dmm[i] * m[i])
        rowi = lax.broadcasted_iota(jnp.int32, (c, 1), 0)
        dgc = each(lambda i: rsum(ff[i]) - rsum(ff[i].T) + dgam[i] * gam[i] - dratio[i] * ratio[i]
                   + jnp.where(rowi == c - 1, jnp.sum(dratio[i] * ratio[i], axis=0, keepdims=True) + dgl[i] * gl[i], 0.0))
        dg_tile = sum(each(lambda i: _colput(dgc[i], hg * hb + i)))
        db_tile = sum(each(lambda i: _colput(dbeta[i], heads + hg * hb + i)))
        for i in hs:
            dqh = dq[i] * (hd ** -0.5)
            ds_ref[hg * hb + i] = ds_new[i]
            dz_ref[:, sls[i]] = dz_h[i]
            dact_ref[0, :, sls[i]] = rq[i] * (dqh - qh[i] * rsum(qh[i] * dqh))
            dact_ref[1, :, sls[i]] = rk[i] * (dk[i] - kn[i] * rsum(kn[i] * dk[i]))
            dact_ref[2, :, sls[i]] = dr[i]
        dgacc_ref[...] += dg_tile
        dbacc_ref[...] += db_tile
        dgnw_ref[0] += dgnw

        @pl.when(hg == ng - 1)
        def _():
            ab = ab_ref[...]
            ea = jnp.exp(alog_ref[...])
            g = -ea * _softplus(ab + dtb_ref[...])
            beta = be_ref[...]
            dg = _hdot(jnp.where(incl, 1.0, 0.0), dgacc_ref[...], TN)
            lane = lax.broadcasted_iota(jnp.int32, ab.shape, 1)
            da = jnp.where(lane < heads, dg * (-ea) * _sigmoid(ab + dtb_ref[...]), 0.0)
            db = dbacc_ref[...] * beta * (1.0 - beta)
            dab_ref[...] = (da + db).astype(BF16)
            dalog_ref[0] += jnp.sum(jnp.where(lane < heads, dg * g, 0.0), axis=0, keepdims=True)
            ddtb_ref[0] += jnp.sum(da, axis=0, keepdims=True)

    row = lambda b, n, g: b * nch + (nch - 1 - n)
    rev = lambda n: nch - 1 - n
    return _pcall(
        body, grid=(bsz, nch, ng),
        in_specs=[pl.BlockSpec((3, c, wb), lambda b, n, g: (0, row(b, n, g), g)),
                  pl.BlockSpec((c, wb), lambda b, n, g: (row(b, n, g), z_col0 // wb + g)),
                  pl.BlockSpec((c, LANE), lambda b, n, g: (row(b, n, g), ab_col0 // LANE)),
                  pl.BlockSpec((1, LANE), lambda b, n, g: (0, 0)),
                  pl.BlockSpec((1, LANE), lambda b, n, g: (0, 0)),
                  pl.BlockSpec((1, hd), lambda b, n, g: (0, 0)),
                  pl.BlockSpec((1, 1, hb, hd, hd), lambda b, n, g: (b, rev(n), g, 0, 0)),
                  pl.BlockSpec((1, 1, hb, c, c), lambda b, n, g: (b, rev(n), g, 0, 0)),
                  pl.BlockSpec((c, wb), lambda b, n, g: (row(b, n, g), g))],
        out_specs=[pl.BlockSpec((3, c, wb), lambda b, n, g: (0, row(b, n, g), g)),
                   pl.BlockSpec((c, wb), lambda b, n, g: (row(b, n, g), g)),
                   pl.BlockSpec((c, LANE), lambda b, n, g: (row(b, n, g), 0)),
                   pl.BlockSpec((1, 1, LANE), lambda b, n, g: (b, 0, 0)),
                   pl.BlockSpec((1, 1, LANE), lambda b, n, g: (b, 0, 0)),
                   pl.BlockSpec((1, 1, hd), lambda b, n, g: (b, 0, 0))],
        out_shape=[jax.ShapeDtypeStruct((3, bsz * seq, heads * hd), F32),
                   jax.ShapeDtypeStruct((bsz * seq, heads * hd), BF16),
                   jax.ShapeDtypeStruct((bsz * seq, LANE), BF16),
                   jax.ShapeDtypeStruct((bsz, 1, LANE), F32),
                   jax.ShapeDtypeStruct((bsz, 1, LANE), F32),
                   jax.ShapeDtypeStruct((bsz, 1, hd), F32)],
        scratch_shapes=[pltpu.VMEM((heads, hd, hd), F32), pltpu.VMEM((c, LANE), F32), pltpu.VMEM((LANE, c), F32),
                        pltpu.VMEM((c, LANE), F32), pltpu.VMEM((c, LANE), F32), pltpu.VMEM((c, LANE), F32)],
        semantics=("parallel", "arbitrary", "arbitrary"),
        operands=[qkv, proj, proj, alog, dtb, gnw, ssave, tsave, dout], name="gdn_bwd", comms=comms)


ELEMWISE_BLOCK_ELEMS = 256 * 1024


def _rows_tile(rows, cols):
    want = max(16, ELEMWISE_BLOCK_ELEMS // cols)
    if rows <= want:
        return rows
    t = (want // 16) * 16
    while t > 16 and rows % t:
        t -= 16
    return t if rows % t == 0 else rows


def _piece_specs(pieces, tr, cols):
    specs, leads = [], []
    for p, (arr, lead) in enumerate(pieces):
        if arr.ndim == 3:
            specs.append(pl.BlockSpec((1, tr, cols), functools.partial(lambda i, idx, p: (idx[p], i, 0), p=p)))
        else:
            specs.append(pl.BlockSpec((tr, cols), lambda i, idx: (i, 0)))
        leads.append(jnp.asarray(0 if lead is None else lead, jnp.int32))
    return jnp.stack(leads), specs


def _sum_pieces(refs):
    total = None
    for r in refs:
        v = r[...].astype(F32)
        v = v.reshape(v.shape[-2:])
        total = v if total is None else total + v
    return total


def _adamw(w, m, v, pieces, name, comms=()):
    rows, cols = w.shape
    tr = _rows_tile(rows, cols)
    leads, pspecs = _piece_specs(pieces, tr, cols)
    npc = len(pieces)
    c1 = 1.0 - ADAM_B1 ** ADAM_STEP
    c2 = 1.0 - ADAM_B2 ** ADAM_STEP

    def body(idx_ref, w_ref, m_ref, v_ref, *rest):
        g = _sum_pieces(rest[:npc])
        g_ref, d_ref, nm_ref, nv_ref = rest[npc:]
        nm = ADAM_B1 * m_ref[...] + (1.0 - ADAM_B1) * g
        nv = ADAM_B2 * v_ref[...] + (1.0 - ADAM_B2) * (g * g)
        g_ref[...] = g
        nm_ref[...] = nm
        nv_ref[...] = nv
        d_ref[...] = -ADAM_LR * ((nm / c1) / (jnp.sqrt(nv / c2) + ADAM_EPS) + ADAM_WD * w_ref[...])

    wspec = pl.BlockSpec((tr, cols), lambda i, idx: (i, 0))
    res, lands = _pcall(body, grid=(rows // tr,), in_specs=[wspec] * 3 + pspecs, out_specs=[wspec] * 4,
                        out_shape=[jax.ShapeDtypeStruct((rows, cols), F32)] * 4, semantics=("parallel",),
                        prefetch=[leads], operands=[w, m, v, *[p for p, _ in pieces]], name=name, comms=comms)
    return (res, lands) if comms else res


def _sum_to(pieces, out_dtype, name):
    arr0 = pieces[0][0]
    rows, cols = arr0.shape[-2:]
    tr = _rows_tile(rows, cols)
    leads, pspecs = _piece_specs(pieces, tr, cols)

    def body(idx_ref, *rest):
        rest[-1][...] = _sum_pieces(rest[:-1]).astype(out_dtype)

    return pl.pallas_call(
        body,
        grid_spec=pltpu.PrefetchScalarGridSpec(num_scalar_prefetch=1, grid=(rows // tr,), in_specs=pspecs,
                                               out_specs=pl.BlockSpec((tr, cols), lambda i, idx: (i, 0))),
        out_shape=jax.ShapeDtypeStruct((rows, cols), out_dtype),
        compiler_params=_params("parallel"), name=name)(leads, *[p for p, _ in pieces])


def _pair_add(a, recv, place, name, comms=()):
    _, rows, cols = a.shape
    tr = _rows_tile(rows, cols)
    x, y, c = place
    idx = jnp.stack([2 * (1 - x) + y, 2 * x + (1 - y), 2 * (1 - x) + (1 - y), c]).astype(jnp.int32)

    def body(p_ref, a_ref, r_ref, o_ref):
        o_ref[...] = (a_ref[...].astype(F32) + r_ref[...].astype(F32)).astype(BF16)

    res, lands = _pcall(
        body, grid=(3, rows // tr),
        in_specs=[pl.BlockSpec((1, tr, cols), lambda j, i, p: (2 * p[j] + p[3], i, 0)),
                  pl.BlockSpec((1, tr, cols), lambda j, i, p: (p[j], i, 0))],
        out_specs=[pl.BlockSpec((1, tr, cols), lambda j, i, p: (j, i, 0))],
        out_shape=[jax.ShapeDtypeStruct((3, rows, cols), BF16)], semantics=("parallel", "parallel"),
        prefetch=[idx], operands=[a, recv], name=name, comms=comms)
    return (res[0], lands) if comms else res[0]


def _to_frame(w, offs, fw, name):
    rows, n = w.shape
    tr = _tile(rows, 256)

    def body(off_ref, w_ref, o_ref, pad_ref):
        pad_ref[...] = jnp.zeros_like(pad_ref)
        pad_ref[:, 0:n] = w_ref[...]
        y = pad_ref[...]
        off1, len1, off2 = off_ref[0], off_ref[1], off_ref[2]
        col = lax.broadcasted_iota(jnp.int32, y.shape, 1)
        o_ref[...] = jnp.where(col < off1 + len1, pltpu.roll(y, off1, axis=1),
                               jnp.where(col >= off2 + len1, pltpu.roll(y, off2, axis=1), 0.0)).astype(BF16)

    res, _ = _pcall(body, grid=(rows // tr,), in_specs=[pl.BlockSpec((tr, n), lambda i, o: (i, 0))],
                    out_specs=[pl.BlockSpec((tr, fw), lambda i, o: (i, 0))],
                    out_shape=[jax.ShapeDtypeStruct((rows, fw), BF16)], scratch_shapes=[pltpu.VMEM((tr, fw), F32)],
                    semantics=("parallel",), prefetch=[offs], operands=[w], name=name)
    return res[0]


def _in_proj_core(xn, frames, ids, lay, into, name, comms=()):
    t, d = xn.shape
    n = ids.shape[0]
    tm = _tile(t, 1024)
    nb = max(b for b in range(1, lay.nc + 1) if lay.nc % b == 0 and b * MXU <= 768)
    tn, nj = nb * MXU, lay.nc // nb

    def body(ids_ref, a_ref, b_ref, *rest):
        o_ref = rest[-1]
        o_ref[...] = jnp.dot(a_ref[...], b_ref[0], preferred_element_type=F32)

    col = lambda s, j: pl.multiple_of(lay.frame_block(s) * MXU + lay.c0 * MXU + j * tn, MXU)
    n_into = 0 if into is None else 1
    res, lands = _pcall(
        body, grid=(t // tm, n, nj),
        in_specs=[pl.BlockSpec((tm, d), lambda i, f, j, ids: (i, 0)),
                  pl.BlockSpec((pl.Element(1), pl.Element(d), pl.Element(tn)),
                               lambda i, f, j, ids: (ids[f], 0, pl.multiple_of(lay.c0 * MXU + j * tn, MXU)))]
        + [pl.BlockSpec(memory_space=pl.ANY)] * n_into,
        out_specs=[pl.BlockSpec((pl.Element(tm), pl.Element(tn)), lambda i, f, j, ids: (i * tm, col(ids[f], j)))],
        out_shape=[jax.ShapeDtypeStruct((t, lay.wp), F32)], semantics=("parallel", "arbitrary", "arbitrary"),
        prefetch=[ids], operands=[xn, frames] + ([into] if n_into else []), name=name, comms=comms,
        fill={2: 0} if n_into else None)
    return res[0], lands


def _in_proj_rest(xn, frames, table, wp, into, name, comms=()):
    t, d = xn.shape
    tm = _tile(t, 1024)
    n = table.shape[1]

    def body(t_ref, a_ref, b1_ref, b2_ref, *rest):
        j = pl.program_id(1)
        b = b1_ref[0]
        b = jnp.where(t_ref[5, j] > 0, b + b2_ref[0], b)
        rest[-1][...] = jnp.dot(a_ref[...], b, preferred_element_type=F32)

    n_into = 0 if into is None else 1
    res, lands = _pcall(
        body, grid=(t // tm, n),
        in_specs=[pl.BlockSpec((tm, d), lambda i, j, tb: (i, 0)),
                  pl.BlockSpec((1, d, MXU), lambda i, j, tb: (tb[1, j], 0, tb[2, j])),
                  pl.BlockSpec((1, d, MXU), lambda i, j, tb: (tb[3, j], 0, tb[4, j]))]
        + [pl.BlockSpec(memory_space=pl.ANY)] * n_into,
        out_specs=[pl.BlockSpec((tm, MXU), lambda i, j, tb: (i, tb[0, j]))],
        out_shape=[jax.ShapeDtypeStruct((t, wp), F32)], semantics=("parallel", "arbitrary"),
        prefetch=[table], operands=[xn, frames, frames] + ([into] if n_into else []), name=name, comms=comms,
        fill={3: 0} if n_into else None)
    return res[0], lands


def _place():
    x, y, c = lax.axis_index("x"), lax.axis_index("y"), lax.axis_index("c")
    chips = [(1 - x, y), (x, 1 - y), (1 - x, 1 - y)]
    return x, y, c, chips


def _allreduce_small(buf, name):
    rows = buf.shape[0]

    def body(x_ref, o_ref, g_ref, send_sems, recv_sems):
        x, y, c, chips = _place()
        me, sibling = (x, y, c), (x, y, 1 - c)

        def copy(k, block, to, src=None):
            dst = g_ref.at[4 * block[0] + 2 * block[1] + block[2]]
            return pltpu.make_async_remote_copy(src_ref=dst if src is None else src, dst_ref=dst,
                                                send_sem=send_sems.at[k], recv_sem=recv_sems.at[k],
                                                device_id=to, device_id_type=MESH)

        first = [copy(0, me, sibling, src=x_ref)]
        first += [copy(1 + j, me, (*chip, c), src=x_ref) for j, chip in enumerate(chips)]
        for cp in first:
            cp.start()
        passed = [copy(4 + j, (*chip, c), sibling) for j, chip in enumerate(chips)]
        for j, chip in enumerate(chips):
            copy(1 + j, (*chip, c), me).wait_recv()
            passed[j].start()
        copy(0, sibling, me).wait_recv()
        for j, chip in enumerate(chips):
            copy(4 + j, (*chip, 1 - c), me).wait_recv()
        for cp in first + passed:
            cp.wait_send()
        g_ref[4 * x + 2 * y + c] = x_ref[...]
        total = g_ref[0]
        for s in range(1, N_DEV):
            total = total + g_ref[s]
        o_ref[...] = total

    vm = pl.BlockSpec(memory_space=pltpu.VMEM)
    return pl.pallas_call(
        body, in_specs=[vm], out_specs=vm, out_shape=jax.ShapeDtypeStruct((rows, LANE), F32),
        scratch_shapes=[pltpu.VMEM((N_DEV, rows, LANE), F32), pltpu.SemaphoreType.DMA((7,)), pltpu.SemaphoreType.DMA((7,))],
        name=name)(buf)


def _rows(ref, rows):
    return ref if rows is None else ref.at[pl.ds(rows[0], rows[1] - rows[0])]


AG_ALL = ("here", "sibling", 0, 1, 2)


def _ag(shard=None, into=None, to=(), forward=(), rows=None):
    def plan(srcs, lands):
        x, y, c, chips = _place()
        buf = lands[0]
        out = []
        if to:
            src = _rows(srcs[0], rows)
            dst = _rows(buf.at[4 * x + 2 * y + c], rows)
            for who in to:
                if who == "here":
                    out.append(("local", src, dst, None))
                elif who == "sibling":
                    out.append(("remote", src, dst, (x, y, 1 - c)))
                else:
                    out.append(("remote", src, dst, (*chips[who], c)))
        for j in forward:
            r = _rows(buf.at[4 * chips[j][0] + 2 * chips[j][1] + c], rows)
            out.append(("remote", r, r, (x, y, 1 - c)))
        return out

    srcs = ([shard] if to else []) + ([into] if into is not None else [])
    land = jax.ShapeDtypeStruct(into.shape, into.dtype) if into is not None else jax.ShapeDtypeStruct((N_DEV,) + shard.shape, shard.dtype)
    return _Comm(srcs, [land], plan, len(to) + len(forward), alias={len(srcs) - 1: 0} if into is not None else None)


def _ag_first(shard, rows=None, into=None, to=AG_ALL):
    return _ag(shard=shard, into=into, to=to, rows=rows)


def _ag_second(g, rows=None, of=(0, 1, 2)):
    return _ag(into=g, forward=of, rows=rows)


def _rs_first(grad):
    def plan(srcs, lands):
        x, y, c, _ = _place()
        (a,), (land,) = srcs, lands
        return [("remote", a.at[2 * j + (1 - c)], land.at[j], (x, y, 1 - c)) for j in range(4)]

    return _Comm([grad], [jax.ShapeDtypeStruct((4,) + grad.shape[1:], grad.dtype)], plan, 4)


def _rs_second(pair, rows=None, into=None):
    def plan(srcs, lands):
        x, y, c, chips = _place()
        return [("remote", _rows(srcs[0].at[j], rows), _rows(lands[0].at[j], rows), (cx, cy, c))
                for j, (cx, cy) in enumerate(chips)]

    land = jax.ShapeDtypeStruct((3,) + pair.shape[1:], pair.dtype)
    if into is None:
        return _Comm([pair], [land], plan, 3)
    return _Comm([pair, into], [land], plan, 3, alias={1: 0})


class _InLayout:
    def __init__(self, n_in, gw, heads, scw):
        self.n_in, self.split = n_in, 4 * gw + 2 * heads
        self.gap = LANE - 2 * heads
        self.ab_col, self.sc_col = 4 * gw, 4 * gw + LANE
        self.used = 4 * gw + LANE + 3 * scw
        p0 = [s * n_in + (self.gap if s * n_in >= self.split else 0) for s in range(N_DEV)]
        self.fstart = [(p // MXU) * MXU for p in p0]
        need = []
        for s in range(N_DEV):
            straddle = s * n_in < self.split < (s + 1) * n_in
            need.append(p0[s] - self.fstart[s] + n_in + (self.gap if straddle else 0))
        self.fw = -(-max(need) // MXU) * MXU
        self.wp = max(f + self.fw for f in self.fstart)
        assert self.wp >= self.used and self.wp % MXU == 0
        nfb = self.fw // MXU
        rows = []
        for jb in range(self.wp // MXU):
            src = [(s, jb - self.fstart[s] // MXU) for s in range(N_DEV) if 0 <= jb - self.fstart[s] // MXU < nfb]
            assert 1 <= len(src) <= 2, (jb, src)
            (s1, b1), (s2, b2) = src[0], src[-1]
            rows.append((s1, b1, s2, b2, int(len(src) == 2)))
        self.table = np.asarray(rows, np.int32).T.copy()
        single = [all(rows[self.fstart[s] // MXU + b][4] == 0 for s in range(N_DEV)) for b in range(nfb)]
        runs, b = [], 0
        while b < nfb:
            if single[b]:
                e = b
                while e < nfb and single[e]:
                    e += 1
                runs.append((e - b, b))
                b = e
            else:
                b += 1
        self.nc, self.c0 = max(runs) if runs else (0, 0)
        in_core = {self.fstart[s] // MXU + b for s in range(N_DEV) for b in range(self.c0, self.c0 + self.nc)}
        self.rest = np.asarray([(jb,) + rows[jb] for jb in range(self.wp // MXU) if jb not in in_core], np.int32).T.copy()

    def frame_block(self, s):
        p = s * self.n_in
        return (p + jnp.where(p >= self.split, self.gap, 0)) // MXU

    def offsets(self, s):
        p = s * self.n_in
        after = p >= self.split
        off1 = p + jnp.where(after, self.gap, 0) - self.frame_block(s) * MXU
        len1 = jnp.where(after, self.n_in, jnp.clip(self.split - p, 0, self.n_in))
        off2 = off1 + jnp.where(len1 < self.n_in, self.gap, 0)
        return off1, len1, off2

    def to_frame(self, w, s):
        return _to_frame(w, jnp.stack(self.offsets(s)).astype(jnp.int32), self.fw, "w_in_frame")

    def from_frame(self, f, s):
        off1, len1, off2 = self.offsets(s)
        a = lax.dynamic_slice(f, (0, off1), (f.shape[0], self.n_in))
        b = lax.dynamic_slice(f, (0, off2), (f.shape[0], self.n_in))
        col = lax.broadcasted_iota(jnp.int32, (1, self.n_in), 1)
        return jnp.where(col < len1, a, b)


def _pack_rows(parts):
    rows = []
    for p in parts:
        flat = p.reshape(-1)
        pad = (-flat.shape[0]) % LANE
        rows.append(jnp.pad(flat, (0, pad)).reshape(-1, LANE))
    buf = jnp.concatenate(rows, axis=0)
    return jnp.pad(buf, ((0, (-buf.shape[0]) % 8), (0, 0)))


def _unpack_rows(buf, shapes):
    out, r = [], 0
    for shp in shapes:
        size = int(np.prod(shp))
        nr = -(-size // LANE)
        out.append(buf[r:r + nr].reshape(-1)[:size].reshape(shp))
        r += nr
    return out


def _pad_lanes(v):
    return jnp.pad(v, ((0, 0), (0, LANE - v.shape[1])))


def kernel(x, norm_mix_pre, w_in, conv_qkv_w, a_log, dt_bias, gdn_norm_w, conv_sc_w, w_out, norm_mix_post, norm_mlp_pre, w_up, w_down, norm_mlp_post, loss_target, m_norm_mix_pre, m_w_in, m_conv_qkv_w, m_a_log, m_dt_bias, m_gdn_norm_w, m_conv_sc_w, m_w_out, m_norm_mix_post, m_norm_mlp_pre, m_w_up, m_w_down, m_norm_mlp_post, v_norm_mix_pre, v_w_in, v_conv_qkv_w, v_a_log, v_dt_bias, v_gdn_norm_w, v_conv_sc_w, v_w_out, v_norm_mix_post, v_norm_mlp_pre, v_w_up, v_w_down, v_norm_mlp_post):
    bsz, seq, d = x.shape
    t = bsz * seq
    heads, hd = a_log.shape[-1], gdn_norm_w.shape[-1]
    gw = heads * hd
    scw = conv_sc_w.shape[-1] * N_DEV
    dff_w = w_up.shape[-1] * N_DEV
    lay = _InLayout(w_in.shape[-1], gw, heads, scw)
    mx, my, mc = lax.axis_index("x"), lax.axis_index("y"), lax.axis_index("c")
    me = 4 * mx + 2 * my + mc
    chip = 2 * mx + my

    x2 = x.reshape(t, d)
    tgt = loss_target.reshape(t, d)
    g1, g2, g3, g4 = norm_mix_pre, norm_mix_post, norm_mlp_pre, norm_mlp_post

    frame = lay.to_frame(w_in[0], me)
    w_out_b, w_up_b, w_down_b = w_out[0].astype(BF16), w_up[0].astype(BF16), w_down[0].astype(BF16)
    up_cols = dff_w // N_DEV
    qu, qd = d // 4, up_cols // 4
    kq, ks = conv_qkv_w.shape[1], conv_sc_w.shape[1]
    cq_n, cs_n = conv_qkv_w.shape[-1], conv_sc_w.shape[-1]
    cq_full = lax.dynamic_update_slice(jnp.zeros((kq, 3 * gw), F32), conv_qkv_w[0], (0, me * cq_n))
    cs_full = lax.dynamic_update_slice(jnp.zeros((ks, scw), F32), conv_sc_w[0], (0, me * cs_n))
    conv_q, conv_s = _unpack_rows(_allreduce_small(_pack_rows([cq_full, cs_full]), "allgather_conv"),
                                  [(kq, 3 * gw), (ks, scw)])
    alog_t, dtb_t = _pad_lanes(a_log), _pad_lanes(dt_bias)

    xn, (g_in,) = _rms_fwd(x2, g1, comms=[_ag(shard=frame, to=("here", "sibling"))])
    dev = lambda px, py, pc: (4 * px + 2 * py + pc).astype(jnp.int32)
    ids = [jnp.stack([dev(mx, my, mc), dev(mx, my, 1 - mc)]),
           jnp.stack([dev(1 - mx, my, mc), dev(mx, 1 - my, mc)]),
           jnp.stack([dev(1 - mx, my, 1 - mc), dev(mx, 1 - my, 1 - mc)]),
           jnp.stack([dev(1 - mx, 1 - my, mc), dev(1 - mx, 1 - my, 1 - mc)])]
    assert lay.nc > 0, "the frames have no columns of their own at these sizes"
    proj, (g_in,) = _in_proj_core(xn, g_in, ids[0], lay, None, "in_proj_0", comms=[_ag(shard=frame, into=g_in, to=(0, 1))])
    proj, (g_in,) = _in_proj_core(xn, g_in, ids[1], lay, proj, "in_proj_1",
                                  comms=[_ag(shard=frame, into=g_in, to=(2,), forward=(0, 1))])
    proj, (g_in, g_out) = _in_proj_core(xn, g_in, ids[2], lay, proj, "in_proj_2",
                                        comms=[_ag(into=g_in, forward=(2,)), _ag_first(w_out_b)])
    proj, (g_up,) = _in_proj_core(xn, g_in, ids[3], lay, proj, "in_proj_3", comms=[_ag_first(w_up_b, rows=(0, qu))])
    proj, (g_up,) = _in_proj_rest(xn, g_in, jnp.asarray(lay.rest), lay.wp, proj, "in_proj_rest",
                                  comms=[_ag_first(w_up_b, rows=(qu, 2 * qu), into=g_up)])
    qkv, (g_out, g_up) = _qkvconv_fwd(proj, conv_q, bsz, seq, gw,
                                      comms=[_ag_second(g_out), _ag_first(w_up_b, rows=(2 * qu, 3 * qu), into=g_up)])
    (gdn_out, ssave, tsave), (g_up, g_down) = _gdn_fwd(
        qkv, proj, alog_t, dtb_t, gdn_norm_w, bsz, seq, heads, 3 * gw, lay.ab_col,
        comms=[_ag_first(w_up_b, rows=(3 * qu, 4 * qu), into=g_up), _ag_first(w_down_b, rows=(0, qd))])
    sc_out = _sc_fwd(proj, conv_s, bsz, seq, scw, lay.sc_col)
    mixed = jnp.concatenate([gdn_out, sc_out], axis=1)
    w_out_f = g_out.reshape(d, d)
    (mix,), (g_up, g_down) = _matmul(mixed, w_out_f, mode="nn", out_dtypes=[F32], name="out_proj",
                                     comms=[_ag_second(g_up), _ag_first(w_down_b, rows=(qd, 2 * qd), into=g_down)])
    h, xn2 = _post1(x2, mix, g2, g3)

    def up_epilogue(acc):
        r = jnp.maximum(acc, 0.0)
        return r, r * r

    tq = t // 4
    act_hid = None
    for part in range(4):
        if part < 2:
            leg = [_ag_first(w_down_b, rows=((2 + part) * qd, (3 + part) * qd), into=g_down)]
        else:
            leg = [_ag_second(g_down)] if part == 2 else []
        res = _matmul(
            xn2, g_up, mode="nn", out_dtypes=[BF16, BF16], name="mlp_up_%d" % part, n_cols=dff_w, epilogue=up_epilogue,
            b_spec=lambda tk, tn: pl.BlockSpec((1, tk, tn), lambda i, j, k: (j // (up_cols // tn), k, j % (up_cols // tn))),
            a_rows=(part * tq, tq), out_into=act_hid, comms=leg)
        if leg:
            act_hid, (g_down,) = res
        else:
            act_hid = res
    act, hid = act_hid
    w_down_f = g_down.reshape(dff_w, d)
    (ff,) = _matmul(hid, w_down_f, mode="nn", out_dtypes=[F32], name="mlp_down", tn=1024, tk=2048)
    dff, dy, dg4, loss_p = _post2_loss(h, ff, g4, tgt)

    def pieces(part, sib, got):
        return [(part, me), (sib, chip), (got, 0), (got, 1), (got, 2)]

    place = (mx, my, mc)

    (dpre,) = _matmul(dff, w_down_f, mode="nt", out_dtypes=[BF16], name="d_hidden", extras=[act],
                      epilogue=lambda acc, a: (acc * (2.0 * a.astype(F32)),))
    (dw_down,) = _matmul(hid, dff, mode="tn", out_dtypes=[BF16], name="dw_down")
    dw_down = dw_down.reshape(N_DEV, dff_w // N_DEV, d)
    (dxn2,), (sib_down,) = _matmul(
        dpre, g_up, mode="nt", out_dtypes=[F32], name="d_xn2", n_cols=d, tn=1024, tk=min(up_cols, 2048),
        b_spec=lambda tk, tn: pl.BlockSpec((1, tn, tk), lambda i, j, k: (k // (up_cols // tk), j, k % (up_cols // tk))),
        comms=[_rs_first(dw_down)])
    pair_down = _pair_add(dw_down, sib_down, place, "pair_add_down")
    (dw_up,), (got_down,) = _matmul(
        xn2, dpre, mode="tn", out_dtypes=[BF16], name="dw_up",
        out_custom=lambda tm, tn: ((N_DEV, d, up_cols), (1, tm, tn), lambda i, j, k: (j // (up_cols // tn), i, j % (up_cols // tn))),
        comms=[_rs_second(pair_down, rows=(0, 3 * qd))])
    (dmix, dh, dg2, dg3), (got_down, sib_up) = _mid_bwd(
        h, mix, dy, dxn2, g2, g3, comms=[_rs_second(pair_down, rows=(3 * qd, 4 * qd), into=got_down), _rs_first(dw_up)])
    pair_up = _pair_add(dw_up, sib_up, place, "pair_add_up")
    (dmixed,) = _matmul(dmix, w_out_f, mode="nt", out_dtypes=[F32], name="d_mixed")
    (dw_out,), (got_up,) = _matmul(mixed, dmix, mode="tn", out_dtypes=[BF16], name="dw_out",
                                   comms=[_rs_second(pair_up, rows=(0, qu))])
    dw_out = dw_out.reshape(N_DEV, d // N_DEV, d)
    (dscb, dscc, dsch, dconv_s), (sib_out,) = _sc_bwd(proj, conv_s, dmixed, bsz, seq, scw, lay.sc_col, gw,
                                                      comms=[_rs_first(dw_out)])
    pair_out = _pair_add(dw_out, sib_out, place, "pair_add_out")
    (dact, dz, dab, dalog, ddtb, dgnw), (got_up,) = _gdn_bwd(
        qkv, proj, alog_t, dtb_t, gdn_norm_w, ssave, tsave, dmixed, bsz, seq, heads, 3 * gw, lay.ab_col,
        comms=[_rs_second(pair_up, rows=(qu, 3 * qu), into=got_up)])
    (dqkv, dconv_q), (got_up,) = _qkvconv_bwd(proj, conv_q, dact, bsz, seq, gw,
                                              comms=[_rs_second(pair_up, rows=(3 * qu, 4 * qu), into=got_up)])
    dproj = jnp.concatenate([dqkv, dz, dab, dscb, dscc, dsch, jnp.zeros((t, lay.wp - lay.used), BF16)], axis=1)
    nfb = lay.fw // MXU
    (dw_in,), (got_out,) = _matmul(
        xn, dproj, mode="tn", out_dtypes=[BF16], name="dw_in", n_cols=N_DEV * lay.fw, tn=MXU,
        b_spec=lambda tk, tn: pl.BlockSpec((tk, tn), lambda i, j, k: (k, lay.frame_block(j // nfb) + j % nfb)),
        out_custom=lambda tm, tn: ((N_DEV, d, lay.fw), (1, tm, tn), lambda i, j, k: (j // nfb, i, j % nfb)),
        comms=[_rs_second(pair_out)])
    tk_in = _tile(lay.fw, 1024)
    kpf = lay.fw // tk_in

    def d_xn(part, into, comms):
        return _matmul(
            dproj, g_in, mode="nt", out_dtypes=[F32], name="d_xn_%d" % part, n_cols=d, tn=1024, tk=tk_in,
            k_total=N_DEV * lay.fw, a_rows=(part * tq, tq), out_into=None if into is None else [into], comms=comms,
            a_spec=lambda tm, tk, r0: pl.BlockSpec(
                (pl.Element(tm), pl.Element(tk)),
                lambda i, j, k: (pl.multiple_of(i * tm + r0, 16),
                                 pl.multiple_of(lay.frame_block(k // kpf) * MXU + (k % kpf) * tk, LANE))),
            b_spec=lambda tk, tn: pl.BlockSpec((1, tn, tk), lambda i, j, k: (k // kpf, j, k % kpf)))

    (dxn,), (sib_in,) = d_xn(0, None, [_rs_first(dw_in)])
    pair_in = _pair_add(dw_in, sib_in, place, "pair_add_in")
    (dxn,), (got_in,) = d_xn(1, dxn, [_rs_second(pair_in, rows=(0, qu))])
    (dxn,), (got_in,) = d_xn(2, dxn, [_rs_second(pair_in, rows=(qu, 2 * qu), into=got_in)])
    (dxn,), (got_in,) = d_xn(3, dxn, [_rs_second(pair_in, rows=(2 * qu, 4 * qu), into=got_in)])
    grad_x, dg1 = _pre_bwd(x2, dh, dxn, g1)

    gin_frame = _sum_to(pieces(dw_in, sib_in, got_in), F32, "grad_w_in_frame")
    big = {
        "w_in": _adamw(w_in[0], m_w_in[0], v_w_in[0], [(lay.from_frame(gin_frame, me), None)], "adamw_w_in"),
        "w_out": _adamw(w_out[0], m_w_out[0], v_w_out[0], pieces(dw_out, sib_out, got_out), "adamw_w_out"),
        "w_up": _adamw(w_up[0], m_w_up[0], v_w_up[0], pieces(dw_up, sib_up, got_up), "adamw_w_up"),
        "w_down": _adamw(w_down[0], m_w_down[0], v_w_down[0], pieces(dw_down, sib_down, got_down), "adamw_w_down"),
    }

    small_shapes = [(kq, 3 * gw), (ks, scw), (1, d), (1, d), (1, d), (1, d), (1, LANE), (1, LANE), (1, hd), (1, LANE)]
    small = _unpack_rows(
        _allreduce_small(_pack_rows([dconv_q, dconv_s, dg1, dg2, dg3, dg4, jnp.sum(dalog, axis=0), jnp.sum(ddtb, axis=0),
                                     jnp.sum(dgnw, axis=0), loss_p]), "allreduce_small"), small_shapes)
    gq, gs, sg1, sg2, sg3, sg4, salog, sdtb, sgnw, sloss = small
    loss = sloss[0, 0]
    small_grads = {
        "norm_mix_pre": sg1, "conv_qkv_w": lax.dynamic_slice(gq, (0, me * cq_n), (kq, cq_n)),
        "a_log": salog[:, :heads], "dt_bias": sdtb[:, :heads], "gdn_norm_w": sgnw,
        "conv_sc_w": lax.dynamic_slice(gs, (0, me * cs_n), (ks, cs_n)),
        "norm_mix_post": sg2, "norm_mlp_pre": sg3, "norm_mlp_post": sg4,
    }
    weights = {"norm_mix_pre": (norm_mix_pre, m_norm_mix_pre, v_norm_mix_pre), "conv_qkv_w": (conv_qkv_w[0], m_conv_qkv_w[0], v_conv_qkv_w[0]),
               "a_log": (a_log, m_a_log, v_a_log), "dt_bias": (dt_bias, m_dt_bias, v_dt_bias),
               "gdn_norm_w": (gdn_norm_w, m_gdn_norm_w, v_gdn_norm_w), "conv_sc_w": (conv_sc_w[0], m_conv_sc_w[0], v_conv_sc_w[0]),
               "norm_mix_post": (norm_mix_post, m_norm_mix_post, v_norm_mix_post),
               "norm_mlp_pre": (norm_mlp_pre, m_norm_mlp_pre, v_norm_mlp_pre),
               "norm_mlp_post": (norm_mlp_post, m_norm_mlp_post, v_norm_mlp_post)}
    res = dict(big)
    for name, (w, m, v) in weights.items():
        res[name] = _adamw(w, m, v, [(small_grads[name], None)], "adamw_" + name)

    order = ["norm_mix_pre", "w_in", "conv_qkv_w", "a_log", "dt_bias", "gdn_norm_w", "conv_sc_w", "w_out", "norm_mix_post",
             "norm_mlp_pre", "w_up", "w_down", "norm_mlp_post"]
    shapes = {"norm_mix_pre": norm_mix_pre.shape, "w_in": w_in.shape, "conv_qkv_w": conv_qkv_w.shape, "a_log": a_log.shape,
              "dt_bias": dt_bias.shape, "gdn_norm_w": gdn_norm_w.shape, "conv_sc_w": conv_sc_w.shape, "w_out": w_out.shape,
              "norm_mix_post": norm_mix_post.shape, "norm_mlp_pre": norm_mlp_pre.shape, "w_up": w_up.shape,
              "w_down": w_down.shape, "norm_mlp_post": norm_mlp_post.shape}
    outs = [loss, grad_x.reshape(bsz, seq, d)]
    for part in range(4):
        outs += [res[nm][part].reshape(shapes[nm]) for nm in order]
    return tuple(outs)
```

```python
import functools
import math

import numpy as np
import jax
import jax.numpy as jnp
from jax import lax
from jax.experimental import pallas as pl
from jax.experimental.pallas import tpu as pltpu

F32 = jnp.float32
BF16 = jnp.bfloat16
HI = lax.Precision.HIGHEST
MESH = pl.DeviceIdType.MESH

N_DEV = 8
LANE = 128
MXU = 256
CHUNK = 64
NORM_EPS = 1e-6
L2_EPS = 1e-6
VMEM_LIMIT = 56 * 1024 * 1024

ADAM_LR = 0.001
ADAM_B1 = 0.9
ADAM_B2 = 0.999
ADAM_EPS = 1e-08
ADAM_WD = 0.01
ADAM_STEP = 10

NN = (((1,), (0,)), ((), ()))
NT = (((1,), (1,)), ((), ()))
TN = (((0,), (0,)), ((), ()))


def _params(*sem):
    return pltpu.CompilerParams(dimension_semantics=sem, vmem_limit_bytes=VMEM_LIMIT)


def _tile(n, want):
    if n <= want:
        return n
    t = (want // LANE) * LANE
    while t > LANE and n % t:
        t -= LANE
    assert n % t == 0, (n, want)
    return t


class _Comm:
    def __init__(self, srcs, lands, plan, n, alias=None):
        self.srcs, self.lands, self.plan, self.n, self.alias = list(srcs), list(lands), plan, n, dict(alias or {})


def _pcall(body, *, grid, in_specs, out_specs, out_shape, operands, name, scratch_shapes=(), semantics=None,
           prefetch=(), comms=(), fill=None):
    n_pf, n_in, n_out, n_scr = len(prefetch), len(in_specs), len(out_specs), len(scratch_shapes)
    srcs = [s for cm in comms for s in cm.srcs]
    lands = [l for cm in comms for l in cm.lands]
    n_src, n_land = len(srcs), len(lands)
    n_copies = sum(cm.n for cm in comms)
    aliases, so, lo = {n_pf + a: b for a, b in (fill or {}).items()}, 0, 0
    for cm in comms:
        for a, b in cm.alias.items():
            aliases[n_pf + n_in + so + a] = n_out + lo + b
        so, lo = so + len(cm.srcs), lo + len(cm.lands)
    any_spec = pl.BlockSpec(memory_space=pl.ANY)

    def wrapped(*refs):
        pf, r = refs[:n_pf], refs[n_pf:]
        ins, csrc = r[:n_in], r[n_in:n_in + n_src]
        outs = r[n_in + n_src:n_in + n_src + n_out]
        cland = r[n_in + n_src + n_out:n_in + n_src + n_out + n_land]
        rest = r[n_in + n_src + n_out + n_land:]
        scratch = rest[:n_scr]
        if not comms:
            body(*pf, *ins, *outs, *scratch)
            return
        send_sems, recv_sems = rest[n_scr:]

        def copies():
            out, k, s0, l0 = [], 0, 0, 0
            for cm in comms:
                for kind, src, dst, dev in cm.plan(csrc[s0:s0 + len(cm.srcs)], cland[l0:l0 + len(cm.lands)]):
                    if kind == "local":
                        out.append((kind, pltpu.make_async_copy(src, dst, send_sems.at[k])))
                    else:
                        out.append((kind, pltpu.make_async_remote_copy(
                            src_ref=src, dst_ref=dst, send_sem=send_sems.at[k], recv_sem=recv_sems.at[k],
                            device_id=dev, device_id_type=MESH)))
                    k += 1
                s0, l0 = s0 + len(cm.srcs), l0 + len(cm.lands)
            assert k == n_copies
            return out

        ids = [pl.program_id(a) for a in range(len(grid))]
        first = functools.reduce(jnp.logical_and, [i == 0 for i in ids])
        last = functools.reduce(jnp.logical_and, [i == g - 1 for i, g in zip(ids, grid)])

        @pl.when(first)
        def _():
            for _, cp in copies():
                cp.start()

        body(*pf, *ins, *outs, *scratch)

        @pl.when(last)
        def _():
            cps = copies()
            for kind, cp in cps:
                if kind == "remote":
                    cp.wait_recv()
            for kind, cp in cps:
                if kind == "remote":
                    cp.wait_send()
                else:
                    cp.wait()

    sems = [pltpu.SemaphoreType.DMA((n_copies,)), pltpu.SemaphoreType.DMA((n_copies,))] if comms else []
    if semantics is None or comms:
        semantics = ("arbitrary",) * len(grid)
    res = pl.pallas_call(
        wrapped,
        grid_spec=pltpu.PrefetchScalarGridSpec(
            num_scalar_prefetch=n_pf, grid=tuple(grid), in_specs=list(in_specs) + [any_spec] * n_src,
            out_specs=list(out_specs) + [any_spec] * n_land, scratch_shapes=list(scratch_shapes) + sems),
        out_shape=list(out_shape) + lands,
        input_output_aliases=aliases,
        compiler_params=_params(*semantics), name=name)(*prefetch, *operands, *srcs)
    return list(res[:n_out]), list(res[n_out:])


def _bdot(a, b, dims=NN):
    return lax.dot_general(a.astype(BF16), b.astype(BF16), dims, preferred_element_type=F32)


def _hdot(a, b, dims=NN):
    return lax.dot_general(a, b, dims, preferred_element_type=F32, precision=HI)


def _mdot(a, b, dims=NN):
    return lax.dot_general(a, b, dims, preferred_element_type=F32, precision=lax.Precision.HIGH)


def _sigmoid(x):
    return 1.0 / (1.0 + jnp.exp(-x))


def _softplus(x):
    return jnp.maximum(x, 0.0) + jnp.log(1.0 + jnp.exp(-jnp.abs(x)))


def _matmul(a, b, *, mode, out_dtypes, name, n_cols=None, tm=1024, tn=512, tk=4096, epilogue=None, extras=(),
            b_spec=None, out_custom=None, a_rows=None, out_into=None, a_spec=None, k_total=None, comms=()):
    if mode == "tn":
        K, M = a.shape
    else:
        M, K = a.shape
    if k_total is not None:
        K = k_total
    r0 = 0
    if a_rows is not None:
        r0, M = a_rows
    N = n_cols if n_cols is not None else (b.shape[0] if mode == "nt" else b.shape[1])
    tm, tk, tn = _tile(M, tm), _tile(K, tk), _tile(N, tn)
    assert r0 % tm == 0
    i0 = r0 // tm
    if b_spec is None:
        b_spec = pl.BlockSpec((tn, tk), lambda i, j, k: (j, k)) if mode == "nt" else pl.BlockSpec((tk, tn), lambda i, j, k: (k, j))
    else:
        b_spec = b_spec(tk, tn)
    gm, gn, nk = M // tm, N // tn, K // tk
    if a_rows is not None:
        out_shapes = [(a.shape[0], N)] * len(out_dtypes)
        out_blocks = [(tm, tn)] * len(out_dtypes)
        out_index = [lambda i, j, k: (i + i0, j)] * len(out_dtypes)
    elif out_custom is None:
        out_shapes = [(M, N)] * len(out_dtypes)
        out_blocks = [(tm, tn)] * len(out_dtypes)
        out_index = [lambda i, j, k: (i, j)] * len(out_dtypes)
    else:
        shape, blk, ix = out_custom(tm, tn)
        out_shapes, out_blocks, out_index = [shape] * len(out_dtypes), [blk] * len(out_dtypes), [ix] * len(out_dtypes)
    if a_spec is not None:
        a_spec = a_spec(tm, tk, r0)
    elif mode == "tn":
        a_spec = pl.BlockSpec((tk, tm), lambda i, j, k: (k, i))
    else:
        a_spec = pl.BlockSpec((tm, tk), lambda i, j, k: (i + i0, k))
    hoist = mode == "tn" and nk == 1 and gn > 1
    dims = {"nn": NN, "nt": NT, "tn": TN}[mode]
    n_ex, n_out = len(extras), len(out_dtypes)
    out_into = [] if out_into is None else list(out_into)
    n_into = len(out_into)
    assert n_into in (0, n_out)

    def body(a_ref, b_ref, *rest):
        ex, outs = rest[:n_ex], rest[n_ex + n_into:n_ex + n_into + n_out]

        def finish(acc):
            res = epilogue(acc, *[e[...] for e in ex]) if epilogue is not None else (acc,)
            for o, r in zip(outs, res):
                o[...] = r.reshape(o.shape).astype(o.dtype)

        bb = b_ref[...]
        bb = bb.reshape(bb.shape[-2:])
        if hoist:
            at_ref = rest[-1]

            @pl.when(pl.program_id(1) == 0)
            def _():
                at_ref[...] = a_ref[...].T

            finish(lax.dot_general(at_ref[...], bb, NN, preferred_element_type=F32))
            return
        part = lax.dot_general(a_ref[...], bb, dims, preferred_element_type=F32)
        if nk == 1:
            finish(part)
        else:
            acc = rest[-1]
            k = pl.program_id(2)

            @pl.when(k == 0)
            def _():
                acc[...] = part

            @pl.when(k > 0)
            def _():
                acc[...] += part

            @pl.when(k == nk - 1)
            def _():
                finish(acc[...])

    scratch = [pltpu.VMEM((tm, tk), BF16)] if hoist else ([pltpu.VMEM((tm, tn), F32)] if nk > 1 else [])
    outs, lands = _pcall(
        body, grid=(gm, gn, nk),
        in_specs=([a_spec, b_spec] + [pl.BlockSpec((tm, tn), lambda i, j, k: (i, j)) for _ in extras]
                  + [pl.BlockSpec(memory_space=pl.ANY)] * n_into),
        out_specs=[pl.BlockSpec(blk, ix) for blk, ix in zip(out_blocks, out_index)],
        out_shape=[jax.ShapeDtypeStruct(s, d) for s, d in zip(out_shapes, out_dtypes)],
        scratch_shapes=scratch, semantics=("parallel", "arbitrary", "arbitrary"),
        operands=[a, b, *extras, *out_into], name=name, comms=comms,
        fill={2 + n_ex + o: o for o in range(n_into)})
    return (outs, lands) if comms else outs


TR = 128


def _rms(x):
    return lax.rsqrt(jnp.mean(x * x, axis=-1, keepdims=True) + NORM_EPS)


def _rms_bwd(x, r, w, dy):
    u = dy * w
    dx = r * u - x * (r * r * r) * jnp.mean(x * u, axis=-1, keepdims=True)
    return dx, dy * x * r


def _row_call(body, ins, row_flags, outs, name, n_rows, comms=()):
    tr = min(TR, n_rows)
    in_specs = []
    for arr, is_row in zip(ins, row_flags):
        if is_row:
            in_specs.append(pl.BlockSpec((tr, arr.shape[1]), lambda i: (i, 0)))
        else:
            in_specs.append(pl.BlockSpec(arr.shape, lambda i: (0, 0)))
    out_specs, out_shape = [], []
    for shape, dtype, kind in outs:
        if kind == "row":
            out_specs.append(pl.BlockSpec((tr, shape[1]), lambda i: (i, 0)))
        else:
            out_specs.append(pl.BlockSpec(shape, lambda i: (0, 0)))
        out_shape.append(jax.ShapeDtypeStruct(shape, dtype))
    res, lands = _pcall(body, grid=(n_rows // tr,), in_specs=in_specs, out_specs=out_specs, out_shape=out_shape,
                        operands=list(ins), name=name, comms=comms)
    return (res, lands) if comms else res


def _acc_out(ref, val):
    @pl.when(pl.program_id(0) == 0)
    def _():
        ref[...] = val

    @pl.when(pl.program_id(0) > 0)
    def _():
        ref[...] += val


def _rms_fwd(x, g, comms):
    T, D = x.shape

    def body(x_ref, g_ref, o_ref):
        xv = x_ref[...]
        o_ref[...] = (xv * _rms(xv) * g_ref[...]).astype(BF16)

    res, lands = _row_call(body, [x, g], [True, False], [((T, D), BF16, "row")], "rms_fwd", T, comms=comms)
    return res[0], lands


def _post1(x, mix, g2, g3, comms=()):
    T, D = x.shape

    def body(x_ref, mix_ref, g2_ref, g3_ref, h_ref, xn2_ref):
        mv = mix_ref[...]
        h = x_ref[...] + mv * _rms(mv) * g2_ref[...]
        h_ref[...] = h
        xn2_ref[...] = (h * _rms(h) * g3_ref[...]).astype(BF16)

    return _row_call(body, [x, mix, g2, g3], [True, True, False, False],
                     [((T, D), F32, "row"), ((T, D), BF16, "row")], "post1", T, comms=comms)


def _post2_loss(h, ff, g4, target):
    T, D = h.shape

    def body(h_ref, ff_ref, g4_ref, t_ref, dff_ref, dy_ref, dg4_ref, loss_ref):
        fv = ff_ref[...]
        r = _rms(fv)
        err = h_ref[...] + fv * r * g4_ref[...] - t_ref[...]
        dy = err * (1.0 / D)
        dy_ref[...] = dy
        dff, dwt = _rms_bwd(fv, r, g4_ref[...], dy)
        dff_ref[...] = dff.astype(BF16)
        _acc_out(dg4_ref, jnp.sum(dwt, axis=0, keepdims=True))
        part = 0.5 * jnp.sum(jnp.mean(err * err, axis=-1, keepdims=True), axis=0, keepdims=True)
        _acc_out(loss_ref, jnp.broadcast_to(part, (1, LANE)))

    return _row_call(body, [h, ff, g4, target], [True, True, False, True],
                     [((T, D), BF16, "row"), ((T, D), F32, "row"), ((1, D), F32, "acc"), ((1, LANE), F32, "acc")],
                     "post2_loss", T)


def _mid_bwd(h, mix, dy, dxn2, g2, g3, comms=()):
    T, D = h.shape

    def body(h_ref, mix_ref, dy_ref, dxn2_ref, g2_ref, g3_ref, dmix_ref, dh_ref, dg2_ref, dg3_ref):
        hv = h_ref[...]
        d1, dw3 = _rms_bwd(hv, _rms(hv), g3_ref[...], dxn2_ref[...])
        dh = dy_ref[...] + d1
        dh_ref[...] = dh
        mv = mix_ref[...]
        dmix, dw2 = _rms_bwd(mv, _rms(mv), g2_ref[...], dh)
        dmix_ref[...] = dmix.astype(BF16)
        _acc_out(dg2_ref, jnp.sum(dw2, axis=0, keepdims=True))
        _acc_out(dg3_ref, jnp.sum(dw3, axis=0, keepdims=True))

    return _row_call(body, [h, mix, dy, dxn2, g2, g3], [True, True, True, True, False, False],
                     [((T, D), BF16, "row"), ((T, D), F32, "row"), ((1, D), F32, "acc"), ((1, D), F32, "acc")],
                     "mid_bwd", T, comms=comms)


def _pre_bwd(x, dh, dxn, g1, comms=()):
    T, D = x.shape

    def body(x_ref, dh_ref, dxn_ref, g1_ref, gx_ref, dg1_ref):
        xv = x_ref[...]
        d1, dw1 = _rms_bwd(xv, _rms(xv), g1_ref[...], dxn_ref[...])
        gx_ref[...] = dh_ref[...] + d1
        _acc_out(dg1_ref, jnp.sum(dw1, axis=0, keepdims=True))

    return _row_call(body, [x, dh, dxn, g1], [True, True, True, False],
                     [((T, D), F32, "row"), ((1, D), F32, "acc")], "pre_bwd", T, comms=comms)


def _shift_down(x, s):
    if s == 0:
        return x
    row = lax.broadcasted_iota(jnp.int32, x.shape, 0)
    return jnp.where(row >= s, pltpu.roll(x, s, axis=0), 0.0)


def _shift_up(x, s):
    if s == 0:
        return x
    n = x.shape[0]
    row = lax.broadcasted_iota(jnp.int32, x.shape, 0)
    return jnp.where(row < n - s, pltpu.roll(x, n - s, axis=0), 0.0)


def _conv(x, w):
    kw = w.shape[0]
    out = w[kw - 1:kw, :] * x
    for j in range(kw - 1):
        out = out + w[j:j + 1, :] * _shift_down(x, kw - 1 - j)
    return out


def _conv_bwd(x, w, dout):
    kw = w.shape[0]
    dx = w[kw - 1:kw, :] * dout
    dws = []
    for j in range(kw - 1):
        dx = dx + w[j:j + 1, :] * _shift_up(dout, kw - 1 - j)
        dws.append(jnp.sum(dout * _shift_down(x, kw - 1 - j), axis=0, keepdims=True))
    dws.append(jnp.sum(dout * x, axis=0, keepdims=True))
    return dx, jnp.concatenate(dws, axis=0)


def _qkvconv_fwd(proj, w, bsz, seq, gw, comms=()):
    nct = gw // LANE
    kw = w.shape[0]

    def body(p_ref, w_ref, o_ref):
        cv = _conv(p_ref[...], w_ref[...])
        o_ref[...] = (cv * _sigmoid(cv)).reshape(o_ref.shape)

    res, lands = _pcall(
        body, grid=(3, bsz, nct),
        in_specs=[pl.BlockSpec((seq, LANE), lambda p, b, c: (b, p * nct + c)),
                  pl.BlockSpec((kw, LANE), lambda p, b, c: (0, p * nct + c))],
        out_specs=[pl.BlockSpec((1, seq, LANE), lambda p, b, c: (p, b, c))],
        out_shape=[jax.ShapeDtypeStruct((3, bsz * seq, gw), F32)],
        semantics=("parallel", "parallel", "parallel"), operands=[proj, w], name="qkvconv_fwd", comms=comms)
    return res[0], lands


def _qkvconv_bwd(proj, w, dact, bsz, seq, gw, comms=()):
    nct = gw // LANE
    kw = w.shape[0]

    def body(p_ref, w_ref, d_ref, dp_ref, dw_ref):
        pre = p_ref[...]
        wv = w_ref[...]
        cv = _conv(pre, wv)
        sg = _sigmoid(cv)
        dcv = d_ref[...].reshape(cv.shape) * (sg * (1.0 + cv * (1.0 - sg)))
        dpre, dw = _conv_bwd(pre, wv, dcv)
        dp_ref[...] = dpre.astype(BF16)
        b = pl.program_id(2)

        @pl.when(b == 0)
        def _():
            dw_ref[...] = dw

        @pl.when(b > 0)
        def _():
            dw_ref[...] += dw

    res, lands = _pcall(
        body, grid=(3, nct, bsz),
        in_specs=[pl.BlockSpec((seq, LANE), lambda p, c, b: (b, p * nct + c)),
                  pl.BlockSpec((kw, LANE), lambda p, c, b: (0, p * nct + c)),
                  pl.BlockSpec((1, seq, LANE), lambda p, c, b: (p, b, c))],
        out_specs=[pl.BlockSpec((seq, LANE), lambda p, c, b: (b, p * nct + c)),
                   pl.BlockSpec((kw, LANE), lambda p, c, b: (0, p * nct + c))],
        out_shape=[jax.ShapeDtypeStruct((bsz * seq, 3 * gw), BF16), jax.ShapeDtypeStruct((kw, 3 * gw), F32)],
        semantics=("parallel", "parallel", "arbitrary"), operands=[proj, w, dact], name="qkvconv_bwd", comms=comms)
    return res, lands


def _sc_fwd(proj, w, bsz, seq, scw, col0):
    nct = scw // LANE
    c0 = col0 // LANE
    kw = w.shape[0]

    def body(b_ref, c_ref, h_ref, w_ref, o_ref):
        o_ref[...] = (b_ref[...] * _conv(c_ref[...] * h_ref[...], w_ref[...])).astype(BF16)

    return pl.pallas_call(
        body, grid=(bsz, nct),
        in_specs=[pl.BlockSpec((seq, LANE), lambda b, c: (b, c0 + c)),
                  pl.BlockSpec((seq, LANE), lambda b, c: (b, c0 + nct + c)),
                  pl.BlockSpec((seq, LANE), lambda b, c: (b, c0 + 2 * nct + c)),
                  pl.BlockSpec((kw, LANE), lambda b, c: (0, c))],
        out_specs=pl.BlockSpec((seq, LANE), lambda b, c: (b, c)),
        out_shape=jax.ShapeDtypeStruct((bsz * seq, scw), BF16),
        compiler_params=_params("parallel", "parallel"), name="sc_fwd")(proj, proj, proj, w)


def _sc_bwd(proj, w, dout, bsz, seq, scw, col0, dcol0, comms=()):
    nct = scw // LANE
    c0 = col0 // LANE
    d0 = dcol0 // LANE
    kw = w.shape[0]

    def body(b_ref, c_ref, h_ref, w_ref, d_ref, db_ref, dc_ref, dh_ref, dw_ref):
        cc, hh, wv, dv = c_ref[...], h_ref[...], w_ref[...], d_ref[...]
        m = cc * hh
        db_ref[...] = (dv * _conv(m, wv)).astype(BF16)
        dm, dw = _conv_bwd(m, wv, dv * b_ref[...])
        dc_ref[...] = (dm * hh).astype(BF16)
        dh_ref[...] = (dm * cc).astype(BF16)
        b = pl.program_id(1)

        @pl.when(b == 0)
        def _():
            dw_ref[...] = dw

        @pl.when(b > 0)
        def _():
            dw_ref[...] += dw

    res, lands = _pcall(
        body, grid=(nct, bsz),
        in_specs=[pl.BlockSpec((seq, LANE), lambda c, b: (b, c0 + c)),
                  pl.BlockSpec((seq, LANE), lambda c, b: (b, c0 + nct + c)),
                  pl.BlockSpec((seq, LANE), lambda c, b: (b, c0 + 2 * nct + c)),
                  pl.BlockSpec((kw, LANE), lambda c, b: (0, c)),
                  pl.BlockSpec((seq, LANE), lambda c, b: (b, d0 + c))],
        out_specs=[pl.BlockSpec((seq, LANE), lambda c, b: (b, c)),
                   pl.BlockSpec((seq, LANE), lambda c, b: (b, c)),
                   pl.BlockSpec((seq, LANE), lambda c, b: (b, c)),
                   pl.BlockSpec((kw, LANE), lambda c, b: (0, c))],
        out_shape=[jax.ShapeDtypeStruct((bsz * seq, scw), BF16)] * 3 + [jax.ShapeDtypeStruct((kw, scw), F32)],
        semantics=("parallel", "arbitrary"), operands=[proj, proj, proj, w, dout], name="sc_bwd", comms=comms)
    return res, lands


HEADS_PER_STEP = 16


def _colsel(tile, idx):
    lane = lax.broadcasted_iota(jnp.int32, tile.shape, 1)
    return jnp.sum(jnp.where(lane == idx, tile, 0.0), axis=1, keepdims=True)


def _rowsel(tile, idx):
    row = lax.broadcasted_iota(jnp.int32, tile.shape, 0)
    return jnp.sum(jnp.where(row == idx, tile, 0.0), axis=0, keepdims=True)


def _colput(col, idx, width=LANE):
    lane = lax.broadcasted_iota(jnp.int32, (col.shape[0], width), 1)
    return jnp.where(lane == idx, col, 0.0)


def _tri_masks(c):
    row = lax.broadcasted_iota(jnp.int32, (c, c), 0)
    col = lax.broadcasted_iota(jnp.int32, (c, c), 1)
    return row >= col, row > col, row == col


def _unit_lower_inverses(ms):
    c = ms[0].shape[0]
    _, _, eye = _tri_masks(c)
    ps = [-m for m in ms]
    ts = [jnp.where(eye, 1.0, 0.0) + p for p in ps]
    for _ in range(int(math.log2(c)) - 1):
        ps = [_mdot(p, p) for p in ps]
        ts = [t + _mdot(t, p) for t, p in zip(ts, ps)]
    return ts


def _gates(ab, alog, dtb):
    g = -jnp.exp(alog) * _softplus(ab + dtb)
    return g, _sigmoid(ab)


def _l2n(x):
    r = lax.rsqrt(jnp.sum(x * x, axis=-1, keepdims=True) + L2_EPS)
    return x * r, r


def _gdn_chunk_common(q, k, gc, gr, bc):
    c, dk = q.shape
    incl, strict, _ = _tri_masks(c)
    qh, rq = _l2n(q)
    kn, rk = _l2n(k)
    qn = qh * (dk ** -0.5)
    dm = jnp.where(incl, jnp.exp(jnp.where(incl, gc - gr, 0.0)), 0.0)
    kk = _bdot(kn, kn, NT)
    m = jnp.where(strict, bc * kk * dm, 0.0)
    pm = jnp.where(incl, _bdot(qn, kn, NT) * dm, 0.0)
    return qh, rq, kn, rk, qn, dm, kk, m, pm


def _gdn_fwd(qkv, proj, alog, dtb, gnw, bsz, seq, heads, z_col0, ab_col0, comms=()):
    c = CHUNK
    nch = seq // c
    hb = min(HEADS_PER_STEP, heads)
    ng = heads // hb
    hd = qkv.shape[2] // heads
    wb = hb * hd

    def body(qkv_ref, z_ref, ab_ref, alog_ref, dtb_ref, gnw_ref, o_ref, ssave_ref, tsave_ref, s_ref, gc_ref, gt_ref, be_ref):
        n, hg = pl.program_id(1), pl.program_id(2)

        @pl.when((n == 0) & (hg == 0))
        def _():
            s_ref[...] = jnp.zeros_like(s_ref)

        @pl.when(hg == 0)
        def _():
            g, beta = _gates(ab_ref[...], alog_ref[...], dtb_ref[...])
            incl, _, _ = _tri_masks(c)
            gcum = _hdot(jnp.where(incl, 1.0, 0.0), g)
            gc_ref[...] = gcum
            gt_ref[...] = gcum.T
            be_ref[...] = beta

        gc_t, gt_t, be_t, gnw_v = gc_ref[...], gt_ref[...], be_ref[...], gnw_ref[...]
        hs = range(hb)
        sls = [slice(hh * hd, (hh + 1) * hd) for hh in hs]
        states = [s_ref[hg * hb + hh] for hh in hs]
        gcs = [_colsel(gc_t, hg * hb + hh) for hh in hs]
        grs = [_rowsel(gt_t, hg * hb + hh) for hh in hs]
        bcs = [_colsel(be_t, heads + hg * hb + hh) for hh in hs]
        com = [_gdn_chunk_common(qkv_ref[0, :, sls[hh]], qkv_ref[1, :, sls[hh]], gcs[hh], grs[hh], bcs[hh]) for hh in hs]
        kns, qns, pms = [cm[2] for cm in com], [cm[4] for cm in com], [cm[8] for cm in com]
        tms = _unit_lower_inverses([cm[7] for cm in com])
        gams = [jnp.exp(gc) for gc in gcs]
        glasts = [gc[c - 1:c, :] for gc in gcs]
        kss = [_bdot(kns[hh], states[hh]) for hh in hs]
        qss = [_bdot(qns[hh], states[hh]) for hh in hs]
        vns = [_bdot(tms[hh], bcs[hh] * (qkv_ref[2, :, sls[hh]] - gams[hh] * kss[hh])) for hh in hs]
        os_ = [gams[hh] * qss[hh] + _bdot(pms[hh], vns[hh]) for hh in hs]
        snews = [states[hh] * jnp.exp(glasts[hh]) + _bdot(kns[hh] * jnp.exp(glasts[hh] - gcs[hh]), vns[hh], TN) for hh in hs]
        for hh in hs:
            o = os_[hh]
            on = o * lax.rsqrt(jnp.mean(o * o, axis=-1, keepdims=True) + NORM_EPS) * gnw_v
            zz = z_ref[:, sls[hh]]
            ssave_ref[0, 0, hh] = states[hh]
            tsave_ref[0, 0, hh] = tms[hh]
            s_ref[hg * hb + hh] = snews[hh]
            o_ref[:, sls[hh]] = (on * (zz * _sigmoid(zz))).astype(BF16)

    row = lambda b, n, g: b * nch + n
    return _pcall(
        body, grid=(bsz, nch, ng),
        in_specs=[pl.BlockSpec((3, c, wb), lambda b, n, g: (0, row(b, n, g), g)),
                  pl.BlockSpec((c, wb), lambda b, n, g: (row(b, n, g), z_col0 // wb + g)),
                  pl.BlockSpec((c, LANE), lambda b, n, g: (row(b, n, g), ab_col0 // LANE)),
                  pl.BlockSpec((1, LANE), lambda b, n, g: (0, 0)),
                  pl.BlockSpec((1, LANE), lambda b, n, g: (0, 0)),
                  pl.BlockSpec((1, hd), lambda b, n, g: (0, 0))],
        out_specs=[pl.BlockSpec((c, wb), lambda b, n, g: (row(b, n, g), g)),
                   pl.BlockSpec((1, 1, hb, hd, hd), lambda b, n, g: (b, n, g, 0, 0)),
                   pl.BlockSpec((1, 1, hb, c, c), lambda b, n, g: (b, n, g, 0, 0))],
        out_shape=[jax.ShapeDtypeStruct((bsz * seq, heads * hd), BF16),
                   jax.ShapeDtypeStruct((bsz, nch, heads, hd, hd), F32),
                   jax.ShapeDtypeStruct((bsz, nch, heads, c, c), F32)],
        scratch_shapes=[pltpu.VMEM((heads, hd, hd), F32), pltpu.VMEM((c, LANE), F32), pltpu.VMEM((LANE, c), F32),
                        pltpu.VMEM((c, LANE), F32)],
        semantics=("parallel", "arbitrary", "arbitrary"), operands=[qkv, proj, proj, alog, dtb, gnw], name="gdn_fwd",
        comms=comms)


def _gdn_bwd(qkv, proj, alog, dtb, gnw, ssave, tsave, dout, bsz, seq, heads, z_col0, ab_col0, comms=()):
    c = CHUNK
    nch = seq // c
    hb = min(HEADS_PER_STEP, heads)
    ng = heads // hb
    hd = qkv.shape[2] // heads
    wb = hb * hd

    def body(qkv_ref, z_ref, ab_ref, alog_ref, dtb_ref, gnw_ref, ssave_ref, tsave_ref, do_ref,
             dact_ref, dz_ref, dab_ref, dalog_ref, ddtb_ref, dgnw_ref,
             ds_ref, gc_ref, gt_ref, be_ref, dgacc_ref, dbacc_ref):
        n, hg = pl.program_id(1), pl.program_id(2)
        incl, strict, _ = _tri_masks(c)

        @pl.when((n == 0) & (hg == 0))
        def _():
            ds_ref[...] = jnp.zeros_like(ds_ref)
            dalog_ref[...] = jnp.zeros_like(dalog_ref)
            ddtb_ref[...] = jnp.zeros_like(ddtb_ref)
            dgnw_ref[...] = jnp.zeros_like(dgnw_ref)

        @pl.when(hg == 0)
        def _():
            g, beta = _gates(ab_ref[...], alog_ref[...], dtb_ref[...])
            gcum = _hdot(jnp.where(incl, 1.0, 0.0), g)
            gc_ref[...] = gcum
            gt_ref[...] = gcum.T
            be_ref[...] = beta
            dgacc_ref[...] = jnp.zeros_like(dgacc_ref)
            dbacc_ref[...] = jnp.zeros_like(dbacc_ref)

        gc_t, gt_t, be_t, gnw_v = gc_ref[...], gt_ref[...], be_ref[...], gnw_ref[...]
        hs = range(hb)

        def each(f):
            return [f(hh) for hh in hs]

        rsum = lambda a: jnp.sum(a, axis=-1, keepdims=True)
        sls = each(lambda i: slice(i * hd, (i + 1) * hd))
        ds_in = each(lambda i: ds_ref[hg * hb + i])
        gc = each(lambda i: _colsel(gc_t, hg * hb + i))
        gr = each(lambda i: _rowsel(gt_t, hg * hb + i))
        bc = each(lambda i: _colsel(be_t, heads + hg * hb + i))
        com = each(lambda i: _gdn_chunk_common(qkv_ref[0, :, sls[i]], qkv_ref[1, :, sls[i]], gc[i], gr[i], bc[i]))
        qh, rq, kn, rk, qn, dm, kk, m, pm = [[cm[j] for cm in com] for j in range(9)]
        tm = each(lambda i: tsave_ref[0, 0, i])
        s = each(lambda i: ssave_ref[0, 0, i])
        gam = each(lambda i: jnp.exp(gc[i]))
        glast = each(lambda i: gc[i][c - 1:c, :])
        gl = each(lambda i: jnp.exp(glast[i]))
        ratio = each(lambda i: jnp.exp(glast[i] - gc[i]))
        ks = each(lambda i: _bdot(kn[i], s[i]))
        qs = each(lambda i: _bdot(qn[i], s[i]))
        r = each(lambda i: qkv_ref[2, :, sls[i]] - gam[i] * ks[i])
        vn = each(lambda i: _bdot(tm[i], bc[i] * r[i]))
        o = each(lambda i: gam[i] * qs[i] + _bdot(pm[i], vn[i]))
        ro = each(lambda i: lax.rsqrt(jnp.mean(o[i] * o[i], axis=-1, keepdims=True) + NORM_EPS))
        zz = each(lambda i: z_ref[:, sls[i]])
        sz = each(lambda i: _sigmoid(zz[i]))
        dd = each(lambda i: do_ref[:, sls[i]])
        don = each(lambda i: dd[i] * (zz[i] * sz[i]))
        dz_h = each(lambda i: (dd[i] * (o[i] * ro[i] * gnw_v) * (sz[i] * (1.0 + zz[i] * (1.0 - sz[i])))).astype(BF16))
        dgnw = sum(each(lambda i: jnp.sum(don[i] * o[i] * ro[i], axis=0, keepdims=True)))
        uu = each(lambda i: don[i] * gnw_v)
        d_o = each(lambda i: ro[i] * uu[i] - o[i] * (ro[i] * ro[i] * ro[i]) * jnp.mean(o[i] * uu[i], axis=-1, keepdims=True))
        dqs = each(lambda i: gam[i] * d_o[i])
        dq = each(lambda i: _bdot(dqs[i], s[i], NT))
        ds_new = each(lambda i: _bdot(qn[i], dqs[i], TN))
        dp = each(lambda i: jnp.where(incl, _bdot(d_o[i], vn[i], NT), 0.0))
        dvn = each(lambda i: _bdot(pm[i], d_o[i], TN))
        dgam = each(lambda i: rsum(d_o[i] * qs[i]))
        dkd = each(lambda i: _bdot(vn[i], ds_in[i], NT))
        dvn = each(lambda i: dvn[i] + _bdot(kn[i] * ratio[i], ds_in[i]))
        ds_new = each(lambda i: ds_new[i] + gl[i] * ds_in[i])
        dgl = each(lambda i: jnp.sum(jnp.sum(ds_in[i] * s[i], axis=1, keepdims=True), axis=0, keepdims=True))
        dratio = each(lambda i: rsum(dkd[i] * kn[i]))
        dpd = each(lambda i: dp[i] * dm[i])
        dq = each(lambda i: dq[i] + _bdot(dpd[i], kn[i]))
        dk = each(lambda i: ratio[i] * dkd[i] + _bdot(dpd[i], qn[i], TN))
        dx = each(lambda i: _bdot(tm[i], dvn[i], TN))
        dr = each(lambda i: bc[i] * dx[i])
        gdr = each(lambda i: gam[i] * dr[i])
        dk = each(lambda i: dk[i] - _bdot(gdr[i], s[i], NT))
        ds_new = each(lambda i: ds_new[i] - _bdot(kn[i], gdr[i], TN))
        dmm = each(lambda i: jnp.where(strict, -_bdot(dx[i], vn[i], NT), 0.0))
        ee = each(lambda i: dmm[i] * dm[i])
        be_e = each(lambda i: bc[i] * ee[i])
        dk = each(lambda i: dk[i] + _bdot(be_e[i], kn[i]) + _bdot(be_e[i], kn[i], TN))
        dbeta = each(lambda i: rsum(dx[i] * r[i]) + rsum(ee[i] * kk[i]))
        dgam = each(lambda i: dgam[i] - rsum(dr[i] * ks[i]))
        ff = each(lambda i: dp[i] * pm[i] + dmm[i] * m[i])
        rowi = lax.broadcasted_iota(jnp.int32, (c, 1), 0)
        dgc = each(lambda i: rsum(ff[i]) - rsum(ff[i].T) + dgam[i] * gam[i] - dratio[i] * ratio[i]
                   + jnp.where(rowi == c - 1, jnp.sum(dratio[i] * ratio[i], axis=0, keepdims=True) + dgl[i] * gl[i], 0.0))
        dg_tile = sum(each(lambda i: _colput(dgc[i], hg * hb + i)))
        db_tile = sum(each(lambda i: _colput(dbeta[i], heads + hg * hb + i)))
        for i in hs:
            dqh = dq[i] * (hd ** -0.5)
            ds_ref[hg * hb + i] = ds_new[i]
            dz_ref[:, sls[i]] = dz_h[i]
            dact_ref[0, :, sls[i]] = rq[i] * (dqh - qh[i] * rsum(qh[i] * dqh))
            dact_ref[1, :, sls[i]] = rk[i] * (dk[i] - kn[i] * rsum(kn[i] * dk[i]))
            dact_ref[2, :, sls[i]] = dr[i]
        dgacc_ref[...] += dg_tile
        dbacc_ref[...] += db_tile
        dgnw_ref[0] += dgnw

        @pl.when(hg == ng - 1)
        def _():
            ab = ab_ref[...]
            ea = jnp.exp(alog_ref[...])
            g = -ea * _softplus(ab + dtb_ref[...])
            beta = be_ref[...]
            dg = _hdot(jnp.where(incl, 1.0, 0.0), dgacc_ref[...], TN)
            lane = lax.broadcasted_iota(jnp.int32, ab.shape, 1)
            da = jnp.where(lane < heads, dg * (-ea) * _sigmoid(ab + dtb_ref[...]), 0.0)
            db = dbacc_ref[...] * beta * (1.0 - beta)
            dab_ref[...] = (da + db).astype(BF16)
            dalog_ref[0] += jnp.sum(jnp.where(lane < heads, dg * g, 0.0), axis=0, keepdims=True)
            ddtb_ref[0] += jnp.sum(da, axis=0, keepdims=True)

    row = lambda b, n, g: b * nch + (nch - 1 - n)
    rev = lambda n: nch - 1 - n
    return _pcall(
        body, grid=(bsz, nch, ng),
        in_specs=[pl.BlockSpec((3, c, wb), lambda b, n, g: (0, row(b, n, g), g)),
                  pl.BlockSpec((c, wb), lambda b, n, g: (row(b, n, g), z_col0 // wb + g)),
                  pl.BlockSpec((c, LANE), lambda b, n, g: (row(b, n, g), ab_col0 // LANE)),
                  pl.BlockSpec((1, LANE), lambda b, n, g: (0, 0)),
                  pl.BlockSpec((1, LANE), lambda b, n, g: (0, 0)),
                  pl.BlockSpec((1, hd), lambda b, n, g: (0, 0)),
                  pl.BlockSpec((1, 1, hb, hd, hd), lambda b, n, g: (b, rev(n), g, 0, 0)),
                  pl.BlockSpec((1, 1, hb, c, c), lambda b, n, g: (b, rev(n), g, 0, 0)),
                  pl.BlockSpec((c, wb), lambda b, n, g: (row(b, n, g), g))],
        out_specs=[pl.BlockSpec((3, c, wb), lambda b, n, g: (0, row(b, n, g), g)),
                   pl.BlockSpec((c, wb), lambda b, n, g: (row(b, n, g), g)),
                   pl.BlockSpec((c, LANE), lambda b, n, g: (row(b, n, g), 0)),
                   pl.BlockSpec((1, 1, LANE), lambda b, n, g: (b, 0, 0)),
                   pl.BlockSpec((1, 1, LANE), lambda b, n, g: (b, 0, 0)),
                   pl.BlockSpec((1, 1, hd), lambda b, n, g: (b, 0, 0))],
        out_shape=[jax.ShapeDtypeStruct((3, bsz * seq, heads * hd), F32),
                   jax.ShapeDtypeStruct((bsz * seq, heads * hd), BF16),
                   jax.ShapeDtypeStruct((bsz * seq, LANE), BF16),
                   jax.ShapeDtypeStruct((bsz, 1, LANE), F32),
                   jax.ShapeDtypeStruct((bsz, 1, LANE), F32),
                   jax.ShapeDtypeStruct((bsz, 1, hd), F32)],
        scratch_shapes=[pltpu.VMEM((heads, hd, hd), F32), pltpu.VMEM((c, LANE), F32), pltpu.VMEM((LANE, c), F32),
                        pltpu.VMEM((c, LANE), F32), pltpu.VMEM((c, LANE), F32), pltpu.VMEM((c, LANE), F32)],
        semantics=("parallel", "arbitrary", "arbitrary"),
        operands=[qkv, proj, proj, alog, dtb, gnw, ssave, tsave, dout], name="gdn_bwd", comms=comms)


ELEMWISE_BLOCK_ELEMS = 256 * 1024


def _rows_tile(rows, cols):
    want = max(16, ELEMWISE_BLOCK_ELEMS // cols)
    if rows <= want:
        return rows
    t = (want // 16) * 16
    while t > 16 and rows % t:
        t -= 16
    return t if rows % t == 0 else rows


def _piece_specs(pieces, tr, cols):
    specs, leads = [], []
    for p, (arr, lead) in enumerate(pieces):
        if arr.ndim == 3:
            specs.append(pl.BlockSpec((1, tr, cols), functools.partial(lambda i, idx, p: (idx[p], i, 0), p=p)))
        else:
            specs.append(pl.BlockSpec((tr, cols), lambda i, idx: (i, 0)))
        leads.append(jnp.asarray(0 if lead is None else lead, jnp.int32))
    return jnp.stack(leads), specs


def _sum_pieces(refs):
    total = None
    for r in refs:
        v = r[...].astype(F32)
        v = v.reshape(v.shape[-2:])
        total = v if total is None else total + v
    return total


def _adamw(w, m, v, pieces, name, comms=()):
    rows, cols = w.shape
    tr = _rows_tile(rows, cols)
    leads, pspecs = _piece_specs(pieces, tr, cols)
    npc = len(pieces)
    c1 = 1.0 - ADAM_B1 ** ADAM_STEP
    c2 = 1.0 - ADAM_B2 ** ADAM_STEP

    def body(idx_ref, w_ref, m_ref, v_ref, *rest):
        g = _sum_pieces(rest[:npc])
        g_ref, d_ref, nm_ref, nv_ref = rest[npc:]
        nm = ADAM_B1 * m_ref[...] + (1.0 - ADAM_B1) * g
        nv = ADAM_B2 * v_ref[...] + (1.0 - ADAM_B2) * (g * g)
        g_ref[...] = g
        nm_ref[...] = nm
        nv_ref[...] = nv
        d_ref[...] = -ADAM_LR * ((nm / c1) / (jnp.sqrt(nv / c2) + ADAM_EPS) + ADAM_WD * w_ref[...])

    wspec = pl.BlockSpec((tr, cols), lambda i, idx: (i, 0))
    res, lands = _pcall(body, grid=(rows // tr,), in_specs=[wspec] * 3 + pspecs, out_specs=[wspec] * 4,
                        out_shape=[jax.ShapeDtypeStruct((rows, cols), F32)] * 4, semantics=("parallel",),
                        prefetch=[leads], operands=[w, m, v, *[p for p, _ in pieces]], name=name, comms=comms)
    return (res, lands) if comms else res


def _sum_to(pieces, out_dtype, name):
    arr0 = pieces[0][0]
    rows, cols = arr0.shape[-2:]
    tr = _rows_tile(rows, cols)
    leads, pspecs = _piece_specs(pieces, tr, cols)

    def body(idx_ref, *rest):
        rest[-1][...] = _sum_pieces(rest[:-1]).astype(out_dtype)

    return pl.pallas_call(
        body,
        grid_spec=pltpu.PrefetchScalarGridSpec(num_scalar_prefetch=1, grid=(rows // tr,), in_specs=pspecs,
                                               out_specs=pl.BlockSpec((tr, cols), lambda i, idx: (i, 0))),
        out_shape=jax.ShapeDtypeStruct((rows, cols), out_dtype),
        compiler_params=_params("parallel"), name=name)(leads, *[p for p, _ in pieces])


def _pair_add(a, recv, place, name, comms=()):
    _, rows, cols = a.shape
    tr = _rows_tile(rows, cols)
    x, y, c = place
    idx = jnp.stack([2 * (1 - x) + y, 2 * x + (1 - y), 2 * (1 - x) + (1 - y), c]).astype(jnp.int32)

    def body(p_ref, a_ref, r_ref, o_ref):
        o_ref[...] = (a_ref[...].astype(F32) + r_ref[...].astype(F32)).astype(BF16)

    res, lands = _pcall(
        body, grid=(3, rows // tr),
        in_specs=[pl.BlockSpec((1, tr, cols), lambda j, i, p: (2 * p[j] + p[3], i, 0)),
                  pl.BlockSpec((1, tr, cols), lambda j, i, p: (p[j], i, 0))],
        out_specs=[pl.BlockSpec((1, tr, cols), lambda j, i, p: (j, i, 0))],
        out_shape=[jax.ShapeDtypeStruct((3, rows, cols), BF16)], semantics=("parallel", "parallel"),
        prefetch=[idx], operands=[a, recv], name=name, comms=comms)
    return (res[0], lands) if comms else res[0]


def _to_frame(w, offs, fw, name):
    rows, n = w.shape
    tr = _tile(rows, 256)

    def body(off_ref, w_ref, o_ref, pad_ref):
        pad_ref[...] = jnp.zeros_like(pad_ref)
        pad_ref[:, 0:n] = w_ref[...]
        y = pad_ref[...]
        off1, len1, off2 = off_ref[0], off_ref[1], off_ref[2]
        col = lax.broadcasted_iota(jnp.int32, y.shape, 1)
        o_ref[0] = jnp.where(col < off1 + len1, pltpu.roll(y, off1, axis=1),
                             jnp.where(col >= off2 + len1, pltpu.roll(y, off2, axis=1), 0.0)).astype(BF16)

    res, _ = _pcall(body, grid=(rows // tr,), in_specs=[pl.BlockSpec((tr, n), lambda i, o: (i, 0))],
                    out_specs=[pl.BlockSpec((1, tr, fw), lambda i, o: (o[3], i, 0))],
                    out_shape=[jax.ShapeDtypeStruct((N_DEV, rows, fw), BF16)], scratch_shapes=[pltpu.VMEM((tr, fw), F32)],
                    semantics=("parallel",), prefetch=[offs], operands=[w], name=name)
    return res[0]


def _in_proj_core(xn, frames, ids, lay, into, name, comms=()):
    t, d = xn.shape
    n = ids.shape[0]
    tm = _tile(t, 1024)
    nb = max(b for b in range(1, lay.nc + 1) if lay.nc % b == 0 and b * MXU <= 768)
    tn, nj = nb * MXU, lay.nc // nb

    def body(ids_ref, a_ref, b_ref, *rest):
        o_ref = rest[-1]
        o_ref[...] = jnp.dot(a_ref[...], b_ref[0], preferred_element_type=F32)

    col = lambda s, j: pl.multiple_of(lay.frame_block(s) * MXU + lay.c0 * MXU + j * tn, MXU)
    n_into = 0 if into is None else 1
    res, lands = _pcall(
        body, grid=(t // tm, n, nj),
        in_specs=[pl.BlockSpec((tm, d), lambda i, f, j, ids: (i, 0)),
                  pl.BlockSpec((pl.Element(1), pl.Element(d), pl.Element(tn)),
                               lambda i, f, j, ids: (f, 0, pl.multiple_of(lay.c0 * MXU + j * tn, MXU)))]
        + [pl.BlockSpec(memory_space=pl.ANY)] * n_into,
        out_specs=[pl.BlockSpec((pl.Element(tm), pl.Element(tn)), lambda i, f, j, ids: (i * tm, col(ids[f], j)))],
        out_shape=[jax.ShapeDtypeStruct((t, lay.wp), F32)], semantics=("parallel", "arbitrary", "arbitrary"),
        prefetch=[ids], operands=[xn, frames] + ([into] if n_into else []), name=name, comms=comms,
        fill={2: 0} if n_into else None)
    return res[0], lands


def _in_proj_rest(xn, frames, table, wp, into, name, comms=()):
    t, d = xn.shape
    tm = _tile(t, 1024)
    n = table.shape[1]

    def body(t_ref, a_ref, b1_ref, b2_ref, *rest):
        j = pl.program_id(1)
        b = b1_ref[0]
        b = jnp.where(t_ref[5, j] > 0, b + b2_ref[0], b)
        rest[-1][...] = jnp.dot(a_ref[...], b, preferred_element_type=F32)

    n_into = 0 if into is None else 1
    res, lands = _pcall(
        body, grid=(t // tm, n),
        in_specs=[pl.BlockSpec((tm, d), lambda i, j, tb: (i, 0)),
                  pl.BlockSpec((1, d, MXU), lambda i, j, tb: (tb[1, j], 0, tb[2, j])),
                  pl.BlockSpec((1, d, MXU), lambda i, j, tb: (tb[3, j], 0, tb[4, j]))]
        + [pl.BlockSpec(memory_space=pl.ANY)] * n_into,
        out_specs=[pl.BlockSpec((tm, MXU), lambda i, j, tb: (i, tb[0, j]))],
        out_shape=[jax.ShapeDtypeStruct((t, wp), F32)], semantics=("parallel", "arbitrary"),
        prefetch=[table], operands=[xn, frames, frames] + ([into] if n_into else []), name=name, comms=comms,
        fill={3: 0} if n_into else None)
    return res[0], lands


def _place():
    x, y, c = lax.axis_index("x"), lax.axis_index("y"), lax.axis_index("c")
    chips = [(1 - x, y), (x, 1 - y), (1 - x, 1 - y)]
    return x, y, c, chips


def _allreduce_small(buf, name):
    rows = buf.shape[0]

    def body(x_ref, o_ref, g_ref, send_sems, recv_sems):
        x, y, c, chips = _place()
        me, sibling = (x, y, c), (x, y, 1 - c)

        def copy(k, block, to, src=None):
            dst = g_ref.at[4 * block[0] + 2 * block[1] + block[2]]
            return pltpu.make_async_remote_copy(src_ref=dst if src is None else src, dst_ref=dst,
                                                send_sem=send_sems.at[k], recv_sem=recv_sems.at[k],
                                                device_id=to, device_id_type=MESH)

        first = [copy(0, me, sibling, src=x_ref)]
        first += [copy(1 + j, me, (*chip, c), src=x_ref) for j, chip in enumerate(chips)]
        for cp in first:
            cp.start()
        passed = [copy(4 + j, (*chip, c), sibling) for j, chip in enumerate(chips)]
        for j, chip in enumerate(chips):
            copy(1 + j, (*chip, c), me).wait_recv()
            passed[j].start()
        copy(0, sibling, me).wait_recv()
        for j, chip in enumerate(chips):
            copy(4 + j, (*chip, 1 - c), me).wait_recv()
        for cp in first + passed:
            cp.wait_send()
        g_ref[4 * x + 2 * y + c] = x_ref[...]
        total = g_ref[0]
        for s in range(1, N_DEV):
            total = total + g_ref[s]
        o_ref[...] = total

    vm = pl.BlockSpec(memory_space=pltpu.VMEM)
    return pl.pallas_call(
        body, in_specs=[vm], out_specs=vm, out_shape=jax.ShapeDtypeStruct((rows, LANE), F32),
        scratch_shapes=[pltpu.VMEM((N_DEV, rows, LANE), F32), pltpu.SemaphoreType.DMA((7,)), pltpu.SemaphoreType.DMA((7,))],
        name=name)(buf)


def _rows(ref, rows):
    return ref if rows is None else ref.at[pl.ds(rows[0], rows[1] - rows[0])]


AG_ALL = ("here", "sibling", 0, 1, 2)


def _ag(shard=None, into=None, to=(), forward=(), rows=None):
    def plan(srcs, lands):
        x, y, c, chips = _place()
        buf = lands[0]
        out = []
        if to:
            dst = _rows(buf.at[4 * x + 2 * y + c], rows)
            src = dst if shard is None else _rows(srcs[0], rows)
            for who in to:
                if who == "here":
                    out.append(("local", src, dst, None))
                elif who == "sibling":
                    out.append(("remote", src, dst, (x, y, 1 - c)))
                else:
                    out.append(("remote", src, dst, (*chips[who], c)))
        for j in forward:
            r = _rows(buf.at[4 * chips[j][0] + 2 * chips[j][1] + c], rows)
            out.append(("remote", r, r, (x, y, 1 - c)))
        return out

    srcs = ([shard] if to and shard is not None else []) + ([into] if into is not None else [])
    land = jax.ShapeDtypeStruct(into.shape, into.dtype) if into is not None else jax.ShapeDtypeStruct((N_DEV,) + shard.shape, shard.dtype)
    return _Comm(srcs, [land], plan, len(to) + len(forward), alias={len(srcs) - 1: 0} if into is not None else None)


def _ag_first(shard, rows=None, into=None, to=AG_ALL):
    return _ag(shard=shard, into=into, to=to, rows=rows)


def _ag_second(g, rows=None, of=(0, 1, 2)):
    return _ag(into=g, forward=of, rows=rows)


def _rs_first(grad):
    def plan(srcs, lands):
        x, y, c, _ = _place()
        (a,), (land,) = srcs, lands
        return [("remote", a.at[2 * j + (1 - c)], land.at[j], (x, y, 1 - c)) for j in range(4)]

    return _Comm([grad], [jax.ShapeDtypeStruct((4,) + grad.shape[1:], grad.dtype)], plan, 4)


def _rs_second(pair, rows=None, into=None):
    def plan(srcs, lands):
        x, y, c, chips = _place()
        return [("remote", _rows(srcs[0].at[j], rows), _rows(lands[0].at[j], rows), (cx, cy, c))
                for j, (cx, cy) in enumerate(chips)]

    land = jax.ShapeDtypeStruct((3,) + pair.shape[1:], pair.dtype)
    if into is None:
        return _Comm([pair], [land], plan, 3)
    return _Comm([pair, into], [land], plan, 3, alias={1: 0})


class _InLayout:
    def __init__(self, n_in, gw, heads, scw):
        self.n_in, self.split = n_in, 4 * gw + 2 * heads
        self.gap = LANE - 2 * heads
        self.ab_col, self.sc_col = 4 * gw, 4 * gw + LANE
        self.used = 4 * gw + LANE + 3 * scw
        p0 = [s * n_in + (self.gap if s * n_in >= self.split else 0) for s in range(N_DEV)]
        self.fstart = [(p // MXU) * MXU for p in p0]
        need = []
        for s in range(N_DEV):
            straddle = s * n_in < self.split < (s + 1) * n_in
            need.append(p0[s] - self.fstart[s] + n_in + (self.gap if straddle else 0))
        self.fw = -(-max(need) // MXU) * MXU
        self.wp = max(f + self.fw for f in self.fstart)
        assert self.wp >= self.used and self.wp % MXU == 0
        nfb = self.fw // MXU
        rows = []
        for jb in range(self.wp // MXU):
            src = [(s, jb - self.fstart[s] // MXU) for s in range(N_DEV) if 0 <= jb - self.fstart[s] // MXU < nfb]
            assert 1 <= len(src) <= 2, (jb, src)
            (s1, b1), (s2, b2) = src[0], src[-1]
            rows.append((s1, b1, s2, b2, int(len(src) == 2)))
        self.table = np.asarray(rows, np.int32).T.copy()
        single = [all(rows[self.fstart[s] // MXU + b][4] == 0 for s in range(N_DEV)) for b in range(nfb)]
        runs, b = [], 0
        while b < nfb:
            if single[b]:
                e = b
                while e < nfb and single[e]:
                    e += 1
                runs.append((e - b, b))
                b = e
            else:
                b += 1
        self.nc, self.c0 = max(runs) if runs else (0, 0)
        in_core = {self.fstart[s] // MXU + b for s in range(N_DEV) for b in range(self.c0, self.c0 + self.nc)}
        self.rest = np.asarray([(jb,) + rows[jb] for jb in range(self.wp // MXU) if jb not in in_core], np.int32).T.copy()

    def frame_block(self, s):
        p = s * self.n_in
        return (p + jnp.where(p >= self.split, self.gap, 0)) // MXU

    def offsets(self, s):
        p = s * self.n_in
        after = p >= self.split
        off1 = p + jnp.where(after, self.gap, 0) - self.frame_block(s) * MXU
        len1 = jnp.where(after, self.n_in, jnp.clip(self.split - p, 0, self.n_in))
        off2 = off1 + jnp.where(len1 < self.n_in, self.gap, 0)
        return off1, len1, off2

    def to_frame(self, w, s):
        return _to_frame(w, jnp.stack(self.offsets(s) + (s,)).astype(jnp.int32), self.fw, "w_in_frame")

    def from_frame(self, f, s):
        off1, len1, off2 = self.offsets(s)
        a = lax.dynamic_slice(f, (0, off1), (f.shape[0], self.n_in))
        b = lax.dynamic_slice(f, (0, off2), (f.shape[0], self.n_in))
        col = lax.broadcasted_iota(jnp.int32, (1, self.n_in), 1)
        return jnp.where(col < len1, a, b)


def _pack_rows(parts):
    rows = []
    for p in parts:
        flat = p.reshape(-1)
        pad = (-flat.shape[0]) % LANE
        rows.append(jnp.pad(flat, (0, pad)).reshape(-1, LANE))
    buf = jnp.concatenate(rows, axis=0)
    return jnp.pad(buf, ((0, (-buf.shape[0]) % 8), (0, 0)))


def _unpack_rows(buf, shapes):
    out, r = [], 0
    for shp in shapes:
        size = int(np.prod(shp))
        nr = -(-size // LANE)
        out.append(buf[r:r + nr].reshape(-1)[:size].reshape(shp))
        r += nr
    return out


def _pad_lanes(v):
    return jnp.pad(v, ((0, 0), (0, LANE - v.shape[1])))


def kernel(x, norm_mix_pre, w_in, conv_qkv_w, a_log, dt_bias, gdn_norm_w, conv_sc_w, w_out, norm_mix_post, norm_mlp_pre, w_up, w_down, norm_mlp_post, loss_target, m_norm_mix_pre, m_w_in, m_conv_qkv_w, m_a_log, m_dt_bias, m_gdn_norm_w, m_conv_sc_w, m_w_out, m_norm_mix_post, m_norm_mlp_pre, m_w_up, m_w_down, m_norm_mlp_post, v_norm_mix_pre, v_w_in, v_conv_qkv_w, v_a_log, v_dt_bias, v_gdn_norm_w, v_conv_sc_w, v_w_out, v_norm_mix_post, v_norm_mlp_pre, v_w_up, v_w_down, v_norm_mlp_post):
    bsz, seq, d = x.shape
    t = bsz * seq
    heads, hd = a_log.shape[-1], gdn_norm_w.shape[-1]
    gw = heads * hd
    scw = conv_sc_w.shape[-1] * N_DEV
    dff_w = w_up.shape[-1] * N_DEV
    lay = _InLayout(w_in.shape[-1], gw, heads, scw)
    mx, my, mc = lax.axis_index("x"), lax.axis_index("y"), lax.axis_index("c")
    me = 4 * mx + 2 * my + mc
    chip = 2 * mx + my

    x2 = x.reshape(t, d)
    tgt = loss_target.reshape(t, d)
    g1, g2, g3, g4 = norm_mix_pre, norm_mix_post, norm_mlp_pre, norm_mlp_post

    g_in = lay.to_frame(w_in[0], me)
    w_out_b, w_up_b, w_down_b = w_out[0].astype(BF16), w_up[0].astype(BF16), w_down[0].astype(BF16)
    up_cols = dff_w // N_DEV
    qu, qd = d // 4, up_cols // 4
    kq, ks = conv_qkv_w.shape[1], conv_sc_w.shape[1]
    cq_n, cs_n = conv_qkv_w.shape[-1], conv_sc_w.shape[-1]
    cq_full = lax.dynamic_update_slice(jnp.zeros((kq, 3 * gw), F32), conv_qkv_w[0], (0, me * cq_n))
    cs_full = lax.dynamic_update_slice(jnp.zeros((ks, scw), F32), conv_sc_w[0], (0, me * cs_n))
    conv_q, conv_s = _unpack_rows(_allreduce_small(_pack_rows([cq_full, cs_full]), "allgather_conv"),
                                  [(kq, 3 * gw), (ks, scw)])
    alog_t, dtb_t = _pad_lanes(a_log), _pad_lanes(dt_bias)

    xn, (g_in,) = _rms_fwd(x2, g1, comms=[_ag(into=g_in, to=("sibling",))])

    def arrived(k):
        return jnp.stack([lax.dynamic_index_in_dim(g_in, s, 0, keepdims=False) for s in ids[k]])

    dev = lambda px, py, pc: (4 * px + 2 * py + pc).astype(jnp.int32)
    ids = [jnp.stack([dev(mx, my, mc), dev(mx, my, 1 - mc)]),
           jnp.stack([dev(1 - mx, my, mc), dev(mx, 1 - my, mc)]),
           jnp.stack([dev(1 - mx, my, 1 - mc), dev(mx, 1 - my, 1 - mc)]),
           jnp.stack([dev(1 - mx, 1 - my, mc), dev(1 - mx, 1 - my, 1 - mc)])]
    assert lay.nc > 0, "the frames have no columns of their own at these sizes"
    proj, (g_in,) = _in_proj_core(xn, arrived(0), ids[0], lay, None, "in_proj_0", comms=[_ag(into=g_in, to=(0, 1))])
    proj, (g_in,) = _in_proj_core(xn, arrived(1), ids[1], lay, proj, "in_proj_1",
                                  comms=[_ag(into=g_in, to=(2,), forward=(0, 1))])
    proj, (g_in, g_out) = _in_proj_core(xn, arrived(2), ids[2], lay, proj, "in_proj_2",
                                        comms=[_ag(into=g_in, forward=(2,)), _ag_first(w_out_b)])
    proj, (g_up,) = _in_proj_core(xn, arrived(3), ids[3], lay, proj, "in_proj_3", comms=[_ag_first(w_up_b, rows=(0, qu))])
    proj, (g_up,) = _in_proj_rest(xn, g_in, jnp.asarray(lay.rest), lay.wp, proj, "in_proj_rest",
                                  comms=[_ag_first(w_up_b, rows=(qu, 2 * qu), into=g_up)])
    qkv, (g_out, g_up) = _qkvconv_fwd(proj, conv_q, bsz, seq, gw,
                                      comms=[_ag_second(g_out), _ag_first(w_up_b, rows=(2 * qu, 3 * qu), into=g_up)])
    (gdn_out, ssave, tsave), (g_up, g_down) = _gdn_fwd(
        qkv, proj, alog_t, dtb_t, gdn_norm_w, bsz, seq, heads, 3 * gw, lay.ab_col,
        comms=[_ag_first(w_up_b, rows=(3 * qu, 4 * qu), into=g_up), _ag_first(w_down_b, rows=(0, qd))])
    sc_out = _sc_fwd(proj, conv_s, bsz, seq, scw, lay.sc_col)
    mixed = jnp.concatenate([gdn_out, sc_out], axis=1)
    w_out_f = g_out.reshape(d, d)
    (mix,), (g_up, g_down) = _matmul(mixed, w_out_f, mode="nn", out_dtypes=[F32], name="out_proj",
                                     comms=[_ag_second(g_up), _ag_first(w_down_b, rows=(qd, 2 * qd), into=g_down)])
    h, xn2 = _post1(x2, mix, g2, g3)

    def up_epilogue(acc):
        r = jnp.maximum(acc, 0.0)
        return r, r * r

    tq = t // 4
    act_hid = None
    for part in range(4):
        if part < 2:
            leg = [_ag_first(w_down_b, rows=((2 + part) * qd, (3 + part) * qd), into=g_down)]
        else:
            leg = [_ag_second(g_down)] if part == 2 else []
        res = _matmul(
            xn2, g_up, mode="nn", out_dtypes=[BF16, BF16], name="mlp_up_%d" % part, n_cols=dff_w, epilogue=up_epilogue,
            b_spec=lambda tk, tn: pl.BlockSpec((1, tk, tn), lambda i, j, k: (j // (up_cols // tn), k, j % (up_cols // tn))),
            a_rows=(part * tq, tq), out_into=act_hid, comms=leg)
        if leg:
            act_hid, (g_down,) = res
        else:
            act_hid = res
    act, hid = act_hid
    w_down_f = g_down.reshape(dff_w, d)
    (ff,) = _matmul(hid, w_down_f, mode="nn", out_dtypes=[F32], name="mlp_down", tn=1024, tk=2048)
    dff, dy, dg4, loss_p = _post2_loss(h, ff, g4, tgt)

    def pieces(part, sib, got):
        return [(part, me), (sib, chip), (got, 0), (got, 1), (got, 2)]

    place = (mx, my, mc)

    (dpre,) = _matmul(dff, w_down_f, mode="nt", out_dtypes=[BF16], name="d_hidden", extras=[act],
                      epilogue=lambda acc, a: (acc * (2.0 * a.astype(F32)),))
    (dw_down,) = _matmul(hid, dff, mode="tn", out_dtypes=[BF16], name="dw_down")
    dw_down = dw_down.reshape(N_DEV, dff_w // N_DEV, d)
    (dxn2,), (sib_down,) = _matmul(
        dpre, g_up, mode="nt", out_dtypes=[F32], name="d_xn2", n_cols=d, tn=1024, tk=min(up_cols, 2048),
        b_spec=lambda tk, tn: pl.BlockSpec((1, tn, tk), lambda i, j, k: (k // (up_cols // tk), j, k % (up_cols // tk))),
        comms=[_rs_first(dw_down)])
    pair_down = _pair_add(dw_down, sib_down, place, "pair_add_down")
    (dw_up,), (got_down,) = _matmul(
        xn2, dpre, mode="tn", out_dtypes=[BF16], name="dw_up",
        out_custom=lambda tm, tn: ((N_DEV, d, up_cols), (1, tm, tn), lambda i, j, k: (j // (up_cols // tn), i, j % (up_cols // tn))),
        comms=[_rs_second(pair_down, rows=(0, 3 * qd))])
    (dmix, dh, dg2, dg3), (got_down, sib_up) = _mid_bwd(
        h, mix, dy, dxn2, g2, g3, comms=[_rs_second(pair_down, rows=(3 * qd, 4 * qd), into=got_down), _rs_first(dw_up)])
    pair_up = _pair_add(dw_up, sib_up, place, "pair_add_up")
    (dmixed,) = _matmul(dmix, w_out_f, mode="nt", out_dtypes=[F32], name="d_mixed")
    (dw_out,), (got_up,) = _matmul(mixed, dmix, mode="tn", out_dtypes=[BF16], name="dw_out",
                                   comms=[_rs_second(pair_up, rows=(0, qu))])
    dw_out = dw_out.reshape(N_DEV, d // N_DEV, d)
    (dscb, dscc, dsch, dconv_s), (sib_out,) = _sc_bwd(proj, conv_s, dmixed, bsz, seq, scw, lay.sc_col, gw,
                                                      comms=[_rs_first(dw_out)])
    pair_out = _pair_add(dw_out, sib_out, place, "pair_add_out")
    (dact, dz, dab, dalog, ddtb, dgnw), (got_up,) = _gdn_bwd(
        qkv, proj, alog_t, dtb_t, gdn_norm_w, ssave, tsave, dmixed, bsz, seq, heads, 3 * gw, lay.ab_col,
        comms=[_rs_second(pair_up, rows=(qu, 3 * qu), into=got_up)])
    (dqkv, dconv_q), (got_up,) = _qkvconv_bwd(proj, conv_q, dact, bsz, seq, gw,
                                              comms=[_rs_second(pair_up, rows=(3 * qu, 4 * qu), into=got_up)])
    dproj = jnp.concatenate([dqkv, dz, dab, dscb, dscc, dsch, jnp.zeros((t, lay.wp - lay.used), BF16)], axis=1)
    nfb = lay.fw // MXU
    (dw_in,), (got_out,) = _matmul(
        xn, dproj, mode="tn", out_dtypes=[BF16], name="dw_in", n_cols=N_DEV * lay.fw, tn=MXU,
        b_spec=lambda tk, tn: pl.BlockSpec((tk, tn), lambda i, j, k: (k, lay.frame_block(j // nfb) + j % nfb)),
        out_custom=lambda tm, tn: ((N_DEV, d, lay.fw), (1, tm, tn), lambda i, j, k: (j // nfb, i, j % nfb)),
        comms=[_rs_second(pair_out)])
    tk_in = _tile(lay.fw, 1024)
    kpf = lay.fw // tk_in

    def d_xn(part, into, comms):
        return _matmul(
            dproj, g_in, mode="nt", out_dtypes=[F32], name="d_xn_%d" % part, n_cols=d, tn=1024, tk=tk_in,
            k_total=N_DEV * lay.fw, a_rows=(part * tq, tq), out_into=None if into is None else [into], comms=comms,
            a_spec=lambda tm, tk, r0: pl.BlockSpec(
                (pl.Element(tm), pl.Element(tk)),
                lambda i, j, k: (pl.multiple_of(i * tm + r0, 16),
                                 pl.multiple_of(lay.frame_block(k // kpf) * MXU + (k % kpf) * tk, LANE))),
            b_spec=lambda tk, tn: pl.BlockSpec((1, tn, tk), lambda i, j, k: (k // kpf, j, k % kpf)))

    (dxn,), (sib_in,) = d_xn(0, None, [_rs_first(dw_in)])
    pair_in = _pair_add(dw_in, sib_in, place, "pair_add_in")
    (dxn,), (got_in,) = d_xn(1, dxn, [_rs_second(pair_in, rows=(0, qu))])
    (dxn,), (got_in,) = d_xn(2, dxn, [_rs_second(pair_in, rows=(qu, 2 * qu), into=got_in)])
    (dxn,), (got_in,) = d_xn(3, dxn, [_rs_second(pair_in, rows=(2 * qu, 4 * qu), into=got_in)])
    grad_x, dg1 = _pre_bwd(x2, dh, dxn, g1)

    gin_frame = _sum_to(pieces(dw_in, sib_in, got_in), F32, "grad_w_in_frame")
    big = {
        "w_in": _adamw(w_in[0], m_w_in[0], v_w_in[0], [(lay.from_frame(gin_frame, me), None)], "adamw_w_in"),
        "w_out": _adamw(w_out[0], m_w_out[0], v_w_out[0], pieces(dw_out, sib_out, got_out), "adamw_w_out"),
        "w_up": _adamw(w_up[0], m_w_up[0], v_w_up[0], pieces(dw_up, sib_up, got_up), "adamw_w_up"),
        "w_down": _adamw(w_down[0], m_w_down[0], v_w_down[0], pieces(dw_down, sib_down, got_down), "adamw_w_down"),
    }

    small_shapes = [(kq, 3 * gw), (ks, scw), (1, d), (1, d), (1, d), (1, d), (1, LANE), (1, LANE), (1, hd), (1, LANE)]
    small = _unpack_rows(
        _allreduce_small(_pack_rows([dconv_q, dconv_s, dg1, dg2, dg3, dg4, jnp.sum(dalog, axis=0), jnp.sum(ddtb, axis=0),
                                     jnp.sum(dgnw, axis=0), loss_p]), "allreduce_small"), small_shapes)
    gq, gs, sg1, sg2, sg3, sg4, salog, sdtb, sgnw, sloss = small
    loss = sloss[0, 0]
    small_grads = {
        "norm_mix_pre": sg1, "conv_qkv_w": lax.dynamic_slice(gq, (0, me * cq_n), (kq, cq_n)),
        "a_log": salog[:, :heads], "dt_bias": sdtb[:, :heads], "gdn_norm_w": sgnw,
        "conv_sc_w": lax.dynamic_slice(gs, (0, me * cs_n), (ks, cs_n)),
        "norm_mix_post": sg2, "norm_mlp_pre": sg3, "norm_mlp_post": sg4,
    }
    weights = {"norm_mix_pre": (norm_mix_pre, m_norm_mix_pre, v_norm_mix_pre), "conv_qkv_w": (conv_qkv_w[0], m_conv_qkv_w[0], v_conv_qkv_w[0]),
               "a_log": (a_log, m_a_log, v_a_log), "dt_bias": (dt_bias, m_dt_bias, v_dt_bias),
               "gdn_norm_w": (gdn_norm_w, m_gdn_norm_w, v_gdn_norm_w), "conv_sc_w": (conv_sc_w[0], m_conv_sc_w[0], v_conv_sc_w[0]),
               "norm_mix_post": (norm_mix_post, m_norm_mix_post, v_norm_mix_post),
               "norm_mlp_pre": (norm_mlp_pre, m_norm_mlp_pre, v_norm_mlp_pre),
               "norm_mlp_post": (norm_mlp_post, m_norm_mlp_post, v_norm_mlp_post)}
    res = dict(big)
    for name, (w, m, v) in weights.items():
        res[name] = _adamw(w, m, v, [(small_grads[name], None)], "adamw_" + name)

    order = ["norm_mix_pre", "w_in", "conv_qkv_w", "a_log", "dt_bias", "gdn_norm_w", "conv_sc_w", "w_out", "norm_mix_post",
             "norm_mlp_pre", "w_up", "w_down", "norm_mlp_post"]
    shapes = {"norm_mix_pre": norm_mix_pre.shape, "w_in": w_in.shape, "conv_qkv_w": conv_qkv_w.shape, "a_log": a_log.shape,
              "dt_bias": dt_bias.shape, "gdn_norm_w": gdn_norm_w.shape, "conv_sc_w": conv_sc_w.shape, "w_out": w_out.shape,
              "norm_mix_post": norm_mix_post.shape, "norm_mlp_pre": norm_mlp_pre.shape, "w_up": w_up.shape,
              "w_down": w_down.shape, "norm_mlp_post": norm_mlp_post.shape}
    outs = [loss, grad_x.reshape(bsz, seq, d)]
    for part in range(4):
        outs += [res[nm][part].reshape(shapes[nm]) for nm in order]
    return tuple(outs)
```

```python
import functools
import math

import numpy as np
import jax
import jax.numpy as jnp
from jax import lax
from jax.experimental import pallas as pl
from jax.experimental.pallas import tpu as pltpu

F32 = jnp.float32
BF16 = jnp.bfloat16
HI = lax.Precision.HIGHEST
MESH = pl.DeviceIdType.MESH

N_DEV = 8
LANE = 128
MXU = 256
CHUNK = 64
NORM_EPS = 1e-6
L2_EPS = 1e-6
VMEM_LIMIT = 56 * 1024 * 1024

ADAM_LR = 0.001
ADAM_B1 = 0.9
ADAM_B2 = 0.999
ADAM_EPS = 1e-08
ADAM_WD = 0.01
ADAM_STEP = 10

NN = (((1,), (0,)), ((), ()))
NT = (((1,), (1,)), ((), ()))
TN = (((0,), (0,)), ((), ()))


def _params(*sem):
    return pltpu.CompilerParams(dimension_semantics=sem, vmem_limit_bytes=VMEM_LIMIT)


def _tile(n, want):
    if n <= want:
        return n
    t = (want // LANE) * LANE
    while t > LANE and n % t:
        t -= LANE
    assert n % t == 0, (n, want)
    return t


class _Comm:
    def __init__(self, srcs, lands, plan, n, alias=None):
        self.srcs, self.lands, self.plan, self.n, self.alias = list(srcs), list(lands), plan, n, dict(alias or {})


def _pcall(body, *, grid, in_specs, out_specs, out_shape, operands, name, scratch_shapes=(), semantics=None,
           prefetch=(), comms=(), fill=None):
    n_pf, n_in, n_out, n_scr = len(prefetch), len(in_specs), len(out_specs), len(scratch_shapes)
    srcs = [s for cm in comms for s in cm.srcs]
    lands = [l for cm in comms for l in cm.lands]
    n_src, n_land = len(srcs), len(lands)
    n_copies = sum(cm.n for cm in comms)
    aliases, so, lo = {n_pf + a: b for a, b in (fill or {}).items()}, 0, 0
    for cm in comms:
        for a, b in cm.alias.items():
            aliases[n_pf + n_in + so + a] = n_out + lo + b
        so, lo = so + len(cm.srcs), lo + len(cm.lands)
    any_spec = pl.BlockSpec(memory_space=pl.ANY)

    def wrapped(*refs):
        pf, r = refs[:n_pf], refs[n_pf:]
        ins, csrc = r[:n_in], r[n_in:n_in + n_src]
        outs = r[n_in + n_src:n_in + n_src + n_out]
        cland = r[n_in + n_src + n_out:n_in + n_src + n_out + n_land]
        rest = r[n_in + n_src + n_out + n_land:]
        scratch = rest[:n_scr]
        if not comms:
            body(*pf, *ins, *outs, *scratch)
            return
        send_sems, recv_sems = rest[n_scr:]

        def copies():
            out, k, s0, l0 = [], 0, 0, 0
            for cm in comms:
                for kind, src, dst, dev in cm.plan(csrc[s0:s0 + len(cm.srcs)], cland[l0:l0 + len(cm.lands)]):
                    if kind == "local":
                        out.append((kind, pltpu.make_async_copy(src, dst, send_sems.at[k])))
                    else:
                        out.append((kind, pltpu.make_async_remote_copy(
                            src_ref=src, dst_ref=dst, send_sem=send_sems.at[k], recv_sem=recv_sems.at[k],
                            device_id=dev, device_id_type=MESH)))
                    k += 1
                s0, l0 = s0 + len(cm.srcs), l0 + len(cm.lands)
            assert k == n_copies
            return out

        ids = [pl.program_id(a) for a in range(len(grid))]
        first = functools.reduce(jnp.logical_and, [i == 0 for i in ids])
        last = functools.reduce(jnp.logical_and, [i == g - 1 for i, g in zip(ids, grid)])

        @pl.when(first)
        def _():
            for _, cp in copies():
                cp.start()

        body(*pf, *ins, *outs, *scratch)

        @pl.when(last)
        def _():
            cps = copies()
            for kind, cp in cps:
                if kind == "remote":
                    cp.wait_recv()
            for kind, cp in cps:
                if kind == "remote":
                    cp.wait_send()
                else:
                    cp.wait()

    sems = [pltpu.SemaphoreType.DMA((n_copies,)), pltpu.SemaphoreType.DMA((n_copies,))] if comms else []
    if semantics is None or comms:
        semantics = ("arbitrary",) * len(grid)
    res = pl.pallas_call(
        wrapped,
        grid_spec=pltpu.PrefetchScalarGridSpec(
            num_scalar_prefetch=n_pf, grid=tuple(grid), in_specs=list(in_specs) + [any_spec] * n_src,
            out_specs=list(out_specs) + [any_spec] * n_land, scratch_shapes=list(scratch_shapes) + sems),
        out_shape=list(out_shape) + lands,
        input_output_aliases=aliases,
        compiler_params=_params(*semantics), name=name)(*prefetch, *operands, *srcs)
    return list(res[:n_out]), list(res[n_out:])


def _bdot(a, b, dims=NN):
    return lax.dot_general(a.astype(BF16), b.astype(BF16), dims, preferred_element_type=F32)


def _hdot(a, b, dims=NN):
    return lax.dot_general(a, b, dims, preferred_element_type=F32, precision=HI)


def _mdot(a, b, dims=NN):
    return lax.dot_general(a, b, dims, preferred_element_type=F32, precision=lax.Precision.HIGH)


def _sigmoid(x):
    return 1.0 / (1.0 + jnp.exp(-x))


def _softplus(x):
    return jnp.maximum(x, 0.0) + jnp.log(1.0 + jnp.exp(-jnp.abs(x)))


def _matmul(a, b, *, mode, out_dtypes, name, n_cols=None, tm=1024, tn=512, tk=4096, epilogue=None, extras=(),
            b_spec=None, out_custom=None, a_rows=None, out_into=None, a_spec=None, k_total=None, comms=()):
    if mode == "tn":
        K, M = a.shape
    else:
        M, K = a.shape
    if k_total is not None:
        K = k_total
    r0 = 0
    if a_rows is not None:
        r0, M = a_rows
    N = n_cols if n_cols is not None else (b.shape[0] if mode == "nt" else b.shape[1])
    tm, tk, tn = _tile(M, tm), _tile(K, tk), _tile(N, tn)
    assert r0 % tm == 0
    i0 = r0 // tm
    if b_spec is None:
        b_spec = pl.BlockSpec((tn, tk), lambda i, j, k: (j, k)) if mode == "nt" else pl.BlockSpec((tk, tn), lambda i, j, k: (k, j))
    else:
        b_spec = b_spec(tk, tn)
    gm, gn, nk = M // tm, N // tn, K // tk
    if a_rows is not None:
        out_shapes = [(a.shape[0], N)] * len(out_dtypes)
        out_blocks = [(tm, tn)] * len(out_dtypes)
        out_index = [lambda i, j, k: (i + i0, j)] * len(out_dtypes)
    elif out_custom is None:
        out_shapes = [(M, N)] * len(out_dtypes)
        out_blocks = [(tm, tn)] * len(out_dtypes)
        out_index = [lambda i, j, k: (i, j)] * len(out_dtypes)
    else:
        shape, blk, ix = out_custom(tm, tn)
        out_shapes, out_blocks, out_index = [shape] * len(out_dtypes), [blk] * len(out_dtypes), [ix] * len(out_dtypes)
    if a_spec is not None:
        a_spec = a_spec(tm, tk, r0)
    elif mode == "tn":
        a_spec = pl.BlockSpec((tk, tm), lambda i, j, k: (k, i))
    else:
        a_spec = pl.BlockSpec((tm, tk), lambda i, j, k: (i + i0, k))
    hoist = mode == "tn" and nk == 1 and gn > 1
    dims = {"nn": NN, "nt": NT, "tn": TN}[mode]
    n_ex, n_out = len(extras), len(out_dtypes)
    out_into = [] if out_into is None else list(out_into)
    n_into = len(out_into)
    assert n_into in (0, n_out)

    def body(a_ref, b_ref, *rest):
        ex, outs = rest[:n_ex], rest[n_ex + n_into:n_ex + n_into + n_out]

        def finish(acc):
            res = epilogue(acc, *[e[...] for e in ex]) if epilogue is not None else (acc,)
            for o, r in zip(outs, res):
                o[...] = r.reshape(o.shape).astype(o.dtype)

        bb = b_ref[...]
        bb = bb.reshape(bb.shape[-2:])
        if hoist:
            at_ref = rest[-1]

            @pl.when(pl.program_id(1) == 0)
            def _():
                at_ref[...] = a_ref[...].T

            finish(lax.dot_general(at_ref[...], bb, NN, preferred_element_type=F32))
            return
        part = lax.dot_general(a_ref[...], bb, dims, preferred_element_type=F32)
        if nk == 1:
            finish(part)
        else:
            acc = rest[-1]
            k = pl.program_id(2)

            @pl.when(k == 0)
            def _():
                acc[...] = part

            @pl.when(k > 0)
            def _():
                acc[...] += part

            @pl.when(k == nk - 1)
            def _():
                finish(acc[...])

    scratch = [pltpu.VMEM((tm, tk), BF16)] if hoist else ([pltpu.VMEM((tm, tn), F32)] if nk > 1 else [])
    outs, lands = _pcall(
        body, grid=(gm, gn, nk),
        in_specs=([a_spec, b_spec] + [pl.BlockSpec((tm, tn), lambda i, j, k: (i, j)) for _ in extras]
                  + [pl.BlockSpec(memory_space=pl.ANY)] * n_into),
        out_specs=[pl.BlockSpec(blk, ix) for blk, ix in zip(out_blocks, out_index)],
        out_shape=[jax.ShapeDtypeStruct(s, d) for s, d in zip(out_shapes, out_dtypes)],
        scratch_shapes=scratch, semantics=("parallel", "arbitrary", "arbitrary"),
        operands=[a, b, *extras, *out_into], name=name, comms=comms,
        fill={2 + n_ex + o: o for o in range(n_into)})
    return (outs, lands) if comms else outs


TR = 256
QKV_CONV_COLS = 256


def _rms(x):
    return lax.rsqrt(jnp.mean(x * x, axis=-1, keepdims=True) + NORM_EPS)


def _rms_bwd(x, r, w, dy):
    u = dy * w
    dx = r * u - x * (r * r * r) * jnp.mean(x * u, axis=-1, keepdims=True)
    return dx, dy * x * r


def _row_call(body, ins, row_flags, outs, name, n_rows, comms=()):
    n_row = sum(row_flags) + sum(kind == "row" for _, _, kind in outs)
    tr = min(TR if n_row <= 5 else TR // 2, n_rows)
    in_specs = []
    for arr, is_row in zip(ins, row_flags):
        if is_row:
            in_specs.append(pl.BlockSpec((tr, arr.shape[1]), lambda i: (i, 0)))
        else:
            in_specs.append(pl.BlockSpec(arr.shape, lambda i: (0, 0)))
    out_specs, out_shape = [], []
    for shape, dtype, kind in outs:
        if kind == "row":
            out_specs.append(pl.BlockSpec((tr, shape[1]), lambda i: (i, 0)))
        else:
            out_specs.append(pl.BlockSpec(shape, lambda i: (0, 0)))
        out_shape.append(jax.ShapeDtypeStruct(shape, dtype))
    res, lands = _pcall(body, grid=(n_rows // tr,), in_specs=in_specs, out_specs=out_specs, out_shape=out_shape,
                        operands=list(ins), name=name, comms=comms)
    return (res, lands) if comms else res


def _acc_out(ref, val):
    @pl.when(pl.program_id(0) == 0)
    def _():
        ref[...] = val

    @pl.when(pl.program_id(0) > 0)
    def _():
        ref[...] += val


def _rms_fwd(x, g, comms):
    T, D = x.shape

    def body(x_ref, g_ref, o_ref):
        xv = x_ref[...]
        o_ref[...] = (xv * _rms(xv) * g_ref[...]).astype(BF16)

    res, lands = _row_call(body, [x, g], [True, False], [((T, D), BF16, "row")], "rms_fwd", T, comms=comms)
    return res[0], lands


def _post1(x, mix, g2, g3, comms=()):
    T, D = x.shape

    def body(x_ref, mix_ref, g2_ref, g3_ref, h_ref, xn2_ref):
        mv = mix_ref[...]
        h = x_ref[...] + mv * _rms(mv) * g2_ref[...]
        h_ref[...] = h
        xn2_ref[...] = (h * _rms(h) * g3_ref[...]).astype(BF16)

    return _row_call(body, [x, mix, g2, g3], [True, True, False, False],
                     [((T, D), F32, "row"), ((T, D), BF16, "row")], "post1", T, comms=comms)


def _post2_loss(h, ff, g4, target):
    T, D = h.shape

    def body(h_ref, ff_ref, g4_ref, t_ref, dff_ref, dy_ref, dg4_ref, loss_ref):
        fv = ff_ref[...]
        r = _rms(fv)
        err = h_ref[...] + fv * r * g4_ref[...] - t_ref[...]
        dy = err * (1.0 / D)
        dy_ref[...] = dy
        dff, dwt = _rms_bwd(fv, r, g4_ref[...], dy)
        dff_ref[...] = dff.astype(BF16)
        _acc_out(dg4_ref, jnp.sum(dwt, axis=0, keepdims=True))
        part = 0.5 * jnp.sum(jnp.mean(err * err, axis=-1, keepdims=True), axis=0, keepdims=True)
        _acc_out(loss_ref, jnp.broadcast_to(part, (1, LANE)))

    return _row_call(body, [h, ff, g4, target], [True, True, False, True],
                     [((T, D), BF16, "row"), ((T, D), F32, "row"), ((1, D), F32, "acc"), ((1, LANE), F32, "acc")],
                     "post2_loss", T)


def _mid_bwd(h, mix, dy, dxn2, g2, g3, comms=()):
    T, D = h.shape

    def body(h_ref, mix_ref, dy_ref, dxn2_ref, g2_ref, g3_ref, dmix_ref, dh_ref, dg2_ref, dg3_ref):
        hv = h_ref[...]
        d1, dw3 = _rms_bwd(hv, _rms(hv), g3_ref[...], dxn2_ref[...])
        dh = dy_ref[...] + d1
        dh_ref[...] = dh
        mv = mix_ref[...]
        dmix, dw2 = _rms_bwd(mv, _rms(mv), g2_ref[...], dh)
        dmix_ref[...] = dmix.astype(BF16)
        _acc_out(dg2_ref, jnp.sum(dw2, axis=0, keepdims=True))
        _acc_out(dg3_ref, jnp.sum(dw3, axis=0, keepdims=True))

    return _row_call(body, [h, mix, dy, dxn2, g2, g3], [True, True, True, True, False, False],
                     [((T, D), BF16, "row"), ((T, D), F32, "row"), ((1, D), F32, "acc"), ((1, D), F32, "acc")],
                     "mid_bwd", T, comms=comms)


def _pre_bwd(x, dh, dxn, g1, comms=()):
    T, D = x.shape

    def body(x_ref, dh_ref, dxn_ref, g1_ref, gx_ref, dg1_ref):
        xv = x_ref[...]
        d1, dw1 = _rms_bwd(xv, _rms(xv), g1_ref[...], dxn_ref[...])
        gx_ref[...] = dh_ref[...] + d1
        _acc_out(dg1_ref, jnp.sum(dw1, axis=0, keepdims=True))

    return _row_call(body, [x, dh, dxn, g1], [True, True, True, False],
                     [((T, D), F32, "row"), ((1, D), F32, "acc")], "pre_bwd", T, comms=comms)


def _shift_down(x, s):
    if s == 0:
        return x
    row = lax.broadcasted_iota(jnp.int32, x.shape, 0)
    return jnp.where(row >= s, pltpu.roll(x, s, axis=0), 0.0)


def _shift_up(x, s):
    if s == 0:
        return x
    n = x.shape[0]
    row = lax.broadcasted_iota(jnp.int32, x.shape, 0)
    return jnp.where(row < n - s, pltpu.roll(x, n - s, axis=0), 0.0)


def _conv(x, w):
    kw = w.shape[0]
    out = w[kw - 1:kw, :] * x
    for j in range(kw - 1):
        out = out + w[j:j + 1, :] * _shift_down(x, kw - 1 - j)
    return out


def _conv_bwd(x, w, dout):
    kw = w.shape[0]
    dx = w[kw - 1:kw, :] * dout
    dws = []
    for j in range(kw - 1):
        dx = dx + w[j:j + 1, :] * _shift_up(dout, kw - 1 - j)
        dws.append(jnp.sum(dout * _shift_down(x, kw - 1 - j), axis=0, keepdims=True))
    dws.append(jnp.sum(dout * x, axis=0, keepdims=True))
    return dx, jnp.concatenate(dws, axis=0)


def _qkvconv_fwd(proj, w, bsz, seq, gw, comms=()):
    cw = QKV_CONV_COLS
    nct = gw // cw
    kw = w.shape[0]

    def body(p_ref, w_ref, o_ref):
        cv = _conv(p_ref[...], w_ref[...])
        o_ref[...] = (cv * _sigmoid(cv)).reshape(o_ref.shape)

    res, lands = _pcall(
        body, grid=(3, bsz, nct),
        in_specs=[pl.BlockSpec((seq, cw), lambda p, b, c: (b, p * nct + c)),
                  pl.BlockSpec((kw, cw), lambda p, b, c: (0, p * nct + c))],
        out_specs=[pl.BlockSpec((1, seq, cw), lambda p, b, c: (p, b, c))],
        out_shape=[jax.ShapeDtypeStruct((3, bsz * seq, gw), F32)],
        semantics=("parallel", "parallel", "parallel"), operands=[proj, w], name="qkvconv_fwd", comms=comms)
    return res[0], lands


def _qkvconv_bwd(proj, w, dact, bsz, seq, gw, comms=()):
    cw = QKV_CONV_COLS
    nct = gw // cw
    kw = w.shape[0]

    def body(p_ref, w_ref, d_ref, dp_ref, dw_ref):
        pre = p_ref[...]
        wv = w_ref[...]
        cv = _conv(pre, wv)
        sg = _sigmoid(cv)
        dcv = d_ref[...].reshape(cv.shape) * (sg * (1.0 + cv * (1.0 - sg)))
        dpre, dw = _conv_bwd(pre, wv, dcv)
        dp_ref[...] = dpre.astype(BF16)
        b = pl.program_id(2)

        @pl.when(b == 0)
        def _():
            dw_ref[...] = dw

        @pl.when(b > 0)
        def _():
            dw_ref[...] += dw

    res, lands = _pcall(
        body, grid=(3, nct, bsz),
        in_specs=[pl.BlockSpec((seq, cw), lambda p, c, b: (b, p * nct + c)),
                  pl.BlockSpec((kw, cw), lambda p, c, b: (0, p * nct + c)),
                  pl.BlockSpec((1, seq, cw), lambda p, c, b: (p, b, c))],
        out_specs=[pl.BlockSpec((seq, cw), lambda p, c, b: (b, p * nct + c)),
                   pl.BlockSpec((kw, cw), lambda p, c, b: (0, p * nct + c))],
        out_shape=[jax.ShapeDtypeStruct((bsz * seq, 3 * gw), BF16), jax.ShapeDtypeStruct((kw, 3 * gw), F32)],
        semantics=("parallel", "parallel", "arbitrary"), operands=[proj, w, dact], name="qkvconv_bwd", comms=comms)
    return res, lands


def _sc_fwd(proj, w, bsz, seq, scw, col0):
    nct = scw // LANE
    c0 = col0 // LANE
    kw = w.shape[0]

    def body(b_ref, c_ref, h_ref, w_ref, o_ref):
        o_ref[...] = (b_ref[...] * _conv(c_ref[...] * h_ref[...], w_ref[...])).astype(BF16)

    return pl.pallas_call(
        body, grid=(bsz, nct),
        in_specs=[pl.BlockSpec((seq, LANE), lambda b, c: (b, c0 + c)),
                  pl.BlockSpec((seq, LANE), lambda b, c: (b, c0 + nct + c)),
                  pl.BlockSpec((seq, LANE), lambda b, c: (b, c0 + 2 * nct + c)),
                  pl.BlockSpec((kw, LANE), lambda b, c: (0, c))],
        out_specs=pl.BlockSpec((seq, LANE), lambda b, c: (b, c)),
        out_shape=jax.ShapeDtypeStruct((bsz * seq, scw), BF16),
        compiler_params=_params("parallel", "parallel"), name="sc_fwd")(proj, proj, proj, w)


def _sc_bwd(proj, w, dout, bsz, seq, scw, col0, dcol0, comms=()):
    nct = scw // LANE
    c0 = col0 // LANE
    d0 = dcol0 // LANE
    kw = w.shape[0]

    def body(b_ref, c_ref, h_ref, w_ref, d_ref, db_ref, dc_ref, dh_ref, dw_ref):
        cc, hh, wv, dv = c_ref[...], h_ref[...], w_ref[...], d_ref[...]
        m = cc * hh
        db_ref[...] = (dv * _conv(m, wv)).astype(BF16)
        dm, dw = _conv_bwd(m, wv, dv * b_ref[...])
        dc_ref[...] = (dm * hh).astype(BF16)
        dh_ref[...] = (dm * cc).astype(BF16)
        b = pl.program_id(1)

        @pl.when(b == 0)
        def _():
            dw_ref[...] = dw

        @pl.when(b > 0)
        def _():
            dw_ref[...] += dw

    res, lands = _pcall(
        body, grid=(nct, bsz),
        in_specs=[pl.BlockSpec((seq, LANE), lambda c, b: (b, c0 + c)),
                  pl.BlockSpec((seq, LANE), lambda c, b: (b, c0 + nct + c)),
                  pl.BlockSpec((seq, LANE), lambda c, b: (b, c0 + 2 * nct + c)),
                  pl.BlockSpec((kw, LANE), lambda c, b: (0, c)),
                  pl.BlockSpec((seq, LANE), lambda c, b: (b, d0 + c))],
        out_specs=[pl.BlockSpec((seq, LANE), lambda c, b: (b, c)),
                   pl.BlockSpec((seq, LANE), lambda c, b: (b, c)),
                   pl.BlockSpec((seq, LANE), lambda c, b: (b, c)),
                   pl.BlockSpec((kw, LANE), lambda c, b: (0, c))],
        out_shape=[jax.ShapeDtypeStruct((bsz * seq, scw), BF16)] * 3 + [jax.ShapeDtypeStruct((kw, scw), F32)],
        semantics=("parallel", "arbitrary"), operands=[proj, proj, proj, w, dout], name="sc_bwd", comms=comms)
    return res, lands


HEADS_PER_STEP = 16


def _colsel(tile, idx):
    lane = lax.broadcasted_iota(jnp.int32, tile.shape, 1)
    return jnp.sum(jnp.where(lane == idx, tile, 0.0), axis=1, keepdims=True)


def _rowsel(tile, idx):
    row = lax.broadcasted_iota(jnp.int32, tile.shape, 0)
    return jnp.sum(jnp.where(row == idx, tile, 0.0), axis=0, keepdims=True)


def _colput(col, idx, width=LANE):
    lane = lax.broadcasted_iota(jnp.int32, (col.shape[0], width), 1)
    return jnp.where(lane == idx, col, 0.0)


def _tri_masks(c):
    row = lax.broadcasted_iota(jnp.int32, (c, c), 0)
    col = lax.broadcasted_iota(jnp.int32, (c, c), 1)
    return row >= col, row > col, row == col


def _unit_lower_inverses(ms):
    c = ms[0].shape[0]
    _, _, eye = _tri_masks(c)
    ps = [-m for m in ms]
    ts = [jnp.where(eye, 1.0, 0.0) + p for p in ps]
    for _ in range(int(math.log2(c)) - 1):
        ps = [_mdot(p, p) for p in ps]
        ts = [t + _mdot(t, p) for t, p in zip(ts, ps)]
    return ts


def _gates(ab, alog, dtb):
    g = -jnp.exp(alog) * _softplus(ab + dtb)
    return g, _sigmoid(ab)


def _l2n(x):
    r = lax.rsqrt(jnp.sum(x * x, axis=-1, keepdims=True) + L2_EPS)
    return x * r, r


def _gdn_chunk_common(q, k, gc, gr, bc):
    c, dk = q.shape
    incl, strict, _ = _tri_masks(c)
    qh, rq = _l2n(q)
    kn, rk = _l2n(k)
    qn = qh * (dk ** -0.5)
    dm = jnp.where(incl, jnp.exp(jnp.where(incl, gc - gr, 0.0)), 0.0)
    kk = _bdot(kn, kn, NT)
    m = jnp.where(strict, bc * kk * dm, 0.0)
    pm = jnp.where(incl, _bdot(qn, kn, NT) * dm, 0.0)
    return qh, rq, kn, rk, qn, dm, kk, m, pm


def _gdn_fwd(qkv, proj, alog, dtb, gnw, bsz, seq, heads, z_col0, ab_col0, comms=()):
    c = CHUNK
    nch = seq // c
    hb = min(HEADS_PER_STEP, heads)
    ng = heads // hb
    hd = qkv.shape[2] // heads
    wb = hb * hd

    def body(qkv_ref, z_ref, ab_ref, alog_ref, dtb_ref, gnw_ref, o_ref, ssave_ref, tsave_ref, s_ref, gc_ref, gt_ref, be_ref):
        n, hg = pl.program_id(1), pl.program_id(2)

        @pl.when((n == 0) & (hg == 0))
        def _():
            s_ref[...] = jnp.zeros_like(s_ref)

        @pl.when(hg == 0)
        def _():
            g, beta = _gates(ab_ref[...], alog_ref[...], dtb_ref[...])
            incl, _, _ = _tri_masks(c)
            gcum = _hdot(jnp.where(incl, 1.0, 0.0), g)
            gc_ref[...] = gcum
            gt_ref[...] = gcum.T
            be_ref[...] = beta

        gc_t, gt_t, be_t, gnw_v = gc_ref[...], gt_ref[...], be_ref[...], gnw_ref[...]
        hs = range(hb)
        sls = [slice(hh * hd, (hh + 1) * hd) for hh in hs]
        states = [s_ref[hg * hb + hh] for hh in hs]
        gcs = [_colsel(gc_t, hg * hb + hh) for hh in hs]
        grs = [_rowsel(gt_t, hg * hb + hh) for hh in hs]
        bcs = [_colsel(be_t, heads + hg * hb + hh) for hh in hs]
        com = [_gdn_chunk_common(qkv_ref[0, :, sls[hh]], qkv_ref[1, :, sls[hh]], gcs[hh], grs[hh], bcs[hh]) for hh in hs]
        kns, qns, pms = [cm[2] for cm in com], [cm[4] for cm in com], [cm[8] for cm in com]
        tms = _unit_lower_inverses([cm[7] for cm in com])
        gams = [jnp.exp(gc) for gc in gcs]
        glasts = [gc[c - 1:c, :] for gc in gcs]
        kss = [_bdot(kns[hh], states[hh]) for hh in hs]
        qss = [_bdot(qns[hh], states[hh]) for hh in hs]
        vns = [_bdot(tms[hh], bcs[hh] * (qkv_ref[2, :, sls[hh]] - gams[hh] * kss[hh])) for hh in hs]
        os_ = [gams[hh] * qss[hh] + _bdot(pms[hh], vns[hh]) for hh in hs]
        snews = [states[hh] * jnp.exp(glasts[hh]) + _bdot(kns[hh] * jnp.exp(glasts[hh] - gcs[hh]), vns[hh], TN) for hh in hs]
        for hh in hs:
            o = os_[hh]
            on = o * lax.rsqrt(jnp.mean(o * o, axis=-1, keepdims=True) + NORM_EPS) * gnw_v
            zz = z_ref[:, sls[hh]]
            ssave_ref[0, 0, hh] = states[hh]
            tsave_ref[0, 0, hh] = tms[hh]
            s_ref[hg * hb + hh] = snews[hh]
            o_ref[:, sls[hh]] = (on * (zz * _sigmoid(zz))).astype(BF16)

    row = lambda b, n, g: b * nch + n
    return _pcall(
        body, grid=(bsz, nch, ng),
        in_specs=[pl.BlockSpec((3, c, wb), lambda b, n, g: (0, row(b, n, g), g)),
                  pl.BlockSpec((c, wb), lambda b, n, g: (row(b, n, g), z_col0 // wb + g)),
                  pl.BlockSpec((c, LANE), lambda b, n, g: (row(b, n, g), ab_col0 // LANE)),
                  pl.BlockSpec((1, LANE), lambda b, n, g: (0, 0)),
                  pl.BlockSpec((1, LANE), lambda b, n, g: (0, 0)),
                  pl.BlockSpec((1, hd), lambda b, n, g: (0, 0))],
        out_specs=[pl.BlockSpec((c, wb), lambda b, n, g: (row(b, n, g), g)),
                   pl.BlockSpec((1, 1, hb, hd, hd), lambda b, n, g: (b, n, g, 0, 0)),
                   pl.BlockSpec((1, 1, hb, c, c), lambda b, n, g: (b, n, g, 0, 0))],
        out_shape=[jax.ShapeDtypeStruct((bsz * seq, heads * hd), BF16),
                   jax.ShapeDtypeStruct((bsz, nch, heads, hd, hd), F32),
                   jax.ShapeDtypeStruct((bsz, nch, heads, c, c), F32)],
        scratch_shapes=[pltpu.VMEM((heads, hd, hd), F32), pltpu.VMEM((c, LANE), F32), pltpu.VMEM((LANE, c), F32),
                        pltpu.VMEM((c, LANE), F32)],
        semantics=("parallel", "arbitrary", "arbitrary"), operands=[qkv, proj, proj, alog, dtb, gnw], name="gdn_fwd",
        comms=comms)


def _gdn_bwd(qkv, proj, alog, dtb, gnw, ssave, tsave, dout, bsz, seq, heads, z_col0, ab_col0, comms=()):
    c = CHUNK
    nch = seq // c
    hb = min(HEADS_PER_STEP, heads)
    ng = heads // hb
    hd = qkv.shape[2] // heads
    wb = hb * hd

    def body(qkv_ref, z_ref, ab_ref, alog_ref, dtb_ref, gnw_ref, ssave_ref, tsave_ref, do_ref,
             dact_ref, dz_ref, dab_ref, dalog_ref, ddtb_ref, dgnw_ref,
             ds_ref, gc_ref, gt_ref, be_ref, dgacc_ref, dbacc_ref):
        n, hg = pl.program_id(1), pl.program_id(2)
        incl, strict, _ = _tri_masks(c)

        @pl.when((n == 0) & (hg == 0))
        def _():
            ds_ref[...] = jnp.zeros_like(ds_ref)
            dalog_ref[...] = jnp.zeros_like(dalog_ref)
            ddtb_ref[...] = jnp.zeros_like(ddtb_ref)
            dgnw_ref[...] = jnp.zeros_like(dgnw_ref)

        @pl.when(hg == 0)
        def _():
            g, beta = _gates(ab_ref[...], alog_ref[...], dtb_ref[...])
            gcum = _hdot(jnp.where(incl, 1.0, 0.0), g)
            gc_ref[...] = gcum
            gt_ref[...] = gcum.T
            be_ref[...] = beta
            dgacc_ref[...] = jnp.zeros_like(dgacc_ref)
            dbacc_ref[...] = jnp.zeros_like(dbacc_ref)

        gc_t, gt_t, be_t, gnw_v = gc_ref[...], gt_ref[...], be_ref[...], gnw_ref[...]
        hs = range(hb)

        def each(f):
            return [f(hh) for hh in hs]

        rsum = lambda a: jnp.sum(a, axis=-1, keepdims=True)
        sls = each(lambda i: slice(i * hd, (i + 1) * hd))
        ds_in = each(lambda i: ds_ref[hg * hb + i])
        gc = each(lambda i: _colsel(gc_t, hg * hb + i))
        gr = each(lambda i: _rowsel(gt_t, hg * hb + i))
        bc = each(lambda i: _colsel(be_t, heads + hg * hb + i))
        com = each(lambda i: _gdn_chunk_common(qkv_ref[0, :, sls[i]], qkv_ref[1, :, sls[i]], gc[i], gr[i], bc[i]))
        qh, rq, kn, rk, qn, dm, kk, m, pm = [[cm[j] for cm in com] for j in range(9)]
        tm = each(lambda i: tsave_ref[0, 0, i])
        s = each(lambda i: ssave_ref[0, 0, i])
        gam = each(lambda i: jnp.exp(gc[i]))
        glast = each(lambda i: gc[i][c - 1:c, :])
        gl = each(lambda i: jnp.exp(glast[i]))
        ratio = each(lambda i: jnp.exp(glast[i] - gc[i]))
        ks = each(lambda i: _bdot(kn[i], s[i]))
        qs = each(lambda i: _bdot(qn[i], s[i]))
        r = each(lambda i: qkv_ref[2, :, sls[i]] - gam[i] * ks[i])
        vn = each(lambda i: _bdot(tm[i], bc[i] * r[i]))
        o = each(lambda i: gam[i] * qs[i] + _bdot(pm[i], vn[i]))
        ro = each(lambda i: lax.rsqrt(jnp.mean(o[i] * o[i], axis=-1, keepdims=True) + NORM_EPS))
        zz = each(lambda i: z_ref[:, sls[i]])
        sz = each(lambda i: _sigmoid(zz[i]))
        dd = each(lambda i: do_ref[:, sls[i]])
        don = each(lambda i: dd[i] * (zz[i] * sz[i]))
        dz_h = each(lambda i: (dd[i] * (o[i] * ro[i] * gnw_v) * (sz[i] * (1.0 + zz[i] * (1.0 - sz[i])))).astype(BF16))
        dgnw = sum(each(lambda i: jnp.sum(don[i] * o[i] * ro[i], axis=0, keepdims=True)))
        uu = each(lambda i: don[i] * gnw_v)
        d_o = each(lambda i: ro[i] * uu[i] - o[i] * (ro[i] * ro[i] * ro[i]) * jnp.mean(o[i] * uu[i], axis=-1, keepdims=True))
        dqs = each(lambda i: gam[i] * d_o[i])
        dq = each(lambda i: _bdot(dqs[i], s[i], NT))
        ds_new = each(lambda i: _bdot(qn[i], dqs[i], TN))
        dp = each(lambda i: jnp.where(incl, _bdot(d_o[i], vn[i], NT), 0.0))
        dvn = each(lambda i: _bdot(pm[i], d_o[i], TN))
        dgam = each(lambda i: rsum(d_o[i] * qs[i]))
        dkd = each(lambda i: _bdot(vn[i], ds_in[i], NT))
        dvn = each(lambda i: dvn[i] + _bdot(kn[i] * ratio[i], ds_in[i]))
        ds_new = each(lambda i: ds_new[i] + gl[i] * ds_in[i])
        dgl = each(lambda i: jnp.sum(jnp.sum(ds_in[i] * s[i], axis=1, keepdims=True), axis=0, keepdims=True))
        dratio = each(lambda i: rsum(dkd[i] * kn[i]))
        dpd = each(lambda i: dp[i] * dm[i])
        dq = each(lambda i: dq[i] + _bdot(dpd[i], kn[i]))
        dk = each(lambda i: ratio[i] * dkd[i] + _bdot(dpd[i], qn[i], TN))
        dx = each(lambda i: _bdot(tm[i], dvn[i], TN))
        dr = each(lambda i: bc[i] * dx[i])
        gdr = each(lambda i: gam[i] * dr[i])
        dk = each(lambda i: dk[i] - _bdot(gdr[i], s[i], NT))
        ds_new = each(lambda i: ds_new[i] - _bdot(kn[i], gdr[i], TN))
        dmm = each(lambda i: jnp.where(strict, -_bdot(dx[i], vn[i], NT), 0.0))
        ee = each(lambda i: dmm[i] * dm[i])
        be_e = each(lambda i: bc[i] * ee[i])
        dk = each(lambda i: dk[i] + _bdot(be_e[i], kn[i]) + _bdot(be_e[i], kn[i], TN))
        dbeta = each(lambda i: rsum(dx[i] * r[i]) + rsum(ee[i] * kk[i]))
        dgam = each(lambda i: dgam[i] - rsum(dr[i] * ks[i]))
        ff = each(lambda i: dp[i] * pm[i] + dmm[i] * m[i])
        rowi = lax.broadcasted_iota(jnp.int32, (c, 1), 0)
        dgc = each(lambda i: rsum(ff[i]) - rsum(ff[i].T) + dgam[i] * gam[i] - dratio[i] * ratio[i]
                   + jnp.where(rowi == c - 1, jnp.sum(dratio[i] * ratio[i], axis=0, keepdims=True) + dgl[i] * gl[i], 0.0))
        dg_tile = sum(each(lambda i: _colput(dgc[i], hg * hb + i)))
        db_tile = sum(each(lambda i: _colput(dbeta[i], heads + hg * hb + i)))
        for i in hs:
            dqh = dq[i] * (hd ** -0.5)
            ds_ref[hg * hb + i] = ds_new[i]
            dz_ref[:, sls[i]] = dz_h[i]
            dact_ref[0, :, sls[i]] = rq[i] * (dqh - qh[i] * rsum(qh[i] * dqh))
            dact_ref[1, :, sls[i]] = rk[i] * (dk[i] - kn[i] * rsum(kn[i] * dk[i]))
            dact_ref[2, :, sls[i]] = dr[i]
        dgacc_ref[...] += dg_tile
        dbacc_ref[...] += db_tile
        dgnw_ref[0] += dgnw

        @pl.when(hg == ng - 1)
        def _():
            ab = ab_ref[...]
            ea = jnp.exp(alog_ref[...])
            g = -ea * _softplus(ab + dtb_ref[...])
            beta = be_ref[...]
            dg = _hdot(jnp.where(incl, 1.0, 0.0), dgacc_ref[...], TN)
            lane = lax.broadcasted_iota(jnp.int32, ab.shape, 1)
            da = jnp.where(lane < heads, dg * (-ea) * _sigmoid(ab + dtb_ref[...]), 0.0)
            db = dbacc_ref[...] * beta * (1.0 - beta)
            dab_ref[...] = (da + db).astype(BF16)
            dalog_ref[0] += jnp.sum(jnp.where(lane < heads, dg * g, 0.0), axis=0, keepdims=True)
            ddtb_ref[0] += jnp.sum(da, axis=0, keepdims=True)

    row = lambda b, n, g: b * nch + (nch - 1 - n)
    rev = lambda n: nch - 1 - n
    return _pcall(
        body, grid=(bsz, nch, ng),
        in_specs=[pl.BlockSpec((3, c, wb), lambda b, n, g: (0, row(b, n, g), g)),
                  pl.BlockSpec((c, wb), lambda b, n, g: (row(b, n, g), z_col0 // wb + g)),
                  pl.BlockSpec((c, LANE), lambda b, n, g: (row(b, n, g), ab_col0 // LANE)),
                  pl.BlockSpec((1, LANE), lambda b, n, g: (0, 0)),
                  pl.BlockSpec((1, LANE), lambda b, n, g: (0, 0)),
                  pl.BlockSpec((1, hd), lambda b, n, g: (0, 0)),
                  pl.BlockSpec((1, 1, hb, hd, hd), lambda b, n, g: (b, rev(n), g, 0, 0)),
                  pl.BlockSpec((1, 1, hb, c, c), lambda b, n, g: (b, rev(n), g, 0, 0)),
                  pl.BlockSpec((c, wb), lambda b, n, g: (row(b, n, g), g))],
        out_specs=[pl.BlockSpec((3, c, wb), lambda b, n, g: (0, row(b, n, g), g)),
                   pl.BlockSpec((c, wb), lambda b, n, g: (row(b, n, g), g)),
                   pl.BlockSpec((c, LANE), lambda b, n, g: (row(b, n, g), 0)),
                   pl.BlockSpec((1, 1, LANE), lambda b, n, g: (b, 0, 0)),
                   pl.BlockSpec((1, 1, LANE), lambda b, n, g: (b, 0, 0)),
                   pl.BlockSpec((1, 1, hd), lambda b, n, g: (b, 0, 0))],
        out_shape=[jax.ShapeDtypeStruct((3, bsz * seq, heads * hd), F32),
                   jax.ShapeDtypeStruct((bsz * seq, heads * hd), BF16),
                   jax.ShapeDtypeStruct((bsz * seq, LANE), BF16),
                   jax.ShapeDtypeStruct((bsz, 1, LANE), F32),
                   jax.ShapeDtypeStruct((bsz, 1, LANE), F32),
                   jax.ShapeDtypeStruct((bsz, 1, hd), F32)],
        scratch_shapes=[pltpu.VMEM((heads, hd, hd), F32), pltpu.VMEM((c, LANE), F32), pltpu.VMEM((LANE, c), F32),
                        pltpu.VMEM((c, LANE), F32), pltpu.VMEM((c, LANE), F32), pltpu.VMEM((c, LANE), F32)],
        semantics=("parallel", "arbitrary", "arbitrary"),
        operands=[qkv, proj, proj, alog, dtb, gnw, ssave, tsave, dout], name="gdn_bwd", comms=comms)


ELEMWISE_BLOCK_ELEMS = 256 * 1024


def _rows_tile(rows, cols):
    want = max(16, ELEMWISE_BLOCK_ELEMS // cols)
    if rows <= want:
        return rows
    t = (want // 16) * 16
    while t > 16 and rows % t:
        t -= 16
    return t if rows % t == 0 else rows


def _piece_specs(pieces, tr, cols):
    specs, leads = [], []
    for p, (arr, lead) in enumerate(pieces):
        if arr.ndim == 3:
            specs.append(pl.BlockSpec((1, tr, cols), functools.partial(lambda i, idx, p: (idx[p], i, 0), p=p)))
        else:
            specs.append(pl.BlockSpec((tr, cols), lambda i, idx: (i, 0)))
        leads.append(jnp.asarray(0 if lead is None else lead, jnp.int32))
    return jnp.stack(leads), specs


def _sum_pieces(refs):
    total = None
    for r in refs:
        v = r[...].astype(F32)
        v = v.reshape(v.shape[-2:])
        total = v if total is None else total + v
    return total


def _adamw(w, m, v, pieces, name, comms=()):
    rows, cols = w.shape
    tr = _rows_tile(rows, cols)
    leads, pspecs = _piece_specs(pieces, tr, cols)
    npc = len(pieces)
    c1 = 1.0 - ADAM_B1 ** ADAM_STEP
    c2 = 1.0 - ADAM_B2 ** ADAM_STEP

    def body(idx_ref, w_ref, m_ref, v_ref, *rest):
        g = _sum_pieces(rest[:npc])
        g_ref, d_ref, nm_ref, nv_ref = rest[npc:]
        nm = ADAM_B1 * m_ref[...] + (1.0 - ADAM_B1) * g
        nv = ADAM_B2 * v_ref[...] + (1.0 - ADAM_B2) * (g * g)
        g_ref[...] = g
        nm_ref[...] = nm
        nv_ref[...] = nv
        d_ref[...] = -ADAM_LR * ((nm / c1) / (jnp.sqrt(nv / c2) + ADAM_EPS) + ADAM_WD * w_ref[...])

    wspec = pl.BlockSpec((tr, cols), lambda i, idx: (i, 0))
    res, lands = _pcall(body, grid=(rows // tr,), in_specs=[wspec] * 3 + pspecs, out_specs=[wspec] * 4,
                        out_shape=[jax.ShapeDtypeStruct((rows, cols), F32)] * 4, semantics=("parallel",),
                        prefetch=[leads], operands=[w, m, v, *[p for p, _ in pieces]], name=name, comms=comms)
    return (res, lands) if comms else res


def _sum_to(pieces, out_dtype, name):
    arr0 = pieces[0][0]
    rows, cols = arr0.shape[-2:]
    tr = _rows_tile(rows, cols)
    leads, pspecs = _piece_specs(pieces, tr, cols)

    def body(idx_ref, *rest):
        rest[-1][...] = _sum_pieces(rest[:-1]).astype(out_dtype)

    return pl.pallas_call(
        body,
        grid_spec=pltpu.PrefetchScalarGridSpec(num_scalar_prefetch=1, grid=(rows // tr,), in_specs=pspecs,
                                               out_specs=pl.BlockSpec((tr, cols), lambda i, idx: (i, 0))),
        out_shape=jax.ShapeDtypeStruct((rows, cols), out_dtype),
        compiler_params=_params("parallel"), name=name)(leads, *[p for p, _ in pieces])


def _pair_add(a, recv, place, name, comms=()):
    _, rows, cols = a.shape
    tr = _rows_tile(rows, cols)
    x, y, c = place
    idx = jnp.stack([2 * (1 - x) + y, 2 * x + (1 - y), 2 * (1 - x) + (1 - y), c]).astype(jnp.int32)

    def body(p_ref, a_ref, r_ref, o_ref):
        o_ref[...] = (a_ref[...].astype(F32) + r_ref[...].astype(F32)).astype(BF16)

    res, lands = _pcall(
        body, grid=(3, rows // tr),
        in_specs=[pl.BlockSpec((1, tr, cols), lambda j, i, p: (2 * p[j] + p[3], i, 0)),
                  pl.BlockSpec((1, tr, cols), lambda j, i, p: (p[j], i, 0))],
        out_specs=[pl.BlockSpec((1, tr, cols), lambda j, i, p: (j, i, 0))],
        out_shape=[jax.ShapeDtypeStruct((3, rows, cols), BF16)], semantics=("parallel", "parallel"),
        prefetch=[idx], operands=[a, recv], name=name, comms=comms)
    return (res[0], lands) if comms else res[0]


def _to_frame(w, offs, fw, name):
    rows, n = w.shape
    tr = _tile(rows, 256)

    def body(off_ref, w_ref, o_ref, pad_ref):
        pad_ref[...] = jnp.zeros_like(pad_ref)
        pad_ref[:, 0:n] = w_ref[...]
        y = pad_ref[...]
        off1, len1, off2 = off_ref[0], off_ref[1], off_ref[2]
        col = lax.broadcasted_iota(jnp.int32, y.shape, 1)
        o_ref[0] = jnp.where(col < off1 + len1, pltpu.roll(y, off1, axis=1),
                             jnp.where(col >= off2 + len1, pltpu.roll(y, off2, axis=1), 0.0)).astype(BF16)

    res, _ = _pcall(body, grid=(rows // tr,), in_specs=[pl.BlockSpec((tr, n), lambda i, o: (i, 0))],
                    out_specs=[pl.BlockSpec((1, tr, fw), lambda i, o: (o[3], i, 0))],
                    out_shape=[jax.ShapeDtypeStruct((N_DEV, rows, fw), BF16)], scratch_shapes=[pltpu.VMEM((tr, fw), F32)],
                    semantics=("parallel",), prefetch=[offs], operands=[w], name=name)
    return res[0]


def _in_proj_core(xn, frames, ids, lay, into, name, comms=()):
    t, d = xn.shape
    n = ids.shape[0]
    tm = _tile(t, 1024)
    nb = max(b for b in range(1, lay.nc + 1) if lay.nc % b == 0 and b * MXU <= 768)
    tn, nj = nb * MXU, lay.nc // nb

    def body(ids_ref, a_ref, b_ref, *rest):
        o_ref = rest[-1]
        o_ref[...] = jnp.dot(a_ref[...], b_ref[0], preferred_element_type=F32)

    col = lambda s, j: pl.multiple_of(lay.frame_block(s) * MXU + lay.c0 * MXU + j * tn, MXU)
    n_into = 0 if into is None else 1
    res, lands = _pcall(
        body, grid=(t // tm, n, nj),
        in_specs=[pl.BlockSpec((tm, d), lambda i, f, j, ids: (i, 0)),
                  pl.BlockSpec((pl.Element(1), pl.Element(d), pl.Element(tn)),
                               lambda i, f, j, ids: (f, 0, pl.multiple_of(lay.c0 * MXU + j * tn, MXU)))]
        + [pl.BlockSpec(memory_space=pl.ANY)] * n_into,
        out_specs=[pl.BlockSpec((pl.Element(tm), pl.Element(tn)), lambda i, f, j, ids: (i * tm, col(ids[f], j)))],
        out_shape=[jax.ShapeDtypeStruct((t, lay.wp), F32)], semantics=("parallel", "arbitrary", "arbitrary"),
        prefetch=[ids], operands=[xn, frames] + ([into] if n_into else []), name=name, comms=comms,
        fill={2: 0} if n_into else None)
    return res[0], lands


def _in_proj_rest(xn, frames, table, wp, into, name, comms=()):
    t, d = xn.shape
    tm = _tile(t, 1024)
    n = table.shape[1]

    def body(t_ref, a_ref, b1_ref, b2_ref, *rest):
        j = pl.program_id(1)
        b = b1_ref[0]
        b = jnp.where(t_ref[5, j] > 0, b + b2_ref[0], b)
        rest[-1][...] = jnp.dot(a_ref[...], b, preferred_element_type=F32)

    n_into = 0 if into is None else 1
    res, lands = _pcall(
        body, grid=(t // tm, n),
        in_specs=[pl.BlockSpec((tm, d), lambda i, j, tb: (i, 0)),
                  pl.BlockSpec((1, d, MXU), lambda i, j, tb: (tb[1, j], 0, tb[2, j])),
                  pl.BlockSpec((1, d, MXU), lambda i, j, tb: (tb[3, j], 0, tb[4, j]))]
        + [pl.BlockSpec(memory_space=pl.ANY)] * n_into,
        out_specs=[pl.BlockSpec((tm, MXU), lambda i, j, tb: (i, tb[0, j]))],
        out_shape=[jax.ShapeDtypeStruct((t, wp), F32)], semantics=("parallel", "arbitrary"),
        prefetch=[table], operands=[xn, frames, frames] + ([into] if n_into else []), name=name, comms=comms,
        fill={3: 0} if n_into else None)
    return res[0], lands


def _place():
    x, y, c = lax.axis_index("x"), lax.axis_index("y"), lax.axis_index("c")
    chips = [(1 - x, y), (x, 1 - y), (1 - x, 1 - y)]
    return x, y, c, chips


def _allreduce_small(buf, name):
    rows = buf.shape[0]

    def body(x_ref, o_ref, g_ref, send_sems, recv_sems):
        x, y, c, chips = _place()
        me, sibling = (x, y, c), (x, y, 1 - c)

        def copy(k, block, to, src=None):
            dst = g_ref.at[4 * block[0] + 2 * block[1] + block[2]]
            return pltpu.make_async_remote_copy(src_ref=dst if src is None else src, dst_ref=dst,
                                                send_sem=send_sems.at[k], recv_sem=recv_sems.at[k],
                                                device_id=to, device_id_type=MESH)

        first = [copy(0, me, sibling, src=x_ref)]
        first += [copy(1 + j, me, (*chip, c), src=x_ref) for j, chip in enumerate(chips)]
        for cp in first:
            cp.start()
        passed = [copy(4 + j, (*chip, c), sibling) for j, chip in enumerate(chips)]
        for j, chip in enumerate(chips):
            copy(1 + j, (*chip, c), me).wait_recv()
            passed[j].start()
        copy(0, sibling, me).wait_recv()
        for j, chip in enumerate(chips):
            copy(4 + j, (*chip, 1 - c), me).wait_recv()
        for cp in first + passed:
            cp.wait_send()
        g_ref[4 * x + 2 * y + c] = x_ref[...]
        total = g_ref[0]
        for s in range(1, N_DEV):
            total = total + g_ref[s]
        o_ref[...] = total

    vm = pl.BlockSpec(memory_space=pltpu.VMEM)
    return pl.pallas_call(
        body, in_specs=[vm], out_specs=vm, out_shape=jax.ShapeDtypeStruct((rows, LANE), F32),
        scratch_shapes=[pltpu.VMEM((N_DEV, rows, LANE), F32), pltpu.SemaphoreType.DMA((7,)), pltpu.SemaphoreType.DMA((7,))],
        name=name)(buf)


def _rows(ref, rows):
    return ref if rows is None else ref.at[pl.ds(rows[0], rows[1] - rows[0])]


AG_ALL = ("here", "sibling", 0, 1, 2)


def _ag(shard=None, into=None, to=(), forward=(), rows=None):
    def plan(srcs, lands):
        x, y, c, chips = _place()
        buf = lands[0]
        out = []
        if to:
            dst = _rows(buf.at[4 * x + 2 * y + c], rows)
            src = dst if shard is None else _rows(srcs[0], rows)
            for who in to:
                if who == "here":
                    out.append(("local", src, dst, None))
                elif who == "sibling":
                    out.append(("remote", src, dst, (x, y, 1 - c)))
                else:
                    out.append(("remote", src, dst, (*chips[who], c)))
        for j in forward:
            r = _rows(buf.at[4 * chips[j][0] + 2 * chips[j][1] + c], rows)
            out.append(("remote", r, r, (x, y, 1 - c)))
        return out

    srcs = ([shard] if to and shard is not None else []) + ([into] if into is not None else [])
    land = jax.ShapeDtypeStruct(into.shape, into.dtype) if into is not None else jax.ShapeDtypeStruct((N_DEV,) + shard.shape, shard.dtype)
    return _Comm(srcs, [land], plan, len(to) + len(forward), alias={len(srcs) - 1: 0} if into is not None else None)


def _ag_first(shard, rows=None, into=None, to=AG_ALL):
    return _ag(shard=shard, into=into, to=to, rows=rows)


def _ag_second(g, rows=None, of=(0, 1, 2)):
    return _ag(into=g, forward=of, rows=rows)


def _rs_first(grad):
    def plan(srcs, lands):
        x, y, c, _ = _place()
        (a,), (land,) = srcs, lands
        return [("remote", a.at[2 * j + (1 - c)], land.at[j], (x, y, 1 - c)) for j in range(4)]

    return _Comm([grad], [jax.ShapeDtypeStruct((4,) + grad.shape[1:], grad.dtype)], plan, 4)


def _rs_second(pair, rows=None, into=None):
    def plan(srcs, lands):
        x, y, c, chips = _place()
        return [("remote", _rows(srcs[0].at[j], rows), _rows(lands[0].at[j], rows), (cx, cy, c))
                for j, (cx, cy) in enumerate(chips)]

    land = jax.ShapeDtypeStruct((3,) + pair.shape[1:], pair.dtype)
    if into is None:
        return _Comm([pair], [land], plan, 3)
    return _Comm([pair, into], [land], plan, 3, alias={1: 0})


class _InLayout:
    def __init__(self, n_in, gw, heads, scw):
        self.n_in, self.split = n_in, 4 * gw + 2 * heads
        self.gap = LANE - 2 * heads
        self.ab_col, self.sc_col = 4 * gw, 4 * gw + LANE
        self.used = 4 * gw + LANE + 3 * scw
        p0 = [s * n_in + (self.gap if s * n_in >= self.split else 0) for s in range(N_DEV)]
        self.fstart = [(p // MXU) * MXU for p in p0]
        need = []
        for s in range(N_DEV):
            straddle = s * n_in < self.split < (s + 1) * n_in
            need.append(p0[s] - self.fstart[s] + n_in + (self.gap if straddle else 0))
        self.fw = -(-max(need) // MXU) * MXU
        self.wp = max(f + self.fw for f in self.fstart)
        assert self.wp >= self.used and self.wp % MXU == 0
        nfb = self.fw // MXU
        rows = []
        for jb in range(self.wp // MXU):
            src = [(s, jb - self.fstart[s] // MXU) for s in range(N_DEV) if 0 <= jb - self.fstart[s] // MXU < nfb]
            assert 1 <= len(src) <= 2, (jb, src)
            (s1, b1), (s2, b2) = src[0], src[-1]
            rows.append((s1, b1, s2, b2, int(len(src) == 2)))
        self.table = np.asarray(rows, np.int32).T.copy()
        single = [all(rows[self.fstart[s] // MXU + b][4] == 0 for s in range(N_DEV)) for b in range(nfb)]
        runs, b = [], 0
        while b < nfb:
            if single[b]:
                e = b
                while e < nfb and single[e]:
                    e += 1
                runs.append((e - b, b))
                b = e
            else:
                b += 1
        self.nc, self.c0 = max(runs) if runs else (0, 0)
        in_core = {self.fstart[s] // MXU + b for s in range(N_DEV) for b in range(self.c0, self.c0 + self.nc)}
        self.rest = np.asarray([(jb,) + rows[jb] for jb in range(self.wp // MXU) if jb not in in_core], np.int32).T.copy()

    def frame_block(self, s):
        p = s * self.n_in
        return (p + jnp.where(p >= self.split, self.gap, 0)) // MXU

    def offsets(self, s):
        p = s * self.n_in
        after = p >= self.split
        off1 = p + jnp.where(after, self.gap, 0) - self.frame_block(s) * MXU
        len1 = jnp.where(after, self.n_in, jnp.clip(self.split - p, 0, self.n_in))
        off2 = off1 + jnp.where(len1 < self.n_in, self.gap, 0)
        return off1, len1, off2

    def to_frame(self, w, s):
        return _to_frame(w, jnp.stack(self.offsets(s) + (s,)).astype(jnp.int32), self.fw, "w_in_frame")

    def from_frame(self, f, s):
        off1, len1, off2 = self.offsets(s)
        a = lax.dynamic_slice(f, (0, off1), (f.shape[0], self.n_in))
        b = lax.dynamic_slice(f, (0, off2), (f.shape[0], self.n_in))
        col = lax.broadcasted_iota(jnp.int32, (1, self.n_in), 1)
        return jnp.where(col < len1, a, b)


def _pack_rows(parts):
    rows = []
    for p in parts:
        flat = p.reshape(-1)
        pad = (-flat.shape[0]) % LANE
        rows.append(jnp.pad(flat, (0, pad)).reshape(-1, LANE))
    buf = jnp.concatenate(rows, axis=0)
    return jnp.pad(buf, ((0, (-buf.shape[0]) % 8), (0, 0)))


def _unpack_rows(buf, shapes):
    out, r = [], 0
    for shp in shapes:
        size = int(np.prod(shp))
        nr = -(-size // LANE)
        out.append(buf[r:r + nr].reshape(-1)[:size].reshape(shp))
        r += nr
    return out


def _pad_lanes(v):
    return jnp.pad(v, ((0, 0), (0, LANE - v.shape[1])))


def kernel(x, norm_mix_pre, w_in, conv_qkv_w, a_log, dt_bias, gdn_norm_w, conv_sc_w, w_out, norm_mix_post, norm_mlp_pre, w_up, w_down, norm_mlp_post, loss_target, m_norm_mix_pre, m_w_in, m_conv_qkv_w, m_a_log, m_dt_bias, m_gdn_norm_w, m_conv_sc_w, m_w_out, m_norm_mix_post, m_norm_mlp_pre, m_w_up, m_w_down, m_norm_mlp_post, v_norm_mix_pre, v_w_in, v_conv_qkv_w, v_a_log, v_dt_bias, v_gdn_norm_w, v_conv_sc_w, v_w_out, v_norm_mix_post, v_norm_mlp_pre, v_w_up, v_w_down, v_norm_mlp_post):
    bsz, seq, d = x.shape
    t = bsz * seq
    heads, hd = a_log.shape[-1], gdn_norm_w.shape[-1]
    gw = heads * hd
    scw = conv_sc_w.shape[-1] * N_DEV
    dff_w = w_up.shape[-1] * N_DEV
    lay = _InLayout(w_in.shape[-1], gw, heads, scw)
    mx, my, mc = lax.axis_index("x"), lax.axis_index("y"), lax.axis_index("c")
    me = 4 * mx + 2 * my + mc
    chip = 2 * mx + my

    x2 = x.reshape(t, d)
    tgt = loss_target.reshape(t, d)
    g1, g2, g3, g4 = norm_mix_pre, norm_mix_post, norm_mlp_pre, norm_mlp_post

    g_in = lay.to_frame(w_in[0], me)
    w_out_b, w_up_b, w_down_b = w_out[0].astype(BF16), w_up[0].astype(BF16), w_down[0].astype(BF16)
    up_cols = dff_w // N_DEV
    qu, qd = d // 4, up_cols // 4
    kq, ks = conv_qkv_w.shape[1], conv_sc_w.shape[1]
    cq_n, cs_n = conv_qkv_w.shape[-1], conv_sc_w.shape[-1]
    cq_full = lax.dynamic_update_slice(jnp.zeros((kq, 3 * gw), F32), conv_qkv_w[0], (0, me * cq_n))
    cs_full = lax.dynamic_update_slice(jnp.zeros((ks, scw), F32), conv_sc_w[0], (0, me * cs_n))
    conv_q, conv_s = _unpack_rows(_allreduce_small(_pack_rows([cq_full, cs_full]), "allgather_conv"),
                                  [(kq, 3 * gw), (ks, scw)])
    alog_t, dtb_t = _pad_lanes(a_log), _pad_lanes(dt_bias)

    xn, (g_in,) = _rms_fwd(x2, g1, comms=[_ag(into=g_in, to=("sibling",))])

    def arrived(k):
        return jnp.stack([lax.dynamic_index_in_dim(g_in, s, 0, keepdims=False) for s in ids[k]])

    dev = lambda px, py, pc: (4 * px + 2 * py + pc).astype(jnp.int32)
    ids = [jnp.stack([dev(mx, my, mc), dev(mx, my, 1 - mc)]),
           jnp.stack([dev(1 - mx, my, mc), dev(mx, 1 - my, mc)]),
           jnp.stack([dev(1 - mx, my, 1 - mc), dev(mx, 1 - my, 1 - mc)]),
           jnp.stack([dev(1 - mx, 1 - my, mc), dev(1 - mx, 1 - my, 1 - mc)])]
    assert lay.nc > 0, "the frames have no columns of their own at these sizes"
    proj, (g_in,) = _in_proj_core(xn, arrived(0), ids[0], lay, None, "in_proj_0", comms=[_ag(into=g_in, to=(0, 1))])
    proj, (g_in,) = _in_proj_core(xn, arrived(1), ids[1], lay, proj, "in_proj_1",
                                  comms=[_ag(into=g_in, to=(2,), forward=(0, 1))])
    proj, (g_in, g_out) = _in_proj_core(xn, arrived(2), ids[2], lay, proj, "in_proj_2",
                                        comms=[_ag(into=g_in, forward=(2,)), _ag_first(w_out_b)])
    proj, (g_up,) = _in_proj_core(xn, arrived(3), ids[3], lay, proj, "in_proj_3", comms=[_ag_first(w_up_b, rows=(0, qu))])
    proj, (g_up,) = _in_proj_rest(xn, g_in, jnp.asarray(lay.rest), lay.wp, proj, "in_proj_rest",
                                  comms=[_ag_first(w_up_b, rows=(qu, 2 * qu), into=g_up)])
    qkv, (g_out, g_up) = _qkvconv_fwd(proj, conv_q, bsz, seq, gw,
                                      comms=[_ag_second(g_out), _ag_first(w_up_b, rows=(2 * qu, 3 * qu), into=g_up)])
    (gdn_out, ssave, tsave), (g_up, g_down) = _gdn_fwd(
        qkv, proj, alog_t, dtb_t, gdn_norm_w, bsz, seq, heads, 3 * gw, lay.ab_col,
        comms=[_ag_first(w_up_b, rows=(3 * qu, 4 * qu), into=g_up), _ag_first(w_down_b, rows=(0, qd))])
    sc_out = _sc_fwd(proj, conv_s, bsz, seq, scw, lay.sc_col)
    mixed = jnp.concatenate([gdn_out, sc_out], axis=1)
    w_out_f = g_out.reshape(d, d)
    (mix,), (g_up, g_down) = _matmul(mixed, w_out_f, mode="nn", out_dtypes=[F32], name="out_proj",
                                     comms=[_ag_second(g_up), _ag_first(w_down_b, rows=(qd, 2 * qd), into=g_down)])
    h, xn2 = _post1(x2, mix, g2, g3)

    def up_epilogue(acc):
        r = jnp.maximum(acc, 0.0)
        return r, r * r

    tq = t // 4
    act_hid = None
    for part in range(4):
        if part < 2:
            leg = [_ag_first(w_down_b, rows=((2 + part) * qd, (3 + part) * qd), into=g_down)]
        else:
            leg = [_ag_second(g_down)] if part == 2 else []
        res = _matmul(
            xn2, g_up, mode="nn", out_dtypes=[BF16, BF16], name="mlp_up_%d" % part, n_cols=dff_w, epilogue=up_epilogue,
            b_spec=lambda tk, tn: pl.BlockSpec((1, tk, tn), lambda i, j, k: (j // (up_cols // tn), k, j % (up_cols // tn))),
            a_rows=(part * tq, tq), out_into=act_hid, comms=leg)
        if leg:
            act_hid, (g_down,) = res
        else:
            act_hid = res
    act, hid = act_hid
    w_down_f = g_down.reshape(dff_w, d)
    (ff,) = _matmul(hid, w_down_f, mode="nn", out_dtypes=[F32], name="mlp_down", tn=1024, tk=2048)
    dff, dy, dg4, loss_p = _post2_loss(h, ff, g4, tgt)

    def pieces(part, sib, got):
        return [(part, me), (sib, chip), (got, 0), (got, 1), (got, 2)]

    place = (mx, my, mc)

    (dpre,) = _matmul(dff, w_down_f, mode="nt", out_dtypes=[BF16], name="d_hidden", extras=[act],
                      epilogue=lambda acc, a: (acc * (2.0 * a.astype(F32)),))
    (dw_down,) = _matmul(hid, dff, mode="tn", out_dtypes=[BF16], name="dw_down")
    dw_down = dw_down.reshape(N_DEV, dff_w // N_DEV, d)
    (dxn2,), (sib_down,) = _matmul(
        dpre, g_up, mode="nt", out_dtypes=[F32], name="d_xn2", n_cols=d, tn=1024, tk=min(up_cols, 2048),
        b_spec=lambda tk, tn: pl.BlockSpec((1, tn, tk), lambda i, j, k: (k // (up_cols // tk), j, k % (up_cols // tk))),
        comms=[_rs_first(dw_down)])
    pair_down = _pair_add(dw_down, sib_down, place, "pair_add_down")
    (dw_up,), (got_down,) = _matmul(
        xn2, dpre, mode="tn", out_dtypes=[BF16], name="dw_up",
        out_custom=lambda tm, tn: ((N_DEV, d, up_cols), (1, tm, tn), lambda i, j, k: (j // (up_cols // tn), i, j % (up_cols // tn))),
        comms=[_rs_second(pair_down, rows=(0, 3 * qd))])
    (dmix, dh, dg2, dg3), (got_down, sib_up) = _mid_bwd(
        h, mix, dy, dxn2, g2, g3, comms=[_rs_second(pair_down, rows=(3 * qd, 4 * qd), into=got_down), _rs_first(dw_up)])
    pair_up = _pair_add(dw_up, sib_up, place, "pair_add_up")
    (dmixed,) = _matmul(dmix, w_out_f, mode="nt", out_dtypes=[F32], name="d_mixed")
    (dw_out,), (got_up,) = _matmul(mixed, dmix, mode="tn", out_dtypes=[BF16], name="dw_out",
                                   comms=[_rs_second(pair_up, rows=(0, qu))])
    dw_out = dw_out.reshape(N_DEV, d // N_DEV, d)
    (dscb, dscc, dsch, dconv_s), (sib_out,) = _sc_bwd(proj, conv_s, dmixed, bsz, seq, scw, lay.sc_col, gw,
                                                      comms=[_rs_first(dw_out)])
    pair_out = _pair_add(dw_out, sib_out, place, "pair_add_out")
    (dact, dz, dab, dalog, ddtb, dgnw), (got_up,) = _gdn_bwd(
        qkv, proj, alog_t, dtb_t, gdn_norm_w, ssave, tsave, dmixed, bsz, seq, heads, 3 * gw, lay.ab_col,
        comms=[_rs_second(pair_up, rows=(qu, 3 * qu), into=got_up)])
    (dqkv, dconv_q), (got_up,) = _qkvconv_bwd(proj, conv_q, dact, bsz, seq, gw,
                                              comms=[_rs_second(pair_up, rows=(3 * qu, 4 * qu), into=got_up)])
    dproj = jnp.concatenate([dqkv, dz, dab, dscb, dscc, dsch, jnp.zeros((t, lay.wp - lay.used), BF16)], axis=1)
    tn_in = _tile(lay.fw, 1024)
    nfb = lay.fw // tn_in
    (dw_in,), (got_out,) = _matmul(
        xn, dproj, mode="tn", out_dtypes=[BF16], name="dw_in", n_cols=N_DEV * lay.fw, tn=tn_in,
        b_spec=lambda tk, tn: pl.BlockSpec(
            (pl.Element(tk), pl.Element(tn)),
            lambda i, j, k: (0, pl.multiple_of(lay.frame_block(j // nfb) * MXU + (j % nfb) * tn, LANE))),
        out_custom=lambda tm, tn: ((N_DEV, d, lay.fw), (1, tm, tn), lambda i, j, k: (j // nfb, i, j % nfb)),
        comms=[_rs_second(pair_out)])
    tk_in = _tile(lay.fw, 2048)
    kpf = lay.fw // tk_in

    def d_xn(part, into, comms):
        return _matmul(
            dproj, g_in, mode="nt", out_dtypes=[F32], name="d_xn_%d" % part, n_cols=d, tn=1024, tk=tk_in,
            k_total=N_DEV * lay.fw, a_rows=(part * tq, tq), out_into=None if into is None else [into], comms=comms,
            a_spec=lambda tm, tk, r0: pl.BlockSpec(
                (pl.Element(tm), pl.Element(tk)),
                lambda i, j, k: (pl.multiple_of(i * tm + r0, 16),
                                 pl.multiple_of(lay.frame_block(k // kpf) * MXU + (k % kpf) * tk, LANE))),
            b_spec=lambda tk, tn: pl.BlockSpec((1, tn, tk), lambda i, j, k: (k // kpf, j, k % kpf)))

    (dxn,), (sib_in,) = d_xn(0, None, [_rs_first(dw_in)])
    pair_in = _pair_add(dw_in, sib_in, place, "pair_add_in")
    (dxn,), (got_in,) = d_xn(1, dxn, [_rs_second(pair_in, rows=(0, qu))])
    (dxn,), (got_in,) = d_xn(2, dxn, [_rs_second(pair_in, rows=(qu, 2 * qu), into=got_in)])
    (dxn,), (got_in,) = d_xn(3, dxn, [_rs_second(pair_in, rows=(2 * qu, 4 * qu), into=got_in)])
    grad_x, dg1 = _pre_bwd(x2, dh, dxn, g1)

    gin_frame = _sum_to(pieces(dw_in, sib_in, got_in), F32, "grad_w_in_frame")
    big = {
        "w_in": _adamw(w_in[0], m_w_in[0], v_w_in[0], [(lay.from_frame(gin_frame, me), None)], "adamw_w_in"),
        "w_out": _adamw(w_out[0], m_w_out[0], v_w_out[0], pieces(dw_out, sib_out, got_out), "adamw_w_out"),
        "w_up": _adamw(w_up[0], m_w_up[0], v_w_up[0], pieces(dw_up, sib_up, got_up), "adamw_w_up"),
        "w_down": _adamw(w_down[0], m_w_down[0], v_w_down[0], pieces(dw_down, sib_down, got_down), "adamw_w_down"),
    }

    small_shapes = [(kq, 3 * gw), (ks, scw), (1, d), (1, d), (1, d), (1, d), (1, LANE), (1, LANE), (1, hd), (1, LANE)]
    small = _unpack_rows(
        _allreduce_small(_pack_rows([dconv_q, dconv_s, dg1, dg2, dg3, dg4, jnp.sum(dalog, axis=0), jnp.sum(ddtb, axis=0),
                                     jnp.sum(dgnw, axis=0), loss_p]), "allreduce_small"), small_shapes)
    gq, gs, sg1, sg2, sg3, sg4, salog, sdtb, sgnw, sloss = small
    loss = sloss[0, 0]
    small_grads = {
        "norm_mix_pre": sg1, "conv_qkv_w": lax.dynamic_slice(gq, (0, me * cq_n), (kq, cq_n)),
        "a_log": salog[:, :heads], "dt_bias": sdtb[:, :heads], "gdn_norm_w": sgnw,
        "conv_sc_w": lax.dynamic_slice(gs, (0, me * cs_n), (ks, cs_n)),
        "norm_mix_post": sg2, "norm_mlp_pre": sg3, "norm_mlp_post": sg4,
    }
    weights = {"norm_mix_pre": (norm_mix_pre, m_norm_mix_pre, v_norm_mix_pre), "conv_qkv_w": (conv_qkv_w[0], m_conv_qkv_w[0], v_conv_qkv_w[0]),
               "a_log": (a_log, m_a_log, v_a_log), "dt_bias": (dt_bias, m_dt_bias, v_dt_bias),
               "gdn_norm_w": (gdn_norm_w, m_gdn_norm_w, v_gdn_norm_w), "conv_sc_w": (conv_sc_w[0], m_conv_sc_w[0], v_conv_sc_w[0]),
               "norm_mix_post": (norm_mix_post, m_norm_mix_post, v_norm_mix_post),
               "norm_mlp_pre": (norm_mlp_pre, m_norm_mlp_pre, v_norm_mlp_pre),
               "norm_mlp_post": (norm_mlp_post, m_norm_mlp_post, v_norm_mlp_post)}
    res = dict(big)
    for name, (w, m, v) in weights.items():
        res[name] = _adamw(w, m, v, [(small_grads[name], None)], "adamw_" + name)

    order = ["norm_mix_pre", "w_in", "conv_qkv_w", "a_log", "dt_bias", "gdn_norm_w", "conv_sc_w", "w_out", "norm_mix_post",
             "norm_mlp_pre", "w_up", "w_down", "norm_mlp_post"]
    shapes = {"norm_mix_pre": norm_mix_pre.shape, "w_in": w_in.shape, "conv_qkv_w": conv_qkv_w.shape, "a_log": a_log.shape,
              "dt_bias": dt_bias.shape, "gdn_norm_w": gdn_norm_w.shape, "conv_sc_w": conv_sc_w.shape, "w_out": w_out.shape,
              "norm_mix_post": norm_mix_post.shape, "norm_mlp_pre": norm_mlp_pre.shape, "w_up": w_up.shape,
              "w_down": w_down.shape, "norm_mlp_post": norm_mlp_post.shape}
    outs = [loss, grad_x.reshape(bsz, seq, d)]
    for part in range(4):
        outs += [res[nm][part].reshape(shapes[nm]) for nm in order]
    return tuple(outs)
```

```python
import functools
import math

import numpy as np
import jax
import jax.numpy as jnp
from jax import lax
from jax.experimental import pallas as pl
from jax.experimental.pallas import tpu as pltpu

F32 = jnp.float32
BF16 = jnp.bfloat16
HI = lax.Precision.HIGHEST
MESH = pl.DeviceIdType.MESH

N_DEV = 8
LANE = 128
MXU = 256
CHUNK = 64
NORM_EPS = 1e-6
L2_EPS = 1e-6
VMEM_LIMIT = 56 * 1024 * 1024

ADAM_LR = 0.001
ADAM_B1 = 0.9
ADAM_B2 = 0.999
ADAM_EPS = 1e-08
ADAM_WD = 0.01
ADAM_STEP = 10

NN = (((1,), (0,)), ((), ()))
NT = (((1,), (1,)), ((), ()))
TN = (((0,), (0,)), ((), ()))


def _params(*sem):
    return pltpu.CompilerParams(dimension_semantics=sem, vmem_limit_bytes=VMEM_LIMIT)


def _tile(n, want):
    if n <= want:
        return n
    t = (want // LANE) * LANE
    while t > LANE and n % t:
        t -= LANE
    assert n % t == 0, (n, want)
    return t


class _Comm:
    def __init__(self, srcs, lands, plan, n, alias=None):
        self.srcs, self.lands, self.plan, self.n, self.alias = list(srcs), list(lands), plan, n, dict(alias or {})


def _pcall(body, *, grid, in_specs, out_specs, out_shape, operands, name, scratch_shapes=(), semantics=None,
           prefetch=(), comms=(), fill=None):
    n_pf, n_in, n_out, n_scr = len(prefetch), len(in_specs), len(out_specs), len(scratch_shapes)
    srcs = [s for cm in comms for s in cm.srcs]
    lands = [l for cm in comms for l in cm.lands]
    n_src, n_land = len(srcs), len(lands)
    n_copies = sum(cm.n for cm in comms)
    aliases, so, lo = {n_pf + a: b for a, b in (fill or {}).items()}, 0, 0
    for cm in comms:
        for a, b in cm.alias.items():
            aliases[n_pf + n_in + so + a] = n_out + lo + b
        so, lo = so + len(cm.srcs), lo + len(cm.lands)
    any_spec = pl.BlockSpec(memory_space=pl.ANY)

    def wrapped(*refs):
        pf, r = refs[:n_pf], refs[n_pf:]
        ins, csrc = r[:n_in], r[n_in:n_in + n_src]
        outs = r[n_in + n_src:n_in + n_src + n_out]
        cland = r[n_in + n_src + n_out:n_in + n_src + n_out + n_land]
        rest = r[n_in + n_src + n_out + n_land:]
        scratch = rest[:n_scr]
        if not comms:
            body(*pf, *ins, *outs, *scratch)
            return
        send_sems, recv_sems = rest[n_scr:]

        def copies():
            out, k, s0, l0 = [], 0, 0, 0
            for cm in comms:
                for kind, src, dst, dev in cm.plan(csrc[s0:s0 + len(cm.srcs)], cland[l0:l0 + len(cm.lands)]):
                    if kind == "local":
                        out.append((kind, pltpu.make_async_copy(src, dst, send_sems.at[k])))
                    else:
                        out.append((kind, pltpu.make_async_remote_copy(
                            src_ref=src, dst_ref=dst, send_sem=send_sems.at[k], recv_sem=recv_sems.at[k],
                            device_id=dev, device_id_type=MESH)))
                    k += 1
                s0, l0 = s0 + len(cm.srcs), l0 + len(cm.lands)
            assert k == n_copies
            return out

        ids = [pl.program_id(a) for a in range(len(grid))]
        first = functools.reduce(jnp.logical_and, [i == 0 for i in ids])
        last = functools.reduce(jnp.logical_and, [i == g - 1 for i, g in zip(ids, grid)])

        @pl.when(first)
        def _():
            for _, cp in copies():
                cp.start()

        body(*pf, *ins, *outs, *scratch)

        @pl.when(last)
        def _():
            cps = copies()
            for kind, cp in cps:
                if kind == "remote":
                    cp.wait_recv()
            for kind, cp in cps:
                if kind == "remote":
                    cp.wait_send()
                else:
                    cp.wait()

    sems = [pltpu.SemaphoreType.DMA((n_copies,)), pltpu.SemaphoreType.DMA((n_copies,))] if comms else []
    if semantics is None or comms:
        semantics = ("arbitrary",) * len(grid)
    res = pl.pallas_call(
        wrapped,
        grid_spec=pltpu.PrefetchScalarGridSpec(
            num_scalar_prefetch=n_pf, grid=tuple(grid), in_specs=list(in_specs) + [any_spec] * n_src,
            out_specs=list(out_specs) + [any_spec] * n_land, scratch_shapes=list(scratch_shapes) + sems),
        out_shape=list(out_shape) + lands,
        input_output_aliases=aliases,
        compiler_params=_params(*semantics), name=name)(*prefetch, *operands, *srcs)
    return list(res[:n_out]), list(res[n_out:])


def _bdot(a, b, dims=NN):
    return lax.dot_general(a.astype(BF16), b.astype(BF16), dims, preferred_element_type=F32)


def _hdot(a, b, dims=NN):
    return lax.dot_general(a, b, dims, preferred_element_type=F32, precision=HI)


def _mdot(a, b, dims=NN):
    return lax.dot_general(a, b, dims, preferred_element_type=F32, precision=lax.Precision.HIGH)


def _sigmoid(x):
    return 1.0 / (1.0 + jnp.exp(-x))


def _softplus(x):
    return jnp.maximum(x, 0.0) + jnp.log(1.0 + jnp.exp(-jnp.abs(x)))


def _matmul(a, b, *, mode, out_dtypes, name, n_cols=None, tm=1024, tn=512, tk=4096, epilogue=None, extras=(),
            b_spec=None, out_custom=None, a_rows=None, out_into=None, a_spec=None, k_total=None, comms=()):
    if mode == "tn":
        K, M = a.shape
    else:
        M, K = a.shape
    if k_total is not None:
        K = k_total
    r0 = 0
    if a_rows is not None:
        r0, M = a_rows
    N = n_cols if n_cols is not None else (b.shape[0] if mode == "nt" else b.shape[1])
    tm, tk, tn = _tile(M, tm), _tile(K, tk), _tile(N, tn)
    assert r0 % tm == 0
    i0 = r0 // tm
    if b_spec is None:
        b_spec = pl.BlockSpec((tn, tk), lambda i, j, k: (j, k)) if mode == "nt" else pl.BlockSpec((tk, tn), lambda i, j, k: (k, j))
    else:
        b_spec = b_spec(tk, tn)
    gm, gn, nk = M // tm, N // tn, K // tk
    if out_custom is not None:
        shape, blk, ix = out_custom(tm, tn)
        out_shapes, out_blocks, out_index = [shape] * len(out_dtypes), [blk] * len(out_dtypes), [ix] * len(out_dtypes)
    elif a_rows is not None:
        out_shapes = [(a.shape[0], N)] * len(out_dtypes)
        out_blocks = [(tm, tn)] * len(out_dtypes)
        out_index = [lambda i, j, k: (i + i0, j)] * len(out_dtypes)
    else:
        out_shapes = [(M, N)] * len(out_dtypes)
        out_blocks = [(tm, tn)] * len(out_dtypes)
        out_index = [lambda i, j, k: (i, j)] * len(out_dtypes)
    if a_spec is not None:
        a_spec = a_spec(tm, tk, r0)
    elif mode == "tn":
        a_spec = pl.BlockSpec((tk, tm), lambda i, j, k: (k, i + i0))
    else:
        a_spec = pl.BlockSpec((tm, tk), lambda i, j, k: (i + i0, k))
    hoist = mode == "tn" and nk == 1 and gn > 1
    dims = {"nn": NN, "nt": NT, "tn": TN}[mode]
    n_ex, n_out = len(extras), len(out_dtypes)
    out_into = [] if out_into is None else list(out_into)
    n_into = len(out_into)
    assert n_into in (0, n_out)

    def body(a_ref, b_ref, *rest):
        ex, outs = rest[:n_ex], rest[n_ex + n_into:n_ex + n_into + n_out]

        def finish(acc):
            res = epilogue(acc, *[e[...] for e in ex]) if epilogue is not None else (acc,)
            for o, r in zip(outs, res):
                o[...] = r.reshape(o.shape).astype(o.dtype)

        bb = b_ref[...]
        bb = bb.reshape(bb.shape[-2:])
        if hoist:
            at_ref = rest[-1]

            @pl.when(pl.program_id(1) == 0)
            def _():
                at_ref[...] = a_ref[...].T

            finish(lax.dot_general(at_ref[...], bb, NN, preferred_element_type=F32))
            return
        part = lax.dot_general(a_ref[...], bb, dims, preferred_element_type=F32)
        if nk == 1:
            finish(part)
        else:
            acc = rest[-1]
            k = pl.program_id(2)

            @pl.when(k == 0)
            def _():
                acc[...] = part

            @pl.when(k > 0)
            def _():
                acc[...] += part

            @pl.when(k == nk - 1)
            def _():
                finish(acc[...])

    scratch = [pltpu.VMEM((tm, tk), BF16)] if hoist else ([pltpu.VMEM((tm, tn), F32)] if nk > 1 else [])
    outs, lands = _pcall(
        body, grid=(gm, gn, nk),
        in_specs=([a_spec, b_spec] + [pl.BlockSpec((tm, tn), lambda i, j, k: (i, j)) for _ in extras]
                  + [pl.BlockSpec(memory_space=pl.ANY)] * n_into),
        out_specs=[pl.BlockSpec(blk, ix) for blk, ix in zip(out_blocks, out_index)],
        out_shape=[jax.ShapeDtypeStruct(s, d) for s, d in zip(out_shapes, out_dtypes)],
        scratch_shapes=scratch, semantics=("parallel", "arbitrary", "arbitrary"),
        operands=[a, b, *extras, *out_into], name=name, comms=comms,
        fill={2 + n_ex + o: o for o in range(n_into)})
    return (outs, lands) if comms else outs


TR = 256
QKV_CONV_COLS = 256


def _rms(x):
    return lax.rsqrt(jnp.mean(x * x, axis=-1, keepdims=True) + NORM_EPS)


def _rms_bwd(x, r, w, dy):
    u = dy * w
    dx = r * u - x * (r * r * r) * jnp.mean(x * u, axis=-1, keepdims=True)
    return dx, dy * x * r


def _row_call(body, ins, row_flags, outs, name, n_rows, comms=()):
    n_row = sum(row_flags) + sum(kind == "row" for _, _, kind in outs)
    tr = min(TR if n_row <= 5 else TR // 2, n_rows)
    in_specs = []
    for arr, is_row in zip(ins, row_flags):
        if is_row:
            in_specs.append(pl.BlockSpec((tr, arr.shape[1]), lambda i: (i, 0)))
        else:
            in_specs.append(pl.BlockSpec(arr.shape, lambda i: (0, 0)))
    out_specs, out_shape = [], []
    for shape, dtype, kind in outs:
        if kind == "row":
            out_specs.append(pl.BlockSpec((tr, shape[1]), lambda i: (i, 0)))
        else:
            out_specs.append(pl.BlockSpec(shape, lambda i: (0, 0)))
        out_shape.append(jax.ShapeDtypeStruct(shape, dtype))
    res, lands = _pcall(body, grid=(n_rows // tr,), in_specs=in_specs, out_specs=out_specs, out_shape=out_shape,
                        operands=list(ins), name=name, comms=comms)
    return (res, lands) if comms else res


def _acc_out(ref, val):
    @pl.when(pl.program_id(0) == 0)
    def _():
        ref[...] = val

    @pl.when(pl.program_id(0) > 0)
    def _():
        ref[...] += val


def _rms_fwd(x, g, comms):
    T, D = x.shape

    def body(x_ref, g_ref, o_ref):
        xv = x_ref[...]
        o_ref[...] = (xv * _rms(xv) * g_ref[...]).astype(BF16)

    res, lands = _row_call(body, [x, g], [True, False], [((T, D), BF16, "row")], "rms_fwd", T, comms=comms)
    return res[0], lands


def _post1(x, mix, g2, g3, comms=()):
    T, D = x.shape

    def body(x_ref, mix_ref, g2_ref, g3_ref, h_ref, xn2_ref):
        mv = mix_ref[...]
        h = x_ref[...] + mv * _rms(mv) * g2_ref[...]
        h_ref[...] = h
        xn2_ref[...] = (h * _rms(h) * g3_ref[...]).astype(BF16)

    return _row_call(body, [x, mix, g2, g3], [True, True, False, False],
                     [((T, D), F32, "row"), ((T, D), BF16, "row")], "post1", T, comms=comms)


def _post2_loss(h, ff, g4, target):
    T, D = h.shape

    def body(h_ref, ff_ref, g4_ref, t_ref, dff_ref, dy_ref, dg4_ref, loss_ref):
        fv = ff_ref[...]
        r = _rms(fv)
        err = h_ref[...] + fv * r * g4_ref[...] - t_ref[...]
        dy = err * (1.0 / D)
        dy_ref[...] = dy
        dff, dwt = _rms_bwd(fv, r, g4_ref[...], dy)
        dff_ref[...] = dff.astype(BF16)
        _acc_out(dg4_ref, jnp.sum(dwt, axis=0, keepdims=True))
        part = 0.5 * jnp.sum(jnp.mean(err * err, axis=-1, keepdims=True), axis=0, keepdims=True)
        _acc_out(loss_ref, jnp.broadcast_to(part, (1, LANE)))

    return _row_call(body, [h, ff, g4, target], [True, True, False, True],
                     [((T, D), BF16, "row"), ((T, D), F32, "row"), ((1, D), F32, "acc"), ((1, LANE), F32, "acc")],
                     "post2_loss", T)


def _mid_bwd(h, mix, dy, dxn2, g2, g3, comms=()):
    T, D = h.shape

    def body(h_ref, mix_ref, dy_ref, dxn2_ref, g2_ref, g3_ref, dmix_ref, dh_ref, dg2_ref, dg3_ref):
        hv = h_ref[...]
        d1, dw3 = _rms_bwd(hv, _rms(hv), g3_ref[...], dxn2_ref[...])
        dh = dy_ref[...] + d1
        dh_ref[...] = dh
        mv = mix_ref[...]
        dmix, dw2 = _rms_bwd(mv, _rms(mv), g2_ref[...], dh)
        dmix_ref[...] = dmix.astype(BF16)
        _acc_out(dg2_ref, jnp.sum(dw2, axis=0, keepdims=True))
        _acc_out(dg3_ref, jnp.sum(dw3, axis=0, keepdims=True))

    return _row_call(body, [h, mix, dy, dxn2, g2, g3], [True, True, True, True, False, False],
                     [((T, D), BF16, "row"), ((T, D), F32, "row"), ((1, D), F32, "acc"), ((1, D), F32, "acc")],
                     "mid_bwd", T, comms=comms)


def _pre_bwd(x, dh, dxn, g1, comms=()):
    T, D = x.shape

    def body(x_ref, dh_ref, dxn_ref, g1_ref, gx_ref, dg1_ref):
        xv = x_ref[...]
        d1, dw1 = _rms_bwd(xv, _rms(xv), g1_ref[...], dxn_ref[...])
        gx_ref[...] = dh_ref[...] + d1
        _acc_out(dg1_ref, jnp.sum(dw1, axis=0, keepdims=True))

    return _row_call(body, [x, dh, dxn, g1], [True, True, True, False],
                     [((T, D), F32, "row"), ((1, D), F32, "acc")], "pre_bwd", T, comms=comms)


def _shift_down(x, s):
    if s == 0:
        return x
    row = lax.broadcasted_iota(jnp.int32, x.shape, 0)
    return jnp.where(row >= s, pltpu.roll(x, s, axis=0), 0.0)


def _shift_up(x, s):
    if s == 0:
        return x
    n = x.shape[0]
    row = lax.broadcasted_iota(jnp.int32, x.shape, 0)
    return jnp.where(row < n - s, pltpu.roll(x, n - s, axis=0), 0.0)


def _conv(x, w):
    kw = w.shape[0]
    out = w[kw - 1:kw, :] * x
    for j in range(kw - 1):
        out = out + w[j:j + 1, :] * _shift_down(x, kw - 1 - j)
    return out


def _conv_bwd(x, w, dout):
    kw = w.shape[0]
    dx = w[kw - 1:kw, :] * dout
    dws = []
    for j in range(kw - 1):
        dx = dx + w[j:j + 1, :] * _shift_up(dout, kw - 1 - j)
        dws.append(jnp.sum(dout * _shift_down(x, kw - 1 - j), axis=0, keepdims=True))
    dws.append(jnp.sum(dout * x, axis=0, keepdims=True))
    return dx, jnp.concatenate(dws, axis=0)


def _qkvconv_fwd(proj, w, bsz, seq, gw, comms=()):
    cw = QKV_CONV_COLS
    nct = gw // cw
    kw = w.shape[0]

    def body(p_ref, w_ref, o_ref):
        cv = _conv(p_ref[...], w_ref[...])
        o_ref[...] = (cv * _sigmoid(cv)).reshape(o_ref.shape)

    res, lands = _pcall(
        body, grid=(3, bsz, nct),
        in_specs=[pl.BlockSpec((seq, cw), lambda p, b, c: (b, p * nct + c)),
                  pl.BlockSpec((kw, cw), lambda p, b, c: (0, p * nct + c))],
        out_specs=[pl.BlockSpec((1, seq, cw), lambda p, b, c: (p, b, c))],
        out_shape=[jax.ShapeDtypeStruct((3, bsz * seq, gw), F32)],
        semantics=("parallel", "parallel", "parallel"), operands=[proj, w], name="qkvconv_fwd", comms=comms)
    return res[0], lands


def _qkvconv_bwd(proj, w, dact, bsz, seq, gw, comms=()):
    cw = QKV_CONV_COLS
    nct = gw // cw
    kw = w.shape[0]

    def body(p_ref, w_ref, d_ref, dp_ref, dw_ref):
        pre = p_ref[...]
        wv = w_ref[...]
        cv = _conv(pre, wv)
        sg = _sigmoid(cv)
        dcv = d_ref[...].reshape(cv.shape) * (sg * (1.0 + cv * (1.0 - sg)))
        dpre, dw = _conv_bwd(pre, wv, dcv)
        dp_ref[...] = dpre.astype(BF16)
        b = pl.program_id(2)

        @pl.when(b == 0)
        def _():
            dw_ref[...] = dw

        @pl.when(b > 0)
        def _():
            dw_ref[...] += dw

    res, lands = _pcall(
        body, grid=(3, nct, bsz),
        in_specs=[pl.BlockSpec((seq, cw), lambda p, c, b: (b, p * nct + c)),
                  pl.BlockSpec((kw, cw), lambda p, c, b: (0, p * nct + c)),
                  pl.BlockSpec((1, seq, cw), lambda p, c, b: (p, b, c))],
        out_specs=[pl.BlockSpec((seq, cw), lambda p, c, b: (b, p * nct + c)),
                   pl.BlockSpec((kw, cw), lambda p, c, b: (0, p * nct + c))],
        out_shape=[jax.ShapeDtypeStruct((bsz * seq, 3 * gw), BF16), jax.ShapeDtypeStruct((kw, 3 * gw), F32)],
        semantics=("parallel", "parallel", "arbitrary"), operands=[proj, w, dact], name="qkvconv_bwd", comms=comms)
    return res, lands


def _sc_fwd(proj, w, bsz, seq, scw, col0):
    nct = scw // LANE
    c0 = col0 // LANE
    kw = w.shape[0]

    def body(b_ref, c_ref, h_ref, w_ref, o_ref):
        o_ref[...] = (b_ref[...] * _conv(c_ref[...] * h_ref[...], w_ref[...])).astype(BF16)

    return pl.pallas_call(
        body, grid=(bsz, nct),
        in_specs=[pl.BlockSpec((seq, LANE), lambda b, c: (b, c0 + c)),
                  pl.BlockSpec((seq, LANE), lambda b, c: (b, c0 + nct + c)),
                  pl.BlockSpec((seq, LANE), lambda b, c: (b, c0 + 2 * nct + c)),
                  pl.BlockSpec((kw, LANE), lambda b, c: (0, c))],
        out_specs=pl.BlockSpec((seq, LANE), lambda b, c: (b, c)),
        out_shape=jax.ShapeDtypeStruct((bsz * seq, scw), BF16),
        compiler_params=_params("parallel", "parallel"), name="sc_fwd")(proj, proj, proj, w)


def _sc_bwd(proj, w, dout, bsz, seq, scw, col0, dcol0, comms=()):
    nct = scw // LANE
    c0 = col0 // LANE
    d0 = dcol0 // LANE
    kw = w.shape[0]

    def body(b_ref, c_ref, h_ref, w_ref, d_ref, db_ref, dc_ref, dh_ref, dw_ref):
        cc, hh, wv, dv = c_ref[...], h_ref[...], w_ref[...], d_ref[...]
        m = cc * hh
        db_ref[...] = (dv * _conv(m, wv)).astype(BF16)
        dm, dw = _conv_bwd(m, wv, dv * b_ref[...])
        dc_ref[...] = (dm * hh).astype(BF16)
        dh_ref[...] = (dm * cc).astype(BF16)
        b = pl.program_id(1)

        @pl.when(b == 0)
        def _():
            dw_ref[...] = dw

        @pl.when(b > 0)
        def _():
            dw_ref[...] += dw

    res, lands = _pcall(
        body, grid=(nct, bsz),
        in_specs=[pl.BlockSpec((seq, LANE), lambda c, b: (b, c0 + c)),
                  pl.BlockSpec((seq, LANE), lambda c, b: (b, c0 + nct + c)),
                  pl.BlockSpec((seq, LANE), lambda c, b: (b, c0 + 2 * nct + c)),
                  pl.BlockSpec((kw, LANE), lambda c, b: (0, c)),
                  pl.BlockSpec((seq, LANE), lambda c, b: (b, d0 + c))],
        out_specs=[pl.BlockSpec((seq, LANE), lambda c, b: (b, c)),
                   pl.BlockSpec((seq, LANE), lambda c, b: (b, c)),
                   pl.BlockSpec((seq, LANE), lambda c, b: (b, c)),
                   pl.BlockSpec((kw, LANE), lambda c, b: (0, c))],
        out_shape=[jax.ShapeDtypeStruct((bsz * seq, scw), BF16)] * 3 + [jax.ShapeDtypeStruct((kw, scw), F32)],
        semantics=("parallel", "arbitrary"), operands=[proj, proj, proj, w, dout], name="sc_bwd", comms=comms)
    return res, lands


HEADS_PER_STEP = 16


def _colsel(tile, idx):
    lane = lax.broadcasted_iota(jnp.int32, tile.shape, 1)
    return jnp.sum(jnp.where(lane == idx, tile, 0.0), axis=1, keepdims=True)


def _rowsel(tile, idx):
    row = lax.broadcasted_iota(jnp.int32, tile.shape, 0)
    return jnp.sum(jnp.where(row == idx, tile, 0.0), axis=0, keepdims=True)


def _colput(col, idx, width=LANE):
    lane = lax.broadcasted_iota(jnp.int32, (col.shape[0], width), 1)
    return jnp.where(lane == idx, col, 0.0)


def _tri_masks(c):
    row = lax.broadcasted_iota(jnp.int32, (c, c), 0)
    col = lax.broadcasted_iota(jnp.int32, (c, c), 1)
    return row >= col, row > col, row == col


def _unit_lower_inverses(ms):
    c = ms[0].shape[0]
    _, _, eye = _tri_masks(c)
    ps = [-m for m in ms]
    ts = [jnp.where(eye, 1.0, 0.0) + p for p in ps]
    for _ in range(int(math.log2(c)) - 1):
        ps = [_mdot(p, p) for p in ps]
        ts = [t + _mdot(t, p) for t, p in zip(ts, ps)]
    return ts


def _gates(ab, alog, dtb):
    g = -jnp.exp(alog) * _softplus(ab + dtb)
    return g, _sigmoid(ab)


def _l2n(x):
    r = lax.rsqrt(jnp.sum(x * x, axis=-1, keepdims=True) + L2_EPS)
    return x * r, r


def _gdn_chunk_common(q, k, gc, gr, bc):
    c, dk = q.shape
    incl, strict, _ = _tri_masks(c)
    qh, rq = _l2n(q)
    kn, rk = _l2n(k)
    qn = qh * (dk ** -0.5)
    dm = jnp.where(incl, jnp.exp(jnp.where(incl, gc - gr, 0.0)), 0.0)
    kk = _bdot(kn, kn, NT)
    m = jnp.where(strict, bc * kk * dm, 0.0)
    pm = jnp.where(incl, _bdot(qn, kn, NT) * dm, 0.0)
    return qh, rq, kn, rk, qn, dm, kk, m, pm


def _gdn_fwd(qkv, proj, alog, dtb, gnw, bsz, seq, heads, z_col0, ab_col0, comms=()):
    c = CHUNK
    nch = seq // c
    hb = min(HEADS_PER_STEP, heads)
    ng = heads // hb
    hd = qkv.shape[2] // heads
    wb = hb * hd

    def body(qkv_ref, z_ref, ab_ref, alog_ref, dtb_ref, gnw_ref, o_ref, ssave_ref, tsave_ref, s_ref, gc_ref, gt_ref, be_ref):
        n, hg = pl.program_id(1), pl.program_id(2)

        @pl.when((n == 0) & (hg == 0))
        def _():
            s_ref[...] = jnp.zeros_like(s_ref)

        @pl.when(hg == 0)
        def _():
            g, beta = _gates(ab_ref[...], alog_ref[...], dtb_ref[...])
            incl, _, _ = _tri_masks(c)
            gcum = _hdot(jnp.where(incl, 1.0, 0.0), g)
            gc_ref[...] = gcum
            gt_ref[...] = gcum.T
            be_ref[...] = beta

        gc_t, gt_t, be_t, gnw_v = gc_ref[...], gt_ref[...], be_ref[...], gnw_ref[...]
        hs = range(hb)
        sls = [slice(hh * hd, (hh + 1) * hd) for hh in hs]
        states = [s_ref[hg * hb + hh] for hh in hs]
        gcs = [_colsel(gc_t, hg * hb + hh) for hh in hs]
        grs = [_rowsel(gt_t, hg * hb + hh) for hh in hs]
        bcs = [_colsel(be_t, heads + hg * hb + hh) for hh in hs]
        com = [_gdn_chunk_common(qkv_ref[0, :, sls[hh]], qkv_ref[1, :, sls[hh]], gcs[hh], grs[hh], bcs[hh]) for hh in hs]
        kns, qns, pms = [cm[2] for cm in com], [cm[4] for cm in com], [cm[8] for cm in com]
        tms = _unit_lower_inverses([cm[7] for cm in com])
        gams = [jnp.exp(gc) for gc in gcs]
        glasts = [gc[c - 1:c, :] for gc in gcs]
        kss = [_bdot(kns[hh], states[hh]) for hh in hs]
        qss = [_bdot(qns[hh], states[hh]) for hh in hs]
        vns = [_bdot(tms[hh], bcs[hh] * (qkv_ref[2, :, sls[hh]] - gams[hh] * kss[hh])) for hh in hs]
        os_ = [gams[hh] * qss[hh] + _bdot(pms[hh], vns[hh]) for hh in hs]
        snews = [states[hh] * jnp.exp(glasts[hh]) + _bdot(kns[hh] * jnp.exp(glasts[hh] - gcs[hh]), vns[hh], TN) for hh in hs]
        for hh in hs:
            o = os_[hh]
            on = o * lax.rsqrt(jnp.mean(o * o, axis=-1, keepdims=True) + NORM_EPS) * gnw_v
            zz = z_ref[:, sls[hh]]
            ssave_ref[0, 0, hh] = states[hh]
            tsave_ref[0, 0, hh] = tms[hh]
            s_ref[hg * hb + hh] = snews[hh]
            o_ref[:, sls[hh]] = (on * (zz * _sigmoid(zz))).astype(BF16)

    row = lambda b, n, g: b * nch + n
    return _pcall(
        body, grid=(bsz, nch, ng),
        in_specs=[pl.BlockSpec((3, c, wb), lambda b, n, g: (0, row(b, n, g), g)),
                  pl.BlockSpec((c, wb), lambda b, n, g: (row(b, n, g), z_col0 // wb + g)),
                  pl.BlockSpec((c, LANE), lambda b, n, g: (row(b, n, g), ab_col0 // LANE)),
                  pl.BlockSpec((1, LANE), lambda b, n, g: (0, 0)),
                  pl.BlockSpec((1, LANE), lambda b, n, g: (0, 0)),
                  pl.BlockSpec((1, hd), lambda b, n, g: (0, 0))],
        out_specs=[pl.BlockSpec((c, wb), lambda b, n, g: (row(b, n, g), g)),
                   pl.BlockSpec((1, 1, hb, hd, hd), lambda b, n, g: (b, n, g, 0, 0)),
                   pl.BlockSpec((1, 1, hb, c, c), lambda b, n, g: (b, n, g, 0, 0))],
        out_shape=[jax.ShapeDtypeStruct((bsz * seq, heads * hd), BF16),
                   jax.ShapeDtypeStruct((bsz, nch, heads, hd, hd), F32),
                   jax.ShapeDtypeStruct((bsz, nch, heads, c, c), F32)],
        scratch_shapes=[pltpu.VMEM((heads, hd, hd), F32), pltpu.VMEM((c, LANE), F32), pltpu.VMEM((LANE, c), F32),
                        pltpu.VMEM((c, LANE), F32)],
        semantics=("parallel", "arbitrary", "arbitrary"), operands=[qkv, proj, proj, alog, dtb, gnw], name="gdn_fwd",
        comms=comms)


def _gdn_bwd(qkv, proj, alog, dtb, gnw, ssave, tsave, dout, bsz, seq, heads, z_col0, ab_col0, comms=()):
    c = CHUNK
    nch = seq // c
    hb = min(HEADS_PER_STEP, heads)
    ng = heads // hb
    hd = qkv.shape[2] // heads
    wb = hb * hd

    def body(qkv_ref, z_ref, ab_ref, alog_ref, dtb_ref, gnw_ref, ssave_ref, tsave_ref, do_ref,
             dact_ref, dz_ref, dab_ref, dalog_ref, ddtb_ref, dgnw_ref,
             ds_ref, gc_ref, gt_ref, be_ref, dgacc_ref, dbacc_ref):
        n, hg = pl.program_id(1), pl.program_id(2)
        incl, strict, _ = _tri_masks(c)

        @pl.when((n == 0) & (hg == 0))
        def _():
            ds_ref[...] = jnp.zeros_like(ds_ref)
            dalog_ref[...] = jnp.zeros_like(dalog_ref)
            ddtb_ref[...] = jnp.zeros_like(ddtb_ref)
            dgnw_ref[...] = jnp.zeros_like(dgnw_ref)

        @pl.when(hg == 0)
        def _():
            g, beta = _gates(ab_ref[...], alog_ref[...], dtb_ref[...])
            gcum = _hdot(jnp.where(incl, 1.0, 0.0), g)
            gc_ref[...] = gcum
            gt_ref[...] = gcum.T
            be_ref[...] = beta
            dgacc_ref[...] = jnp.zeros_like(dgacc_ref)
            dbacc_ref[...] = jnp.zeros_like(dbacc_ref)

        gc_t, gt_t, be_t, gnw_v = gc_ref[...], gt_ref[...], be_ref[...], gnw_ref[...]
        hs = range(hb)

        def each(f):
            return [f(hh) for hh in hs]

        rsum = lambda a: jnp.sum(a, axis=-1, keepdims=True)
        sls = each(lambda i: slice(i * hd, (i + 1) * hd))
        ds_in = each(lambda i: ds_ref[hg * hb + i])
        gc = each(lambda i: _colsel(gc_t, hg * hb + i))
        gr = each(lambda i: _rowsel(gt_t, hg * hb + i))
        bc = each(lambda i: _colsel(be_t, heads + hg * hb + i))
        com = each(lambda i: _gdn_chunk_common(qkv_ref[0, :, sls[i]], qkv_ref[1, :, sls[i]], gc[i], gr[i], bc[i]))
        qh, rq, kn, rk, qn, dm, kk, m, pm = [[cm[j] for cm in com] for j in range(9)]
        tm = each(lambda i: tsave_ref[0, 0, i])
        s = each(lambda i: ssave_ref[0, 0, i])
        gam = each(lambda i: jnp.exp(gc[i]))
        glast = each(lambda i: gc[i][c - 1:c, :])
        gl = each(lambda i: jnp.exp(glast[i]))
        ratio = each(lambda i: jnp.exp(glast[i] - gc[i]))
        ks = each(lambda i: _bdot(kn[i], s[i]))
        qs = each(lambda i: _bdot(qn[i], s[i]))
        r = each(lambda i: qkv_ref[2, :, sls[i]] - gam[i] * ks[i])
        vn = each(lambda i: _bdot(tm[i], bc[i] * r[i]))
        o = each(lambda i: gam[i] * qs[i] + _bdot(pm[i], vn[i]))
        ro = each(lambda i: lax.rsqrt(jnp.mean(o[i] * o[i], axis=-1, keepdims=True) + NORM_EPS))
        zz = each(lambda i: z_ref[:, sls[i]])
        sz = each(lambda i: _sigmoid(zz[i]))
        dd = each(lambda i: do_ref[:, sls[i]])
        don = each(lambda i: dd[i] * (zz[i] * sz[i]))
        dz_h = each(lambda i: (dd[i] * (o[i] * ro[i] * gnw_v) * (sz[i] * (1.0 + zz[i] * (1.0 - sz[i])))).astype(BF16))
        dgnw = sum(each(lambda i: jnp.sum(don[i] * o[i] * ro[i], axis=0, keepdims=True)))
        uu = each(lambda i: don[i] * gnw_v)
        d_o = each(lambda i: ro[i] * uu[i] - o[i] * (ro[i] * ro[i] * ro[i]) * jnp.mean(o[i] * uu[i], axis=-1, keepdims=True))
        dqs = each(lambda i: gam[i] * d_o[i])
        dq = each(lambda i: _bdot(dqs[i], s[i], NT))
        ds_new = each(lambda i: _bdot(qn[i], dqs[i], TN))
        dp = each(lambda i: jnp.where(incl, _bdot(d_o[i], vn[i], NT), 0.0))
        dvn = each(lambda i: _bdot(pm[i], d_o[i], TN))
        dgam = each(lambda i: rsum(d_o[i] * qs[i]))
        dkd = each(lambda i: _bdot(vn[i], ds_in[i], NT))
        dvn = each(lambda i: dvn[i] + _bdot(kn[i] * ratio[i], ds_in[i]))
        ds_new = each(lambda i: ds_new[i] + gl[i] * ds_in[i])
        dgl = each(lambda i: jnp.sum(jnp.sum(ds_in[i] * s[i], axis=1, keepdims=True), axis=0, keepdims=True))
        dratio = each(lambda i: rsum(dkd[i] * kn[i]))
        dpd = each(lambda i: dp[i] * dm[i])
        dq = each(lambda i: dq[i] + _bdot(dpd[i], kn[i]))
        dk = each(lambda i: ratio[i] * dkd[i] + _bdot(dpd[i], qn[i], TN))
        dx = each(lambda i: _bdot(tm[i], dvn[i], TN))
        dr = each(lambda i: bc[i] * dx[i])
        gdr = each(lambda i: gam[i] * dr[i])
        dk = each(lambda i: dk[i] - _bdot(gdr[i], s[i], NT))
        ds_new = each(lambda i: ds_new[i] - _bdot(kn[i], gdr[i], TN))
        dmm = each(lambda i: jnp.where(strict, -_bdot(dx[i], vn[i], NT), 0.0))
        ee = each(lambda i: dmm[i] * dm[i])
        be_e = each(lambda i: bc[i] * ee[i])
        dk = each(lambda i: dk[i] + _bdot(be_e[i], kn[i]) + _bdot(be_e[i], kn[i], TN))
        dbeta = each(lambda i: rsum(dx[i] * r[i]) + rsum(ee[i] * kk[i]))
        dgam = each(lambda i: dgam[i] - rsum(dr[i] * ks[i]))
        ff = each(lambda i: dp[i] * pm[i] + dmm[i] * m[i])
        rowi = lax.broadcasted_iota(jnp.int32, (c, 1), 0)
        dgc = each(lambda i: rsum(ff[i]) - rsum(ff[i].T) + dgam[i] * gam[i] - dratio[i] * ratio[i]
                   + jnp.where(rowi == c - 1, jnp.sum(dratio[i] * ratio[i], axis=0, keepdims=True) + dgl[i] * gl[i], 0.0))
        dg_tile = sum(each(lambda i: _colput(dgc[i], hg * hb + i)))
        db_tile = sum(each(lambda i: _colput(dbeta[i], heads + hg * hb + i)))
        for i in hs:
            dqh = dq[i] * (hd ** -0.5)
            ds_ref[hg * hb + i] = ds_new[i]
            dz_ref[:, sls[i]] = dz_h[i]
            dact_ref[0, :, sls[i]] = rq[i] * (dqh - qh[i] * rsum(qh[i] * dqh))
            dact_ref[1, :, sls[i]] = rk[i] * (dk[i] - kn[i] * rsum(kn[i] * dk[i]))
            dact_ref[2, :, sls[i]] = dr[i]
        dgacc_ref[...] += dg_tile
        dbacc_ref[...] += db_tile
        dgnw_ref[0] += dgnw

        @pl.when(hg == ng - 1)
        def _():
            ab = ab_ref[...]
            ea = jnp.exp(alog_ref[...])
            g = -ea * _softplus(ab + dtb_ref[...])
            beta = be_ref[...]
            dg = _hdot(jnp.where(incl, 1.0, 0.0), dgacc_ref[...], TN)
            lane = lax.broadcasted_iota(jnp.int32, ab.shape, 1)
            da = jnp.where(lane < heads, dg * (-ea) * _sigmoid(ab + dtb_ref[...]), 0.0)
            db = dbacc_ref[...] * beta * (1.0 - beta)
            dab_ref[...] = (da + db).astype(BF16)
            dalog_ref[0] += jnp.sum(jnp.where(lane < heads, dg * g, 0.0), axis=0, keepdims=True)
            ddtb_ref[0] += jnp.sum(da, axis=0, keepdims=True)

    row = lambda b, n, g: b * nch + (nch - 1 - n)
    rev = lambda n: nch - 1 - n
    return _pcall(
        body, grid=(bsz, nch, ng),
        in_specs=[pl.BlockSpec((3, c, wb), lambda b, n, g: (0, row(b, n, g), g)),
                  pl.BlockSpec((c, wb), lambda b, n, g: (row(b, n, g), z_col0 // wb + g)),
                  pl.BlockSpec((c, LANE), lambda b, n, g: (row(b, n, g), ab_col0 // LANE)),
                  pl.BlockSpec((1, LANE), lambda b, n, g: (0, 0)),
                  pl.BlockSpec((1, LANE), lambda b, n, g: (0, 0)),
                  pl.BlockSpec((1, hd), lambda b, n, g: (0, 0)),
                  pl.BlockSpec((1, 1, hb, hd, hd), lambda b, n, g: (b, rev(n), g, 0, 0)),
                  pl.BlockSpec((1, 1, hb, c, c), lambda b, n, g: (b, rev(n), g, 0, 0)),
                  pl.BlockSpec((c, wb), lambda b, n, g: (row(b, n, g), g))],
        out_specs=[pl.BlockSpec((3, c, wb), lambda b, n, g: (0, row(b, n, g), g)),
                   pl.BlockSpec((c, wb), lambda b, n, g: (row(b, n, g), g)),
                   pl.BlockSpec((c, LANE), lambda b, n, g: (row(b, n, g), 0)),
                   pl.BlockSpec((1, 1, LANE), lambda b, n, g: (b, 0, 0)),
                   pl.BlockSpec((1, 1, LANE), lambda b, n, g: (b, 0, 0)),
                   pl.BlockSpec((1, 1, hd), lambda b, n, g: (b, 0, 0))],
        out_shape=[jax.ShapeDtypeStruct((3, bsz * seq, heads * hd), F32),
                   jax.ShapeDtypeStruct((bsz * seq, heads * hd), BF16),
                   jax.ShapeDtypeStruct((bsz * seq, LANE), BF16),
                   jax.ShapeDtypeStruct((bsz, 1, LANE), F32),
                   jax.ShapeDtypeStruct((bsz, 1, LANE), F32),
                   jax.ShapeDtypeStruct((bsz, 1, hd), F32)],
        scratch_shapes=[pltpu.VMEM((heads, hd, hd), F32), pltpu.VMEM((c, LANE), F32), pltpu.VMEM((LANE, c), F32),
                        pltpu.VMEM((c, LANE), F32), pltpu.VMEM((c, LANE), F32), pltpu.VMEM((c, LANE), F32)],
        semantics=("parallel", "arbitrary", "arbitrary"),
        operands=[qkv, proj, proj, alog, dtb, gnw, ssave, tsave, dout], name="gdn_bwd", comms=comms)


ELEMWISE_BLOCK_ELEMS = 256 * 1024


def _rows_tile(rows, cols):
    want = max(16, ELEMWISE_BLOCK_ELEMS // cols)
    if rows <= want:
        return rows
    t = (want // 16) * 16
    while t > 16 and rows % t:
        t -= 16
    return t if rows % t == 0 else rows


def _piece_specs(pieces, tr, cols):
    specs, leads = [], []
    for p, (arr, lead) in enumerate(pieces):
        if arr.ndim == 3:
            specs.append(pl.BlockSpec((1, tr, cols), functools.partial(lambda i, idx, p: (idx[p], i, 0), p=p)))
        else:
            specs.append(pl.BlockSpec((tr, cols), lambda i, idx: (i, 0)))
        leads.append(jnp.asarray(0 if lead is None else lead, jnp.int32))
    return jnp.stack(leads), specs


def _sum_pieces(refs):
    total = None
    for r in refs:
        v = r[...].astype(F32)
        v = v.reshape(v.shape[-2:])
        total = v if total is None else total + v
    return total


def _adamw(w, m, v, pieces, name, comms=()):
    rows, cols = w.shape
    tr = _rows_tile(rows, cols)
    leads, pspecs = _piece_specs(pieces, tr, cols)
    npc = len(pieces)
    c1 = 1.0 - ADAM_B1 ** ADAM_STEP
    c2 = 1.0 - ADAM_B2 ** ADAM_STEP

    def body(idx_ref, w_ref, m_ref, v_ref, *rest):
        g = _sum_pieces(rest[:npc])
        g_ref, d_ref, nm_ref, nv_ref = rest[npc:]
        nm = ADAM_B1 * m_ref[...] + (1.0 - ADAM_B1) * g
        nv = ADAM_B2 * v_ref[...] + (1.0 - ADAM_B2) * (g * g)
        g_ref[...] = g
        nm_ref[...] = nm
        nv_ref[...] = nv
        d_ref[...] = -ADAM_LR * ((nm / c1) / (jnp.sqrt(nv / c2) + ADAM_EPS) + ADAM_WD * w_ref[...])

    wspec = pl.BlockSpec((tr, cols), lambda i, idx: (i, 0))
    res, lands = _pcall(body, grid=(rows // tr,), in_specs=[wspec] * 3 + pspecs, out_specs=[wspec] * 4,
                        out_shape=[jax.ShapeDtypeStruct((rows, cols), F32)] * 4, semantics=("parallel",),
                        prefetch=[leads], operands=[w, m, v, *[p for p, _ in pieces]], name=name, comms=comms)
    return (res, lands) if comms else res


def _sum_to(pieces, out_dtype, name):
    arr0 = pieces[0][0]
    rows, cols = arr0.shape[-2:]
    tr = _rows_tile(rows, cols)
    leads, pspecs = _piece_specs(pieces, tr, cols)

    def body(idx_ref, *rest):
        rest[-1][...] = _sum_pieces(rest[:-1]).astype(out_dtype)

    return pl.pallas_call(
        body,
        grid_spec=pltpu.PrefetchScalarGridSpec(num_scalar_prefetch=1, grid=(rows // tr,), in_specs=pspecs,
                                               out_specs=pl.BlockSpec((tr, cols), lambda i, idx: (i, 0))),
        out_shape=jax.ShapeDtypeStruct((rows, cols), out_dtype),
        compiler_params=_params("parallel"), name=name)(leads, *[p for p, _ in pieces])


def _pair_add(a, recv, place, name, comms=()):
    _, rows, cols = a.shape
    tr = _rows_tile(rows, cols)
    x, y, c = place
    idx = jnp.stack([2 * (1 - x) + y, 2 * x + (1 - y), 2 * (1 - x) + (1 - y), c]).astype(jnp.int32)

    def body(p_ref, a_ref, r_ref, o_ref):
        o_ref[...] = (a_ref[...].astype(F32) + r_ref[...].astype(F32)).astype(BF16)

    res, lands = _pcall(
        body, grid=(3, rows // tr),
        in_specs=[pl.BlockSpec((1, tr, cols), lambda j, i, p: (2 * p[j] + p[3], i, 0)),
                  pl.BlockSpec((1, tr, cols), lambda j, i, p: (p[j], i, 0))],
        out_specs=[pl.BlockSpec((1, tr, cols), lambda j, i, p: (j, i, 0))],
        out_shape=[jax.ShapeDtypeStruct((3, rows, cols), BF16)], semantics=("parallel", "parallel"),
        prefetch=[idx], operands=[a, recv], name=name, comms=comms)
    return (res[0], lands) if comms else res[0]


def _to_frame(w, offs, fw, name):
    rows, n = w.shape
    tr = _tile(rows, 256)

    def body(off_ref, w_ref, o_ref, pad_ref):
        pad_ref[...] = jnp.zeros_like(pad_ref)
        pad_ref[:, 0:n] = w_ref[...]
        y = pad_ref[...]
        off1, len1, off2 = off_ref[0], off_ref[1], off_ref[2]
        col = lax.broadcasted_iota(jnp.int32, y.shape, 1)
        o_ref[0] = jnp.where(col < off1 + len1, pltpu.roll(y, off1, axis=1),
                             jnp.where(col >= off2 + len1, pltpu.roll(y, off2, axis=1), 0.0)).astype(BF16)

    res, _ = _pcall(body, grid=(rows // tr,), in_specs=[pl.BlockSpec((tr, n), lambda i, o: (i, 0))],
                    out_specs=[pl.BlockSpec((1, tr, fw), lambda i, o: (o[3], i, 0))],
                    out_shape=[jax.ShapeDtypeStruct((N_DEV, rows, fw), BF16)], scratch_shapes=[pltpu.VMEM((tr, fw), F32)],
                    semantics=("parallel",), prefetch=[offs], operands=[w], name=name)
    return res[0]


def _in_proj_core(xn, frames, ids, lay, into, name, comms=()):
    t, d = xn.shape
    n = ids.shape[0]
    tm = _tile(t, 1024)
    nb = max(b for b in range(1, lay.nc + 1) if lay.nc % b == 0 and b * MXU <= 768)
    tn, nj = nb * MXU, lay.nc // nb

    def body(ids_ref, a_ref, b_ref, *rest):
        o_ref = rest[-1]
        o_ref[...] = jnp.dot(a_ref[...], b_ref[0], preferred_element_type=F32)

    col = lambda s, j: pl.multiple_of(lay.frame_block(s) * MXU + lay.c0 * MXU + j * tn, MXU)
    n_into = 0 if into is None else 1
    res, lands = _pcall(
        body, grid=(t // tm, n, nj),
        in_specs=[pl.BlockSpec((tm, d), lambda i, f, j, ids: (i, 0)),
                  pl.BlockSpec((pl.Element(1), pl.Element(d), pl.Element(tn)),
                               lambda i, f, j, ids: (f, 0, pl.multiple_of(lay.c0 * MXU + j * tn, MXU)))]
        + [pl.BlockSpec(memory_space=pl.ANY)] * n_into,
        out_specs=[pl.BlockSpec((pl.Element(tm), pl.Element(tn)), lambda i, f, j, ids: (i * tm, col(ids[f], j)))],
        out_shape=[jax.ShapeDtypeStruct((t, lay.wp), F32)], semantics=("parallel", "arbitrary", "arbitrary"),
        prefetch=[ids], operands=[xn, frames] + ([into] if n_into else []), name=name, comms=comms,
        fill={2: 0} if n_into else None)
    return res[0], lands


def _in_proj_rest(xn, frames, table, wp, into, name, comms=()):
    t, d = xn.shape
    tm = _tile(t, 1024)
    n = table.shape[1]

    def body(t_ref, a_ref, b1_ref, b2_ref, *rest):
        j = pl.program_id(1)
        b = b1_ref[0]
        b = jnp.where(t_ref[5, j] > 0, b + b2_ref[0], b)
        rest[-1][...] = jnp.dot(a_ref[...], b, preferred_element_type=F32)

    n_into = 0 if into is None else 1
    res, lands = _pcall(
        body, grid=(t // tm, n),
        in_specs=[pl.BlockSpec((tm, d), lambda i, j, tb: (i, 0)),
                  pl.BlockSpec((1, d, MXU), lambda i, j, tb: (tb[1, j], 0, tb[2, j])),
                  pl.BlockSpec((1, d, MXU), lambda i, j, tb: (tb[3, j], 0, tb[4, j]))]
        + [pl.BlockSpec(memory_space=pl.ANY)] * n_into,
        out_specs=[pl.BlockSpec((tm, MXU), lambda i, j, tb: (i, tb[0, j]))],
        out_shape=[jax.ShapeDtypeStruct((t, wp), F32)], semantics=("parallel", "arbitrary"),
        prefetch=[table], operands=[xn, frames, frames] + ([into] if n_into else []), name=name, comms=comms,
        fill={3: 0} if n_into else None)
    return res[0], lands


def _place():
    x, y, c = lax.axis_index("x"), lax.axis_index("y"), lax.axis_index("c")
    chips = [(1 - x, y), (x, 1 - y), (1 - x, 1 - y)]
    return x, y, c, chips


def _allreduce_small(buf, name):
    rows = buf.shape[0]

    def body(x_ref, o_ref, g_ref, send_sems, recv_sems):
        x, y, c, chips = _place()
        me, sibling = (x, y, c), (x, y, 1 - c)

        def copy(k, block, to, src=None):
            dst = g_ref.at[4 * block[0] + 2 * block[1] + block[2]]
            return pltpu.make_async_remote_copy(src_ref=dst if src is None else src, dst_ref=dst,
                                                send_sem=send_sems.at[k], recv_sem=recv_sems.at[k],
                                                device_id=to, device_id_type=MESH)

        first = [copy(0, me, sibling, src=x_ref)]
        first += [copy(1 + j, me, (*chip, c), src=x_ref) for j, chip in enumerate(chips)]
        for cp in first:
            cp.start()
        passed = [copy(4 + j, (*chip, c), sibling) for j, chip in enumerate(chips)]
        for j, chip in enumerate(chips):
            copy(1 + j, (*chip, c), me).wait_recv()
            passed[j].start()
        copy(0, sibling, me).wait_recv()
        for j, chip in enumerate(chips):
            copy(4 + j, (*chip, 1 - c), me).wait_recv()
        for cp in first + passed:
            cp.wait_send()
        g_ref[4 * x + 2 * y + c] = x_ref[...]
        total = g_ref[0]
        for s in range(1, N_DEV):
            total = total + g_ref[s]
        o_ref[...] = total

    vm = pl.BlockSpec(memory_space=pltpu.VMEM)
    return pl.pallas_call(
        body, in_specs=[vm], out_specs=vm, out_shape=jax.ShapeDtypeStruct((rows, LANE), F32),
        scratch_shapes=[pltpu.VMEM((N_DEV, rows, LANE), F32), pltpu.SemaphoreType.DMA((7,)), pltpu.SemaphoreType.DMA((7,))],
        name=name)(buf)


def _rows(ref, rows):
    return ref if rows is None else ref.at[pl.ds(rows[0], rows[1] - rows[0])]


AG_ALL = ("here", "sibling", 0, 1, 2)


def _ag(shard=None, into=None, to=(), forward=(), rows=None):
    def plan(srcs, lands):
        x, y, c, chips = _place()
        buf = lands[0]
        out = []
        if to:
            dst = _rows(buf.at[4 * x + 2 * y + c], rows)
            src = dst if shard is None else _rows(srcs[0], rows)
            for who in to:
                if who == "here":
                    out.append(("local", src, dst, None))
                elif who == "sibling":
                    out.append(("remote", src, dst, (x, y, 1 - c)))
                else:
                    out.append(("remote", src, dst, (*chips[who], c)))
        for j in forward:
            r = _rows(buf.at[4 * chips[j][0] + 2 * chips[j][1] + c], rows)
            out.append(("remote", r, r, (x, y, 1 - c)))
        return out

    srcs = ([shard] if to and shard is not None else []) + ([into] if into is not None else [])
    land = jax.ShapeDtypeStruct(into.shape, into.dtype) if into is not None else jax.ShapeDtypeStruct((N_DEV,) + shard.shape, shard.dtype)
    return _Comm(srcs, [land], plan, len(to) + len(forward), alias={len(srcs) - 1: 0} if into is not None else None)


def _ag_first(shard, rows=None, into=None, to=AG_ALL):
    return _ag(shard=shard, into=into, to=to, rows=rows)


def _ag_second(g, rows=None, of=(0, 1, 2)):
    return _ag(into=g, forward=of, rows=rows)


def _rs_first(grad):
    def plan(srcs, lands):
        x, y, c, _ = _place()
        (a,), (land,) = srcs, lands
        return [("remote", a.at[2 * j + (1 - c)], land.at[j], (x, y, 1 - c)) for j in range(4)]

    return _Comm([grad], [jax.ShapeDtypeStruct((4,) + grad.shape[1:], grad.dtype)], plan, 4)


def _rs_second(pair, rows=None, into=None):
    def plan(srcs, lands):
        x, y, c, chips = _place()
        return [("remote", _rows(srcs[0].at[j], rows), _rows(lands[0].at[j], rows), (cx, cy, c))
                for j, (cx, cy) in enumerate(chips)]

    land = jax.ShapeDtypeStruct((3,) + pair.shape[1:], pair.dtype)
    if into is None:
        return _Comm([pair], [land], plan, 3)
    return _Comm([pair, into], [land], plan, 3, alias={1: 0})


class _InLayout:
    def __init__(self, n_in, gw, heads, scw):
        self.n_in, self.split = n_in, 4 * gw + 2 * heads
        self.gap = LANE - 2 * heads
        self.ab_col, self.sc_col = 4 * gw, 4 * gw + LANE
        self.used = 4 * gw + LANE + 3 * scw
        p0 = [s * n_in + (self.gap if s * n_in >= self.split else 0) for s in range(N_DEV)]
        self.fstart = [(p // MXU) * MXU for p in p0]
        need = []
        for s in range(N_DEV):
            straddle = s * n_in < self.split < (s + 1) * n_in
            need.append(p0[s] - self.fstart[s] + n_in + (self.gap if straddle else 0))
        self.fw = -(-max(need) // MXU) * MXU
        self.wp = max(f + self.fw for f in self.fstart)
        assert self.wp >= self.used and self.wp % MXU == 0
        nfb = self.fw // MXU
        rows = []
        for jb in range(self.wp // MXU):
            src = [(s, jb - self.fstart[s] // MXU) for s in range(N_DEV) if 0 <= jb - self.fstart[s] // MXU < nfb]
            assert 1 <= len(src) <= 2, (jb, src)
            (s1, b1), (s2, b2) = src[0], src[-1]
            rows.append((s1, b1, s2, b2, int(len(src) == 2)))
        self.table = np.asarray(rows, np.int32).T.copy()
        single = [all(rows[self.fstart[s] // MXU + b][4] == 0 for s in range(N_DEV)) for b in range(nfb)]
        runs, b = [], 0
        while b < nfb:
            if single[b]:
                e = b
                while e < nfb and single[e]:
                    e += 1
                runs.append((e - b, b))
                b = e
            else:
                b += 1
        self.nc, self.c0 = max(runs) if runs else (0, 0)
        in_core = {self.fstart[s] // MXU + b for s in range(N_DEV) for b in range(self.c0, self.c0 + self.nc)}
        self.rest = np.asarray([(jb,) + rows[jb] for jb in range(self.wp // MXU) if jb not in in_core], np.int32).T.copy()

    def frame_block(self, s):
        p = s * self.n_in
        return (p + jnp.where(p >= self.split, self.gap, 0)) // MXU

    def offsets(self, s):
        p = s * self.n_in
        after = p >= self.split
        off1 = p + jnp.where(after, self.gap, 0) - self.frame_block(s) * MXU
        len1 = jnp.where(after, self.n_in, jnp.clip(self.split - p, 0, self.n_in))
        off2 = off1 + jnp.where(len1 < self.n_in, self.gap, 0)
        return off1, len1, off2

    def to_frame(self, w, s):
        return _to_frame(w, jnp.stack(self.offsets(s) + (s,)).astype(jnp.int32), self.fw, "w_in_frame")

    def from_frame(self, f, s):
        off1, len1, off2 = self.offsets(s)
        a = lax.dynamic_slice(f, (0, off1), (f.shape[0], self.n_in))
        b = lax.dynamic_slice(f, (0, off2), (f.shape[0], self.n_in))
        col = lax.broadcasted_iota(jnp.int32, (1, self.n_in), 1)
        return jnp.where(col < len1, a, b)


def _pack_rows(parts):
    rows = []
    for p in parts:
        flat = p.reshape(-1)
        pad = (-flat.shape[0]) % LANE
        rows.append(jnp.pad(flat, (0, pad)).reshape(-1, LANE))
    buf = jnp.concatenate(rows, axis=0)
    return jnp.pad(buf, ((0, (-buf.shape[0]) % 8), (0, 0)))


def _unpack_rows(buf, shapes):
    out, r = [], 0
    for shp in shapes:
        size = int(np.prod(shp))
        nr = -(-size // LANE)
        out.append(buf[r:r + nr].reshape(-1)[:size].reshape(shp))
        r += nr
    return out


def _pad_lanes(v):
    return jnp.pad(v, ((0, 0), (0, LANE - v.shape[1])))


def kernel(x, norm_mix_pre, w_in, conv_qkv_w, a_log, dt_bias, gdn_norm_w, conv_sc_w, w_out, norm_mix_post, norm_mlp_pre, w_up, w_down, norm_mlp_post, loss_target, m_norm_mix_pre, m_w_in, m_conv_qkv_w, m_a_log, m_dt_bias, m_gdn_norm_w, m_conv_sc_w, m_w_out, m_norm_mix_post, m_norm_mlp_pre, m_w_up, m_w_down, m_norm_mlp_post, v_norm_mix_pre, v_w_in, v_conv_qkv_w, v_a_log, v_dt_bias, v_gdn_norm_w, v_conv_sc_w, v_w_out, v_norm_mix_post, v_norm_mlp_pre, v_w_up, v_w_down, v_norm_mlp_post):
    bsz, seq, d = x.shape
    t = bsz * seq
    heads, hd = a_log.shape[-1], gdn_norm_w.shape[-1]
    gw = heads * hd
    scw = conv_sc_w.shape[-1] * N_DEV
    dff_w = w_up.shape[-1] * N_DEV
    lay = _InLayout(w_in.shape[-1], gw, heads, scw)
    mx, my, mc = lax.axis_index("x"), lax.axis_index("y"), lax.axis_index("c")
    me = 4 * mx + 2 * my + mc
    chip = 2 * mx + my

    x2 = x.reshape(t, d)
    tgt = loss_target.reshape(t, d)
    g1, g2, g3, g4 = norm_mix_pre, norm_mix_post, norm_mlp_pre, norm_mlp_post

    g_in = lay.to_frame(w_in[0], me)
    w_out_b, w_up_b, w_down_b = w_out[0].astype(BF16), w_up[0].astype(BF16), w_down[0].astype(BF16)
    up_cols = dff_w // N_DEV
    qu, qd = d // 4, up_cols // 4
    kq, ks = conv_qkv_w.shape[1], conv_sc_w.shape[1]
    cq_n, cs_n = conv_qkv_w.shape[-1], conv_sc_w.shape[-1]
    cq_full = lax.dynamic_update_slice(jnp.zeros((kq, 3 * gw), F32), conv_qkv_w[0], (0, me * cq_n))
    cs_full = lax.dynamic_update_slice(jnp.zeros((ks, scw), F32), conv_sc_w[0], (0, me * cs_n))
    conv_q, conv_s = _unpack_rows(_allreduce_small(_pack_rows([cq_full, cs_full]), "allgather_conv"),
                                  [(kq, 3 * gw), (ks, scw)])
    alog_t, dtb_t = _pad_lanes(a_log), _pad_lanes(dt_bias)

    xn, (g_in,) = _rms_fwd(x2, g1, comms=[_ag(into=g_in, to=("sibling",))])

    def arrived(k):
        return jnp.stack([lax.dynamic_index_in_dim(g_in, s, 0, keepdims=False) for s in ids[k]])

    dev = lambda px, py, pc: (4 * px + 2 * py + pc).astype(jnp.int32)
    ids = [jnp.stack([dev(mx, my, mc), dev(mx, my, 1 - mc)]),
           jnp.stack([dev(1 - mx, my, mc), dev(mx, 1 - my, mc)]),
           jnp.stack([dev(1 - mx, my, 1 - mc), dev(mx, 1 - my, 1 - mc)]),
           jnp.stack([dev(1 - mx, 1 - my, mc), dev(1 - mx, 1 - my, 1 - mc)])]
    assert lay.nc > 0, "the frames have no columns of their own at these sizes"
    proj, (g_in,) = _in_proj_core(xn, arrived(0), ids[0], lay, None, "in_proj_0", comms=[_ag(into=g_in, to=(0, 1))])
    proj, (g_in,) = _in_proj_core(xn, arrived(1), ids[1], lay, proj, "in_proj_1",
                                  comms=[_ag(into=g_in, to=(2,), forward=(0, 1))])
    proj, (g_in, g_out) = _in_proj_core(xn, arrived(2), ids[2], lay, proj, "in_proj_2",
                                        comms=[_ag(into=g_in, forward=(2,)), _ag_first(w_out_b)])
    eu = qu // 2
    proj, (g_up,) = _in_proj_core(xn, arrived(3), ids[3], lay, proj, "in_proj_3", comms=[_ag_first(w_up_b, rows=(0, eu))])
    proj, (g_up,) = _in_proj_rest(xn, g_in, jnp.asarray(lay.rest), lay.wp, proj, "in_proj_rest",
                                  comms=[_ag_first(w_up_b, rows=(eu, 2 * eu), into=g_up)])
    qkv, (g_out, g_up) = _qkvconv_fwd(proj, conv_q, bsz, seq, gw,
                                      comms=[_ag_second(g_out), _ag_first(w_up_b, rows=(2 * eu, 4 * eu), into=g_up)])
    (gdn_out, ssave, tsave), (g_up,) = _gdn_fwd(
        qkv, proj, alog_t, dtb_t, gdn_norm_w, bsz, seq, heads, 3 * gw, lay.ab_col,
        comms=[_ag_first(w_up_b, rows=(4 * eu, 8 * eu), into=g_up)])
    sc_out = _sc_fwd(proj, conv_s, bsz, seq, scw, lay.sc_col)
    mixed = jnp.concatenate([gdn_out, sc_out], axis=1)
    w_out_f = g_out.reshape(d, d)
    (mix,), (g_up, g_down) = _matmul(mixed, w_out_f, mode="nn", out_dtypes=[F32], name="out_proj",
                                     comms=[_ag_second(g_up), _ag_first(w_down_b, rows=(0, qd))])
    h, xn2 = _post1(x2, mix, g2, g3)

    def up_epilogue(acc):
        r = jnp.maximum(acc, 0.0)
        return r, r * r

    tq = t // 4
    act_hid = None
    for part in range(4):
        if part < 3:
            leg = [_ag_first(w_down_b, rows=((1 + part) * qd, (2 + part) * qd), into=g_down)]
        else:
            leg = [_ag_second(g_down)]
        act_hid, (g_down,) = _matmul(
            xn2, g_up, mode="nn", out_dtypes=[BF16, BF16], name="mlp_up_%d" % part, n_cols=dff_w, epilogue=up_epilogue,
            b_spec=lambda tk, tn: pl.BlockSpec((1, tk, tn), lambda i, j, k: (j // (up_cols // tn), k, j % (up_cols // tn))),
            a_rows=(part * tq, tq), out_into=act_hid, comms=leg)
    act, hid = act_hid
    w_down_f = g_down.reshape(dff_w, d)
    (ff,) = _matmul(hid, w_down_f, mode="nn", out_dtypes=[F32], name="mlp_down", tn=1024, tk=2048)
    dff, dy, dg4, loss_p = _post2_loss(h, ff, g4, tgt)

    def pieces(part, sib, got):
        return [(part, me), (sib, chip), (got, 0), (got, 1), (got, 2)]

    place = (mx, my, mc)

    (dpre,) = _matmul(dff, w_down_f, mode="nt", out_dtypes=[BF16], name="d_hidden", extras=[act],
                      epilogue=lambda acc, a: (acc * (2.0 * a.astype(F32)),))
    (dw_down,) = _matmul(hid, dff, mode="tn", out_dtypes=[BF16], name="dw_down")
    dw_down = dw_down.reshape(N_DEV, dff_w // N_DEV, d)
    (dxn2,), (sib_down,) = _matmul(
        dpre, g_up, mode="nt", out_dtypes=[F32], name="d_xn2", n_cols=d, tn=1024, tk=min(up_cols, 2048),
        b_spec=lambda tk, tn: pl.BlockSpec((1, tn, tk), lambda i, j, k: (k // (up_cols // tk), j, k % (up_cols // tk))),
        comms=[_rs_first(dw_down)])
    pair_down = _pair_add(dw_down, sib_down, place, "pair_add_down")
    (dw_up,), (got_down,) = _matmul(
        xn2, dpre, mode="tn", out_dtypes=[BF16], name="dw_up",
        out_custom=lambda tm, tn: ((N_DEV, d, up_cols), (1, tm, tn), lambda i, j, k: (j // (up_cols // tn), i, j % (up_cols // tn))),
        comms=[_rs_second(pair_down, rows=(0, 3 * qd))])
    (dmix, dh, dg2, dg3), (got_down, sib_up) = _mid_bwd(
        h, mix, dy, dxn2, g2, g3, comms=[_rs_second(pair_down, rows=(3 * qd, 4 * qd), into=got_down), _rs_first(dw_up)])
    pair_up = _pair_add(dw_up, sib_up, place, "pair_add_up")
    (dmixed,) = _matmul(dmix, w_out_f, mode="nt", out_dtypes=[F32], name="d_mixed")
    (dw_out,), (got_up,) = _matmul(mixed, dmix, mode="tn", out_dtypes=[BF16], name="dw_out",
                                   comms=[_rs_second(pair_up, rows=(0, qu))])
    dw_out = dw_out.reshape(N_DEV, d // N_DEV, d)
    (dscb, dscc, dsch, dconv_s), (sib_out,) = _sc_bwd(proj, conv_s, dmixed, bsz, seq, scw, lay.sc_col, gw,
                                                      comms=[_rs_first(dw_out)])
    pair_out = _pair_add(dw_out, sib_out, place, "pair_add_out")
    (dact, dz, dab, dalog, ddtb, dgnw), (got_up,) = _gdn_bwd(
        qkv, proj, alog_t, dtb_t, gdn_norm_w, ssave, tsave, dmixed, bsz, seq, heads, 3 * gw, lay.ab_col,
        comms=[_rs_second(pair_up, rows=(qu, 3 * qu), into=got_up)])
    (dqkv, dconv_q), (got_up,) = _qkvconv_bwd(proj, conv_q, dact, bsz, seq, gw,
                                              comms=[_rs_second(pair_up, rows=(3 * qu, 4 * qu), into=got_up)])
    dproj = jnp.concatenate([dqkv, dz, dab, dscb, dscc, dsch, jnp.zeros((t, lay.wp - lay.used), BF16)], axis=1)
    tn_in = _tile(lay.fw, 1024)
    nfb = lay.fw // tn_in
    hd2 = d // 2

    def dw_in_half(half, comms):
        return _matmul(
            xn, dproj, mode="tn", out_dtypes=[BF16], name="dw_in_%d" % half, n_cols=N_DEV * lay.fw, tn=tn_in,
            a_rows=(half * hd2, hd2),
            b_spec=lambda tk, tn: pl.BlockSpec(
                (pl.Element(tk), pl.Element(tn)),
                lambda i, j, k: (0, pl.multiple_of(lay.frame_block(j // nfb) * MXU + (j % nfb) * tn, LANE))),
            out_custom=lambda tm, tn: ((N_DEV, hd2, lay.fw), (1, tm, tn), lambda i, j, k: (j // nfb, i, j % nfb)),
            comms=comms)

    (dw_in_a,), (got_out,) = dw_in_half(0, [_rs_second(pair_out)])
    (dw_in_b,), (sib_a,) = dw_in_half(1, [_rs_first(dw_in_a)])
    pair_a = _pair_add(dw_in_a, sib_a, place, "pair_add_in_a")
    tk_in = _tile(lay.fw, 2048)
    kpf = lay.fw // tk_in

    def d_xn(part, into, comms):
        return _matmul(
            dproj, g_in, mode="nt", out_dtypes=[F32], name="d_xn_%d" % part, n_cols=d, tn=1024, tk=tk_in,
            k_total=N_DEV * lay.fw, a_rows=(part * tq, tq), out_into=None if into is None else [into], comms=comms,
            a_spec=lambda tm, tk, r0: pl.BlockSpec(
                (pl.Element(tm), pl.Element(tk)),
                lambda i, j, k: (pl.multiple_of(i * tm + r0, 16),
                                 pl.multiple_of(lay.frame_block(k // kpf) * MXU + (k % kpf) * tk, LANE))),
            b_spec=lambda tk, tn: pl.BlockSpec((1, tn, tk), lambda i, j, k: (k // kpf, j, k % kpf)))

    (dxn,), (got_a, sib_b) = d_xn(0, None, [_rs_second(pair_a, rows=(0, qu)), _rs_first(dw_in_b)])
    pair_b = _pair_add(dw_in_b, sib_b, place, "pair_add_in_b")
    (dxn,), (got_a,) = d_xn(1, dxn, [_rs_second(pair_a, rows=(qu, 2 * qu), into=got_a)])
    (dxn,), (got_b,) = d_xn(2, dxn, [_rs_second(pair_b, rows=(0, qu))])
    (dxn,), (got_b,) = d_xn(3, dxn, [_rs_second(pair_b, rows=(qu, 2 * qu), into=got_b)])
    grad_x, dg1 = _pre_bwd(x2, dh, dxn, g1)

    gin_frame = jnp.concatenate([_sum_to(pieces(dw_in_a, sib_a, got_a), F32, "grad_w_in_frame_a"),
                                 _sum_to(pieces(dw_in_b, sib_b, got_b), F32, "grad_w_in_frame_b")], axis=0)
    big = {
        "w_in": _adamw(w_in[0], m_w_in[0], v_w_in[0], [(lay.from_frame(gin_frame, me), None)], "adamw_w_in"),
        "w_out": _adamw(w_out[0], m_w_out[0], v_w_out[0], pieces(dw_out, sib_out, got_out), "adamw_w_out"),
        "w_up": _adamw(w_up[0], m_w_up[0], v_w_up[0], pieces(dw_up, sib_up, got_up), "adamw_w_up"),
        "w_down": _adamw(w_down[0], m_w_down[0], v_w_down[0], pieces(dw_down, sib_down, got_down), "adamw_w_down"),
    }

    small_shapes = [(kq, 3 * gw), (ks, scw), (1, d), (1, d), (1, d), (1, d), (1, LANE), (1, LANE), (1, hd), (1, LANE)]
    small = _unpack_rows(
        _allreduce_small(_pack_rows([dconv_q, dconv_s, dg1, dg2, dg3, dg4, jnp.sum(dalog, axis=0), jnp.sum(ddtb, axis=0),
                                     jnp.sum(dgnw, axis=0), loss_p]), "allreduce_small"), small_shapes)
    gq, gs, sg1, sg2, sg3, sg4, salog, sdtb, sgnw, sloss = small
    loss = sloss[0, 0]
    small_grads = {
        "norm_mix_pre": sg1, "conv_qkv_w": lax.dynamic_slice(gq, (0, me * cq_n), (kq, cq_n)),
        "a_log": salog[:, :heads], "dt_bias": sdtb[:, :heads], "gdn_norm_w": sgnw,
        "conv_sc_w": lax.dynamic_slice(gs, (0, me * cs_n), (ks, cs_n)),
        "norm_mix_post": sg2, "norm_mlp_pre": sg3, "norm_mlp_post": sg4,
    }
    weights = {"norm_mix_pre": (norm_mix_pre, m_norm_mix_pre, v_norm_mix_pre), "conv_qkv_w": (conv_qkv_w[0], m_conv_qkv_w[0], v_conv_qkv_w[0]),
               "a_log": (a_log, m_a_log, v_a_log), "dt_bias": (dt_bias, m_dt_bias, v_dt_bias),
               "gdn_norm_w": (gdn_norm_w, m_gdn_norm_w, v_gdn_norm_w), "conv_sc_w": (conv_sc_w[0], m_conv_sc_w[0], v_conv_sc_w[0]),
               "norm_mix_post": (norm_mix_post, m_norm_mix_post, v_norm_mix_post),
               "norm_mlp_pre": (norm_mlp_pre, m_norm_mlp_pre, v_norm_mlp_pre),
               "norm_mlp_post": (norm_mlp_post, m_norm_mlp_post, v_norm_mlp_post)}
    res = dict(big)
    for name, (w, m, v) in weights.items():
        res[name] = _adamw(w, m, v, [(small_grads[name], None)], "adamw_" + name)

    order = ["norm_mix_pre", "w_in", "conv_qkv_w", "a_log", "dt_bias", "gdn_norm_w", "conv_sc_w", "w_out", "norm_mix_post",
             "norm_mlp_pre", "w_up", "w_down", "norm_mlp_post"]
    shapes = {"norm_mix_pre": norm_mix_pre.shape, "w_in": w_in.shape, "conv_qkv_w": conv_qkv_w.shape, "a_log": a_log.shape,
              "dt_bias": dt_bias.shape, "gdn_norm_w": gdn_norm_w.shape, "conv_sc_w": conv_sc_w.shape, "w_out": w_out.shape,
              "norm_mix_post": norm_mix_post.shape, "norm_mlp_pre": norm_mlp_pre.shape, "w_up": w_up.shape,
              "w_down": w_down.shape, "norm_mlp_post": norm_mlp_post.shape}
    outs = [loss, grad_x.reshape(bsz, seq, d)]
    for part in range(4):
        outs += [res[nm][part].reshape(shapes[nm]) for nm in order]
    return tuple(outs)
```

```python
import functools
import math

import numpy as np
import jax
import jax.numpy as jnp
from jax import lax
from jax.experimental import pallas as pl
from jax.experimental.pallas import tpu as pltpu

F32 = jnp.float32
BF16 = jnp.bfloat16
HI = lax.Precision.HIGHEST
MESH = pl.DeviceIdType.MESH

N_DEV = 8
LANE = 128
MXU = 256
CHUNK = 64
NORM_EPS = 1e-6
L2_EPS = 1e-6
VMEM_LIMIT = 56 * 1024 * 1024

ADAM_LR = 0.001
ADAM_B1 = 0.9
ADAM_B2 = 0.999
ADAM_EPS = 1e-08
ADAM_WD = 0.01
ADAM_STEP = 10

NN = (((1,), (0,)), ((), ()))
NT = (((1,), (1,)), ((), ()))
TN = (((0,), (0,)), ((), ()))


def _params(*sem):
    return pltpu.CompilerParams(dimension_semantics=sem, vmem_limit_bytes=VMEM_LIMIT)


def _tile(n, want):
    if n <= want:
        return n
    t = (want // LANE) * LANE
    while t > LANE and n % t:
        t -= LANE
    assert n % t == 0, (n, want)
    return t


class _Comm:
    def __init__(self, srcs, lands, plan, n, alias=None):
        self.srcs, self.lands, self.plan, self.n, self.alias = list(srcs), list(lands), plan, n, dict(alias or {})


def _pcall(body, *, grid, in_specs, out_specs, out_shape, operands, name, scratch_shapes=(), semantics=None,
           prefetch=(), comms=(), fill=None):
    n_pf, n_in, n_out, n_scr = len(prefetch), len(in_specs), len(out_specs), len(scratch_shapes)
    srcs = [s for cm in comms for s in cm.srcs]
    lands = [l for cm in comms for l in cm.lands]
    n_src, n_land = len(srcs), len(lands)
    n_copies = sum(cm.n for cm in comms)
    aliases, so, lo = {n_pf + a: b for a, b in (fill or {}).items()}, 0, 0
    for cm in comms:
        for a, b in cm.alias.items():
            aliases[n_pf + n_in + so + a] = n_out + lo + b
        so, lo = so + len(cm.srcs), lo + len(cm.lands)
    any_spec = pl.BlockSpec(memory_space=pl.ANY)

    def wrapped(*refs):
        pf, r = refs[:n_pf], refs[n_pf:]
        ins, csrc = r[:n_in], r[n_in:n_in + n_src]
        outs = r[n_in + n_src:n_in + n_src + n_out]
        cland = r[n_in + n_src + n_out:n_in + n_src + n_out + n_land]
        rest = r[n_in + n_src + n_out + n_land:]
        scratch = rest[:n_scr]
        if not comms:
            body(*pf, *ins, *outs, *scratch)
            return
        send_sems, recv_sems = rest[n_scr:]

        def copies():
            out, k, s0, l0 = [], 0, 0, 0
            for cm in comms:
                for kind, src, dst, dev in cm.plan(csrc[s0:s0 + len(cm.srcs)], cland[l0:l0 + len(cm.lands)]):
                    if kind == "local":
                        out.append((kind, pltpu.make_async_copy(src, dst, send_sems.at[k])))
                    else:
                        out.append((kind, pltpu.make_async_remote_copy(
                            src_ref=src, dst_ref=dst, send_sem=send_sems.at[k], recv_sem=recv_sems.at[k],
                            device_id=dev, device_id_type=MESH)))
                    k += 1
                s0, l0 = s0 + len(cm.srcs), l0 + len(cm.lands)
            assert k == n_copies
            return out

        ids = [pl.program_id(a) for a in range(len(grid))]
        first = functools.reduce(jnp.logical_and, [i == 0 for i in ids])
        last = functools.reduce(jnp.logical_and, [i == g - 1 for i, g in zip(ids, grid)])

        @pl.when(first)
        def _():
            for _, cp in copies():
                cp.start()

        body(*pf, *ins, *outs, *scratch)

        @pl.when(last)
        def _():
            cps = copies()
            for kind, cp in cps:
                if kind == "remote":
                    cp.wait_recv()
            for kind, cp in cps:
                if kind == "remote":
                    cp.wait_send()
                else:
                    cp.wait()

    sems = [pltpu.SemaphoreType.DMA((n_copies,)), pltpu.SemaphoreType.DMA((n_copies,))] if comms else []
    if semantics is None or comms:
        semantics = ("arbitrary",) * len(grid)
    res = pl.pallas_call(
        wrapped,
        grid_spec=pltpu.PrefetchScalarGridSpec(
            num_scalar_prefetch=n_pf, grid=tuple(grid), in_specs=list(in_specs) + [any_spec] * n_src,
            out_specs=list(out_specs) + [any_spec] * n_land, scratch_shapes=list(scratch_shapes) + sems),
        out_shape=list(out_shape) + lands,
        input_output_aliases=aliases,
        compiler_params=_params(*semantics), name=name)(*prefetch, *operands, *srcs)
    return list(res[:n_out]), list(res[n_out:])


def _bdot(a, b, dims=NN):
    return lax.dot_general(a.astype(BF16), b.astype(BF16), dims, preferred_element_type=F32)


def _hdot(a, b, dims=NN):
    return lax.dot_general(a, b, dims, preferred_element_type=F32, precision=HI)


def _mdot(a, b, dims=NN):
    return lax.dot_general(a, b, dims, preferred_element_type=F32, precision=lax.Precision.HIGH)


def _sigmoid(x):
    return 1.0 / (1.0 + jnp.exp(-x))


def _softplus(x):
    return jnp.maximum(x, 0.0) + jnp.log(1.0 + jnp.exp(-jnp.abs(x)))


def _matmul(a, b, *, mode, out_dtypes, name, n_cols=None, tm=1024, tn=512, tk=4096, epilogue=None, extras=(),
            b_spec=None, out_custom=None, a_rows=None, out_into=None, a_spec=None, k_total=None, comms=()):
    if mode == "tn":
        K, M = a.shape
    else:
        M, K = a.shape
    if k_total is not None:
        K = k_total
    r0 = 0
    if a_rows is not None:
        r0, M = a_rows
    N = n_cols if n_cols is not None else (b.shape[0] if mode == "nt" else b.shape[1])
    tm, tk, tn = _tile(M, tm), _tile(K, tk), _tile(N, tn)
    assert r0 % tm == 0
    i0 = r0 // tm
    if b_spec is None:
        b_spec = pl.BlockSpec((tn, tk), lambda i, j, k: (j, k)) if mode == "nt" else pl.BlockSpec((tk, tn), lambda i, j, k: (k, j))
    else:
        b_spec = b_spec(tk, tn)
    gm, gn, nk = M // tm, N // tn, K // tk
    if out_custom is not None:
        shape, blk, ix = out_custom(tm, tn)
        out_shapes, out_blocks, out_index = [shape] * len(out_dtypes), [blk] * len(out_dtypes), [ix] * len(out_dtypes)
    elif a_rows is not None:
        out_shapes = [(a.shape[0], N)] * len(out_dtypes)
        out_blocks = [(tm, tn)] * len(out_dtypes)
        out_index = [lambda i, j, k: (i + i0, j)] * len(out_dtypes)
    else:
        out_shapes = [(M, N)] * len(out_dtypes)
        out_blocks = [(tm, tn)] * len(out_dtypes)
        out_index = [lambda i, j, k: (i, j)] * len(out_dtypes)
    if a_spec is not None:
        a_spec = a_spec(tm, tk, r0)
    elif mode == "tn":
        a_spec = pl.BlockSpec((tk, tm), lambda i, j, k: (k, i + i0))
    else:
        a_spec = pl.BlockSpec((tm, tk), lambda i, j, k: (i + i0, k))
    hoist = mode == "tn" and nk == 1 and gn > 1
    dims = {"nn": NN, "nt": NT, "tn": TN}[mode]
    n_ex, n_out = len(extras), len(out_dtypes)
    out_into = [] if out_into is None else list(out_into)
    n_into = len(out_into)
    assert n_into in (0, n_out)

    def body(a_ref, b_ref, *rest):
        ex, outs = rest[:n_ex], rest[n_ex + n_into:n_ex + n_into + n_out]

        def finish(acc):
            res = epilogue(acc, *[e[...] for e in ex]) if epilogue is not None else (acc,)
            for o, r in zip(outs, res):
                o[...] = r.reshape(o.shape).astype(o.dtype)

        bb = b_ref[...]
        bb = bb.reshape(bb.shape[-2:])
        if hoist:
            at_ref = rest[-1]

            @pl.when(pl.program_id(1) == 0)
            def _():
                at_ref[...] = a_ref[...].T

            finish(lax.dot_general(at_ref[...], bb, NN, preferred_element_type=F32))
            return
        part = lax.dot_general(a_ref[...], bb, dims, preferred_element_type=F32)
        if nk == 1:
            finish(part)
        else:
            acc = rest[-1]
            k = pl.program_id(2)

            @pl.when(k == 0)
            def _():
                acc[...] = part

            @pl.when(k > 0)
            def _():
                acc[...] += part

            @pl.when(k == nk - 1)
            def _():
                finish(acc[...])

    scratch = [pltpu.VMEM((tm, tk), BF16)] if hoist else ([pltpu.VMEM((tm, tn), F32)] if nk > 1 else [])
    outs, lands = _pcall(
        body, grid=(gm, gn, nk),
        in_specs=([a_spec, b_spec] + [pl.BlockSpec((tm, tn), lambda i, j, k: (i, j)) for _ in extras]
                  + [pl.BlockSpec(memory_space=pl.ANY)] * n_into),
        out_specs=[pl.BlockSpec(blk, ix) for blk, ix in zip(out_blocks, out_index)],
        out_shape=[jax.ShapeDtypeStruct(s, d) for s, d in zip(out_shapes, out_dtypes)],
        scratch_shapes=scratch, semantics=("parallel", "arbitrary", "arbitrary"),
        operands=[a, b, *extras, *out_into], name=name, comms=comms,
        fill={2 + n_ex + o: o for o in range(n_into)})
    return (outs, lands) if comms else outs


TR = 256
QKV_CONV_COLS = 256


def _rms(x):
    return lax.rsqrt(jnp.mean(x * x, axis=-1, keepdims=True) + NORM_EPS)


def _rms_bwd(x, r, w, dy):
    u = dy * w
    dx = r * u - x * (r * r * r) * jnp.mean(x * u, axis=-1, keepdims=True)
    return dx, dy * x * r


def _row_call(body, ins, row_flags, outs, name, n_rows, comms=()):
    n_row = sum(row_flags) + sum(kind == "row" for _, _, kind in outs)
    tr = min(TR if n_row <= 5 else TR // 2, n_rows)
    in_specs = []
    for arr, is_row in zip(ins, row_flags):
        if is_row:
            in_specs.append(pl.BlockSpec((tr, arr.shape[1]), lambda i: (i, 0)))
        else:
            in_specs.append(pl.BlockSpec(arr.shape, lambda i: (0, 0)))
    out_specs, out_shape = [], []
    for shape, dtype, kind in outs:
        if kind == "row":
            out_specs.append(pl.BlockSpec((tr, shape[1]), lambda i: (i, 0)))
        else:
            out_specs.append(pl.BlockSpec(shape, lambda i: (0, 0)))
        out_shape.append(jax.ShapeDtypeStruct(shape, dtype))
    res, lands = _pcall(body, grid=(n_rows // tr,), in_specs=in_specs, out_specs=out_specs, out_shape=out_shape,
                        operands=list(ins), name=name, comms=comms)
    return (res, lands) if comms else res


def _acc_out(ref, val):
    @pl.when(pl.program_id(0) == 0)
    def _():
        ref[...] = val

    @pl.when(pl.program_id(0) > 0)
    def _():
        ref[...] += val


def _rms_fwd(x, g, comms):
    T, D = x.shape

    def body(x_ref, g_ref, o_ref):
        xv = x_ref[...]
        o_ref[...] = (xv * _rms(xv) * g_ref[...]).astype(BF16)

    res, lands = _row_call(body, [x, g], [True, False], [((T, D), BF16, "row")], "rms_fwd", T, comms=comms)
    return res[0], lands


def _post1(x, mix, g2, g3, comms=()):
    T, D = x.shape

    def body(x_ref, mix_ref, g2_ref, g3_ref, h_ref, xn2_ref):
        mv = mix_ref[...]
        h = x_ref[...] + mv * _rms(mv) * g2_ref[...]
        h_ref[...] = h
        xn2_ref[...] = (h * _rms(h) * g3_ref[...]).astype(BF16)

    return _row_call(body, [x, mix, g2, g3], [True, True, False, False],
                     [((T, D), F32, "row"), ((T, D), BF16, "row")], "post1", T, comms=comms)


def _post2_loss(h, ff, g4, target):
    T, D = h.shape

    def body(h_ref, ff_ref, g4_ref, t_ref, dff_ref, dy_ref, dg4_ref, loss_ref):
        fv = ff_ref[...]
        r = _rms(fv)
        err = h_ref[...] + fv * r * g4_ref[...] - t_ref[...]
        dy = err * (1.0 / D)
        dy_ref[...] = dy
        dff, dwt = _rms_bwd(fv, r, g4_ref[...], dy)
        dff_ref[...] = dff.astype(BF16)
        _acc_out(dg4_ref, jnp.sum(dwt, axis=0, keepdims=True))
        part = 0.5 * jnp.sum(jnp.mean(err * err, axis=-1, keepdims=True), axis=0, keepdims=True)
        _acc_out(loss_ref, jnp.broadcast_to(part, (1, LANE)))

    return _row_call(body, [h, ff, g4, target], [True, True, False, True],
                     [((T, D), BF16, "row"), ((T, D), F32, "row"), ((1, D), F32, "acc"), ((1, LANE), F32, "acc")],
                     "post2_loss", T)


def _mid_bwd(h, mix, dy, dxn2, g2, g3, comms=()):
    T, D = h.shape

    def body(h_ref, mix_ref, dy_ref, dxn2_ref, g2_ref, g3_ref, dmix_ref, dh_ref, dg2_ref, dg3_ref):
        hv = h_ref[...]
        d1, dw3 = _rms_bwd(hv, _rms(hv), g3_ref[...], dxn2_ref[...])
        dh = dy_ref[...] + d1
        dh_ref[...] = dh
        mv = mix_ref[...]
        dmix, dw2 = _rms_bwd(mv, _rms(mv), g2_ref[...], dh)
        dmix_ref[...] = dmix.astype(BF16)
        _acc_out(dg2_ref, jnp.sum(dw2, axis=0, keepdims=True))
        _acc_out(dg3_ref, jnp.sum(dw3, axis=0, keepdims=True))

    return _row_call(body, [h, mix, dy, dxn2, g2, g3], [True, True, True, True, False, False],
                     [((T, D), BF16, "row"), ((T, D), F32, "row"), ((1, D), F32, "acc"), ((1, D), F32, "acc")],
                     "mid_bwd", T, comms=comms)


def _pre_bwd(x, dh, dxn, g1, comms=()):
    T, D = x.shape

    def body(x_ref, dh_ref, dxn_ref, g1_ref, gx_ref, dg1_ref):
        xv = x_ref[...]
        d1, dw1 = _rms_bwd(xv, _rms(xv), g1_ref[...], dxn_ref[...])
        gx_ref[...] = dh_ref[...] + d1
        _acc_out(dg1_ref, jnp.sum(dw1, axis=0, keepdims=True))

    return _row_call(body, [x, dh, dxn, g1], [True, True, True, False],
                     [((T, D), F32, "row"), ((1, D), F32, "acc")], "pre_bwd", T, comms=comms)


def _shift_down(x, s):
    if s == 0:
        return x
    row = lax.broadcasted_iota(jnp.int32, x.shape, 0)
    return jnp.where(row >= s, pltpu.roll(x, s, axis=0), 0.0)


def _shift_up(x, s):
    if s == 0:
        return x
    n = x.shape[0]
    row = lax.broadcasted_iota(jnp.int32, x.shape, 0)
    return jnp.where(row < n - s, pltpu.roll(x, n - s, axis=0), 0.0)


def _conv(x, w):
    kw = w.shape[0]
    out = w[kw - 1:kw, :] * x
    for j in range(kw - 1):
        out = out + w[j:j + 1, :] * _shift_down(x, kw - 1 - j)
    return out


def _conv_bwd(x, w, dout):
    kw = w.shape[0]
    dx = w[kw - 1:kw, :] * dout
    dws = []
    for j in range(kw - 1):
        dx = dx + w[j:j + 1, :] * _shift_up(dout, kw - 1 - j)
        dws.append(jnp.sum(dout * _shift_down(x, kw - 1 - j), axis=0, keepdims=True))
    dws.append(jnp.sum(dout * x, axis=0, keepdims=True))
    return dx, jnp.concatenate(dws, axis=0)


def _qkvconv_fwd(proj, w, bsz, seq, gw, comms=()):
    cw = QKV_CONV_COLS
    nct = gw // cw
    kw = w.shape[0]

    def body(p_ref, w_ref, o_ref):
        cv = _conv(p_ref[...], w_ref[...])
        o_ref[...] = (cv * _sigmoid(cv)).reshape(o_ref.shape)

    res, lands = _pcall(
        body, grid=(3, bsz, nct),
        in_specs=[pl.BlockSpec((seq, cw), lambda p, b, c: (b, p * nct + c)),
                  pl.BlockSpec((kw, cw), lambda p, b, c: (0, p * nct + c))],
        out_specs=[pl.BlockSpec((1, seq, cw), lambda p, b, c: (p, b, c))],
        out_shape=[jax.ShapeDtypeStruct((3, bsz * seq, gw), F32)],
        semantics=("parallel", "parallel", "parallel"), operands=[proj, w], name="qkvconv_fwd", comms=comms)
    return res[0], lands


def _qkvconv_bwd(proj, w, dact, bsz, seq, gw, comms=()):
    cw = QKV_CONV_COLS
    nct = gw // cw
    kw = w.shape[0]

    def body(p_ref, w_ref, d_ref, dp_ref, dw_ref):
        pre = p_ref[...]
        wv = w_ref[...]
        cv = _conv(pre, wv)
        sg = _sigmoid(cv)
        dcv = d_ref[...].reshape(cv.shape) * (sg * (1.0 + cv * (1.0 - sg)))
        dpre, dw = _conv_bwd(pre, wv, dcv)
        dp_ref[...] = dpre.astype(BF16)
        b = pl.program_id(2)

        @pl.when(b == 0)
        def _():
            dw_ref[...] = dw

        @pl.when(b > 0)
        def _():
            dw_ref[...] += dw

    res, lands = _pcall(
        body, grid=(3, nct, bsz),
        in_specs=[pl.BlockSpec((seq, cw), lambda p, c, b: (b, p * nct + c)),
                  pl.BlockSpec((kw, cw), lambda p, c, b: (0, p * nct + c)),
                  pl.BlockSpec((1, seq, cw), lambda p, c, b: (p, b, c))],
        out_specs=[pl.BlockSpec((seq, cw), lambda p, c, b: (b, p * nct + c)),
                   pl.BlockSpec((kw, cw), lambda p, c, b: (0, p * nct + c))],
        out_shape=[jax.ShapeDtypeStruct((bsz * seq, 3 * gw), BF16), jax.ShapeDtypeStruct((kw, 3 * gw), F32)],
        semantics=("parallel", "parallel", "arbitrary"), operands=[proj, w, dact], name="qkvconv_bwd", comms=comms)
    return res, lands


def _sc_fwd(proj, w, bsz, seq, scw, col0):
    nct = scw // LANE
    c0 = col0 // LANE
    kw = w.shape[0]

    def body(b_ref, c_ref, h_ref, w_ref, o_ref):
        o_ref[...] = (b_ref[...] * _conv(c_ref[...] * h_ref[...], w_ref[...])).astype(BF16)

    return pl.pallas_call(
        body, grid=(bsz, nct),
        in_specs=[pl.BlockSpec((seq, LANE), lambda b, c: (b, c0 + c)),
                  pl.BlockSpec((seq, LANE), lambda b, c: (b, c0 + nct + c)),
                  pl.BlockSpec((seq, LANE), lambda b, c: (b, c0 + 2 * nct + c)),
                  pl.BlockSpec((kw, LANE), lambda b, c: (0, c))],
        out_specs=pl.BlockSpec((seq, LANE), lambda b, c: (b, c)),
        out_shape=jax.ShapeDtypeStruct((bsz * seq, scw), BF16),
        compiler_params=_params("parallel", "parallel"), name="sc_fwd")(proj, proj, proj, w)


def _sc_bwd(proj, w, dout, bsz, seq, scw, col0, dcol0, comms=()):
    nct = scw // LANE
    c0 = col0 // LANE
    d0 = dcol0 // LANE
    kw = w.shape[0]

    def body(b_ref, c_ref, h_ref, w_ref, d_ref, db_ref, dc_ref, dh_ref, dw_ref):
        cc, hh, wv, dv = c_ref[...], h_ref[...], w_ref[...], d_ref[...]
        m = cc * hh
        db_ref[...] = (dv * _conv(m, wv)).astype(BF16)
        dm, dw = _conv_bwd(m, wv, dv * b_ref[...])
        dc_ref[...] = (dm * hh).astype(BF16)
        dh_ref[...] = (dm * cc).astype(BF16)
        b = pl.program_id(1)

        @pl.when(b == 0)
        def _():
            dw_ref[...] = dw

        @pl.when(b > 0)
        def _():
            dw_ref[...] += dw

    res, lands = _pcall(
        body, grid=(nct, bsz),
        in_specs=[pl.BlockSpec((seq, LANE), lambda c, b: (b, c0 + c)),
                  pl.BlockSpec((seq, LANE), lambda c, b: (b, c0 + nct + c)),
                  pl.BlockSpec((seq, LANE), lambda c, b: (b, c0 + 2 * nct + c)),
                  pl.BlockSpec((kw, LANE), lambda c, b: (0, c)),
                  pl.BlockSpec((seq, LANE), lambda c, b: (b, d0 + c))],
        out_specs=[pl.BlockSpec((seq, LANE), lambda c, b: (b, c)),
                   pl.BlockSpec((seq, LANE), lambda c, b: (b, c)),
                   pl.BlockSpec((seq, LANE), lambda c, b: (b, c)),
                   pl.BlockSpec((kw, LANE), lambda c, b: (0, c))],
        out_shape=[jax.ShapeDtypeStruct((bsz * seq, scw), BF16)] * 3 + [jax.ShapeDtypeStruct((kw, scw), F32)],
        semantics=("parallel", "arbitrary"), operands=[proj, proj, proj, w, dout], name="sc_bwd", comms=comms)
    return res, lands


HEADS_PER_STEP = 16


def _colsel(tile, idx):
    lane = lax.broadcasted_iota(jnp.int32, tile.shape, 1)
    return jnp.sum(jnp.where(lane == idx, tile, 0.0), axis=1, keepdims=True)


def _rowsel(tile, idx):
    row = lax.broadcasted_iota(jnp.int32, tile.shape, 0)
    return jnp.sum(jnp.where(row == idx, tile, 0.0), axis=0, keepdims=True)


def _colput(col, idx, width=LANE):
    lane = lax.broadcasted_iota(jnp.int32, (col.shape[0], width), 1)
    return jnp.where(lane == idx, col, 0.0)


def _tri_masks(c):
    row = lax.broadcasted_iota(jnp.int32, (c, c), 0)
    col = lax.broadcasted_iota(jnp.int32, (c, c), 1)
    return row >= col, row > col, row == col


def _unit_lower_inverses(ms):
    c = ms[0].shape[0]
    _, _, eye = _tri_masks(c)
    ps = [-m for m in ms]
    ts = [jnp.where(eye, 1.0, 0.0) + p for p in ps]
    for _ in range(int(math.log2(c)) - 1):
        ps = [_mdot(p, p) for p in ps]
        ts = [t + _mdot(t, p) for t, p in zip(ts, ps)]
    return ts


def _gates(ab, alog, dtb):
    g = -jnp.exp(alog) * _softplus(ab + dtb)
    return g, _sigmoid(ab)


def _l2n(x):
    r = lax.rsqrt(jnp.sum(x * x, axis=-1, keepdims=True) + L2_EPS)
    return x * r, r


def _gdn_chunk_common(q, k, gc, gr, bc):
    c, dk = q.shape
    incl, strict, _ = _tri_masks(c)
    qh, rq = _l2n(q)
    kn, rk = _l2n(k)
    qn = qh * (dk ** -0.5)
    dm = jnp.where(incl, jnp.exp(jnp.where(incl, gc - gr, 0.0)), 0.0)
    kk = _bdot(kn, kn, NT)
    m = jnp.where(strict, bc * kk * dm, 0.0)
    pm = jnp.where(incl, _bdot(qn, kn, NT) * dm, 0.0)
    return qh, rq, kn, rk, qn, dm, kk, m, pm


def _gdn_fwd(qkv, proj, alog, dtb, gnw, bsz, seq, heads, z_col0, ab_col0, comms=()):
    c = CHUNK
    nch = seq // c
    hb = min(HEADS_PER_STEP, heads)
    ng = heads // hb
    hd = qkv.shape[2] // heads
    wb = hb * hd

    def body(qkv_ref, z_ref, ab_ref, alog_ref, dtb_ref, gnw_ref, o_ref, ssave_ref, tsave_ref, s_ref, gc_ref, gt_ref, be_ref):
        n, hg = pl.program_id(1), pl.program_id(2)

        @pl.when((n == 0) & (hg == 0))
        def _():
            s_ref[...] = jnp.zeros_like(s_ref)

        @pl.when(hg == 0)
        def _():
            g, beta = _gates(ab_ref[...], alog_ref[...], dtb_ref[...])
            incl, _, _ = _tri_masks(c)
            gcum = _hdot(jnp.where(incl, 1.0, 0.0), g)
            gc_ref[...] = gcum
            gt_ref[...] = gcum.T
            be_ref[...] = beta

        gc_t, gt_t, be_t, gnw_v = gc_ref[...], gt_ref[...], be_ref[...], gnw_ref[...]
        hs = range(hb)
        sls = [slice(hh * hd, (hh + 1) * hd) for hh in hs]
        states = [s_ref[hg * hb + hh] for hh in hs]
        gcs = [_colsel(gc_t, hg * hb + hh) for hh in hs]
        grs = [_rowsel(gt_t, hg * hb + hh) for hh in hs]
        bcs = [_colsel(be_t, heads + hg * hb + hh) for hh in hs]
        com = [_gdn_chunk_common(qkv_ref[0, :, sls[hh]], qkv_ref[1, :, sls[hh]], gcs[hh], grs[hh], bcs[hh]) for hh in hs]
        kns, qns, pms = [cm[2] for cm in com], [cm[4] for cm in com], [cm[8] for cm in com]
        tms = _unit_lower_inverses([cm[7] for cm in com])
        gams = [jnp.exp(gc) for gc in gcs]
        glasts = [gc[c - 1:c, :] for gc in gcs]
        kss = [_bdot(kns[hh], states[hh]) for hh in hs]
        qss = [_bdot(qns[hh], states[hh]) for hh in hs]
        vns = [_bdot(tms[hh], bcs[hh] * (qkv_ref[2, :, sls[hh]] - gams[hh] * kss[hh])) for hh in hs]
        os_ = [gams[hh] * qss[hh] + _bdot(pms[hh], vns[hh]) for hh in hs]
        snews = [states[hh] * jnp.exp(glasts[hh]) + _bdot(kns[hh] * jnp.exp(glasts[hh] - gcs[hh]), vns[hh], TN) for hh in hs]
        for hh in hs:
            o = os_[hh]
            on = o * lax.rsqrt(jnp.mean(o * o, axis=-1, keepdims=True) + NORM_EPS) * gnw_v
            zz = z_ref[:, sls[hh]]
            ssave_ref[0, 0, hh] = states[hh]
            tsave_ref[0, 0, hh] = tms[hh]
            s_ref[hg * hb + hh] = snews[hh]
            o_ref[:, sls[hh]] = (on * (zz * _sigmoid(zz))).astype(BF16)

    row = lambda b, n, g: b * nch + n
    return _pcall(
        body, grid=(bsz, nch, ng),
        in_specs=[pl.BlockSpec((3, c, wb), lambda b, n, g: (0, row(b, n, g), g)),
                  pl.BlockSpec((c, wb), lambda b, n, g: (row(b, n, g), z_col0 // wb + g)),
                  pl.BlockSpec((c, LANE), lambda b, n, g: (row(b, n, g), ab_col0 // LANE)),
                  pl.BlockSpec((1, LANE), lambda b, n, g: (0, 0)),
                  pl.BlockSpec((1, LANE), lambda b, n, g: (0, 0)),
                  pl.BlockSpec((1, hd), lambda b, n, g: (0, 0))],
        out_specs=[pl.BlockSpec((c, wb), lambda b, n, g: (row(b, n, g), g)),
                   pl.BlockSpec((1, 1, hb, hd, hd), lambda b, n, g: (b, n, g, 0, 0)),
                   pl.BlockSpec((1, 1, hb, c, c), lambda b, n, g: (b, n, g, 0, 0))],
        out_shape=[jax.ShapeDtypeStruct((bsz * seq, heads * hd), BF16),
                   jax.ShapeDtypeStruct((bsz, nch, heads, hd, hd), F32),
                   jax.ShapeDtypeStruct((bsz, nch, heads, c, c), F32)],
        scratch_shapes=[pltpu.VMEM((heads, hd, hd), F32), pltpu.VMEM((c, LANE), F32), pltpu.VMEM((LANE, c), F32),
                        pltpu.VMEM((c, LANE), F32)],
        semantics=("parallel", "arbitrary", "arbitrary"), operands=[qkv, proj, proj, alog, dtb, gnw], name="gdn_fwd",
        comms=comms)


def _gdn_bwd(qkv, proj, alog, dtb, gnw, ssave, tsave, dout, bsz, seq, heads, z_col0, ab_col0, comms=()):
    c = CHUNK
    nch = seq // c
    hb = min(HEADS_PER_STEP, heads)
    ng = heads // hb
    hd = qkv.shape[2] // heads
    wb = hb * hd

    def body(qkv_ref, z_ref, ab_ref, alog_ref, dtb_ref, gnw_ref, ssave_ref, tsave_ref, do_ref,
             dact_ref, dz_ref, dab_ref, dalog_ref, ddtb_ref, dgnw_ref,
             ds_ref, gc_ref, gt_ref, be_ref, dgacc_ref, dbacc_ref):
        n, hg = pl.program_id(1), pl.program_id(2)
        incl, strict, _ = _tri_masks(c)

        @pl.when((n == 0) & (hg == 0))
        def _():
            ds_ref[...] = jnp.zeros_like(ds_ref)
            dalog_ref[...] = jnp.zeros_like(dalog_ref)
            ddtb_ref[...] = jnp.zeros_like(ddtb_ref)
            dgnw_ref[...] = jnp.zeros_like(dgnw_ref)

        @pl.when(hg == 0)
        def _():
            g, beta = _gates(ab_ref[...], alog_ref[...], dtb_ref[...])
            gcum = _hdot(jnp.where(incl, 1.0, 0.0), g)
            gc_ref[...] = gcum
            gt_ref[...] = gcum.T
            be_ref[...] = beta
            dgacc_ref[...] = jnp.zeros_like(dgacc_ref)
            dbacc_ref[...] = jnp.zeros_like(dbacc_ref)

        gc_t, gt_t, be_t, gnw_v = gc_ref[...], gt_ref[...], be_ref[...], gnw_ref[...]
        hs = range(hb)

        def each(f):
            return [f(hh) for hh in hs]

        rsum = lambda a: jnp.sum(a, axis=-1, keepdims=True)
        sls = each(lambda i: slice(i * hd, (i + 1) * hd))
        ds_in = each(lambda i: ds_ref[hg * hb + i])
        gc = each(lambda i: _colsel(gc_t, hg * hb + i))
        gr = each(lambda i: _rowsel(gt_t, hg * hb + i))
        bc = each(lambda i: _colsel(be_t, heads + hg * hb + i))
        com = each(lambda i: _gdn_chunk_common(qkv_ref[0, :, sls[i]], qkv_ref[1, :, sls[i]], gc[i], gr[i], bc[i]))
        qh, rq, kn, rk, qn, dm, kk, m, pm = [[cm[j] for cm in com] for j in range(9)]
        tm = each(lambda i: tsave_ref[0, 0, i])
        s = each(lambda i: ssave_ref[0, 0, i])
        gam = each(lambda i: jnp.exp(gc[i]))
        glast = each(lambda i: gc[i][c - 1:c, :])
        gl = each(lambda i: jnp.exp(glast[i]))
        ratio = each(lambda i: jnp.exp(glast[i] - gc[i]))
        ks = each(lambda i: _bdot(kn[i], s[i]))
        qs = each(lambda i: _bdot(qn[i], s[i]))
        r = each(lambda i: qkv_ref[2, :, sls[i]] - gam[i] * ks[i])
        vn = each(lambda i: _bdot(tm[i], bc[i] * r[i]))
        o = each(lambda i: gam[i] * qs[i] + _bdot(pm[i], vn[i]))
        ro = each(lambda i: lax.rsqrt(jnp.mean(o[i] * o[i], axis=-1, keepdims=True) + NORM_EPS))
        zz = each(lambda i: z_ref[:, sls[i]])
        sz = each(lambda i: _sigmoid(zz[i]))
        dd = each(lambda i: do_ref[:, sls[i]])
        don = each(lambda i: dd[i] * (zz[i] * sz[i]))
        dz_h = each(lambda i: (dd[i] * (o[i] * ro[i] * gnw_v) * (sz[i] * (1.0 + zz[i] * (1.0 - sz[i])))).astype(BF16))
        dgnw = sum(each(lambda i: jnp.sum(don[i] * o[i] * ro[i], axis=0, keepdims=True)))
        uu = each(lambda i: don[i] * gnw_v)
        d_o = each(lambda i: ro[i] * uu[i] - o[i] * (ro[i] * ro[i] * ro[i]) * jnp.mean(o[i] * uu[i], axis=-1, keepdims=True))
        dqs = each(lambda i: gam[i] * d_o[i])
        dq = each(lambda i: _bdot(dqs[i], s[i], NT))
        ds_new = each(lambda i: _bdot(qn[i], dqs[i], TN))
        dp = each(lambda i: jnp.where(incl, _bdot(d_o[i], vn[i], NT), 0.0))
        dvn = each(lambda i: _bdot(pm[i], d_o[i], TN))
        dgam = each(lambda i: rsum(d_o[i] * qs[i]))
        dkd = each(lambda i: _bdot(vn[i], ds_in[i], NT))
        dvn = each(lambda i: dvn[i] + _bdot(kn[i] * ratio[i], ds_in[i]))
        ds_new = each(lambda i: ds_new[i] + gl[i] * ds_in[i])
        dgl = each(lambda i: jnp.sum(jnp.sum(ds_in[i] * s[i], axis=1, keepdims=True), axis=0, keepdims=True))
        dratio = each(lambda i: rsum(dkd[i] * kn[i]))
        dpd = each(lambda i: dp[i] * dm[i])
        dq = each(lambda i: dq[i] + _bdot(dpd[i], kn[i]))
        dk = each(lambda i: ratio[i] * dkd[i] + _bdot(dpd[i], qn[i], TN))
        dx = each(lambda i: _bdot(tm[i], dvn[i], TN))
        dr = each(lambda i: bc[i] * dx[i])
        gdr = each(lambda i: gam[i] * dr[i])
        dk = each(lambda i: dk[i] - _bdot(gdr[i], s[i], NT))
        ds_new = each(lambda i: ds_new[i] - _bdot(kn[i], gdr[i], TN))
        dmm = each(lambda i: jnp.where(strict, -_bdot(dx[i], vn[i], NT), 0.0))
        ee = each(lambda i: dmm[i] * dm[i])
        be_e = each(lambda i: bc[i] * ee[i])
        dk = each(lambda i: dk[i] + _bdot(be_e[i], kn[i]) + _bdot(be_e[i], kn[i], TN))
        dbeta = each(lambda i: rsum(dx[i] * r[i]) + rsum(ee[i] * kk[i]))
        dgam = each(lambda i: dgam[i] - rsum(dr[i] * ks[i]))
        ff = each(lambda i: dp[i] * pm[i] + dmm[i] * m[i])
        rowi = lax.broadcasted_iota(jnp.int32, (c, 1), 0)
        dgc = each(lambda i: rsum(ff[i]) - rsum(ff[i].T) + dgam[i] * gam[i] - dratio[i] * ratio[i]
                   + jnp.where(rowi == c - 1, jnp.sum(dratio[i] * ratio[i], axis=0, keepdims=True) + dgl[i] * gl[i], 0.0))
        dg_tile = sum(each(lambda i: _colput(dgc[i], hg * hb + i)))
        db_tile = sum(each(lambda i: _colput(dbeta[i], heads + hg * hb + i)))
        for i in hs:
            dqh = dq[i] * (hd ** -0.5)
            ds_ref[hg * hb + i] = ds_new[i]
            dz_ref[:, sls[i]] = dz_h[i]
            dact_ref[0, :, sls[i]] = rq[i] * (dqh - qh[i] * rsum(qh[i] * dqh))
            dact_ref[1, :, sls[i]] = rk[i] * (dk[i] - kn[i] * rsum(kn[i] * dk[i]))
            dact_ref[2, :, sls[i]] = dr[i]
        dgacc_ref[...] += dg_tile
        dbacc_ref[...] += db_tile
        dgnw_ref[0] += dgnw

        @pl.when(hg == ng - 1)
        def _():
            ab = ab_ref[...]
            ea = jnp.exp(alog_ref[...])
            g = -ea * _softplus(ab + dtb_ref[...])
            beta = be_ref[...]
            dg = _hdot(jnp.where(incl, 1.0, 0.0), dgacc_ref[...], TN)
            lane = lax.broadcasted_iota(jnp.int32, ab.shape, 1)
            da = jnp.where(lane < heads, dg * (-ea) * _sigmoid(ab + dtb_ref[...]), 0.0)
            db = dbacc_ref[...] * beta * (1.0 - beta)
            dab_ref[...] = (da + db).astype(BF16)
            dalog_ref[0] += jnp.sum(jnp.where(lane < heads, dg * g, 0.0), axis=0, keepdims=True)
            ddtb_ref[0] += jnp.sum(da, axis=0, keepdims=True)

    row = lambda b, n, g: b * nch + (nch - 1 - n)
    rev = lambda n: nch - 1 - n
    return _pcall(
        body, grid=(bsz, nch, ng),
        in_specs=[pl.BlockSpec((3, c, wb), lambda b, n, g: (0, row(b, n, g), g)),
                  pl.BlockSpec((c, wb), lambda b, n, g: (row(b, n, g), z_col0 // wb + g)),
                  pl.BlockSpec((c, LANE), lambda b, n, g: (row(b, n, g), ab_col0 // LANE)),
                  pl.BlockSpec((1, LANE), lambda b, n, g: (0, 0)),
                  pl.BlockSpec((1, LANE), lambda b, n, g: (0, 0)),
                  pl.BlockSpec((1, hd), lambda b, n, g: (0, 0)),
                  pl.BlockSpec((1, 1, hb, hd, hd), lambda b, n, g: (b, rev(n), g, 0, 0)),
                  pl.BlockSpec((1, 1, hb, c, c), lambda b, n, g: (b, rev(n), g, 0, 0)),
                  pl.BlockSpec((c, wb), lambda b, n, g: (row(b, n, g), g))],
        out_specs=[pl.BlockSpec((3, c, wb), lambda b, n, g: (0, row(b, n, g), g)),
                   pl.BlockSpec((c, wb), lambda b, n, g: (row(b, n, g), g)),
                   pl.BlockSpec((c, LANE), lambda b, n, g: (row(b, n, g), 0)),
                   pl.BlockSpec((1, 1, LANE), lambda b, n, g: (b, 0, 0)),
                   pl.BlockSpec((1, 1, LANE), lambda b, n, g: (b, 0, 0)),
                   pl.BlockSpec((1, 1, hd), lambda b, n, g: (b, 0, 0))],
        out_shape=[jax.ShapeDtypeStruct((3, bsz * seq, heads * hd), F32),
                   jax.ShapeDtypeStruct((bsz * seq, heads * hd), BF16),
                   jax.ShapeDtypeStruct((bsz * seq, LANE), BF16),
                   jax.ShapeDtypeStruct((bsz, 1, LANE), F32),
                   jax.ShapeDtypeStruct((bsz, 1, LANE), F32),
                   jax.ShapeDtypeStruct((bsz, 1, hd), F32)],
        scratch_shapes=[pltpu.VMEM((heads, hd, hd), F32), pltpu.VMEM((c, LANE), F32), pltpu.VMEM((LANE, c), F32),
                        pltpu.VMEM((c, LANE), F32), pltpu.VMEM((c, LANE), F32), pltpu.VMEM((c, LANE), F32)],
        semantics=("parallel", "arbitrary", "arbitrary"),
        operands=[qkv, proj, proj, alog, dtb, gnw, ssave, tsave, dout], name="gdn_bwd", comms=comms)


ELEMWISE_BLOCK_ELEMS = 256 * 1024


def _rows_tile(rows, cols):
    want = max(16, ELEMWISE_BLOCK_ELEMS // cols)
    if rows <= want:
        return rows
    t = (want // 16) * 16
    while t > 16 and rows % t:
        t -= 16
    return t if rows % t == 0 else rows


def _piece_specs(pieces, tr, cols):
    specs, leads = [], []
    for p, (arr, lead) in enumerate(pieces):
        if arr.ndim == 3:
            specs.append(pl.BlockSpec((1, tr, cols), functools.partial(lambda i, idx, p: (idx[p], i, 0), p=p)))
        else:
            specs.append(pl.BlockSpec((tr, cols), lambda i, idx: (i, 0)))
        leads.append(jnp.asarray(0 if lead is None else lead, jnp.int32))
    return jnp.stack(leads), specs


def _sum_pieces(refs):
    total = None
    for r in refs:
        v = r[...].astype(F32)
        v = v.reshape(v.shape[-2:])
        total = v if total is None else total + v
    return total


def _adamw(w, m, v, pieces, name, comms=()):
    rows, cols = w.shape
    tr = _rows_tile(rows, cols)
    leads, pspecs = _piece_specs(pieces, tr, cols)
    npc = len(pieces)
    c1 = 1.0 - ADAM_B1 ** ADAM_STEP
    c2 = 1.0 - ADAM_B2 ** ADAM_STEP

    def body(idx_ref, w_ref, m_ref, v_ref, *rest):
        g = _sum_pieces(rest[:npc])
        g_ref, d_ref, nm_ref, nv_ref = rest[npc:]
        nm = ADAM_B1 * m_ref[...] + (1.0 - ADAM_B1) * g
        nv = ADAM_B2 * v_ref[...] + (1.0 - ADAM_B2) * (g * g)
        g_ref[...] = g
        nm_ref[...] = nm
        nv_ref[...] = nv
        d_ref[...] = -ADAM_LR * ((nm / c1) / (jnp.sqrt(nv / c2) + ADAM_EPS) + ADAM_WD * w_ref[...])

    wspec = pl.BlockSpec((tr, cols), lambda i, idx: (i, 0))
    res, lands = _pcall(body, grid=(rows // tr,), in_specs=[wspec] * 3 + pspecs, out_specs=[wspec] * 4,
                        out_shape=[jax.ShapeDtypeStruct((rows, cols), F32)] * 4, semantics=("parallel",),
                        prefetch=[leads], operands=[w, m, v, *[p for p, _ in pieces]], name=name, comms=comms)
    return (res, lands) if comms else res


def _sum_to(pieces, out_dtype, name):
    arr0 = pieces[0][0]
    rows, cols = arr0.shape[-2:]
    tr = _rows_tile(rows, cols)
    leads, pspecs = _piece_specs(pieces, tr, cols)

    def body(idx_ref, *rest):
        rest[-1][...] = _sum_pieces(rest[:-1]).astype(out_dtype)

    return pl.pallas_call(
        body,
        grid_spec=pltpu.PrefetchScalarGridSpec(num_scalar_prefetch=1, grid=(rows // tr,), in_specs=pspecs,
                                               out_specs=pl.BlockSpec((tr, cols), lambda i, idx: (i, 0))),
        out_shape=jax.ShapeDtypeStruct((rows, cols), out_dtype),
        compiler_params=_params("parallel"), name=name)(leads, *[p for p, _ in pieces])


def _pair_add(a, recv, place, name, comms=()):
    _, rows, cols = a.shape
    tr = _rows_tile(rows, cols)
    x, y, c = place
    idx = jnp.stack([2 * (1 - x) + y, 2 * x + (1 - y), 2 * (1 - x) + (1 - y), c]).astype(jnp.int32)

    def body(p_ref, a_ref, r_ref, o_ref):
        o_ref[...] = (a_ref[...].astype(F32) + r_ref[...].astype(F32)).astype(BF16)

    res, lands = _pcall(
        body, grid=(3, rows // tr),
        in_specs=[pl.BlockSpec((1, tr, cols), lambda j, i, p: (2 * p[j] + p[3], i, 0)),
                  pl.BlockSpec((1, tr, cols), lambda j, i, p: (p[j], i, 0))],
        out_specs=[pl.BlockSpec((1, tr, cols), lambda j, i, p: (j, i, 0))],
        out_shape=[jax.ShapeDtypeStruct((3, rows, cols), BF16)], semantics=("parallel", "parallel"),
        prefetch=[idx], operands=[a, recv], name=name, comms=comms)
    return (res[0], lands) if comms else res[0]


def _to_frame(w, offs, fw, name):
    rows, n = w.shape
    tr = _tile(rows, 256)

    def body(off_ref, w_ref, o_ref, pad_ref):
        pad_ref[...] = jnp.zeros_like(pad_ref)
        pad_ref[:, 0:n] = w_ref[...]
        y = pad_ref[...]
        off1, len1, off2 = off_ref[0], off_ref[1], off_ref[2]
        col = lax.broadcasted_iota(jnp.int32, y.shape, 1)
        o_ref[0] = jnp.where(col < off1 + len1, pltpu.roll(y, off1, axis=1),
                             jnp.where(col >= off2 + len1, pltpu.roll(y, off2, axis=1), 0.0)).astype(BF16)

    res, _ = _pcall(body, grid=(rows // tr,), in_specs=[pl.BlockSpec((tr, n), lambda i, o: (i, 0))],
                    out_specs=[pl.BlockSpec((1, tr, fw), lambda i, o: (o[3], i, 0))],
                    out_shape=[jax.ShapeDtypeStruct((N_DEV, rows, fw), BF16)], scratch_shapes=[pltpu.VMEM((tr, fw), F32)],
                    semantics=("parallel",), prefetch=[offs], operands=[w], name=name)
    return res[0]


def _in_proj_core(xn, frames, ids, lay, into, name, comms=()):
    t, d = xn.shape
    n = ids.shape[0]
    tm = _tile(t, 1024)
    nb = max(b for b in range(1, lay.nc + 1) if lay.nc % b == 0 and b * MXU <= 768)
    tn, nj = nb * MXU, lay.nc // nb
    by_id = not isinstance(frames, (list, tuple))
    fr = [frames] if by_id else list(frames)
    n_fr = len(fr)

    def body(ids_ref, a_ref, *rest):
        o_ref = rest[-1]
        if by_id:
            o_ref[...] = jnp.dot(a_ref[...], rest[0][0], preferred_element_type=F32)
            return
        f = pl.program_id(1) // nj
        for k in range(n_fr):
            @pl.when(f == k)
            def _(k=k):
                o_ref[...] = jnp.dot(a_ref[...], rest[k][...], preferred_element_type=F32)

    core = lambda j: pl.multiple_of(lay.c0 * MXU + j * tn, MXU)
    col = lambda s, j: pl.multiple_of(lay.frame_block(s) * MXU + lay.c0 * MXU + j * tn, MXU)
    if by_id:
        b_specs = [pl.BlockSpec((pl.Element(1), pl.Element(d), pl.Element(tn)), lambda i, fj, ids: (ids[fj // nj], 0, core(fj % nj)))]
    else:
        b_specs = [pl.BlockSpec((pl.Element(d), pl.Element(tn)),
                                functools.partial(lambda i, fj, ids, k: (0, core(jnp.clip(fj - k * nj, 0, nj - 1))), k=k))
                   for k in range(n_fr)]
    n_into = 0 if into is None else 1
    res, lands = _pcall(
        body, grid=(t // tm, n * nj),
        in_specs=[pl.BlockSpec((tm, d), lambda i, fj, ids: (i, 0))] + b_specs + [pl.BlockSpec(memory_space=pl.ANY)] * n_into,
        out_specs=[pl.BlockSpec((pl.Element(tm), pl.Element(tn)), lambda i, fj, ids: (i * tm, col(ids[fj // nj], fj % nj)))],
        out_shape=[jax.ShapeDtypeStruct((t, lay.wp), F32)], semantics=("parallel", "arbitrary"),
        prefetch=[ids], operands=[xn] + fr + ([into] if n_into else []), name=name, comms=comms,
        fill={1 + n_fr: 0} if n_into else None)
    return res[0], lands


def _in_proj_rest(xn, frames, table, wp, into, name, comms=()):
    t, d = xn.shape
    tm = _tile(t, 1024)
    n = table.shape[1]

    def body(t_ref, a_ref, b1_ref, b2_ref, *rest):
        j = pl.program_id(1)
        b = b1_ref[0]
        b = jnp.where(t_ref[5, j] > 0, b + b2_ref[0], b)
        rest[-1][...] = jnp.dot(a_ref[...], b, preferred_element_type=F32)

    n_into = 0 if into is None else 1
    res, lands = _pcall(
        body, grid=(t // tm, n),
        in_specs=[pl.BlockSpec((tm, d), lambda i, j, tb: (i, 0)),
                  pl.BlockSpec((1, d, MXU), lambda i, j, tb: (tb[1, j], 0, tb[2, j])),
                  pl.BlockSpec((1, d, MXU), lambda i, j, tb: (tb[3, j], 0, tb[4, j]))]
        + [pl.BlockSpec(memory_space=pl.ANY)] * n_into,
        out_specs=[pl.BlockSpec((tm, MXU), lambda i, j, tb: (i, tb[0, j]))],
        out_shape=[jax.ShapeDtypeStruct((t, wp), F32)], semantics=("parallel", "arbitrary"),
        prefetch=[table], operands=[xn, frames, frames] + ([into] if n_into else []), name=name, comms=comms,
        fill={3: 0} if n_into else None)
    return res[0], lands


def _place():
    x, y, c = lax.axis_index("x"), lax.axis_index("y"), lax.axis_index("c")
    chips = [(1 - x, y), (x, 1 - y), (1 - x, 1 - y)]
    return x, y, c, chips


def _allreduce_small(buf, name):
    rows = buf.shape[0]

    def body(x_ref, o_ref, g_ref, send_sems, recv_sems):
        x, y, c, chips = _place()
        me, sibling = (x, y, c), (x, y, 1 - c)

        def copy(k, block, to, src=None):
            dst = g_ref.at[4 * block[0] + 2 * block[1] + block[2]]
            return pltpu.make_async_remote_copy(src_ref=dst if src is None else src, dst_ref=dst,
                                                send_sem=send_sems.at[k], recv_sem=recv_sems.at[k],
                                                device_id=to, device_id_type=MESH)

        first = [copy(0, me, sibling, src=x_ref)]
        first += [copy(1 + j, me, (*chip, c), src=x_ref) for j, chip in enumerate(chips)]
        for cp in first:
            cp.start()
        passed = [copy(4 + j, (*chip, c), sibling) for j, chip in enumerate(chips)]
        for j, chip in enumerate(chips):
            copy(1 + j, (*chip, c), me).wait_recv()
            passed[j].start()
        copy(0, sibling, me).wait_recv()
        for j, chip in enumerate(chips):
            copy(4 + j, (*chip, 1 - c), me).wait_recv()
        for cp in first + passed:
            cp.wait_send()
        g_ref[4 * x + 2 * y + c] = x_ref[...]
        total = g_ref[0]
        for s in range(1, N_DEV):
            total = total + g_ref[s]
        o_ref[...] = total

    vm = pl.BlockSpec(memory_space=pltpu.VMEM)
    return pl.pallas_call(
        body, in_specs=[vm], out_specs=vm, out_shape=jax.ShapeDtypeStruct((rows, LANE), F32),
        scratch_shapes=[pltpu.VMEM((N_DEV, rows, LANE), F32), pltpu.SemaphoreType.DMA((7,)), pltpu.SemaphoreType.DMA((7,))],
        name=name)(buf)


def _rows(ref, rows):
    return ref if rows is None else ref.at[pl.ds(rows[0], rows[1] - rows[0])]


AG_ALL = ("here", "sibling", 0, 1, 2)


def _ag(shard=None, into=None, to=(), forward=(), rows=None):
    def plan(srcs, lands):
        x, y, c, chips = _place()
        buf = lands[0]
        out = []
        if to:
            dst = _rows(buf.at[4 * x + 2 * y + c], rows)
            src = dst if shard is None else _rows(srcs[0], rows)
            for who in to:
                if who == "here":
                    out.append(("local", src, dst, None))
                elif who == "sibling":
                    out.append(("remote", src, dst, (x, y, 1 - c)))
                else:
                    out.append(("remote", src, dst, (*chips[who], c)))
        for j in forward:
            r = _rows(buf.at[4 * chips[j][0] + 2 * chips[j][1] + c], rows)
            out.append(("remote", r, r, (x, y, 1 - c)))
        return out

    srcs = ([shard] if to and shard is not None else []) + ([into] if into is not None else [])
    land = jax.ShapeDtypeStruct(into.shape, into.dtype) if into is not None else jax.ShapeDtypeStruct((N_DEV,) + shard.shape, shard.dtype)
    return _Comm(srcs, [land], plan, len(to) + len(forward), alias={len(srcs) - 1: 0} if into is not None else None)


def _ag_first(shard, rows=None, into=None, to=AG_ALL):
    return _ag(shard=shard, into=into, to=to, rows=rows)


def _ag_second(g, rows=None, of=(0, 1, 2)):
    return _ag(into=g, forward=of, rows=rows)


def _rs_first(grad):
    def plan(srcs, lands):
        x, y, c, _ = _place()
        (a,), (land,) = srcs, lands
        return [("remote", a.at[2 * j + (1 - c)], land.at[j], (x, y, 1 - c)) for j in range(4)]

    return _Comm([grad], [jax.ShapeDtypeStruct((4,) + grad.shape[1:], grad.dtype)], plan, 4)


def _rs_second(pair, rows=None, into=None):
    def plan(srcs, lands):
        x, y, c, chips = _place()
        return [("remote", _rows(srcs[0].at[j], rows), _rows(lands[0].at[j], rows), (cx, cy, c))
                for j, (cx, cy) in enumerate(chips)]

    land = jax.ShapeDtypeStruct((3,) + pair.shape[1:], pair.dtype)
    if into is None:
        return _Comm([pair], [land], plan, 3)
    return _Comm([pair, into], [land], plan, 3, alias={1: 0})


class _InLayout:
    def __init__(self, n_in, gw, heads, scw):
        self.n_in, self.split = n_in, 4 * gw + 2 * heads
        self.gap = LANE - 2 * heads
        self.ab_col, self.sc_col = 4 * gw, 4 * gw + LANE
        self.used = 4 * gw + LANE + 3 * scw
        p0 = [s * n_in + (self.gap if s * n_in >= self.split else 0) for s in range(N_DEV)]
        self.fstart = [(p // MXU) * MXU for p in p0]
        need = []
        for s in range(N_DEV):
            straddle = s * n_in < self.split < (s + 1) * n_in
            need.append(p0[s] - self.fstart[s] + n_in + (self.gap if straddle else 0))
        self.fw = -(-max(need) // MXU) * MXU
        self.wp = max(f + self.fw for f in self.fstart)
        assert self.wp >= self.used and self.wp % MXU == 0
        nfb = self.fw // MXU
        rows = []
        for jb in range(self.wp // MXU):
            src = [(s, jb - self.fstart[s] // MXU) for s in range(N_DEV) if 0 <= jb - self.fstart[s] // MXU < nfb]
            assert 1 <= len(src) <= 2, (jb, src)
            (s1, b1), (s2, b2) = src[0], src[-1]
            rows.append((s1, b1, s2, b2, int(len(src) == 2)))
        self.table = np.asarray(rows, np.int32).T.copy()
        single = [all(rows[self.fstart[s] // MXU + b][4] == 0 for s in range(N_DEV)) for b in range(nfb)]
        runs, b = [], 0
        while b < nfb:
            if single[b]:
                e = b
                while e < nfb and single[e]:
                    e += 1
                runs.append((e - b, b))
                b = e
            else:
                b += 1
        self.nc, self.c0 = max(runs) if runs else (0, 0)
        in_core = {self.fstart[s] // MXU + b for s in range(N_DEV) for b in range(self.c0, self.c0 + self.nc)}
        self.rest = np.asarray([(jb,) + rows[jb] for jb in range(self.wp // MXU) if jb not in in_core], np.int32).T.copy()

    def frame_block(self, s):
        p = s * self.n_in
        return (p + jnp.where(p >= self.split, self.gap, 0)) // MXU

    def offsets(self, s):
        p = s * self.n_in
        after = p >= self.split
        off1 = p + jnp.where(after, self.gap, 0) - self.frame_block(s) * MXU
        len1 = jnp.where(after, self.n_in, jnp.clip(self.split - p, 0, self.n_in))
        off2 = off1 + jnp.where(len1 < self.n_in, self.gap, 0)
        return off1, len1, off2

    def to_frame(self, w, s):
        return _to_frame(w, jnp.stack(self.offsets(s) + (s,)).astype(jnp.int32), self.fw, "w_in_frame")

    def from_frame(self, f, s):
        off1, len1, off2 = self.offsets(s)
        a = lax.dynamic_slice(f, (0, off1), (f.shape[0], self.n_in))
        b = lax.dynamic_slice(f, (0, off2), (f.shape[0], self.n_in))
        col = lax.broadcasted_iota(jnp.int32, (1, self.n_in), 1)
        return jnp.where(col < len1, a, b)


def _pack_rows(parts):
    rows = []
    for p in parts:
        flat = p.reshape(-1)
        pad = (-flat.shape[0]) % LANE
        rows.append(jnp.pad(flat, (0, pad)).reshape(-1, LANE))
    buf = jnp.concatenate(rows, axis=0)
    return jnp.pad(buf, ((0, (-buf.shape[0]) % 8), (0, 0)))


def _unpack_rows(buf, shapes):
    out, r = [], 0
    for shp in shapes:
        size = int(np.prod(shp))
        nr = -(-size // LANE)
        out.append(buf[r:r + nr].reshape(-1)[:size].reshape(shp))
        r += nr
    return out


def _pad_lanes(v):
    return jnp.pad(v, ((0, 0), (0, LANE - v.shape[1])))


def kernel(x, norm_mix_pre, w_in, conv_qkv_w, a_log, dt_bias, gdn_norm_w, conv_sc_w, w_out, norm_mix_post, norm_mlp_pre, w_up, w_down, norm_mlp_post, loss_target, m_norm_mix_pre, m_w_in, m_conv_qkv_w, m_a_log, m_dt_bias, m_gdn_norm_w, m_conv_sc_w, m_w_out, m_norm_mix_post, m_norm_mlp_pre, m_w_up, m_w_down, m_norm_mlp_post, v_norm_mix_pre, v_w_in, v_conv_qkv_w, v_a_log, v_dt_bias, v_gdn_norm_w, v_conv_sc_w, v_w_out, v_norm_mix_post, v_norm_mlp_pre, v_w_up, v_w_down, v_norm_mlp_post):
    bsz, seq, d = x.shape
    t = bsz * seq
    heads, hd = a_log.shape[-1], gdn_norm_w.shape[-1]
    gw = heads * hd
    scw = conv_sc_w.shape[-1] * N_DEV
    dff_w = w_up.shape[-1] * N_DEV
    lay = _InLayout(w_in.shape[-1], gw, heads, scw)
    mx, my, mc = lax.axis_index("x"), lax.axis_index("y"), lax.axis_index("c")
    me = 4 * mx + 2 * my + mc
    chip = 2 * mx + my

    x2 = x.reshape(t, d)
    tgt = loss_target.reshape(t, d)
    g1, g2, g3, g4 = norm_mix_pre, norm_mix_post, norm_mlp_pre, norm_mlp_post

    g_in = lay.to_frame(w_in[0], me)
    w_out_b, w_up_b, w_down_b = w_out[0].astype(BF16), w_up[0].astype(BF16), w_down[0].astype(BF16)
    up_cols = dff_w // N_DEV
    qu, qd = d // 4, up_cols // 4
    kq, ks = conv_qkv_w.shape[1], conv_sc_w.shape[1]
    cq_n, cs_n = conv_qkv_w.shape[-1], conv_sc_w.shape[-1]
    cq_full = lax.dynamic_update_slice(jnp.zeros((kq, 3 * gw), F32), conv_qkv_w[0], (0, me * cq_n))
    cs_full = lax.dynamic_update_slice(jnp.zeros((ks, scw), F32), conv_sc_w[0], (0, me * cs_n))
    conv_q, conv_s = _unpack_rows(_allreduce_small(_pack_rows([cq_full, cs_full]), "allgather_conv"),
                                  [(kq, 3 * gw), (ks, scw)])
    alog_t, dtb_t = _pad_lanes(a_log), _pad_lanes(dt_bias)

    xn, (g_in,) = _rms_fwd(x2, g1, comms=[_ag(into=g_in, to=("sibling",))])

    def arrived(k):
        return [lax.dynamic_index_in_dim(g_in, s, 0, keepdims=False) for s in ids[k]]

    dev = lambda px, py, pc: (4 * px + 2 * py + pc).astype(jnp.int32)
    ids = [jnp.stack([dev(mx, my, mc), dev(mx, my, 1 - mc)]),
           jnp.stack([dev(1 - mx, my, mc), dev(mx, 1 - my, mc)]),
           jnp.stack([dev(1 - mx, my, 1 - mc), dev(mx, 1 - my, 1 - mc)]),
           jnp.stack([dev(1 - mx, 1 - my, mc), dev(1 - mx, 1 - my, 1 - mc)])]
    assert lay.nc > 0, "the frames have no columns of their own at these sizes"
    proj, (g_in,) = _in_proj_core(xn, arrived(0), ids[0], lay, None, "in_proj_0", comms=[_ag(into=g_in, to=(0, 1))])
    proj, (g_in,) = _in_proj_core(xn, arrived(1), ids[1], lay, proj, "in_proj_1",
                                  comms=[_ag(into=g_in, to=(2,), forward=(0, 1))])
    proj, (g_in, g_out) = _in_proj_core(xn, arrived(2), ids[2], lay, proj, "in_proj_2",
                                        comms=[_ag(into=g_in, forward=(2,)), _ag_first(w_out_b)])
    eu = qu // 2
    proj, (g_up,) = _in_proj_core(xn, g_in, ids[3], lay, proj, "in_proj_3", comms=[_ag_first(w_up_b, rows=(0, eu))])
    proj, (g_up,) = _in_proj_rest(xn, g_in, jnp.asarray(lay.rest), lay.wp, proj, "in_proj_rest",
                                  comms=[_ag_first(w_up_b, rows=(eu, 2 * eu), into=g_up)])
    qkv, (g_out, g_up) = _qkvconv_fwd(proj, conv_q, bsz, seq, gw,
                                      comms=[_ag_second(g_out), _ag_first(w_up_b, rows=(2 * eu, 4 * eu), into=g_up)])
    (gdn_out, ssave, tsave), (g_up,) = _gdn_fwd(
        qkv, proj, alog_t, dtb_t, gdn_norm_w, bsz, seq, heads, 3 * gw, lay.ab_col,
        comms=[_ag_first(w_up_b, rows=(4 * eu, 8 * eu), into=g_up)])
    sc_out = _sc_fwd(proj, conv_s, bsz, seq, scw, lay.sc_col)
    mixed = jnp.concatenate([gdn_out, sc_out], axis=1)
    w_out_f = g_out.reshape(d, d)
    cuts = [(up_cols * c) // 64 for c in (0, 16, 22, 36, 50, 64)]
    (mix,), (g_up, g_down) = _matmul(mixed, w_out_f, mode="nn", out_dtypes=[F32], name="out_proj",
                                     comms=[_ag_second(g_up), _ag_first(w_down_b, rows=(cuts[0], cuts[1]))])
    (h, xn2), (g_down,) = _post1(x2, mix, g2, g3, comms=[_ag_first(w_down_b, rows=(cuts[1], cuts[2]), into=g_down)])

    def up_epilogue(acc):
        r = jnp.maximum(acc, 0.0)
        return r, r * r

    tq = t // 4
    act_hid = None
    for part in range(4):
        if part < 3:
            leg = [_ag_first(w_down_b, rows=(cuts[2 + part], cuts[3 + part]), into=g_down)]
        else:
            leg = [_ag_second(g_down)]
        act_hid, (g_down,) = _matmul(
            xn2, g_up, mode="nn", out_dtypes=[BF16, BF16], name="mlp_up_%d" % part, n_cols=dff_w, epilogue=up_epilogue,
            b_spec=lambda tk, tn: pl.BlockSpec((1, tk, tn), lambda i, j, k: (j // (up_cols // tn), k, j % (up_cols // tn))),
            a_rows=(part * tq, tq), out_into=act_hid, comms=leg)
    act, hid = act_hid
    w_down_f = g_down.reshape(dff_w, d)
    (ff,) = _matmul(hid, w_down_f, mode="nn", out_dtypes=[F32], name="mlp_down", tn=1024, tk=2048)
    dff, dy, dg4, loss_p = _post2_loss(h, ff, g4, tgt)

    def pieces(part, sib, got):
        return [(part, me), (sib, chip), (got, 0), (got, 1), (got, 2)]

    place = (mx, my, mc)

    (dpre,) = _matmul(dff, w_down_f, mode="nt", out_dtypes=[BF16], name="d_hidden", extras=[act],
                      epilogue=lambda acc, a: (acc * (2.0 * a.astype(F32)),))
    (dw_down,) = _matmul(hid, dff, mode="tn", out_dtypes=[BF16], name="dw_down")
    dw_down = dw_down.reshape(N_DEV, dff_w // N_DEV, d)
    (dxn2,), (sib_down,) = _matmul(
        dpre, g_up, mode="nt", out_dtypes=[F32], name="d_xn2", n_cols=d, tn=1024, tk=min(up_cols, 2048),
        b_spec=lambda tk, tn: pl.BlockSpec((1, tn, tk), lambda i, j, k: (k // (up_cols // tk), j, k % (up_cols // tk))),
        comms=[_rs_first(dw_down)])
    pair_down = _pair_add(dw_down, sib_down, place, "pair_add_down")
    (dw_up,), (got_down,) = _matmul(
        xn2, dpre, mode="tn", out_dtypes=[BF16], name="dw_up",
        out_custom=lambda tm, tn: ((N_DEV, d, up_cols), (1, tm, tn), lambda i, j, k: (j // (up_cols // tn), i, j % (up_cols // tn))),
        comms=[_rs_second(pair_down, rows=(0, 3 * qd))])
    (dmix, dh, dg2, dg3), (got_down, sib_up) = _mid_bwd(
        h, mix, dy, dxn2, g2, g3, comms=[_rs_second(pair_down, rows=(3 * qd, 4 * qd), into=got_down), _rs_first(dw_up)])
    pair_up = _pair_add(dw_up, sib_up, place, "pair_add_up")
    (dmixed,) = _matmul(dmix, w_out_f, mode="nt", out_dtypes=[F32], name="d_mixed")
    (dw_out,), (got_up,) = _matmul(mixed, dmix, mode="tn", out_dtypes=[BF16], name="dw_out",
                                   comms=[_rs_second(pair_up, rows=(0, qu))])
    dw_out = dw_out.reshape(N_DEV, d // N_DEV, d)
    (dscb, dscc, dsch, dconv_s), (sib_out,) = _sc_bwd(proj, conv_s, dmixed, bsz, seq, scw, lay.sc_col, gw,
                                                      comms=[_rs_first(dw_out)])
    pair_out = _pair_add(dw_out, sib_out, place, "pair_add_out")
    (dact, dz, dab, dalog, ddtb, dgnw), (got_up,) = _gdn_bwd(
        qkv, proj, alog_t, dtb_t, gdn_norm_w, ssave, tsave, dmixed, bsz, seq, heads, 3 * gw, lay.ab_col,
        comms=[_rs_second(pair_up, rows=(qu, 3 * qu), into=got_up)])
    (dqkv, dconv_q), (got_up,) = _qkvconv_bwd(proj, conv_q, dact, bsz, seq, gw,
                                              comms=[_rs_second(pair_up, rows=(3 * qu, 4 * qu), into=got_up)])
    dproj = jnp.concatenate([dqkv, dz, dab, dscb, dscc, dsch, jnp.zeros((t, lay.wp - lay.used), BF16)], axis=1)
    tn_in = _tile(lay.fw, 1024)
    nfb = lay.fw // tn_in
    hd2 = d // 2

    def dw_in_half(half, comms):
        return _matmul(
            xn, dproj, mode="tn", out_dtypes=[BF16], name="dw_in_%d" % half, n_cols=N_DEV * lay.fw, tn=tn_in,
            a_rows=(half * hd2, hd2),
            b_spec=lambda tk, tn: pl.BlockSpec(
                (pl.Element(tk), pl.Element(tn)),
                lambda i, j, k: (0, pl.multiple_of(lay.frame_block(j // nfb) * MXU + (j % nfb) * tn, LANE))),
            out_custom=lambda tm, tn: ((N_DEV, hd2, lay.fw), (1, tm, tn), lambda i, j, k: (j // nfb, i, j % nfb)),
            comms=comms)

    (dw_in_a,), (got_out,) = dw_in_half(0, [_rs_second(pair_out)])
    (dw_in_b,), (sib_a,) = dw_in_half(1, [_rs_first(dw_in_a)])
    pair_a = _pair_add(dw_in_a, sib_a, place, "pair_add_in_a")
    tk_in = _tile(lay.fw, 2048)
    kpf = lay.fw // tk_in

    def d_xn(part, into, comms):
        return _matmul(
            dproj, g_in, mode="nt", out_dtypes=[F32], name="d_xn_%d" % part, n_cols=d, tn=1024, tk=tk_in,
            k_total=N_DEV * lay.fw, a_rows=(part * tq, tq), out_into=None if into is None else [into], comms=comms,
            a_spec=lambda tm, tk, r0: pl.BlockSpec(
                (pl.Element(tm), pl.Element(tk)),
                lambda i, j, k: (pl.multiple_of(i * tm + r0, 16),
                                 pl.multiple_of(lay.frame_block(k // kpf) * MXU + (k % kpf) * tk, LANE))),
            b_spec=lambda tk, tn: pl.BlockSpec((1, tn, tk), lambda i, j, k: (k // kpf, j, k % kpf)))

    (dxn,), (got_a, sib_b) = d_xn(0, None, [_rs_second(pair_a, rows=(0, qu)), _rs_first(dw_in_b)])
    pair_b = _pair_add(dw_in_b, sib_b, place, "pair_add_in_b")
    (dxn,), (got_a,) = d_xn(1, dxn, [_rs_second(pair_a, rows=(qu, 2 * qu), into=got_a)])
    (dxn,), (got_b,) = d_xn(2, dxn, [_rs_second(pair_b, rows=(0, qu))])
    (dxn,), (got_b,) = d_xn(3, dxn, [_rs_second(pair_b, rows=(qu, 2 * qu), into=got_b)])
    grad_x, dg1 = _pre_bwd(x2, dh, dxn, g1)

    gin_frame = jnp.concatenate([_sum_to(pieces(dw_in_a, sib_a, got_a), F32, "grad_w_in_frame_a"),
                                 _sum_to(pieces(dw_in_b, sib_b, got_b), F32, "grad_w_in_frame_b")], axis=0)
    big = {
        "w_in": _adamw(w_in[0], m_w_in[0], v_w_in[0], [(lay.from_frame(gin_frame, me), None)], "adamw_w_in"),
        "w_out": _adamw(w_out[0], m_w_out[0], v_w_out[0], pieces(dw_out, sib_out, got_out), "adamw_w_out"),
        "w_up": _adamw(w_up[0], m_w_up[0], v_w_up[0], pieces(dw_up, sib_up, got_up), "adamw_w_up"),
        "w_down": _adamw(w_down[0], m_w_down[0], v_w_down[0], pieces(dw_down, sib_down, got_down), "adamw_w_down"),
    }

    small_shapes = [(kq, 3 * gw), (ks, scw), (1, d), (1, d), (1, d), (1, d), (1, LANE), (1, LANE), (1, hd), (1, LANE)]
    small = _unpack_rows(
        _allreduce_small(_pack_rows([dconv_q, dconv_s, dg1, dg2, dg3, dg4, jnp.sum(dalog, axis=0), jnp.sum(ddtb, axis=0),
                                     jnp.sum(dgnw, axis=0), loss_p]), "allreduce_small"), small_shapes)
    gq, gs, sg1, sg2, sg3, sg4, salog, sdtb, sgnw, sloss = small
    loss = sloss[0, 0]
    small_grads = {
        "norm_mix_pre": sg1, "conv_qkv_w": lax.dynamic_slice(gq, (0, me * cq_n), (kq, cq_n)),
        "a_log": salog[:, :heads], "dt_bias": sdtb[:, :heads], "gdn_norm_w": sgnw,
        "conv_sc_w": lax.dynamic_slice(gs, (0, me * cs_n), (ks, cs_n)),
        "norm_mix_post": sg2, "norm_mlp_pre": sg3, "norm_mlp_post": sg4,
    }
    weights = {"norm_mix_pre": (norm_mix_pre, m_norm_mix_pre, v_norm_mix_pre), "conv_qkv_w": (conv_qkv_w[0], m_conv_qkv_w[0], v_conv_qkv_w[0]),
               "a_log": (a_log, m_a_log, v_a_log), "dt_bias": (dt_bias, m_dt_bias, v_dt_bias),
               "gdn_norm_w": (gdn_norm_w, m_gdn_norm_w, v_gdn_norm_w), "conv_sc_w": (conv_sc_w[0], m_conv_sc_w[0], v_conv_sc_w[0]),
               "norm_mix_post": (norm_mix_post, m_norm_mix_post, v_norm_mix_post),
               "norm_mlp_pre": (norm_mlp_pre, m_norm_mlp_pre, v_norm_mlp_pre),
               "norm_mlp_post": (norm_mlp_post, m_norm_mlp_post, v_norm_mlp_post)}
    res = dict(big)
    for name, (w, m, v) in weights.items():
        res[name] = _adamw(w, m, v, [(small_grads[name], None)], "adamw_" + name)

    order = ["norm_mix_pre", "w_in", "conv_qkv_w", "a_log", "dt_bias", "gdn_norm_w", "conv_sc_w", "w_out", "norm_mix_post",
             "norm_mlp_pre", "w_up", "w_down", "norm_mlp_post"]
    shapes = {"norm_mix_pre": norm_mix_pre.shape, "w_in": w_in.shape, "conv_qkv_w": conv_qkv_w.shape, "a_log": a_log.shape,
              "dt_bias": dt_bias.shape, "gdn_norm_w": gdn_norm_w.shape, "conv_sc_w": conv_sc_w.shape, "w_out": w_out.shape,
              "norm_mix_post": norm_mix_post.shape, "norm_mlp_pre": norm_mlp_pre.shape, "w_up": w_up.shape,
              "w_down": w_down.shape, "norm_mlp_post": norm_mlp_post.shape}
    outs = [loss, grad_x.reshape(bsz, seq, d)]
    for part in range(4):
        outs += [res[nm][part].reshape(shapes[nm]) for nm in order]
    return tuple(outs)
```

```python
import functools
import math

import numpy as np
import jax
import jax.numpy as jnp
from jax import lax
from jax.experimental import pallas as pl
from jax.experimental.pallas import tpu as pltpu

F32 = jnp.float32
BF16 = jnp.bfloat16
HI = lax.Precision.HIGHEST
MESH = pl.DeviceIdType.MESH

N_DEV = 8
LANE = 128
MXU = 256
CHUNK = 64
NORM_EPS = 1e-6
L2_EPS = 1e-6
VMEM_LIMIT = 56 * 1024 * 1024

ADAM_LR = 0.001
ADAM_B1 = 0.9
ADAM_B2 = 0.999
ADAM_EPS = 1e-08
ADAM_WD = 0.01
ADAM_STEP = 10

NN = (((1,), (0,)), ((), ()))
NT = (((1,), (1,)), ((), ()))
TN = (((0,), (0,)), ((), ()))


def _params(*sem):
    return pltpu.CompilerParams(dimension_semantics=sem, vmem_limit_bytes=VMEM_LIMIT)


def _tile(n, want):
    if n <= want:
        return n
    t = (want // LANE) * LANE
    while t > LANE and n % t:
        t -= LANE
    assert n % t == 0, (n, want)
    return t


class _Comm:
    def __init__(self, srcs, lands, plan, n, alias=None):
        self.srcs, self.lands, self.plan, self.n, self.alias = list(srcs), list(lands), plan, n, dict(alias or {})


def _pcall(body, *, grid, in_specs, out_specs, out_shape, operands, name, scratch_shapes=(), semantics=None,
           prefetch=(), comms=(), fill=None):
    n_pf, n_in, n_out, n_scr = len(prefetch), len(in_specs), len(out_specs), len(scratch_shapes)
    srcs = [s for cm in comms for s in cm.srcs]
    lands = [l for cm in comms for l in cm.lands]
    n_src, n_land = len(srcs), len(lands)
    n_copies = sum(cm.n for cm in comms)
    aliases, so, lo = {n_pf + a: b for a, b in (fill or {}).items()}, 0, 0
    for cm in comms:
        for a, b in cm.alias.items():
            aliases[n_pf + n_in + so + a] = n_out + lo + b
        so, lo = so + len(cm.srcs), lo + len(cm.lands)
    any_spec = pl.BlockSpec(memory_space=pl.ANY)

    def wrapped(*refs):
        pf, r = refs[:n_pf], refs[n_pf:]
        ins, csrc = r[:n_in], r[n_in:n_in + n_src]
        outs = r[n_in + n_src:n_in + n_src + n_out]
        cland = r[n_in + n_src + n_out:n_in + n_src + n_out + n_land]
        rest = r[n_in + n_src + n_out + n_land:]
        scratch = rest[:n_scr]
        if not comms:
            body(*pf, *ins, *outs, *scratch)
            return
        send_sems, recv_sems = rest[n_scr:]

        def copies():
            out, k, s0, l0 = [], 0, 0, 0
            for cm in comms:
                for kind, src, dst, dev in cm.plan(csrc[s0:s0 + len(cm.srcs)], cland[l0:l0 + len(cm.lands)]):
                    if kind == "local":
                        out.append((kind, pltpu.make_async_copy(src, dst, send_sems.at[k])))
                    else:
                        out.append((kind, pltpu.make_async_remote_copy(
                            src_ref=src, dst_ref=dst, send_sem=send_sems.at[k], recv_sem=recv_sems.at[k],
                            device_id=dev, device_id_type=MESH)))
                    k += 1
                s0, l0 = s0 + len(cm.srcs), l0 + len(cm.lands)
            assert k == n_copies
            return out

        ids = [pl.program_id(a) for a in range(len(grid))]
        first = functools.reduce(jnp.logical_and, [i == 0 for i in ids])
        last = functools.reduce(jnp.logical_and, [i == g - 1 for i, g in zip(ids, grid)])

        @pl.when(first)
        def _():
            for _, cp in copies():
                cp.start()

        body(*pf, *ins, *outs, *scratch)

        @pl.when(last)
        def _():
            cps = copies()
            for kind, cp in cps:
                if kind == "remote":
                    cp.wait_recv()
            for kind, cp in cps:
                if kind == "remote":
                    cp.wait_send()
                else:
                    cp.wait()

    sems = [pltpu.SemaphoreType.DMA((n_copies,)), pltpu.SemaphoreType.DMA((n_copies,))] if comms else []
    if semantics is None or comms:
        semantics = ("arbitrary",) * len(grid)
    res = pl.pallas_call(
        wrapped,
        grid_spec=pltpu.PrefetchScalarGridSpec(
            num_scalar_prefetch=n_pf, grid=tuple(grid), in_specs=list(in_specs) + [any_spec] * n_src,
            out_specs=list(out_specs) + [any_spec] * n_land, scratch_shapes=list(scratch_shapes) + sems),
        out_shape=list(out_shape) + lands,
        input_output_aliases=aliases,
        compiler_params=_params(*semantics), name=name)(*prefetch, *operands, *srcs)
    return list(res[:n_out]), list(res[n_out:])


def _bdot(a, b, dims=NN):
    return lax.dot_general(a.astype(BF16), b.astype(BF16), dims, preferred_element_type=F32)


def _hdot(a, b, dims=NN):
    return lax.dot_general(a, b, dims, preferred_element_type=F32, precision=HI)


def _mdot(a, b, dims=NN):
    return lax.dot_general(a, b, dims, preferred_element_type=F32, precision=lax.Precision.HIGH)


def _sigmoid(x):
    return 1.0 / (1.0 + jnp.exp(-x))


def _softplus(x):
    return jnp.maximum(x, 0.0) + jnp.log(1.0 + jnp.exp(-jnp.abs(x)))


def _matmul(a, b, *, mode, out_dtypes, name, n_cols=None, tm=1024, tn=512, tk=4096, epilogue=None, extras=(),
            b_spec=None, out_custom=None, a_rows=None, out_into=None, a_spec=None, k_total=None, comms=()):
    if mode == "tn":
        K, M = a.shape
    else:
        M, K = a.shape
    if k_total is not None:
        K = k_total
    r0 = 0
    if a_rows is not None:
        r0, M = a_rows
    N = n_cols if n_cols is not None else (b.shape[0] if mode == "nt" else b.shape[1])
    tm, tk, tn = _tile(M, tm), _tile(K, tk), _tile(N, tn)
    assert r0 % tm == 0
    i0 = r0 // tm
    if b_spec is None:
        b_spec = pl.BlockSpec((tn, tk), lambda i, j, k: (j, k)) if mode == "nt" else pl.BlockSpec((tk, tn), lambda i, j, k: (k, j))
    else:
        b_spec = b_spec(tk, tn)
    gm, gn, nk = M // tm, N // tn, K // tk
    if out_custom is not None:
        shape, blk, ix = out_custom(tm, tn)
        out_shapes, out_blocks, out_index = [shape] * len(out_dtypes), [blk] * len(out_dtypes), [ix] * len(out_dtypes)
    elif a_rows is not None:
        out_shapes = [(a.shape[0], N)] * len(out_dtypes)
        out_blocks = [(tm, tn)] * len(out_dtypes)
        out_index = [lambda i, j, k: (i + i0, j)] * len(out_dtypes)
    else:
        out_shapes = [(M, N)] * len(out_dtypes)
        out_blocks = [(tm, tn)] * len(out_dtypes)
        out_index = [lambda i, j, k: (i, j)] * len(out_dtypes)
    if a_spec is not None:
        a_spec = a_spec(tm, tk, r0)
    elif mode == "tn":
        a_spec = pl.BlockSpec((tk, tm), lambda i, j, k: (k, i + i0))
    else:
        a_spec = pl.BlockSpec((tm, tk), lambda i, j, k: (i + i0, k))
    hoist = mode == "tn" and nk == 1 and gn > 1
    dims = {"nn": NN, "nt": NT, "tn": TN}[mode]
    n_ex, n_out = len(extras), len(out_dtypes)
    out_into = [] if out_into is None else list(out_into)
    n_into = len(out_into)
    assert n_into in (0, n_out)

    def body(a_ref, b_ref, *rest):
        ex, outs = rest[:n_ex], rest[n_ex + n_into:n_ex + n_into + n_out]

        def finish(acc):
            res = epilogue(acc, *[e[...] for e in ex]) if epilogue is not None else (acc,)
            for o, r in zip(outs, res):
                o[...] = r.reshape(o.shape).astype(o.dtype)

        bb = b_ref[...]
        bb = bb.reshape(bb.shape[-2:])
        if hoist:
            at_ref = rest[-1]

            @pl.when(pl.program_id(1) == 0)
            def _():
                at_ref[...] = a_ref[...].T

            finish(lax.dot_general(at_ref[...], bb, NN, preferred_element_type=F32))
            return
        part = lax.dot_general(a_ref[...], bb, dims, preferred_element_type=F32)
        if nk == 1:
            finish(part)
        else:
            acc = rest[-1]
            k = pl.program_id(2)

            @pl.when(k == 0)
            def _():
                acc[...] = part

            @pl.when(k > 0)
            def _():
                acc[...] += part

            @pl.when(k == nk - 1)
            def _():
                finish(acc[...])

    scratch = [pltpu.VMEM((tm, tk), BF16)] if hoist else ([pltpu.VMEM((tm, tn), F32)] if nk > 1 else [])
    outs, lands = _pcall(
        body, grid=(gm, gn, nk),
        in_specs=([a_spec, b_spec] + [pl.BlockSpec((tm, tn), lambda i, j, k: (i, j)) for _ in extras]
                  + [pl.BlockSpec(memory_space=pl.ANY)] * n_into),
        out_specs=[pl.BlockSpec(blk, ix) for blk, ix in zip(out_blocks, out_index)],
        out_shape=[jax.ShapeDtypeStruct(s, d) for s, d in zip(out_shapes, out_dtypes)],
        scratch_shapes=scratch, semantics=("parallel", "arbitrary", "arbitrary"),
        operands=[a, b, *extras, *out_into], name=name, comms=comms,
        fill={2 + n_ex + o: o for o in range(n_into)})
    return (outs, lands) if comms else outs


TR = 256
QKV_CONV_COLS = 256


def _rms(x):
    return lax.rsqrt(jnp.mean(x * x, axis=-1, keepdims=True) + NORM_EPS)


def _rms_bwd(x, r, w, dy):
    u = dy * w
    dx = r * u - x * (r * r * r) * jnp.mean(x * u, axis=-1, keepdims=True)
    return dx, dy * x * r


def _row_call(body, ins, row_flags, outs, name, n_rows, comms=()):
    n_row = sum(row_flags) + sum(kind == "row" for _, _, kind in outs)
    tr = min(TR if n_row <= 5 else TR // 2, n_rows)
    in_specs = []
    for arr, is_row in zip(ins, row_flags):
        if is_row:
            in_specs.append(pl.BlockSpec((tr, arr.shape[1]), lambda i: (i, 0)))
        else:
            in_specs.append(pl.BlockSpec(arr.shape, lambda i: (0, 0)))
    out_specs, out_shape = [], []
    for shape, dtype, kind in outs:
        if kind == "row":
            out_specs.append(pl.BlockSpec((tr, shape[1]), lambda i: (i, 0)))
        else:
            out_specs.append(pl.BlockSpec(shape, lambda i: (0, 0)))
        out_shape.append(jax.ShapeDtypeStruct(shape, dtype))
    res, lands = _pcall(body, grid=(n_rows // tr,), in_specs=in_specs, out_specs=out_specs, out_shape=out_shape,
                        operands=list(ins), name=name, comms=comms)
    return (res, lands) if comms else res


def _acc_out(ref, val):
    @pl.when(pl.program_id(0) == 0)
    def _():
        ref[...] = val

    @pl.when(pl.program_id(0) > 0)
    def _():
        ref[...] += val


def _rms_fwd(x, g, comms):
    T, D = x.shape

    def body(x_ref, g_ref, o_ref):
        xv = x_ref[...]
        o_ref[...] = (xv * _rms(xv) * g_ref[...]).astype(BF16)

    res, lands = _row_call(body, [x, g], [True, False], [((T, D), BF16, "row")], "rms_fwd", T, comms=comms)
    return res[0], lands


def _post1(x, mix, g2, g3, comms=()):
    T, D = x.shape

    def body(x_ref, mix_ref, g2_ref, g3_ref, h_ref, xn2_ref):
        mv = mix_ref[...]
        h = x_ref[...] + mv * _rms(mv) * g2_ref[...]
        h_ref[...] = h
        xn2_ref[...] = (h * _rms(h) * g3_ref[...]).astype(BF16)

    return _row_call(body, [x, mix, g2, g3], [True, True, False, False],
                     [((T, D), F32, "row"), ((T, D), BF16, "row")], "post1", T, comms=comms)


def _post2_loss(h, ff, g4, target):
    T, D = h.shape

    def body(h_ref, ff_ref, g4_ref, t_ref, dff_ref, dy_ref, dg4_ref, loss_ref):
        fv = ff_ref[...]
        r = _rms(fv)
        err = h_ref[...] + fv * r * g4_ref[...] - t_ref[...]
        dy = err * (1.0 / D)
        dy_ref[...] = dy
        dff, dwt = _rms_bwd(fv, r, g4_ref[...], dy)
        dff_ref[...] = dff.astype(BF16)
        _acc_out(dg4_ref, jnp.sum(dwt, axis=0, keepdims=True))
        part = 0.5 * jnp.sum(jnp.mean(err * err, axis=-1, keepdims=True), axis=0, keepdims=True)
        _acc_out(loss_ref, jnp.broadcast_to(part, (1, LANE)))

    return _row_call(body, [h, ff, g4, target], [True, True, False, True],
                     [((T, D), BF16, "row"), ((T, D), F32, "row"), ((1, D), F32, "acc"), ((1, LANE), F32, "acc")],
                     "post2_loss", T)


def _mid_bwd(h, mix, dy, dxn2, g2, g3, comms=()):
    T, D = h.shape

    def body(h_ref, mix_ref, dy_ref, dxn2_ref, g2_ref, g3_ref, dmix_ref, dh_ref, dg2_ref, dg3_ref):
        hv = h_ref[...]
        d1, dw3 = _rms_bwd(hv, _rms(hv), g3_ref[...], dxn2_ref[...])
        dh = dy_ref[...] + d1
        dh_ref[...] = dh
        mv = mix_ref[...]
        dmix, dw2 = _rms_bwd(mv, _rms(mv), g2_ref[...], dh)
        dmix_ref[...] = dmix.astype(BF16)
        _acc_out(dg2_ref, jnp.sum(dw2, axis=0, keepdims=True))
        _acc_out(dg3_ref, jnp.sum(dw3, axis=0, keepdims=True))

    return _row_call(body, [h, mix, dy, dxn2, g2, g3], [True, True, True, True, False, False],
                     [((T, D), BF16, "row"), ((T, D), F32, "row"), ((1, D), F32, "acc"), ((1, D), F32, "acc")],
                     "mid_bwd", T, comms=comms)


def _pre_bwd(x, dh, dxn, g1, comms=()):
    T, D = x.shape

    def body(x_ref, dh_ref, dxn_ref, g1_ref, gx_ref, dg1_ref):
        xv = x_ref[...]
        d1, dw1 = _rms_bwd(xv, _rms(xv), g1_ref[...], dxn_ref[...])
        gx_ref[...] = dh_ref[...] + d1
        _acc_out(dg1_ref, jnp.sum(dw1, axis=0, keepdims=True))

    return _row_call(body, [x, dh, dxn, g1], [True, True, True, False],
                     [((T, D), F32, "row"), ((1, D), F32, "acc")], "pre_bwd", T, comms=comms)


def _shift_down(x, s):
    if s == 0:
        return x
    row = lax.broadcasted_iota(jnp.int32, x.shape, 0)
    return jnp.where(row >= s, pltpu.roll(x, s, axis=0), 0.0)


def _shift_up(x, s):
    if s == 0:
        return x
    n = x.shape[0]
    row = lax.broadcasted_iota(jnp.int32, x.shape, 0)
    return jnp.where(row < n - s, pltpu.roll(x, n - s, axis=0), 0.0)


def _conv(x, w):
    kw = w.shape[0]
    out = w[kw - 1:kw, :] * x
    for j in range(kw - 1):
        out = out + w[j:j + 1, :] * _shift_down(x, kw - 1 - j)
    return out


def _conv_bwd(x, w, dout):
    kw = w.shape[0]
    dx = w[kw - 1:kw, :] * dout
    dws = []
    for j in range(kw - 1):
        dx = dx + w[j:j + 1, :] * _shift_up(dout, kw - 1 - j)
        dws.append(jnp.sum(dout * _shift_down(x, kw - 1 - j), axis=0, keepdims=True))
    dws.append(jnp.sum(dout * x, axis=0, keepdims=True))
    return dx, jnp.concatenate(dws, axis=0)


def _qkvconv_fwd(proj, w, bsz, seq, gw, comms=()):
    cw = QKV_CONV_COLS
    nct = gw // cw
    kw = w.shape[0]

    def body(p_ref, w_ref, o_ref):
        cv = _conv(p_ref[...], w_ref[...])
        o_ref[...] = (cv * _sigmoid(cv)).reshape(o_ref.shape)

    res, lands = _pcall(
        body, grid=(3, bsz, nct),
        in_specs=[pl.BlockSpec((seq, cw), lambda p, b, c: (b, p * nct + c)),
                  pl.BlockSpec((kw, cw), lambda p, b, c: (0, p * nct + c))],
        out_specs=[pl.BlockSpec((1, seq, cw), lambda p, b, c: (p, b, c))],
        out_shape=[jax.ShapeDtypeStruct((3, bsz * seq, gw), F32)],
        semantics=("parallel", "parallel", "parallel"), operands=[proj, w], name="qkvconv_fwd", comms=comms)
    return res[0], lands


def _qkvconv_bwd(proj, w, dact, bsz, seq, gw, comms=()):
    cw = QKV_CONV_COLS
    nct = gw // cw
    kw = w.shape[0]

    def body(p_ref, w_ref, d_ref, dp_ref, dw_ref):
        pre = p_ref[...]
        wv = w_ref[...]
        cv = _conv(pre, wv)
        sg = _sigmoid(cv)
        dcv = d_ref[...].reshape(cv.shape) * (sg * (1.0 + cv * (1.0 - sg)))
        dpre, dw = _conv_bwd(pre, wv, dcv)
        dp_ref[...] = dpre.astype(BF16)
        b = pl.program_id(2)

        @pl.when(b == 0)
        def _():
            dw_ref[...] = dw

        @pl.when(b > 0)
        def _():
            dw_ref[...] += dw

    res, lands = _pcall(
        body, grid=(3, nct, bsz),
        in_specs=[pl.BlockSpec((seq, cw), lambda p, c, b: (b, p * nct + c)),
                  pl.BlockSpec((kw, cw), lambda p, c, b: (0, p * nct + c)),
                  pl.BlockSpec((1, seq, cw), lambda p, c, b: (p, b, c))],
        out_specs=[pl.BlockSpec((seq, cw), lambda p, c, b: (b, p * nct + c)),
                   pl.BlockSpec((kw, cw), lambda p, c, b: (0, p * nct + c))],
        out_shape=[jax.ShapeDtypeStruct((bsz * seq, 3 * gw), BF16), jax.ShapeDtypeStruct((kw, 3 * gw), F32)],
        semantics=("parallel", "parallel", "arbitrary"), operands=[proj, w, dact], name="qkvconv_bwd", comms=comms)
    return res, lands


def _sc_fwd(proj, w, bsz, seq, scw, col0):
    nct = scw // LANE
    c0 = col0 // LANE
    kw = w.shape[0]

    def body(b_ref, c_ref, h_ref, w_ref, o_ref):
        o_ref[...] = (b_ref[...] * _conv(c_ref[...] * h_ref[...], w_ref[...])).astype(BF16)

    return pl.pallas_call(
        body, grid=(bsz, nct),
        in_specs=[pl.BlockSpec((seq, LANE), lambda b, c: (b, c0 + c)),
                  pl.BlockSpec((seq, LANE), lambda b, c: (b, c0 + nct + c)),
                  pl.BlockSpec((seq, LANE), lambda b, c: (b, c0 + 2 * nct + c)),
                  pl.BlockSpec((kw, LANE), lambda b, c: (0, c))],
        out_specs=pl.BlockSpec((seq, LANE), lambda b, c: (b, c)),
        out_shape=jax.ShapeDtypeStruct((bsz * seq, scw), BF16),
        compiler_params=_params("parallel", "parallel"), name="sc_fwd")(proj, proj, proj, w)


def _sc_bwd(proj, w, dout, bsz, seq, scw, col0, dcol0, comms=()):
    nct = scw // LANE
    c0 = col0 // LANE
    d0 = dcol0 // LANE
    kw = w.shape[0]

    def body(b_ref, c_ref, h_ref, w_ref, d_ref, db_ref, dc_ref, dh_ref, dw_ref):
        cc, hh, wv, dv = c_ref[...], h_ref[...], w_ref[...], d_ref[...]
        m = cc * hh
        db_ref[...] = (dv * _conv(m, wv)).astype(BF16)
        dm, dw = _conv_bwd(m, wv, dv * b_ref[...])
        dc_ref[...] = (dm * hh).astype(BF16)
        dh_ref[...] = (dm * cc).astype(BF16)
        b = pl.program_id(1)

        @pl.when(b == 0)
        def _():
            dw_ref[...] = dw

        @pl.when(b > 0)
        def _():
            dw_ref[...] += dw

    res, lands = _pcall(
        body, grid=(nct, bsz),
        in_specs=[pl.BlockSpec((seq, LANE), lambda c, b: (b, c0 + c)),
                  pl.BlockSpec((seq, LANE), lambda c, b: (b, c0 + nct + c)),
                  pl.BlockSpec((seq, LANE), lambda c, b: (b, c0 + 2 * nct + c)),
                  pl.BlockSpec((kw, LANE), lambda c, b: (0, c)),
                  pl.BlockSpec((seq, LANE), lambda c, b: (b, d0 + c))],
        out_specs=[pl.BlockSpec((seq, LANE), lambda c, b: (b, c)),
                   pl.BlockSpec((seq, LANE), lambda c, b: (b, c)),
                   pl.BlockSpec((seq, LANE), lambda c, b: (b, c)),
                   pl.BlockSpec((kw, LANE), lambda c, b: (0, c))],
        out_shape=[jax.ShapeDtypeStruct((bsz * seq, scw), BF16)] * 3 + [jax.ShapeDtypeStruct((kw, scw), F32)],
        semantics=("parallel", "arbitrary"), operands=[proj, proj, proj, w, dout], name="sc_bwd", comms=comms)
    return res, lands


HEADS_PER_STEP = 16


def _colsel(tile, idx):
    lane = lax.broadcasted_iota(jnp.int32, tile.shape, 1)
    return jnp.sum(jnp.where(lane == idx, tile, 0.0), axis=1, keepdims=True)


def _rowsel(tile, idx):
    row = lax.broadcasted_iota(jnp.int32, tile.shape, 0)
    return jnp.sum(jnp.where(row == idx, tile, 0.0), axis=0, keepdims=True)


def _colput(col, idx, width=LANE):
    lane = lax.broadcasted_iota(jnp.int32, (col.shape[0], width), 1)
    return jnp.where(lane == idx, col, 0.0)


def _tri_masks(c):
    row = lax.broadcasted_iota(jnp.int32, (c, c), 0)
    col = lax.broadcasted_iota(jnp.int32, (c, c), 1)
    return row >= col, row > col, row == col


def _unit_lower_inverses(ms):
    c = ms[0].shape[0]
    _, _, eye = _tri_masks(c)
    ps = [-m for m in ms]
    ts = [jnp.where(eye, 1.0, 0.0) + p for p in ps]
    for _ in range(int(math.log2(c)) - 1):
        ps = [_mdot(p, p) for p in ps]
        ts = [t + _mdot(t, p) for t, p in zip(ts, ps)]
    return ts


def _gates(ab, alog, dtb):
    g = -jnp.exp(alog) * _softplus(ab + dtb)
    return g, _sigmoid(ab)


def _l2n(x):
    r = lax.rsqrt(jnp.sum(x * x, axis=-1, keepdims=True) + L2_EPS)
    return x * r, r


def _gdn_chunk_common(q, k, gc, gr, bc):
    c, dk = q.shape
    incl, strict, _ = _tri_masks(c)
    qh, rq = _l2n(q)
    kn, rk = _l2n(k)
    qn = qh * (dk ** -0.5)
    dm = jnp.where(incl, jnp.exp(jnp.where(incl, gc - gr, 0.0)), 0.0)
    kk = _bdot(kn, kn, NT)
    m = jnp.where(strict, bc * kk * dm, 0.0)
    pm = jnp.where(incl, _bdot(qn, kn, NT) * dm, 0.0)
    return qh, rq, kn, rk, qn, dm, kk, m, pm


def _gdn_fwd(qkv, proj, alog, dtb, gnw, bsz, seq, heads, z_col0, ab_col0, comms=()):
    c = CHUNK
    nch = seq // c
    hb = min(HEADS_PER_STEP, heads)
    ng = heads // hb
    hd = qkv.shape[2] // heads
    wb = hb * hd

    def body(qkv_ref, z_ref, ab_ref, alog_ref, dtb_ref, gnw_ref, o_ref, ssave_ref, tsave_ref, s_ref, gc_ref, gt_ref, be_ref):
        n, hg = pl.program_id(1), pl.program_id(2)

        @pl.when((n == 0) & (hg == 0))
        def _():
            s_ref[...] = jnp.zeros_like(s_ref)

        @pl.when(hg == 0)
        def _():
            g, beta = _gates(ab_ref[...], alog_ref[...], dtb_ref[...])
            incl, _, _ = _tri_masks(c)
            gcum = _hdot(jnp.where(incl, 1.0, 0.0), g)
            gc_ref[...] = gcum
            gt_ref[...] = gcum.T
            be_ref[...] = beta

        gc_t, gt_t, be_t, gnw_v = gc_ref[...], gt_ref[...], be_ref[...], gnw_ref[...]
        hs = range(hb)
        sls = [slice(hh * hd, (hh + 1) * hd) for hh in hs]
        states = [s_ref[hg * hb + hh] for hh in hs]
        gcs = [_colsel(gc_t, hg * hb + hh) for hh in hs]
        grs = [_rowsel(gt_t, hg * hb + hh) for hh in hs]
        bcs = [_colsel(be_t, heads + hg * hb + hh) for hh in hs]
        com = [_gdn_chunk_common(qkv_ref[0, :, sls[hh]], qkv_ref[1, :, sls[hh]], gcs[hh], grs[hh], bcs[hh]) for hh in hs]
        kns, qns, pms = [cm[2] for cm in com], [cm[4] for cm in com], [cm[8] for cm in com]
        tms = _unit_lower_inverses([cm[7] for cm in com])
        gams = [jnp.exp(gc) for gc in gcs]
        glasts = [gc[c - 1:c, :] for gc in gcs]
        kss = [_bdot(kns[hh], states[hh]) for hh in hs]
        qss = [_bdot(qns[hh], states[hh]) for hh in hs]
        vns = [_bdot(tms[hh], bcs[hh] * (qkv_ref[2, :, sls[hh]] - gams[hh] * kss[hh])) for hh in hs]
        os_ = [gams[hh] * qss[hh] + _bdot(pms[hh], vns[hh]) for hh in hs]
        snews = [states[hh] * jnp.exp(glasts[hh]) + _bdot(kns[hh] * jnp.exp(glasts[hh] - gcs[hh]), vns[hh], TN) for hh in hs]
        for hh in hs:
            o = os_[hh]
            on = o * lax.rsqrt(jnp.mean(o * o, axis=-1, keepdims=True) + NORM_EPS) * gnw_v
            zz = z_ref[:, sls[hh]]
            ssave_ref[0, 0, hh] = states[hh]
            tsave_ref[0, 0, hh] = tms[hh]
            s_ref[hg * hb + hh] = snews[hh]
            o_ref[:, sls[hh]] = (on * (zz * _sigmoid(zz))).astype(BF16)

    row = lambda b, n, g: b * nch + n
    return _pcall(
        body, grid=(bsz, nch, ng),
        in_specs=[pl.BlockSpec((3, c, wb), lambda b, n, g: (0, row(b, n, g), g)),
                  pl.BlockSpec((c, wb), lambda b, n, g: (row(b, n, g), z_col0 // wb + g)),
                  pl.BlockSpec((c, LANE), lambda b, n, g: (row(b, n, g), ab_col0 // LANE)),
                  pl.BlockSpec((1, LANE), lambda b, n, g: (0, 0)),
                  pl.BlockSpec((1, LANE), lambda b, n, g: (0, 0)),
                  pl.BlockSpec((1, hd), lambda b, n, g: (0, 0))],
        out_specs=[pl.BlockSpec((c, wb), lambda b, n, g: (row(b, n, g), g)),
                   pl.BlockSpec((1, 1, hb, hd, hd), lambda b, n, g: (b, n, g, 0, 0)),
                   pl.BlockSpec((1, 1, hb, c, c), lambda b, n, g: (b, n, g, 0, 0))],
        out_shape=[jax.ShapeDtypeStruct((bsz * seq, heads * hd), BF16),
                   jax.ShapeDtypeStruct((bsz, nch, heads, hd, hd), F32),
                   jax.ShapeDtypeStruct((bsz, nch, heads, c, c), F32)],
        scratch_shapes=[pltpu.VMEM((heads, hd, hd), F32), pltpu.VMEM((c, LANE), F32), pltpu.VMEM((LANE, c), F32),
                        pltpu.VMEM((c, LANE), F32)],
        semantics=("parallel", "arbitrary", "arbitrary"), operands=[qkv, proj, proj, alog, dtb, gnw], name="gdn_fwd",
        comms=comms)


def _gdn_bwd(qkv, proj, alog, dtb, gnw, ssave, tsave, dout, bsz, seq, heads, z_col0, ab_col0, comms=()):
    c = CHUNK
    nch = seq // c
    hb = min(HEADS_PER_STEP, heads)
    ng = heads // hb
    hd = qkv.shape[2] // heads
    wb = hb * hd

    def body(qkv_ref, z_ref, ab_ref, alog_ref, dtb_ref, gnw_ref, ssave_ref, tsave_ref, do_ref,
             dact_ref, dz_ref, dab_ref, dalog_ref, ddtb_ref, dgnw_ref,
             ds_ref, gc_ref, gt_ref, be_ref, dgacc_ref, dbacc_ref):
        n, hg = pl.program_id(1), pl.program_id(2)
        incl, strict, _ = _tri_masks(c)

        @pl.when((n == 0) & (hg == 0))
        def _():
            ds_ref[...] = jnp.zeros_like(ds_ref)
            dalog_ref[...] = jnp.zeros_like(dalog_ref)
            ddtb_ref[...] = jnp.zeros_like(ddtb_ref)
            dgnw_ref[...] = jnp.zeros_like(dgnw_ref)

        @pl.when(hg == 0)
        def _():
            g, beta = _gates(ab_ref[...], alog_ref[...], dtb_ref[...])
            gcum = _hdot(jnp.where(incl, 1.0, 0.0), g)
            gc_ref[...] = gcum
            gt_ref[...] = gcum.T
            be_ref[...] = beta
            dgacc_ref[...] = jnp.zeros_like(dgacc_ref)
            dbacc_ref[...] = jnp.zeros_like(dbacc_ref)

        gc_t, gt_t, be_t, gnw_v = gc_ref[...], gt_ref[...], be_ref[...], gnw_ref[...]
        hs = range(hb)

        def each(f):
            return [f(hh) for hh in hs]

        rsum = lambda a: jnp.sum(a, axis=-1, keepdims=True)
        sls = each(lambda i: slice(i * hd, (i + 1) * hd))
        ds_in = each(lambda i: ds_ref[hg * hb + i])
        gc = each(lambda i: _colsel(gc_t, hg * hb + i))
        gr = each(lambda i: _rowsel(gt_t, hg * hb + i))
        bc = each(lambda i: _colsel(be_t, heads + hg * hb + i))
        com = each(lambda i: _gdn_chunk_common(qkv_ref[0, :, sls[i]], qkv_ref[1, :, sls[i]], gc[i], gr[i], bc[i]))
        qh, rq, kn, rk, qn, dm, kk, m, pm = [[cm[j] for cm in com] for j in range(9)]
        tm = each(lambda i: tsave_ref[0, 0, i])
        s = each(lambda i: ssave_ref[0, 0, i])
        gam = each(lambda i: jnp.exp(gc[i]))
        glast = each(lambda i: gc[i][c - 1:c, :])
        gl = each(lambda i: jnp.exp(glast[i]))
        ratio = each(lambda i: jnp.exp(glast[i] - gc[i]))
        ks = each(lambda i: _bdot(kn[i], s[i]))
        qs = each(lambda i: _bdot(qn[i], s[i]))
        r = each(lambda i: qkv_ref[2, :, sls[i]] - gam[i] * ks[i])
        vn = each(lambda i: _bdot(tm[i], bc[i] * r[i]))
        o = each(lambda i: gam[i] * qs[i] + _bdot(pm[i], vn[i]))
        ro = each(lambda i: lax.rsqrt(jnp.mean(o[i] * o[i], axis=-1, keepdims=True) + NORM_EPS))
        zz = each(lambda i: z_ref[:, sls[i]])
        sz = each(lambda i: _sigmoid(zz[i]))
        dd = each(lambda i: do_ref[:, sls[i]])
        don = each(lambda i: dd[i] * (zz[i] * sz[i]))
        dz_h = each(lambda i: (dd[i] * (o[i] * ro[i] * gnw_v) * (sz[i] * (1.0 + zz[i] * (1.0 - sz[i])))).astype(BF16))
        dgnw = sum(each(lambda i: jnp.sum(don[i] * o[i] * ro[i], axis=0, keepdims=True)))
        uu = each(lambda i: don[i] * gnw_v)
        d_o = each(lambda i: ro[i] * uu[i] - o[i] * (ro[i] * ro[i] * ro[i]) * jnp.mean(o[i] * uu[i], axis=-1, keepdims=True))
        dqs = each(lambda i: gam[i] * d_o[i])
        dq = each(lambda i: _bdot(dqs[i], s[i], NT))
        ds_new = each(lambda i: _bdot(qn[i], dqs[i], TN))
        dp = each(lambda i: jnp.where(incl, _bdot(d_o[i], vn[i], NT), 0.0))
        dvn = each(lambda i: _bdot(pm[i], d_o[i], TN))
        dgam = each(lambda i: rsum(d_o[i] * qs[i]))
        dkd = each(lambda i: _bdot(vn[i], ds_in[i], NT))
        dvn = each(lambda i: dvn[i] + _bdot(kn[i] * ratio[i], ds_in[i]))
        ds_new = each(lambda i: ds_new[i] + gl[i] * ds_in[i])
        dgl = each(lambda i: jnp.sum(jnp.sum(ds_in[i] * s[i], axis=1, keepdims=True), axis=0, keepdims=True))
        dratio = each(lambda i: rsum(dkd[i] * kn[i]))
        dpd = each(lambda i: dp[i] * dm[i])
        dq = each(lambda i: dq[i] + _bdot(dpd[i], kn[i]))
        dk = each(lambda i: ratio[i] * dkd[i] + _bdot(dpd[i], qn[i], TN))
        dx = each(lambda i: _bdot(tm[i], dvn[i], TN))
        dr = each(lambda i: bc[i] * dx[i])
        gdr = each(lambda i: gam[i] * dr[i])
        dk = each(lambda i: dk[i] - _bdot(gdr[i], s[i], NT))
        ds_new = each(lambda i: ds_new[i] - _bdot(kn[i], gdr[i], TN))
        dmm = each(lambda i: jnp.where(strict, -_bdot(dx[i], vn[i], NT), 0.0))
        ee = each(lambda i: dmm[i] * dm[i])
        be_e = each(lambda i: bc[i] * ee[i])
        dk = each(lambda i: dk[i] + _bdot(be_e[i], kn[i]) + _bdot(be_e[i], kn[i], TN))
        dbeta = each(lambda i: rsum(dx[i] * r[i]) + rsum(ee[i] * kk[i]))
        dgam = each(lambda i: dgam[i] - rsum(dr[i] * ks[i]))
        ff = each(lambda i: dp[i] * pm[i] + dmm[i] * m[i])
        rowi = lax.broadcasted_iota(jnp.int32, (c, 1), 0)
        dgc = each(lambda i: rsum(ff[i]) - rsum(ff[i].T) + dgam[i] * gam[i] - dratio[i] * ratio[i]
                   + jnp.where(rowi == c - 1, jnp.sum(dratio[i] * ratio[i], axis=0, keepdims=True) + dgl[i] * gl[i], 0.0))
        dg_tile = sum(each(lambda i: _colput(dgc[i], hg * hb + i)))
        db_tile = sum(each(lambda i: _colput(dbeta[i], heads + hg * hb + i)))
        for i in hs:
            dqh = dq[i] * (hd ** -0.5)
            ds_ref[hg * hb + i] = ds_new[i]
            dz_ref[:, sls[i]] = dz_h[i]
            dact_ref[0, :, sls[i]] = rq[i] * (dqh - qh[i] * rsum(qh[i] * dqh))
            dact_ref[1, :, sls[i]] = rk[i] * (dk[i] - kn[i] * rsum(kn[i] * dk[i]))
            dact_ref[2, :, sls[i]] = dr[i]
        dgacc_ref[...] += dg_tile
        dbacc_ref[...] += db_tile
        dgnw_ref[0] += dgnw

        @pl.when(hg == ng - 1)
        def _():
            ab = ab_ref[...]
            ea = jnp.exp(alog_ref[...])
            g = -ea * _softplus(ab + dtb_ref[...])
            beta = be_ref[...]
            dg = _hdot(jnp.where(incl, 1.0, 0.0), dgacc_ref[...], TN)
            lane = lax.broadcasted_iota(jnp.int32, ab.shape, 1)
            da = jnp.where(lane < heads, dg * (-ea) * _sigmoid(ab + dtb_ref[...]), 0.0)
            db = dbacc_ref[...] * beta * (1.0 - beta)
            dab_ref[...] = (da + db).astype(BF16)
            dalog_ref[0] += jnp.sum(jnp.where(lane < heads, dg * g, 0.0), axis=0, keepdims=True)
            ddtb_ref[0] += jnp.sum(da, axis=0, keepdims=True)

    row = lambda b, n, g: b * nch + (nch - 1 - n)
    rev = lambda n: nch - 1 - n
    return _pcall(
        body, grid=(bsz, nch, ng),
        in_specs=[pl.BlockSpec((3, c, wb), lambda b, n, g: (0, row(b, n, g), g)),
                  pl.BlockSpec((c, wb), lambda b, n, g: (row(b, n, g), z_col0 // wb + g)),
                  pl.BlockSpec((c, LANE), lambda b, n, g: (row(b, n, g), ab_col0 // LANE)),
                  pl.BlockSpec((1, LANE), lambda b, n, g: (0, 0)),
                  pl.BlockSpec((1, LANE), lambda b, n, g: (0, 0)),
                  pl.BlockSpec((1, hd), lambda b, n, g: (0, 0)),
                  pl.BlockSpec((1, 1, hb, hd, hd), lambda b, n, g: (b, rev(n), g, 0, 0)),
                  pl.BlockSpec((1, 1, hb, c, c), lambda b, n, g: (b, rev(n), g, 0, 0)),
                  pl.BlockSpec((c, wb), lambda b, n, g: (row(b, n, g), g))],
        out_specs=[pl.BlockSpec((3, c, wb), lambda b, n, g: (0, row(b, n, g), g)),
                   pl.BlockSpec((c, wb), lambda b, n, g: (row(b, n, g), g)),
                   pl.BlockSpec((c, LANE), lambda b, n, g: (row(b, n, g), 0)),
                   pl.BlockSpec((1, 1, LANE), lambda b, n, g: (b, 0, 0)),
                   pl.BlockSpec((1, 1, LANE), lambda b, n, g: (b, 0, 0)),
                   pl.BlockSpec((1, 1, hd), lambda b, n, g: (b, 0, 0))],
        out_shape=[jax.ShapeDtypeStruct((3, bsz * seq, heads * hd), F32),
                   jax.ShapeDtypeStruct((bsz * seq, heads * hd), BF16),
                   jax.ShapeDtypeStruct((bsz * seq, LANE), BF16),
                   jax.ShapeDtypeStruct((bsz, 1, LANE), F32),
                   jax.ShapeDtypeStruct((bsz, 1, LANE), F32),
                   jax.ShapeDtypeStruct((bsz, 1, hd), F32)],
        scratch_shapes=[pltpu.VMEM((heads, hd, hd), F32), pltpu.VMEM((c, LANE), F32), pltpu.VMEM((LANE, c), F32),
                        pltpu.VMEM((c, LANE), F32), pltpu.VMEM((c, LANE), F32), pltpu.VMEM((c, LANE), F32)],
        semantics=("parallel", "arbitrary", "arbitrary"),
        operands=[qkv, proj, proj, alog, dtb, gnw, ssave, tsave, dout], name="gdn_bwd", comms=comms)


ELEMWISE_BLOCK_ELEMS = 256 * 1024


def _rows_tile(rows, cols):
    want = max(16, ELEMWISE_BLOCK_ELEMS // cols)
    if rows <= want:
        return rows
    t = (want // 16) * 16
    while t > 16 and rows % t:
        t -= 16
    return t if rows % t == 0 else rows


def _piece_specs(pieces, tr, cols):
    specs, leads = [], []
    for p, (arr, lead) in enumerate(pieces):
        if arr.ndim == 3:
            specs.append(pl.BlockSpec((1, tr, cols), functools.partial(lambda i, idx, p: (idx[p], i, 0), p=p)))
        else:
            specs.append(pl.BlockSpec((tr, cols), lambda i, idx: (i, 0)))
        leads.append(jnp.asarray(0 if lead is None else lead, jnp.int32))
    return jnp.stack(leads), specs


def _sum_pieces(refs):
    total = None
    for r in refs:
        v = r[...].astype(F32)
        v = v.reshape(v.shape[-2:])
        total = v if total is None else total + v
    return total


def _adamw(w, m, v, pieces, name, comms=()):
    rows, cols = w.shape
    tr = _rows_tile(rows, cols)
    leads, pspecs = _piece_specs(pieces, tr, cols)
    npc = len(pieces)
    c1 = 1.0 - ADAM_B1 ** ADAM_STEP
    c2 = 1.0 - ADAM_B2 ** ADAM_STEP

    def body(idx_ref, w_ref, m_ref, v_ref, *rest):
        g = _sum_pieces(rest[:npc])
        g_ref, d_ref, nm_ref, nv_ref = rest[npc:]
        nm = ADAM_B1 * m_ref[...] + (1.0 - ADAM_B1) * g
        nv = ADAM_B2 * v_ref[...] + (1.0 - ADAM_B2) * (g * g)
        g_ref[...] = g
        nm_ref[...] = nm
        nv_ref[...] = nv
        d_ref[...] = -ADAM_LR * ((nm / c1) / (jnp.sqrt(nv / c2) + ADAM_EPS) + ADAM_WD * w_ref[...])

    wspec = pl.BlockSpec((tr, cols), lambda i, idx: (i, 0))
    res, lands = _pcall(body, grid=(rows // tr,), in_specs=[wspec] * 3 + pspecs, out_specs=[wspec] * 4,
                        out_shape=[jax.ShapeDtypeStruct((rows, cols), F32)] * 4, semantics=("parallel",),
                        prefetch=[leads], operands=[w, m, v, *[p for p, _ in pieces]], name=name, comms=comms)
    return (res, lands) if comms else res


def _sum_to(pieces, out_dtype, name):
    arr0 = pieces[0][0]
    rows, cols = arr0.shape[-2:]
    tr = _rows_tile(rows, cols)
    leads, pspecs = _piece_specs(pieces, tr, cols)

    def body(idx_ref, *rest):
        rest[-1][...] = _sum_pieces(rest[:-1]).astype(out_dtype)

    return pl.pallas_call(
        body,
        grid_spec=pltpu.PrefetchScalarGridSpec(num_scalar_prefetch=1, grid=(rows // tr,), in_specs=pspecs,
                                               out_specs=pl.BlockSpec((tr, cols), lambda i, idx: (i, 0))),
        out_shape=jax.ShapeDtypeStruct((rows, cols), out_dtype),
        compiler_params=_params("parallel"), name=name)(leads, *[p for p, _ in pieces])


def _pair_add(a, recv, place, name, comms=()):
    _, rows, cols = a.shape
    tr = _rows_tile(rows, cols)
    x, y, c = place
    idx = jnp.stack([2 * (1 - x) + y, 2 * x + (1 - y), 2 * (1 - x) + (1 - y), c]).astype(jnp.int32)

    def body(p_ref, a_ref, r_ref, o_ref):
        o_ref[...] = (a_ref[...].astype(F32) + r_ref[...].astype(F32)).astype(BF16)

    res, lands = _pcall(
        body, grid=(3, rows // tr),
        in_specs=[pl.BlockSpec((1, tr, cols), lambda j, i, p: (2 * p[j] + p[3], i, 0)),
                  pl.BlockSpec((1, tr, cols), lambda j, i, p: (p[j], i, 0))],
        out_specs=[pl.BlockSpec((1, tr, cols), lambda j, i, p: (j, i, 0))],
        out_shape=[jax.ShapeDtypeStruct((3, rows, cols), BF16)], semantics=("parallel", "parallel"),
        prefetch=[idx], operands=[a, recv], name=name, comms=comms)
    return (res[0], lands) if comms else res[0]


def _to_frame(w, offs, fw, name):
    rows, n = w.shape
    tr = _tile(rows, 256)

    def body(off_ref, w_ref, o_ref, pad_ref):
        pad_ref[...] = jnp.zeros_like(pad_ref)
        pad_ref[:, 0:n] = w_ref[...]
        y = pad_ref[...]
        off1, len1, off2 = off_ref[0], off_ref[1], off_ref[2]
        col = lax.broadcasted_iota(jnp.int32, y.shape, 1)
        o_ref[0] = jnp.where(col < off1 + len1, pltpu.roll(y, off1, axis=1),
                             jnp.where(col >= off2 + len1, pltpu.roll(y, off2, axis=1), 0.0)).astype(BF16)

    res, _ = _pcall(body, grid=(rows // tr,), in_specs=[pl.BlockSpec((tr, n), lambda i, o: (i, 0))],
                    out_specs=[pl.BlockSpec((1, tr, fw), lambda i, o: (o[3], i, 0))],
                    out_shape=[jax.ShapeDtypeStruct((N_DEV, rows, fw), BF16)], scratch_shapes=[pltpu.VMEM((tr, fw), F32)],
                    semantics=("parallel",), prefetch=[offs], operands=[w], name=name)
    return res[0]


def _in_proj_core(xn, frames, ids, lay, into, name, comms=()):
    t, d = xn.shape
    n = ids.shape[0]
    tm = _tile(t, 1024)
    nb = max(b for b in range(1, lay.nc + 1) if lay.nc % b == 0 and b * MXU <= 768)
    tn, nj = nb * MXU, lay.nc // nb
    by_id = not isinstance(frames, (list, tuple))
    fr = [frames] if by_id else list(frames)
    n_fr = len(fr)

    def body(ids_ref, a_ref, *rest):
        o_ref = rest[-1]
        if by_id:
            o_ref[...] = jnp.dot(a_ref[...], rest[0][0], preferred_element_type=F32)
            return
        f = pl.program_id(1) // nj
        for k in range(n_fr):
            @pl.when(f == k)
            def _(k=k):
                o_ref[...] = jnp.dot(a_ref[...], rest[k][...], preferred_element_type=F32)

    core = lambda j: pl.multiple_of(lay.c0 * MXU + j * tn, MXU)
    col = lambda s, j: pl.multiple_of(lay.frame_block(s) * MXU + lay.c0 * MXU + j * tn, MXU)
    if by_id:
        b_specs = [pl.BlockSpec((pl.Element(1), pl.Element(d), pl.Element(tn)), lambda i, fj, ids: (ids[fj // nj], 0, core(fj % nj)))]
    else:
        b_specs = [pl.BlockSpec((pl.Element(d), pl.Element(tn)),
                                functools.partial(lambda i, fj, ids, k: (0, core(jnp.clip(fj - k * nj, 0, nj - 1))), k=k))
                   for k in range(n_fr)]
    n_into = 0 if into is None else 1
    res, lands = _pcall(
        body, grid=(t // tm, n * nj),
        in_specs=[pl.BlockSpec((tm, d), lambda i, fj, ids: (i, 0))] + b_specs + [pl.BlockSpec(memory_space=pl.ANY)] * n_into,
        out_specs=[pl.BlockSpec((pl.Element(tm), pl.Element(tn)), lambda i, fj, ids: (i * tm, col(ids[fj // nj], fj % nj)))],
        out_shape=[jax.ShapeDtypeStruct((t, lay.wp), F32)], semantics=("parallel", "arbitrary"),
        prefetch=[ids], operands=[xn] + fr + ([into] if n_into else []), name=name, comms=comms,
        fill={1 + n_fr: 0} if n_into else None)
    return res[0], lands


def _in_proj_rest(xn, frames, table, wp, into, name, comms=()):
    t, d = xn.shape
    tm = _tile(t, 1024)
    n = table.shape[1]

    def body(t_ref, a_ref, b1_ref, b2_ref, *rest):
        j = pl.program_id(1)
        b = b1_ref[0]
        b = jnp.where(t_ref[5, j] > 0, b + b2_ref[0], b)
        rest[-1][...] = jnp.dot(a_ref[...], b, preferred_element_type=F32)

    n_into = 0 if into is None else 1
    res, lands = _pcall(
        body, grid=(t // tm, n),
        in_specs=[pl.BlockSpec((tm, d), lambda i, j, tb: (i, 0)),
                  pl.BlockSpec((1, d, MXU), lambda i, j, tb: (tb[1, j], 0, tb[2, j])),
                  pl.BlockSpec((1, d, MXU), lambda i, j, tb: (tb[3, j], 0, tb[4, j]))]
        + [pl.BlockSpec(memory_space=pl.ANY)] * n_into,
        out_specs=[pl.BlockSpec((tm, MXU), lambda i, j, tb: (i, tb[0, j]))],
        out_shape=[jax.ShapeDtypeStruct((t, wp), F32)], semantics=("parallel", "arbitrary"),
        prefetch=[table], operands=[xn, frames, frames] + ([into] if n_into else []), name=name, comms=comms,
        fill={3: 0} if n_into else None)
    return res[0], lands


def _place():
    x, y, c = lax.axis_index("x"), lax.axis_index("y"), lax.axis_index("c")
    chips = [(1 - x, y), (x, 1 - y), (1 - x, 1 - y)]
    return x, y, c, chips


def _allreduce_small(buf, name):
    rows = buf.shape[0]

    def body(x_ref, o_ref, g_ref, send_sems, recv_sems):
        x, y, c, chips = _place()
        me, sibling = (x, y, c), (x, y, 1 - c)

        def copy(k, block, to, src=None):
            dst = g_ref.at[4 * block[0] + 2 * block[1] + block[2]]
            return pltpu.make_async_remote_copy(src_ref=dst if src is None else src, dst_ref=dst,
                                                send_sem=send_sems.at[k], recv_sem=recv_sems.at[k],
                                                device_id=to, device_id_type=MESH)

        first = [copy(0, me, sibling, src=x_ref)]
        first += [copy(1 + j, me, (*chip, c), src=x_ref) for j, chip in enumerate(chips)]
        for cp in first:
            cp.start()
        passed = [copy(4 + j, (*chip, c), sibling) for j, chip in enumerate(chips)]
        for j, chip in enumerate(chips):
            copy(1 + j, (*chip, c), me).wait_recv()
            passed[j].start()
        copy(0, sibling, me).wait_recv()
        for j, chip in enumerate(chips):
            copy(4 + j, (*chip, 1 - c), me).wait_recv()
        for cp in first + passed:
            cp.wait_send()
        g_ref[4 * x + 2 * y + c] = x_ref[...]
        total = g_ref[0]
        for s in range(1, N_DEV):
            total = total + g_ref[s]
        o_ref[...] = total

    vm = pl.BlockSpec(memory_space=pltpu.VMEM)
    return pl.pallas_call(
        body, in_specs=[vm], out_specs=vm, out_shape=jax.ShapeDtypeStruct((rows, LANE), F32),
        scratch_shapes=[pltpu.VMEM((N_DEV, rows, LANE), F32), pltpu.SemaphoreType.DMA((7,)), pltpu.SemaphoreType.DMA((7,))],
        name=name)(buf)


def _rows(ref, rows):
    return ref if rows is None else ref.at[pl.ds(rows[0], rows[1] - rows[0])]


AG_ALL = ("here", "sibling", 0, 1, 2)


def _ag(shard=None, into=None, to=(), forward=(), rows=None):
    def plan(srcs, lands):
        x, y, c, chips = _place()
        buf = lands[0]
        out = []
        if to:
            dst = _rows(buf.at[4 * x + 2 * y + c], rows)
            src = dst if shard is None else _rows(srcs[0], rows)
            for who in to:
                if who == "here":
                    out.append(("local", src, dst, None))
                elif who == "sibling":
                    out.append(("remote", src, dst, (x, y, 1 - c)))
                else:
                    out.append(("remote", src, dst, (*chips[who], c)))
        for j in forward:
            r = _rows(buf.at[4 * chips[j][0] + 2 * chips[j][1] + c], rows)
            out.append(("remote", r, r, (x, y, 1 - c)))
        return out

    srcs = ([shard] if to and shard is not None else []) + ([into] if into is not None else [])
    land = jax.ShapeDtypeStruct(into.shape, into.dtype) if into is not None else jax.ShapeDtypeStruct((N_DEV,) + shard.shape, shard.dtype)
    return _Comm(srcs, [land], plan, len(to) + len(forward), alias={len(srcs) - 1: 0} if into is not None else None)


def _ag_first(shard, rows=None, into=None, to=AG_ALL):
    return _ag(shard=shard, into=into, to=to, rows=rows)


def _ag_second(g, rows=None, of=(0, 1, 2)):
    return _ag(into=g, forward=of, rows=rows)


def _rs_first(grad):
    def plan(srcs, lands):
        x, y, c, _ = _place()
        (a,), (land,) = srcs, lands
        return [("remote", a.at[2 * j + (1 - c)], land.at[j], (x, y, 1 - c)) for j in range(4)]

    return _Comm([grad], [jax.ShapeDtypeStruct((4,) + grad.shape[1:], grad.dtype)], plan, 4)


def _rs_second(pair, rows=None, into=None):
    def plan(srcs, lands):
        x, y, c, chips = _place()
        return [("remote", _rows(srcs[0].at[j], rows), _rows(lands[0].at[j], rows), (cx, cy, c))
                for j, (cx, cy) in enumerate(chips)]

    land = jax.ShapeDtypeStruct((3,) + pair.shape[1:], pair.dtype)
    if into is None:
        return _Comm([pair], [land], plan, 3)
    return _Comm([pair, into], [land], plan, 3, alias={1: 0})


class _InLayout:
    def __init__(self, n_in, gw, heads, scw):
        self.n_in, self.split = n_in, 4 * gw + 2 * heads
        self.gap = LANE - 2 * heads
        self.ab_col, self.sc_col = 4 * gw, 4 * gw + LANE
        self.used = 4 * gw + LANE + 3 * scw
        p0 = [s * n_in + (self.gap if s * n_in >= self.split else 0) for s in range(N_DEV)]
        self.fstart = [(p // MXU) * MXU for p in p0]
        need = []
        for s in range(N_DEV):
            straddle = s * n_in < self.split < (s + 1) * n_in
            need.append(p0[s] - self.fstart[s] + n_in + (self.gap if straddle else 0))
        self.fw = -(-max(need) // MXU) * MXU
        self.wp = max(f + self.fw for f in self.fstart)
        assert self.wp >= self.used and self.wp % MXU == 0
        nfb = self.fw // MXU
        rows = []
        for jb in range(self.wp // MXU):
            src = [(s, jb - self.fstart[s] // MXU) for s in range(N_DEV) if 0 <= jb - self.fstart[s] // MXU < nfb]
            assert 1 <= len(src) <= 2, (jb, src)
            (s1, b1), (s2, b2) = src[0], src[-1]
            rows.append((s1, b1, s2, b2, int(len(src) == 2)))
        self.table = np.asarray(rows, np.int32).T.copy()
        single = [all(rows[self.fstart[s] // MXU + b][4] == 0 for s in range(N_DEV)) for b in range(nfb)]
        runs, b = [], 0
        while b < nfb:
            if single[b]:
                e = b
                while e < nfb and single[e]:
                    e += 1
                runs.append((e - b, b))
                b = e
            else:
                b += 1
        self.nc, self.c0 = max(runs) if runs else (0, 0)
        in_core = {self.fstart[s] // MXU + b for s in range(N_DEV) for b in range(self.c0, self.c0 + self.nc)}
        self.rest = np.asarray([(jb,) + rows[jb] for jb in range(self.wp // MXU) if jb not in in_core], np.int32).T.copy()

    def frame_block(self, s):
        p = s * self.n_in
        return (p + jnp.where(p >= self.split, self.gap, 0)) // MXU

    def offsets(self, s):
        p = s * self.n_in
        after = p >= self.split
        off1 = p + jnp.where(after, self.gap, 0) - self.frame_block(s) * MXU
        len1 = jnp.where(after, self.n_in, jnp.clip(self.split - p, 0, self.n_in))
        off2 = off1 + jnp.where(len1 < self.n_in, self.gap, 0)
        return off1, len1, off2

    def to_frame(self, w, s):
        return _to_frame(w, jnp.stack(self.offsets(s) + (s,)).astype(jnp.int32), self.fw, "w_in_frame")

    def from_frame(self, f, s):
        off1, len1, off2 = self.offsets(s)
        a = lax.dynamic_slice(f, (0, off1), (f.shape[0], self.n_in))
        b = lax.dynamic_slice(f, (0, off2), (f.shape[0], self.n_in))
        col = lax.broadcasted_iota(jnp.int32, (1, self.n_in), 1)
        return jnp.where(col < len1, a, b)


def _pack_rows(parts):
    rows = []
    for p in parts:
        flat = p.reshape(-1)
        pad = (-flat.shape[0]) % LANE
        rows.append(jnp.pad(flat, (0, pad)).reshape(-1, LANE))
    buf = jnp.concatenate(rows, axis=0)
    return jnp.pad(buf, ((0, (-buf.shape[0]) % 8), (0, 0)))


def _unpack_rows(buf, shapes):
    out, r = [], 0
    for shp in shapes:
        size = int(np.prod(shp))
        nr = -(-size // LANE)
        out.append(buf[r:r + nr].reshape(-1)[:size].reshape(shp))
        r += nr
    return out


def _pad_lanes(v):
    return jnp.pad(v, ((0, 0), (0, LANE - v.shape[1])))


def kernel(x, norm_mix_pre, w_in, conv_qkv_w, a_log, dt_bias, gdn_norm_w, conv_sc_w, w_out, norm_mix_post, norm_mlp_pre, w_up, w_down, norm_mlp_post, loss_target, m_norm_mix_pre, m_w_in, m_conv_qkv_w, m_a_log, m_dt_bias, m_gdn_norm_w, m_conv_sc_w, m_w_out, m_norm_mix_post, m_norm_mlp_pre, m_w_up, m_w_down, m_norm_mlp_post, v_norm_mix_pre, v_w_in, v_conv_qkv_w, v_a_log, v_dt_bias, v_gdn_norm_w, v_conv_sc_w, v_w_out, v_norm_mix_post, v_norm_mlp_pre, v_w_up, v_w_down, v_norm_mlp_post):
    bsz, seq, d = x.shape
    t = bsz * seq
    heads, hd = a_log.shape[-1], gdn_norm_w.shape[-1]
    gw = heads * hd
    scw = conv_sc_w.shape[-1] * N_DEV
    dff_w = w_up.shape[-1] * N_DEV
    lay = _InLayout(w_in.shape[-1], gw, heads, scw)
    mx, my, mc = lax.axis_index("x"), lax.axis_index("y"), lax.axis_index("c")
    me = 4 * mx + 2 * my + mc
    chip = 2 * mx + my

    x2 = x.reshape(t, d)
    tgt = loss_target.reshape(t, d)
    g1, g2, g3, g4 = norm_mix_pre, norm_mix_post, norm_mlp_pre, norm_mlp_post

    g_in = lay.to_frame(w_in[0], me)
    w_out_b, w_up_b, w_down_b = w_out[0].astype(BF16), w_up[0].astype(BF16), w_down[0].astype(BF16)
    up_cols = dff_w // N_DEV
    qu, qd = d // 4, up_cols // 4
    kq, ks = conv_qkv_w.shape[1], conv_sc_w.shape[1]
    cq_n, cs_n = conv_qkv_w.shape[-1], conv_sc_w.shape[-1]
    cq_full = lax.dynamic_update_slice(jnp.zeros((kq, 3 * gw), F32), conv_qkv_w[0], (0, me * cq_n))
    cs_full = lax.dynamic_update_slice(jnp.zeros((ks, scw), F32), conv_sc_w[0], (0, me * cs_n))
    conv_q, conv_s = _unpack_rows(_allreduce_small(_pack_rows([cq_full, cs_full]), "allgather_conv"),
                                  [(kq, 3 * gw), (ks, scw)])
    alog_t, dtb_t = _pad_lanes(a_log), _pad_lanes(dt_bias)

    xn, (g_in,) = _rms_fwd(x2, g1, comms=[_ag(into=g_in, to=("sibling",))])

    def arrived(k):
        return [lax.dynamic_index_in_dim(g_in, s, 0, keepdims=False) for s in ids[k]]

    dev = lambda px, py, pc: (4 * px + 2 * py + pc).astype(jnp.int32)
    ids = [jnp.stack([dev(mx, my, mc), dev(mx, my, 1 - mc)]),
           jnp.stack([dev(1 - mx, my, mc), dev(mx, 1 - my, mc)]),
           jnp.stack([dev(1 - mx, my, 1 - mc), dev(mx, 1 - my, 1 - mc)]),
           jnp.stack([dev(1 - mx, 1 - my, mc), dev(1 - mx, 1 - my, 1 - mc)])]
    assert lay.nc > 0, "the frames have no columns of their own at these sizes"
    proj, (g_in,) = _in_proj_core(xn, arrived(0), ids[0], lay, None, "in_proj_0", comms=[_ag(into=g_in, to=(0, 1))])
    proj, (g_in,) = _in_proj_core(xn, arrived(1), ids[1], lay, proj, "in_proj_1",
                                  comms=[_ag(into=g_in, to=(2,), forward=(0, 1))])
    proj, (g_in, g_out) = _in_proj_core(xn, arrived(2), ids[2], lay, proj, "in_proj_2",
                                        comms=[_ag(into=g_in, forward=(2,)), _ag_first(w_out_b)])
    eu = qu // 2
    proj, (g_up,) = _in_proj_core(xn, g_in, ids[3], lay, proj, "in_proj_3", comms=[_ag_first(w_up_b, rows=(0, eu))])
    proj, (g_up,) = _in_proj_rest(xn, g_in, jnp.asarray(lay.rest), lay.wp, proj, "in_proj_rest",
                                  comms=[_ag_first(w_up_b, rows=(eu, 2 * eu), into=g_up)])
    qkv, (g_out, g_up) = _qkvconv_fwd(proj, conv_q, bsz, seq, gw,
                                      comms=[_ag_second(g_out), _ag_first(w_up_b, rows=(2 * eu, 4 * eu), into=g_up)])
    (gdn_out, ssave, tsave), (g_up,) = _gdn_fwd(
        qkv, proj, alog_t, dtb_t, gdn_norm_w, bsz, seq, heads, 3 * gw, lay.ab_col,
        comms=[_ag_first(w_up_b, rows=(4 * eu, 8 * eu), into=g_up)])
    sc_out = _sc_fwd(proj, conv_s, bsz, seq, scw, lay.sc_col)
    mixed = jnp.concatenate([gdn_out, sc_out], axis=1)
    w_out_f = g_out.reshape(d, d)
    cuts = [(up_cols * c) // 64 for c in (0, 16, 22, 36, 50, 64)]
    (mix,), (g_up, g_down) = _matmul(mixed, w_out_f, mode="nn", out_dtypes=[F32], name="out_proj",
                                     comms=[_ag_second(g_up), _ag_first(w_down_b, rows=(cuts[0], cuts[1]))])
    (h, xn2), (g_down,) = _post1(x2, mix, g2, g3, comms=[_ag_first(w_down_b, rows=(cuts[1], cuts[2]), into=g_down)])

    def up_epilogue(acc):
        r = jnp.maximum(acc, 0.0)
        return r, r * r

    tq = t // 4
    act_hid = None
    for part in range(4):
        if part < 3:
            leg = [_ag_first(w_down_b, rows=(cuts[2 + part], cuts[3 + part]), into=g_down)]
        else:
            leg = [_ag_second(g_down)]
        act_hid, (g_down,) = _matmul(
            xn2, g_up, mode="nn", out_dtypes=[BF16, BF16], name="mlp_up_%d" % part, n_cols=dff_w, epilogue=up_epilogue,
            b_spec=lambda tk, tn: pl.BlockSpec((1, tk, tn), lambda i, j, k: (j // (up_cols // tn), k, j % (up_cols // tn))),
            a_rows=(part * tq, tq), out_into=act_hid, comms=leg)
    act, hid = act_hid
    w_down_f = g_down.reshape(dff_w, d)
    (ff,) = _matmul(hid, w_down_f, mode="nn", out_dtypes=[F32], name="mlp_down", tn=1024, tk=2048)
    dff, dy, dg4, loss_p = _post2_loss(h, ff, g4, tgt)

    def pieces(part, sib, got):
        return [(part, me), (sib, chip), (got, 0), (got, 1), (got, 2)]

    place = (mx, my, mc)

    (dpre,) = _matmul(dff, w_down_f, mode="nt", out_dtypes=[BF16], name="d_hidden", extras=[act],
                      epilogue=lambda acc, a: (acc * (2.0 * a.astype(F32)),))
    (dw_down,) = _matmul(hid, dff, mode="tn", out_dtypes=[BF16], name="dw_down")
    dw_down = dw_down.reshape(N_DEV, dff_w // N_DEV, d)
    (dxn2,), (sib_down,) = _matmul(
        dpre, g_up, mode="nt", out_dtypes=[F32], name="d_xn2", n_cols=d, tn=1024, tk=min(up_cols, 2048),
        b_spec=lambda tk, tn: pl.BlockSpec((1, tn, tk), lambda i, j, k: (k // (up_cols // tk), j, k % (up_cols // tk))),
        comms=[_rs_first(dw_down)])
    pair_down = _pair_add(dw_down, sib_down, place, "pair_add_down")
    (dw_up,), (got_down,) = _matmul(
        xn2, dpre, mode="tn", out_dtypes=[BF16], name="dw_up",
        out_custom=lambda tm, tn: ((N_DEV, d, up_cols), (1, tm, tn), lambda i, j, k: (j // (up_cols // tn), i, j % (up_cols // tn))),
        comms=[_rs_second(pair_down, rows=(0, 7 * qd // 2))])
    (dmix, dh, dg2, dg3), (got_down, sib_up) = _mid_bwd(
        h, mix, dy, dxn2, g2, g3, comms=[_rs_second(pair_down, rows=(7 * qd // 2, 4 * qd), into=got_down), _rs_first(dw_up)])
    pair_up = _pair_add(dw_up, sib_up, place, "pair_add_up")
    (dmixed,) = _matmul(dmix, w_out_f, mode="nt", out_dtypes=[F32], name="d_mixed")
    (dw_out,), (got_up,) = _matmul(mixed, dmix, mode="tn", out_dtypes=[BF16], name="dw_out",
                                   comms=[_rs_second(pair_up, rows=(0, qu))])
    dw_out = dw_out.reshape(N_DEV, d // N_DEV, d)
    (dscb, dscc, dsch, dconv_s), (sib_out,) = _sc_bwd(proj, conv_s, dmixed, bsz, seq, scw, lay.sc_col, gw,
                                                      comms=[_rs_first(dw_out)])
    pair_out = _pair_add(dw_out, sib_out, place, "pair_add_out")
    (dact, dz, dab, dalog, ddtb, dgnw), (got_up,) = _gdn_bwd(
        qkv, proj, alog_t, dtb_t, gdn_norm_w, ssave, tsave, dmixed, bsz, seq, heads, 3 * gw, lay.ab_col,
        comms=[_rs_second(pair_up, rows=(qu, 3 * qu), into=got_up)])
    (dqkv, dconv_q), (got_up,) = _qkvconv_bwd(proj, conv_q, dact, bsz, seq, gw,
                                              comms=[_rs_second(pair_up, rows=(3 * qu, 4 * qu), into=got_up)])
    dproj = jnp.concatenate([dqkv, dz, dab, dscb, dscc, dsch, jnp.zeros((t, lay.wp - lay.used), BF16)], axis=1)
    tn_in = _tile(lay.fw, 1024)
    nfb = lay.fw // tn_in
    hd2 = d // 2

    def dw_in_half(half, comms):
        return _matmul(
            xn, dproj, mode="tn", out_dtypes=[BF16], name="dw_in_%d" % half, n_cols=N_DEV * lay.fw, tn=tn_in,
            a_rows=(half * hd2, hd2),
            b_spec=lambda tk, tn: pl.BlockSpec(
                (pl.Element(tk), pl.Element(tn)),
                lambda i, j, k: (0, pl.multiple_of(lay.frame_block(j // nfb) * MXU + (j % nfb) * tn, LANE))),
            out_custom=lambda tm, tn: ((N_DEV, hd2, lay.fw), (1, tm, tn), lambda i, j, k: (j // nfb, i, j % nfb)),
            comms=comms)

    (dw_in_a,), (got_out,) = dw_in_half(0, [_rs_second(pair_out)])
    (dw_in_b,), (sib_a,) = dw_in_half(1, [_rs_first(dw_in_a)])
    pair_a = _pair_add(dw_in_a, sib_a, place, "pair_add_in_a")
    tk_in = _tile(lay.fw, 2048)
    kpf = lay.fw // tk_in

    def d_xn(part, into, comms):
        return _matmul(
            dproj, g_in, mode="nt", out_dtypes=[F32], name="d_xn_%d" % part, n_cols=d, tn=1024, tk=tk_in,
            k_total=N_DEV * lay.fw, a_rows=(part * tq, tq), out_into=None if into is None else [into], comms=comms,
            a_spec=lambda tm, tk, r0: pl.BlockSpec(
                (pl.Element(tm), pl.Element(tk)),
                lambda i, j, k: (pl.multiple_of(i * tm + r0, 16),
                                 pl.multiple_of(lay.frame_block(k // kpf) * MXU + (k % kpf) * tk, LANE))),
            b_spec=lambda tk, tn: pl.BlockSpec((1, tn, tk), lambda i, j, k: (k // kpf, j, k % kpf)))

    (dxn,), (got_a, sib_b) = d_xn(0, None, [_rs_second(pair_a, rows=(0, qu)), _rs_first(dw_in_b)])
    pair_b = _pair_add(dw_in_b, sib_b, place, "pair_add_in_b")
    (dxn,), (got_a,) = d_xn(1, dxn, [_rs_second(pair_a, rows=(qu, 2 * qu), into=got_a)])
    (dxn,), (got_b,) = d_xn(2, dxn, [_rs_second(pair_b, rows=(0, qu))])
    (dxn,), (got_b,) = d_xn(3, dxn, [_rs_second(pair_b, rows=(qu, 2 * qu), into=got_b)])
    grad_x, dg1 = _pre_bwd(x2, dh, dxn, g1)

    gin_frame = jnp.concatenate([_sum_to(pieces(dw_in_a, sib_a, got_a), F32, "grad_w_in_frame_a"),
                                 _sum_to(pieces(dw_in_b, sib_b, got_b), F32, "grad_w_in_frame_b")], axis=0)
    big = {
        "w_in": _adamw(w_in[0], m_w_in[0], v_w_in[0], [(lay.from_frame(gin_frame, me), None)], "adamw_w_in"),
        "w_out": _adamw(w_out[0], m_w_out[0], v_w_out[0], pieces(dw_out, sib_out, got_out), "adamw_w_out"),
        "w_up": _adamw(w_up[0], m_w_up[0], v_w_up[0], pieces(dw_up, sib_up, got_up), "adamw_w_up"),
        "w_down": _adamw(w_down[0], m_w_down[0], v_w_down[0], pieces(dw_down, sib_down, got_down), "adamw_w_down"),
    }

    small_shapes = [(kq, 3 * gw), (ks, scw), (1, d), (1, d), (1, d), (1, d), (1, LANE), (1, LANE), (1, hd), (1, LANE)]
    small = _unpack_rows(
        _allreduce_small(_pack_rows([dconv_q, dconv_s, dg1, dg2, dg3, dg4, jnp.sum(dalog, axis=0), jnp.sum(ddtb, axis=0),
                                     jnp.sum(dgnw, axis=0), loss_p]), "allreduce_small"), small_shapes)
    gq, gs, sg1, sg2, sg3, sg4, salog, sdtb, sgnw, sloss = small
    loss = sloss[0, 0]
    small_grads = {
        "norm_mix_pre": sg1, "conv_qkv_w": lax.dynamic_slice(gq, (0, me * cq_n), (kq, cq_n)),
        "a_log": salog[:, :heads], "dt_bias": sdtb[:, :heads], "gdn_norm_w": sgnw,
        "conv_sc_w": lax.dynamic_slice(gs, (0, me * cs_n), (ks, cs_n)),
        "norm_mix_post": sg2, "norm_mlp_pre": sg3, "norm_mlp_post": sg4,
    }
    weights = {"norm_mix_pre": (norm_mix_pre, m_norm_mix_pre, v_norm_mix_pre), "conv_qkv_w": (conv_qkv_w[0], m_conv_qkv_w[0], v_conv_qkv_w[0]),
               "a_log": (a_log, m_a_log, v_a_log), "dt_bias": (dt_bias, m_dt_bias, v_dt_bias),
               "gdn_norm_w": (gdn_norm_w, m_gdn_norm_w, v_gdn_norm_w), "conv_sc_w": (conv_sc_w[0], m_conv_sc_w[0], v_conv_sc_w[0]),
               "norm_mix_post": (norm_mix_post, m_norm_mix_post, v_norm_mix_post),
               "norm_mlp_pre": (norm_mlp_pre, m_norm_mlp_pre, v_norm_mlp_pre),
               "norm_mlp_post": (norm_mlp_post, m_norm_mlp_post, v_norm_mlp_post)}
    res = dict(big)
    for name, (w, m, v) in weights.items():
        res[name] = _adamw(w, m, v, [(small_grads[name], None)], "adamw_" + name)

    order = ["norm_mix_pre", "w_in", "conv_qkv_w", "a_log", "dt_bias", "gdn_norm_w", "conv_sc_w", "w_out", "norm_mix_post",
             "norm_mlp_pre", "w_up", "w_down", "norm_mlp_post"]
    shapes = {"norm_mix_pre": norm_mix_pre.shape, "w_in": w_in.shape, "conv_qkv_w": conv_qkv_w.shape, "a_log": a_log.shape,
              "dt_bias": dt_bias.shape, "gdn_norm_w": gdn_norm_w.shape, "conv_sc_w": conv_sc_w.shape, "w_out": w_out.shape,
              "norm_mix_post": norm_mix_post.shape, "norm_mlp_pre": norm_mlp_pre.shape, "w_up": w_up.shape,
              "w_down": w_down.shape, "norm_mlp_post": norm_mlp_post.shape}
    outs = [loss, grad_x.reshape(bsz, seq, d)]
    for part in range(4):
        outs += [res[nm][part].reshape(shapes[nm]) for nm in order]
    return tuple(outs)
```

```python
import functools
import math

import numpy as np
import jax
import jax.numpy as jnp
from jax import lax
from jax.experimental import pallas as pl
from jax.experimental.pallas import tpu as pltpu

F32 = jnp.float32
BF16 = jnp.bfloat16
HI = lax.Precision.HIGHEST
MESH = pl.DeviceIdType.MESH

N_DEV = 8
LANE = 128
MXU = 256
CHUNK = 64
NORM_EPS = 1e-6
L2_EPS = 1e-6
VMEM_LIMIT = 56 * 1024 * 1024

ADAM_LR = 0.001
ADAM_B1 = 0.9
ADAM_B2 = 0.999
ADAM_EPS = 1e-08
ADAM_WD = 0.01
ADAM_STEP = 10

NN = (((1,), (0,)), ((), ()))
NT = (((1,), (1,)), ((), ()))
TN = (((0,), (0,)), ((), ()))


def _params(*sem):
    return pltpu.CompilerParams(dimension_semantics=sem, vmem_limit_bytes=VMEM_LIMIT)


def _tile(n, want):
    if n <= want:
        return n
    t = (want // LANE) * LANE
    while t > LANE and n % t:
        t -= LANE
    assert n % t == 0, (n, want)
    return t


class _Comm:
    def __init__(self, srcs, lands, plan, n, alias=None):
        self.srcs, self.lands, self.plan, self.n, self.alias = list(srcs), list(lands), plan, n, dict(alias or {})


def _pcall(body, *, grid, in_specs, out_specs, out_shape, operands, name, scratch_shapes=(), semantics=None,
           prefetch=(), comms=(), fill=None):
    n_pf, n_in, n_out, n_scr = len(prefetch), len(in_specs), len(out_specs), len(scratch_shapes)
    srcs = [s for cm in comms for s in cm.srcs]
    lands = [l for cm in comms for l in cm.lands]
    n_src, n_land = len(srcs), len(lands)
    n_copies = sum(cm.n for cm in comms)
    aliases, so, lo = {n_pf + a: b for a, b in (fill or {}).items()}, 0, 0
    for cm in comms:
        for a, b in cm.alias.items():
            aliases[n_pf + n_in + so + a] = n_out + lo + b
        so, lo = so + len(cm.srcs), lo + len(cm.lands)
    any_spec = pl.BlockSpec(memory_space=pl.ANY)

    def wrapped(*refs):
        pf, r = refs[:n_pf], refs[n_pf:]
        ins, csrc = r[:n_in], r[n_in:n_in + n_src]
        outs = r[n_in + n_src:n_in + n_src + n_out]
        cland = r[n_in + n_src + n_out:n_in + n_src + n_out + n_land]
        rest = r[n_in + n_src + n_out + n_land:]
        scratch = rest[:n_scr]
        if not comms:
            body(*pf, *ins, *outs, *scratch)
            return
        send_sems, recv_sems = rest[n_scr:]

        def copies():
            out, k, s0, l0 = [], 0, 0, 0
            for cm in comms:
                for kind, src, dst, dev in cm.plan(csrc[s0:s0 + len(cm.srcs)], cland[l0:l0 + len(cm.lands)]):
                    if kind == "local":
                        out.append((kind, pltpu.make_async_copy(src, dst, send_sems.at[k])))
                    else:
                        out.append((kind, pltpu.make_async_remote_copy(
                            src_ref=src, dst_ref=dst, send_sem=send_sems.at[k], recv_sem=recv_sems.at[k],
                            device_id=dev, device_id_type=MESH)))
                    k += 1
                s0, l0 = s0 + len(cm.srcs), l0 + len(cm.lands)
            assert k == n_copies
            return out

        ids = [pl.program_id(a) for a in range(len(grid))]
        first = functools.reduce(jnp.logical_and, [i == 0 for i in ids])
        last = functools.reduce(jnp.logical_and, [i == g - 1 for i, g in zip(ids, grid)])

        @pl.when(first)
        def _():
            for _, cp in copies():
                cp.start()

        body(*pf, *ins, *outs, *scratch)

        @pl.when(last)
        def _():
            cps = copies()
            for kind, cp in cps:
                if kind == "remote":
                    cp.wait_recv()
            for kind, cp in cps:
                if kind == "remote":
                    cp.wait_send()
                else:
                    cp.wait()

    sems = [pltpu.SemaphoreType.DMA((n_copies,)), pltpu.SemaphoreType.DMA((n_copies,))] if comms else []
    if semantics is None or comms:
        semantics = ("arbitrary",) * len(grid)
    res = pl.pallas_call(
        wrapped,
        grid_spec=pltpu.PrefetchScalarGridSpec(
            num_scalar_prefetch=n_pf, grid=tuple(grid), in_specs=list(in_specs) + [any_spec] * n_src,
            out_specs=list(out_specs) + [any_spec] * n_land, scratch_shapes=list(scratch_shapes) + sems),
        out_shape=list(out_shape) + lands,
        input_output_aliases=aliases,
        compiler_params=_params(*semantics), name=name)(*prefetch, *operands, *srcs)
    return list(res[:n_out]), list(res[n_out:])


def _bdot(a, b, dims=NN):
    return lax.dot_general(a.astype(BF16), b.astype(BF16), dims, preferred_element_type=F32)


def _hdot(a, b, dims=NN):
    return lax.dot_general(a, b, dims, preferred_element_type=F32, precision=HI)


def _mdot(a, b, dims=NN):
    return lax.dot_general(a, b, dims, preferred_element_type=F32, precision=lax.Precision.HIGH)


def _sigmoid(x):
    return 1.0 / (1.0 + jnp.exp(-x))


def _softplus(x):
    return jnp.maximum(x, 0.0) + jnp.log(1.0 + jnp.exp(-jnp.abs(x)))


def _matmul(a, b, *, mode, out_dtypes, name, n_cols=None, tm=1024, tn=512, tk=4096, epilogue=None, extras=(),
            b_spec=None, out_custom=None, a_rows=None, out_into=None, a_spec=None, k_total=None, comms=()):
    if mode == "tn":
        K, M = a.shape
    else:
        M, K = a.shape
    if k_total is not None:
        K = k_total
    r0 = 0
    if a_rows is not None:
        r0, M = a_rows
    N = n_cols if n_cols is not None else (b.shape[0] if mode == "nt" else b.shape[1])
    tm, tk, tn = _tile(M, tm), _tile(K, tk), _tile(N, tn)
    assert r0 % tm == 0
    i0 = r0 // tm
    if b_spec is None:
        b_spec = pl.BlockSpec((tn, tk), lambda i, j, k: (j, k)) if mode == "nt" else pl.BlockSpec((tk, tn), lambda i, j, k: (k, j))
    else:
        b_spec = b_spec(tk, tn)
    gm, gn, nk = M // tm, N // tn, K // tk
    if out_custom is not None:
        shape, blk, ix = out_custom(tm, tn)
        out_shapes, out_blocks, out_index = [shape] * len(out_dtypes), [blk] * len(out_dtypes), [ix] * len(out_dtypes)
    elif a_rows is not None:
        out_shapes = [(a.shape[0], N)] * len(out_dtypes)
        out_blocks = [(tm, tn)] * len(out_dtypes)
        out_index = [lambda i, j, k: (i + i0, j)] * len(out_dtypes)
    else:
        out_shapes = [(M, N)] * len(out_dtypes)
        out_blocks = [(tm, tn)] * len(out_dtypes)
        out_index = [lambda i, j, k: (i, j)] * len(out_dtypes)
    if a_spec is not None:
        a_spec = a_spec(tm, tk, r0)
    elif mode == "tn":
        a_spec = pl.BlockSpec((tk, tm), lambda i, j, k: (k, i + i0))
    else:
        a_spec = pl.BlockSpec((tm, tk), lambda i, j, k: (i + i0, k))
    hoist = mode == "tn" and nk == 1 and gn > 1
    dims = {"nn": NN, "nt": NT, "tn": TN}[mode]
    n_ex, n_out = len(extras), len(out_dtypes)
    out_into = [] if out_into is None else list(out_into)
    n_into = len(out_into)
    assert n_into in (0, n_out)

    def body(a_ref, b_ref, *rest):
        ex, outs = rest[:n_ex], rest[n_ex + n_into:n_ex + n_into + n_out]

        def finish(acc):
            res = epilogue(acc, *[e[...] for e in ex]) if epilogue is not None else (acc,)
            for o, r in zip(outs, res):
                o[...] = r.reshape(o.shape).astype(o.dtype)

        bb = b_ref[...]
        bb = bb.reshape(bb.shape[-2:])
        if hoist:
            at_ref = rest[-1]

            @pl.when(pl.program_id(1) == 0)
            def _():
                at_ref[...] = a_ref[...].T

            finish(lax.dot_general(at_ref[...], bb, NN, preferred_element_type=F32))
            return
        part = lax.dot_general(a_ref[...], bb, dims, preferred_element_type=F32)
        if nk == 1:
            finish(part)
        else:
            acc = rest[-1]
            k = pl.program_id(2)

            @pl.when(k == 0)
            def _():
                acc[...] = part

            @pl.when(k > 0)
            def _():
                acc[...] += part

            @pl.when(k == nk - 1)
            def _():
                finish(acc[...])

    scratch = [pltpu.VMEM((tm, tk), BF16)] if hoist else ([pltpu.VMEM((tm, tn), F32)] if nk > 1 else [])
    outs, lands = _pcall(
        body, grid=(gm, gn, nk),
        in_specs=([a_spec, b_spec] + [pl.BlockSpec((tm, tn), lambda i, j, k: (i, j)) for _ in extras]
                  + [pl.BlockSpec(memory_space=pl.ANY)] * n_into),
        out_specs=[pl.BlockSpec(blk, ix) for blk, ix in zip(out_blocks, out_index)],
        out_shape=[jax.ShapeDtypeStruct(s, d) for s, d in zip(out_shapes, out_dtypes)],
        scratch_shapes=scratch, semantics=("parallel", "arbitrary", "arbitrary"),
        operands=[a, b, *extras, *out_into], name=name, comms=comms,
        fill={2 + n_ex + o: o for o in range(n_into)})
    return (outs, lands) if comms else outs


TR = 256
QKV_CONV_COLS = 256


def _rms(x):
    return lax.rsqrt(jnp.mean(x * x, axis=-1, keepdims=True) + NORM_EPS)


def _rms_bwd(x, r, w, dy):
    u = dy * w
    dx = r * u - x * (r * r * r) * jnp.mean(x * u, axis=-1, keepdims=True)
    return dx, dy * x * r


def _row_call(body, ins, row_flags, outs, name, n_rows, comms=()):
    n_row = sum(row_flags) + sum(kind == "row" for _, _, kind in outs)
    tr = min(TR if n_row <= 5 else TR // 2, n_rows)
    in_specs = []
    for arr, is_row in zip(ins, row_flags):
        if is_row:
            in_specs.append(pl.BlockSpec((tr, arr.shape[1]), lambda i: (i, 0)))
        else:
            in_specs.append(pl.BlockSpec(arr.shape, lambda i: (0, 0)))
    out_specs, out_shape = [], []
    for shape, dtype, kind in outs:
        if kind == "row":
            out_specs.append(pl.BlockSpec((tr, shape[1]), lambda i: (i, 0)))
        else:
            out_specs.append(pl.BlockSpec(shape, lambda i: (0, 0)))
        out_shape.append(jax.ShapeDtypeStruct(shape, dtype))
    res, lands = _pcall(body, grid=(n_rows // tr,), in_specs=in_specs, out_specs=out_specs, out_shape=out_shape,
                        operands=list(ins), name=name, comms=comms)
    return (res, lands) if comms else res


def _acc_out(ref, val):
    @pl.when(pl.program_id(0) == 0)
    def _():
        ref[...] = val

    @pl.when(pl.program_id(0) > 0)
    def _():
        ref[...] += val


def _rms_fwd(x, g, comms):
    T, D = x.shape

    def body(x_ref, g_ref, o_ref):
        xv = x_ref[...]
        o_ref[...] = (xv * _rms(xv) * g_ref[...]).astype(BF16)

    res, lands = _row_call(body, [x, g], [True, False], [((T, D), BF16, "row")], "rms_fwd", T, comms=comms)
    return res[0], lands


def _post1(x, mix, g2, g3, comms=()):
    T, D = x.shape

    def body(x_ref, mix_ref, g2_ref, g3_ref, h_ref, xn2_ref):
        mv = mix_ref[...]
        h = x_ref[...] + mv * _rms(mv) * g2_ref[...]
        h_ref[...] = h
        xn2_ref[...] = (h * _rms(h) * g3_ref[...]).astype(BF16)

    return _row_call(body, [x, mix, g2, g3], [True, True, False, False],
                     [((T, D), F32, "row"), ((T, D), BF16, "row")], "post1", T, comms=comms)


def _post2_loss(h, ff, g4, target):
    T, D = h.shape

    def body(h_ref, ff_ref, g4_ref, t_ref, dff_ref, dy_ref, dg4_ref, loss_ref):
        fv = ff_ref[...]
        r = _rms(fv)
        err = h_ref[...] + fv * r * g4_ref[...] - t_ref[...]
        dy = err * (1.0 / D)
        dy_ref[...] = dy
        dff, dwt = _rms_bwd(fv, r, g4_ref[...], dy)
        dff_ref[...] = dff.astype(BF16)
        _acc_out(dg4_ref, jnp.sum(dwt, axis=0, keepdims=True))
        part = 0.5 * jnp.sum(jnp.mean(err * err, axis=-1, keepdims=True), axis=0, keepdims=True)
        _acc_out(loss_ref, jnp.broadcast_to(part, (1, LANE)))

    return _row_call(body, [h, ff, g4, target], [True, True, False, True],
                     [((T, D), BF16, "row"), ((T, D), F32, "row"), ((1, D), F32, "acc"), ((1, LANE), F32, "acc")],
                     "post2_loss", T)


def _mid_bwd(h, mix, dy, dxn2, g2, g3, comms=()):
    T, D = h.shape

    def body(h_ref, mix_ref, dy_ref, dxn2_ref, g2_ref, g3_ref, dmix_ref, dh_ref, dg2_ref, dg3_ref):
        hv = h_ref[...]
        d1, dw3 = _rms_bwd(hv, _rms(hv), g3_ref[...], dxn2_ref[...])
        dh = dy_ref[...] + d1
        dh_ref[...] = dh
        mv = mix_ref[...]
        dmix, dw2 = _rms_bwd(mv, _rms(mv), g2_ref[...], dh)
        dmix_ref[...] = dmix.astype(BF16)
        _acc_out(dg2_ref, jnp.sum(dw2, axis=0, keepdims=True))
        _acc_out(dg3_ref, jnp.sum(dw3, axis=0, keepdims=True))

    return _row_call(body, [h, mix, dy, dxn2, g2, g3], [True, True, True, True, False, False],
                     [((T, D), BF16, "row"), ((T, D), F32, "row"), ((1, D), F32, "acc"), ((1, D), F32, "acc")],
                     "mid_bwd", T, comms=comms)


def _pre_bwd(x, dh, dxn, g1, comms=()):
    T, D = x.shape

    def body(x_ref, dh_ref, dxn_ref, g1_ref, gx_ref, dg1_ref):
        xv = x_ref[...]
        d1, dw1 = _rms_bwd(xv, _rms(xv), g1_ref[...], dxn_ref[...])
        gx_ref[...] = dh_ref[...] + d1
        _acc_out(dg1_ref, jnp.sum(dw1, axis=0, keepdims=True))

    return _row_call(body, [x, dh, dxn, g1], [True, True, True, False],
                     [((T, D), F32, "row"), ((1, D), F32, "acc")], "pre_bwd", T, comms=comms)


def _shift_down(x, s):
    if s == 0:
        return x
    row = lax.broadcasted_iota(jnp.int32, x.shape, 0)
    return jnp.where(row >= s, pltpu.roll(x, s, axis=0), 0.0)


def _shift_up(x, s):
    if s == 0:
        return x
    n = x.shape[0]
    row = lax.broadcasted_iota(jnp.int32, x.shape, 0)
    return jnp.where(row < n - s, pltpu.roll(x, n - s, axis=0), 0.0)


def _conv(x, w):
    kw = w.shape[0]
    out = w[kw - 1:kw, :] * x
    for j in range(kw - 1):
        out = out + w[j:j + 1, :] * _shift_down(x, kw - 1 - j)
    return out


def _conv_bwd(x, w, dout):
    kw = w.shape[0]
    dx = w[kw - 1:kw, :] * dout
    dws = []
    for j in range(kw - 1):
        dx = dx + w[j:j + 1, :] * _shift_up(dout, kw - 1 - j)
        dws.append(jnp.sum(dout * _shift_down(x, kw - 1 - j), axis=0, keepdims=True))
    dws.append(jnp.sum(dout * x, axis=0, keepdims=True))
    return dx, jnp.concatenate(dws, axis=0)


def _qkvconv_fwd(proj, w, bsz, seq, gw, comms=()):
    cw = QKV_CONV_COLS
    nct = gw // cw
    kw = w.shape[0]

    def body(p_ref, w_ref, o_ref):
        cv = _conv(p_ref[...], w_ref[...])
        o_ref[...] = (cv * _sigmoid(cv)).reshape(o_ref.shape)

    res, lands = _pcall(
        body, grid=(3, bsz, nct),
        in_specs=[pl.BlockSpec((seq, cw), lambda p, b, c: (b, p * nct + c)),
                  pl.BlockSpec((kw, cw), lambda p, b, c: (0, p * nct + c))],
        out_specs=[pl.BlockSpec((1, seq, cw), lambda p, b, c: (p, b, c))],
        out_shape=[jax.ShapeDtypeStruct((3, bsz * seq, gw), F32)],
        semantics=("parallel", "parallel", "parallel"), operands=[proj, w], name="qkvconv_fwd", comms=comms)
    return res[0], lands


def _qkvconv_bwd(proj, w, dact, bsz, seq, gw, comms=()):
    cw = QKV_CONV_COLS
    nct = gw // cw
    kw = w.shape[0]

    def body(p_ref, w_ref, d_ref, dp_ref, dw_ref):
        pre = p_ref[...]
        wv = w_ref[...]
        cv = _conv(pre, wv)
        sg = _sigmoid(cv)
        dcv = d_ref[...].reshape(cv.shape) * (sg * (1.0 + cv * (1.0 - sg)))
        dpre, dw = _conv_bwd(pre, wv, dcv)
        dp_ref[...] = dpre.astype(BF16)
        b = pl.program_id(2)

        @pl.when(b == 0)
        def _():
            dw_ref[...] = dw

        @pl.when(b > 0)
        def _():
            dw_ref[...] += dw

    res, lands = _pcall(
        body, grid=(3, nct, bsz),
        in_specs=[pl.BlockSpec((seq, cw), lambda p, c, b: (b, p * nct + c)),
                  pl.BlockSpec((kw, cw), lambda p, c, b: (0, p * nct + c)),
                  pl.BlockSpec((1, seq, cw), lambda p, c, b: (p, b, c))],
        out_specs=[pl.BlockSpec((seq, cw), lambda p, c, b: (b, p * nct + c)),
                   pl.BlockSpec((kw, cw), lambda p, c, b: (0, p * nct + c))],
        out_shape=[jax.ShapeDtypeStruct((bsz * seq, 3 * gw), BF16), jax.ShapeDtypeStruct((kw, 3 * gw), F32)],
        semantics=("parallel", "parallel", "arbitrary"), operands=[proj, w, dact], name="qkvconv_bwd", comms=comms)
    return res, lands


def _sc_fwd(proj, w, bsz, seq, scw, col0):
    nct = scw // LANE
    c0 = col0 // LANE
    kw = w.shape[0]

    def body(b_ref, c_ref, h_ref, w_ref, o_ref):
        o_ref[...] = (b_ref[...] * _conv(c_ref[...] * h_ref[...], w_ref[...])).astype(BF16)

    return pl.pallas_call(
        body, grid=(bsz, nct),
        in_specs=[pl.BlockSpec((seq, LANE), lambda b, c: (b, c0 + c)),
                  pl.BlockSpec((seq, LANE), lambda b, c: (b, c0 + nct + c)),
                  pl.BlockSpec((seq, LANE), lambda b, c: (b, c0 + 2 * nct + c)),
                  pl.BlockSpec((kw, LANE), lambda b, c: (0, c))],
        out_specs=pl.BlockSpec((seq, LANE), lambda b, c: (b, c)),
        out_shape=jax.ShapeDtypeStruct((bsz * seq, scw), BF16),
        compiler_params=_params("parallel", "parallel"), name="sc_fwd")(proj, proj, proj, w)


def _sc_bwd(proj, w, dout, bsz, seq, scw, col0, dcol0, comms=()):
    nct = scw // LANE
    c0 = col0 // LANE
    d0 = dcol0 // LANE
    kw = w.shape[0]

    def body(b_ref, c_ref, h_ref, w_ref, d_ref, db_ref, dc_ref, dh_ref, dw_ref):
        cc, hh, wv, dv = c_ref[...], h_ref[...], w_ref[...], d_ref[...]
        m = cc * hh
        db_ref[...] = (dv * _conv(m, wv)).astype(BF16)
        dm, dw = _conv_bwd(m, wv, dv * b_ref[...])
        dc_ref[...] = (dm * hh).astype(BF16)
        dh_ref[...] = (dm * cc).astype(BF16)
        b = pl.program_id(1)

        @pl.when(b == 0)
        def _():
            dw_ref[...] = dw

        @pl.when(b > 0)
        def _():
            dw_ref[...] += dw

    res, lands = _pcall(
        body, grid=(nct, bsz),
        in_specs=[pl.BlockSpec((seq, LANE), lambda c, b: (b, c0 + c)),
                  pl.BlockSpec((seq, LANE), lambda c, b: (b, c0 + nct + c)),
                  pl.BlockSpec((seq, LANE), lambda c, b: (b, c0 + 2 * nct + c)),
                  pl.BlockSpec((kw, LANE), lambda c, b: (0, c)),
                  pl.BlockSpec((seq, LANE), lambda c, b: (b, d0 + c))],
        out_specs=[pl.BlockSpec((seq, LANE), lambda c, b: (b, c)),
                   pl.BlockSpec((seq, LANE), lambda c, b: (b, c)),
                   pl.BlockSpec((seq, LANE), lambda c, b: (b, c)),
                   pl.BlockSpec((kw, LANE), lambda c, b: (0, c))],
        out_shape=[jax.ShapeDtypeStruct((bsz * seq, scw), BF16)] * 3 + [jax.ShapeDtypeStruct((kw, scw), F32)],
        semantics=("parallel", "arbitrary"), operands=[proj, proj, proj, w, dout], name="sc_bwd", comms=comms)
    return res, lands


HEADS_PER_STEP = 16


def _colsel(tile, idx):
    lane = lax.broadcasted_iota(jnp.int32, tile.shape, 1)
    return jnp.sum(jnp.where(lane == idx, tile, 0.0), axis=1, keepdims=True)


def _rowsel(tile, idx):
    row = lax.broadcasted_iota(jnp.int32, tile.shape, 0)
    return jnp.sum(jnp.where(row == idx, tile, 0.0), axis=0, keepdims=True)


def _colput(col, idx, width=LANE):
    lane = lax.broadcasted_iota(jnp.int32, (col.shape[0], width), 1)
    return jnp.where(lane == idx, col, 0.0)


def _tri_masks(c):
    row = lax.broadcasted_iota(jnp.int32, (c, c), 0)
    col = lax.broadcasted_iota(jnp.int32, (c, c), 1)
    return row >= col, row > col, row == col


def _unit_lower_inverses(ms):
    c = ms[0].shape[0]
    _, _, eye = _tri_masks(c)
    ps = [-m for m in ms]
    ts = [jnp.where(eye, 1.0, 0.0) + p for p in ps]
    for _ in range(int(math.log2(c)) - 1):
        ps = [_mdot(p, p) for p in ps]
        ts = [t + _mdot(t, p) for t, p in zip(ts, ps)]
    return ts


def _gates(ab, alog, dtb):
    g = -jnp.exp(alog) * _softplus(ab + dtb)
    return g, _sigmoid(ab)


def _l2n(x):
    r = lax.rsqrt(jnp.sum(x * x, axis=-1, keepdims=True) + L2_EPS)
    return x * r, r


def _gdn_chunk_common(q, k, gc, gr, bc):
    c, dk = q.shape
    incl, strict, _ = _tri_masks(c)
    qh, rq = _l2n(q)
    kn, rk = _l2n(k)
    qn = qh * (dk ** -0.5)
    dm = jnp.where(incl, jnp.exp(jnp.where(incl, gc - gr, 0.0)), 0.0)
    kk = _bdot(kn, kn, NT)
    m = jnp.where(strict, bc * kk * dm, 0.0)
    pm = jnp.where(incl, _bdot(qn, kn, NT) * dm, 0.0)
    return qh, rq, kn, rk, qn, dm, kk, m, pm


def _gdn_fwd(qkv, proj, alog, dtb, gnw, bsz, seq, heads, z_col0, ab_col0, comms=()):
    c = CHUNK
    nch = seq // c
    hb = min(HEADS_PER_STEP, heads)
    ng = heads // hb
    hd = qkv.shape[2] // heads
    wb = hb * hd

    def body(qkv_ref, z_ref, ab_ref, alog_ref, dtb_ref, gnw_ref, o_ref, ssave_ref, tsave_ref, s_ref, gc_ref, gt_ref, be_ref):
        n, hg = pl.program_id(1), pl.program_id(2)

        @pl.when((n == 0) & (hg == 0))
        def _():
            s_ref[...] = jnp.zeros_like(s_ref)

        @pl.when(hg == 0)
        def _():
            g, beta = _gates(ab_ref[...], alog_ref[...], dtb_ref[...])
            incl, _, _ = _tri_masks(c)
            gcum = _hdot(jnp.where(incl, 1.0, 0.0), g)
            gc_ref[...] = gcum
            gt_ref[...] = gcum.T
            be_ref[...] = beta

        gc_t, gt_t, be_t, gnw_v = gc_ref[...], gt_ref[...], be_ref[...], gnw_ref[...]
        hs = range(hb)
        sls = [slice(hh * hd, (hh + 1) * hd) for hh in hs]
        states = [s_ref[hg * hb + hh] for hh in hs]
        gcs = [_colsel(gc_t, hg * hb + hh) for hh in hs]
        grs = [_rowsel(gt_t, hg * hb + hh) for hh in hs]
        bcs = [_colsel(be_t, heads + hg * hb + hh) for hh in hs]
        com = [_gdn_chunk_common(qkv_ref[0, :, sls[hh]], qkv_ref[1, :, sls[hh]], gcs[hh], grs[hh], bcs[hh]) for hh in hs]
        kns, qns, pms = [cm[2] for cm in com], [cm[4] for cm in com], [cm[8] for cm in com]
        tms = _unit_lower_inverses([cm[7] for cm in com])
        gams = [jnp.exp(gc) for gc in gcs]
        glasts = [gc[c - 1:c, :] for gc in gcs]
        kss = [_bdot(kns[hh], states[hh]) for hh in hs]
        qss = [_bdot(qns[hh], states[hh]) for hh in hs]
        vns = [_bdot(tms[hh], bcs[hh] * (qkv_ref[2, :, sls[hh]] - gams[hh] * kss[hh])) for hh in hs]
        os_ = [gams[hh] * qss[hh] + _bdot(pms[hh], vns[hh]) for hh in hs]
        snews = [states[hh] * jnp.exp(glasts[hh]) + _bdot(kns[hh] * jnp.exp(glasts[hh] - gcs[hh]), vns[hh], TN) for hh in hs]
        for hh in hs:
            o = os_[hh]
            on = o * lax.rsqrt(jnp.mean(o * o, axis=-1, keepdims=True) + NORM_EPS) * gnw_v
            zz = z_ref[:, sls[hh]]
            ssave_ref[0, 0, hh] = states[hh]
            tsave_ref[0, 0, hh] = tms[hh]
            s_ref[hg * hb + hh] = snews[hh]
            o_ref[:, sls[hh]] = (on * (zz * _sigmoid(zz))).astype(BF16)

    row = lambda b, n, g: b * nch + n
    return _pcall(
        body, grid=(bsz, nch, ng),
        in_specs=[pl.BlockSpec((3, c, wb), lambda b, n, g: (0, row(b, n, g), g)),
                  pl.BlockSpec((c, wb), lambda b, n, g: (row(b, n, g), z_col0 // wb + g)),
                  pl.BlockSpec((c, LANE), lambda b, n, g: (row(b, n, g), ab_col0 // LANE)),
                  pl.BlockSpec((1, LANE), lambda b, n, g: (0, 0)),
                  pl.BlockSpec((1, LANE), lambda b, n, g: (0, 0)),
                  pl.BlockSpec((1, hd), lambda b, n, g: (0, 0))],
        out_specs=[pl.BlockSpec((c, wb), lambda b, n, g: (row(b, n, g), g)),
                   pl.BlockSpec((1, 1, hb, hd, hd), lambda b, n, g: (b, n, g, 0, 0)),
                   pl.BlockSpec((1, 1, hb, c, c), lambda b, n, g: (b, n, g, 0, 0))],
        out_shape=[jax.ShapeDtypeStruct((bsz * seq, heads * hd), BF16),
                   jax.ShapeDtypeStruct((bsz, nch, heads, hd, hd), F32),
                   jax.ShapeDtypeStruct((bsz, nch, heads, c, c), F32)],
        scratch_shapes=[pltpu.VMEM((heads, hd, hd), F32), pltpu.VMEM((c, LANE), F32), pltpu.VMEM((LANE, c), F32),
                        pltpu.VMEM((c, LANE), F32)],
        semantics=("parallel", "arbitrary", "arbitrary"), operands=[qkv, proj, proj, alog, dtb, gnw], name="gdn_fwd",
        comms=comms)


def _gdn_bwd(qkv, proj, alog, dtb, gnw, ssave, tsave, dout, bsz, seq, heads, z_col0, ab_col0, comms=()):
    c = CHUNK
    nch = seq // c
    hb = min(HEADS_PER_STEP, heads)
    ng = heads // hb
    hd = qkv.shape[2] // heads
    wb = hb * hd

    def body(qkv_ref, z_ref, ab_ref, alog_ref, dtb_ref, gnw_ref, ssave_ref, tsave_ref, do_ref,
             dact_ref, dz_ref, dab_ref, dalog_ref, ddtb_ref, dgnw_ref,
             ds_ref, gc_ref, gt_ref, be_ref, dgacc_ref, dbacc_ref):
        n, hg = pl.program_id(1), pl.program_id(2)
        incl, strict, _ = _tri_masks(c)

        @pl.when((n == 0) & (hg == 0))
        def _():
            ds_ref[...] = jnp.zeros_like(ds_ref)
            dalog_ref[...] = jnp.zeros_like(dalog_ref)
            ddtb_ref[...] = jnp.zeros_like(ddtb_ref)
            dgnw_ref[...] = jnp.zeros_like(dgnw_ref)

        @pl.when(hg == 0)
        def _():
            g, beta = _gates(ab_ref[...], alog_ref[...], dtb_ref[...])
            gcum = _hdot(jnp.where(incl, 1.0, 0.0), g)
            gc_ref[...] = gcum
            gt_ref[...] = gcum.T
            be_ref[...] = beta
            dgacc_ref[...] = jnp.zeros_like(dgacc_ref)
            dbacc_ref[...] = jnp.zeros_like(dbacc_ref)

        gc_t, gt_t, be_t, gnw_v = gc_ref[...], gt_ref[...], be_ref[...], gnw_ref[...]
        hs = range(hb)

        def each(f):
            return [f(hh) for hh in hs]

        rsum = lambda a: jnp.sum(a, axis=-1, keepdims=True)
        sls = each(lambda i: slice(i * hd, (i + 1) * hd))
        ds_in = each(lambda i: ds_ref[hg * hb + i])
        gc = each(lambda i: _colsel(gc_t, hg * hb + i))
        gr = each(lambda i: _rowsel(gt_t, hg * hb + i))
        bc = each(lambda i: _colsel(be_t, heads + hg * hb + i))
        com = each(lambda i: _gdn_chunk_common(qkv_ref[0, :, sls[i]], qkv_ref[1, :, sls[i]], gc[i], gr[i], bc[i]))
        qh, rq, kn, rk, qn, dm, kk, m, pm = [[cm[j] for cm in com] for j in range(9)]
        tm = each(lambda i: tsave_ref[0, 0, i])
        s = each(lambda i: ssave_ref[0, 0, i])
        gam = each(lambda i: jnp.exp(gc[i]))
        glast = each(lambda i: gc[i][c - 1:c, :])
        gl = each(lambda i: jnp.exp(glast[i]))
        ratio = each(lambda i: jnp.exp(glast[i] - gc[i]))
        ks = each(lambda i: _bdot(kn[i], s[i]))
        qs = each(lambda i: _bdot(qn[i], s[i]))
        r = each(lambda i: qkv_ref[2, :, sls[i]] - gam[i] * ks[i])
        vn = each(lambda i: _bdot(tm[i], bc[i] * r[i]))
        o = each(lambda i: gam[i] * qs[i] + _bdot(pm[i], vn[i]))
        ro = each(lambda i: lax.rsqrt(jnp.mean(o[i] * o[i], axis=-1, keepdims=True) + NORM_EPS))
        zz = each(lambda i: z_ref[:, sls[i]])
        sz = each(lambda i: _sigmoid(zz[i]))
        dd = each(lambda i: do_ref[:, sls[i]])
        don = each(lambda i: dd[i] * (zz[i] * sz[i]))
        dz_h = each(lambda i: (dd[i] * (o[i] * ro[i] * gnw_v) * (sz[i] * (1.0 + zz[i] * (1.0 - sz[i])))).astype(BF16))
        dgnw = sum(each(lambda i: jnp.sum(don[i] * o[i] * ro[i], axis=0, keepdims=True)))
        uu = each(lambda i: don[i] * gnw_v)
        d_o = each(lambda i: ro[i] * uu[i] - o[i] * (ro[i] * ro[i] * ro[i]) * jnp.mean(o[i] * uu[i], axis=-1, keepdims=True))
        dqs = each(lambda i: gam[i] * d_o[i])
        dq = each(lambda i: _bdot(dqs[i], s[i], NT))
        ds_new = each(lambda i: _bdot(qn[i], dqs[i], TN))
        dp = each(lambda i: jnp.where(incl, _bdot(d_o[i], vn[i], NT), 0.0))
        dvn = each(lambda i: _bdot(pm[i], d_o[i], TN))
        dgam = each(lambda i: rsum(d_o[i] * qs[i]))
        dkd = each(lambda i: _bdot(vn[i], ds_in[i], NT))
        dvn = each(lambda i: dvn[i] + _bdot(kn[i] * ratio[i], ds_in[i]))
        ds_new = each(lambda i: ds_new[i] + gl[i] * ds_in[i])
        dgl = each(lambda i: jnp.sum(jnp.sum(ds_in[i] * s[i], axis=1, keepdims=True), axis=0, keepdims=True))
        dratio = each(lambda i: rsum(dkd[i] * kn[i]))
        dpd = each(lambda i: dp[i] * dm[i])
        dq = each(lambda i: dq[i] + _bdot(dpd[i], kn[i]))
        dk = each(lambda i: ratio[i] * dkd[i] + _bdot(dpd[i], qn[i], TN))
        dx = each(lambda i: _bdot(tm[i], dvn[i], TN))
        dr = each(lambda i: bc[i] * dx[i])
        gdr = each(lambda i: gam[i] * dr[i])
        dk = each(lambda i: dk[i] - _bdot(gdr[i], s[i], NT))
        ds_new = each(lambda i: ds_new[i] - _bdot(kn[i], gdr[i], TN))
        dmm = each(lambda i: jnp.where(strict, -_bdot(dx[i], vn[i], NT), 0.0))
        ee = each(lambda i: dmm[i] * dm[i])
        be_e = each(lambda i: bc[i] * ee[i])
        dk = each(lambda i: dk[i] + _bdot(be_e[i], kn[i]) + _bdot(be_e[i], kn[i], TN))
        dbeta = each(lambda i: rsum(dx[i] * r[i]) + rsum(ee[i] * kk[i]))
        dgam = each(lambda i: dgam[i] - rsum(dr[i] * ks[i]))
        ff = each(lambda i: dp[i] * pm[i] + dmm[i] * m[i])
        rowi = lax.broadcasted_iota(jnp.int32, (c, 1), 0)
        dgc = each(lambda i: rsum(ff[i]) - rsum(ff[i].T) + dgam[i] * gam[i] - dratio[i] * ratio[i]
                   + jnp.where(rowi == c - 1, jnp.sum(dratio[i] * ratio[i], axis=0, keepdims=True) + dgl[i] * gl[i], 0.0))
        dg_tile = sum(each(lambda i: _colput(dgc[i], hg * hb + i)))
        db_tile = sum(each(lambda i: _colput(dbeta[i], heads + hg * hb + i)))
        for i in hs:
            dqh = dq[i] * (hd ** -0.5)
            ds_ref[hg * hb + i] = ds_new[i]
            dz_ref[:, sls[i]] = dz_h[i]
            dact_ref[0, :, sls[i]] = rq[i] * (dqh - qh[i] * rsum(qh[i] * dqh))
            dact_ref[1, :, sls[i]] = rk[i] * (dk[i] - kn[i] * rsum(kn[i] * dk[i]))
            dact_ref[2, :, sls[i]] = dr[i]
        dgacc_ref[...] += dg_tile
        dbacc_ref[...] += db_tile
        dgnw_ref[0] += dgnw

        @pl.when(hg == ng - 1)
        def _():
            ab = ab_ref[...]
            ea = jnp.exp(alog_ref[...])
            g = -ea * _softplus(ab + dtb_ref[...])
            beta = be_ref[...]
            dg = _hdot(jnp.where(incl, 1.0, 0.0), dgacc_ref[...], TN)
            lane = lax.broadcasted_iota(jnp.int32, ab.shape, 1)
            da = jnp.where(lane < heads, dg * (-ea) * _sigmoid(ab + dtb_ref[...]), 0.0)
            db = dbacc_ref[...] * beta * (1.0 - beta)
            dab_ref[...] = (da + db).astype(BF16)
            dalog_ref[0] += jnp.sum(jnp.where(lane < heads, dg * g, 0.0), axis=0, keepdims=True)
            ddtb_ref[0] += jnp.sum(da, axis=0, keepdims=True)

    row = lambda b, n, g: b * nch + (nch - 1 - n)
    rev = lambda n: nch - 1 - n
    return _pcall(
        body, grid=(bsz, nch, ng),
        in_specs=[pl.BlockSpec((3, c, wb), lambda b, n, g: (0, row(b, n, g), g)),
                  pl.BlockSpec((c, wb), lambda b, n, g: (row(b, n, g), z_col0 // wb + g)),
                  pl.BlockSpec((c, LANE), lambda b, n, g: (row(b, n, g), ab_col0 // LANE)),
                  pl.BlockSpec((1, LANE), lambda b, n, g: (0, 0)),
                  pl.BlockSpec((1, LANE), lambda b, n, g: (0, 0)),
                  pl.BlockSpec((1, hd), lambda b, n, g: (0, 0)),
                  pl.BlockSpec((1, 1, hb, hd, hd), lambda b, n, g: (b, rev(n), g, 0, 0)),
                  pl.BlockSpec((1, 1, hb, c, c), lambda b, n, g: (b, rev(n), g, 0, 0)),
                  pl.BlockSpec((c, wb), lambda b, n, g: (row(b, n, g), g))],
        out_specs=[pl.BlockSpec((3, c, wb), lambda b, n, g: (0, row(b, n, g), g)),
                   pl.BlockSpec((c, wb), lambda b, n, g: (row(b, n, g), g)),
                   pl.BlockSpec((c, LANE), lambda b, n, g: (row(b, n, g), 0)),
                   pl.BlockSpec((1, 1, LANE), lambda b, n, g: (b, 0, 0)),
                   pl.BlockSpec((1, 1, LANE), lambda b, n, g: (b, 0, 0)),
                   pl.BlockSpec((1, 1, hd), lambda b, n, g: (b, 0, 0))],
        out_shape=[jax.ShapeDtypeStruct((3, bsz * seq, heads * hd), F32),
                   jax.ShapeDtypeStruct((bsz * seq, heads * hd), BF16),
                   jax.ShapeDtypeStruct((bsz * seq, LANE), BF16),
                   jax.ShapeDtypeStruct((bsz, 1, LANE), F32),
                   jax.ShapeDtypeStruct((bsz, 1, LANE), F32),
                   jax.ShapeDtypeStruct((bsz, 1, hd), F32)],
        scratch_shapes=[pltpu.VMEM((heads, hd, hd), F32), pltpu.VMEM((c, LANE), F32), pltpu.VMEM((LANE, c), F32),
                        pltpu.VMEM((c, LANE), F32), pltpu.VMEM((c, LANE), F32), pltpu.VMEM((c, LANE), F32)],
        semantics=("parallel", "arbitrary", "arbitrary"),
        operands=[qkv, proj, proj, alog, dtb, gnw, ssave, tsave, dout], name="gdn_bwd", comms=comms)


ELEMWISE_BLOCK_ELEMS = 256 * 1024


def _rows_tile(rows, cols):
    want = max(16, ELEMWISE_BLOCK_ELEMS // cols)
    if rows <= want:
        return rows
    t = (want // 16) * 16
    while t > 16 and rows % t:
        t -= 16
    return t if rows % t == 0 else rows


def _piece_specs(pieces, tr, cols):
    specs, leads = [], []
    for p, (arr, lead) in enumerate(pieces):
        if arr.ndim == 3:
            specs.append(pl.BlockSpec((1, tr, cols), functools.partial(lambda i, idx, p: (idx[p], i, 0), p=p)))
        else:
            specs.append(pl.BlockSpec((tr, cols), lambda i, idx: (i, 0)))
        leads.append(jnp.asarray(0 if lead is None else lead, jnp.int32))
    return jnp.stack(leads), specs


def _sum_pieces(refs):
    total = None
    for r in refs:
        v = r[...].astype(F32)
        v = v.reshape(v.shape[-2:])
        total = v if total is None else total + v
    return total


def _adamw(w, m, v, pieces, name, comms=()):
    rows, cols = w.shape
    tr = _rows_tile(rows, cols)
    leads, pspecs = _piece_specs(pieces, tr, cols)
    npc = len(pieces)
    c1 = 1.0 - ADAM_B1 ** ADAM_STEP
    c2 = 1.0 - ADAM_B2 ** ADAM_STEP

    def body(idx_ref, w_ref, m_ref, v_ref, *rest):
        g = _sum_pieces(rest[:npc])
        g_ref, d_ref, nm_ref, nv_ref = rest[npc:]
        nm = ADAM_B1 * m_ref[...] + (1.0 - ADAM_B1) * g
        nv = ADAM_B2 * v_ref[...] + (1.0 - ADAM_B2) * (g * g)
        g_ref[...] = g
        nm_ref[...] = nm
        nv_ref[...] = nv
        d_ref[...] = -ADAM_LR * ((nm / c1) / (jnp.sqrt(nv / c2) + ADAM_EPS) + ADAM_WD * w_ref[...])

    wspec = pl.BlockSpec((tr, cols), lambda i, idx: (i, 0))
    res, lands = _pcall(body, grid=(rows // tr,), in_specs=[wspec] * 3 + pspecs, out_specs=[wspec] * 4,
                        out_shape=[jax.ShapeDtypeStruct((rows, cols), F32)] * 4, semantics=("parallel",),
                        prefetch=[leads], operands=[w, m, v, *[p for p, _ in pieces]], name=name, comms=comms)
    return (res, lands) if comms else res


def _sum_to(pieces, out_dtype, name):
    arr0 = pieces[0][0]
    rows, cols = arr0.shape[-2:]
    tr = _rows_tile(rows, cols)
    leads, pspecs = _piece_specs(pieces, tr, cols)

    def body(idx_ref, *rest):
        rest[-1][...] = _sum_pieces(rest[:-1]).astype(out_dtype)

    return pl.pallas_call(
        body,
        grid_spec=pltpu.PrefetchScalarGridSpec(num_scalar_prefetch=1, grid=(rows // tr,), in_specs=pspecs,
                                               out_specs=pl.BlockSpec((tr, cols), lambda i, idx: (i, 0))),
        out_shape=jax.ShapeDtypeStruct((rows, cols), out_dtype),
        compiler_params=_params("parallel"), name=name)(leads, *[p for p, _ in pieces])


def _pair_add(a, recv, place, name, comms=()):
    _, rows, cols = a.shape
    tr = _rows_tile(rows, cols)
    x, y, c = place
    idx = jnp.stack([2 * (1 - x) + y, 2 * x + (1 - y), 2 * (1 - x) + (1 - y), c]).astype(jnp.int32)

    def body(p_ref, a_ref, r_ref, o_ref):
        o_ref[...] = (a_ref[...].astype(F32) + r_ref[...].astype(F32)).astype(BF16)

    res, lands = _pcall(
        body, grid=(3, rows // tr),
        in_specs=[pl.BlockSpec((1, tr, cols), lambda j, i, p: (2 * p[j] + p[3], i, 0)),
                  pl.BlockSpec((1, tr, cols), lambda j, i, p: (p[j], i, 0))],
        out_specs=[pl.BlockSpec((1, tr, cols), lambda j, i, p: (j, i, 0))],
        out_shape=[jax.ShapeDtypeStruct((3, rows, cols), BF16)], semantics=("parallel", "parallel"),
        prefetch=[idx], operands=[a, recv], name=name, comms=comms)
    return (res[0], lands) if comms else res[0]


def _to_frame(w, offs, fw, name):
    rows, n = w.shape
    tr = _tile(rows, 256)

    def body(off_ref, w_ref, o_ref, pad_ref):
        pad_ref[...] = jnp.zeros_like(pad_ref)
        pad_ref[:, 0:n] = w_ref[...]
        y = pad_ref[...]
        off1, len1, off2 = off_ref[0], off_ref[1], off_ref[2]
        col = lax.broadcasted_iota(jnp.int32, y.shape, 1)
        o_ref[0] = jnp.where(col < off1 + len1, pltpu.roll(y, off1, axis=1),
                             jnp.where(col >= off2 + len1, pltpu.roll(y, off2, axis=1), 0.0)).astype(BF16)

    res, _ = _pcall(body, grid=(rows // tr,), in_specs=[pl.BlockSpec((tr, n), lambda i, o: (i, 0))],
                    out_specs=[pl.BlockSpec((1, tr, fw), lambda i, o: (o[3], i, 0))],
                    out_shape=[jax.ShapeDtypeStruct((N_DEV, rows, fw), BF16)], scratch_shapes=[pltpu.VMEM((tr, fw), F32)],
                    semantics=("parallel",), prefetch=[offs], operands=[w], name=name)
    return res[0]


def _in_proj_core(xn, frames, ids, lay, into, name, comms=()):
    t, d = xn.shape
    n = ids.shape[0]
    tm = _tile(t, 1024)
    nb = max(b for b in range(1, lay.nc + 1) if lay.nc % b == 0 and b * MXU <= 768)
    tn, nj = nb * MXU, lay.nc // nb
    by_id = not isinstance(frames, (list, tuple))
    fr = [frames] if by_id else list(frames)
    n_fr = len(fr)

    def body(ids_ref, a_ref, *rest):
        o_ref = rest[-1]
        if by_id:
            o_ref[...] = jnp.dot(a_ref[...], rest[0][0], preferred_element_type=F32)
            return
        f = pl.program_id(1) // nj
        for k in range(n_fr):
            @pl.when(f == k)
            def _(k=k):
                o_ref[...] = jnp.dot(a_ref[...], rest[k][...], preferred_element_type=F32)

    core = lambda j: pl.multiple_of(lay.c0 * MXU + j * tn, MXU)
    col = lambda s, j: pl.multiple_of(lay.frame_block(s) * MXU + lay.c0 * MXU + j * tn, MXU)
    if by_id:
        b_specs = [pl.BlockSpec((pl.Element(1), pl.Element(d), pl.Element(tn)), lambda i, fj, ids: (ids[fj // nj], 0, core(fj % nj)))]
    else:
        b_specs = [pl.BlockSpec((pl.Element(d), pl.Element(tn)),
                                functools.partial(lambda i, fj, ids, k: (0, core(jnp.clip(fj - k * nj, 0, nj - 1))), k=k))
                   for k in range(n_fr)]
    n_into = 0 if into is None else 1
    res, lands = _pcall(
        body, grid=(t // tm, n * nj),
        in_specs=[pl.BlockSpec((tm, d), lambda i, fj, ids: (i, 0))] + b_specs + [pl.BlockSpec(memory_space=pl.ANY)] * n_into,
        out_specs=[pl.BlockSpec((pl.Element(tm), pl.Element(tn)), lambda i, fj, ids: (i * tm, col(ids[fj // nj], fj % nj)))],
        out_shape=[jax.ShapeDtypeStruct((t, lay.wp), F32)], semantics=("parallel", "arbitrary"),
        prefetch=[ids], operands=[xn] + fr + ([into] if n_into else []), name=name, comms=comms,
        fill={1 + n_fr: 0} if n_into else None)
    return res[0], lands


def _in_proj_rest(xn, frames, table, wp, into, name, comms=()):
    t, d = xn.shape
    tm = _tile(t, 1024)
    n = table.shape[1]

    def body(t_ref, a_ref, b1_ref, b2_ref, *rest):
        j = pl.program_id(1)
        b = b1_ref[0]
        b = jnp.where(t_ref[5, j] > 0, b + b2_ref[0], b)
        rest[-1][...] = jnp.dot(a_ref[...], b, preferred_element_type=F32)

    n_into = 0 if into is None else 1
    res, lands = _pcall(
        body, grid=(t // tm, n),
        in_specs=[pl.BlockSpec((tm, d), lambda i, j, tb: (i, 0)),
                  pl.BlockSpec((1, d, MXU), lambda i, j, tb: (tb[1, j], 0, tb[2, j])),
                  pl.BlockSpec((1, d, MXU), lambda i, j, tb: (tb[3, j], 0, tb[4, j]))]
        + [pl.BlockSpec(memory_space=pl.ANY)] * n_into,
        out_specs=[pl.BlockSpec((tm, MXU), lambda i, j, tb: (i, tb[0, j]))],
        out_shape=[jax.ShapeDtypeStruct((t, wp), F32)], semantics=("parallel", "arbitrary"),
        prefetch=[table], operands=[xn, frames, frames] + ([into] if n_into else []), name=name, comms=comms,
        fill={3: 0} if n_into else None)
    return res[0], lands


def _place():
    x, y, c = lax.axis_index("x"), lax.axis_index("y"), lax.axis_index("c")
    chips = [(1 - x, y), (x, 1 - y), (1 - x, 1 - y)]
    return x, y, c, chips


def _allreduce_small(buf, name):
    rows = buf.shape[0]

    def body(x_ref, o_ref, g_ref, send_sems, recv_sems):
        x, y, c, chips = _place()
        me, sibling = (x, y, c), (x, y, 1 - c)

        def copy(k, block, to, src=None):
            dst = g_ref.at[4 * block[0] + 2 * block[1] + block[2]]
            return pltpu.make_async_remote_copy(src_ref=dst if src is None else src, dst_ref=dst,
                                                send_sem=send_sems.at[k], recv_sem=recv_sems.at[k],
                                                device_id=to, device_id_type=MESH)

        first = [copy(0, me, sibling, src=x_ref)]
        first += [copy(1 + j, me, (*chip, c), src=x_ref) for j, chip in enumerate(chips)]
        for cp in first:
            cp.start()
        passed = [copy(4 + j, (*chip, c), sibling) for j, chip in enumerate(chips)]
        for j, chip in enumerate(chips):
            copy(1 + j, (*chip, c), me).wait_recv()
            passed[j].start()
        copy(0, sibling, me).wait_recv()
        for j, chip in enumerate(chips):
            copy(4 + j, (*chip, 1 - c), me).wait_recv()
        for cp in first + passed:
            cp.wait_send()
        g_ref[4 * x + 2 * y + c] = x_ref[...]
        total = g_ref[0]
        for s in range(1, N_DEV):
            total = total + g_ref[s]
        o_ref[...] = total

    vm = pl.BlockSpec(memory_space=pltpu.VMEM)
    return pl.pallas_call(
        body, in_specs=[vm], out_specs=vm, out_shape=jax.ShapeDtypeStruct((rows, LANE), F32),
        scratch_shapes=[pltpu.VMEM((N_DEV, rows, LANE), F32), pltpu.SemaphoreType.DMA((7,)), pltpu.SemaphoreType.DMA((7,))],
        name=name)(buf)


def _rows(ref, rows):
    return ref if rows is None else ref.at[pl.ds(rows[0], rows[1] - rows[0])]


AG_ALL = ("here", "sibling", 0, 1, 2)


def _ag(shard=None, into=None, to=(), forward=(), rows=None):
    def plan(srcs, lands):
        x, y, c, chips = _place()
        buf = lands[0]
        out = []
        if to:
            dst = _rows(buf.at[4 * x + 2 * y + c], rows)
            src = dst if shard is None else _rows(srcs[0], rows)
            for who in to:
                if who == "here":
                    out.append(("local", src, dst, None))
                elif who == "sibling":
                    out.append(("remote", src, dst, (x, y, 1 - c)))
                else:
                    out.append(("remote", src, dst, (*chips[who], c)))
        for j in forward:
            r = _rows(buf.at[4 * chips[j][0] + 2 * chips[j][1] + c], rows)
            out.append(("remote", r, r, (x, y, 1 - c)))
        return out

    srcs = ([shard] if to and shard is not None else []) + ([into] if into is not None else [])
    land = jax.ShapeDtypeStruct(into.shape, into.dtype) if into is not None else jax.ShapeDtypeStruct((N_DEV,) + shard.shape, shard.dtype)
    return _Comm(srcs, [land], plan, len(to) + len(forward), alias={len(srcs) - 1: 0} if into is not None else None)


def _ag_first(shard, rows=None, into=None, to=AG_ALL):
    return _ag(shard=shard, into=into, to=to, rows=rows)


def _ag_second(g, rows=None, of=(0, 1, 2)):
    return _ag(into=g, forward=of, rows=rows)


def _rs_first(grad):
    def plan(srcs, lands):
        x, y, c, _ = _place()
        (a,), (land,) = srcs, lands
        return [("remote", a.at[2 * j + (1 - c)], land.at[j], (x, y, 1 - c)) for j in range(4)]

    return _Comm([grad], [jax.ShapeDtypeStruct((4,) + grad.shape[1:], grad.dtype)], plan, 4)


def _rs_second(pair, rows=None, into=None):
    def plan(srcs, lands):
        x, y, c, chips = _place()
        return [("remote", _rows(srcs[0].at[j], rows), _rows(lands[0].at[j], rows), (cx, cy, c))
                for j, (cx, cy) in enumerate(chips)]

    land = jax.ShapeDtypeStruct((3,) + pair.shape[1:], pair.dtype)
    if into is None:
        return _Comm([pair], [land], plan, 3)
    return _Comm([pair, into], [land], plan, 3, alias={1: 0})


class _InLayout:
    def __init__(self, n_in, gw, heads, scw):
        self.n_in, self.split = n_in, 4 * gw + 2 * heads
        self.gap = LANE - 2 * heads
        self.ab_col, self.sc_col = 4 * gw, 4 * gw + LANE
        self.used = 4 * gw + LANE + 3 * scw
        p0 = [s * n_in + (self.gap if s * n_in >= self.split else 0) for s in range(N_DEV)]
        self.fstart = [(p // MXU) * MXU for p in p0]
        need = []
        for s in range(N_DEV):
            straddle = s * n_in < self.split < (s + 1) * n_in
            need.append(p0[s] - self.fstart[s] + n_in + (self.gap if straddle else 0))
        self.fw = -(-max(need) // MXU) * MXU
        self.wp = max(f + self.fw for f in self.fstart)
        assert self.wp >= self.used and self.wp % MXU == 0
        nfb = self.fw // MXU
        rows = []
        for jb in range(self.wp // MXU):
            src = [(s, jb - self.fstart[s] // MXU) for s in range(N_DEV) if 0 <= jb - self.fstart[s] // MXU < nfb]
            assert 1 <= len(src) <= 2, (jb, src)
            (s1, b1), (s2, b2) = src[0], src[-1]
            rows.append((s1, b1, s2, b2, int(len(src) == 2)))
        self.table = np.asarray(rows, np.int32).T.copy()
        single = [all(rows[self.fstart[s] // MXU + b][4] == 0 for s in range(N_DEV)) for b in range(nfb)]
        runs, b = [], 0
        while b < nfb:
            if single[b]:
                e = b
                while e < nfb and single[e]:
                    e += 1
                runs.append((e - b, b))
                b = e
            else:
                b += 1
        self.nc, self.c0 = max(runs) if runs else (0, 0)
        in_core = {self.fstart[s] // MXU + b for s in range(N_DEV) for b in range(self.c0, self.c0 + self.nc)}
        self.rest = np.asarray([(jb,) + rows[jb] for jb in range(self.wp // MXU) if jb not in in_core], np.int32).T.copy()

    def frame_block(self, s):
        p = s * self.n_in
        return (p + jnp.where(p >= self.split, self.gap, 0)) // MXU

    def offsets(self, s):
        p = s * self.n_in
        after = p >= self.split
        off1 = p + jnp.where(after, self.gap, 0) - self.frame_block(s) * MXU
        len1 = jnp.where(after, self.n_in, jnp.clip(self.split - p, 0, self.n_in))
        off2 = off1 + jnp.where(len1 < self.n_in, self.gap, 0)
        return off1, len1, off2

    def to_frame(self, w, s):
        return _to_frame(w, jnp.stack(self.offsets(s) + (s,)).astype(jnp.int32), self.fw, "w_in_frame")

    def from_frame(self, f, s):
        off1, len1, off2 = self.offsets(s)
        a = lax.dynamic_slice(f, (0, off1), (f.shape[0], self.n_in))
        b = lax.dynamic_slice(f, (0, off2), (f.shape[0], self.n_in))
        col = lax.broadcasted_iota(jnp.int32, (1, self.n_in), 1)
        return jnp.where(col < len1, a, b)


def _pack_rows(parts):
    rows = []
    for p in parts:
        flat = p.reshape(-1)
        pad = (-flat.shape[0]) % LANE
        rows.append(jnp.pad(flat, (0, pad)).reshape(-1, LANE))
    buf = jnp.concatenate(rows, axis=0)
    return jnp.pad(buf, ((0, (-buf.shape[0]) % 8), (0, 0)))


def _unpack_rows(buf, shapes):
    out, r = [], 0
    for shp in shapes:
        size = int(np.prod(shp))
        nr = -(-size // LANE)
        out.append(buf[r:r + nr].reshape(-1)[:size].reshape(shp))
        r += nr
    return out


def _pad_lanes(v):
    return jnp.pad(v, ((0, 0), (0, LANE - v.shape[1])))


def kernel(x, norm_mix_pre, w_in, conv_qkv_w, a_log, dt_bias, gdn_norm_w, conv_sc_w, w_out, norm_mix_post, norm_mlp_pre, w_up, w_down, norm_mlp_post, loss_target, m_norm_mix_pre, m_w_in, m_conv_qkv_w, m_a_log, m_dt_bias, m_gdn_norm_w, m_conv_sc_w, m_w_out, m_norm_mix_post, m_norm_mlp_pre, m_w_up, m_w_down, m_norm_mlp_post, v_norm_mix_pre, v_w_in, v_conv_qkv_w, v_a_log, v_dt_bias, v_gdn_norm_w, v_conv_sc_w, v_w_out, v_norm_mix_post, v_norm_mlp_pre, v_w_up, v_w_down, v_norm_mlp_post):
    bsz, seq, d = x.shape
    t = bsz * seq
    heads, hd = a_log.shape[-1], gdn_norm_w.shape[-1]
    gw = heads * hd
    scw = conv_sc_w.shape[-1] * N_DEV
    dff_w = w_up.shape[-1] * N_DEV
    lay = _InLayout(w_in.shape[-1], gw, heads, scw)
    mx, my, mc = lax.axis_index("x"), lax.axis_index("y"), lax.axis_index("c")
    me = 4 * mx + 2 * my + mc
    chip = 2 * mx + my

    x2 = x.reshape(t, d)
    tgt = loss_target.reshape(t, d)
    g1, g2, g3, g4 = norm_mix_pre, norm_mix_post, norm_mlp_pre, norm_mlp_post

    g_in = lay.to_frame(w_in[0], me)
    w_out_b, w_up_b, w_down_b = w_out[0].astype(BF16), w_up[0].astype(BF16), w_down[0].astype(BF16)
    up_cols = dff_w // N_DEV
    qu, qd = d // 4, up_cols // 4
    kq, ks = conv_qkv_w.shape[1], conv_sc_w.shape[1]
    cq_n, cs_n = conv_qkv_w.shape[-1], conv_sc_w.shape[-1]
    cq_full = lax.dynamic_update_slice(jnp.zeros((kq, 3 * gw), F32), conv_qkv_w[0], (0, me * cq_n))
    cs_full = lax.dynamic_update_slice(jnp.zeros((ks, scw), F32), conv_sc_w[0], (0, me * cs_n))
    conv_q, conv_s = _unpack_rows(_allreduce_small(_pack_rows([cq_full, cs_full]), "allgather_conv"),
                                  [(kq, 3 * gw), (ks, scw)])
    alog_t, dtb_t = _pad_lanes(a_log), _pad_lanes(dt_bias)

    xn, (g_in,) = _rms_fwd(x2, g1, comms=[_ag(into=g_in, to=("sibling",))])

    def arrived(k):
        return [lax.dynamic_index_in_dim(g_in, s, 0, keepdims=False) for s in ids[k]]

    dev = lambda px, py, pc: (4 * px + 2 * py + pc).astype(jnp.int32)
    ids = [jnp.stack([dev(mx, my, mc), dev(mx, my, 1 - mc)]),
           jnp.stack([dev(1 - mx, my, mc), dev(mx, 1 - my, mc)]),
           jnp.stack([dev(1 - mx, my, 1 - mc), dev(mx, 1 - my, 1 - mc)]),
           jnp.stack([dev(1 - mx, 1 - my, mc), dev(1 - mx, 1 - my, 1 - mc)])]
    assert lay.nc > 0, "the frames have no columns of their own at these sizes"
    proj, (g_in,) = _in_proj_core(xn, arrived(0), ids[0], lay, None, "in_proj_0", comms=[_ag(into=g_in, to=(0, 1))])
    proj, (g_in,) = _in_proj_core(xn, arrived(1), ids[1], lay, proj, "in_proj_1",
                                  comms=[_ag(into=g_in, to=(2,), forward=(0, 1))])
    proj, (g_in, g_out) = _in_proj_core(xn, arrived(2), ids[2], lay, proj, "in_proj_2",
                                        comms=[_ag(into=g_in, forward=(2,)), _ag_first(w_out_b)])
    eu = qu // 2
    proj, (g_up,) = _in_proj_core(xn, g_in, ids[3], lay, proj, "in_proj_3", comms=[_ag_first(w_up_b, rows=(0, eu))])
    proj, (g_up,) = _in_proj_rest(xn, g_in, jnp.asarray(lay.rest), lay.wp, proj, "in_proj_rest",
                                  comms=[_ag_first(w_up_b, rows=(eu, 2 * eu), into=g_up)])
    qkv, (g_out, g_up) = _qkvconv_fwd(proj, conv_q, bsz, seq, gw,
                                      comms=[_ag_second(g_out), _ag_first(w_up_b, rows=(2 * eu, 4 * eu), into=g_up)])
    (gdn_out, ssave, tsave), (g_up,) = _gdn_fwd(
        qkv, proj, alog_t, dtb_t, gdn_norm_w, bsz, seq, heads, 3 * gw, lay.ab_col,
        comms=[_ag_first(w_up_b, rows=(4 * eu, 8 * eu), into=g_up)])
    sc_out = _sc_fwd(proj, conv_s, bsz, seq, scw, lay.sc_col)
    mixed = jnp.concatenate([gdn_out, sc_out], axis=1)
    w_out_f = g_out.reshape(d, d)
    cuts = [(up_cols * c) // 64 for c in (0, 16, 22, 36, 50, 64)]
    (mix,), (g_up, g_down) = _matmul(mixed, w_out_f, mode="nn", out_dtypes=[F32], name="out_proj",
                                     comms=[_ag_second(g_up), _ag_first(w_down_b, rows=(cuts[0], cuts[1]))])
    (h, xn2), (g_down,) = _post1(x2, mix, g2, g3, comms=[_ag_first(w_down_b, rows=(cuts[1], cuts[2]), into=g_down)])

    def up_epilogue(acc):
        r = jnp.maximum(acc, 0.0)
        return r, r * r

    tq = t // 4
    act_hid = None
    for part in range(4):
        if part < 3:
            leg = [_ag_first(w_down_b, rows=(cuts[2 + part], cuts[3 + part]), into=g_down)]
        else:
            leg = [_ag_second(g_down)]
        act_hid, (g_down,) = _matmul(
            xn2, g_up, mode="nn", out_dtypes=[BF16, BF16], name="mlp_up_%d" % part, n_cols=dff_w, epilogue=up_epilogue,
            b_spec=lambda tk, tn: pl.BlockSpec((1, tk, tn), lambda i, j, k: (j // (up_cols // tn), k, j % (up_cols // tn))),
            a_rows=(part * tq, tq), out_into=act_hid, comms=leg)
    act, hid = act_hid
    w_down_f = g_down.reshape(dff_w, d)
    (ff,) = _matmul(hid, w_down_f, mode="nn", out_dtypes=[F32], name="mlp_down", tn=1024, tk=2048)
    dff, dy, dg4, loss_p = _post2_loss(h, ff, g4, tgt)

    def pieces(part, sib, got):
        return [(part, me), (sib, chip), (got, 0), (got, 1), (got, 2)]

    place = (mx, my, mc)

    (dpre,) = _matmul(dff, w_down_f, mode="nt", out_dtypes=[BF16], name="d_hidden", extras=[act],
                      epilogue=lambda acc, a: (acc * (2.0 * a.astype(F32)),))
    (dw_down,) = _matmul(hid, dff, mode="tn", out_dtypes=[BF16], name="dw_down", tn=1024)
    dw_down = dw_down.reshape(N_DEV, dff_w // N_DEV, d)
    (dxn2,), (sib_down,) = _matmul(
        dpre, g_up, mode="nt", out_dtypes=[F32], name="d_xn2", n_cols=d, tn=1024, tk=min(up_cols, 2048),
        b_spec=lambda tk, tn: pl.BlockSpec((1, tn, tk), lambda i, j, k: (k // (up_cols // tk), j, k % (up_cols // tk))),
        comms=[_rs_first(dw_down)])
    pair_down = _pair_add(dw_down, sib_down, place, "pair_add_down")
    (dw_up,), (got_down,) = _matmul(
        xn2, dpre, mode="tn", out_dtypes=[BF16], name="dw_up", tn=1024,
        out_custom=lambda tm, tn: ((N_DEV, d, up_cols), (1, tm, tn), lambda i, j, k: (j // (up_cols // tn), i, j % (up_cols // tn))),
        comms=[_rs_second(pair_down, rows=(0, 3 * qd))])
    (dmix, dh, dg2, dg3), (got_down, sib_up) = _mid_bwd(
        h, mix, dy, dxn2, g2, g3, comms=[_rs_second(pair_down, rows=(3 * qd, 4 * qd), into=got_down), _rs_first(dw_up)])
    pair_up = _pair_add(dw_up, sib_up, place, "pair_add_up")
    (dmixed,) = _matmul(dmix, w_out_f, mode="nt", out_dtypes=[F32], name="d_mixed")
    (dw_out,), (got_up,) = _matmul(mixed, dmix, mode="tn", out_dtypes=[BF16], name="dw_out",
                                   comms=[_rs_second(pair_up, rows=(0, qu))])
    dw_out = dw_out.reshape(N_DEV, d // N_DEV, d)
    (dscb, dscc, dsch, dconv_s), (sib_out,) = _sc_bwd(proj, conv_s, dmixed, bsz, seq, scw, lay.sc_col, gw,
                                                      comms=[_rs_first(dw_out)])
    pair_out = _pair_add(dw_out, sib_out, place, "pair_add_out")
    (dact, dz, dab, dalog, ddtb, dgnw), (got_up,) = _gdn_bwd(
        qkv, proj, alog_t, dtb_t, gdn_norm_w, ssave, tsave, dmixed, bsz, seq, heads, 3 * gw, lay.ab_col,
        comms=[_rs_second(pair_up, rows=(qu, 3 * qu), into=got_up)])
    (dqkv, dconv_q), (got_up,) = _qkvconv_bwd(proj, conv_q, dact, bsz, seq, gw,
                                              comms=[_rs_second(pair_up, rows=(3 * qu, 4 * qu), into=got_up)])
    dproj = jnp.concatenate([dqkv, dz, dab, dscb, dscc, dsch, jnp.zeros((t, lay.wp - lay.used), BF16)], axis=1)
    tn_in = _tile(lay.fw, 1024)
    nfb = lay.fw // tn_in
    hd2 = d // 2

    def dw_in_half(half, comms):
        return _matmul(
            xn, dproj, mode="tn", out_dtypes=[BF16], name="dw_in_%d" % half, n_cols=N_DEV * lay.fw, tn=tn_in,
            a_rows=(half * hd2, hd2),
            b_spec=lambda tk, tn: pl.BlockSpec(
                (pl.Element(tk), pl.Element(tn)),
                lambda i, j, k: (0, pl.multiple_of(lay.frame_block(j // nfb) * MXU + (j % nfb) * tn, LANE))),
            out_custom=lambda tm, tn: ((N_DEV, hd2, lay.fw), (1, tm, tn), lambda i, j, k: (j // nfb, i, j % nfb)),
            comms=comms)

    (dw_in_a,), (got_out,) = dw_in_half(0, [_rs_second(pair_out)])
    (dw_in_b,), (sib_a,) = dw_in_half(1, [_rs_first(dw_in_a)])
    pair_a = _pair_add(dw_in_a, sib_a, place, "pair_add_in_a")
    tk_in = _tile(lay.fw, 2048)
    kpf = lay.fw // tk_in

    def d_xn(part, into, comms):
        return _matmul(
            dproj, g_in, mode="nt", out_dtypes=[F32], name="d_xn_%d" % part, n_cols=d, tn=1024, tk=tk_in,
            k_total=N_DEV * lay.fw, a_rows=(part * tq, tq), out_into=None if into is None else [into], comms=comms,
            a_spec=lambda tm, tk, r0: pl.BlockSpec(
                (pl.Element(tm), pl.Element(tk)),
                lambda i, j, k: (pl.multiple_of(i * tm + r0, 16),
                                 pl.multiple_of(lay.frame_block(k // kpf) * MXU + (k % kpf) * tk, LANE))),
            b_spec=lambda tk, tn: pl.BlockSpec((1, tn, tk), lambda i, j, k: (k // kpf, j, k % kpf)))

    (dxn,), (got_a, sib_b) = d_xn(0, None, [_rs_second(pair_a, rows=(0, qu)), _rs_first(dw_in_b)])
    pair_b = _pair_add(dw_in_b, sib_b, place, "pair_add_in_b")
    (dxn,), (got_a,) = d_xn(1, dxn, [_rs_second(pair_a, rows=(qu, 2 * qu), into=got_a)])
    (dxn,), (got_b,) = d_xn(2, dxn, [_rs_second(pair_b, rows=(0, qu))])
    (dxn,), (got_b,) = d_xn(3, dxn, [_rs_second(pair_b, rows=(qu, 2 * qu), into=got_b)])
    grad_x, dg1 = _pre_bwd(x2, dh, dxn, g1)

    gin_frame = jnp.concatenate([_sum_to(pieces(dw_in_a, sib_a, got_a), F32, "grad_w_in_frame_a"),
                                 _sum_to(pieces(dw_in_b, sib_b, got_b), F32, "grad_w_in_frame_b")], axis=0)
    big = {
        "w_in": _adamw(w_in[0], m_w_in[0], v_w_in[0], [(lay.from_frame(gin_frame, me), None)], "adamw_w_in"),
        "w_out": _adamw(w_out[0], m_w_out[0], v_w_out[0], pieces(dw_out, sib_out, got_out), "adamw_w_out"),
        "w_up": _adamw(w_up[0], m_w_up[0], v_w_up[0], pieces(dw_up, sib_up, got_up), "adamw_w_up"),
        "w_down": _adamw(w_down[0], m_w_down[0], v_w_down[0], pieces(dw_down, sib_down, got_down), "adamw_w_down"),
    }

    small_shapes = [(kq, 3 * gw), (ks, scw), (1, d), (1, d), (1, d), (1, d), (1, LANE), (1, LANE), (1, hd), (1, LANE)]
    small = _unpack_rows(
        _allreduce_small(_pack_rows([dconv_q, dconv_s, dg1, dg2, dg3, dg4, jnp.sum(dalog, axis=0), jnp.sum(ddtb, axis=0),
                                     jnp.sum(dgnw, axis=0), loss_p]), "allreduce_small"), small_shapes)
    gq, gs, sg1, sg2, sg3, sg4, salog, sdtb, sgnw, sloss = small
    loss = sloss[0, 0]
    small_grads = {
        "norm_mix_pre": sg1, "conv_qkv_w": lax.dynamic_slice(gq, (0, me * cq_n), (kq, cq_n)),
        "a_log": salog[:, :heads], "dt_bias": sdtb[:, :heads], "gdn_norm_w": sgnw,
        "conv_sc_w": lax.dynamic_slice(gs, (0, me * cs_n), (ks, cs_n)),
        "norm_mix_post": sg2, "norm_mlp_pre": sg3, "norm_mlp_post": sg4,
    }
    weights = {"norm_mix_pre": (norm_mix_pre, m_norm_mix_pre, v_norm_mix_pre), "conv_qkv_w": (conv_qkv_w[0], m_conv_qkv_w[0], v_conv_qkv_w[0]),
               "a_log": (a_log, m_a_log, v_a_log), "dt_bias": (dt_bias, m_dt_bias, v_dt_bias),
               "gdn_norm_w": (gdn_norm_w, m_gdn_norm_w, v_gdn_norm_w), "conv_sc_w": (conv_sc_w[0], m_conv_sc_w[0], v_conv_sc_w[0]),
               "norm_mix_post": (norm_mix_post, m_norm_mix_post, v_norm_mix_post),
               "norm_mlp_pre": (norm_mlp_pre, m_norm_mlp_pre, v_norm_mlp_pre),
               "norm_mlp_post": (norm_mlp_post, m_norm_mlp_post, v_norm_mlp_post)}
    res = dict(big)
    for name, (w, m, v) in weights.items():
        res[name] = _adamw(w, m, v, [(small_grads[name], None)], "adamw_" + name)

    order = ["norm_mix_pre", "w_in", "conv_qkv_w", "a_log", "dt_bias", "gdn_norm_w", "conv_sc_w", "w_out", "norm_mix_post",
             "norm_mlp_pre", "w_up", "w_down", "norm_mlp_post"]
    shapes = {"norm_mix_pre": norm_mix_pre.shape, "w_in": w_in.shape, "conv_qkv_w": conv_qkv_w.shape, "a_log": a_log.shape,
              "dt_bias": dt_bias.shape, "gdn_norm_w": gdn_norm_w.shape, "conv_sc_w": conv_sc_w.shape, "w_out": w_out.shape,
              "norm_mix_post": norm_mix_post.shape, "norm_mlp_pre": norm_mlp_pre.shape, "w_up": w_up.shape,
              "w_down": w_down.shape, "norm_mlp_post": norm_mlp_post.shape}
    outs = [loss, grad_x.reshape(bsz, seq, d)]
    for part in range(4):
        outs += [res[nm][part].reshape(shapes[nm]) for nm in order]
    return tuple(outs)
```

```python
import functools
import math

import numpy as np
import jax
import jax.numpy as jnp
from jax import lax
from jax.experimental import pallas as pl
from jax.experimental.pallas import tpu as pltpu

F32 = jnp.float32
BF16 = jnp.bfloat16
HI = lax.Precision.HIGHEST
MESH = pl.DeviceIdType.MESH

N_DEV = 8
LANE = 128
MXU = 256
CHUNK = 64
NORM_EPS = 1e-6
L2_EPS = 1e-6
VMEM_LIMIT = 56 * 1024 * 1024

ADAM_LR = 0.001
ADAM_B1 = 0.9
ADAM_B2 = 0.999
ADAM_EPS = 1e-08
ADAM_WD = 0.01
ADAM_STEP = 10

NN = (((1,), (0,)), ((), ()))
NT = (((1,), (1,)), ((), ()))
TN = (((0,), (0,)), ((), ()))


def _params(*sem):
    return pltpu.CompilerParams(dimension_semantics=sem, vmem_limit_bytes=VMEM_LIMIT)


def _tile(n, want):
    if n <= want:
        return n
    t = (want // LANE) * LANE
    while t > LANE and n % t:
        t -= LANE
    assert n % t == 0, (n, want)
    return t


class _Comm:
    def __init__(self, srcs, lands, plan, n, alias=None):
        self.srcs, self.lands, self.plan, self.n, self.alias = list(srcs), list(lands), plan, n, dict(alias or {})


def _pcall(body, *, grid, in_specs, out_specs, out_shape, operands, name, scratch_shapes=(), semantics=None,
           prefetch=(), comms=(), fill=None):
    n_pf, n_in, n_out, n_scr = len(prefetch), len(in_specs), len(out_specs), len(scratch_shapes)
    srcs = [s for cm in comms for s in cm.srcs]
    lands = [l for cm in comms for l in cm.lands]
    n_src, n_land = len(srcs), len(lands)
    n_copies = sum(cm.n for cm in comms)
    aliases, so, lo = {n_pf + a: b for a, b in (fill or {}).items()}, 0, 0
    for cm in comms:
        for a, b in cm.alias.items():
            aliases[n_pf + n_in + so + a] = n_out + lo + b
        so, lo = so + len(cm.srcs), lo + len(cm.lands)
    any_spec = pl.BlockSpec(memory_space=pl.ANY)

    def wrapped(*refs):
        pf, r = refs[:n_pf], refs[n_pf:]
        ins, csrc = r[:n_in], r[n_in:n_in + n_src]
        outs = r[n_in + n_src:n_in + n_src + n_out]
        cland = r[n_in + n_src + n_out:n_in + n_src + n_out + n_land]
        rest = r[n_in + n_src + n_out + n_land:]
        scratch = rest[:n_scr]
        if not comms:
            body(*pf, *ins, *outs, *scratch)
            return
        send_sems, recv_sems = rest[n_scr:]

        def copies():
            out, k, s0, l0 = [], 0, 0, 0
            for cm in comms:
                for kind, src, dst, dev in cm.plan(csrc[s0:s0 + len(cm.srcs)], cland[l0:l0 + len(cm.lands)]):
                    if kind == "local":
                        out.append((kind, pltpu.make_async_copy(src, dst, send_sems.at[k])))
                    else:
                        out.append((kind, pltpu.make_async_remote_copy(
                            src_ref=src, dst_ref=dst, send_sem=send_sems.at[k], recv_sem=recv_sems.at[k],
                            device_id=dev, device_id_type=MESH)))
                    k += 1
                s0, l0 = s0 + len(cm.srcs), l0 + len(cm.lands)
            assert k == n_copies
            return out

        ids = [pl.program_id(a) for a in range(len(grid))]
        first = functools.reduce(jnp.logical_and, [i == 0 for i in ids])
        last = functools.reduce(jnp.logical_and, [i == g - 1 for i, g in zip(ids, grid)])

        @pl.when(first)
        def _():
            for _, cp in copies():
                cp.start()

        body(*pf, *ins, *outs, *scratch)

        @pl.when(last)
        def _():
            cps = copies()
            for kind, cp in cps:
                if kind == "remote":
                    cp.wait_recv()
            for kind, cp in cps:
                if kind == "remote":
                    cp.wait_send()
                else:
                    cp.wait()

    sems = [pltpu.SemaphoreType.DMA((n_copies,)), pltpu.SemaphoreType.DMA((n_copies,))] if comms else []
    if semantics is None or comms:
        semantics = ("arbitrary",) * len(grid)
    res = pl.pallas_call(
        wrapped,
        grid_spec=pltpu.PrefetchScalarGridSpec(
            num_scalar_prefetch=n_pf, grid=tuple(grid), in_specs=list(in_specs) + [any_spec] * n_src,
            out_specs=list(out_specs) + [any_spec] * n_land, scratch_shapes=list(scratch_shapes) + sems),
        out_shape=list(out_shape) + lands,
        input_output_aliases=aliases,
        compiler_params=_params(*semantics), name=name)(*prefetch, *operands, *srcs)
    return list(res[:n_out]), list(res[n_out:])


def _bdot(a, b, dims=NN):
    return lax.dot_general(a.astype(BF16), b.astype(BF16), dims, preferred_element_type=F32)


def _hdot(a, b, dims=NN):
    return lax.dot_general(a, b, dims, preferred_element_type=F32, precision=HI)


def _mdot(a, b, dims=NN):
    return lax.dot_general(a, b, dims, preferred_element_type=F32, precision=lax.Precision.HIGH)


def _sigmoid(x):
    return 1.0 / (1.0 + jnp.exp(-x))


def _softplus(x):
    return jnp.maximum(x, 0.0) + jnp.log(1.0 + jnp.exp(-jnp.abs(x)))


def _matmul(a, b, *, mode, out_dtypes, name, n_cols=None, tm=1024, tn=512, tk=4096, epilogue=None, extras=(),
            b_spec=None, out_custom=None, a_rows=None, out_into=None, a_spec=None, k_total=None, comms=()):
    if mode == "tn":
        K, M = a.shape
    else:
        M, K = a.shape
    if k_total is not None:
        K = k_total
    r0 = 0
    if a_rows is not None:
        r0, M = a_rows
    N = n_cols if n_cols is not None else (b.shape[0] if mode == "nt" else b.shape[1])
    tm, tk, tn = _tile(M, tm), _tile(K, tk), _tile(N, tn)
    assert r0 % tm == 0
    i0 = r0 // tm
    if b_spec is None:
        b_spec = pl.BlockSpec((tn, tk), lambda i, j, k: (j, k)) if mode == "nt" else pl.BlockSpec((tk, tn), lambda i, j, k: (k, j))
    else:
        b_spec = b_spec(tk, tn)
    gm, gn, nk = M // tm, N // tn, K // tk
    if out_custom is not None:
        shape, blk, ix = out_custom(tm, tn)
        out_shapes, out_blocks, out_index = [shape] * len(out_dtypes), [blk] * len(out_dtypes), [ix] * len(out_dtypes)
    elif a_rows is not None:
        out_shapes = [(a.shape[0], N)] * len(out_dtypes)
        out_blocks = [(tm, tn)] * len(out_dtypes)
        out_index = [lambda i, j, k: (i + i0, j)] * len(out_dtypes)
    else:
        out_shapes = [(M, N)] * len(out_dtypes)
        out_blocks = [(tm, tn)] * len(out_dtypes)
        out_index = [lambda i, j, k: (i, j)] * len(out_dtypes)
    if a_spec is not None:
        a_spec = a_spec(tm, tk, r0)
    elif mode == "tn":
        a_spec = pl.BlockSpec((tk, tm), lambda i, j, k: (k, i + i0))
    else:
        a_spec = pl.BlockSpec((tm, tk), lambda i, j, k: (i + i0, k))
    hoist = mode == "tn" and nk == 1 and gn > 1
    dims = {"nn": NN, "nt": NT, "tn": TN}[mode]
    n_ex, n_out = len(extras), len(out_dtypes)
    out_into = [] if out_into is None else list(out_into)
    n_into = len(out_into)
    assert n_into in (0, n_out)

    def body(a_ref, b_ref, *rest):
        ex, outs = rest[:n_ex], rest[n_ex + n_into:n_ex + n_into + n_out]

        def finish(acc):
            res = epilogue(acc, *[e[...] for e in ex]) if epilogue is not None else (acc,)
            for o, r in zip(outs, res):
                o[...] = r.reshape(o.shape).astype(o.dtype)

        bb = b_ref[...]
        bb = bb.reshape(bb.shape[-2:])
        if hoist:
            at_ref = rest[-1]

            @pl.when(pl.program_id(1) == 0)
            def _():
                at_ref[...] = a_ref[...].T

            finish(lax.dot_general(at_ref[...], bb, NN, preferred_element_type=F32))
            return
        part = lax.dot_general(a_ref[...], bb, dims, preferred_element_type=F32)
        if nk == 1:
            finish(part)
        else:
            acc = rest[-1]
            k = pl.program_id(2)

            @pl.when(k == 0)
            def _():
                acc[...] = part

            @pl.when(k > 0)
            def _():
                acc[...] += part

            @pl.when(k == nk - 1)
            def _():
                finish(acc[...])

    scratch = [pltpu.VMEM((tm, tk), BF16)] if hoist else ([pltpu.VMEM((tm, tn), F32)] if nk > 1 else [])
    outs, lands = _pcall(
        body, grid=(gm, gn, nk),
        in_specs=([a_spec, b_spec] + [pl.BlockSpec((tm, tn), lambda i, j, k: (i, j)) for _ in extras]
                  + [pl.BlockSpec(memory_space=pl.ANY)] * n_into),
        out_specs=[pl.BlockSpec(blk, ix) for blk, ix in zip(out_blocks, out_index)],
        out_shape=[jax.ShapeDtypeStruct(s, d) for s, d in zip(out_shapes, out_dtypes)],
        scratch_shapes=scratch, semantics=("parallel", "arbitrary", "arbitrary"),
        operands=[a, b, *extras, *out_into], name=name, comms=comms,
        fill={2 + n_ex + o: o for o in range(n_into)})
    return (outs, lands) if comms else outs


TR = 256
QKV_CONV_COLS = 256


def _rms(x):
    return lax.rsqrt(jnp.mean(x * x, axis=-1, keepdims=True) + NORM_EPS)


def _rms_bwd(x, r, w, dy):
    u = dy * w
    dx = r * u - x * (r * r * r) * jnp.mean(x * u, axis=-1, keepdims=True)
    return dx, dy * x * r


def _row_call(body, ins, row_flags, outs, name, n_rows, comms=()):
    n_row = sum(row_flags) + sum(kind == "row" for _, _, kind in outs)
    tr = min(TR if n_row <= 5 else TR // 2, n_rows)
    in_specs = []
    for arr, is_row in zip(ins, row_flags):
        if is_row:
            in_specs.append(pl.BlockSpec((tr, arr.shape[1]), lambda i: (i, 0)))
        else:
            in_specs.append(pl.BlockSpec(arr.shape, lambda i: (0, 0)))
    out_specs, out_shape = [], []
    for shape, dtype, kind in outs:
        if kind == "row":
            out_specs.append(pl.BlockSpec((tr, shape[1]), lambda i: (i, 0)))
        else:
            out_specs.append(pl.BlockSpec(shape, lambda i: (0, 0)))
        out_shape.append(jax.ShapeDtypeStruct(shape, dtype))
    res, lands = _pcall(body, grid=(n_rows // tr,), in_specs=in_specs, out_specs=out_specs, out_shape=out_shape,
                        operands=list(ins), name=name, comms=comms)
    return (res, lands) if comms else res


def _acc_out(ref, val):
    @pl.when(pl.program_id(0) == 0)
    def _():
        ref[...] = val

    @pl.when(pl.program_id(0) > 0)
    def _():
        ref[...] += val


def _rms_fwd(x, g, comms):
    T, D = x.shape

    def body(x_ref, g_ref, o_ref):
        xv = x_ref[...]
        o_ref[...] = (xv * _rms(xv) * g_ref[...]).astype(BF16)

    res, lands = _row_call(body, [x, g], [True, False], [((T, D), BF16, "row")], "rms_fwd", T, comms=comms)
    return res[0], lands


def _post1(x, mix, g2, g3, comms=()):
    T, D = x.shape

    def body(x_ref, mix_ref, g2_ref, g3_ref, h_ref, xn2_ref):
        mv = mix_ref[...]
        h = x_ref[...] + mv * _rms(mv) * g2_ref[...]
        h_ref[...] = h
        xn2_ref[...] = (h * _rms(h) * g3_ref[...]).astype(BF16)

    return _row_call(body, [x, mix, g2, g3], [True, True, False, False],
                     [((T, D), F32, "row"), ((T, D), BF16, "row")], "post1", T, comms=comms)


def _post2_loss(h, ff, g4, target):
    T, D = h.shape

    def body(h_ref, ff_ref, g4_ref, t_ref, dff_ref, dy_ref, dg4_ref, loss_ref):
        fv = ff_ref[...]
        r = _rms(fv)
        err = h_ref[...] + fv * r * g4_ref[...] - t_ref[...]
        dy = err * (1.0 / D)
        dy_ref[...] = dy
        dff, dwt = _rms_bwd(fv, r, g4_ref[...], dy)
        dff_ref[...] = dff.astype(BF16)
        _acc_out(dg4_ref, jnp.sum(dwt, axis=0, keepdims=True))
        part = 0.5 * jnp.sum(jnp.mean(err * err, axis=-1, keepdims=True), axis=0, keepdims=True)
        _acc_out(loss_ref, jnp.broadcast_to(part, (1, LANE)))

    return _row_call(body, [h, ff, g4, target], [True, True, False, True],
                     [((T, D), BF16, "row"), ((T, D), F32, "row"), ((1, D), F32, "acc"), ((1, LANE), F32, "acc")],
                     "post2_loss", T)


def _mid_bwd(h, mix, dy, dxn2, g2, g3, comms=()):
    T, D = h.shape

    def body(h_ref, mix_ref, dy_ref, dxn2_ref, g2_ref, g3_ref, dmix_ref, dh_ref, dg2_ref, dg3_ref):
        hv = h_ref[...]
        d1, dw3 = _rms_bwd(hv, _rms(hv), g3_ref[...], dxn2_ref[...])
        dh = dy_ref[...] + d1
        dh_ref[...] = dh
        mv = mix_ref[...]
        dmix, dw2 = _rms_bwd(mv, _rms(mv), g2_ref[...], dh)
        dmix_ref[...] = dmix.astype(BF16)
        _acc_out(dg2_ref, jnp.sum(dw2, axis=0, keepdims=True))
        _acc_out(dg3_ref, jnp.sum(dw3, axis=0, keepdims=True))

    return _row_call(body, [h, mix, dy, dxn2, g2, g3], [True, True, True, True, False, False],
                     [((T, D), BF16, "row"), ((T, D), F32, "row"), ((1, D), F32, "acc"), ((1, D), F32, "acc")],
                     "mid_bwd", T, comms=comms)


def _pre_bwd(x, dh, dxn, g1, comms=()):
    T, D = x.shape

    def body(x_ref, dh_ref, dxn_ref, g1_ref, gx_ref, dg1_ref):
        xv = x_ref[...]
        d1, dw1 = _rms_bwd(xv, _rms(xv), g1_ref[...], dxn_ref[...])
        gx_ref[...] = dh_ref[...] + d1
        _acc_out(dg1_ref, jnp.sum(dw1, axis=0, keepdims=True))

    return _row_call(body, [x, dh, dxn, g1], [True, True, True, False],
                     [((T, D), F32, "row"), ((1, D), F32, "acc")], "pre_bwd", T, comms=comms)


def _shift_down(x, s):
    if s == 0:
        return x
    row = lax.broadcasted_iota(jnp.int32, x.shape, 0)
    return jnp.where(row >= s, pltpu.roll(x, s, axis=0), 0.0)


def _shift_up(x, s):
    if s == 0:
        return x
    n = x.shape[0]
    row = lax.broadcasted_iota(jnp.int32, x.shape, 0)
    return jnp.where(row < n - s, pltpu.roll(x, n - s, axis=0), 0.0)


def _conv(x, w):
    kw = w.shape[0]
    out = w[kw - 1:kw, :] * x
    for j in range(kw - 1):
        out = out + w[j:j + 1, :] * _shift_down(x, kw - 1 - j)
    return out


def _conv_bwd(x, w, dout):
    kw = w.shape[0]
    dx = w[kw - 1:kw, :] * dout
    dws = []
    for j in range(kw - 1):
        dx = dx + w[j:j + 1, :] * _shift_up(dout, kw - 1 - j)
        dws.append(jnp.sum(dout * _shift_down(x, kw - 1 - j), axis=0, keepdims=True))
    dws.append(jnp.sum(dout * x, axis=0, keepdims=True))
    return dx, jnp.concatenate(dws, axis=0)


def _qkvconv_fwd(proj, w, bsz, seq, gw, comms=()):
    cw = QKV_CONV_COLS
    nct = gw // cw
    kw = w.shape[0]

    def body(p_ref, w_ref, o_ref):
        cv = _conv(p_ref[...], w_ref[...])
        o_ref[...] = (cv * _sigmoid(cv)).reshape(o_ref.shape)

    res, lands = _pcall(
        body, grid=(3, bsz, nct),
        in_specs=[pl.BlockSpec((seq, cw), lambda p, b, c: (b, p * nct + c)),
                  pl.BlockSpec((kw, cw), lambda p, b, c: (0, p * nct + c))],
        out_specs=[pl.BlockSpec((1, seq, cw), lambda p, b, c: (p, b, c))],
        out_shape=[jax.ShapeDtypeStruct((3, bsz * seq, gw), F32)],
        semantics=("parallel", "parallel", "parallel"), operands=[proj, w], name="qkvconv_fwd", comms=comms)
    return res[0], lands


def _qkvconv_bwd(proj, w, dact, bsz, seq, gw, comms=()):
    cw = QKV_CONV_COLS
    nct = gw // cw
    kw = w.shape[0]

    def body(p_ref, w_ref, d_ref, dp_ref, dw_ref):
        pre = p_ref[...]
        wv = w_ref[...]
        cv = _conv(pre, wv)
        sg = _sigmoid(cv)
        dcv = d_ref[...].reshape(cv.shape) * (sg * (1.0 + cv * (1.0 - sg)))
        dpre, dw = _conv_bwd(pre, wv, dcv)
        dp_ref[...] = dpre.astype(BF16)
        b = pl.program_id(2)

        @pl.when(b == 0)
        def _():
            dw_ref[...] = dw

        @pl.when(b > 0)
        def _():
            dw_ref[...] += dw

    res, lands = _pcall(
        body, grid=(3, nct, bsz),
        in_specs=[pl.BlockSpec((seq, cw), lambda p, c, b: (b, p * nct + c)),
                  pl.BlockSpec((kw, cw), lambda p, c, b: (0, p * nct + c)),
                  pl.BlockSpec((1, seq, cw), lambda p, c, b: (p, b, c))],
        out_specs=[pl.BlockSpec((seq, cw), lambda p, c, b: (b, p * nct + c)),
                   pl.BlockSpec((kw, cw), lambda p, c, b: (0, p * nct + c))],
        out_shape=[jax.ShapeDtypeStruct((bsz * seq, 3 * gw), BF16), jax.ShapeDtypeStruct((kw, 3 * gw), F32)],
        semantics=("parallel", "parallel", "arbitrary"), operands=[proj, w, dact], name="qkvconv_bwd", comms=comms)
    return res, lands


def _sc_fwd(proj, w, bsz, seq, scw, col0):
    nct = scw // LANE
    c0 = col0 // LANE
    kw = w.shape[0]

    def body(b_ref, c_ref, h_ref, w_ref, o_ref):
        o_ref[...] = (b_ref[...] * _conv(c_ref[...] * h_ref[...], w_ref[...])).astype(BF16)

    return pl.pallas_call(
        body, grid=(bsz, nct),
        in_specs=[pl.BlockSpec((seq, LANE), lambda b, c: (b, c0 + c)),
                  pl.BlockSpec((seq, LANE), lambda b, c: (b, c0 + nct + c)),
                  pl.BlockSpec((seq, LANE), lambda b, c: (b, c0 + 2 * nct + c)),
                  pl.BlockSpec((kw, LANE), lambda b, c: (0, c))],
        out_specs=pl.BlockSpec((seq, LANE), lambda b, c: (b, c)),
        out_shape=jax.ShapeDtypeStruct((bsz * seq, scw), BF16),
        compiler_params=_params("parallel", "parallel"), name="sc_fwd")(proj, proj, proj, w)


def _sc_bwd(proj, w, dout, bsz, seq, scw, col0, dcol0, comms=()):
    nct = scw // LANE
    c0 = col0 // LANE
    d0 = dcol0 // LANE
    kw = w.shape[0]

    def body(b_ref, c_ref, h_ref, w_ref, d_ref, db_ref, dc_ref, dh_ref, dw_ref):
        cc, hh, wv, dv = c_ref[...], h_ref[...], w_ref[...], d_ref[...]
        m = cc * hh
        db_ref[...] = (dv * _conv(m, wv)).astype(BF16)
        dm, dw = _conv_bwd(m, wv, dv * b_ref[...])
        dc_ref[...] = (dm * hh).astype(BF16)
        dh_ref[...] = (dm * cc).astype(BF16)
        b = pl.program_id(1)

        @pl.when(b == 0)
        def _():
            dw_ref[...] = dw

        @pl.when(b > 0)
        def _():
            dw_ref[...] += dw

    res, lands = _pcall(
        body, grid=(nct, bsz),
        in_specs=[pl.BlockSpec((seq, LANE), lambda c, b: (b, c0 + c)),
                  pl.BlockSpec((seq, LANE), lambda c, b: (b, c0 + nct + c)),
                  pl.BlockSpec((seq, LANE), lambda c, b: (b, c0 + 2 * nct + c)),
                  pl.BlockSpec((kw, LANE), lambda c, b: (0, c)),
                  pl.BlockSpec((seq, LANE), lambda c, b: (b, d0 + c))],
        out_specs=[pl.BlockSpec((seq, LANE), lambda c, b: (b, c)),
                   pl.BlockSpec((seq, LANE), lambda c, b: (b, c)),
                   pl.BlockSpec((seq, LANE), lambda c, b: (b, c)),
                   pl.BlockSpec((kw, LANE), lambda c, b: (0, c))],
        out_shape=[jax.ShapeDtypeStruct((bsz * seq, scw), BF16)] * 3 + [jax.ShapeDtypeStruct((kw, scw), F32)],
        semantics=("parallel", "arbitrary"), operands=[proj, proj, proj, w, dout], name="sc_bwd", comms=comms)
    return res, lands


HEADS_PER_STEP = 16


def _colsel(tile, idx):
    lane = lax.broadcasted_iota(jnp.int32, tile.shape, 1)
    return jnp.sum(jnp.where(lane == idx, tile, 0.0), axis=1, keepdims=True)


def _rowsel(tile, idx):
    row = lax.broadcasted_iota(jnp.int32, tile.shape, 0)
    return jnp.sum(jnp.where(row == idx, tile, 0.0), axis=0, keepdims=True)


def _colput(col, idx, width=LANE):
    lane = lax.broadcasted_iota(jnp.int32, (col.shape[0], width), 1)
    return jnp.where(lane == idx, col, 0.0)


def _tri_masks(c):
    row = lax.broadcasted_iota(jnp.int32, (c, c), 0)
    col = lax.broadcasted_iota(jnp.int32, (c, c), 1)
    return row >= col, row > col, row == col


def _unit_lower_inverses(ms):
    c = ms[0].shape[0]
    _, _, eye = _tri_masks(c)
    ps = [-m for m in ms]
    ts = [jnp.where(eye, 1.0, 0.0) + p for p in ps]
    for _ in range(int(math.log2(c)) - 1):
        ps = [_mdot(p, p) for p in ps]
        ts = [t + _mdot(t, p) for t, p in zip(ts, ps)]
    return ts


def _gates(ab, alog, dtb):
    g = -jnp.exp(alog) * _softplus(ab + dtb)
    return g, _sigmoid(ab)


def _l2n(x):
    r = lax.rsqrt(jnp.sum(x * x, axis=-1, keepdims=True) + L2_EPS)
    return x * r, r


def _gdn_chunk_common(q, k, gc, gr, bc):
    c, dk = q.shape
    incl, strict, _ = _tri_masks(c)
    qh, rq = _l2n(q)
    kn, rk = _l2n(k)
    qn = qh * (dk ** -0.5)
    dm = jnp.where(incl, jnp.exp(jnp.where(incl, gc - gr, 0.0)), 0.0)
    kk = _bdot(kn, kn, NT)
    m = jnp.where(strict, bc * kk * dm, 0.0)
    pm = jnp.where(incl, _bdot(qn, kn, NT) * dm, 0.0)
    return qh, rq, kn, rk, qn, dm, kk, m, pm


def _gdn_fwd(qkv, proj, alog, dtb, gnw, bsz, seq, heads, z_col0, ab_col0, comms=()):
    c = CHUNK
    nch = seq // c
    hb = min(HEADS_PER_STEP, heads)
    ng = heads // hb
    hd = qkv.shape[2] // heads
    wb = hb * hd

    def body(qkv_ref, z_ref, ab_ref, alog_ref, dtb_ref, gnw_ref, o_ref, ssave_ref, tsave_ref, s_ref, gc_ref, gt_ref, be_ref):
        n, hg = pl.program_id(1), pl.program_id(2)

        @pl.when((n == 0) & (hg == 0))
        def _():
            s_ref[...] = jnp.zeros_like(s_ref)

        @pl.when(hg == 0)
        def _():
            g, beta = _gates(ab_ref[...], alog_ref[...], dtb_ref[...])
            incl, _, _ = _tri_masks(c)
            gcum = _hdot(jnp.where(incl, 1.0, 0.0), g)
            gc_ref[...] = gcum
            gt_ref[...] = gcum.T
            be_ref[...] = beta

        gc_t, gt_t, be_t, gnw_v = gc_ref[...], gt_ref[...], be_ref[...], gnw_ref[...]
        hs = range(hb)
        sls = [slice(hh * hd, (hh + 1) * hd) for hh in hs]
        states = [s_ref[hg * hb + hh] for hh in hs]
        gcs = [_colsel(gc_t, hg * hb + hh) for hh in hs]
        grs = [_rowsel(gt_t, hg * hb + hh) for hh in hs]
        bcs = [_colsel(be_t, heads + hg * hb + hh) for hh in hs]
        com = [_gdn_chunk_common(qkv_ref[0, :, sls[hh]], qkv_ref[1, :, sls[hh]], gcs[hh], grs[hh], bcs[hh]) for hh in hs]
        kns, qns, pms = [cm[2] for cm in com], [cm[4] for cm in com], [cm[8] for cm in com]
        tms = _unit_lower_inverses([cm[7] for cm in com])
        gams = [jnp.exp(gc) for gc in gcs]
        glasts = [gc[c - 1:c, :] for gc in gcs]
        kss = [_bdot(kns[hh], states[hh]) for hh in hs]
        qss = [_bdot(qns[hh], states[hh]) for hh in hs]
        vns = [_bdot(tms[hh], bcs[hh] * (qkv_ref[2, :, sls[hh]] - gams[hh] * kss[hh])) for hh in hs]
        os_ = [gams[hh] * qss[hh] + _bdot(pms[hh], vns[hh]) for hh in hs]
        snews = [states[hh] * jnp.exp(glasts[hh]) + _bdot(kns[hh] * jnp.exp(glasts[hh] - gcs[hh]), vns[hh], TN) for hh in hs]
        for hh in hs:
            o = os_[hh]
            on = o * lax.rsqrt(jnp.mean(o * o, axis=-1, keepdims=True) + NORM_EPS) * gnw_v
            zz = z_ref[:, sls[hh]]
            ssave_ref[0, 0, hh] = states[hh]
            tsave_ref[0, 0, hh] = tms[hh]
            s_ref[hg * hb + hh] = snews[hh]
            o_ref[:, sls[hh]] = (on * (zz * _sigmoid(zz))).astype(BF16)

    row = lambda b, n, g: b * nch + n
    return _pcall(
        body, grid=(bsz, nch, ng),
        in_specs=[pl.BlockSpec((3, c, wb), lambda b, n, g: (0, row(b, n, g), g)),
                  pl.BlockSpec((c, wb), lambda b, n, g: (row(b, n, g), z_col0 // wb + g)),
                  pl.BlockSpec((c, LANE), lambda b, n, g: (row(b, n, g), ab_col0 // LANE)),
                  pl.BlockSpec((1, LANE), lambda b, n, g: (0, 0)),
                  pl.BlockSpec((1, LANE), lambda b, n, g: (0, 0)),
                  pl.BlockSpec((1, hd), lambda b, n, g: (0, 0))],
        out_specs=[pl.BlockSpec((c, wb), lambda b, n, g: (row(b, n, g), g)),
                   pl.BlockSpec((1, 1, hb, hd, hd), lambda b, n, g: (b, n, g, 0, 0)),
                   pl.BlockSpec((1, 1, hb, c, c), lambda b, n, g: (b, n, g, 0, 0))],
        out_shape=[jax.ShapeDtypeStruct((bsz * seq, heads * hd), BF16),
                   jax.ShapeDtypeStruct((bsz, nch, heads, hd, hd), F32),
                   jax.ShapeDtypeStruct((bsz, nch, heads, c, c), F32)],
        scratch_shapes=[pltpu.VMEM((heads, hd, hd), F32), pltpu.VMEM((c, LANE), F32), pltpu.VMEM((LANE, c), F32),
                        pltpu.VMEM((c, LANE), F32)],
        semantics=("parallel", "arbitrary", "arbitrary"), operands=[qkv, proj, proj, alog, dtb, gnw], name="gdn_fwd",
        comms=comms)


def _gdn_bwd(qkv, proj, alog, dtb, gnw, ssave, tsave, dout, bsz, seq, heads, z_col0, ab_col0, comms=()):
    c = CHUNK
    nch = seq // c
    hb = min(HEADS_PER_STEP, heads)
    ng = heads // hb
    hd = qkv.shape[2] // heads
    wb = hb * hd

    def body(qkv_ref, z_ref, ab_ref, alog_ref, dtb_ref, gnw_ref, ssave_ref, tsave_ref, do_ref,
             dact_ref, dz_ref, dab_ref, dalog_ref, ddtb_ref, dgnw_ref,
             ds_ref, gc_ref, gt_ref, be_ref, dgacc_ref, dbacc_ref):
        n, hg = pl.program_id(1), pl.program_id(2)
        incl, strict, _ = _tri_masks(c)

        @pl.when((n == 0) & (hg == 0))
        def _():
            ds_ref[...] = jnp.zeros_like(ds_ref)
            dalog_ref[...] = jnp.zeros_like(dalog_ref)
            ddtb_ref[...] = jnp.zeros_like(ddtb_ref)
            dgnw_ref[...] = jnp.zeros_like(dgnw_ref)

        @pl.when(hg == 0)
        def _():
            g, beta = _gates(ab_ref[...], alog_ref[...], dtb_ref[...])
            gcum = _hdot(jnp.where(incl, 1.0, 0.0), g)
            gc_ref[...] = gcum
            gt_ref[...] = gcum.T
            be_ref[...] = beta
            dgacc_ref[...] = jnp.zeros_like(dgacc_ref)
            dbacc_ref[...] = jnp.zeros_like(dbacc_ref)

        gc_t, gt_t, be_t, gnw_v = gc_ref[...], gt_ref[...], be_ref[...], gnw_ref[...]
        hs = range(hb)

        def each(f):
            return [f(hh) for hh in hs]

        rsum = lambda a: jnp.sum(a, axis=-1, keepdims=True)
        sls = each(lambda i: slice(i * hd, (i + 1) * hd))
        ds_in = each(lambda i: ds_ref[hg * hb + i])
        gc = each(lambda i: _colsel(gc_t, hg * hb + i))
        gr = each(lambda i: _rowsel(gt_t, hg * hb + i))
        bc = each(lambda i: _colsel(be_t, heads + hg * hb + i))
        com = each(lambda i: _gdn_chunk_common(qkv_ref[0, :, sls[i]], qkv_ref[1, :, sls[i]], gc[i], gr[i], bc[i]))
        qh, rq, kn, rk, qn, dm, kk, m, pm = [[cm[j] for cm in com] for j in range(9)]
        tm = each(lambda i: tsave_ref[0, 0, i])
        s = each(lambda i: ssave_ref[0, 0, i])
        gam = each(lambda i: jnp.exp(gc[i]))
        glast = each(lambda i: gc[i][c - 1:c, :])
        gl = each(lambda i: jnp.exp(glast[i]))
        ratio = each(lambda i: jnp.exp(glast[i] - gc[i]))
        ks = each(lambda i: _bdot(kn[i], s[i]))
        qs = each(lambda i: _bdot(qn[i], s[i]))
        r = each(lambda i: qkv_ref[2, :, sls[i]] - gam[i] * ks[i])
        vn = each(lambda i: _bdot(tm[i], bc[i] * r[i]))
        o = each(lambda i: gam[i] * qs[i] + _bdot(pm[i], vn[i]))
        ro = each(lambda i: lax.rsqrt(jnp.mean(o[i] * o[i], axis=-1, keepdims=True) + NORM_EPS))
        zz = each(lambda i: z_ref[:, sls[i]])
        sz = each(lambda i: _sigmoid(zz[i]))
        dd = each(lambda i: do_ref[:, sls[i]])
        don = each(lambda i: dd[i] * (zz[i] * sz[i]))
        dz_h = each(lambda i: (dd[i] * (o[i] * ro[i] * gnw_v) * (sz[i] * (1.0 + zz[i] * (1.0 - sz[i])))).astype(BF16))
        dgnw = sum(each(lambda i: jnp.sum(don[i] * o[i] * ro[i], axis=0, keepdims=True)))
        uu = each(lambda i: don[i] * gnw_v)
        d_o = each(lambda i: ro[i] * uu[i] - o[i] * (ro[i] * ro[i] * ro[i]) * jnp.mean(o[i] * uu[i], axis=-1, keepdims=True))
        dqs = each(lambda i: gam[i] * d_o[i])
        dq = each(lambda i: _bdot(dqs[i], s[i], NT))
        ds_new = each(lambda i: _bdot(qn[i], dqs[i], TN))
        dp = each(lambda i: jnp.where(incl, _bdot(d_o[i], vn[i], NT), 0.0))
        dvn = each(lambda i: _bdot(pm[i], d_o[i], TN))
        dgam = each(lambda i: rsum(d_o[i] * qs[i]))
        dkd = each(lambda i: _bdot(vn[i], ds_in[i], NT))
        dvn = each(lambda i: dvn[i] + _bdot(kn[i] * ratio[i], ds_in[i]))
        ds_new = each(lambda i: ds_new[i] + gl[i] * ds_in[i])
        dgl = each(lambda i: jnp.sum(jnp.sum(ds_in[i] * s[i], axis=1, keepdims=True), axis=0, keepdims=True))
        dratio = each(lambda i: rsum(dkd[i] * kn[i]))
        dpd = each(lambda i: dp[i] * dm[i])
        dq = each(lambda i: dq[i] + _bdot(dpd[i], kn[i]))
        dk = each(lambda i: ratio[i] * dkd[i] + _bdot(dpd[i], qn[i], TN))
        dx = each(lambda i: _bdot(tm[i], dvn[i], TN))
        dr = each(lambda i: bc[i] * dx[i])
        gdr = each(lambda i: gam[i] * dr[i])
        dk = each(lambda i: dk[i] - _bdot(gdr[i], s[i], NT))
        ds_new = each(lambda i: ds_new[i] - _bdot(kn[i], gdr[i], TN))
        dmm = each(lambda i: jnp.where(strict, -_bdot(dx[i], vn[i], NT), 0.0))
        ee = each(lambda i: dmm[i] * dm[i])
        be_e = each(lambda i: bc[i] * ee[i])
        dk = each(lambda i: dk[i] + _bdot(be_e[i], kn[i]) + _bdot(be_e[i], kn[i], TN))
        dbeta = each(lambda i: rsum(dx[i] * r[i]) + rsum(ee[i] * kk[i]))
        dgam = each(lambda i: dgam[i] - rsum(dr[i] * ks[i]))
        ff = each(lambda i: dp[i] * pm[i] + dmm[i] * m[i])
        rowi = lax.broadcasted_iota(jnp.int32, (c, 1), 0)
        dgc = each(lambda i: rsum(ff[i]) - rsum(ff[i].T) + dgam[i] * gam[i] - dratio[i] * ratio[i]
                   + jnp.where(rowi == c - 1, jnp.sum(dratio[i] * ratio[i], axis=0, keepdims=True) + dgl[i] * gl[i], 0.0))
        dg_tile = sum(each(lambda i: _colput(dgc[i], hg * hb + i)))
        db_tile = sum(each(lambda i: _colput(dbeta[i], heads + hg * hb + i)))
        for i in hs:
            dqh = dq[i] * (hd ** -0.5)
            ds_ref[hg * hb + i] = ds_new[i]
            dz_ref[:, sls[i]] = dz_h[i]
            dact_ref[0, :, sls[i]] = rq[i] * (dqh - qh[i] * rsum(qh[i] * dqh))
            dact_ref[1, :, sls[i]] = rk[i] * (dk[i] - kn[i] * rsum(kn[i] * dk[i]))
            dact_ref[2, :, sls[i]] = dr[i]
        dgacc_ref[...] += dg_tile
        dbacc_ref[...] += db_tile
        dgnw_ref[0] += dgnw

        @pl.when(hg == ng - 1)
        def _():
            ab = ab_ref[...]
            ea = jnp.exp(alog_ref[...])
            g = -ea * _softplus(ab + dtb_ref[...])
            beta = be_ref[...]
            dg = _hdot(jnp.where(incl, 1.0, 0.0), dgacc_ref[...], TN)
            lane = lax.broadcasted_iota(jnp.int32, ab.shape, 1)
            da = jnp.where(lane < heads, dg * (-ea) * _sigmoid(ab + dtb_ref[...]), 0.0)
            db = dbacc_ref[...] * beta * (1.0 - beta)
            dab_ref[...] = (da + db).astype(BF16)
            dalog_ref[0] += jnp.sum(jnp.where(lane < heads, dg * g, 0.0), axis=0, keepdims=True)
            ddtb_ref[0] += jnp.sum(da, axis=0, keepdims=True)

    row = lambda b, n, g: b * nch + (nch - 1 - n)
    rev = lambda n: nch - 1 - n
    return _pcall(
        body, grid=(bsz, nch, ng),
        in_specs=[pl.BlockSpec((3, c, wb), lambda b, n, g: (0, row(b, n, g), g)),
                  pl.BlockSpec((c, wb), lambda b, n, g: (row(b, n, g), z_col0 // wb + g)),
                  pl.BlockSpec((c, LANE), lambda b, n, g: (row(b, n, g), ab_col0 // LANE)),
                  pl.BlockSpec((1, LANE), lambda b, n, g: (0, 0)),
                  pl.BlockSpec((1, LANE), lambda b, n, g: (0, 0)),
                  pl.BlockSpec((1, hd), lambda b, n, g: (0, 0)),
                  pl.BlockSpec((1, 1, hb, hd, hd), lambda b, n, g: (b, rev(n), g, 0, 0)),
                  pl.BlockSpec((1, 1, hb, c, c), lambda b, n, g: (b, rev(n), g, 0, 0)),
                  pl.BlockSpec((c, wb), lambda b, n, g: (row(b, n, g), g))],
        out_specs=[pl.BlockSpec((3, c, wb), lambda b, n, g: (0, row(b, n, g), g)),
                   pl.BlockSpec((c, wb), lambda b, n, g: (row(b, n, g), g)),
                   pl.BlockSpec((c, LANE), lambda b, n, g: (row(b, n, g), 0)),
                   pl.BlockSpec((1, 1, LANE), lambda b, n, g: (b, 0, 0)),
                   pl.BlockSpec((1, 1, LANE), lambda b, n, g: (b, 0, 0)),
                   pl.BlockSpec((1, 1, hd), lambda b, n, g: (b, 0, 0))],
        out_shape=[jax.ShapeDtypeStruct((3, bsz * seq, heads * hd), F32),
                   jax.ShapeDtypeStruct((bsz * seq, heads * hd), BF16),
                   jax.ShapeDtypeStruct((bsz * seq, LANE), BF16),
                   jax.ShapeDtypeStruct((bsz, 1, LANE), F32),
                   jax.ShapeDtypeStruct((bsz, 1, LANE), F32),
                   jax.ShapeDtypeStruct((bsz, 1, hd), F32)],
        scratch_shapes=[pltpu.VMEM((heads, hd, hd), F32), pltpu.VMEM((c, LANE), F32), pltpu.VMEM((LANE, c), F32),
                        pltpu.VMEM((c, LANE), F32), pltpu.VMEM((c, LANE), F32), pltpu.VMEM((c, LANE), F32)],
        semantics=("parallel", "arbitrary", "arbitrary"),
        operands=[qkv, proj, proj, alog, dtb, gnw, ssave, tsave, dout], name="gdn_bwd", comms=comms)


ELEMWISE_BLOCK_ELEMS = 256 * 1024


def _rows_tile(rows, cols):
    want = max(16, ELEMWISE_BLOCK_ELEMS // cols)
    if rows <= want:
        return rows
    t = (want // 16) * 16
    while t > 16 and rows % t:
        t -= 16
    return t if rows % t == 0 else rows


def _piece_specs(pieces, tr, cols):
    specs, leads = [], []
    for p, (arr, lead) in enumerate(pieces):
        if arr.ndim == 3:
            specs.append(pl.BlockSpec((1, tr, cols), functools.partial(lambda i, idx, p: (idx[p], i, 0), p=p)))
        else:
            specs.append(pl.BlockSpec((tr, cols), lambda i, idx: (i, 0)))
        leads.append(jnp.asarray(0 if lead is None else lead, jnp.int32))
    return jnp.stack(leads), specs


def _sum_pieces(refs):
    total = None
    for r in refs:
        v = r[...].astype(F32)
        v = v.reshape(v.shape[-2:])
        total = v if total is None else total + v
    return total


def _adamw(w, m, v, pieces, name, comms=()):
    rows, cols = w.shape
    tr = _rows_tile(rows, cols)
    leads, pspecs = _piece_specs(pieces, tr, cols)
    npc = len(pieces)
    c1 = 1.0 - ADAM_B1 ** ADAM_STEP
    c2 = 1.0 - ADAM_B2 ** ADAM_STEP

    def body(idx_ref, w_ref, m_ref, v_ref, *rest):
        g = _sum_pieces(rest[:npc])
        g_ref, d_ref, nm_ref, nv_ref = rest[npc:]
        nm = ADAM_B1 * m_ref[...] + (1.0 - ADAM_B1) * g
        nv = ADAM_B2 * v_ref[...] + (1.0 - ADAM_B2) * (g * g)
        g_ref[...] = g
        nm_ref[...] = nm
        nv_ref[...] = nv
        d_ref[...] = -ADAM_LR * ((nm / c1) / (jnp.sqrt(nv / c2) + ADAM_EPS) + ADAM_WD * w_ref[...])

    wspec = pl.BlockSpec((tr, cols), lambda i, idx: (i, 0))
    res, lands = _pcall(body, grid=(rows // tr,), in_specs=[wspec] * 3 + pspecs, out_specs=[wspec] * 4,
                        out_shape=[jax.ShapeDtypeStruct((rows, cols), F32)] * 4, semantics=("parallel",),
                        prefetch=[leads], operands=[w, m, v, *[p for p, _ in pieces]], name=name, comms=comms)
    return (res, lands) if comms else res


def _sum_to(pieces, out_dtype, name):
    arr0 = pieces[0][0]
    rows, cols = arr0.shape[-2:]
    tr = _rows_tile(rows, cols)
    leads, pspecs = _piece_specs(pieces, tr, cols)

    def body(idx_ref, *rest):
        rest[-1][...] = _sum_pieces(rest[:-1]).astype(out_dtype)

    return pl.pallas_call(
        body,
        grid_spec=pltpu.PrefetchScalarGridSpec(num_scalar_prefetch=1, grid=(rows // tr,), in_specs=pspecs,
                                               out_specs=pl.BlockSpec((tr, cols), lambda i, idx: (i, 0))),
        out_shape=jax.ShapeDtypeStruct((rows, cols), out_dtype),
        compiler_params=_params("parallel"), name=name)(leads, *[p for p, _ in pieces])


def _pair_add(a, recv, place, name, comms=()):
    _, rows, cols = a.shape
    tr = _rows_tile(rows, cols)
    x, y, c = place
    idx = jnp.stack([2 * (1 - x) + y, 2 * x + (1 - y), 2 * (1 - x) + (1 - y), c]).astype(jnp.int32)

    def body(p_ref, a_ref, r_ref, o_ref):
        o_ref[...] = (a_ref[...].astype(F32) + r_ref[...].astype(F32)).astype(BF16)

    res, lands = _pcall(
        body, grid=(3, rows // tr),
        in_specs=[pl.BlockSpec((1, tr, cols), lambda j, i, p: (2 * p[j] + p[3], i, 0)),
                  pl.BlockSpec((1, tr, cols), lambda j, i, p: (p[j], i, 0))],
        out_specs=[pl.BlockSpec((1, tr, cols), lambda j, i, p: (j, i, 0))],
        out_shape=[jax.ShapeDtypeStruct((3, rows, cols), BF16)], semantics=("parallel", "parallel"),
        prefetch=[idx], operands=[a, recv], name=name, comms=comms)
    return (res[0], lands) if comms else res[0]


def _to_frame(w, offs, fw, name):
    rows, n = w.shape
    tr = _tile(rows, 256)

    def body(off_ref, w_ref, o_ref, pad_ref):
        pad_ref[...] = jnp.zeros_like(pad_ref)
        pad_ref[:, 0:n] = w_ref[...]
        y = pad_ref[...]
        off1, len1, off2 = off_ref[0], off_ref[1], off_ref[2]
        col = lax.broadcasted_iota(jnp.int32, y.shape, 1)
        o_ref[0] = jnp.where(col < off1 + len1, pltpu.roll(y, off1, axis=1),
                             jnp.where(col >= off2 + len1, pltpu.roll(y, off2, axis=1), 0.0)).astype(BF16)

    res, _ = _pcall(body, grid=(rows // tr,), in_specs=[pl.BlockSpec((tr, n), lambda i, o: (i, 0))],
                    out_specs=[pl.BlockSpec((1, tr, fw), lambda i, o: (o[3], i, 0))],
                    out_shape=[jax.ShapeDtypeStruct((N_DEV, rows, fw), BF16)], scratch_shapes=[pltpu.VMEM((tr, fw), F32)],
                    semantics=("parallel",), prefetch=[offs], operands=[w], name=name)
    return res[0]


def _in_proj_core(xn, frames, ids, lay, into, name, comms=()):
    t, d = xn.shape
    n = ids.shape[0]
    tm = _tile(t, 1024)
    nb = max(b for b in range(1, lay.nc + 1) if lay.nc % b == 0 and b * MXU <= 768)
    tn, nj = nb * MXU, lay.nc // nb
    by_id = not isinstance(frames, (list, tuple))
    fr = [frames] if by_id else list(frames)
    n_fr = len(fr)

    def body(ids_ref, a_ref, *rest):
        o_ref = rest[-1]
        if by_id:
            o_ref[...] = jnp.dot(a_ref[...], rest[0][0], preferred_element_type=F32)
            return
        f = pl.program_id(1) // nj
        for k in range(n_fr):
            @pl.when(f == k)
            def _(k=k):
                o_ref[...] = jnp.dot(a_ref[...], rest[k][...], preferred_element_type=F32)

    core = lambda j: pl.multiple_of(lay.c0 * MXU + j * tn, MXU)
    col = lambda s, j: pl.multiple_of(lay.frame_block(s) * MXU + lay.c0 * MXU + j * tn, MXU)
    if by_id:
        b_specs = [pl.BlockSpec((pl.Element(1), pl.Element(d), pl.Element(tn)), lambda i, fj, ids: (ids[fj // nj], 0, core(fj % nj)))]
    else:
        b_specs = [pl.BlockSpec((pl.Element(d), pl.Element(tn)),
                                functools.partial(lambda i, fj, ids, k: (0, core(jnp.clip(fj - k * nj, 0, nj - 1))), k=k))
                   for k in range(n_fr)]
    n_into = 0 if into is None else 1
    res, lands = _pcall(
        body, grid=(t // tm, n * nj),
        in_specs=[pl.BlockSpec((tm, d), lambda i, fj, ids: (i, 0))] + b_specs + [pl.BlockSpec(memory_space=pl.ANY)] * n_into,
        out_specs=[pl.BlockSpec((pl.Element(tm), pl.Element(tn)), lambda i, fj, ids: (i * tm, col(ids[fj // nj], fj % nj)))],
        out_shape=[jax.ShapeDtypeStruct((t, lay.wp), F32)], semantics=("parallel", "arbitrary"),
        prefetch=[ids], operands=[xn] + fr + ([into] if n_into else []), name=name, comms=comms,
        fill={1 + n_fr: 0} if n_into else None)
    return res[0], lands


def _in_proj_rest(xn, frames, table, wp, into, name, comms=()):
    t, d = xn.shape
    tm = _tile(t, 1024)
    n = table.shape[1]

    def body(t_ref, a_ref, b1_ref, b2_ref, *rest):
        j = pl.program_id(1)
        b = b1_ref[0]
        b = jnp.where(t_ref[5, j] > 0, b + b2_ref[0], b)
        rest[-1][...] = jnp.dot(a_ref[...], b, preferred_element_type=F32)

    n_into = 0 if into is None else 1
    res, lands = _pcall(
        body, grid=(t // tm, n),
        in_specs=[pl.BlockSpec((tm, d), lambda i, j, tb: (i, 0)),
                  pl.BlockSpec((1, d, MXU), lambda i, j, tb: (tb[1, j], 0, tb[2, j])),
                  pl.BlockSpec((1, d, MXU), lambda i, j, tb: (tb[3, j], 0, tb[4, j]))]
        + [pl.BlockSpec(memory_space=pl.ANY)] * n_into,
        out_specs=[pl.BlockSpec((tm, MXU), lambda i, j, tb: (i, tb[0, j]))],
        out_shape=[jax.ShapeDtypeStruct((t, wp), F32)], semantics=("parallel", "arbitrary"),
        prefetch=[table], operands=[xn, frames, frames] + ([into] if n_into else []), name=name, comms=comms,
        fill={3: 0} if n_into else None)
    return res[0], lands


def _place():
    x, y, c = lax.axis_index("x"), lax.axis_index("y"), lax.axis_index("c")
    chips = [(1 - x, y), (x, 1 - y), (1 - x, 1 - y)]
    return x, y, c, chips


def _allreduce_small(buf, name):
    rows = buf.shape[0]

    def body(x_ref, o_ref, g_ref, send_sems, recv_sems):
        x, y, c, chips = _place()
        me, sibling = (x, y, c), (x, y, 1 - c)

        def copy(k, block, to, src=None):
            dst = g_ref.at[4 * block[0] + 2 * block[1] + block[2]]
            return pltpu.make_async_remote_copy(src_ref=dst if src is None else src, dst_ref=dst,
                                                send_sem=send_sems.at[k], recv_sem=recv_sems.at[k],
                                                device_id=to, device_id_type=MESH)

        first = [copy(0, me, sibling, src=x_ref)]
        first += [copy(1 + j, me, (*chip, c), src=x_ref) for j, chip in enumerate(chips)]
        for cp in first:
            cp.start()
        passed = [copy(4 + j, (*chip, c), sibling) for j, chip in enumerate(chips)]
        for j, chip in enumerate(chips):
            copy(1 + j, (*chip, c), me).wait_recv()
            passed[j].start()
        copy(0, sibling, me).wait_recv()
        for j, chip in enumerate(chips):
            copy(4 + j, (*chip, 1 - c), me).wait_recv()
        for cp in first + passed:
            cp.wait_send()
        g_ref[4 * x + 2 * y + c] = x_ref[...]
        total = g_ref[0]
        for s in range(1, N_DEV):
            total = total + g_ref[s]
        o_ref[...] = total

    vm = pl.BlockSpec(memory_space=pltpu.VMEM)
    return pl.pallas_call(
        body, in_specs=[vm], out_specs=vm, out_shape=jax.ShapeDtypeStruct((rows, LANE), F32),
        scratch_shapes=[pltpu.VMEM((N_DEV, rows, LANE), F32), pltpu.SemaphoreType.DMA((7,)), pltpu.SemaphoreType.DMA((7,))],
        name=name)(buf)


def _rows(ref, rows):
    return ref if rows is None else ref.at[pl.ds(rows[0], rows[1] - rows[0])]


AG_ALL = ("here", "sibling", 0, 1, 2)


def _ag(shard=None, into=None, to=(), forward=(), rows=None):
    def plan(srcs, lands):
        x, y, c, chips = _place()
        buf = lands[0]
        out = []
        if to:
            dst = _rows(buf.at[4 * x + 2 * y + c], rows)
            src = dst if shard is None else _rows(srcs[0], rows)
            for who in to:
                if who == "here":
                    out.append(("local", src, dst, None))
                elif who == "sibling":
                    out.append(("remote", src, dst, (x, y, 1 - c)))
                else:
                    out.append(("remote", src, dst, (*chips[who], c)))
        for j in forward:
            r = _rows(buf.at[4 * chips[j][0] + 2 * chips[j][1] + c], rows)
            out.append(("remote", r, r, (x, y, 1 - c)))
        return out

    srcs = ([shard] if to and shard is not None else []) + ([into] if into is not None else [])
    land = jax.ShapeDtypeStruct(into.shape, into.dtype) if into is not None else jax.ShapeDtypeStruct((N_DEV,) + shard.shape, shard.dtype)
    return _Comm(srcs, [land], plan, len(to) + len(forward), alias={len(srcs) - 1: 0} if into is not None else None)


def _ag_first(shard, rows=None, into=None, to=AG_ALL):
    return _ag(shard=shard, into=into, to=to, rows=rows)


def _ag_second(g, rows=None, of=(0, 1, 2)):
    return _ag(into=g, forward=of, rows=rows)


def _rs_first(grad):
    def plan(srcs, lands):
        x, y, c, _ = _place()
        (a,), (land,) = srcs, lands
        return [("remote", a.at[2 * j + (1 - c)], land.at[j], (x, y, 1 - c)) for j in range(4)]

    return _Comm([grad], [jax.ShapeDtypeStruct((4,) + grad.shape[1:], grad.dtype)], plan, 4)


def _rs_second(pair, rows=None, into=None):
    def plan(srcs, lands):
        x, y, c, chips = _place()
        return [("remote", _rows(srcs[0].at[j], rows), _rows(lands[0].at[j], rows), (cx, cy, c))
                for j, (cx, cy) in enumerate(chips)]

    land = jax.ShapeDtypeStruct((3,) + pair.shape[1:], pair.dtype)
    if into is None:
        return _Comm([pair], [land], plan, 3)
    return _Comm([pair, into], [land], plan, 3, alias={1: 0})


class _InLayout:
    def __init__(self, n_in, gw, heads, scw):
        self.n_in, self.split = n_in, 4 * gw + 2 * heads
        self.gap = LANE - 2 * heads
        self.ab_col, self.sc_col = 4 * gw, 4 * gw + LANE
        self.used = 4 * gw + LANE + 3 * scw
        p0 = [s * n_in + (self.gap if s * n_in >= self.split else 0) for s in range(N_DEV)]
        self.fstart = [(p // MXU) * MXU for p in p0]
        need = []
        for s in range(N_DEV):
            straddle = s * n_in < self.split < (s + 1) * n_in
            need.append(p0[s] - self.fstart[s] + n_in + (self.gap if straddle else 0))
        self.fw = -(-max(need) // MXU) * MXU
        self.wp = max(f + self.fw for f in self.fstart)
        assert self.wp >= self.used and self.wp % MXU == 0
        nfb = self.fw // MXU
        rows = []
        for jb in range(self.wp // MXU):
            src = [(s, jb - self.fstart[s] // MXU) for s in range(N_DEV) if 0 <= jb - self.fstart[s] // MXU < nfb]
            assert 1 <= len(src) <= 2, (jb, src)
            (s1, b1), (s2, b2) = src[0], src[-1]
            rows.append((s1, b1, s2, b2, int(len(src) == 2)))
        self.table = np.asarray(rows, np.int32).T.copy()
        single = [all(rows[self.fstart[s] // MXU + b][4] == 0 for s in range(N_DEV)) for b in range(nfb)]
        runs, b = [], 0
        while b < nfb:
            if single[b]:
                e = b
                while e < nfb and single[e]:
                    e += 1
                runs.append((e - b, b))
                b = e
            else:
                b += 1
        self.nc, self.c0 = max(runs) if runs else (0, 0)
        in_core = {self.fstart[s] // MXU + b for s in range(N_DEV) for b in range(self.c0, self.c0 + self.nc)}
        self.rest = np.asarray([(jb,) + rows[jb] for jb in range(self.wp // MXU) if jb not in in_core], np.int32).T.copy()

    def frame_block(self, s):
        p = s * self.n_in
        return (p + jnp.where(p >= self.split, self.gap, 0)) // MXU

    def offsets(self, s):
        p = s * self.n_in
        after = p >= self.split
        off1 = p + jnp.where(after, self.gap, 0) - self.frame_block(s) * MXU
        len1 = jnp.where(after, self.n_in, jnp.clip(self.split - p, 0, self.n_in))
        off2 = off1 + jnp.where(len1 < self.n_in, self.gap, 0)
        return off1, len1, off2

    def to_frame(self, w, s):
        return _to_frame(w, jnp.stack(self.offsets(s) + (s,)).astype(jnp.int32), self.fw, "w_in_frame")

    def from_frame(self, f, s):
        off1, len1, off2 = self.offsets(s)
        a = lax.dynamic_slice(f, (0, off1), (f.shape[0], self.n_in))
        b = lax.dynamic_slice(f, (0, off2), (f.shape[0], self.n_in))
        col = lax.broadcasted_iota(jnp.int32, (1, self.n_in), 1)
        return jnp.where(col < len1, a, b)


def _pack_rows(parts):
    rows = []
    for p in parts:
        flat = p.reshape(-1)
        pad = (-flat.shape[0]) % LANE
        rows.append(jnp.pad(flat, (0, pad)).reshape(-1, LANE))
    buf = jnp.concatenate(rows, axis=0)
    return jnp.pad(buf, ((0, (-buf.shape[0]) % 8), (0, 0)))


def _unpack_rows(buf, shapes):
    out, r = [], 0
    for shp in shapes:
        size = int(np.prod(shp))
        nr = -(-size // LANE)
        out.append(buf[r:r + nr].reshape(-1)[:size].reshape(shp))
        r += nr
    return out


def _pad_lanes(v):
    return jnp.pad(v, ((0, 0), (0, LANE - v.shape[1])))


def kernel(x, norm_mix_pre, w_in, conv_qkv_w, a_log, dt_bias, gdn_norm_w, conv_sc_w, w_out, norm_mix_post, norm_mlp_pre, w_up, w_down, norm_mlp_post, loss_target, m_norm_mix_pre, m_w_in, m_conv_qkv_w, m_a_log, m_dt_bias, m_gdn_norm_w, m_conv_sc_w, m_w_out, m_norm_mix_post, m_norm_mlp_pre, m_w_up, m_w_down, m_norm_mlp_post, v_norm_mix_pre, v_w_in, v_conv_qkv_w, v_a_log, v_dt_bias, v_gdn_norm_w, v_conv_sc_w, v_w_out, v_norm_mix_post, v_norm_mlp_pre, v_w_up, v_w_down, v_norm_mlp_post):
    bsz, seq, d = x.shape
    t = bsz * seq
    heads, hd = a_log.shape[-1], gdn_norm_w.shape[-1]
    gw = heads * hd
    scw = conv_sc_w.shape[-1] * N_DEV
    dff_w = w_up.shape[-1] * N_DEV
    lay = _InLayout(w_in.shape[-1], gw, heads, scw)
    mx, my, mc = lax.axis_index("x"), lax.axis_index("y"), lax.axis_index("c")
    me = 4 * mx + 2 * my + mc
    chip = 2 * mx + my

    x2 = x.reshape(t, d)
    tgt = loss_target.reshape(t, d)
    g1, g2, g3, g4 = norm_mix_pre, norm_mix_post, norm_mlp_pre, norm_mlp_post

    g_in = lay.to_frame(w_in[0], me)
    w_out_b, w_up_b, w_down_b = w_out[0].astype(BF16), w_up[0].astype(BF16), w_down[0].astype(BF16)
    up_cols = dff_w // N_DEV
    qu, qd = d // 4, up_cols // 4
    kq, ks = conv_qkv_w.shape[1], conv_sc_w.shape[1]
    cq_n, cs_n = conv_qkv_w.shape[-1], conv_sc_w.shape[-1]
    cq_full = lax.dynamic_update_slice(jnp.zeros((kq, 3 * gw), F32), conv_qkv_w[0], (0, me * cq_n))
    cs_full = lax.dynamic_update_slice(jnp.zeros((ks, scw), F32), conv_sc_w[0], (0, me * cs_n))
    conv_q, conv_s = _unpack_rows(_allreduce_small(_pack_rows([cq_full, cs_full]), "allgather_conv"),
                                  [(kq, 3 * gw), (ks, scw)])
    alog_t, dtb_t = _pad_lanes(a_log), _pad_lanes(dt_bias)

    xn, (g_in,) = _rms_fwd(x2, g1, comms=[_ag(into=g_in, to=("sibling",))])

    def arrived(k):
        return [lax.dynamic_index_in_dim(g_in, s, 0, keepdims=False) for s in ids[k]]

    dev = lambda px, py, pc: (4 * px + 2 * py + pc).astype(jnp.int32)
    ids = [jnp.stack([dev(mx, my, mc), dev(mx, my, 1 - mc)]),
           jnp.stack([dev(1 - mx, my, mc), dev(mx, 1 - my, mc)]),
           jnp.stack([dev(1 - mx, my, 1 - mc), dev(mx, 1 - my, 1 - mc)]),
           jnp.stack([dev(1 - mx, 1 - my, mc), dev(1 - mx, 1 - my, 1 - mc)])]
    assert lay.nc > 0, "the frames have no columns of their own at these sizes"
    proj, (g_in,) = _in_proj_core(xn, arrived(0), ids[0], lay, None, "in_proj_0", comms=[_ag(into=g_in, to=(0, 1))])
    proj, (g_in,) = _in_proj_core(xn, arrived(1), ids[1], lay, proj, "in_proj_1",
                                  comms=[_ag(into=g_in, to=(2,), forward=(0, 1))])
    proj, (g_in, g_out) = _in_proj_core(xn, arrived(2), ids[2], lay, proj, "in_proj_2",
                                        comms=[_ag(into=g_in, forward=(2,)), _ag_first(w_out_b)])
    eu = qu // 2
    proj, (g_up,) = _in_proj_core(xn, g_in, ids[3], lay, proj, "in_proj_3", comms=[_ag_first(w_up_b, rows=(0, eu))])
    proj, (g_up,) = _in_proj_rest(xn, g_in, jnp.asarray(lay.rest), lay.wp, proj, "in_proj_rest",
                                  comms=[_ag_first(w_up_b, rows=(eu, 2 * eu), into=g_up)])
    qkv, (g_out, g_up) = _qkvconv_fwd(proj, conv_q, bsz, seq, gw,
                                      comms=[_ag_second(g_out), _ag_first(w_up_b, rows=(2 * eu, 4 * eu), into=g_up)])
    (gdn_out, ssave, tsave), (g_up,) = _gdn_fwd(
        qkv, proj, alog_t, dtb_t, gdn_norm_w, bsz, seq, heads, 3 * gw, lay.ab_col,
        comms=[_ag_first(w_up_b, rows=(4 * eu, 8 * eu), into=g_up)])
    sc_out = _sc_fwd(proj, conv_s, bsz, seq, scw, lay.sc_col)
    mixed = jnp.concatenate([gdn_out, sc_out], axis=1)
    w_out_f = g_out.reshape(d, d)
    cuts = [(up_cols * c) // 64 for c in (0, 16, 22, 36, 50, 64)]
    (mix,), (g_up, g_down) = _matmul(mixed, w_out_f, mode="nn", out_dtypes=[F32], name="out_proj",
                                     comms=[_ag_second(g_up), _ag_first(w_down_b, rows=(cuts[0], cuts[1]))])
    (h, xn2), (g_down,) = _post1(x2, mix, g2, g3, comms=[_ag_first(w_down_b, rows=(cuts[1], cuts[2]), into=g_down)])

    def up_epilogue(acc):
        r = jnp.maximum(acc, 0.0)
        return r, r * r

    tq = t // 4
    act_hid = None
    for part in range(4):
        if part < 3:
            leg = [_ag_first(w_down_b, rows=(cuts[2 + part], cuts[3 + part]), into=g_down)]
        else:
            leg = [_ag_second(g_down)]
        act_hid, (g_down,) = _matmul(
            xn2, g_up, mode="nn", out_dtypes=[BF16, BF16], name="mlp_up_%d" % part, n_cols=dff_w, epilogue=up_epilogue,
            b_spec=lambda tk, tn: pl.BlockSpec((1, tk, tn), lambda i, j, k: (j // (up_cols // tn), k, j % (up_cols // tn))),
            a_rows=(part * tq, tq), out_into=act_hid, comms=leg)
    act, hid = act_hid
    w_down_f = g_down.reshape(dff_w, d)
    (ff,) = _matmul(hid, w_down_f, mode="nn", out_dtypes=[F32], name="mlp_down", tn=1024, tk=2048)
    dff, dy, dg4, loss_p = _post2_loss(h, ff, g4, tgt)

    def pieces(part, sib, got):
        return [(part, me), (sib, chip), (got, 0), (got, 1), (got, 2)]

    place = (mx, my, mc)

    (dpre,) = _matmul(dff, w_down_f, mode="nt", out_dtypes=[BF16], name="d_hidden", extras=[act], tn=1024,
                      epilogue=lambda acc, a: (acc * (2.0 * a.astype(F32)),))
    (dw_down,) = _matmul(hid, dff, mode="tn", out_dtypes=[BF16], name="dw_down", tn=1024)
    dw_down = dw_down.reshape(N_DEV, dff_w // N_DEV, d)
    (dxn2,), (sib_down,) = _matmul(
        dpre, g_up, mode="nt", out_dtypes=[F32], name="d_xn2", n_cols=d, tn=1024, tk=min(up_cols, 2048),
        b_spec=lambda tk, tn: pl.BlockSpec((1, tn, tk), lambda i, j, k: (k // (up_cols // tk), j, k % (up_cols // tk))),
        comms=[_rs_first(dw_down)])
    pair_down = _pair_add(dw_down, sib_down, place, "pair_add_down")
    (dw_up,), (got_down,) = _matmul(
        xn2, dpre, mode="tn", out_dtypes=[BF16], name="dw_up", tn=1024,
        out_custom=lambda tm, tn: ((N_DEV, d, up_cols), (1, tm, tn), lambda i, j, k: (j // (up_cols // tn), i, j % (up_cols // tn))),
        comms=[_rs_second(pair_down, rows=(0, 3 * qd))])
    (dmix, dh, dg2, dg3), (got_down, sib_up) = _mid_bwd(
        h, mix, dy, dxn2, g2, g3, comms=[_rs_second(pair_down, rows=(3 * qd, 4 * qd), into=got_down), _rs_first(dw_up)])
    pair_up = _pair_add(dw_up, sib_up, place, "pair_add_up")
    (dmixed,) = _matmul(dmix, w_out_f, mode="nt", out_dtypes=[F32], name="d_mixed")
    (dw_out,), (got_up,) = _matmul(mixed, dmix, mode="tn", out_dtypes=[BF16], name="dw_out",
                                   comms=[_rs_second(pair_up, rows=(0, qu))])
    dw_out = dw_out.reshape(N_DEV, d // N_DEV, d)
    (dscb, dscc, dsch, dconv_s), (sib_out,) = _sc_bwd(proj, conv_s, dmixed, bsz, seq, scw, lay.sc_col, gw,
                                                      comms=[_rs_first(dw_out)])
    pair_out = _pair_add(dw_out, sib_out, place, "pair_add_out")
    (dact, dz, dab, dalog, ddtb, dgnw), (got_up,) = _gdn_bwd(
        qkv, proj, alog_t, dtb_t, gdn_norm_w, ssave, tsave, dmixed, bsz, seq, heads, 3 * gw, lay.ab_col,
        comms=[_rs_second(pair_up, rows=(qu, 3 * qu), into=got_up)])
    (dqkv, dconv_q), (got_up,) = _qkvconv_bwd(proj, conv_q, dact, bsz, seq, gw,
                                              comms=[_rs_second(pair_up, rows=(3 * qu, 4 * qu), into=got_up)])
    dproj = jnp.concatenate([dqkv, dz, dab, dscb, dscc, dsch, jnp.zeros((t, lay.wp - lay.used), BF16)], axis=1)
    tn_in = _tile(lay.fw, 1024)
    nfb = lay.fw // tn_in
    hd2 = d // 2

    def dw_in_half(half, comms):
        return _matmul(
            xn, dproj, mode="tn", out_dtypes=[BF16], name="dw_in_%d" % half, n_cols=N_DEV * lay.fw, tn=tn_in,
            a_rows=(half * hd2, hd2),
            b_spec=lambda tk, tn: pl.BlockSpec(
                (pl.Element(tk), pl.Element(tn)),
                lambda i, j, k: (0, pl.multiple_of(lay.frame_block(j // nfb) * MXU + (j % nfb) * tn, LANE))),
            out_custom=lambda tm, tn: ((N_DEV, hd2, lay.fw), (1, tm, tn), lambda i, j, k: (j // nfb, i, j % nfb)),
            comms=comms)

    (dw_in_a,), (got_out,) = dw_in_half(0, [_rs_second(pair_out)])
    (dw_in_b,), (sib_a,) = dw_in_half(1, [_rs_first(dw_in_a)])
    pair_a = _pair_add(dw_in_a, sib_a, place, "pair_add_in_a")
    tk_in = _tile(lay.fw, 2048)
    kpf = lay.fw // tk_in

    def d_xn(part, into, comms):
        return _matmul(
            dproj, g_in, mode="nt", out_dtypes=[F32], name="d_xn_%d" % part, n_cols=d, tn=1024, tk=tk_in,
            k_total=N_DEV * lay.fw, a_rows=(part * tq, tq), out_into=None if into is None else [into], comms=comms,
            a_spec=lambda tm, tk, r0: pl.BlockSpec(
                (pl.Element(tm), pl.Element(tk)),
                lambda i, j, k: (pl.multiple_of(i * tm + r0, 16),
                                 pl.multiple_of(lay.frame_block(k // kpf) * MXU + (k % kpf) * tk, LANE))),
            b_spec=lambda tk, tn: pl.BlockSpec((1, tn, tk), lambda i, j, k: (k // kpf, j, k % kpf)))

    (dxn,), (got_a, sib_b) = d_xn(0, None, [_rs_second(pair_a, rows=(0, qu)), _rs_first(dw_in_b)])
    pair_b = _pair_add(dw_in_b, sib_b, place, "pair_add_in_b")
    (dxn,), (got_a,) = d_xn(1, dxn, [_rs_second(pair_a, rows=(qu, 2 * qu), into=got_a)])
    (dxn,), (got_b,) = d_xn(2, dxn, [_rs_second(pair_b, rows=(0, qu))])
    (dxn,), (got_b,) = d_xn(3, dxn, [_rs_second(pair_b, rows=(qu, 2 * qu), into=got_b)])
    grad_x, dg1 = _pre_bwd(x2, dh, dxn, g1)

    gin_frame = jnp.concatenate([_sum_to(pieces(dw_in_a, sib_a, got_a), F32, "grad_w_in_frame_a"),
                                 _sum_to(pieces(dw_in_b, sib_b, got_b), F32, "grad_w_in_frame_b")], axis=0)
    big = {
        "w_in": _adamw(w_in[0], m_w_in[0], v_w_in[0], [(lay.from_frame(gin_frame, me), None)], "adamw_w_in"),
        "w_out": _adamw(w_out[0], m_w_out[0], v_w_out[0], pieces(dw_out, sib_out, got_out), "adamw_w_out"),
        "w_up": _adamw(w_up[0], m_w_up[0], v_w_up[0], pieces(dw_up, sib_up, got_up), "adamw_w_up"),
        "w_down": _adamw(w_down[0], m_w_down[0], v_w_down[0], pieces(dw_down, sib_down, got_down), "adamw_w_down"),
    }

    small_shapes = [(kq, 3 * gw), (ks, scw), (1, d), (1, d), (1, d), (1, d), (1, LANE), (1, LANE), (1, hd), (1, LANE)]
    small = _unpack_rows(
        _allreduce_small(_pack_rows([dconv_q, dconv_s, dg1, dg2, dg3, dg4, jnp.sum(dalog, axis=0), jnp.sum(ddtb, axis=0),
                                     jnp.sum(dgnw, axis=0), loss_p]), "allreduce_small"), small_shapes)
    gq, gs, sg1, sg2, sg3, sg4, salog, sdtb, sgnw, sloss = small
    loss = sloss[0, 0]
    small_grads = {
        "norm_mix_pre": sg1, "conv_qkv_w": lax.dynamic_slice(gq, (0, me * cq_n), (kq, cq_n)),
        "a_log": salog[:, :heads], "dt_bias": sdtb[:, :heads], "gdn_norm_w": sgnw,
        "conv_sc_w": lax.dynamic_slice(gs, (0, me * cs_n), (ks, cs_n)),
        "norm_mix_post": sg2, "norm_mlp_pre": sg3, "norm_mlp_post": sg4,
    }
    weights = {"norm_mix_pre": (norm_mix_pre, m_norm_mix_pre, v_norm_mix_pre), "conv_qkv_w": (conv_qkv_w[0], m_conv_qkv_w[0], v_conv_qkv_w[0]),
               "a_log": (a_log, m_a_log, v_a_log), "dt_bias": (dt_bias, m_dt_bias, v_dt_bias),
               "gdn_norm_w": (gdn_norm_w, m_gdn_norm_w, v_gdn_norm_w), "conv_sc_w": (conv_sc_w[0], m_conv_sc_w[0], v_conv_sc_w[0]),
               "norm_mix_post": (norm_mix_post, m_norm_mix_post, v_norm_mix_post),
               "norm_mlp_pre": (norm_mlp_pre, m_norm_mlp_pre, v_norm_mlp_pre),
               "norm_mlp_post": (norm_mlp_post, m_norm_mlp_post, v_norm_mlp_post)}
    res = dict(big)
    for name, (w, m, v) in weights.items():
        res[name] = _adamw(w, m, v, [(small_grads[name], None)], "adamw_" + name)

    order = ["norm_mix_pre", "w_in", "conv_qkv_w", "a_log", "dt_bias", "gdn_norm_w", "conv_sc_w", "w_out", "norm_mix_post",
             "norm_mlp_pre", "w_up", "w_down", "norm_mlp_post"]
    shapes = {"norm_mix_pre": norm_mix_pre.shape, "w_in": w_in.shape, "conv_qkv_w": conv_qkv_w.shape, "a_log": a_log.shape,
              "dt_bias": dt_bias.shape, "gdn_norm_w": gdn_norm_w.shape, "conv_sc_w": conv_sc_w.shape, "w_out": w_out.shape,
              "norm_mix_post": norm_mix_post.shape, "norm_mlp_pre": norm_mlp_pre.shape, "w_up": w_up.shape,
              "w_down": w_down.shape, "norm_mlp_post": norm_mlp_post.shape}
    outs = [loss, grad_x.reshape(bsz, seq, d)]
    for part in range(4):
        outs += [res[nm][part].reshape(shapes[nm]) for nm in order]
    return tuple(outs)
```
